```python
import jax
import jax.numpy as jnp
from jax import lax
import numpy as np

D_MODEL = 2048
BATCH = 2
SEQ = 4096
DEPTH = 1

HEAD_DIM = 64
ATTN_Q_HEADS = 16
ATTN_KV_HEADS = 4
ATTN_GROUP = ATTN_Q_HEADS // ATTN_KV_HEADS
ATTN_Q_W = ATTN_Q_HEADS * HEAD_DIM
ATTN_KV_W = ATTN_KV_HEADS * HEAD_DIM
WINDOW = 128
BLOCK = 128
ROPE_THETA = 10000.0

RWKV_HEADS = 16
RWKV_HEAD_DIM = 64
RWKV_W = RWKV_HEADS * RWKV_HEAD_DIM
DECAY_LORA = 64
AAA_LORA = 64
GATE_LORA = 160
RWKV_GN_EPS = 64e-5

N_GROUPS = 8
EXPERTS_PER_GROUP = 8
N_EXPERTS = N_GROUPS * EXPERTS_PER_GROUP
EXPERT_FF = 1408
TOP_K_IN_GROUP = 2
MOE_BLOCK = 128

LN_EPS = 1e-5
IN_W = ATTN_Q_W + 2 * ATTN_KV_W + 3 * RWKV_W + 2 * D_MODEL

kernel_name = 'hybrid_swa_sink_rwkv7_hiermoe_deepnorm'


def _in_offsets():
    widths = (ATTN_Q_W, ATTN_KV_W, ATTN_KV_W, RWKV_W, RWKV_W, RWKV_W, D_MODEL, D_MODEL)
    offs = []
    acc = 0
    for w in widths[:-1]:
        acc += w
        offs.append(acc)
    return offs


def layer_norm(t, g, b):
    tf = t.astype(jnp.float32)
    mu = jnp.mean(tf, axis=-1, keepdims=True)
    var = jnp.mean(jnp.square(tf - mu), axis=-1, keepdims=True)
    out = (tf - mu) * lax.rsqrt(var + LN_EPS) * g.astype(jnp.float32) + b.astype(jnp.float32)
    return out.astype(t.dtype)


def rope_tables(T):
    inv = 1.0 / (ROPE_THETA ** (jnp.arange(0, HEAD_DIM, 2, dtype=jnp.float32) / HEAD_DIM))
    ang = jnp.arange(T, dtype=jnp.float32)[:, None] * inv[None, :]
    return jnp.cos(ang)[:, None, :], jnp.sin(ang)[:, None, :]


def apply_rope(t, cos, sin):
    tf = t.astype(jnp.float32)
    t1, t2 = jnp.split(tf, 2, axis=-1)
    return jnp.concatenate([t1 * cos - t2 * sin, t2 * cos + t1 * sin], axis=-1)


def _band(t, nb):
    B = t.shape[0]
    tb = t.reshape(B, nb, BLOCK, ATTN_KV_HEADS, HEAD_DIM)
    prev = jnp.pad(tb, ((0, 0), (1, 0), (0, 0), (0, 0), (0, 0)))[:, :-1]
    return jnp.concatenate([prev, tb], axis=2)


def sliding_window_sink_gqa(q, k, v, sinks, cos, sin):
    B, T, _ = q.shape
    dt = q.dtype
    nb = T // BLOCK
    q = apply_rope(q.reshape(B, T, ATTN_Q_HEADS, HEAD_DIM), cos, sin).astype(dt)
    k = apply_rope(k.reshape(B, T, ATTN_KV_HEADS, HEAD_DIM), cos, sin).astype(dt)
    v = v.reshape(B, T, ATTN_KV_HEADS, HEAD_DIM)
    qb = q.reshape(B, nb, BLOCK, ATTN_KV_HEADS, ATTN_GROUP, HEAD_DIM)
    kw = _band(k, nb)
    vw = _band(v, nb)
    s = jnp.einsum('bnqhgd,bnkhd->bnhgqk', qb, kw, preferred_element_type=jnp.float32) * (HEAD_DIM ** -0.5)
    qi = jnp.arange(BLOCK)[:, None]
    kj = jnp.arange(2 * BLOCK)[None, :]
    dist = qi + BLOCK - kj
    in_win = (dist >= 0) & (dist < WINDOW)
    blk = jnp.arange(nb)[:, None, None]
    valid = in_win[None] & ((blk > 0) | (kj[None] >= BLOCK))
    s = jnp.where(valid[None, :, None, None], s, -jnp.inf)
    sink = sinks.astype(jnp.float32).reshape(1, 1, ATTN_KV_HEADS, ATTN_GROUP, 1, 1)
    sink = jnp.broadcast_to(sink, s.shape[:-1] + (1,))
    p = jax.nn.softmax(jnp.concatenate([s, sink], axis=-1), axis=-1)[..., :-1]
    o = jnp.einsum('bnhgqk,bnkhd->bnqhgd', p.astype(dt), vw)
    return o.reshape(B, T, ATTN_Q_W)


def token_shift(t):
    return jnp.pad(t, ((0, 0), (1, 0), (0, 0)))[:, :-1]


def wkv7_scan(r, decay, k, v, a_, b_):
    B, T, H, N = r.shape

    def step(S, inp):
        r_t, d_t, k_t, v_t, a_t, b_t = inp
        sa = jnp.einsum('bhij,bhj->bhi', S, a_t)
        S = S * d_t[:, :, None, :] + sa[..., None] * b_t[:, :, None, :] + v_t[..., None] * k_t[:, :, None, :]
        y = jnp.einsum('bhij,bhj->bhi', S, r_t)
        return S, y

    xs = tuple(jnp.moveaxis(t, 1, 0) for t in (r, decay, k, v, a_, b_))
    S0 = jnp.zeros((B, H, N, N), jnp.float32)
    _, ys = lax.scan(step, S0, xs)
    return jnp.moveaxis(ys, 0, 1)


def rwkv7_time_mix(h, r, k, v, mu_rkv, mu_wag, w0, w1, w2, a0, a1, a2, g1, g2, k_k, k_a, r_k, lnx_w, lnx_b):
    B, T, _ = h.shape
    H, N = RWKV_HEADS, RWKV_HEAD_DIM
    dt = h.dtype
    r = r + (token_shift(r) - r) * mu_rkv[0]
    k = k + (token_shift(k) - k) * mu_rkv[1]
    v = v + (token_shift(v) - v) * mu_rkv[2]
    xx = token_shift(h) - h
    xw = h + xx * mu_wag[0]
    xa = h + xx * mu_wag[1]
    xg = h + xx * mu_wag[2]
    w = -jax.nn.softplus(-(w0 + jnp.tanh(xw @ w1) @ w2)) - 0.5
    a = jax.nn.sigmoid(a0 + (xa @ a1) @ a2)
    g = jax.nn.sigmoid(xg @ g1) @ g2

    def heads(t):
        return t.reshape(B, T, H, N).astype(jnp.float32)

    kk = heads(k * k_k)
    kk = kk / jnp.maximum(jnp.sqrt(jnp.sum(kk * kk, axis=-1, keepdims=True)), 1e-12)
    k = k * (1.0 + (a - 1.0) * k_a)
    rh, kh, vh, ah = heads(r), heads(k), heads(v), heads(a)
    decay = jnp.exp(-jnp.exp(heads(w)))
    y = wkv7_scan(rh, decay, kh, vh, -kk, kk * ah)
    mu = jnp.mean(y, axis=-1, keepdims=True)
    var = jnp.mean(jnp.square(y - mu), axis=-1, keepdims=True)
    y = (y - mu) * lax.rsqrt(var + RWKV_GN_EPS) * lnx_w.astype(jnp.float32).reshape(H, N) + lnx_b.astype(jnp.float32).reshape(H, N)
    y = y + jnp.sum(rh * kh * r_k.astype(jnp.float32), axis=-1, keepdims=True) * vh
    return y.reshape(B, T, RWKV_W).astype(dt) * g


def hybrid_mixer(h, w_in, attn_sinks, mu_rkv, mu_wag, w0, w1, w2, a0, a1, a2, g1, g2, k_k, k_a, r_k,
                 lnx_w, lnx_b, p_attn, p_rwkv, w_o, cos, sin):
    proj = h @ w_in
    q, k, v, r_r, k_r, v_r, gate_a, gate_r = jnp.split(proj, _in_offsets(), axis=-1)
    y_a = sliding_window_sink_gqa(q, k, v, attn_sinks, cos, sin)
    y_r = rwkv7_time_mix(h, r_r, k_r, v_r, mu_rkv, mu_wag, w0, w1, w2, a0, a1, a2, g1, g2,
                         k_k, k_a, r_k, lnx_w, lnx_b)
    merged = jax.nn.sigmoid(gate_a) * (y_a @ p_attn) + jax.nn.sigmoid(gate_r) * (y_r @ p_rwkv)
    return merged @ w_o


def hierarchical_moe(h, w_group, b_group, w_expert, b_expert, w_gate, w_up, w_down):
    B, T, D = h.shape
    n_tok = B * T
    dt = h.dtype
    hf = h.reshape(n_tok, D)
    g_prob = jax.nn.softmax((hf @ w_group).astype(jnp.float32) + b_group.astype(jnp.float32), axis=-1)
    g_top, g_idx = lax.top_k(g_prob, 1)
    e_logits = ((hf @ w_expert).astype(jnp.float32) + b_expert.astype(jnp.float32)).reshape(n_tok, N_GROUPS, EXPERTS_PER_GROUP)
    in_group = e_logits[jnp.arange(n_tok), g_idx[:, 0]]
    e_top, e_idx = lax.top_k(in_group, TOP_K_IN_GROUP)
    gate = g_top * jax.nn.softmax(e_top, axis=-1)
    expert_id = g_idx * EXPERTS_PER_GROUP + e_idx
    n_assign = n_tok * TOP_K_IN_GROUP
    flat_e = expert_id.reshape(-1)
    flat_tok = jnp.repeat(jnp.arange(n_tok, dtype=jnp.int32), TOP_K_IN_GROUP)
    flat_gate = gate.reshape(-1)
    order = jnp.argsort(flat_e)
    se, stok, sgate = flat_e[order], flat_tok[order], flat_gate[order]
    counts = jnp.bincount(flat_e, length=N_EXPERTS)
    starts = jnp.cumsum(counts) - counts
    padded = (counts + MOE_BLOCK - 1) // MOE_BLOCK * MOE_BLOCK
    pends = jnp.cumsum(padded)
    pstarts = pends - padded
    dest = pstarts[se] + jnp.arange(n_assign, dtype=jnp.int32) - starts[se]
    n_blocks = (n_assign + N_EXPERTS * (MOE_BLOCK - 1) + MOE_BLOCK - 1) // MOE_BLOCK
    n_rows = n_blocks * MOE_BLOCK
    row_tok = jnp.full((n_rows,), n_tok, jnp.int32).at[dest].set(stok)
    row_gate = jnp.zeros((n_rows,), jnp.float32).at[dest].set(sgate)
    blk_expert = jnp.minimum(jnp.searchsorted(pends, jnp.arange(n_blocks, dtype=pends.dtype) * MOE_BLOCK, side='right'), N_EXPERTS - 1)
    h_pad = jnp.concatenate([hf, jnp.zeros((1, D), dt)], axis=0)
    xb = h_pad[row_tok].reshape(n_blocks, MOE_BLOCK, D)

    def expert_block(args):
        xblk, e = args
        act = jax.nn.silu(xblk @ w_gate[e]) * (xblk @ w_up[e])
        return act @ w_down[e]

    yb = lax.map(expert_block, (xb, blk_expert)).reshape(n_rows, D)
    out = jnp.zeros((n_tok + 1, D), dt).at[row_tok].add(yb * row_gate[:, None].astype(dt))
    return out[:n_tok].reshape(B, T, D)


def setup_inputs(seed: int = 0) -> dict:
    key = jax.random.key(seed)
    ks = iter(jax.random.split(key, 40))
    L, D, C, H, N = DEPTH, D_MODEL, RWKV_W, RWKV_HEADS, RWKV_HEAD_DIM
    E, F = N_EXPERTS, EXPERT_FF
    beta = (8.0 * DEPTH) ** -0.25

    def nrm(shape, scale):
        return jax.random.normal(next(ks), shape, jnp.float32) * scale

    def uni(shape, lo, hi):
        return jax.random.uniform(next(ks), shape, jnp.float32, lo, hi)

    return {
        'x': nrm((BATCH, SEQ, D), 1.0),
        'w_in': nrm((L, D, IN_W), D ** -0.5),
        'attn_sinks': nrm((L, ATTN_Q_HEADS), 0.5),
        'rw_mu_rkv': uni((L, 3, C), 0.0, 1.0),
        'rw_mu_wag': uni((L, 3, D), 0.0, 1.0),
        'rw_w0': uni((L, C), -6.0, 0.0),
        'rw_w1': nrm((L, D, DECAY_LORA), D ** -0.5),
        'rw_w2': nrm((L, DECAY_LORA, C), 0.5 * DECAY_LORA ** -0.5),
        'rw_a0': nrm((L, C), 0.1),
        'rw_a1': nrm((L, D, AAA_LORA), D ** -0.5),
        'rw_a2': nrm((L, AAA_LORA, C), AAA_LORA ** -0.5),
        'rw_g1': nrm((L, D, GATE_LORA), D ** -0.5),
        'rw_g2': nrm((L, GATE_LORA, C), GATE_LORA ** -0.5),
        'rw_k_k': 0.85 + nrm((L, C), 0.02),
        'rw_k_a': 1.0 + nrm((L, C), 0.02),
        'rw_r_k': nrm((L, H, N), 0.1),
        'rw_lnx_w': 1.0 + nrm((L, C), 0.02),
        'rw_lnx_b': nrm((L, C), 0.02),
        'p_attn': nrm((L, ATTN_Q_W, D), ATTN_Q_W ** -0.5),
        'p_rwkv': nrm((L, C, D), C ** -0.5),
        'w_o': nrm((L, D, D), beta * D ** -0.5),
        'ln1_g': 1.0 + nrm((L, D), 0.02),
        'ln1_b': nrm((L, D), 0.02),
        'w_group': nrm((L, D, N_GROUPS), D ** -0.5),
        'b_group': nrm((L, N_GROUPS), 0.01),
        'w_expert': nrm((L, D, E), D ** -0.5),
        'b_expert': nrm((L, E), 0.01),
        'w_gate': nrm((L, E, D, F), D ** -0.5),
        'w_up': nrm((L, E, D, F), D ** -0.5),
        'w_down': nrm((L, E, F, D), beta * F ** -0.5),
        'ln2_g': 1.0 + nrm((L, D), 0.02),
        'ln2_b': nrm((L, D), 0.02),
    }


def reference(x, w_in, attn_sinks, rw_mu_rkv, rw_mu_wag, rw_w0, rw_w1, rw_w2, rw_a0, rw_a1, rw_a2,
              rw_g1, rw_g2, rw_k_k, rw_k_a, rw_r_k, rw_lnx_w, rw_lnx_b, p_attn, p_rwkv, w_o,
              ln1_g, ln1_b, w_group, b_group, w_expert, b_expert, w_gate, w_up, w_down, ln2_g, ln2_b):
    alpha = (2.0 * DEPTH) ** 0.25
    cos, sin = rope_tables(x.shape[1])
    h = x
    for l in range(DEPTH):
        mix = hybrid_mixer(h, w_in[l], attn_sinks[l], rw_mu_rkv[l], rw_mu_wag[l], rw_w0[l], rw_w1[l], rw_w2[l],
                           rw_a0[l], rw_a1[l], rw_a2[l], rw_g1[l], rw_g2[l], rw_k_k[l], rw_k_a[l], rw_r_k[l],
                           rw_lnx_w[l], rw_lnx_b[l], p_attn[l], p_rwkv[l], w_o[l], cos, sin)
        h = layer_norm(alpha * h + mix, ln1_g[l], ln1_b[l])
        ff = hierarchical_moe(h, w_group[l], b_group[l], w_expert[l], b_expert[l], w_gate[l], w_up[l], w_down[l])
        h = layer_norm(alpha * h + ff, ln2_g[l], ln2_b[l])
    return h
```

```python
import functools

import jax
import jax.numpy as jnp
from jax import lax
from jax.experimental import pallas as pl
from jax.experimental.pallas import tpu as pltpu

F32 = jnp.float32
BF16 = jnp.bfloat16
I32 = jnp.int32

HEAD_DIM = 64
ATTN_Q_HEADS = 16
ATTN_KV_HEADS = 4
ATTN_GROUP = ATTN_Q_HEADS // ATTN_KV_HEADS
ATTN_Q_W = ATTN_Q_HEADS * HEAD_DIM
ATTN_KV_W = ATTN_KV_HEADS * HEAD_DIM
WINDOW = 128
ROPE_THETA = 10000.0
RWKV_HEADS = 16
RWKV_N = 64
RWKV_W = RWKV_HEADS * RWKV_N
RWKV_GN_EPS = 64e-5
N_GROUPS = 8
EXPERTS_PER_GROUP = 8
N_EXPERTS = N_GROUPS * EXPERTS_PER_GROUP
TOP_K = 2
LN_EPS = 1e-5

LANES = 128
SUBLANES = 8
VMEM_LIMIT_BYTES = 56 * 1024 * 1024

WKV_CHUNK = 64
WKV_CHUNKS_PER_STEP = 8
MOE_ROWS = 128
MOE_SUBS_PER_ITEM = 8
MOE_FF_TILE = 128
ROUTE_TILE = 512


def _cparams(sem, vmem=VMEM_LIMIT_BYTES):
    return pltpu.CompilerParams(dimension_semantics=sem, vmem_limit_bytes=vmem)


def _sigmoid(x):
    return 1.0 / (1.0 + jnp.exp(-x))


def _dot(a, b):
    return jnp.dot(a.astype(BF16), b.astype(BF16), preferred_element_type=F32)


def _dot_nt(a, b):
    return lax.dot_general(a.astype(BF16), b.astype(BF16), (((1,), (1,)), ((), ())),
                           preferred_element_type=F32)


def _layer_norm(t, g, b):
    mu = jnp.mean(t, axis=-1, keepdims=True)
    d = t - mu
    var = jnp.mean(d * d, axis=-1, keepdims=True)
    return d * lax.rsqrt(var + LN_EPS) * g + b


def _matmul_kernel(a_ref, b_ref, o_ref):
    o_ref[...] = jnp.dot(a_ref[...], b_ref[...], preferred_element_type=F32).astype(o_ref.dtype)


def _matmul_cols(a, b, col0, ncols, out_dtype, tm=1024, tn=512):
    m, k = a.shape
    tm = min(tm, m)
    cb = col0 // tn
    assert col0 % tn == 0 and ncols % tn == 0 and m % tm == 0
    return pl.pallas_call(
        _matmul_kernel,
        grid=(m // tm, ncols // tn),
        in_specs=[pl.BlockSpec((tm, k), lambda i, j: (i, 0)),
                  pl.BlockSpec((k, tn), lambda i, j: (0, j + cb))],
        out_specs=pl.BlockSpec((tm, tn), lambda i, j: (i, j)),
        out_shape=jax.ShapeDtypeStruct((m, ncols), out_dtype),
        compiler_params=_cparams(("parallel", "arbitrary")),
        name="inproj_matmul",
    )(a, b)


def _rope(x, cosb, sinb):
    half = HEAD_DIM // 2
    lane = lax.broadcasted_iota(I32, cosb.shape, 1)
    first_half = (lane % HEAD_DIM) < half
    outs = []
    for g in range(x.shape[1] // LANES):
        xg = x[:, g * LANES:(g + 1) * LANES]
        partner = jnp.where(first_half, pltpu.roll(xg, LANES - half, axis=1), pltpu.roll(xg, half, axis=1))
        outs.append(xg * cosb + partner * sinb)
    return outs


def _attn_kernel(sinks_ref, q_ref, kc_ref, kp_ref, vc_ref, vp_ref, cosc_ref, sinc_ref, cosp_ref, sinp_ref, o_ref):
    blk = pl.program_id(1)
    tq = q_ref.shape[1]
    qg = _rope(q_ref[0], cosc_ref[...], sinc_ref[...])
    kcg = _rope(kc_ref[0], cosc_ref[...], sinc_ref[...])
    kpg = _rope(kp_ref[0], cosp_ref[...], sinp_ref[...])
    vc = vc_ref[0]
    vp = vp_ref[0]

    def head(groups, h):
        g = groups[h // 2]
        return g[:, (h % 2) * HEAD_DIM:(h % 2 + 1) * HEAD_DIM]

    rows = ATTN_GROUP * tq
    qi = lax.broadcasted_iota(I32, (rows, 2 * tq), 0) % tq
    kj = lax.broadcasted_iota(I32, (rows, 2 * tq), 1)
    dist = qi + tq - kj
    valid = (dist >= 0) & (dist < WINDOW) & ((blk > 0) | (kj >= tq))
    rid = lax.broadcasted_iota(I32, (rows, 1), 0) // tq
    scale = HEAD_DIM ** -0.5
    for kvh in range(ATTN_KV_HEADS):
        qh = jnp.concatenate([head(qg, kvh * ATTN_GROUP + g) for g in range(ATTN_GROUP)], axis=0)
        kw = jnp.concatenate([head(kpg, kvh), head(kcg, kvh)], axis=0)
        vw = jnp.concatenate([vp[:, kvh * HEAD_DIM:(kvh + 1) * HEAD_DIM],
                              vc[:, kvh * HEAD_DIM:(kvh + 1) * HEAD_DIM]], axis=0)
        s = _dot_nt(qh, kw) * scale
        s = jnp.where(valid, s, -jnp.inf)
        sink = jnp.zeros((rows, 1), F32)
        for g in range(ATTN_GROUP):
            sink = jnp.where(rid == g, sinks_ref[kvh * ATTN_GROUP + g], sink)
        m = jnp.maximum(jnp.max(s, axis=-1, keepdims=True), sink)
        e = jnp.exp(s - m)
        denom = jnp.sum(e, axis=-1, keepdims=True) + jnp.exp(sink - m)
        p = e / denom
        o = _dot(p, vw)
        for g in range(ATTN_GROUP):
            hq = kvh * ATTN_GROUP + g
            o_ref[0, :, hq * HEAD_DIM:(hq + 1) * HEAD_DIM] = o[g * tq:(g + 1) * tq]


def _attention(qkv, sinks, cosb, sinb):
    b, t, _ = qkv.shape
    tq = WINDOW
    nb = t // tq
    kcol = ATTN_Q_W // ATTN_KV_W
    prev = lambda i: jnp.maximum(i - 1, 0)
    grid_spec = pltpu.PrefetchScalarGridSpec(
        num_scalar_prefetch=0,
        grid=(b, nb),
        in_specs=[
            pl.BlockSpec(memory_space=pltpu.SMEM),
            pl.BlockSpec((1, tq, ATTN_Q_W), lambda bi, i: (bi, i, 0)),
            pl.BlockSpec((1, tq, ATTN_KV_W), lambda bi, i: (bi, i, kcol)),
            pl.BlockSpec((1, tq, ATTN_KV_W), lambda bi, i: (bi, prev(i), kcol)),
            pl.BlockSpec((1, tq, ATTN_KV_W), lambda bi, i: (bi, i, kcol + 1)),
            pl.BlockSpec((1, tq, ATTN_KV_W), lambda bi, i: (bi, prev(i), kcol + 1)),
            pl.BlockSpec((tq, LANES), lambda bi, i: (i, 0)),
            pl.BlockSpec((tq, LANES), lambda bi, i: (i, 0)),
            pl.BlockSpec((tq, LANES), lambda bi, i: (prev(i), 0)),
            pl.BlockSpec((tq, LANES), lambda bi, i: (prev(i), 0)),
        ],
        out_specs=pl.BlockSpec((1, tq, ATTN_Q_W), lambda bi, i: (bi, i, 0)),
    )
    return pl.pallas_call(
        _attn_kernel,
        grid_spec=grid_spec,
        out_shape=jax.ShapeDtypeStruct((b, t, ATTN_Q_W), F32),
        compiler_params=_cparams(("parallel", "arbitrary")),
        name="swa_attention",
    )(sinks, qkv, qkv, qkv, qkv, qkv, cosb, sinb, cosb, sinb)


def _rope_tables(t):
    inv = 1.0 / (ROPE_THETA ** (jnp.arange(0, HEAD_DIM, 2, dtype=F32) / HEAD_DIM))
    ang = jnp.arange(t, dtype=F32)[:, None] * inv[None, :]
    cos, sin = jnp.cos(ang), jnp.sin(ang)
    reps = LANES // HEAD_DIM
    cosb = jnp.tile(jnp.concatenate([cos, cos], axis=-1), (1, reps))
    sinb = jnp.tile(jnp.concatenate([-sin, sin], axis=-1), (1, reps))
    return cosb, sinb


def _rwkv_prep_kernel(h_ref, hp_ref, r_ref, k_ref, v_ref, rp_ref, kp_ref, vp_ref,
                      mu_rkv_ref, mu_wag_ref, w0_ref, w1_ref, w2_ref, a0_ref, a1_ref, a2_ref,
                      g1_ref, g2_ref, kk_ref, ka_ref,
                      ro_ref, ko_ref, vo_ref, lwo_ref, kko_ref, ago_ref, go_ref):
    first = pl.program_id(1) == 0
    last_row = SUBLANES - 1

    def shifted(cur, prev_ref):
        prev_row = jnp.where(first, 0.0, prev_ref[0, last_row:last_row + 1, :])
        rowid = lax.broadcasted_iota(I32, cur.shape, 0)
        return jnp.where(rowid == 0, prev_row, pltpu.roll(cur, 1, axis=0))

    h = h_ref[0]
    xx = shifted(h, hp_ref) - h
    xw = h + xx * mu_wag_ref[0:1, :]
    xa = h + xx * mu_wag_ref[1:2, :]
    xg = h + xx * mu_wag_ref[2:3, :]
    w_raw = w0_ref[...] + _dot(jnp.tanh(_dot(xw, w1_ref[...])), w2_ref[...])
    neg = -w_raw
    softplus = jnp.maximum(neg, 0.0) + jnp.log1p(jnp.exp(-jnp.abs(neg)))
    w = -softplus - 0.5
    lw = -jnp.exp(w)
    ag = _sigmoid(a0_ref[...] + _dot(_dot(xa, a1_ref[...]), a2_ref[...]))
    go_ref[0] = _dot(_sigmoid(_dot(xg, g1_ref[...])), g2_ref[...])

    r = r_ref[0]
    k = k_ref[0]
    v = v_ref[0]
    r = r + (shifted(r, rp_ref) - r) * mu_rkv_ref[0:1, :]
    k = k + (shifted(k, kp_ref) - k) * mu_rkv_ref[1:2, :]
    v = v + (shifted(v, vp_ref) - v) * mu_rkv_ref[2:3, :]
    kk = k * kk_ref[...]
    k = k * (1.0 + (ag - 1.0) * ka_ref[...])
    for hd in range(RWKV_HEADS):
        sl = slice(hd * RWKV_N, (hd + 1) * RWKV_N)
        ro_ref[0, hd] = r[:, sl]
        ko_ref[0, hd] = k[:, sl]
        vo_ref[0, hd] = v[:, sl]
        lwo_ref[0, hd] = lw[:, sl]
        kko_ref[0, hd] = kk[:, sl]
        ago_ref[0, hd] = ag[:, sl]


def _rwkv_prep(h, rkv, mu_rkv, mu_wag, w0, w1, w2, a0, a1, a2, g1, g2, k_k, k_a, tm=256):
    b, t, d = h.shape
    tm = min(tm, t)
    c = RWKV_W
    spt = tm // SUBLANES
    prevblk = lambda i: jnp.maximum(i * spt - 1, 0)
    full = lambda arr: pl.BlockSpec(arr.shape, lambda bi, i: (0,) * arr.ndim)
    row = lambda arr: arr.reshape(1, -1)
    w0, a0, k_k, k_a = row(w0), row(a0), row(k_k), row(k_a)
    in_specs = [
        pl.BlockSpec((1, tm, d), lambda bi, i: (bi, i, 0)),
        pl.BlockSpec((1, SUBLANES, d), lambda bi, i: (bi, prevblk(i), 0)),
        pl.BlockSpec((1, tm, c), lambda bi, i: (bi, i, 0)),
        pl.BlockSpec((1, tm, c), lambda bi, i: (bi, i, 1)),
        pl.BlockSpec((1, tm, c), lambda bi, i: (bi, i, 2)),
        pl.BlockSpec((1, SUBLANES, c), lambda bi, i: (bi, prevblk(i), 0)),
        pl.BlockSpec((1, SUBLANES, c), lambda bi, i: (bi, prevblk(i), 1)),
        pl.BlockSpec((1, SUBLANES, c), lambda bi, i: (bi, prevblk(i), 2)),
    ] + [full(a) for a in (mu_rkv, mu_wag, w0, w1, w2, a0, a1, a2, g1, g2, k_k, k_a)]
    hm = jax.ShapeDtypeStruct((b, RWKV_HEADS, t, RWKV_N), F32)
    hm_spec = pl.BlockSpec((1, RWKV_HEADS, tm, RWKV_N), lambda bi, i: (bi, 0, i, 0))
    return pl.pallas_call(
        _rwkv_prep_kernel,
        grid=(b, t // tm),
        in_specs=in_specs,
        out_specs=[hm_spec] * 6 + [pl.BlockSpec((1, tm, c), lambda bi, i: (bi, i, 0))],
        out_shape=[hm] * 6 + [jax.ShapeDtypeStruct((b, t, c), F32)],
        compiler_params=_cparams(("parallel", "arbitrary")),
        name="rwkv_prep",
    )(h, h, rkv, rkv, rkv, rkv, rkv, rkv, mu_rkv, mu_wag, w0, w1, w2, a0, a1, a2, g1, g2, k_k, k_a)


def _split3(x):
    hi = x.astype(BF16)
    r1 = x - hi.astype(F32)
    mid = r1.astype(BF16)
    lo = (r1 - mid.astype(F32)).astype(BF16)
    return hi, mid, lo


def _wkv_kernel(r_ref, k_ref, v_ref, lw_ref, kk_ref, ag_ref, rk_ref, lnw_ref, lnb_ref, y_ref, s_ref):
    c = WKV_CHUNK
    n = RWKV_N

    @pl.when(pl.program_id(2) == 0)
    def _():
        s_ref[...] = jnp.zeros_like(s_ref)

    row = lax.broadcasted_iota(I32, (c, c), 0)
    col = lax.broadcasted_iota(I32, (c, c), 1)
    strict = row > col
    incl = row >= col
    eye = (row == col).astype(F32)
    tri = incl.astype(BF16)
    rk = rk_ref[0]
    lnw = lnw_ref[0]
    lnb = lnb_ref[0]
    s = s_ref[...]
    for ci in range(r_ref.shape[2] // c):
        sl = pl.ds(ci * c, c)
        r = r_ref[0, 0, sl, :]
        k = k_ref[0, 0, sl, :]
        v = v_ref[0, 0, sl, :]
        lw = lw_ref[0, 0, sl, :]
        kk = kk_ref[0, 0, sl, :]
        ag = ag_ref[0, 0, sl, :]
        kk = kk / jnp.maximum(jnp.sqrt(jnp.sum(kk * kk, axis=-1, keepdims=True)), 1e-12)
        a = -kk
        b = kk * ag
        cum = sum(jnp.dot(tri, p, preferred_element_type=F32) for p in _split3(lw))
        dinc = jnp.exp(cum)
        dinv = jnp.exp(-cum)
        rt = r * dinc
        kt = k * dinv
        at = a * jnp.exp(cum - lw)
        bt = b * dinv
        d_chunk = dinc[c - 1:c, :]
        p = _dot_nt(jnp.concatenate([at, rt], axis=0), jnp.concatenate([bt, kt], axis=0))
        a_ab = jnp.where(strict, p[:c, :c], 0.0)
        a_ak = jnp.where(strict, p[:c, c:], 0.0)
        a_rb = jnp.where(incl, p[c:, :c], 0.0)
        a_rk = jnp.where(incl, p[c:, c:], 0.0)
        x = a_ab
        tinv = eye + x
        for _ in range(c.bit_length() - 2):
            x = _dot(x, x)
            tinv = tinv + _dot(tinv, x)
        z = _dot(a_ak, v)
        ta = _dot(tinv, jnp.concatenate([at, z], axis=1))
        ry = _dot(a_rb, ta)
        rp = rt + ry[:, :n]
        yv = ry[:, n:] + _dot(a_rk, v)
        vt = _dot_nt(eye, v)
        uvt = _dot_nt(_dot_nt(vt, a_ak), tinv)
        nk = _dot(vt, kt)
        ut = _dot_nt(s, ta[:, :n]) + uvt
        y = _dot_nt(rp, s) + yv
        s = (s + _dot(ut, bt) + nk) * d_chunk
        mu = jnp.mean(y, axis=-1, keepdims=True)
        yc = y - mu
        var = jnp.mean(yc * yc, axis=-1, keepdims=True)
        y = yc * lax.rsqrt(var + RWKV_GN_EPS) * lnw + lnb
        y = y + jnp.sum(r * k * rk, axis=-1, keepdims=True) * v
        y_ref[0, 0, sl, :] = y
    s_ref[...] = s


def _wkv_scan(r, k, v, lw, kk, ag, r_k, lnx_w, lnx_b):
    b, hh, t, n = r.shape
    tc = min(WKV_CHUNK * WKV_CHUNKS_PER_STEP, t)
    blk = pl.BlockSpec((1, 1, tc, n), lambda bi, hi, i: (bi, hi, i, 0))
    par = pl.BlockSpec((1, 1, n), lambda bi, hi, i: (hi, 0, 0))
    per_head = lambda arr: arr.reshape(hh, 1, n)
    return pl.pallas_call(
        _wkv_kernel,
        grid=(b, hh, t // tc),
        in_specs=[blk] * 6 + [par] * 3,
        out_specs=blk,
        out_shape=jax.ShapeDtypeStruct((b, hh, t, n), F32),
        scratch_shapes=[pltpu.VMEM((n, n), F32)],
        compiler_params=_cparams(("parallel", "parallel", "arbitrary")),
        name="wkv7_scan",
    )(r, k, v, lw, kk, ag, per_head(r_k), per_head(lnx_w), per_head(lnx_b))


def _merge_kernel(ya_ref, yr_ref, g_ref, ga_ref, gr_ref, pa_ref, pr_ref, o_ref):
    ma = _dot(ya_ref[...], pa_ref[...])
    mr = _dot(yr_ref[...] * g_ref[...], pr_ref[...])
    o_ref[...] = (_sigmoid(ga_ref[...]) * ma + _sigmoid(gr_ref[...]) * mr).astype(o_ref.dtype)


def _merge(ya, yr, g, gates, p_attn, p_rwkv, tm=256):
    m, c = ya.shape
    d = p_attn.shape[1]
    tm = min(tm, m)
    tile = pl.BlockSpec((tm, c), lambda i: (i, 0))
    return pl.pallas_call(
        _merge_kernel,
        grid=(m // tm,),
        in_specs=[tile, tile, tile,
                  pl.BlockSpec((tm, d), lambda i: (i, 0)),
                  pl.BlockSpec((tm, d), lambda i: (i, 1)),
                  pl.BlockSpec((c, d), lambda i: (0, 0)),
                  pl.BlockSpec((c, d), lambda i: (0, 0))],
        out_specs=pl.BlockSpec((tm, d), lambda i: (i, 0)),
        out_shape=jax.ShapeDtypeStruct((m, d), BF16),
        compiler_params=_cparams(("parallel",)),
        name="gated_merge",
    )(ya, yr, g, gates, gates, p_attn, p_rwkv)


def _outproj_router_kernel(alpha, mg_ref, x_ref, wo_ref, g_ref, b_ref, wr_ref, br_ref,
                           h_ref, hrow_ref, eid_ref, gate_ref):
    mix = jnp.dot(mg_ref[...], wo_ref[...], preferred_element_type=F32)
    h = _layer_norm(alpha * x_ref[...] + mix, g_ref[...], b_ref[...])
    h_ref[...] = h
    for j in range(hrow_ref.shape[1]):
        hrow_ref[:, j, :] = h[:, j * LANES:(j + 1) * LANES]
    logits = jnp.dot(h, wr_ref[...], preferred_element_type=F32, precision=lax.Precision.HIGHEST) + br_ref[...]
    lane = lax.broadcasted_iota(I32, logits.shape, 1)
    ninf = -jnp.inf
    big = jnp.int32(2 * LANES)
    glog = jnp.where(lane < N_GROUPS, logits, ninf)
    gmax = jnp.max(glog, axis=-1, keepdims=True)
    gidx = jnp.min(jnp.where(glog == gmax, lane, big), axis=-1, keepdims=True)
    gtop = 1.0 / jnp.sum(jnp.exp(glog - gmax), axis=-1, keepdims=True)
    eg = (lane - N_GROUPS) // EXPERTS_PER_GROUP
    in_group = (lane >= N_GROUPS) & (lane < N_GROUPS + N_EXPERTS) & (eg == gidx)
    el = jnp.where(in_group, logits, ninf)
    m1 = jnp.max(el, axis=-1, keepdims=True)
    i1 = jnp.min(jnp.where(el == m1, lane, big), axis=-1, keepdims=True)
    el2 = jnp.where(lane == i1, ninf, el)
    m2 = jnp.max(el2, axis=-1, keepdims=True)
    i2 = jnp.min(jnp.where(el2 == m2, lane, big), axis=-1, keepdims=True)
    t = jnp.exp(m2 - m1)
    p1 = 1.0 / (1.0 + t)
    p2 = t / (1.0 + t)
    eid_ref[...] = jnp.where(lane == 0, i1 - N_GROUPS, jnp.where(lane == 1, i2 - N_GROUPS, 0))
    gate_ref[...] = jnp.where(lane == 0, gtop * p1, jnp.where(lane == 1, gtop * p2, 0.0))


def _outproj_router(merged, x, w_o, ln_g, ln_b, w_router, b_router, alpha, tm=256):
    m, d = x.shape
    tm = min(tm, m)
    tile = pl.BlockSpec((tm, d), lambda i: (i, 0))
    vec = pl.BlockSpec((1, d), lambda i: (0, 0))
    small = pl.BlockSpec((tm, LANES), lambda i: (i, 0))
    return pl.pallas_call(
        functools.partial(_outproj_router_kernel, alpha),
        grid=(m // tm,),
        in_specs=[tile, tile, pl.BlockSpec((d, d), lambda i: (0, 0)), vec, vec,
                  pl.BlockSpec((d, LANES), lambda i: (0, 0)), pl.BlockSpec((1, LANES), lambda i: (0, 0))],
        out_specs=[tile, pl.BlockSpec((tm, d // LANES, LANES), lambda i: (i, 0, 0)), small, small],
        out_shape=[jax.ShapeDtypeStruct((m, d), F32), jax.ShapeDtypeStruct((m, d // LANES, LANES), F32),
                   jax.ShapeDtypeStruct((m, LANES), I32), jax.ShapeDtypeStruct((m, LANES), F32)],
        compiler_params=_cparams(("parallel",)),
        name="outproj_ln_router",
    )(merged, x, w_o, ln_g.reshape(1, d), ln_b.reshape(1, d), w_router, b_router)


def _onehots(eid_ref):
    tm = eid_ref.shape[1]
    e_iota = lax.broadcasted_iota(I32, (N_EXPERTS, tm), 0)
    oh0 = (eid_ref[0:1, :] == e_iota).astype(F32)
    oh1 = (eid_ref[1:2, :] == e_iota).astype(F32)
    return oh0, oh1


def _count_kernel(eid_ref, cnt_ref):
    @pl.when(pl.program_id(0) == 0)
    def _():
        cnt_ref[...] = jnp.zeros_like(cnt_ref)

    oh0, oh1 = _onehots(eid_ref)
    cnt_ref[...] += jnp.sum(oh0 + oh1, axis=1, keepdims=True)


def _slot_kernel(eid_ref, pstart_ref, dest_ref, run_ref):
    @pl.when(pl.program_id(0) == 0)
    def _():
        run_ref[...] = jnp.zeros_like(run_ref)

    tm = eid_ref.shape[1]
    oh0, oh1 = _onehots(eid_ref)
    both = oh0 + oh1
    earlier = (lax.broadcasted_iota(I32, (tm, tm), 0) < lax.broadcasted_iota(I32, (tm, tm), 1)).astype(BF16)
    pre = jnp.dot(both.astype(BF16), earlier, preferred_element_type=F32)
    base = pre + run_ref[...] + pstart_ref[...]
    dest_ref[0:1, :] = jnp.sum(oh0 * base, axis=0, keepdims=True).astype(I32)
    dest_ref[1:2, :] = jnp.sum(oh1 * base, axis=0, keepdims=True).astype(I32)
    run_ref[...] += jnp.sum(both, axis=1, keepdims=True)


def _expert_counts(eid_t):
    m = eid_t.shape[1]
    tm = min(ROUTE_TILE, m)
    return pl.pallas_call(
        _count_kernel,
        grid=(m // tm,),
        in_specs=[pl.BlockSpec((TOP_K, tm), lambda i: (0, i))],
        out_specs=pl.BlockSpec((N_EXPERTS, 1), lambda i: (0, 0)),
        out_shape=jax.ShapeDtypeStruct((N_EXPERTS, 1), F32),
        compiler_params=_cparams(("arbitrary",)),
        name="expert_counts",
    )(eid_t)


def _expert_slots(eid_t, pstart):
    m = eid_t.shape[1]
    tm = min(ROUTE_TILE, m)
    return pl.pallas_call(
        _slot_kernel,
        grid=(m // tm,),
        in_specs=[pl.BlockSpec((TOP_K, tm), lambda i: (0, i)),
                  pl.BlockSpec((N_EXPERTS, 1), lambda i: (0, 0))],
        out_specs=pl.BlockSpec((TOP_K, tm), lambda i: (0, i)),
        out_shape=jax.ShapeDtypeStruct((TOP_K, m), I32),
        scratch_shapes=[pltpu.VMEM((N_EXPERTS, 1), F32)],
        compiler_params=_cparams(("arbitrary",)),
        name="expert_slots",
    )(eid_t, pstart)


def _dispatch_kernel(dest_ref, h_ref, xb_in_ref, xb_ref, sem):
    del xb_in_ref
    tm = DISPATCH_TILE
    t0 = pl.program_id(0) * tm

    def copy(n, k):
        tok = t0 + n
        return pltpu.make_async_copy(h_ref.at[tok], xb_ref.at[dest_ref[TOP_K * tok + k]], sem)

    def start(n, carry):
        for k in range(TOP_K):
            copy(n, k).start()
        return carry

    def wait(n, carry):
        for k in range(TOP_K):
            copy(n, k).wait()
        return carry

    lax.fori_loop(0, tm, start, 0)
    lax.fori_loop(0, tm, wait, 0)


DISPATCH_TILE = 512


def _dispatch(dest_flat, h, n_rows):
    m = h.shape[0]
    assert m % DISPATCH_TILE == 0
    grid_spec = pltpu.PrefetchScalarGridSpec(
        num_scalar_prefetch=1,
        grid=(m // DISPATCH_TILE,),
        in_specs=[pl.BlockSpec(memory_space=pl.ANY), pl.BlockSpec(memory_space=pl.ANY)],
        out_specs=pl.BlockSpec(memory_space=pl.ANY),
        scratch_shapes=[pltpu.SemaphoreType.DMA(())],
    )
    return pl.pallas_call(
        _dispatch_kernel,
        grid_spec=grid_spec,
        out_shape=jax.ShapeDtypeStruct((n_rows,) + h.shape[1:], F32),
        input_output_aliases={2: 0},
        compiler_params=pltpu.CompilerParams(dimension_semantics=("arbitrary",), has_side_effects=True),
        name="moe_dispatch",
    )(dest_flat, h, jnp.zeros((n_rows,) + h.shape[1:], F32))


def _expert_kernel(item_e_ref, item_row_ref, item_nsub_ref, wg_ref, wu_ref, wd_ref, xb_ref, yb_ref,
                   xf_ref, x_ref, y_ref, wgu_ref, sem_in, sem_out):
    del item_e_ref
    it = pl.program_id(0)
    f = pl.program_id(1)
    nf = pl.num_programs(1)
    nsub = item_nsub_ref[it]
    row0 = item_row_ref[it]
    rb = MOE_ROWS
    ft = MOE_FF_TILE

    def in_copy(s):
        return pltpu.make_async_copy(xb_ref.at[pl.ds(row0 + s * rb, rb)], xf_ref.at[pl.ds(s * rb, rb)], sem_in)

    def out_copy(s):
        return pltpu.make_async_copy(xf_ref.at[pl.ds(s * rb, rb)], yb_ref.at[pl.ds(row0 + s * rb, rb)], sem_out)

    def each(fn):
        def body(s, carry):
            fn(s)
            return carry
        lax.fori_loop(0, nsub, body, 0)

    @pl.when(nsub > 0)
    def _():
        @pl.when(f == 0)
        def _():
            each(lambda s: in_copy(s).start())
            each(lambda s: in_copy(s).wait())

            def cast(s):
                rows = pl.ds(pl.multiple_of(s * rb, rb), rb)
                for j in range(xf_ref.shape[1]):
                    x_ref[rows, j * LANES:(j + 1) * LANES] = xf_ref[rows, j, :].astype(BF16)
            each(cast)

        wgu_ref[:, :ft] = wg_ref[0].astype(BF16)
        wgu_ref[:, ft:] = wu_ref[0].astype(BF16)
        wd = wd_ref[0].astype(BF16)

        def block(s):
            rows = pl.ds(pl.multiple_of(s * rb, rb), rb)
            gu = jnp.dot(x_ref[rows, :], wgu_ref[...], preferred_element_type=F32)
            gate = gu[:, :ft]
            act = gate * _sigmoid(gate) * gu[:, ft:]
            contrib = jnp.dot(act.astype(BF16), wd, preferred_element_type=F32)

            @pl.when(f == 0)
            def _():
                y_ref[rows, :] = contrib

            @pl.when(f > 0)
            def _():
                y_ref[rows, :] += contrib
        each(block)

        @pl.when(f == nf - 1)
        def _():
            def stage(s):
                rows = pl.ds(pl.multiple_of(s * rb, rb), rb)
                for j in range(xf_ref.shape[1]):
                    xf_ref[rows, j, :] = y_ref[rows, j * LANES:(j + 1) * LANES]
            each(stage)
            each(lambda s: out_copy(s).start())
            each(lambda s: out_copy(s).wait())


def _experts(item_e, item_row, item_nsub, w_gate, w_up, w_down, xb):
    d = w_gate.shape[1]
    ff = w_gate.shape[2]
    nf = ff // MOE_FF_TILE
    n_items = item_e.shape[0]
    rows_max = MOE_ROWS * MOE_SUBS_PER_ITEM

    def f_eff(f, ns, it):
        return jnp.where(ns[it] > 0, f, nf - 1)

    grid_spec = pltpu.PrefetchScalarGridSpec(
        num_scalar_prefetch=3,
        grid=(n_items, nf),
        in_specs=[
            pl.BlockSpec((1, d, MOE_FF_TILE), lambda it, f, ie, ir, ns: (ie[it], 0, f_eff(f, ns, it))),
            pl.BlockSpec((1, d, MOE_FF_TILE), lambda it, f, ie, ir, ns: (ie[it], 0, f_eff(f, ns, it))),
            pl.BlockSpec((1, MOE_FF_TILE, d), lambda it, f, ie, ir, ns: (ie[it], f_eff(f, ns, it), 0)),
            pl.BlockSpec(memory_space=pl.ANY),
        ],
        out_specs=pl.BlockSpec(memory_space=pl.ANY),
        scratch_shapes=[
            pltpu.VMEM((rows_max,) + xb.shape[1:], F32),
            pltpu.VMEM((rows_max, d), BF16),
            pltpu.VMEM((rows_max, d), F32),
            pltpu.VMEM((d, 2 * MOE_FF_TILE), BF16),
            pltpu.SemaphoreType.DMA(()),
            pltpu.SemaphoreType.DMA(()),
        ],
    )
    return pl.pallas_call(
        _expert_kernel,
        grid_spec=grid_spec,
        out_shape=jax.ShapeDtypeStruct(xb.shape, F32),
        input_output_aliases={6: 0},
        compiler_params=pltpu.CompilerParams(dimension_semantics=("arbitrary", "arbitrary"),
                                             vmem_limit_bytes=VMEM_LIMIT_BYTES, has_side_effects=True),
        name="moe_experts",
    )(item_e, item_row, item_nsub, w_gate, w_up, w_down, xb)


COMBINE_TILE = 256


def _combine_kernel(alpha, dest_ref, yb_ref, h_ref, gate_ref, g_ref, b_ref, o_ref, buf_ref, sem):
    tm = COMBINE_TILE
    t0 = pl.program_id(0) * tm

    def copy(n, k):
        return pltpu.make_async_copy(yb_ref.at[dest_ref[TOP_K * (t0 + n) + k]], buf_ref.at[k, n], sem)

    def start(n, carry):
        for k in range(TOP_K):
            copy(n, k).start()
        return carry

    def wait(n, carry):
        for k in range(TOP_K):
            copy(n, k).wait()
        return carry

    lax.fori_loop(0, tm, start, 0)
    lax.fori_loop(0, tm, wait, 0)
    gate = gate_ref[...]
    g0 = gate[:, 0:1]
    g1 = gate[:, 1:2]
    ff = jnp.concatenate([g0 * buf_ref[0, :, j, :] + g1 * buf_ref[1, :, j, :] for j in range(buf_ref.shape[2])], axis=1)
    o_ref[...] = _layer_norm(alpha * h_ref[...] + ff, g_ref[...], b_ref[...])


def _combine(dest_flat, yb, h, gates, ln_g, ln_b, alpha):
    m, d = h.shape
    tm = COMBINE_TILE
    assert m % tm == 0
    grid_spec = pltpu.PrefetchScalarGridSpec(
        num_scalar_prefetch=1,
        grid=(m // tm,),
        in_specs=[pl.BlockSpec(memory_space=pl.ANY),
                  pl.BlockSpec((tm, d), lambda i, dr: (i, 0)),
                  pl.BlockSpec((tm, LANES), lambda i, dr: (i, 0)),
                  pl.BlockSpec((1, d), lambda i, dr: (0, 0)),
                  pl.BlockSpec((1, d), lambda i, dr: (0, 0))],
        out_specs=pl.BlockSpec((tm, d), lambda i, dr: (i, 0)),
        scratch_shapes=[pltpu.VMEM((TOP_K, tm) + yb.shape[1:], F32), pltpu.SemaphoreType.DMA(())],
    )
    return pl.pallas_call(
        functools.partial(_combine_kernel, alpha),
        grid_spec=grid_spec,
        out_shape=jax.ShapeDtypeStruct((m, d), F32),
        compiler_params=_cparams(("arbitrary",)),
        name="moe_combine_ln",
    )(dest_flat, yb, h, gates, ln_g.reshape(1, d), ln_b.reshape(1, d))


def _moe_tables(counts):
    nsub_e = (counts + MOE_ROWS - 1) // MOE_ROWS
    pstart = (jnp.cumsum(nsub_e) - nsub_e) * MOE_ROWS
    nitem_e = (nsub_e + MOE_SUBS_PER_ITEM - 1) // MOE_SUBS_PER_ITEM
    item_end = jnp.cumsum(nitem_e)
    return nsub_e, pstart, nitem_e, item_end


def _moe(h, h_rows, eid, gates, w_gate, w_up, w_down, ln_g, ln_b, alpha):
    m, d = h.shape
    n_assign = m * TOP_K
    n_blocks = (n_assign + N_EXPERTS * (MOE_ROWS - 1) + MOE_ROWS - 1) // MOE_ROWS
    n_rows = n_blocks * MOE_ROWS
    n_items = N_EXPERTS + n_assign // (MOE_ROWS * MOE_SUBS_PER_ITEM)

    eid_t = eid[:, :TOP_K].T
    counts = _expert_counts(eid_t)[:, 0].astype(I32)
    nsub_e, pstart, nitem_e, item_end = _moe_tables(counts)
    dest_t = _expert_slots(eid_t, pstart.astype(F32).reshape(N_EXPERTS, 1))
    dest_flat = dest_t.T.reshape(-1)

    it = jnp.arange(n_items, dtype=I32)
    total_items = item_end[-1]
    item_e = jnp.minimum(jnp.sum(item_end[None, :] <= it[:, None], axis=1).astype(I32), N_EXPERTS - 1)
    j = it - (item_end - nitem_e)[item_e]
    used = it < total_items
    last_e = item_e[jnp.maximum(total_items - 1, 0)]
    item_nsub = jnp.where(used, jnp.clip(nsub_e[item_e] - j * MOE_SUBS_PER_ITEM, 0, MOE_SUBS_PER_ITEM), 0).astype(I32)
    item_row = jnp.where(used, pstart[item_e] + j * (MOE_ROWS * MOE_SUBS_PER_ITEM), 0).astype(I32)
    item_e = jnp.where(used, item_e, last_e).astype(I32)

    xb = _dispatch(dest_flat, h_rows, n_rows)
    yb = _experts(item_e, item_row, item_nsub, w_gate, w_up, w_down, xb)
    return _combine(dest_flat, yb, h, gates, ln_g, ln_b, alpha)


def _pad_cols(w, n):
    return jnp.pad(w, ((0, 0), (0, n - w.shape[1])))


def _pad_rows(w, n):
    return jnp.pad(w, ((0, n - w.shape[0]), (0, 0)))


def kernel(x, w_in, attn_sinks, rw_mu_rkv, rw_mu_wag, rw_w0, rw_w1, rw_w2, rw_a0, rw_a1, rw_a2, rw_g1, rw_g2, rw_k_k, rw_k_a, rw_r_k, rw_lnx_w, rw_lnx_b, p_attn, p_rwkv, w_o, ln1_g, ln1_b, w_group, b_group, w_expert, b_expert, w_gate, w_up, w_down, ln2_g, ln2_b):
    b, t, d = x.shape
    depth = w_in.shape[0]
    m = b * t
    alpha = (2.0 * depth) ** 0.25
    cosb, sinb = _rope_tables(t)
    qkv_w = ATTN_Q_W + 2 * ATTN_KV_W
    rkv_w = 3 * RWKV_W
    h = x
    for l in range(depth):
        hf = h.reshape(m, d)
        hb = hf.astype(BF16)
        w_in_b = w_in[l].astype(BF16)
        qkv = _matmul_cols(hb, w_in_b, 0, qkv_w, F32)
        rkv = _matmul_cols(hb, w_in_b, qkv_w, rkv_w, F32)
        gates = _matmul_cols(hb, w_in_b, qkv_w + rkv_w, 2 * d, F32)

        y_a = _attention(qkv.reshape(b, t, qkv_w), attn_sinks[l], cosb, sinb)

        lora = lambda w, n: _pad_cols(w, n).astype(BF16)
        lorb = lambda w, n: _pad_rows(w, n).astype(BF16)
        n_w = -(-rw_w1.shape[2] // LANES) * LANES
        n_a = -(-rw_a1.shape[2] // LANES) * LANES
        n_g = -(-rw_g1.shape[2] // LANES) * LANES
        r_, k_, v_, lw_, kk_, ag_, g_ = _rwkv_prep(
            h, rkv.reshape(b, t, rkv_w), rw_mu_rkv[l], rw_mu_wag[l], rw_w0[l],
            lora(rw_w1[l], n_w), lorb(rw_w2[l], n_w), rw_a0[l], lora(rw_a1[l], n_a), lorb(rw_a2[l], n_a),
            lora(rw_g1[l], n_g), lorb(rw_g2[l], n_g), rw_k_k[l], rw_k_a[l])
        y_r = _wkv_scan(r_, k_, v_, lw_, kk_, ag_, rw_r_k[l], rw_lnx_w[l], rw_lnx_b[l])
        y_r = jnp.swapaxes(y_r, 1, 2).reshape(m, RWKV_W)

        merged = _merge(y_a.reshape(m, ATTN_Q_W), y_r, g_.reshape(m, RWKV_W), gates,
                        p_attn[l].astype(BF16), p_rwkv[l].astype(BF16))
        w_router = _pad_cols(jnp.concatenate([w_group[l], w_expert[l]], axis=1), LANES)
        b_router = _pad_cols(jnp.concatenate([b_group[l], b_expert[l]])[None, :], LANES)
        h1, h1_rows, eid, gate = _outproj_router(merged, hf, w_o[l].astype(BF16), ln1_g[l], ln1_b[l],
                                        w_router, b_router, alpha)
        h2 = _moe(h1, h1_rows, eid, gate, w_gate[l], w_up[l], w_down[l], ln2_g[l], ln2_b[l], alpha)
        h = h2.reshape(b, t, d)
    return h
```

```python
import functools

import jax
import jax.numpy as jnp
from jax import lax
from jax.experimental import pallas as pl
from jax.experimental.pallas import tpu as pltpu

F32 = jnp.float32
BF16 = jnp.bfloat16
I32 = jnp.int32

HEAD_DIM = 64
ATTN_Q_HEADS = 16
ATTN_KV_HEADS = 4
ATTN_GROUP = ATTN_Q_HEADS // ATTN_KV_HEADS
ATTN_Q_W = ATTN_Q_HEADS * HEAD_DIM
ATTN_KV_W = ATTN_KV_HEADS * HEAD_DIM
WINDOW = 128
ROPE_THETA = 10000.0
RWKV_HEADS = 16
RWKV_N = 64
RWKV_W = RWKV_HEADS * RWKV_N
RWKV_GN_EPS = 64e-5
N_GROUPS = 8
EXPERTS_PER_GROUP = 8
N_EXPERTS = N_GROUPS * EXPERTS_PER_GROUP
TOP_K = 2
LN_EPS = 1e-5

LANES = 128
SUBLANES = 8
VMEM_LIMIT_BYTES = 56 * 1024 * 1024

WKV_CHUNK = 64
WKV_CHUNKS_PER_STEP = 8
WKV_HEADS_PER_STEP = 2
MOE_ROWS = 128
MOE_SUBS_PER_ITEM = 8
MOE_FF_TILE = 128
ROUTE_TILE = 512


def _cparams(sem, vmem=VMEM_LIMIT_BYTES):
    return pltpu.CompilerParams(dimension_semantics=sem, vmem_limit_bytes=vmem)


def _sigmoid(x):
    return 1.0 / (1.0 + jnp.exp(-x))


def _dot(a, b):
    return jnp.dot(a.astype(BF16), b.astype(BF16), preferred_element_type=F32)


def _dot_nt(a, b):
    return lax.dot_general(a.astype(BF16), b.astype(BF16), (((1,), (1,)), ((), ())),
                           preferred_element_type=F32)


def _layer_norm(t, g, b):
    mu = jnp.mean(t, axis=-1, keepdims=True)
    d = t - mu
    var = jnp.mean(d * d, axis=-1, keepdims=True)
    return d * lax.rsqrt(var + LN_EPS) * g + b


def _matmul_kernel(a_ref, b_ref, o_ref):
    o_ref[...] = jnp.dot(a_ref[...], b_ref[...], preferred_element_type=F32).astype(o_ref.dtype)


def _matmul_cols(a, b, col0, ncols, out_dtype, tm=1024, tn=512):
    m, k = a.shape
    tm = min(tm, m)
    cb = col0 // tn
    assert col0 % tn == 0 and ncols % tn == 0 and m % tm == 0
    return pl.pallas_call(
        _matmul_kernel,
        grid=(m // tm, ncols // tn),
        in_specs=[pl.BlockSpec((tm, k), lambda i, j: (i, 0)),
                  pl.BlockSpec((k, tn), lambda i, j: (0, j + cb))],
        out_specs=pl.BlockSpec((tm, tn), lambda i, j: (i, j)),
        out_shape=jax.ShapeDtypeStruct((m, ncols), out_dtype),
        compiler_params=_cparams(("parallel", "arbitrary")),
        name="inproj_matmul",
    )(a, b)


def _rope(x, cosb, sinb):
    half = HEAD_DIM // 2
    lane = lax.broadcasted_iota(I32, cosb.shape, 1)
    first_half = (lane % HEAD_DIM) < half
    outs = []
    for g in range(x.shape[1] // LANES):
        xg = x[:, g * LANES:(g + 1) * LANES]
        partner = jnp.where(first_half, pltpu.roll(xg, LANES - half, axis=1), pltpu.roll(xg, half, axis=1))
        outs.append(xg * cosb + partner * sinb)
    return outs


def _attn_kernel(sinks_ref, q_ref, kc_ref, kp_ref, vc_ref, vp_ref, cosc_ref, sinc_ref, cosp_ref, sinp_ref, o_ref):
    blk = pl.program_id(1)
    tq = q_ref.shape[1]
    qg = _rope(q_ref[0], cosc_ref[...], sinc_ref[...])
    kcg = _rope(kc_ref[0], cosc_ref[...], sinc_ref[...])
    kpg = _rope(kp_ref[0], cosp_ref[...], sinp_ref[...])
    vc = vc_ref[0]
    vp = vp_ref[0]

    def head(groups, h):
        g = groups[h // 2]
        return g[:, (h % 2) * HEAD_DIM:(h % 2 + 1) * HEAD_DIM]

    rows = ATTN_GROUP * tq
    qi = lax.broadcasted_iota(I32, (rows, 2 * tq), 0) % tq
    kj = lax.broadcasted_iota(I32, (rows, 2 * tq), 1)
    dist = qi + tq - kj
    valid = (dist >= 0) & (dist < WINDOW) & ((blk > 0) | (kj >= tq))
    rid = lax.broadcasted_iota(I32, (rows, 1), 0) // tq
    scale = HEAD_DIM ** -0.5
    for kvh in range(ATTN_KV_HEADS):
        qh = jnp.concatenate([head(qg, kvh * ATTN_GROUP + g) for g in range(ATTN_GROUP)], axis=0)
        kw = jnp.concatenate([head(kpg, kvh), head(kcg, kvh)], axis=0)
        vw = jnp.concatenate([vp[:, kvh * HEAD_DIM:(kvh + 1) * HEAD_DIM],
                              vc[:, kvh * HEAD_DIM:(kvh + 1) * HEAD_DIM]], axis=0)
        s = _dot_nt(qh, kw) * scale
        s = jnp.where(valid, s, -jnp.inf)
        sink = jnp.zeros((rows, 1), F32)
        for g in range(ATTN_GROUP):
            sink = jnp.where(rid == g, sinks_ref[kvh * ATTN_GROUP + g], sink)
        m = jnp.maximum(jnp.max(s, axis=-1, keepdims=True), sink)
        e = jnp.exp(s - m)
        denom = jnp.sum(e, axis=-1, keepdims=True) + jnp.exp(sink - m)
        p = e / denom
        o = _dot(p, vw)
        for g in range(ATTN_GROUP):
            hq = kvh * ATTN_GROUP + g
            o_ref[0, :, hq * HEAD_DIM:(hq + 1) * HEAD_DIM] = o[g * tq:(g + 1) * tq]


def _attention(qkv, sinks, cosb, sinb):
    b, t, _ = qkv.shape
    tq = WINDOW
    nb = t // tq
    kcol = ATTN_Q_W // ATTN_KV_W
    prev = lambda i: jnp.maximum(i - 1, 0)
    grid_spec = pltpu.PrefetchScalarGridSpec(
        num_scalar_prefetch=0,
        grid=(b, nb),
        in_specs=[
            pl.BlockSpec(memory_space=pltpu.SMEM),
            pl.BlockSpec((1, tq, ATTN_Q_W), lambda bi, i: (bi, i, 0)),
            pl.BlockSpec((1, tq, ATTN_KV_W), lambda bi, i: (bi, i, kcol)),
            pl.BlockSpec((1, tq, ATTN_KV_W), lambda bi, i: (bi, prev(i), kcol)),
            pl.BlockSpec((1, tq, ATTN_KV_W), lambda bi, i: (bi, i, kcol + 1)),
            pl.BlockSpec((1, tq, ATTN_KV_W), lambda bi, i: (bi, prev(i), kcol + 1)),
            pl.BlockSpec((tq, LANES), lambda bi, i: (i, 0)),
            pl.BlockSpec((tq, LANES), lambda bi, i: (i, 0)),
            pl.BlockSpec((tq, LANES), lambda bi, i: (prev(i), 0)),
            pl.BlockSpec((tq, LANES), lambda bi, i: (prev(i), 0)),
        ],
        out_specs=pl.BlockSpec((1, tq, ATTN_Q_W), lambda bi, i: (bi, i, 0)),
    )
    return pl.pallas_call(
        _attn_kernel,
        grid_spec=grid_spec,
        out_shape=jax.ShapeDtypeStruct((b, t, ATTN_Q_W), F32),
        compiler_params=_cparams(("parallel", "arbitrary")),
        name="swa_attention",
    )(sinks, qkv, qkv, qkv, qkv, qkv, cosb, sinb, cosb, sinb)


def _rope_tables(t):
    inv = 1.0 / (ROPE_THETA ** (jnp.arange(0, HEAD_DIM, 2, dtype=F32) / HEAD_DIM))
    ang = jnp.arange(t, dtype=F32)[:, None] * inv[None, :]
    cos, sin = jnp.cos(ang), jnp.sin(ang)
    reps = LANES // HEAD_DIM
    cosb = jnp.tile(jnp.concatenate([cos, cos], axis=-1), (1, reps))
    sinb = jnp.tile(jnp.concatenate([-sin, sin], axis=-1), (1, reps))
    return cosb, sinb


def _rwkv_prep_kernel(h_ref, hp_ref, r_ref, k_ref, v_ref, rp_ref, kp_ref, vp_ref,
                      mu_rkv_ref, mu_wag_ref, w0_ref, w1_ref, w2_ref, a0_ref, a1_ref, a2_ref,
                      g1_ref, g2_ref, kk_ref, ka_ref,
                      ro_ref, ko_ref, vo_ref, lwo_ref, cumo_ref, kko_ref, ago_ref, go_ref):
    first = pl.program_id(1) == 0
    last_row = SUBLANES - 1

    def shifted(cur, prev_ref):
        prev_row = jnp.where(first, 0.0, prev_ref[0, last_row:last_row + 1, :])
        rowid = lax.broadcasted_iota(I32, cur.shape, 0)
        return jnp.where(rowid == 0, prev_row, pltpu.roll(cur, 1, axis=0))

    h = h_ref[0]
    xx = shifted(h, hp_ref) - h
    xw = h + xx * mu_wag_ref[0:1, :]
    xa = h + xx * mu_wag_ref[1:2, :]
    xg = h + xx * mu_wag_ref[2:3, :]
    w_raw = w0_ref[...] + _dot(jnp.tanh(_dot(xw, w1_ref[...])), w2_ref[...])
    neg = -w_raw
    softplus = jnp.maximum(neg, 0.0) + jnp.log1p(jnp.exp(-jnp.abs(neg)))
    w = -softplus - 0.5
    lw = -jnp.exp(w)
    tm = lw.shape[0]
    row = lax.broadcasted_iota(I32, (tm, tm), 0)
    col = lax.broadcasted_iota(I32, (tm, tm), 1)
    tri = ((row >= col) & (row // WKV_CHUNK == col // WKV_CHUNK)).astype(BF16)
    cum = sum(jnp.dot(tri, piece, preferred_element_type=F32) for piece in _split3(lw))
    ag = _sigmoid(a0_ref[...] + _dot(_dot(xa, a1_ref[...]), a2_ref[...]))
    go_ref[0] = _dot(_sigmoid(_dot(xg, g1_ref[...])), g2_ref[...])

    r = r_ref[0]
    k = k_ref[0]
    v = v_ref[0]
    r = r + (shifted(r, rp_ref) - r) * mu_rkv_ref[0:1, :]
    k = k + (shifted(k, kp_ref) - k) * mu_rkv_ref[1:2, :]
    v = v + (shifted(v, vp_ref) - v) * mu_rkv_ref[2:3, :]
    kk = k * kk_ref[...]
    k = k * (1.0 + (ag - 1.0) * ka_ref[...])
    for hd in range(RWKV_HEADS):
        sl = slice(hd * RWKV_N, (hd + 1) * RWKV_N)
        ro_ref[0, hd] = r[:, sl]
        ko_ref[0, hd] = k[:, sl]
        vo_ref[0, hd] = v[:, sl]
        lwo_ref[0, hd] = lw[:, sl]
        cumo_ref[0, hd] = cum[:, sl]
        kko_ref[0, hd] = kk[:, sl]
        ago_ref[0, hd] = ag[:, sl]


def _rwkv_prep(h, rkv, mu_rkv, mu_wag, w0, w1, w2, a0, a1, a2, g1, g2, k_k, k_a, tm=256):
    b, t, d = h.shape
    tm = min(tm, t)
    c = RWKV_W
    spt = tm // SUBLANES
    prevblk = lambda i: jnp.maximum(i * spt - 1, 0)
    full = lambda arr: pl.BlockSpec(arr.shape, lambda bi, i: (0,) * arr.ndim)
    row = lambda arr: arr.reshape(1, -1)
    w0, a0, k_k, k_a = row(w0), row(a0), row(k_k), row(k_a)
    in_specs = [
        pl.BlockSpec((1, tm, d), lambda bi, i: (bi, i, 0)),
        pl.BlockSpec((1, SUBLANES, d), lambda bi, i: (bi, prevblk(i), 0)),
        pl.BlockSpec((1, tm, c), lambda bi, i: (bi, i, 0)),
        pl.BlockSpec((1, tm, c), lambda bi, i: (bi, i, 1)),
        pl.BlockSpec((1, tm, c), lambda bi, i: (bi, i, 2)),
        pl.BlockSpec((1, SUBLANES, c), lambda bi, i: (bi, prevblk(i), 0)),
        pl.BlockSpec((1, SUBLANES, c), lambda bi, i: (bi, prevblk(i), 1)),
        pl.BlockSpec((1, SUBLANES, c), lambda bi, i: (bi, prevblk(i), 2)),
    ] + [full(a) for a in (mu_rkv, mu_wag, w0, w1, w2, a0, a1, a2, g1, g2, k_k, k_a)]
    hm = jax.ShapeDtypeStruct((b, RWKV_HEADS, t, RWKV_N), F32)
    hm_spec = pl.BlockSpec((1, RWKV_HEADS, tm, RWKV_N), lambda bi, i: (bi, 0, i, 0))
    return pl.pallas_call(
        _rwkv_prep_kernel,
        grid=(b, t // tm),
        in_specs=in_specs,
        out_specs=[hm_spec] * 7 + [pl.BlockSpec((1, tm, c), lambda bi, i: (bi, i, 0))],
        out_shape=[hm] * 7 + [jax.ShapeDtypeStruct((b, t, c), F32)],
        compiler_params=_cparams(("parallel", "arbitrary")),
        name="rwkv_prep",
    )(h, h, rkv, rkv, rkv, rkv, rkv, rkv, mu_rkv, mu_wag, w0, w1, w2, a0, a1, a2, g1, g2, k_k, k_a)


def _split3(x):
    hi = x.astype(BF16)
    r1 = x - hi.astype(F32)
    mid = r1.astype(BF16)
    lo = (r1 - mid.astype(F32)).astype(BF16)
    return hi, mid, lo


def _bmm(a, b):
    return jnp.einsum("gmk,gkn->gmn", a.astype(BF16), b.astype(BF16), preferred_element_type=F32)


def _bmm_nt(a, b):
    return jnp.einsum("gmk,gnk->gmn", a.astype(BF16), b.astype(BF16), preferred_element_type=F32)


def _wkv_kernel(r_ref, k_ref, v_ref, lw_ref, cum_ref, kk_ref, ag_ref, rk_ref, lnw_ref, lnb_ref, y_ref,
                s_ref, st_ref):
    c = WKV_CHUNK
    n = RWKV_N
    hb, tc = r_ref.shape[1], r_ref.shape[2]
    nc = tc // c
    g = hb * nc

    @pl.when(pl.program_id(2) == 0)
    def _():
        s_ref[...] = jnp.zeros_like(s_ref)

    chunks = lambda ref: ref[0].reshape(g, c, n)
    r, k, v, lw, cum, kk, ag = (chunks(ref) for ref in (r_ref, k_ref, v_ref, lw_ref, cum_ref, kk_ref, ag_ref))
    kk = kk / jnp.maximum(jnp.sqrt(jnp.sum(kk * kk, axis=-1, keepdims=True)), 1e-12)
    dinc = jnp.exp(cum)
    dinv = jnp.exp(-cum)
    rt = r * dinc
    kt = k * dinv
    at = -kk * jnp.exp(cum - lw)
    bt = kk * ag * dinv
    d_chunk = dinc[:, c - 1:c, :]

    row = lax.broadcasted_iota(I32, (1, c, c), 1)
    col = lax.broadcasted_iota(I32, (1, c, c), 2)
    strict = row > col
    incl = row >= col
    eye = jnp.broadcast_to((row == col).astype(F32), (g, c, c))
    p = _bmm_nt(jnp.concatenate([at, rt], axis=1), jnp.concatenate([bt, kt], axis=1))
    a_ab = jnp.where(strict, p[:, :c, :c], 0.0)
    a_ak = jnp.where(strict, p[:, :c, c:], 0.0)
    a_rb = jnp.where(incl, p[:, c:, :c], 0.0)
    a_rk = jnp.where(incl, p[:, c:, c:], 0.0)
    x = a_ab
    tinv = eye + x
    for _ in range(c.bit_length() - 2):
        x = _bmm(x, x)
        tinv = tinv + _bmm(tinv, x)
    z = _bmm(a_ak, v)
    ta = _bmm(tinv, jnp.concatenate([at, z], axis=2))
    ry = _bmm(a_rb, ta)
    rp = rt + ry[:, :, :n]
    yv = ry[:, :, n:] + _bmm(a_rk, v)
    vt = _bmm_nt(eye, v)
    att = _bmm_nt(eye, at)
    tt = _bmm_nt(jnp.concatenate([att, _bmm_nt(vt, a_ak)], axis=1), tinv)
    mn = _bmm(tt, bt)
    moff = mn[:, :n].reshape(hb, nc, n, n)
    n2 = (mn[:, n:] + _bmm(vt, kt)).reshape(hb, nc, n, n)
    dch = d_chunk.reshape(hb, nc, 1, n)

    s = s_ref[...]
    for ci in range(nc):
        st_ref[:, ci] = s
        s = (s + _bmm(s, moff[:, ci]) + n2[:, ci]) * dch[:, ci]
    s_ref[...] = s

    y = _bmm_nt(rp, st_ref[...].reshape(g, n, n)) + yv
    mu = jnp.mean(y, axis=-1, keepdims=True)
    yc = y - mu
    var = jnp.mean(yc * yc, axis=-1, keepdims=True)
    y = (yc * lax.rsqrt(var + RWKV_GN_EPS)).reshape(hb, tc, n) * lnw_ref[...] + lnb_ref[...]
    bonus = jnp.sum((r * k).reshape(hb, tc, n) * rk_ref[...], axis=-1, keepdims=True)
    y_ref[0] = y + bonus * v.reshape(hb, tc, n)


def _wkv_scan(r, k, v, lw, cum, kk, ag, r_k, lnx_w, lnx_b):
    b, hh, t, n = r.shape
    tc = min(WKV_CHUNK * WKV_CHUNKS_PER_STEP, t)
    hb = WKV_HEADS_PER_STEP
    blk = pl.BlockSpec((1, hb, tc, n), lambda bi, hi, i: (bi, hi, i, 0))
    par = pl.BlockSpec((hb, 1, n), lambda bi, hi, i: (hi, 0, 0))
    per_head = lambda arr: arr.reshape(hh, 1, n)
    return pl.pallas_call(
        _wkv_kernel,
        grid=(b, hh // hb, t // tc),
        in_specs=[blk] * 7 + [par] * 3,
        out_specs=blk,
        out_shape=jax.ShapeDtypeStruct((b, hh, t, n), F32),
        scratch_shapes=[pltpu.VMEM((hb, n, n), F32), pltpu.VMEM((hb, tc // WKV_CHUNK, n, n), F32)],
        compiler_params=_cparams(("parallel", "parallel", "arbitrary")),
        name="wkv7_scan",
    )(r, k, v, lw, cum, kk, ag, per_head(r_k), per_head(lnx_w), per_head(lnx_b))


def _merge_kernel(ya_ref, yr_ref, g_ref, ga_ref, gr_ref, pa_ref, pr_ref, o_ref):
    ma = _dot(ya_ref[...], pa_ref[...])
    mr = _dot(yr_ref[...] * g_ref[...], pr_ref[...])
    o_ref[...] = (_sigmoid(ga_ref[...]) * ma + _sigmoid(gr_ref[...]) * mr).astype(o_ref.dtype)


def _merge(ya, yr, g, gates, p_attn, p_rwkv, tm=256):
    m, c = ya.shape
    d = p_attn.shape[1]
    tm = min(tm, m)
    tile = pl.BlockSpec((tm, c), lambda i: (i, 0))
    return pl.pallas_call(
        _merge_kernel,
        grid=(m // tm,),
        in_specs=[tile, tile, tile,
                  pl.BlockSpec((tm, d), lambda i: (i, 0)),
                  pl.BlockSpec((tm, d), lambda i: (i, 1)),
                  pl.BlockSpec((c, d), lambda i: (0, 0)),
                  pl.BlockSpec((c, d), lambda i: (0, 0))],
        out_specs=pl.BlockSpec((tm, d), lambda i: (i, 0)),
        out_shape=jax.ShapeDtypeStruct((m, d), BF16),
        compiler_params=_cparams(("parallel",)),
        name="gated_merge",
    )(ya, yr, g, gates, gates, p_attn, p_rwkv)


def _outproj_router_kernel(alpha, mg_ref, x_ref, wo_ref, g_ref, b_ref, wr_ref, br_ref,
                           h_ref, hrow_ref, eid_ref, gate_ref):
    mix = jnp.dot(mg_ref[...], wo_ref[...], preferred_element_type=F32)
    h = _layer_norm(alpha * x_ref[...] + mix, g_ref[...], b_ref[...])
    h_ref[...] = h
    slab = h.shape[1] // LANES
    for j in range(slab):
        hrow_ref[pl.ds(j, h.shape[0], stride=slab), :] = h[:, j * LANES:(j + 1) * LANES]
    logits = jnp.dot(h, wr_ref[...], preferred_element_type=F32, precision=lax.Precision.HIGHEST) + br_ref[...]
    lane = lax.broadcasted_iota(I32, logits.shape, 1)
    ninf = -jnp.inf
    big = jnp.int32(2 * LANES)
    glog = jnp.where(lane < N_GROUPS, logits, ninf)
    gmax = jnp.max(glog, axis=-1, keepdims=True)
    gidx = jnp.min(jnp.where(glog == gmax, lane, big), axis=-1, keepdims=True)
    gtop = 1.0 / jnp.sum(jnp.exp(glog - gmax), axis=-1, keepdims=True)
    eg = (lane - N_GROUPS) // EXPERTS_PER_GROUP
    in_group = (lane >= N_GROUPS) & (lane < N_GROUPS + N_EXPERTS) & (eg == gidx)
    el = jnp.where(in_group, logits, ninf)
    m1 = jnp.max(el, axis=-1, keepdims=True)
    i1 = jnp.min(jnp.where(el == m1, lane, big), axis=-1, keepdims=True)
    el2 = jnp.where(lane == i1, ninf, el)
    m2 = jnp.max(el2, axis=-1, keepdims=True)
    i2 = jnp.min(jnp.where(el2 == m2, lane, big), axis=-1, keepdims=True)
    t = jnp.exp(m2 - m1)
    p1 = 1.0 / (1.0 + t)
    p2 = t / (1.0 + t)
    eid_ref[...] = jnp.where(lane == 0, i1 - N_GROUPS, jnp.where(lane == 1, i2 - N_GROUPS, 0))
    gate_ref[...] = jnp.where(lane == 0, gtop * p1, jnp.where(lane == 1, gtop * p2, 0.0))


def _outproj_router(merged, x, w_o, ln_g, ln_b, w_router, b_router, alpha, tm=256):
    m, d = x.shape
    tm = min(tm, m)
    tile = pl.BlockSpec((tm, d), lambda i: (i, 0))
    vec = pl.BlockSpec((1, d), lambda i: (0, 0))
    small = pl.BlockSpec((tm, LANES), lambda i: (i, 0))
    return pl.pallas_call(
        functools.partial(_outproj_router_kernel, alpha),
        grid=(m // tm,),
        in_specs=[tile, tile, pl.BlockSpec((d, d), lambda i: (0, 0)), vec, vec,
                  pl.BlockSpec((d, LANES), lambda i: (0, 0)), pl.BlockSpec((1, LANES), lambda i: (0, 0))],
        out_specs=[tile, pl.BlockSpec((tm * (d // LANES), LANES), lambda i: (i, 0)), small, small],
        out_shape=[jax.ShapeDtypeStruct((m, d), F32), jax.ShapeDtypeStruct((m * (d // LANES), LANES), F32),
                   jax.ShapeDtypeStruct((m, LANES), I32), jax.ShapeDtypeStruct((m, LANES), F32)],
        compiler_params=_cparams(("parallel",)),
        name="outproj_ln_router",
    )(merged, x, w_o, ln_g.reshape(1, d), ln_b.reshape(1, d), w_router, b_router)


def _onehots(eid_ref):
    tm = eid_ref.shape[1]
    e_iota = lax.broadcasted_iota(I32, (N_EXPERTS, tm), 0)
    oh0 = (eid_ref[0:1, :] == e_iota).astype(F32)
    oh1 = (eid_ref[1:2, :] == e_iota).astype(F32)
    return oh0, oh1


def _count_kernel(eid_ref, cnt_ref):
    @pl.when(pl.program_id(0) == 0)
    def _():
        cnt_ref[...] = jnp.zeros_like(cnt_ref)

    oh0, oh1 = _onehots(eid_ref)
    cnt_ref[...] += jnp.sum(oh0 + oh1, axis=1, keepdims=True)


def _slot_kernel(eid_ref, pstart_ref, dest_ref, run_ref):
    @pl.when(pl.program_id(0) == 0)
    def _():
        run_ref[...] = jnp.zeros_like(run_ref)

    tm = eid_ref.shape[1]
    oh0, oh1 = _onehots(eid_ref)
    both = oh0 + oh1
    earlier = (lax.broadcasted_iota(I32, (tm, tm), 0) < lax.broadcasted_iota(I32, (tm, tm), 1)).astype(BF16)
    pre = jnp.dot(both.astype(BF16), earlier, preferred_element_type=F32)
    base = pre + run_ref[...] + pstart_ref[...]
    dest_ref[0:1, :] = jnp.sum(oh0 * base, axis=0, keepdims=True).astype(I32)
    dest_ref[1:2, :] = jnp.sum(oh1 * base, axis=0, keepdims=True).astype(I32)
    run_ref[...] += jnp.sum(both, axis=1, keepdims=True)


def _expert_counts(eid_t):
    m = eid_t.shape[1]
    tm = min(ROUTE_TILE, m)
    return pl.pallas_call(
        _count_kernel,
        grid=(m // tm,),
        in_specs=[pl.BlockSpec((TOP_K, tm), lambda i: (0, i))],
        out_specs=pl.BlockSpec((N_EXPERTS, 1), lambda i: (0, 0)),
        out_shape=jax.ShapeDtypeStruct((N_EXPERTS, 1), F32),
        compiler_params=_cparams(("arbitrary",)),
        name="expert_counts",
    )(eid_t)


def _expert_slots(eid_t, pstart):
    m = eid_t.shape[1]
    tm = min(ROUTE_TILE, m)
    return pl.pallas_call(
        _slot_kernel,
        grid=(m // tm,),
        in_specs=[pl.BlockSpec((TOP_K, tm), lambda i: (0, i)),
                  pl.BlockSpec((N_EXPERTS, 1), lambda i: (0, 0))],
        out_specs=pl.BlockSpec((TOP_K, tm), lambda i: (0, i)),
        out_shape=jax.ShapeDtypeStruct((TOP_K, m), I32),
        scratch_shapes=[pltpu.VMEM((N_EXPERTS, 1), F32)],
        compiler_params=_cparams(("arbitrary",)),
        name="expert_slots",
    )(eid_t, pstart)


def _row_slab(ref, row, slab):
    return ref.at[pl.ds(pl.multiple_of(row * slab, slab), slab), :]


def _dispatch_kernel(dest_ref, h_ref, xb_in_ref, xb_ref, sem):
    del xb_in_ref
    tm = DISPATCH_TILE
    slab = h_ref.shape[0] // tm
    t0 = pl.program_id(0) * tm

    def copy(n, k):
        return pltpu.make_async_copy(_row_slab(h_ref, n, slab),
                                     _row_slab(xb_ref, dest_ref[TOP_K * (t0 + n) + k], slab), sem)

    def start(n, carry):
        for k in range(TOP_K):
            copy(n, k).start()
        return carry

    def wait(n, carry):
        for k in range(TOP_K):
            copy(n, k).wait()
        return carry

    lax.fori_loop(0, tm, start, 0)
    lax.fori_loop(0, tm, wait, 0)


DISPATCH_TILE = 512


def _dispatch(dest_flat, h_rows, m, n_rows):
    slab = h_rows.shape[0] // m
    assert m % DISPATCH_TILE == 0
    grid_spec = pltpu.PrefetchScalarGridSpec(
        num_scalar_prefetch=1,
        grid=(m // DISPATCH_TILE,),
        in_specs=[pl.BlockSpec((DISPATCH_TILE * slab, LANES), lambda i, dr: (i, 0)),
                  pl.BlockSpec(memory_space=pl.ANY)],
        out_specs=pl.BlockSpec(memory_space=pl.ANY),
        scratch_shapes=[pltpu.SemaphoreType.DMA(())],
    )
    return pl.pallas_call(
        _dispatch_kernel,
        grid_spec=grid_spec,
        out_shape=jax.ShapeDtypeStruct((n_rows * slab, LANES), F32),
        input_output_aliases={2: 0},
        compiler_params=pltpu.CompilerParams(dimension_semantics=("arbitrary",), has_side_effects=True),
        name="moe_dispatch",
    )(dest_flat, h_rows, jnp.zeros((n_rows * slab, LANES), F32))


def _expert_kernel(item_e_ref, item_row_ref, item_nsub_ref, wg_ref, wu_ref, wd_ref, xb_ref, yb_ref,
                   stage_ref, x_ref, act_ref, wgu_ref, wdb_ref, sem_in, sem_out):
    del item_e_ref
    it = pl.program_id(0)
    f = pl.program_id(1)
    nf = pl.num_programs(1)
    nsub = item_nsub_ref[it]
    row0 = item_row_ref[it]
    rb = MOE_ROWS
    ft = MOE_FF_TILE
    d = x_ref.shape[1]
    slab = d // LANES
    blk = rb * slab

    def stage_rows(s):
        return stage_ref.at[pl.ds(pl.multiple_of(s * blk, blk), blk), :]

    def hbm_rows(ref, s):
        return ref.at[pl.ds(pl.multiple_of((row0 + s * rb) * slab, blk), blk), :]

    def each(fn):
        def body(s, carry):
            fn(s)
            return carry
        lax.fori_loop(0, nsub, body, 0)

    @pl.when(nsub > 0)
    def _():
        @pl.when(f == 0)
        def _():
            each(lambda s: pltpu.make_async_copy(hbm_rows(xb_ref, s), stage_rows(s), sem_in).start())
            each(lambda s: pltpu.make_async_copy(hbm_rows(xb_ref, s), stage_rows(s), sem_in).wait())

            def to_matrix(s):
                rows = pl.ds(pl.multiple_of(s * rb, rb), rb)
                xs = stage_ref[pl.ds(pl.multiple_of(s * blk, blk), blk), :].reshape(rb, slab, LANES)
                xt = pltpu.einshape("rjl->jrl", xs)
                for j in range(slab):
                    x_ref[rows, j * LANES:(j + 1) * LANES] = xt[j].astype(BF16)
            each(to_matrix)

        wgu_ref[:, :ft] = wg_ref[0].astype(BF16)
        wgu_ref[:, ft:] = wu_ref[0].astype(BF16)
        wdb_ref[pl.ds(pl.multiple_of(f * ft, ft), ft), :] = wd_ref[0].astype(BF16)

        def gate_up(s):
            rows = pl.ds(pl.multiple_of(s * rb, rb), rb)
            gu = jnp.dot(x_ref[rows, :], wgu_ref[...], preferred_element_type=F32)
            gate = gu[:, :ft]
            act_ref[f, rows, :] = (gate * _sigmoid(gate) * gu[:, ft:]).astype(BF16)
        each(gate_up)

        @pl.when(f == nf - 1)
        def _():
            def down(s):
                rows = pl.ds(pl.multiple_of(s * rb, rb), rb)
                act = jnp.concatenate([act_ref[j, rows, :] for j in range(act_ref.shape[0])], axis=1)
                y = jnp.dot(act, wdb_ref[...], preferred_element_type=F32)
                base = pl.multiple_of(s * blk, blk)
                for j in range(slab):
                    stage_ref[pl.ds(base + j, rb, stride=slab), :] = y[:, j * LANES:(j + 1) * LANES]
            each(down)
            each(lambda s: pltpu.make_async_copy(stage_rows(s), hbm_rows(yb_ref, s), sem_out).start())
            each(lambda s: pltpu.make_async_copy(stage_rows(s), hbm_rows(yb_ref, s), sem_out).wait())


def _experts(item_e, item_row, item_nsub, w_gate, w_up, w_down, xb):
    d = w_gate.shape[1]
    ff = w_gate.shape[2]
    nf = ff // MOE_FF_TILE
    n_items = item_e.shape[0]
    rows_max = MOE_ROWS * MOE_SUBS_PER_ITEM
    slab = d // LANES

    def f_eff(f, ns, it):
        return jnp.where(ns[it] > 0, f, nf - 1)

    grid_spec = pltpu.PrefetchScalarGridSpec(
        num_scalar_prefetch=3,
        grid=(n_items, nf),
        in_specs=[
            pl.BlockSpec((1, d, MOE_FF_TILE), lambda it, f, ie, ir, ns: (ie[it], 0, f_eff(f, ns, it))),
            pl.BlockSpec((1, d, MOE_FF_TILE), lambda it, f, ie, ir, ns: (ie[it], 0, f_eff(f, ns, it))),
            pl.BlockSpec((1, MOE_FF_TILE, d), lambda it, f, ie, ir, ns: (ie[it], f_eff(f, ns, it), 0)),
            pl.BlockSpec(memory_space=pl.ANY),
        ],
        out_specs=pl.BlockSpec(memory_space=pl.ANY),
        scratch_shapes=[
            pltpu.VMEM((rows_max * slab, LANES), F32),
            pltpu.VMEM((rows_max, d), BF16),
            pltpu.VMEM((nf, rows_max, MOE_FF_TILE), BF16),
            pltpu.VMEM((d, 2 * MOE_FF_TILE), BF16),
            pltpu.VMEM((ff, d), BF16),
            pltpu.SemaphoreType.DMA(()),
            pltpu.SemaphoreType.DMA(()),
        ],
    )
    return pl.pallas_call(
        _expert_kernel,
        grid_spec=grid_spec,
        out_shape=jax.ShapeDtypeStruct(xb.shape, F32),
        input_output_aliases={6: 0},
        compiler_params=pltpu.CompilerParams(dimension_semantics=("arbitrary", "arbitrary"),
                                             vmem_limit_bytes=VMEM_LIMIT_BYTES, has_side_effects=True),
        name="moe_experts",
    )(item_e, item_row, item_nsub, w_gate, w_up, w_down, xb)


COMBINE_TILE = 256


def _combine_kernel(alpha, dest_ref, yb_ref, h_ref, gate_ref, g_ref, b_ref, o_ref, buf_ref, sem):
    tm = COMBINE_TILE
    t0 = pl.program_id(0) * tm

    slab = buf_ref.shape[1] // tm

    def copy(n, k):
        return pltpu.make_async_copy(_row_slab(yb_ref, dest_ref[TOP_K * (t0 + n) + k], slab),
                                     _row_slab(buf_ref.at[k], n, slab), sem)

    def start(n, carry):
        for k in range(TOP_K):
            copy(n, k).start()
        return carry

    def wait(n, carry):
        for k in range(TOP_K):
            copy(n, k).wait()
        return carry

    lax.fori_loop(0, tm, start, 0)
    lax.fori_loop(0, tm, wait, 0)
    gate = gate_ref[...]
    g0 = gate[:, 0:1]
    g1 = gate[:, 1:2]
    col = lambda k, j: buf_ref[k, pl.ds(j, tm, stride=slab), :]
    ff = jnp.concatenate([g0 * col(0, j) + g1 * col(1, j) for j in range(slab)], axis=1)
    o_ref[...] = _layer_norm(alpha * h_ref[...] + ff, g_ref[...], b_ref[...])


def _combine(dest_flat, yb, h, gates, ln_g, ln_b, alpha):
    m, d = h.shape
    tm = COMBINE_TILE
    assert m % tm == 0
    grid_spec = pltpu.PrefetchScalarGridSpec(
        num_scalar_prefetch=1,
        grid=(m // tm,),
        in_specs=[pl.BlockSpec(memory_space=pl.ANY),
                  pl.BlockSpec((tm, d), lambda i, dr: (i, 0)),
                  pl.BlockSpec((tm, LANES), lambda i, dr: (i, 0)),
                  pl.BlockSpec((1, d), lambda i, dr: (0, 0)),
                  pl.BlockSpec((1, d), lambda i, dr: (0, 0))],
        out_specs=pl.BlockSpec((tm, d), lambda i, dr: (i, 0)),
        scratch_shapes=[pltpu.VMEM((TOP_K, tm * (d // LANES), LANES), F32), pltpu.SemaphoreType.DMA(())],
    )
    return pl.pallas_call(
        functools.partial(_combine_kernel, alpha),
        grid_spec=grid_spec,
        out_shape=jax.ShapeDtypeStruct((m, d), F32),
        compiler_params=_cparams(("arbitrary",)),
        name="moe_combine_ln",
    )(dest_flat, yb, h, gates, ln_g.reshape(1, d), ln_b.reshape(1, d))


def _moe_tables(counts):
    nsub_e = (counts + MOE_ROWS - 1) // MOE_ROWS
    pstart = (jnp.cumsum(nsub_e) - nsub_e) * MOE_ROWS
    nitem_e = (nsub_e + MOE_SUBS_PER_ITEM - 1) // MOE_SUBS_PER_ITEM
    item_end = jnp.cumsum(nitem_e)
    return nsub_e, pstart, nitem_e, item_end


def _moe(h, h_rows, eid, gates, w_gate, w_up, w_down, ln_g, ln_b, alpha):
    m, d = h.shape
    n_assign = m * TOP_K
    n_blocks = (n_assign + N_EXPERTS * (MOE_ROWS - 1) + MOE_ROWS - 1) // MOE_ROWS
    n_rows = n_blocks * MOE_ROWS
    n_items = N_EXPERTS + n_assign // (MOE_ROWS * MOE_SUBS_PER_ITEM)

    eid_t = eid[:, :TOP_K].T
    counts = _expert_counts(eid_t)[:, 0].astype(I32)
    nsub_e, pstart, nitem_e, item_end = _moe_tables(counts)
    dest_t = _expert_slots(eid_t, pstart.astype(F32).reshape(N_EXPERTS, 1))
    dest_flat = dest_t.T.reshape(-1)

    it = jnp.arange(n_items, dtype=I32)
    total_items = item_end[-1]
    item_e = jnp.minimum(jnp.sum(item_end[None, :] <= it[:, None], axis=1).astype(I32), N_EXPERTS - 1)
    j = it - (item_end - nitem_e)[item_e]
    used = it < total_items
    last_e = item_e[jnp.maximum(total_items - 1, 0)]
    item_nsub = jnp.where(used, jnp.clip(nsub_e[item_e] - j * MOE_SUBS_PER_ITEM, 0, MOE_SUBS_PER_ITEM), 0).astype(I32)
    item_row = jnp.where(used, pstart[item_e] + j * (MOE_ROWS * MOE_SUBS_PER_ITEM), 0).astype(I32)
    item_e = jnp.where(used, item_e, last_e).astype(I32)

    xb = _dispatch(dest_flat, h_rows, m, n_rows)
    yb = _experts(item_e, item_row, item_nsub, w_gate, w_up, w_down, xb)
    return _combine(dest_flat, yb, h, gates, ln_g, ln_b, alpha)


def _pad_cols(w, n):
    return jnp.pad(w, ((0, 0), (0, n - w.shape[1])))


def _pad_rows(w, n):
    return jnp.pad(w, ((0, n - w.shape[0]), (0, 0)))


def kernel(x, w_in, attn_sinks, rw_mu_rkv, rw_mu_wag, rw_w0, rw_w1, rw_w2, rw_a0, rw_a1, rw_a2, rw_g1, rw_g2, rw_k_k, rw_k_a, rw_r_k, rw_lnx_w, rw_lnx_b, p_attn, p_rwkv, w_o, ln1_g, ln1_b, w_group, b_group, w_expert, b_expert, w_gate, w_up, w_down, ln2_g, ln2_b):
    b, t, d = x.shape
    depth = w_in.shape[0]
    m = b * t
    alpha = (2.0 * depth) ** 0.25
    cosb, sinb = _rope_tables(t)
    qkv_w = ATTN_Q_W + 2 * ATTN_KV_W
    rkv_w = 3 * RWKV_W
    h = x
    for l in range(depth):
        hf = h.reshape(m, d)
        hb = hf.astype(BF16)
        w_in_b = w_in[l].astype(BF16)
        qkv = _matmul_cols(hb, w_in_b, 0, qkv_w, F32)
        rkv = _matmul_cols(hb, w_in_b, qkv_w, rkv_w, F32)
        gates = _matmul_cols(hb, w_in_b, qkv_w + rkv_w, 2 * d, F32)

        y_a = _attention(qkv.reshape(b, t, qkv_w), attn_sinks[l], cosb, sinb)

        lora = lambda w, n: _pad_cols(w, n).astype(BF16)
        lorb = lambda w, n: _pad_rows(w, n).astype(BF16)
        n_w = -(-rw_w1.shape[2] // LANES) * LANES
        n_a = -(-rw_a1.shape[2] // LANES) * LANES
        n_g = -(-rw_g1.shape[2] // LANES) * LANES
        r_, k_, v_, lw_, cum_, kk_, ag_, g_ = _rwkv_prep(
            h, rkv.reshape(b, t, rkv_w), rw_mu_rkv[l], rw_mu_wag[l], rw_w0[l],
            lora(rw_w1[l], n_w), lorb(rw_w2[l], n_w), rw_a0[l], lora(rw_a1[l], n_a), lorb(rw_a2[l], n_a),
            lora(rw_g1[l], n_g), lorb(rw_g2[l], n_g), rw_k_k[l], rw_k_a[l])
        y_r = _wkv_scan(r_, k_, v_, lw_, cum_, kk_, ag_, rw_r_k[l], rw_lnx_w[l], rw_lnx_b[l])
        y_r = jnp.swapaxes(y_r, 1, 2).reshape(m, RWKV_W)

        merged = _merge(y_a.reshape(m, ATTN_Q_W), y_r, g_.reshape(m, RWKV_W), gates,
                        p_attn[l].astype(BF16), p_rwkv[l].astype(BF16))
        w_router = _pad_cols(jnp.concatenate([w_group[l], w_expert[l]], axis=1), LANES)
        b_router = _pad_cols(jnp.concatenate([b_group[l], b_expert[l]])[None, :], LANES)
        h1, h1_rows, eid, gate = _outproj_router(merged, hf, w_o[l].astype(BF16), ln1_g[l], ln1_b[l],
                                        w_router, b_router, alpha)
        h2 = _moe(h1, h1_rows, eid, gate, w_gate[l], w_up[l], w_down[l], ln2_g[l], ln2_b[l], alpha)
        h = h2.reshape(b, t, d)
    return h
```

```python
import functools

import jax
import jax.numpy as jnp
from jax import lax
from jax.experimental import pallas as pl
from jax.experimental.pallas import tpu as pltpu

F32 = jnp.float32
BF16 = jnp.bfloat16
I32 = jnp.int32

HEAD_DIM = 64
ATTN_Q_HEADS = 16
ATTN_KV_HEADS = 4
ATTN_GROUP = ATTN_Q_HEADS // ATTN_KV_HEADS
ATTN_Q_W = ATTN_Q_HEADS * HEAD_DIM
ATTN_KV_W = ATTN_KV_HEADS * HEAD_DIM
WINDOW = 128
ROPE_THETA = 10000.0
RWKV_HEADS = 16
RWKV_N = 64
RWKV_W = RWKV_HEADS * RWKV_N
RWKV_GN_EPS = 64e-5
N_GROUPS = 8
EXPERTS_PER_GROUP = 8
N_EXPERTS = N_GROUPS * EXPERTS_PER_GROUP
TOP_K = 2
LN_EPS = 1e-5

LANES = 128
SUBLANES = 8
VMEM_LIMIT_BYTES = 56 * 1024 * 1024

WKV_CHUNK = 64
WKV_CHUNKS_PER_STEP = 8
WKV_HEADS_PER_STEP = 4
MOE_ROWS = 128
MOE_SUBS_PER_ITEM = 8
MOE_FF_TILE = 128
ROUTE_TILE = 512


def _cparams(sem, vmem=VMEM_LIMIT_BYTES):
    return pltpu.CompilerParams(dimension_semantics=sem, vmem_limit_bytes=vmem)


def _sigmoid(x):
    return 1.0 / (1.0 + jnp.exp(-x))


def _dot(a, b):
    return jnp.dot(a.astype(BF16), b.astype(BF16), preferred_element_type=F32)


def _dot_nt(a, b):
    return lax.dot_general(a.astype(BF16), b.astype(BF16), (((1,), (1,)), ((), ())),
                           preferred_element_type=F32)


def _layer_norm(t, g, b):
    mu = jnp.mean(t, axis=-1, keepdims=True)
    d = t - mu
    var = jnp.mean(d * d, axis=-1, keepdims=True)
    return d * lax.rsqrt(var + LN_EPS) * g + b


def _matmul_kernel(a_ref, b_ref, o_ref):
    o_ref[...] = jnp.dot(a_ref[...], b_ref[...], preferred_element_type=F32).astype(o_ref.dtype)


def _matmul_cols(a, b, col0, ncols, out_dtype, tm=1024, tn=512):
    m, k = a.shape
    tm = min(tm, m)
    cb = col0 // tn
    assert col0 % tn == 0 and ncols % tn == 0 and m % tm == 0
    return pl.pallas_call(
        _matmul_kernel,
        grid=(m // tm, ncols // tn),
        in_specs=[pl.BlockSpec((tm, k), lambda i, j: (i, 0)),
                  pl.BlockSpec((k, tn), lambda i, j: (0, j + cb))],
        out_specs=pl.BlockSpec((tm, tn), lambda i, j: (i, j)),
        out_shape=jax.ShapeDtypeStruct((m, ncols), out_dtype),
        compiler_params=_cparams(("parallel", "arbitrary")),
        name="inproj_matmul",
    )(a, b)


def _rope(x, cosb, sinb):
    half = HEAD_DIM // 2
    lane = lax.broadcasted_iota(I32, cosb.shape, 1)
    first_half = (lane % HEAD_DIM) < half
    outs = []
    for g in range(x.shape[1] // LANES):
        xg = x[:, g * LANES:(g + 1) * LANES]
        partner = jnp.where(first_half, pltpu.roll(xg, LANES - half, axis=1), pltpu.roll(xg, half, axis=1))
        outs.append(xg * cosb + partner * sinb)
    return outs


def _attn_kernel(sinks_ref, q_ref, kc_ref, kp_ref, vc_ref, vp_ref, cosc_ref, sinc_ref, cosp_ref, sinp_ref, o_ref):
    blk = pl.program_id(1)
    tq = q_ref.shape[1]
    qg = _rope(q_ref[0], cosc_ref[...], sinc_ref[...])
    kcg = _rope(kc_ref[0], cosc_ref[...], sinc_ref[...])
    kpg = _rope(kp_ref[0], cosp_ref[...], sinp_ref[...])
    vc = vc_ref[0]
    vp = vp_ref[0]

    def head(groups, h):
        g = groups[h // 2]
        return g[:, (h % 2) * HEAD_DIM:(h % 2 + 1) * HEAD_DIM]

    rows = ATTN_GROUP * tq
    qi = lax.broadcasted_iota(I32, (rows, 2 * tq), 0) % tq
    kj = lax.broadcasted_iota(I32, (rows, 2 * tq), 1)
    dist = qi + tq - kj
    valid = (dist >= 0) & (dist < WINDOW) & ((blk > 0) | (kj >= tq))
    rid = lax.broadcasted_iota(I32, (rows, 1), 0) // tq
    scale = HEAD_DIM ** -0.5
    for kvh in range(ATTN_KV_HEADS):
        qh = jnp.concatenate([head(qg, kvh * ATTN_GROUP + g) for g in range(ATTN_GROUP)], axis=0)
        kw = jnp.concatenate([head(kpg, kvh), head(kcg, kvh)], axis=0)
        vw = jnp.concatenate([vp[:, kvh * HEAD_DIM:(kvh + 1) * HEAD_DIM],
                              vc[:, kvh * HEAD_DIM:(kvh + 1) * HEAD_DIM]], axis=0)
        s = _dot_nt(qh, kw) * scale
        s = jnp.where(valid, s, -jnp.inf)
        sink = jnp.zeros((rows, 1), F32)
        for g in range(ATTN_GROUP):
            sink = jnp.where(rid == g, sinks_ref[kvh * ATTN_GROUP + g], sink)
        m = jnp.maximum(jnp.max(s, axis=-1, keepdims=True), sink)
        e = jnp.exp(s - m)
        denom = jnp.sum(e, axis=-1, keepdims=True) + jnp.exp(sink - m)
        p = e / denom
        o = _dot(p, vw)
        for g in range(ATTN_GROUP):
            hq = kvh * ATTN_GROUP + g
            o_ref[0, :, hq * HEAD_DIM:(hq + 1) * HEAD_DIM] = o[g * tq:(g + 1) * tq]


def _attention(qkv, sinks, cosb, sinb):
    b, t, _ = qkv.shape
    tq = WINDOW
    nb = t // tq
    kcol = ATTN_Q_W // ATTN_KV_W
    prev = lambda i: jnp.maximum(i - 1, 0)
    grid_spec = pltpu.PrefetchScalarGridSpec(
        num_scalar_prefetch=0,
        grid=(b, nb),
        in_specs=[
            pl.BlockSpec(memory_space=pltpu.SMEM),
            pl.BlockSpec((1, tq, ATTN_Q_W), lambda bi, i: (bi, i, 0)),
            pl.BlockSpec((1, tq, ATTN_KV_W), lambda bi, i: (bi, i, kcol)),
            pl.BlockSpec((1, tq, ATTN_KV_W), lambda bi, i: (bi, prev(i), kcol)),
            pl.BlockSpec((1, tq, ATTN_KV_W), lambda bi, i: (bi, i, kcol + 1)),
            pl.BlockSpec((1, tq, ATTN_KV_W), lambda bi, i: (bi, prev(i), kcol + 1)),
            pl.BlockSpec((tq, LANES), lambda bi, i: (i, 0)),
            pl.BlockSpec((tq, LANES), lambda bi, i: (i, 0)),
            pl.BlockSpec((tq, LANES), lambda bi, i: (prev(i), 0)),
            pl.BlockSpec((tq, LANES), lambda bi, i: (prev(i), 0)),
        ],
        out_specs=pl.BlockSpec((1, tq, ATTN_Q_W), lambda bi, i: (bi, i, 0)),
    )
    return pl.pallas_call(
        _attn_kernel,
        grid_spec=grid_spec,
        out_shape=jax.ShapeDtypeStruct((b, t, ATTN_Q_W), F32),
        compiler_params=_cparams(("parallel", "arbitrary")),
        name="swa_attention",
    )(sinks, qkv, qkv, qkv, qkv, qkv, cosb, sinb, cosb, sinb)


def _rope_tables(t):
    inv = 1.0 / (ROPE_THETA ** (jnp.arange(0, HEAD_DIM, 2, dtype=F32) / HEAD_DIM))
    ang = jnp.arange(t, dtype=F32)[:, None] * inv[None, :]
    cos, sin = jnp.cos(ang), jnp.sin(ang)
    reps = LANES // HEAD_DIM
    cosb = jnp.tile(jnp.concatenate([cos, cos], axis=-1), (1, reps))
    sinb = jnp.tile(jnp.concatenate([-sin, sin], axis=-1), (1, reps))
    return cosb, sinb


def _rwkv_prep_kernel(h_ref, hp_ref, r_ref, k_ref, v_ref, rp_ref, kp_ref, vp_ref,
                      mu_rkv_ref, mu_wag_ref, w0_ref, w1_ref, w2_ref, a0_ref, a1_ref, a2_ref,
                      g1_ref, g2_ref, kk_ref, ka_ref,
                      ro_ref, ko_ref, vo_ref, lwo_ref, cumo_ref, kko_ref, ago_ref, go_ref):
    first = pl.program_id(1) == 0
    last_row = SUBLANES - 1

    def shifted(cur, prev_ref):
        prev_row = jnp.where(first, 0.0, prev_ref[0, last_row:last_row + 1, :])
        rowid = lax.broadcasted_iota(I32, cur.shape, 0)
        return jnp.where(rowid == 0, prev_row, pltpu.roll(cur, 1, axis=0))

    h = h_ref[0]
    xx = shifted(h, hp_ref) - h
    xw = h + xx * mu_wag_ref[0:1, :]
    xa = h + xx * mu_wag_ref[1:2, :]
    xg = h + xx * mu_wag_ref[2:3, :]
    w_raw = w0_ref[...] + _dot(jnp.tanh(_dot(xw, w1_ref[...])), w2_ref[...])
    neg = -w_raw
    softplus = jnp.maximum(neg, 0.0) + jnp.log1p(jnp.exp(-jnp.abs(neg)))
    w = -softplus - 0.5
    lw = -jnp.exp(w)
    tm = lw.shape[0]
    row = lax.broadcasted_iota(I32, (tm, tm), 0)
    col = lax.broadcasted_iota(I32, (tm, tm), 1)
    tri = ((row >= col) & (row // WKV_CHUNK == col // WKV_CHUNK)).astype(BF16)
    cum = sum(jnp.dot(tri, piece, preferred_element_type=F32) for piece in _split3(lw))
    ag = _sigmoid(a0_ref[...] + _dot(_dot(xa, a1_ref[...]), a2_ref[...]))
    go_ref[0] = _dot(_sigmoid(_dot(xg, g1_ref[...])), g2_ref[...])

    r = r_ref[0]
    k = k_ref[0]
    v = v_ref[0]
    r = r + (shifted(r, rp_ref) - r) * mu_rkv_ref[0:1, :]
    k = k + (shifted(k, kp_ref) - k) * mu_rkv_ref[1:2, :]
    v = v + (shifted(v, vp_ref) - v) * mu_rkv_ref[2:3, :]
    kk = k * kk_ref[...]
    k = k * (1.0 + (ag - 1.0) * ka_ref[...])
    for hd in range(RWKV_HEADS):
        sl = slice(hd * RWKV_N, (hd + 1) * RWKV_N)
        ro_ref[0, hd] = r[:, sl]
        ko_ref[0, hd] = k[:, sl]
        vo_ref[0, hd] = v[:, sl]
        lwo_ref[0, hd] = lw[:, sl]
        cumo_ref[0, hd] = cum[:, sl]
        kko_ref[0, hd] = kk[:, sl]
        ago_ref[0, hd] = ag[:, sl]


def _rwkv_prep(h, rkv, mu_rkv, mu_wag, w0, w1, w2, a0, a1, a2, g1, g2, k_k, k_a, tm=256):
    b, t, d = h.shape
    tm = min(tm, t)
    c = RWKV_W
    spt = tm // SUBLANES
    prevblk = lambda i: jnp.maximum(i * spt - 1, 0)
    full = lambda arr: pl.BlockSpec(arr.shape, lambda bi, i: (0,) * arr.ndim)
    row = lambda arr: arr.reshape(1, -1)
    w0, a0, k_k, k_a = row(w0), row(a0), row(k_k), row(k_a)
    in_specs = [
        pl.BlockSpec((1, tm, d), lambda bi, i: (bi, i, 0)),
        pl.BlockSpec((1, SUBLANES, d), lambda bi, i: (bi, prevblk(i), 0)),
        pl.BlockSpec((1, tm, c), lambda bi, i: (bi, i, 0)),
        pl.BlockSpec((1, tm, c), lambda bi, i: (bi, i, 1)),
        pl.BlockSpec((1, tm, c), lambda bi, i: (bi, i, 2)),
        pl.BlockSpec((1, SUBLANES, c), lambda bi, i: (bi, prevblk(i), 0)),
        pl.BlockSpec((1, SUBLANES, c), lambda bi, i: (bi, prevblk(i), 1)),
        pl.BlockSpec((1, SUBLANES, c), lambda bi, i: (bi, prevblk(i), 2)),
    ] + [full(a) for a in (mu_rkv, mu_wag, w0, w1, w2, a0, a1, a2, g1, g2, k_k, k_a)]
    hm = jax.ShapeDtypeStruct((b, RWKV_HEADS, t, RWKV_N), F32)
    hm_spec = pl.BlockSpec((1, RWKV_HEADS, tm, RWKV_N), lambda bi, i: (bi, 0, i, 0))
    return pl.pallas_call(
        _rwkv_prep_kernel,
        grid=(b, t // tm),
        in_specs=in_specs,
        out_specs=[hm_spec] * 7 + [pl.BlockSpec((1, tm, c), lambda bi, i: (bi, i, 0))],
        out_shape=[hm] * 7 + [jax.ShapeDtypeStruct((b, t, c), F32)],
        compiler_params=_cparams(("parallel", "arbitrary")),
        name="rwkv_prep",
    )(h, h, rkv, rkv, rkv, rkv, rkv, rkv, mu_rkv, mu_wag, w0, w1, w2, a0, a1, a2, g1, g2, k_k, k_a)


def _split3(x):
    hi = x.astype(BF16)
    r1 = x - hi.astype(F32)
    mid = r1.astype(BF16)
    lo = (r1 - mid.astype(F32)).astype(BF16)
    return hi, mid, lo


def _bmm(a, b):
    return jnp.einsum("gmk,gkn->gmn", a.astype(BF16), b.astype(BF16), preferred_element_type=F32)


def _bmm_nt(a, b):
    return jnp.einsum("gmk,gnk->gmn", a.astype(BF16), b.astype(BF16), preferred_element_type=F32)


def _wkv_kernel(r_ref, k_ref, v_ref, lw_ref, cum_ref, kk_ref, ag_ref, rk_ref, lnw_ref, lnb_ref, y_ref,
                s_ref, st_ref):
    c = WKV_CHUNK
    n = RWKV_N
    hb, tc = r_ref.shape[1], r_ref.shape[2]
    nc = tc // c
    g = hb * nc

    @pl.when(pl.program_id(2) == 0)
    def _():
        s_ref[...] = jnp.zeros_like(s_ref)

    chunks = lambda ref: ref[0].reshape(g, c, n)
    r, k, v, lw, cum, kk, ag = (chunks(ref) for ref in (r_ref, k_ref, v_ref, lw_ref, cum_ref, kk_ref, ag_ref))
    kk = kk / jnp.maximum(jnp.sqrt(jnp.sum(kk * kk, axis=-1, keepdims=True)), 1e-12)
    dinc = jnp.exp(cum)
    dinv = jnp.exp(-cum)
    rt = r * dinc
    kt = k * dinv
    at = -kk * jnp.exp(cum - lw)
    bt = kk * ag * dinv
    d_chunk = dinc[:, c - 1:c, :]

    row = lax.broadcasted_iota(I32, (1, c, c), 1)
    col = lax.broadcasted_iota(I32, (1, c, c), 2)
    strict = row > col
    incl = row >= col
    eye = jnp.broadcast_to((row == col).astype(F32), (g, c, c))
    p = _bmm_nt(jnp.concatenate([at, rt], axis=1), jnp.concatenate([bt, kt], axis=1))
    a_ab = jnp.where(strict, p[:, :c, :c], 0.0)
    a_ak = jnp.where(strict, p[:, :c, c:], 0.0)
    a_rb = jnp.where(incl, p[:, c:, :c], 0.0)
    a_rk = jnp.where(incl, p[:, c:, c:], 0.0)
    x = a_ab
    tinv = eye + x
    for _ in range(c.bit_length() - 2):
        x = _bmm(x, x)
        tinv = tinv + _bmm(tinv, x)
    z = _bmm(a_ak, v)
    ta = _bmm(tinv, jnp.concatenate([at, z], axis=2))
    ry = _bmm(a_rb, ta)
    rp = rt + ry[:, :, :n]
    yv = ry[:, :, n:] + _bmm(a_rk, v)
    vt = _bmm_nt(eye, v)
    att = _bmm_nt(eye, at)
    tt = _bmm_nt(jnp.concatenate([att, _bmm_nt(vt, a_ak)], axis=1), tinv)
    mn = _bmm(tt, bt)
    moff = mn[:, :n].reshape(hb, nc, n, n)
    n2 = (mn[:, n:] + _bmm(vt, kt)).reshape(hb, nc, n, n)
    dch = d_chunk.reshape(hb, nc, 1, n)

    s = s_ref[...]
    for ci in range(nc):
        st_ref[:, ci] = s
        s = (s + _bmm(s, moff[:, ci]) + n2[:, ci]) * dch[:, ci]
    s_ref[...] = s

    y = _bmm_nt(rp, st_ref[...].reshape(g, n, n)) + yv
    mu = jnp.mean(y, axis=-1, keepdims=True)
    yc = y - mu
    var = jnp.mean(yc * yc, axis=-1, keepdims=True)
    y = (yc * lax.rsqrt(var + RWKV_GN_EPS)).reshape(hb, tc, n) * lnw_ref[...] + lnb_ref[...]
    bonus = jnp.sum((r * k).reshape(hb, tc, n) * rk_ref[...], axis=-1, keepdims=True)
    y = y + bonus * v.reshape(hb, tc, n)
    y_ref[0] = jnp.concatenate([y[hd] for hd in range(hb)], axis=1)


def _wkv_scan(r, k, v, lw, cum, kk, ag, r_k, lnx_w, lnx_b):
    b, hh, t, n = r.shape
    tc = min(WKV_CHUNK * WKV_CHUNKS_PER_STEP, t)
    hb = WKV_HEADS_PER_STEP
    blk = pl.BlockSpec((1, hb, tc, n), lambda bi, hi, i: (bi, hi, i, 0))
    par = pl.BlockSpec((hb, 1, n), lambda bi, hi, i: (hi, 0, 0))
    per_head = lambda arr: arr.reshape(hh, 1, n)
    return pl.pallas_call(
        _wkv_kernel,
        grid=(b, hh // hb, t // tc),
        in_specs=[blk] * 7 + [par] * 3,
        out_specs=pl.BlockSpec((1, tc, hb * n), lambda bi, hi, i: (bi, i, hi)),
        out_shape=jax.ShapeDtypeStruct((b, t, hh * n), F32),
        scratch_shapes=[pltpu.VMEM((hb, n, n), F32), pltpu.VMEM((hb, tc // WKV_CHUNK, n, n), F32)],
        compiler_params=_cparams(("parallel", "parallel", "arbitrary")),
        name="wkv7_scan",
    )(r, k, v, lw, cum, kk, ag, per_head(r_k), per_head(lnx_w), per_head(lnx_b))


def _merge_kernel(ya_ref, yr_ref, g_ref, ga_ref, gr_ref, pa_ref, pr_ref, o_ref):
    ma = _dot(ya_ref[...], pa_ref[...])
    mr = _dot(yr_ref[...] * g_ref[...], pr_ref[...])
    o_ref[...] = (_sigmoid(ga_ref[...]) * ma + _sigmoid(gr_ref[...]) * mr).astype(o_ref.dtype)


def _merge(ya, yr, g, gates, p_attn, p_rwkv, tm=256):
    m, c = ya.shape
    d = p_attn.shape[1]
    tm = min(tm, m)
    tile = pl.BlockSpec((tm, c), lambda i: (i, 0))
    return pl.pallas_call(
        _merge_kernel,
        grid=(m // tm,),
        in_specs=[tile, tile, tile,
                  pl.BlockSpec((tm, d), lambda i: (i, 0)),
                  pl.BlockSpec((tm, d), lambda i: (i, 1)),
                  pl.BlockSpec((c, d), lambda i: (0, 0)),
                  pl.BlockSpec((c, d), lambda i: (0, 0))],
        out_specs=pl.BlockSpec((tm, d), lambda i: (i, 0)),
        out_shape=jax.ShapeDtypeStruct((m, d), BF16),
        compiler_params=_cparams(("parallel",)),
        name="gated_merge",
    )(ya, yr, g, gates, gates, p_attn, p_rwkv)


def _outproj_router_kernel(alpha, mg_ref, x_ref, wo_ref, g_ref, b_ref, wr_ref, br_ref,
                           h_ref, hrow_ref, eid_ref, gate_ref):
    mix = jnp.dot(mg_ref[...], wo_ref[...], preferred_element_type=F32)
    h = _layer_norm(alpha * x_ref[...] + mix, g_ref[...], b_ref[...])
    h_ref[...] = h
    slab = h.shape[1] // LANES
    for j in range(slab):
        hrow_ref[pl.ds(j, h.shape[0], stride=slab), :] = h[:, j * LANES:(j + 1) * LANES]
    h_hi = h.astype(BF16)
    h_lo = (h - h_hi.astype(F32)).astype(BF16)
    logits = (jnp.dot(h_hi, wr_ref[0], preferred_element_type=F32)
              + jnp.dot(h_lo, wr_ref[0], preferred_element_type=F32)
              + jnp.dot(h_hi, wr_ref[1], preferred_element_type=F32)) + br_ref[...]
    lane = lax.broadcasted_iota(I32, logits.shape, 1)
    ninf = -jnp.inf
    big = jnp.int32(2 * LANES)
    glog = jnp.where(lane < N_GROUPS, logits, ninf)
    gmax = jnp.max(glog, axis=-1, keepdims=True)
    gidx = jnp.min(jnp.where(glog == gmax, lane, big), axis=-1, keepdims=True)
    gtop = 1.0 / jnp.sum(jnp.exp(glog - gmax), axis=-1, keepdims=True)
    eg = (lane - N_GROUPS) // EXPERTS_PER_GROUP
    in_group = (lane >= N_GROUPS) & (lane < N_GROUPS + N_EXPERTS) & (eg == gidx)
    el = jnp.where(in_group, logits, ninf)
    m1 = jnp.max(el, axis=-1, keepdims=True)
    i1 = jnp.min(jnp.where(el == m1, lane, big), axis=-1, keepdims=True)
    el2 = jnp.where(lane == i1, ninf, el)
    m2 = jnp.max(el2, axis=-1, keepdims=True)
    i2 = jnp.min(jnp.where(el2 == m2, lane, big), axis=-1, keepdims=True)
    t = jnp.exp(m2 - m1)
    p1 = 1.0 / (1.0 + t)
    p2 = t / (1.0 + t)
    eid_ref[...] = jnp.where(lane == 0, i1 - N_GROUPS, jnp.where(lane == 1, i2 - N_GROUPS, 0))
    gate_ref[...] = jnp.where(lane == 0, gtop * p1, jnp.where(lane == 1, gtop * p2, 0.0))


def _outproj_router(merged, x, w_o, ln_g, ln_b, w_router, b_router, alpha, tm=256):
    m, d = x.shape
    tm = min(tm, m)
    tile = pl.BlockSpec((tm, d), lambda i: (i, 0))
    vec = pl.BlockSpec((1, d), lambda i: (0, 0))
    small = pl.BlockSpec((tm, LANES), lambda i: (i, 0))
    return pl.pallas_call(
        functools.partial(_outproj_router_kernel, alpha),
        grid=(m // tm,),
        in_specs=[tile, tile, pl.BlockSpec((d, d), lambda i: (0, 0)), vec, vec,
                  pl.BlockSpec((2, d, LANES), lambda i: (0, 0, 0)), pl.BlockSpec((1, LANES), lambda i: (0, 0))],
        out_specs=[tile, pl.BlockSpec((tm * (d // LANES), LANES), lambda i: (i, 0)), small, small],
        out_shape=[jax.ShapeDtypeStruct((m, d), F32), jax.ShapeDtypeStruct((m * (d // LANES), LANES), F32),
                   jax.ShapeDtypeStruct((m, LANES), I32), jax.ShapeDtypeStruct((m, LANES), F32)],
        compiler_params=_cparams(("parallel",)),
        name="outproj_ln_router",
    )(merged, x, w_o, ln_g.reshape(1, d), ln_b.reshape(1, d), w_router, b_router)


def _onehots(eid_ref):
    tm = eid_ref.shape[1]
    e_iota = lax.broadcasted_iota(I32, (N_EXPERTS, tm), 0)
    oh0 = (eid_ref[0:1, :] == e_iota).astype(F32)
    oh1 = (eid_ref[1:2, :] == e_iota).astype(F32)
    return oh0, oh1


def _count_kernel(eid_ref, cnt_ref):
    @pl.when(pl.program_id(0) == 0)
    def _():
        cnt_ref[...] = jnp.zeros_like(cnt_ref)

    oh0, oh1 = _onehots(eid_ref)
    cnt_ref[...] += jnp.sum(oh0 + oh1, axis=1, keepdims=True)


def _slot_kernel(eid_ref, pstart_ref, dest_ref, run_ref):
    @pl.when(pl.program_id(0) == 0)
    def _():
        run_ref[...] = jnp.zeros_like(run_ref)

    tm = eid_ref.shape[1]
    oh0, oh1 = _onehots(eid_ref)
    both = oh0 + oh1
    earlier = (lax.broadcasted_iota(I32, (tm, tm), 0) < lax.broadcasted_iota(I32, (tm, tm), 1)).astype(BF16)
    pre = jnp.dot(both.astype(BF16), earlier, preferred_element_type=F32)
    base = pre + run_ref[...] + pstart_ref[...]
    dest_ref[0:1, :] = jnp.sum(oh0 * base, axis=0, keepdims=True).astype(I32)
    dest_ref[1:2, :] = jnp.sum(oh1 * base, axis=0, keepdims=True).astype(I32)
    run_ref[...] += jnp.sum(both, axis=1, keepdims=True)


def _expert_counts(eid_t):
    m = eid_t.shape[1]
    tm = min(ROUTE_TILE, m)
    return pl.pallas_call(
        _count_kernel,
        grid=(m // tm,),
        in_specs=[pl.BlockSpec((TOP_K, tm), lambda i: (0, i))],
        out_specs=pl.BlockSpec((N_EXPERTS, 1), lambda i: (0, 0)),
        out_shape=jax.ShapeDtypeStruct((N_EXPERTS, 1), F32),
        compiler_params=_cparams(("arbitrary",)),
        name="expert_counts",
    )(eid_t)


def _expert_slots(eid_t, pstart):
    m = eid_t.shape[1]
    tm = min(ROUTE_TILE, m)
    return pl.pallas_call(
        _slot_kernel,
        grid=(m // tm,),
        in_specs=[pl.BlockSpec((TOP_K, tm), lambda i: (0, i)),
                  pl.BlockSpec((N_EXPERTS, 1), lambda i: (0, 0))],
        out_specs=pl.BlockSpec((TOP_K, tm), lambda i: (0, i)),
        out_shape=jax.ShapeDtypeStruct((TOP_K, m), I32),
        scratch_shapes=[pltpu.VMEM((N_EXPERTS, 1), F32)],
        compiler_params=_cparams(("arbitrary",)),
        name="expert_slots",
    )(eid_t, pstart)


def _row_slab(ref, row, slab):
    return ref.at[pl.ds(pl.multiple_of(row * slab, slab), slab), :]


def _dispatch_kernel(dest_ref, h_ref, xb_in_ref, xb_ref, sem):
    del xb_in_ref
    tm = DISPATCH_TILE
    slab = h_ref.shape[0] // tm
    t0 = pl.program_id(0) * tm

    def copy(n, k):
        return pltpu.make_async_copy(_row_slab(h_ref, n, slab),
                                     _row_slab(xb_ref, dest_ref[TOP_K * (t0 + n) + k], slab), sem)

    def start(n, carry):
        for k in range(TOP_K):
            copy(n, k).start()
        return carry

    lax.fori_loop(0, tm, start, 0, unroll=8)
    for k in range(TOP_K):
        pltpu.make_async_copy(h_ref, xb_ref.at[pl.ds(0, h_ref.shape[0]), :], sem).wait()


DISPATCH_TILE = 512


def _dispatch(dest_flat, h_rows, m, n_rows):
    slab = h_rows.shape[0] // m
    assert m % DISPATCH_TILE == 0
    grid_spec = pltpu.PrefetchScalarGridSpec(
        num_scalar_prefetch=1,
        grid=(m // DISPATCH_TILE,),
        in_specs=[pl.BlockSpec((DISPATCH_TILE * slab, LANES), lambda i, dr: (i, 0)),
                  pl.BlockSpec(memory_space=pl.ANY)],
        out_specs=pl.BlockSpec(memory_space=pl.ANY),
        scratch_shapes=[pltpu.SemaphoreType.DMA(())],
    )
    return pl.pallas_call(
        _dispatch_kernel,
        grid_spec=grid_spec,
        out_shape=jax.ShapeDtypeStruct((n_rows * slab, LANES), F32),
        input_output_aliases={2: 0},
        compiler_params=pltpu.CompilerParams(dimension_semantics=("arbitrary",), has_side_effects=True),
        name="moe_dispatch",
    )(dest_flat, h_rows, jnp.zeros((n_rows * slab, LANES), F32))


def _expert_kernel(item_e_ref, item_row_ref, item_nsub_ref, wg_ref, wu_ref, wd_ref, xb_ref, yb_ref,
                   xin_ref, yout_ref, x_ref, act_ref, wgu_ref, wdb_ref, sem_in, sem_out):
    del item_e_ref
    it = pl.program_id(0)
    f = pl.program_id(1)
    n_items = pl.num_programs(0)
    nf = pl.num_programs(1)
    nsub = item_nsub_ref[it]
    row0 = item_row_ref[it]
    nxt = jnp.minimum(it + 1, n_items - 1)
    nsub_next = jnp.where(it + 1 < n_items, item_nsub_ref[nxt], 0)
    prv = jnp.maximum(it - 1, 0)
    nsub_prev = jnp.where(it > 0, item_nsub_ref[prv], 0)
    rb = MOE_ROWS
    ft = MOE_FF_TILE
    d = x_ref.shape[1]
    slab = d // LANES
    blk = rb * slab

    def stage_rows(ref, s):
        return ref.at[pl.ds(pl.multiple_of(s * blk, blk), blk), :]

    def hbm_rows(ref, item_row, s):
        return ref.at[pl.ds(pl.multiple_of((item_row + s * rb) * slab, blk), blk), :]

    def in_copy(item_row, s):
        return pltpu.make_async_copy(hbm_rows(xb_ref, item_row, s), stage_rows(xin_ref, s), sem_in)

    def out_copy(item_row, s):
        return pltpu.make_async_copy(stage_rows(yout_ref, s), hbm_rows(yb_ref, item_row, s), sem_out)

    def loop(n, fn):
        def body(s, carry):
            fn(s)
            return carry
        lax.fori_loop(0, n, body, 0)

    @pl.when(nsub > 0)
    def _():
        @pl.when(f == 0)
        def _():
            @pl.when(it == 0)
            def _():
                loop(nsub, lambda s: in_copy(row0, s).start())
            loop(nsub, lambda s: in_copy(row0, s).wait())

            def to_matrix(s):
                rows = pl.ds(pl.multiple_of(s * rb, rb), rb)
                base = pl.multiple_of(s * blk, blk)
                for j in range(slab):
                    x_ref[rows, j * LANES:(j + 1) * LANES] = xin_ref[pl.ds(base + j, rb, stride=slab), :].astype(BF16)
            loop(nsub, to_matrix)
            loop(nsub_next, lambda s: in_copy(item_row_ref[nxt], s).start())

        wgu_ref[:, :ft] = wg_ref[0].astype(BF16)
        wgu_ref[:, ft:] = wu_ref[0].astype(BF16)
        wdb_ref[pl.ds(pl.multiple_of(f * ft, ft), ft), :] = wd_ref[0].astype(BF16)

        def gate_up(start, size):
            rows = pl.ds(pl.multiple_of(start, rb), size)
            gu = jnp.dot(x_ref[rows, :], wgu_ref[...], preferred_element_type=F32)
            gate = gu[:, :ft]
            act_ref[f, rows, :] = (gate * _sigmoid(gate) * gu[:, ft:]).astype(BF16)

        def pair(p, carry):
            gate_up(p * (2 * rb), 2 * rb)
            return carry
        lax.fori_loop(0, nsub // 2, pair, 0)

        @pl.when(nsub % 2 == 1)
        def _():
            gate_up((nsub - 1) * rb, rb)

        @pl.when(f == nf - 1)
        def _():
            loop(nsub_prev, lambda s: out_copy(item_row_ref[prv], s).wait())

            def down(start, size):
                rows = pl.ds(pl.multiple_of(start, rb), size)
                act = jnp.concatenate([act_ref[j, rows, :] for j in range(act_ref.shape[0])], axis=1)
                y = jnp.dot(act, wdb_ref[...], preferred_element_type=F32)
                base = pl.multiple_of(start * slab, blk)
                for j in range(slab):
                    yout_ref[pl.ds(base + j, size, stride=slab), :] = y[:, j * LANES:(j + 1) * LANES]

            def down_pair(p, carry):
                down(p * (2 * rb), 2 * rb)
                return carry
            lax.fori_loop(0, nsub // 2, down_pair, 0)

            @pl.when(nsub % 2 == 1)
            def _():
                down((nsub - 1) * rb, rb)

            loop(nsub, lambda s: out_copy(row0, s).start())

            @pl.when(nsub_next == 0)
            def _():
                loop(nsub, lambda s: out_copy(row0, s).wait())


def _experts(item_e, item_row, item_nsub, w_gate, w_up, w_down, xb):
    d = w_gate.shape[1]
    ff = w_gate.shape[2]
    nf = ff // MOE_FF_TILE
    n_items = item_e.shape[0]
    rows_max = MOE_ROWS * MOE_SUBS_PER_ITEM
    slab = d // LANES

    def f_eff(f, ns, it):
        return jnp.where(ns[it] > 0, f, nf - 1)

    grid_spec = pltpu.PrefetchScalarGridSpec(
        num_scalar_prefetch=3,
        grid=(n_items, nf),
        in_specs=[
            pl.BlockSpec((1, d, MOE_FF_TILE), lambda it, f, ie, ir, ns: (ie[it], 0, f_eff(f, ns, it))),
            pl.BlockSpec((1, d, MOE_FF_TILE), lambda it, f, ie, ir, ns: (ie[it], 0, f_eff(f, ns, it))),
            pl.BlockSpec((1, MOE_FF_TILE, d), lambda it, f, ie, ir, ns: (ie[it], f_eff(f, ns, it), 0)),
            pl.BlockSpec(memory_space=pl.ANY),
        ],
        out_specs=pl.BlockSpec(memory_space=pl.ANY),
        scratch_shapes=[
            pltpu.VMEM((rows_max * slab, LANES), F32),
            pltpu.VMEM((rows_max * slab, LANES), F32),
            pltpu.VMEM((rows_max, d), BF16),
            pltpu.VMEM((nf, rows_max, MOE_FF_TILE), BF16),
            pltpu.VMEM((d, 2 * MOE_FF_TILE), BF16),
            pltpu.VMEM((ff, d), BF16),
            pltpu.SemaphoreType.DMA(()),
            pltpu.SemaphoreType.DMA(()),
        ],
    )
    return pl.pallas_call(
        _expert_kernel,
        grid_spec=grid_spec,
        out_shape=jax.ShapeDtypeStruct(xb.shape, F32),
        input_output_aliases={6: 0},
        compiler_params=pltpu.CompilerParams(dimension_semantics=("arbitrary", "arbitrary"),
                                             vmem_limit_bytes=VMEM_LIMIT_BYTES, has_side_effects=True),
        name="moe_experts",
    )(item_e, item_row, item_nsub, w_gate, w_up, w_down, xb)


COMBINE_TILE = 256


def _combine_kernel(alpha, dest_ref, yb_ref, h_ref, gate_ref, g_ref, b_ref, o_ref, buf_ref, sem):
    tm = COMBINE_TILE
    t0 = pl.program_id(0) * tm

    slab = buf_ref.shape[1] // tm

    def copy(n, k):
        return pltpu.make_async_copy(_row_slab(yb_ref, dest_ref[TOP_K * (t0 + n) + k], slab),
                                     _row_slab(buf_ref.at[k], n, slab), sem)

    def start(n, carry):
        for k in range(TOP_K):
            copy(n, k).start()
        return carry

    lax.fori_loop(0, tm, start, 0, unroll=8)
    for k in range(TOP_K):
        pltpu.make_async_copy(yb_ref.at[pl.ds(0, buf_ref.shape[1]), :], buf_ref.at[k], sem).wait()
    gate = gate_ref[...]
    g0 = gate[:, 0:1]
    g1 = gate[:, 1:2]
    col = lambda k, j: buf_ref[k, pl.ds(j, tm, stride=slab), :]
    ff = jnp.concatenate([g0 * col(0, j) + g1 * col(1, j) for j in range(slab)], axis=1)
    o_ref[...] = _layer_norm(alpha * h_ref[...] + ff, g_ref[...], b_ref[...])


def _combine(dest_flat, yb, h, gates, ln_g, ln_b, alpha):
    m, d = h.shape
    tm = COMBINE_TILE
    assert m % tm == 0
    grid_spec = pltpu.PrefetchScalarGridSpec(
        num_scalar_prefetch=1,
        grid=(m // tm,),
        in_specs=[pl.BlockSpec(memory_space=pl.ANY),
                  pl.BlockSpec((tm, d), lambda i, dr: (i, 0)),
                  pl.BlockSpec((tm, LANES), lambda i, dr: (i, 0)),
                  pl.BlockSpec((1, d), lambda i, dr: (0, 0)),
                  pl.BlockSpec((1, d), lambda i, dr: (0, 0))],
        out_specs=pl.BlockSpec((tm, d), lambda i, dr: (i, 0)),
        scratch_shapes=[pltpu.VMEM((TOP_K, tm * (d // LANES), LANES), F32), pltpu.SemaphoreType.DMA(())],
    )
    return pl.pallas_call(
        functools.partial(_combine_kernel, alpha),
        grid_spec=grid_spec,
        out_shape=jax.ShapeDtypeStruct((m, d), F32),
        compiler_params=_cparams(("arbitrary",)),
        name="moe_combine_ln",
    )(dest_flat, yb, h, gates, ln_g.reshape(1, d), ln_b.reshape(1, d))


def _moe_tables(counts):
    nsub_e = (counts + MOE_ROWS - 1) // MOE_ROWS
    pstart = (jnp.cumsum(nsub_e) - nsub_e) * MOE_ROWS
    nitem_e = (nsub_e + MOE_SUBS_PER_ITEM - 1) // MOE_SUBS_PER_ITEM
    item_end = jnp.cumsum(nitem_e)
    return nsub_e, pstart, nitem_e, item_end


def _moe(h, h_rows, eid, gates, w_gate, w_up, w_down, ln_g, ln_b, alpha):
    m, d = h.shape
    n_assign = m * TOP_K
    n_blocks = (n_assign + N_EXPERTS * (MOE_ROWS - 1) + MOE_ROWS - 1) // MOE_ROWS
    n_rows = n_blocks * MOE_ROWS
    n_items = N_EXPERTS + n_assign // (MOE_ROWS * MOE_SUBS_PER_ITEM)

    eid_t = eid[:, :TOP_K].T
    counts = _expert_counts(eid_t)[:, 0].astype(I32)
    nsub_e, pstart, nitem_e, item_end = _moe_tables(counts)
    dest_t = _expert_slots(eid_t, pstart.astype(F32).reshape(N_EXPERTS, 1))
    dest_flat = dest_t.T.reshape(-1)

    it = jnp.arange(n_items, dtype=I32)
    total_items = item_end[-1]
    item_e = jnp.minimum(jnp.sum(item_end[None, :] <= it[:, None], axis=1).astype(I32), N_EXPERTS - 1)
    j = it - (item_end - nitem_e)[item_e]
    used = it < total_items
    last_e = item_e[jnp.maximum(total_items - 1, 0)]
    item_nsub = jnp.where(used, jnp.clip(nsub_e[item_e] - j * MOE_SUBS_PER_ITEM, 0, MOE_SUBS_PER_ITEM), 0).astype(I32)
    item_row = jnp.where(used, pstart[item_e] + j * (MOE_ROWS * MOE_SUBS_PER_ITEM), 0).astype(I32)
    item_e = jnp.where(used, item_e, last_e).astype(I32)

    xb = _dispatch(dest_flat, h_rows, m, n_rows)
    yb = _experts(item_e, item_row, item_nsub, w_gate, w_up, w_down, xb)
    return _combine(dest_flat, yb, h, gates, ln_g, ln_b, alpha)


def _pad_cols(w, n):
    return jnp.pad(w, ((0, 0), (0, n - w.shape[1])))


def _pad_rows(w, n):
    return jnp.pad(w, ((0, n - w.shape[0]), (0, 0)))


def kernel(x, w_in, attn_sinks, rw_mu_rkv, rw_mu_wag, rw_w0, rw_w1, rw_w2, rw_a0, rw_a1, rw_a2, rw_g1, rw_g2, rw_k_k, rw_k_a, rw_r_k, rw_lnx_w, rw_lnx_b, p_attn, p_rwkv, w_o, ln1_g, ln1_b, w_group, b_group, w_expert, b_expert, w_gate, w_up, w_down, ln2_g, ln2_b):
    b, t, d = x.shape
    depth = w_in.shape[0]
    m = b * t
    alpha = (2.0 * depth) ** 0.25
    cosb, sinb = _rope_tables(t)
    qkv_w = ATTN_Q_W + 2 * ATTN_KV_W
    rkv_w = 3 * RWKV_W
    h = x
    for l in range(depth):
        hf = h.reshape(m, d)
        hb = hf.astype(BF16)
        w_in_b = w_in[l].astype(BF16)
        qkv = _matmul_cols(hb, w_in_b, 0, qkv_w, F32)
        rkv = _matmul_cols(hb, w_in_b, qkv_w, rkv_w, F32)
        gates = _matmul_cols(hb, w_in_b, qkv_w + rkv_w, 2 * d, F32)

        y_a = _attention(qkv.reshape(b, t, qkv_w), attn_sinks[l], cosb, sinb)

        lora = lambda w, n: _pad_cols(w, n).astype(BF16)
        lorb = lambda w, n: _pad_rows(w, n).astype(BF16)
        n_w = -(-rw_w1.shape[2] // LANES) * LANES
        n_a = -(-rw_a1.shape[2] // LANES) * LANES
        n_g = -(-rw_g1.shape[2] // LANES) * LANES
        r_, k_, v_, lw_, cum_, kk_, ag_, g_ = _rwkv_prep(
            h, rkv.reshape(b, t, rkv_w), rw_mu_rkv[l], rw_mu_wag[l], rw_w0[l],
            lora(rw_w1[l], n_w), lorb(rw_w2[l], n_w), rw_a0[l], lora(rw_a1[l], n_a), lorb(rw_a2[l], n_a),
            lora(rw_g1[l], n_g), lorb(rw_g2[l], n_g), rw_k_k[l], rw_k_a[l])
        y_r = _wkv_scan(r_, k_, v_, lw_, cum_, kk_, ag_, rw_r_k[l], rw_lnx_w[l], rw_lnx_b[l])

        merged = _merge(y_a.reshape(m, ATTN_Q_W), y_r.reshape(m, RWKV_W), g_.reshape(m, RWKV_W), gates,
                        p_attn[l].astype(BF16), p_rwkv[l].astype(BF16))
        w_router = _pad_cols(jnp.concatenate([w_group[l], w_expert[l]], axis=1), LANES)
        w_router_hi = w_router.astype(BF16)
        w_router = jnp.stack([w_router_hi, (w_router - w_router_hi.astype(F32)).astype(BF16)])
        b_router = _pad_cols(jnp.concatenate([b_group[l], b_expert[l]])[None, :], LANES)
        h1, h1_rows, eid, gate = _outproj_router(merged, hf, w_o[l].astype(BF16), ln1_g[l], ln1_b[l],
                                        w_router, b_router, alpha)
        h2 = _moe(h1, h1_rows, eid, gate, w_gate[l], w_up[l], w_down[l], ln2_g[l], ln2_b[l], alpha)
        h = h2.reshape(b, t, d)
    return h
```

```python
import functools

import jax
import jax.numpy as jnp
from jax import lax
from jax.experimental import pallas as pl
from jax.experimental.pallas import tpu as pltpu

F32 = jnp.float32
BF16 = jnp.bfloat16
I32 = jnp.int32

HEAD_DIM = 64
ATTN_Q_HEADS = 16
ATTN_KV_HEADS = 4
ATTN_GROUP = ATTN_Q_HEADS // ATTN_KV_HEADS
ATTN_Q_W = ATTN_Q_HEADS * HEAD_DIM
ATTN_KV_W = ATTN_KV_HEADS * HEAD_DIM
WINDOW = 128
ROPE_THETA = 10000.0
RWKV_HEADS = 16
RWKV_N = 64
RWKV_W = RWKV_HEADS * RWKV_N
RWKV_GN_EPS = 64e-5
N_GROUPS = 8
EXPERTS_PER_GROUP = 8
N_EXPERTS = N_GROUPS * EXPERTS_PER_GROUP
TOP_K = 2
LN_EPS = 1e-5

LANES = 128
SUBLANES = 8
VMEM_LIMIT_BYTES = 56 * 1024 * 1024

WKV_CHUNK = 64
WKV_CHUNKS_PER_STEP = 8
WKV_HEADS_PER_STEP = 4
MOE_ROWS = 128
MOE_SUBS_PER_ITEM = 4
MOE_K_STEPS = 4
ROUTE_TILE = 512


def _cparams(sem, vmem=VMEM_LIMIT_BYTES):
    return pltpu.CompilerParams(dimension_semantics=sem, vmem_limit_bytes=vmem)


def _sigmoid(x):
    return 1.0 / (1.0 + jnp.exp(-x))


def _dot(a, b):
    return jnp.dot(a.astype(BF16), b.astype(BF16), preferred_element_type=F32)


def _dot_nt(a, b):
    return lax.dot_general(a.astype(BF16), b.astype(BF16), (((1,), (1,)), ((), ())),
                           preferred_element_type=F32)


def _layer_norm(t, g, b):
    mu = jnp.mean(t, axis=-1, keepdims=True)
    d = t - mu
    var = jnp.mean(d * d, axis=-1, keepdims=True)
    return d * lax.rsqrt(var + LN_EPS) * g + b


def _matmul_kernel(a_ref, b_ref, o_ref):
    o_ref[...] = jnp.dot(a_ref[...], b_ref[...], preferred_element_type=F32).astype(o_ref.dtype)


def _matmul_cols(a, b, col0, ncols, out_dtype, tm=1024, tn=512):
    m, k = a.shape
    tm = min(tm, m)
    cb = col0 // tn
    assert col0 % tn == 0 and ncols % tn == 0 and m % tm == 0
    return pl.pallas_call(
        _matmul_kernel,
        grid=(m // tm, ncols // tn),
        in_specs=[pl.BlockSpec((tm, k), lambda i, j: (i, 0)),
                  pl.BlockSpec((k, tn), lambda i, j: (0, j + cb))],
        out_specs=pl.BlockSpec((tm, tn), lambda i, j: (i, j)),
        out_shape=jax.ShapeDtypeStruct((m, ncols), out_dtype),
        compiler_params=_cparams(("parallel", "arbitrary")),
        name="inproj_matmul",
    )(a, b)


def _rope(x, cosb, sinb):
    half = HEAD_DIM // 2
    lane = lax.broadcasted_iota(I32, cosb.shape, 1)
    first_half = (lane % HEAD_DIM) < half
    outs = []
    for g in range(x.shape[1] // LANES):
        xg = x[:, g * LANES:(g + 1) * LANES]
        partner = jnp.where(first_half, pltpu.roll(xg, LANES - half, axis=1), pltpu.roll(xg, half, axis=1))
        outs.append(xg * cosb + partner * sinb)
    return outs


def _attn_kernel(sinks_ref, q_ref, kc_ref, kp_ref, vc_ref, vp_ref, cosc_ref, sinc_ref, cosp_ref, sinp_ref, o_ref):
    blk = pl.program_id(1)
    tq = q_ref.shape[1]
    qg = _rope(q_ref[0], cosc_ref[...], sinc_ref[...])
    kcg = _rope(kc_ref[0], cosc_ref[...], sinc_ref[...])
    kpg = _rope(kp_ref[0], cosp_ref[...], sinp_ref[...])
    vc = vc_ref[0]
    vp = vp_ref[0]

    def head(groups, h):
        g = groups[h // 2]
        return g[:, (h % 2) * HEAD_DIM:(h % 2 + 1) * HEAD_DIM]

    rows = ATTN_GROUP * tq
    qi = lax.broadcasted_iota(I32, (rows, 2 * tq), 0) % tq
    kj = lax.broadcasted_iota(I32, (rows, 2 * tq), 1)
    dist = qi + tq - kj
    valid = (dist >= 0) & (dist < WINDOW) & ((blk > 0) | (kj >= tq))
    rid = lax.broadcasted_iota(I32, (rows, 1), 0) // tq
    scale = HEAD_DIM ** -0.5
    for kvh in range(ATTN_KV_HEADS):
        qh = jnp.concatenate([head(qg, kvh * ATTN_GROUP + g) for g in range(ATTN_GROUP)], axis=0)
        kw = jnp.concatenate([head(kpg, kvh), head(kcg, kvh)], axis=0)
        vw = jnp.concatenate([vp[:, kvh * HEAD_DIM:(kvh + 1) * HEAD_DIM],
                              vc[:, kvh * HEAD_DIM:(kvh + 1) * HEAD_DIM]], axis=0)
        s = _dot_nt(qh, kw) * scale
        s = jnp.where(valid, s, -jnp.inf)
        sink = jnp.zeros((rows, 1), F32)
        for g in range(ATTN_GROUP):
            sink = jnp.where(rid == g, sinks_ref[kvh * ATTN_GROUP + g], sink)
        m = jnp.maximum(jnp.max(s, axis=-1, keepdims=True), sink)
        e = jnp.exp(s - m)
        denom = jnp.sum(e, axis=-1, keepdims=True) + jnp.exp(sink - m)
        p = e / denom
        o = _dot(p, vw)
        for g in range(ATTN_GROUP):
            hq = kvh * ATTN_GROUP + g
            o_ref[0, :, hq * HEAD_DIM:(hq + 1) * HEAD_DIM] = o[g * tq:(g + 1) * tq]


def _attention(qkv, sinks, cosb, sinb):
    b, t, _ = qkv.shape
    tq = WINDOW
    nb = t // tq
    kcol = ATTN_Q_W // ATTN_KV_W
    prev = lambda i: jnp.maximum(i - 1, 0)
    grid_spec = pltpu.PrefetchScalarGridSpec(
        num_scalar_prefetch=0,
        grid=(b, nb),
        in_specs=[
            pl.BlockSpec(memory_space=pltpu.SMEM),
            pl.BlockSpec((1, tq, ATTN_Q_W), lambda bi, i: (bi, i, 0)),
            pl.BlockSpec((1, tq, ATTN_KV_W), lambda bi, i: (bi, i, kcol)),
            pl.BlockSpec((1, tq, ATTN_KV_W), lambda bi, i: (bi, prev(i), kcol)),
            pl.BlockSpec((1, tq, ATTN_KV_W), lambda bi, i: (bi, i, kcol + 1)),
            pl.BlockSpec((1, tq, ATTN_KV_W), lambda bi, i: (bi, prev(i), kcol + 1)),
            pl.BlockSpec((tq, LANES), lambda bi, i: (i, 0)),
            pl.BlockSpec((tq, LANES), lambda bi, i: (i, 0)),
            pl.BlockSpec((tq, LANES), lambda bi, i: (prev(i), 0)),
            pl.BlockSpec((tq, LANES), lambda bi, i: (prev(i), 0)),
        ],
        out_specs=pl.BlockSpec((1, tq, ATTN_Q_W), lambda bi, i: (bi, i, 0)),
    )
    return pl.pallas_call(
        _attn_kernel,
        grid_spec=grid_spec,
        out_shape=jax.ShapeDtypeStruct((b, t, ATTN_Q_W), F32),
        compiler_params=_cparams(("parallel", "arbitrary")),
        name="swa_attention",
    )(sinks, qkv, qkv, qkv, qkv, qkv, cosb, sinb, cosb, sinb)


def _rope_tables(t):
    inv = 1.0 / (ROPE_THETA ** (jnp.arange(0, HEAD_DIM, 2, dtype=F32) / HEAD_DIM))
    ang = jnp.arange(t, dtype=F32)[:, None] * inv[None, :]
    cos, sin = jnp.cos(ang), jnp.sin(ang)
    reps = LANES // HEAD_DIM
    cosb = jnp.tile(jnp.concatenate([cos, cos], axis=-1), (1, reps))
    sinb = jnp.tile(jnp.concatenate([-sin, sin], axis=-1), (1, reps))
    return cosb, sinb


def _rwkv_prep_kernel(h_ref, hp_ref, r_ref, k_ref, v_ref, rp_ref, kp_ref, vp_ref,
                      mu_rkv_ref, mu_wag_ref, w0_ref, w1_ref, w2_ref, a0_ref, a1_ref, a2_ref,
                      g1_ref, g2_ref, kk_ref, ka_ref,
                      ro_ref, ko_ref, vo_ref, lwo_ref, cumo_ref, kko_ref, ago_ref, go_ref):
    first = pl.program_id(1) == 0
    last_row = SUBLANES - 1

    def shifted(cur, prev_ref):
        prev_row = jnp.where(first, 0.0, prev_ref[0, last_row:last_row + 1, :])
        rowid = lax.broadcasted_iota(I32, cur.shape, 0)
        return jnp.where(rowid == 0, prev_row, pltpu.roll(cur, 1, axis=0))

    h = h_ref[0]
    xx = shifted(h, hp_ref) - h
    xw = h + xx * mu_wag_ref[0:1, :]
    xa = h + xx * mu_wag_ref[1:2, :]
    xg = h + xx * mu_wag_ref[2:3, :]
    w_raw = w0_ref[...] + _dot(jnp.tanh(_dot(xw, w1_ref[...])), w2_ref[...])
    neg = -w_raw
    softplus = jnp.maximum(neg, 0.0) + jnp.log1p(jnp.exp(-jnp.abs(neg)))
    w = -softplus - 0.5
    lw = -jnp.exp(w)
    tm = lw.shape[0]
    row = lax.broadcasted_iota(I32, (tm, tm), 0)
    col = lax.broadcasted_iota(I32, (tm, tm), 1)
    tri = ((row >= col) & (row // WKV_CHUNK == col // WKV_CHUNK)).astype(BF16)
    cum = sum(jnp.dot(tri, piece, preferred_element_type=F32) for piece in _split3(lw))
    ag = _sigmoid(a0_ref[...] + _dot(_dot(xa, a1_ref[...]), a2_ref[...]))
    go_ref[0] = _dot(_sigmoid(_dot(xg, g1_ref[...])), g2_ref[...])

    r = r_ref[0]
    k = k_ref[0]
    v = v_ref[0]
    r = r + (shifted(r, rp_ref) - r) * mu_rkv_ref[0:1, :]
    k = k + (shifted(k, kp_ref) - k) * mu_rkv_ref[1:2, :]
    v = v + (shifted(v, vp_ref) - v) * mu_rkv_ref[2:3, :]
    kk = k * kk_ref[...]
    k = k * (1.0 + (ag - 1.0) * ka_ref[...])
    for hd in range(RWKV_HEADS):
        sl = slice(hd * RWKV_N, (hd + 1) * RWKV_N)
        ro_ref[0, hd] = r[:, sl]
        ko_ref[0, hd] = k[:, sl]
        vo_ref[0, hd] = v[:, sl]
        lwo_ref[0, hd] = lw[:, sl]
        cumo_ref[0, hd] = cum[:, sl]
        kko_ref[0, hd] = kk[:, sl]
        ago_ref[0, hd] = ag[:, sl]


def _rwkv_prep(h, rkv, mu_rkv, mu_wag, w0, w1, w2, a0, a1, a2, g1, g2, k_k, k_a, tm=256):
    b, t, d = h.shape
    tm = min(tm, t)
    c = RWKV_W
    spt = tm // SUBLANES
    prevblk = lambda i: jnp.maximum(i * spt - 1, 0)
    full = lambda arr: pl.BlockSpec(arr.shape, lambda bi, i: (0,) * arr.ndim)
    row = lambda arr: arr.reshape(1, -1)
    w0, a0, k_k, k_a = row(w0), row(a0), row(k_k), row(k_a)
    in_specs = [
        pl.BlockSpec((1, tm, d), lambda bi, i: (bi, i, 0)),
        pl.BlockSpec((1, SUBLANES, d), lambda bi, i: (bi, prevblk(i), 0)),
        pl.BlockSpec((1, tm, c), lambda bi, i: (bi, i, 0)),
        pl.BlockSpec((1, tm, c), lambda bi, i: (bi, i, 1)),
        pl.BlockSpec((1, tm, c), lambda bi, i: (bi, i, 2)),
        pl.BlockSpec((1, SUBLANES, c), lambda bi, i: (bi, prevblk(i), 0)),
        pl.BlockSpec((1, SUBLANES, c), lambda bi, i: (bi, prevblk(i), 1)),
        pl.BlockSpec((1, SUBLANES, c), lambda bi, i: (bi, prevblk(i), 2)),
    ] + [full(a) for a in (mu_rkv, mu_wag, w0, w1, w2, a0, a1, a2, g1, g2, k_k, k_a)]
    hm = jax.ShapeDtypeStruct((b, RWKV_HEADS, t, RWKV_N), F32)
    hm_spec = pl.BlockSpec((1, RWKV_HEADS, tm, RWKV_N), lambda bi, i: (bi, 0, i, 0))
    return pl.pallas_call(
        _rwkv_prep_kernel,
        grid=(b, t // tm),
        in_specs=in_specs,
        out_specs=[hm_spec] * 7 + [pl.BlockSpec((1, tm, c), lambda bi, i: (bi, i, 0))],
        out_shape=[hm] * 7 + [jax.ShapeDtypeStruct((b, t, c), F32)],
        compiler_params=_cparams(("parallel", "arbitrary")),
        name="rwkv_prep",
    )(h, h, rkv, rkv, rkv, rkv, rkv, rkv, mu_rkv, mu_wag, w0, w1, w2, a0, a1, a2, g1, g2, k_k, k_a)


def _split3(x):
    hi = x.astype(BF16)
    r1 = x - hi.astype(F32)
    mid = r1.astype(BF16)
    lo = (r1 - mid.astype(F32)).astype(BF16)
    return hi, mid, lo


def _bmm(a, b):
    return jnp.einsum("gmk,gkn->gmn", a.astype(BF16), b.astype(BF16), preferred_element_type=F32)


def _bmm_nt(a, b):
    return jnp.einsum("gmk,gnk->gmn", a.astype(BF16), b.astype(BF16), preferred_element_type=F32)


def _wkv_kernel(r_ref, k_ref, v_ref, lw_ref, cum_ref, kk_ref, ag_ref, rk_ref, lnw_ref, lnb_ref, y_ref,
                s_ref, st_ref):
    c = WKV_CHUNK
    n = RWKV_N
    hb, tc = r_ref.shape[1], r_ref.shape[2]
    nc = tc // c
    g = hb * nc

    @pl.when(pl.program_id(2) == 0)
    def _():
        s_ref[...] = jnp.zeros_like(s_ref)

    chunks = lambda ref: ref[0].reshape(g, c, n)
    r, k, v, lw, cum, kk, ag = (chunks(ref) for ref in (r_ref, k_ref, v_ref, lw_ref, cum_ref, kk_ref, ag_ref))
    kk = kk / jnp.maximum(jnp.sqrt(jnp.sum(kk * kk, axis=-1, keepdims=True)), 1e-12)
    dinc = jnp.exp(cum)
    dinv = jnp.exp(-cum)
    rt = r * dinc
    kt = k * dinv
    at = -kk * jnp.exp(cum - lw)
    bt = kk * ag * dinv
    d_chunk = dinc[:, c - 1:c, :]

    row = lax.broadcasted_iota(I32, (1, c, c), 1)
    col = lax.broadcasted_iota(I32, (1, c, c), 2)
    strict = row > col
    incl = row >= col
    eye = jnp.broadcast_to((row == col).astype(F32), (g, c, c))
    p = _bmm_nt(jnp.concatenate([at, rt], axis=1), jnp.concatenate([bt, kt], axis=1))
    a_ab = jnp.where(strict, p[:, :c, :c], 0.0)
    a_ak = jnp.where(strict, p[:, :c, c:], 0.0)
    a_rb = jnp.where(incl, p[:, c:, :c], 0.0)
    a_rk = jnp.where(incl, p[:, c:, c:], 0.0)
    x = _bmm(a_ab, a_ab)
    tinv = eye + a_ab
    levels = c.bit_length() - 2
    for j in range(1, levels):
        both = _bmm(jnp.concatenate([x, tinv], axis=1), x)
        x = both[:, :c]
        tinv = tinv + both[:, c:]
    tinv = tinv + _bmm(tinv, x)
    z = _bmm(a_ak, v)
    ta = _bmm(tinv, jnp.concatenate([at, z], axis=2))
    ry = _bmm(a_rb, ta)
    rp = rt + ry[:, :, :n]
    yv = ry[:, :, n:] + _bmm(a_rk, v)
    vt = _bmm_nt(eye, v)
    att = _bmm_nt(eye, at)
    tt = _bmm_nt(jnp.concatenate([att, _bmm_nt(vt, a_ak)], axis=1), tinv)
    mn = _bmm(tt, bt)
    moff = mn[:, :n].reshape(hb, nc, n, n)
    n2 = (mn[:, n:] + _bmm(vt, kt)).reshape(hb, nc, n, n)
    dch = d_chunk.reshape(hb, nc, 1, n)

    s = s_ref[...]
    for ci in range(nc):
        st_ref[:, ci] = s
        s = (s + _bmm(s, moff[:, ci]) + n2[:, ci]) * dch[:, ci]
    s_ref[...] = s

    y = _bmm_nt(rp, st_ref[...].reshape(g, n, n)) + yv
    mu = jnp.mean(y, axis=-1, keepdims=True)
    yc = y - mu
    var = jnp.mean(yc * yc, axis=-1, keepdims=True)
    y = (yc * lax.rsqrt(var + RWKV_GN_EPS)).reshape(hb, tc, n) * lnw_ref[...] + lnb_ref[...]
    bonus = jnp.sum((r * k).reshape(hb, tc, n) * rk_ref[...], axis=-1, keepdims=True)
    y = y + bonus * v.reshape(hb, tc, n)
    y_ref[0] = jnp.concatenate([y[hd] for hd in range(hb)], axis=1)


def _wkv_scan(r, k, v, lw, cum, kk, ag, r_k, lnx_w, lnx_b):
    b, hh, t, n = r.shape
    tc = min(WKV_CHUNK * WKV_CHUNKS_PER_STEP, t)
    hb = WKV_HEADS_PER_STEP
    blk = pl.BlockSpec((1, hb, tc, n), lambda bi, hi, i: (bi, hi, i, 0))
    par = pl.BlockSpec((hb, 1, n), lambda bi, hi, i: (hi, 0, 0))
    per_head = lambda arr: arr.reshape(hh, 1, n)
    return pl.pallas_call(
        _wkv_kernel,
        grid=(b, hh // hb, t // tc),
        in_specs=[blk] * 7 + [par] * 3,
        out_specs=pl.BlockSpec((1, tc, hb * n), lambda bi, hi, i: (bi, i, hi)),
        out_shape=jax.ShapeDtypeStruct((b, t, hh * n), F32),
        scratch_shapes=[pltpu.VMEM((hb, n, n), F32), pltpu.VMEM((hb, tc // WKV_CHUNK, n, n), F32)],
        compiler_params=_cparams(("parallel", "parallel", "arbitrary")),
        name="wkv7_scan",
    )(r, k, v, lw, cum, kk, ag, per_head(r_k), per_head(lnx_w), per_head(lnx_b))


def _merge_kernel(ya_ref, yr_ref, g_ref, ga_ref, gr_ref, pa_ref, pr_ref, o_ref):
    ma = _dot(ya_ref[...], pa_ref[...])
    mr = _dot(yr_ref[...] * g_ref[...], pr_ref[...])
    o_ref[...] = (_sigmoid(ga_ref[...]) * ma + _sigmoid(gr_ref[...]) * mr).astype(o_ref.dtype)


def _merge(ya, yr, g, gates, p_attn, p_rwkv, tm=256):
    m, c = ya.shape
    d = p_attn.shape[1]
    tm = min(tm, m)
    tile = pl.BlockSpec((tm, c), lambda i: (i, 0))
    return pl.pallas_call(
        _merge_kernel,
        grid=(m // tm,),
        in_specs=[tile, tile, tile,
                  pl.BlockSpec((tm, d), lambda i: (i, 0)),
                  pl.BlockSpec((tm, d), lambda i: (i, 1)),
                  pl.BlockSpec((c, d), lambda i: (0, 0)),
                  pl.BlockSpec((c, d), lambda i: (0, 0))],
        out_specs=pl.BlockSpec((tm, d), lambda i: (i, 0)),
        out_shape=jax.ShapeDtypeStruct((m, d), BF16),
        compiler_params=_cparams(("parallel",)),
        name="gated_merge",
    )(ya, yr, g, gates, gates, p_attn, p_rwkv)


def _outproj_router_kernel(alpha, mg_ref, x_ref, wo_ref, g_ref, b_ref, wr_ref, br_ref,
                           h_ref, hrow_ref, eid_ref, gate_ref):
    mix = jnp.dot(mg_ref[...], wo_ref[...], preferred_element_type=F32)
    h = _layer_norm(alpha * x_ref[...] + mix, g_ref[...], b_ref[...])
    h_ref[...] = h
    slab = h.shape[1] // LANES
    for j in range(slab):
        hrow_ref[pl.ds(j, h.shape[0], stride=slab), :] = h[:, j * LANES:(j + 1) * LANES]
    h_hi = h.astype(BF16)
    h_lo = (h - h_hi.astype(F32)).astype(BF16)
    logits = (jnp.dot(h_hi, wr_ref[0], preferred_element_type=F32)
              + jnp.dot(h_lo, wr_ref[0], preferred_element_type=F32)
              + jnp.dot(h_hi, wr_ref[1], preferred_element_type=F32)) + br_ref[...]
    lane = lax.broadcasted_iota(I32, logits.shape, 1)
    ninf = -jnp.inf
    big = jnp.int32(2 * LANES)
    glog = jnp.where(lane < N_GROUPS, logits, ninf)
    gmax = jnp.max(glog, axis=-1, keepdims=True)
    gidx = jnp.min(jnp.where(glog == gmax, lane, big), axis=-1, keepdims=True)
    gtop = 1.0 / jnp.sum(jnp.exp(glog - gmax), axis=-1, keepdims=True)
    eg = (lane - N_GROUPS) // EXPERTS_PER_GROUP
    in_group = (lane >= N_GROUPS) & (lane < N_GROUPS + N_EXPERTS) & (eg == gidx)
    el = jnp.where(in_group, logits, ninf)
    m1 = jnp.max(el, axis=-1, keepdims=True)
    i1 = jnp.min(jnp.where(el == m1, lane, big), axis=-1, keepdims=True)
    el2 = jnp.where(lane == i1, ninf, el)
    m2 = jnp.max(el2, axis=-1, keepdims=True)
    i2 = jnp.min(jnp.where(el2 == m2, lane, big), axis=-1, keepdims=True)
    t = jnp.exp(m2 - m1)
    p1 = 1.0 / (1.0 + t)
    p2 = t / (1.0 + t)
    eid_ref[...] = jnp.where(lane == 0, i1 - N_GROUPS, jnp.where(lane == 1, i2 - N_GROUPS, 0))
    gate_ref[...] = jnp.where(lane == 0, gtop * p1, jnp.where(lane == 1, gtop * p2, 0.0))


def _outproj_router(merged, x, w_o, ln_g, ln_b, w_router, b_router, alpha, tm=256):
    m, d = x.shape
    tm = min(tm, m)
    tile = pl.BlockSpec((tm, d), lambda i: (i, 0))
    vec = pl.BlockSpec((1, d), lambda i: (0, 0))
    small = pl.BlockSpec((tm, LANES), lambda i: (i, 0))
    return pl.pallas_call(
        functools.partial(_outproj_router_kernel, alpha),
        grid=(m // tm,),
        in_specs=[tile, tile, pl.BlockSpec((d, d), lambda i: (0, 0)), vec, vec,
                  pl.BlockSpec((2, d, LANES), lambda i: (0, 0, 0)), pl.BlockSpec((1, LANES), lambda i: (0, 0))],
        out_specs=[tile, pl.BlockSpec((tm * (d // LANES), LANES), lambda i: (i, 0)), small, small],
        out_shape=[jax.ShapeDtypeStruct((m, d), F32), jax.ShapeDtypeStruct((m * (d // LANES), LANES), F32),
                   jax.ShapeDtypeStruct((m, LANES), I32), jax.ShapeDtypeStruct((m, LANES), F32)],
        compiler_params=_cparams(("parallel",)),
        name="outproj_ln_router",
    )(merged, x, w_o, ln_g.reshape(1, d), ln_b.reshape(1, d), w_router, b_router)


def _onehots(eid_ref):
    tm = eid_ref.shape[1]
    e_iota = lax.broadcasted_iota(I32, (N_EXPERTS, tm), 0)
    oh0 = (eid_ref[0:1, :] == e_iota).astype(F32)
    oh1 = (eid_ref[1:2, :] == e_iota).astype(F32)
    return oh0, oh1


def _count_kernel(eid_ref, cnt_ref):
    @pl.when(pl.program_id(0) == 0)
    def _():
        cnt_ref[...] = jnp.zeros_like(cnt_ref)

    oh0, oh1 = _onehots(eid_ref)
    cnt_ref[...] += jnp.sum(oh0 + oh1, axis=1, keepdims=True)


def _slot_kernel(eid_ref, pstart_ref, dest_ref, run_ref):
    @pl.when(pl.program_id(0) == 0)
    def _():
        run_ref[...] = jnp.zeros_like(run_ref)

    tm = eid_ref.shape[1]
    oh0, oh1 = _onehots(eid_ref)
    both = oh0 + oh1
    earlier = (lax.broadcasted_iota(I32, (tm, tm), 0) < lax.broadcasted_iota(I32, (tm, tm), 1)).astype(BF16)
    pre = jnp.dot(both.astype(BF16), earlier, preferred_element_type=F32)
    base = pre + run_ref[...] + pstart_ref[...]
    dest_ref[0:1, :] = jnp.sum(oh0 * base, axis=0, keepdims=True).astype(I32)
    dest_ref[1:2, :] = jnp.sum(oh1 * base, axis=0, keepdims=True).astype(I32)
    run_ref[...] += jnp.sum(both, axis=1, keepdims=True)


def _expert_counts(eid_t):
    m = eid_t.shape[1]
    tm = min(ROUTE_TILE, m)
    return pl.pallas_call(
        _count_kernel,
        grid=(m // tm,),
        in_specs=[pl.BlockSpec((TOP_K, tm), lambda i: (0, i))],
        out_specs=pl.BlockSpec((N_EXPERTS, 1), lambda i: (0, 0)),
        out_shape=jax.ShapeDtypeStruct((N_EXPERTS, 1), F32),
        compiler_params=_cparams(("arbitrary",)),
        name="expert_counts",
    )(eid_t)


def _expert_slots(eid_t, pstart):
    m = eid_t.shape[1]
    tm = min(ROUTE_TILE, m)
    return pl.pallas_call(
        _slot_kernel,
        grid=(m // tm,),
        in_specs=[pl.BlockSpec((TOP_K, tm), lambda i: (0, i)),
                  pl.BlockSpec((N_EXPERTS, 1), lambda i: (0, 0))],
        out_specs=pl.BlockSpec((TOP_K, tm), lambda i: (0, i)),
        out_shape=jax.ShapeDtypeStruct((TOP_K, m), I32),
        scratch_shapes=[pltpu.VMEM((N_EXPERTS, 1), F32)],
        compiler_params=_cparams(("arbitrary",)),
        name="expert_slots",
    )(eid_t, pstart)


def _row_slab(ref, row, slab):
    return ref.at[pl.ds(pl.multiple_of(row * slab, slab), slab), :]


def _dispatch_kernel(dest_ref, h_ref, xb_in_ref, xb_ref, sem):
    del xb_in_ref
    tm = DISPATCH_TILE
    slab = h_ref.shape[0] // tm
    t0 = pl.program_id(0) * tm

    def copy(n, k):
        return pltpu.make_async_copy(_row_slab(h_ref, n, slab),
                                     _row_slab(xb_ref, dest_ref[TOP_K * (t0 + n) + k], slab), sem)

    def start(n, carry):
        for k in range(TOP_K):
            copy(n, k).start()
        return carry

    lax.fori_loop(0, tm, start, 0, unroll=8)
    for k in range(TOP_K):
        pltpu.make_async_copy(h_ref, xb_ref.at[pl.ds(0, h_ref.shape[0]), :], sem).wait()


DISPATCH_TILE = 512


def _dispatch(dest_flat, h_rows, m, n_rows):
    slab = h_rows.shape[0] // m
    assert m % DISPATCH_TILE == 0
    grid_spec = pltpu.PrefetchScalarGridSpec(
        num_scalar_prefetch=1,
        grid=(m // DISPATCH_TILE,),
        in_specs=[pl.BlockSpec((DISPATCH_TILE * slab, LANES), lambda i, dr: (i, 0)),
                  pl.BlockSpec(memory_space=pl.ANY)],
        out_specs=pl.BlockSpec(memory_space=pl.ANY),
        scratch_shapes=[pltpu.SemaphoreType.DMA(())],
    )
    return pl.pallas_call(
        _dispatch_kernel,
        grid_spec=grid_spec,
        out_shape=jax.ShapeDtypeStruct((n_rows * slab, LANES), F32),
        input_output_aliases={2: 0},
        compiler_params=pltpu.CompilerParams(dimension_semantics=("arbitrary",), has_side_effects=True),
        name="moe_dispatch",
    )(dest_flat, h_rows, jnp.zeros((n_rows * slab, LANES), F32))


def _expert_kernel(item_e_ref, item_row_ref, item_nsub_ref, wg_ref, wu_ref, wd_ref, xb_ref, yb_ref,
                   xin_ref, yout_ref, x_ref, acc_ref, wgu_ref, wdb_ref, sem_in, sem_out):
    del item_e_ref
    it = pl.program_id(0)
    f = pl.program_id(1)
    n_items = pl.num_programs(0)
    nf = pl.num_programs(1)
    kc = wg_ref.shape[1]
    ff = wg_ref.shape[2]
    fc = wd_ref.shape[1]
    nsub = item_nsub_ref[it]
    row0 = item_row_ref[it]
    nxt = jnp.minimum(it + 1, n_items - 1)
    nsub_next = jnp.where(it + 1 < n_items, item_nsub_ref[nxt], 0)
    prv = jnp.maximum(it - 1, 0)
    nsub_prev = jnp.where(it > 0, item_nsub_ref[prv], 0)
    rb = MOE_ROWS
    slab = x_ref.shape[0] * kc // LANES
    blk = rb * slab

    def stage_rows(ref, s):
        return ref.at[pl.ds(pl.multiple_of(s * blk, blk), blk), :]

    def hbm_rows(ref, item_row, s):
        return ref.at[pl.ds(pl.multiple_of((item_row + s * rb) * slab, blk), blk), :]

    def in_copy(item_row, s):
        return pltpu.make_async_copy(hbm_rows(xb_ref, item_row, s), stage_rows(xin_ref, s), sem_in)

    def out_copy(item_row, s):
        return pltpu.make_async_copy(stage_rows(yout_ref, s), hbm_rows(yb_ref, item_row, s), sem_out)

    def loop(n, fn):
        def body(s, carry):
            fn(s)
            return carry
        lax.fori_loop(0, n, body, 0)

    @pl.when(nsub > 0)
    def _():
        @pl.when(f == 0)
        def _():
            @pl.when(it == 0)
            def _():
                loop(nsub, lambda s: in_copy(row0, s).start())
            loop(nsub, lambda s: in_copy(row0, s).wait())

            def to_matrix(s):
                rows = pl.ds(pl.multiple_of(s * rb, rb), rb)
                base = pl.multiple_of(s * blk, blk)
                for j in range(slab):
                    c0 = (j * LANES) % kc
                    x_ref[(j * LANES) // kc, rows, c0:c0 + LANES] = (
                        xin_ref[pl.ds(base + j, rb, stride=slab), :].astype(BF16))
            loop(nsub, to_matrix)
            loop(nsub_next, lambda s: in_copy(item_row_ref[nxt], s).start())

        wgu_ref[:, :ff] = wg_ref[0].astype(BF16)
        wgu_ref[:, ff:] = wu_ref[0].astype(BF16)
        wdb_ref[pl.ds(pl.multiple_of(f * fc, 2 * SUBLANES), fc), :] = wd_ref[0].astype(BF16)

        def gate_up(start, size):
            rows = pl.ds(pl.multiple_of(start, rb), size)
            part = jnp.dot(x_ref[f, rows, :], wgu_ref[...], preferred_element_type=F32)
            acc_ref[rows, :] = jnp.where(f > 0, acc_ref[rows, :], 0.0) + part

        def pair(p, carry):
            gate_up(p * (2 * rb), 2 * rb)
            return carry
        lax.fori_loop(0, nsub // 2, pair, 0)

        @pl.when(nsub % 2 == 1)
        def _():
            gate_up((nsub - 1) * rb, rb)

        @pl.when(f == nf - 1)
        def _():
            loop(nsub_prev, lambda s: out_copy(item_row_ref[prv], s).wait())

            def down(start, size):
                rows = pl.ds(pl.multiple_of(start, rb), size)
                gate = acc_ref[rows, :ff]
                act = (gate * _sigmoid(gate) * acc_ref[rows, ff:]).astype(BF16)
                y = jnp.dot(act, wdb_ref[...], preferred_element_type=F32)
                base = pl.multiple_of(start * slab, blk)
                for j in range(slab):
                    yout_ref[pl.ds(base + j, size, stride=slab), :] = y[:, j * LANES:(j + 1) * LANES]

            def down_pair(p, carry):
                down(p * (2 * rb), 2 * rb)
                return carry
            lax.fori_loop(0, nsub // 2, down_pair, 0)

            @pl.when(nsub % 2 == 1)
            def _():
                down((nsub - 1) * rb, rb)

            loop(nsub, lambda s: out_copy(row0, s).start())

            @pl.when(nsub_next == 0)
            def _():
                loop(nsub, lambda s: out_copy(row0, s).wait())


def _experts(item_e, item_row, item_nsub, w_gate, w_up, w_down, xb):
    d = w_gate.shape[1]
    ff = w_gate.shape[2]
    nk = MOE_K_STEPS
    kc = d // nk
    fc = ff // nk
    assert d % nk == 0 and ff % nk == 0 and kc % LANES == 0 and fc % (2 * SUBLANES) == 0 and ff % LANES == 0
    n_items = item_e.shape[0]
    rows_max = MOE_ROWS * MOE_SUBS_PER_ITEM
    slab = d // LANES

    def k_eff(k, ns, it):
        return jnp.where(ns[it] > 0, k, nk - 1)

    grid_spec = pltpu.PrefetchScalarGridSpec(
        num_scalar_prefetch=3,
        grid=(n_items, nk),
        in_specs=[
            pl.BlockSpec((1, kc, ff), lambda it, k, ie, ir, ns: (ie[it], k_eff(k, ns, it), 0)),
            pl.BlockSpec((1, kc, ff), lambda it, k, ie, ir, ns: (ie[it], k_eff(k, ns, it), 0)),
            pl.BlockSpec((1, fc, d), lambda it, k, ie, ir, ns: (ie[it], k_eff(k, ns, it), 0)),
            pl.BlockSpec(memory_space=pl.ANY),
        ],
        out_specs=pl.BlockSpec(memory_space=pl.ANY),
        scratch_shapes=[
            pltpu.VMEM((rows_max * slab, LANES), F32),
            pltpu.VMEM((rows_max * slab, LANES), F32),
            pltpu.VMEM((nk, rows_max, kc), BF16),
            pltpu.VMEM((rows_max, 2 * ff), F32),
            pltpu.VMEM((kc, 2 * ff), BF16),
            pltpu.VMEM((ff, d), BF16),
            pltpu.SemaphoreType.DMA(()),
            pltpu.SemaphoreType.DMA(()),
        ],
    )
    return pl.pallas_call(
        _expert_kernel,
        grid_spec=grid_spec,
        out_shape=jax.ShapeDtypeStruct(xb.shape, F32),
        input_output_aliases={6: 0},
        compiler_params=pltpu.CompilerParams(dimension_semantics=("arbitrary", "arbitrary"),
                                             vmem_limit_bytes=VMEM_LIMIT_BYTES, has_side_effects=True),
        name="moe_experts",
    )(item_e, item_row, item_nsub, w_gate, w_up, w_down, xb)


COMBINE_TILE = 256


def _combine_kernel(alpha, dest_ref, yb_ref, h_ref, gate_ref, g_ref, b_ref, o_ref, buf_ref, sem):
    tm = COMBINE_TILE
    t0 = pl.program_id(0) * tm

    slab = buf_ref.shape[1] // tm

    def copy(n, k):
        return pltpu.make_async_copy(_row_slab(yb_ref, dest_ref[TOP_K * (t0 + n) + k], slab),
                                     _row_slab(buf_ref.at[k], n, slab), sem)

    def start(n, carry):
        for k in range(TOP_K):
            copy(n, k).start()
        return carry

    lax.fori_loop(0, tm, start, 0, unroll=8)
    for k in range(TOP_K):
        pltpu.make_async_copy(yb_ref.at[pl.ds(0, buf_ref.shape[1]), :], buf_ref.at[k], sem).wait()
    gate = gate_ref[...]
    g0 = gate[:, 0:1]
    g1 = gate[:, 1:2]
    col = lambda k, j: buf_ref[k, pl.ds(j, tm, stride=slab), :]
    ff = jnp.concatenate([g0 * col(0, j) + g1 * col(1, j) for j in range(slab)], axis=1)
    o_ref[...] = _layer_norm(alpha * h_ref[...] + ff, g_ref[...], b_ref[...])


def _combine(dest_flat, yb, h, gates, ln_g, ln_b, alpha):
    m, d = h.shape
    tm = COMBINE_TILE
    assert m % tm == 0
    grid_spec = pltpu.PrefetchScalarGridSpec(
        num_scalar_prefetch=1,
        grid=(m // tm,),
        in_specs=[pl.BlockSpec(memory_space=pl.ANY),
                  pl.BlockSpec((tm, d), lambda i, dr: (i, 0)),
                  pl.BlockSpec((tm, LANES), lambda i, dr: (i, 0)),
                  pl.BlockSpec((1, d), lambda i, dr: (0, 0)),
                  pl.BlockSpec((1, d), lambda i, dr: (0, 0))],
        out_specs=pl.BlockSpec((tm, d), lambda i, dr: (i, 0)),
        scratch_shapes=[pltpu.VMEM((TOP_K, tm * (d // LANES), LANES), F32), pltpu.SemaphoreType.DMA(())],
    )
    return pl.pallas_call(
        functools.partial(_combine_kernel, alpha),
        grid_spec=grid_spec,
        out_shape=jax.ShapeDtypeStruct((m, d), F32),
        compiler_params=_cparams(("arbitrary",)),
        name="moe_combine_ln",
    )(dest_flat, yb, h, gates, ln_g.reshape(1, d), ln_b.reshape(1, d))


def _moe_tables(counts):
    nsub_e = (counts + MOE_ROWS - 1) // MOE_ROWS
    pstart = (jnp.cumsum(nsub_e) - nsub_e) * MOE_ROWS
    nitem_e = (nsub_e + MOE_SUBS_PER_ITEM - 1) // MOE_SUBS_PER_ITEM
    item_end = jnp.cumsum(nitem_e)
    return nsub_e, pstart, nitem_e, item_end


def _moe(h, h_rows, eid, gates, w_gate, w_up, w_down, ln_g, ln_b, alpha):
    m, d = h.shape
    n_assign = m * TOP_K
    n_blocks = (n_assign + N_EXPERTS * (MOE_ROWS - 1) + MOE_ROWS - 1) // MOE_ROWS
    n_rows = n_blocks * MOE_ROWS
    n_items = N_EXPERTS + n_assign // (MOE_ROWS * MOE_SUBS_PER_ITEM)

    eid_t = eid[:, :TOP_K].T
    counts = _expert_counts(eid_t)[:, 0].astype(I32)
    nsub_e, pstart, nitem_e, item_end = _moe_tables(counts)
    dest_t = _expert_slots(eid_t, pstart.astype(F32).reshape(N_EXPERTS, 1))
    dest_flat = dest_t.T.reshape(-1)

    it = jnp.arange(n_items, dtype=I32)
    total_items = item_end[-1]
    item_e = jnp.minimum(jnp.sum(item_end[None, :] <= it[:, None], axis=1).astype(I32), N_EXPERTS - 1)
    j = it - (item_end - nitem_e)[item_e]
    used = it < total_items
    last_e = item_e[jnp.maximum(total_items - 1, 0)]
    item_nsub = jnp.where(used, jnp.clip(nsub_e[item_e] - j * MOE_SUBS_PER_ITEM, 0, MOE_SUBS_PER_ITEM), 0).astype(I32)
    item_row = jnp.where(used, pstart[item_e] + j * (MOE_ROWS * MOE_SUBS_PER_ITEM), 0).astype(I32)
    item_e = jnp.where(used, item_e, last_e).astype(I32)

    xb = _dispatch(dest_flat, h_rows, m, n_rows)
    yb = _experts(item_e, item_row, item_nsub, w_gate, w_up, w_down, xb)
    return _combine(dest_flat, yb, h, gates, ln_g, ln_b, alpha)


def _pad_cols(w, n):
    return jnp.pad(w, ((0, 0), (0, n - w.shape[1])))


def _pad_rows(w, n):
    return jnp.pad(w, ((0, n - w.shape[0]), (0, 0)))


def kernel(x, w_in, attn_sinks, rw_mu_rkv, rw_mu_wag, rw_w0, rw_w1, rw_w2, rw_a0, rw_a1, rw_a2, rw_g1, rw_g2, rw_k_k, rw_k_a, rw_r_k, rw_lnx_w, rw_lnx_b, p_attn, p_rwkv, w_o, ln1_g, ln1_b, w_group, b_group, w_expert, b_expert, w_gate, w_up, w_down, ln2_g, ln2_b):
    b, t, d = x.shape
    depth = w_in.shape[0]
    m = b * t
    alpha = (2.0 * depth) ** 0.25
    cosb, sinb = _rope_tables(t)
    qkv_w = ATTN_Q_W + 2 * ATTN_KV_W
    rkv_w = 3 * RWKV_W
    h = x
    for l in range(depth):
        hf = h.reshape(m, d)
        hb = hf.astype(BF16)
        w_in_b = w_in[l].astype(BF16)
        qkv = _matmul_cols(hb, w_in_b, 0, qkv_w, F32)
        rkv = _matmul_cols(hb, w_in_b, qkv_w, rkv_w, F32)
        gates = _matmul_cols(hb, w_in_b, qkv_w + rkv_w, 2 * d, F32)

        y_a = _attention(qkv.reshape(b, t, qkv_w), attn_sinks[l], cosb, sinb)

        lora = lambda w, n: _pad_cols(w, n).astype(BF16)
        lorb = lambda w, n: _pad_rows(w, n).astype(BF16)
        n_w = -(-rw_w1.shape[2] // LANES) * LANES
        n_a = -(-rw_a1.shape[2] // LANES) * LANES
        n_g = -(-rw_g1.shape[2] // LANES) * LANES
        r_, k_, v_, lw_, cum_, kk_, ag_, g_ = _rwkv_prep(
            h, rkv.reshape(b, t, rkv_w), rw_mu_rkv[l], rw_mu_wag[l], rw_w0[l],
            lora(rw_w1[l], n_w), lorb(rw_w2[l], n_w), rw_a0[l], lora(rw_a1[l], n_a), lorb(rw_a2[l], n_a),
            lora(rw_g1[l], n_g), lorb(rw_g2[l], n_g), rw_k_k[l], rw_k_a[l])
        y_r = _wkv_scan(r_, k_, v_, lw_, cum_, kk_, ag_, rw_r_k[l], rw_lnx_w[l], rw_lnx_b[l])

        merged = _merge(y_a.reshape(m, ATTN_Q_W), y_r.reshape(m, RWKV_W), g_.reshape(m, RWKV_W), gates,
                        p_attn[l].astype(BF16), p_rwkv[l].astype(BF16))
        w_router = _pad_cols(jnp.concatenate([w_group[l], w_expert[l]], axis=1), LANES)
        w_router_hi = w_router.astype(BF16)
        w_router = jnp.stack([w_router_hi, (w_router - w_router_hi.astype(F32)).astype(BF16)])
        b_router = _pad_cols(jnp.concatenate([b_group[l], b_expert[l]])[None, :], LANES)
        h1, h1_rows, eid, gate = _outproj_router(merged, hf, w_o[l].astype(BF16), ln1_g[l], ln1_b[l],
                                        w_router, b_router, alpha)
        h2 = _moe(h1, h1_rows, eid, gate, w_gate[l], w_up[l], w_down[l], ln2_g[l], ln2_b[l], alpha)
        h = h2.reshape(b, t, d)
    return h
```

```python
import functools

import jax
import jax.numpy as jnp
from jax import lax
from jax.experimental import pallas as pl
from jax.experimental.pallas import tpu as pltpu

F32 = jnp.float32
BF16 = jnp.bfloat16
I32 = jnp.int32

HEAD_DIM = 64
ATTN_Q_HEADS = 16
ATTN_KV_HEADS = 4
ATTN_GROUP = ATTN_Q_HEADS // ATTN_KV_HEADS
ATTN_Q_W = ATTN_Q_HEADS * HEAD_DIM
ATTN_KV_W = ATTN_KV_HEADS * HEAD_DIM
WINDOW = 128
ROPE_THETA = 10000.0
RWKV_HEADS = 16
RWKV_N = 64
RWKV_W = RWKV_HEADS * RWKV_N
RWKV_GN_EPS = 64e-5
N_GROUPS = 8
EXPERTS_PER_GROUP = 8
N_EXPERTS = N_GROUPS * EXPERTS_PER_GROUP
TOP_K = 2
LN_EPS = 1e-5

LANES = 128
SUBLANES = 8
VMEM_LIMIT_BYTES = 56 * 1024 * 1024

WKV_CHUNK = 64
WKV_CHUNKS_PER_STEP = 4
WKV_HEADS_PER_STEP = 8
MOE_ROWS = 128
MOE_SUBS_PER_ITEM = 4
MOE_K_STEPS = 4
MOE_GU_PIECES = 4
MOE_DOWN_PIECES = 2
ROUTE_TILE = 512


def _cparams(sem, vmem=VMEM_LIMIT_BYTES):
    return pltpu.CompilerParams(dimension_semantics=sem, vmem_limit_bytes=vmem)


def _sigmoid(x):
    return 1.0 / (1.0 + jnp.exp(-x))


def _dot(a, b):
    return jnp.dot(a.astype(BF16), b.astype(BF16), preferred_element_type=F32)


def _dot_nt(a, b):
    return lax.dot_general(a.astype(BF16), b.astype(BF16), (((1,), (1,)), ((), ())),
                           preferred_element_type=F32)


def _layer_norm(t, g, b):
    mu = jnp.mean(t, axis=-1, keepdims=True)
    d = t - mu
    var = jnp.mean(d * d, axis=-1, keepdims=True)
    return d * lax.rsqrt(var + LN_EPS) * g + b


def _matmul_kernel(a_ref, b_ref, o_ref):
    o_ref[...] = jnp.dot(a_ref[...], b_ref[...], preferred_element_type=F32).astype(o_ref.dtype)


def _matmul_cols(a, b, col0, ncols, out_dtype, tm=1024, tn=512):
    m, k = a.shape
    tm = min(tm, m)
    cb = col0 // tn
    assert col0 % tn == 0 and ncols % tn == 0 and m % tm == 0
    return pl.pallas_call(
        _matmul_kernel,
        grid=(m // tm, ncols // tn),
        in_specs=[pl.BlockSpec((tm, k), lambda i, j: (i, 0)),
                  pl.BlockSpec((k, tn), lambda i, j: (0, j + cb))],
        out_specs=pl.BlockSpec((tm, tn), lambda i, j: (i, j)),
        out_shape=jax.ShapeDtypeStruct((m, ncols), out_dtype),
        compiler_params=_cparams(("parallel", "arbitrary")),
        name="inproj_matmul",
    )(a, b)


def _rope(x, cosb, sinb):
    half = HEAD_DIM // 2
    lane = lax.broadcasted_iota(I32, cosb.shape, 1)
    first_half = (lane % HEAD_DIM) < half
    outs = []
    for g in range(x.shape[1] // LANES):
        xg = x[:, g * LANES:(g + 1) * LANES]
        partner = jnp.where(first_half, pltpu.roll(xg, LANES - half, axis=1), pltpu.roll(xg, half, axis=1))
        outs.append(xg * cosb + partner * sinb)
    return outs


def _attn_kernel(sinks_ref, q_ref, kc_ref, kp_ref, vc_ref, vp_ref, cosc_ref, sinc_ref, cosp_ref, sinp_ref, o_ref):
    blk = pl.program_id(1)
    tq = q_ref.shape[1]
    qg = _rope(q_ref[0], cosc_ref[...], sinc_ref[...])
    kcg = _rope(kc_ref[0], cosc_ref[...], sinc_ref[...])
    kpg = _rope(kp_ref[0], cosp_ref[...], sinp_ref[...])
    vc = vc_ref[0]
    vp = vp_ref[0]

    def head(groups, h):
        g = groups[h // 2]
        return g[:, (h % 2) * HEAD_DIM:(h % 2 + 1) * HEAD_DIM]

    rows = ATTN_GROUP * tq
    qi = lax.broadcasted_iota(I32, (rows, 2 * tq), 0) % tq
    kj = lax.broadcasted_iota(I32, (rows, 2 * tq), 1)
    dist = qi + tq - kj
    valid = (dist >= 0) & (dist < WINDOW) & ((blk > 0) | (kj >= tq))
    rid = lax.broadcasted_iota(I32, (rows, 1), 0) // tq
    scale = HEAD_DIM ** -0.5
    for kvh in range(ATTN_KV_HEADS):
        qh = jnp.concatenate([head(qg, kvh * ATTN_GROUP + g) for g in range(ATTN_GROUP)], axis=0)
        kw = jnp.concatenate([head(kpg, kvh), head(kcg, kvh)], axis=0)
        vw = jnp.concatenate([vp[:, kvh * HEAD_DIM:(kvh + 1) * HEAD_DIM],
                              vc[:, kvh * HEAD_DIM:(kvh + 1) * HEAD_DIM]], axis=0)
        s = _dot_nt(qh, kw) * scale
        s = jnp.where(valid, s, -jnp.inf)
        sink = jnp.zeros((rows, 1), F32)
        for g in range(ATTN_GROUP):
            sink = jnp.where(rid == g, sinks_ref[kvh * ATTN_GROUP + g], sink)
        m = jnp.maximum(jnp.max(s, axis=-1, keepdims=True), sink)
        e = jnp.exp(s - m)
        denom = jnp.sum(e, axis=-1, keepdims=True) + jnp.exp(sink - m)
        o = _dot(e, vw) / denom
        for g in range(ATTN_GROUP):
            hq = kvh * ATTN_GROUP + g
            o_ref[0, :, hq * HEAD_DIM:(hq + 1) * HEAD_DIM] = o[g * tq:(g + 1) * tq]


def _attention(qkv, sinks, cosb, sinb):
    b, t, _ = qkv.shape
    tq = WINDOW
    nb = t // tq
    kcol = ATTN_Q_W // ATTN_KV_W
    prev = lambda i: jnp.maximum(i - 1, 0)
    grid_spec = pltpu.PrefetchScalarGridSpec(
        num_scalar_prefetch=0,
        grid=(b, nb),
        in_specs=[
            pl.BlockSpec(memory_space=pltpu.SMEM),
            pl.BlockSpec((1, tq, ATTN_Q_W), lambda bi, i: (bi, i, 0)),
            pl.BlockSpec((1, tq, ATTN_KV_W), lambda bi, i: (bi, i, kcol)),
            pl.BlockSpec((1, tq, ATTN_KV_W), lambda bi, i: (bi, prev(i), kcol)),
            pl.BlockSpec((1, tq, ATTN_KV_W), lambda bi, i: (bi, i, kcol + 1)),
            pl.BlockSpec((1, tq, ATTN_KV_W), lambda bi, i: (bi, prev(i), kcol + 1)),
            pl.BlockSpec((tq, LANES), lambda bi, i: (i, 0)),
            pl.BlockSpec((tq, LANES), lambda bi, i: (i, 0)),
            pl.BlockSpec((tq, LANES), lambda bi, i: (prev(i), 0)),
            pl.BlockSpec((tq, LANES), lambda bi, i: (prev(i), 0)),
        ],
        out_specs=pl.BlockSpec((1, tq, ATTN_Q_W), lambda bi, i: (bi, i, 0)),
    )
    return pl.pallas_call(
        _attn_kernel,
        grid_spec=grid_spec,
        out_shape=jax.ShapeDtypeStruct((b, t, ATTN_Q_W), F32),
        compiler_params=_cparams(("parallel", "arbitrary")),
        name="swa_attention",
    )(sinks, qkv, qkv, qkv, qkv, qkv, cosb, sinb, cosb, sinb)


def _rope_tables(t):
    inv = 1.0 / (ROPE_THETA ** (jnp.arange(0, HEAD_DIM, 2, dtype=F32) / HEAD_DIM))
    ang = jnp.arange(t, dtype=F32)[:, None] * inv[None, :]
    cos, sin = jnp.cos(ang), jnp.sin(ang)
    reps = LANES // HEAD_DIM
    cosb = jnp.tile(jnp.concatenate([cos, cos], axis=-1), (1, reps))
    sinb = jnp.tile(jnp.concatenate([-sin, sin], axis=-1), (1, reps))
    return cosb, sinb


def _rwkv_prep_kernel(h_ref, hp_ref, r_ref, k_ref, v_ref, rp_ref, kp_ref, vp_ref,
                      mu_rkv_ref, mu_wag_ref, w0_ref, w1_ref, w2_ref, a0_ref, a1_ref, a2_ref,
                      g1_ref, g2_ref, kk_ref, ka_ref,
                      ro_ref, ko_ref, vo_ref, lwo_ref, cumo_ref, kko_ref, ago_ref, go_ref):
    first = pl.program_id(1) == 0
    last_row = SUBLANES - 1

    def shifted(cur, prev_ref):
        prev_row = jnp.where(first, 0.0, prev_ref[0, last_row:last_row + 1, :])
        rowid = lax.broadcasted_iota(I32, cur.shape, 0)
        return jnp.where(rowid == 0, prev_row, pltpu.roll(cur, 1, axis=0))

    h = h_ref[0]
    xx = shifted(h, hp_ref) - h
    xw = h + xx * mu_wag_ref[0:1, :]
    xa = h + xx * mu_wag_ref[1:2, :]
    xg = h + xx * mu_wag_ref[2:3, :]
    w_raw = w0_ref[...] + _dot(jnp.tanh(_dot(xw, w1_ref[...])), w2_ref[...])
    neg = -w_raw
    softplus = jnp.maximum(neg, 0.0) + jnp.log1p(jnp.exp(-jnp.abs(neg)))
    w = -softplus - 0.5
    lw = -jnp.exp(w)
    tm = lw.shape[0]
    row = lax.broadcasted_iota(I32, (tm, tm), 0)
    col = lax.broadcasted_iota(I32, (tm, tm), 1)
    tri = ((row >= col) & (row // WKV_CHUNK == col // WKV_CHUNK)).astype(BF16)
    cum = sum(jnp.dot(tri, piece, preferred_element_type=F32) for piece in _split3(lw))
    ag = _sigmoid(a0_ref[...] + _dot(_dot(xa, a1_ref[...]), a2_ref[...]))
    go_ref[0] = _dot(_sigmoid(_dot(xg, g1_ref[...])), g2_ref[...])

    r = r_ref[0]
    k = k_ref[0]
    v = v_ref[0]
    r = r + (shifted(r, rp_ref) - r) * mu_rkv_ref[0:1, :]
    k = k + (shifted(k, kp_ref) - k) * mu_rkv_ref[1:2, :]
    v = v + (shifted(v, vp_ref) - v) * mu_rkv_ref[2:3, :]
    kk = k * kk_ref[...]
    k = k * (1.0 + (ag - 1.0) * ka_ref[...])
    for hd in range(RWKV_HEADS):
        sl = slice(hd * RWKV_N, (hd + 1) * RWKV_N)
        ro_ref[0, hd] = r[:, sl]
        ko_ref[0, hd] = k[:, sl]
        vo_ref[0, hd] = v[:, sl]
        lwo_ref[0, hd] = lw[:, sl]
        cumo_ref[0, hd] = cum[:, sl]
        kko_ref[0, hd] = kk[:, sl]
        ago_ref[0, hd] = ag[:, sl]


def _rwkv_prep(h, rkv, mu_rkv, mu_wag, w0, w1, w2, a0, a1, a2, g1, g2, k_k, k_a, tm=256):
    b, t, d = h.shape
    tm = min(tm, t)
    c = RWKV_W
    spt = tm // SUBLANES
    prevblk = lambda i: jnp.maximum(i * spt - 1, 0)
    full = lambda arr: pl.BlockSpec(arr.shape, lambda bi, i: (0,) * arr.ndim)
    row = lambda arr: arr.reshape(1, -1)
    w0, a0, k_k, k_a = row(w0), row(a0), row(k_k), row(k_a)
    in_specs = [
        pl.BlockSpec((1, tm, d), lambda bi, i: (bi, i, 0)),
        pl.BlockSpec((1, SUBLANES, d), lambda bi, i: (bi, prevblk(i), 0)),
        pl.BlockSpec((1, tm, c), lambda bi, i: (bi, i, 0)),
        pl.BlockSpec((1, tm, c), lambda bi, i: (bi, i, 1)),
        pl.BlockSpec((1, tm, c), lambda bi, i: (bi, i, 2)),
        pl.BlockSpec((1, SUBLANES, c), lambda bi, i: (bi, prevblk(i), 0)),
        pl.BlockSpec((1, SUBLANES, c), lambda bi, i: (bi, prevblk(i), 1)),
        pl.BlockSpec((1, SUBLANES, c), lambda bi, i: (bi, prevblk(i), 2)),
    ] + [full(a) for a in (mu_rkv, mu_wag, w0, w1, w2, a0, a1, a2, g1, g2, k_k, k_a)]
    hm = jax.ShapeDtypeStruct((b, RWKV_HEADS, t, RWKV_N), F32)
    hm_spec = pl.BlockSpec((1, RWKV_HEADS, tm, RWKV_N), lambda bi, i: (bi, 0, i, 0))
    return pl.pallas_call(
        _rwkv_prep_kernel,
        grid=(b, t // tm),
        in_specs=in_specs,
        out_specs=[hm_spec] * 7 + [pl.BlockSpec((1, tm, c), lambda bi, i: (bi, i, 0))],
        out_shape=[hm] * 7 + [jax.ShapeDtypeStruct((b, t, c), F32)],
        compiler_params=_cparams(("parallel", "arbitrary")),
        name="rwkv_prep",
    )(h, h, rkv, rkv, rkv, rkv, rkv, rkv, mu_rkv, mu_wag, w0, w1, w2, a0, a1, a2, g1, g2, k_k, k_a)


def _split3(x):
    hi = x.astype(BF16)
    r1 = x - hi.astype(F32)
    mid = r1.astype(BF16)
    lo = (r1 - mid.astype(F32)).astype(BF16)
    return hi, mid, lo


def _bmm(a, b):
    return jnp.einsum("gmk,gkn->gmn", a.astype(BF16), b.astype(BF16), preferred_element_type=F32)


def _bmm_nt(a, b):
    return jnp.einsum("gmk,gnk->gmn", a.astype(BF16), b.astype(BF16), preferred_element_type=F32)


def _wkv_kernel(r_ref, k_ref, v_ref, lw_ref, cum_ref, kk_ref, ag_ref, rk_ref, lnw_ref, lnb_ref, y_ref,
                s_ref, st_ref):
    c = WKV_CHUNK
    n = RWKV_N
    hb, tc = r_ref.shape[1], r_ref.shape[2]
    nc = tc // c
    g = hb * nc

    @pl.when(pl.program_id(2) == 0)
    def _():
        s_ref[...] = jnp.zeros_like(s_ref)

    chunks = lambda ref: ref[0].reshape(g, c, n)
    r, k, v, lw, cum, kk, ag = (chunks(ref) for ref in (r_ref, k_ref, v_ref, lw_ref, cum_ref, kk_ref, ag_ref))
    kk = kk / jnp.maximum(jnp.sqrt(jnp.sum(kk * kk, axis=-1, keepdims=True)), 1e-12)
    dinc = jnp.exp(cum)
    dinv = jnp.exp(-cum)
    rt = r * dinc
    kt = k * dinv
    at = -kk * jnp.exp(cum - lw)
    bt = kk * ag * dinv
    d_chunk = dinc[:, c - 1:c, :]

    row = lax.broadcasted_iota(I32, (1, c, c), 1)
    col = lax.broadcasted_iota(I32, (1, c, c), 2)
    strict = row > col
    incl = row >= col
    eye = jnp.broadcast_to((row == col).astype(F32), (g, c, c))
    p = _bmm_nt(jnp.concatenate([at, rt], axis=1), jnp.concatenate([bt, kt], axis=1))
    a_ab = jnp.where(strict, p[:, :c, :c], 0.0)
    a_ak = jnp.where(strict, p[:, :c, c:], 0.0)
    a_rb = jnp.where(incl, p[:, c:, :c], 0.0)
    a_rk = jnp.where(incl, p[:, c:, c:], 0.0)
    x = _bmm(a_ab, a_ab)
    tinv = eye + a_ab
    levels = c.bit_length() - 2
    for j in range(1, levels):
        both = _bmm(jnp.concatenate([x, tinv], axis=1), x)
        x = both[:, :c]
        tinv = tinv + both[:, c:]
    tinv = tinv + _bmm(tinv, x)
    z = _bmm(a_ak, v)
    ta = _bmm(tinv, jnp.concatenate([at, z], axis=2))
    ry = _bmm(a_rb, ta)
    rp = rt + ry[:, :, :n]
    yv = ry[:, :, n:] + _bmm(a_rk, v)
    vt = _bmm_nt(eye, v)
    att = _bmm_nt(eye, at)
    tt = _bmm_nt(jnp.concatenate([att, _bmm_nt(vt, a_ak)], axis=1), tinv)
    mn = _bmm(tt, bt)
    moff = mn[:, :n].reshape(hb, nc, n, n)
    n2 = (mn[:, n:] + _bmm(vt, kt)).reshape(hb, nc, n, n)
    dch = d_chunk.reshape(hb, nc, 1, n)

    s = s_ref[...]
    for ci in range(nc):
        st_ref[:, ci] = s
        s = (s + _bmm(s, moff[:, ci]) + n2[:, ci]) * dch[:, ci]
    s_ref[...] = s

    y = _bmm_nt(rp, st_ref[...].reshape(g, n, n)) + yv
    mu = jnp.mean(y, axis=-1, keepdims=True)
    yc = y - mu
    var = jnp.mean(yc * yc, axis=-1, keepdims=True)
    y = (yc * lax.rsqrt(var + RWKV_GN_EPS)).reshape(hb, tc, n) * lnw_ref[...] + lnb_ref[...]
    bonus = jnp.sum((r * k).reshape(hb, tc, n) * rk_ref[...], axis=-1, keepdims=True)
    y = y + bonus * v.reshape(hb, tc, n)
    y_ref[0] = jnp.concatenate([y[hd] for hd in range(hb)], axis=1)


def _wkv_scan(r, k, v, lw, cum, kk, ag, r_k, lnx_w, lnx_b):
    b, hh, t, n = r.shape
    tc = min(WKV_CHUNK * WKV_CHUNKS_PER_STEP, t)
    hb = WKV_HEADS_PER_STEP
    blk = pl.BlockSpec((1, hb, tc, n), lambda bi, hi, i: (bi, hi, i, 0))
    par = pl.BlockSpec((hb, 1, n), lambda bi, hi, i: (hi, 0, 0))
    per_head = lambda arr: arr.reshape(hh, 1, n)
    return pl.pallas_call(
        _wkv_kernel,
        grid=(b, hh // hb, t // tc),
        in_specs=[blk] * 7 + [par] * 3,
        out_specs=pl.BlockSpec((1, tc, hb * n), lambda bi, hi, i: (bi, i, hi)),
        out_shape=jax.ShapeDtypeStruct((b, t, hh * n), F32),
        scratch_shapes=[pltpu.VMEM((hb, n, n), F32), pltpu.VMEM((hb, tc // WKV_CHUNK, n, n), F32)],
        compiler_params=_cparams(("parallel", "parallel", "arbitrary")),
        name="wkv7_scan",
    )(r, k, v, lw, cum, kk, ag, per_head(r_k), per_head(lnx_w), per_head(lnx_b))


def _merge_kernel(ya_ref, yr_ref, g_ref, ga_ref, gr_ref, pa_ref, pr_ref, o_ref):
    ma = _dot(ya_ref[...], pa_ref[...])
    mr = _dot(yr_ref[...] * g_ref[...], pr_ref[...])
    o_ref[...] = (_sigmoid(ga_ref[...]) * ma + _sigmoid(gr_ref[...]) * mr).astype(o_ref.dtype)


def _merge(ya, yr, g, gates, p_attn, p_rwkv, tm=256):
    m, c = ya.shape
    d = p_attn.shape[1]
    tm = min(tm, m)
    tile = pl.BlockSpec((tm, c), lambda i: (i, 0))
    return pl.pallas_call(
        _merge_kernel,
        grid=(m // tm,),
        in_specs=[tile, tile, tile,
                  pl.BlockSpec((tm, d), lambda i: (i, 0)),
                  pl.BlockSpec((tm, d), lambda i: (i, 1)),
                  pl.BlockSpec((c, d), lambda i: (0, 0)),
                  pl.BlockSpec((c, d), lambda i: (0, 0))],
        out_specs=pl.BlockSpec((tm, d), lambda i: (i, 0)),
        out_shape=jax.ShapeDtypeStruct((m, d), BF16),
        compiler_params=_cparams(("parallel",)),
        name="gated_merge",
    )(ya, yr, g, gates, gates, p_attn, p_rwkv)


def _outproj_router_kernel(alpha, mg_ref, x_ref, wo_ref, g_ref, b_ref, wr_ref, br_ref,
                           h_ref, hrow_ref, eid_ref, gate_ref):
    mix = jnp.dot(mg_ref[...], wo_ref[...], preferred_element_type=F32)
    h = _layer_norm(alpha * x_ref[...] + mix, g_ref[...], b_ref[...])
    h_ref[...] = h
    slab = h.shape[1] // LANES
    for j in range(slab):
        hrow_ref[pl.ds(j, h.shape[0], stride=slab), :] = h[:, j * LANES:(j + 1) * LANES]
    h_hi = h.astype(BF16)
    h_lo = (h - h_hi.astype(F32)).astype(BF16)
    logits = (jnp.dot(h_hi, wr_ref[0], preferred_element_type=F32)
              + jnp.dot(h_lo, wr_ref[0], preferred_element_type=F32)
              + jnp.dot(h_hi, wr_ref[1], preferred_element_type=F32)) + br_ref[...]
    lane = lax.broadcasted_iota(I32, logits.shape, 1)
    ninf = -jnp.inf
    big = jnp.int32(2 * LANES)
    glog = jnp.where(lane < N_GROUPS, logits, ninf)
    gmax = jnp.max(glog, axis=-1, keepdims=True)
    gidx = jnp.min(jnp.where(glog == gmax, lane, big), axis=-1, keepdims=True)
    gtop = 1.0 / jnp.sum(jnp.exp(glog - gmax), axis=-1, keepdims=True)
    eg = (lane - N_GROUPS) // EXPERTS_PER_GROUP
    in_group = (lane >= N_GROUPS) & (lane < N_GROUPS + N_EXPERTS) & (eg == gidx)
    el = jnp.where(in_group, logits, ninf)
    m1 = jnp.max(el, axis=-1, keepdims=True)
    i1 = jnp.min(jnp.where(el == m1, lane, big), axis=-1, keepdims=True)
    el2 = jnp.where(lane == i1, ninf, el)
    m2 = jnp.max(el2, axis=-1, keepdims=True)
    i2 = jnp.min(jnp.where(el2 == m2, lane, big), axis=-1, keepdims=True)
    t = jnp.exp(m2 - m1)
    p1 = 1.0 / (1.0 + t)
    p2 = t / (1.0 + t)
    eid_ref[...] = jnp.where(lane == 0, i1 - N_GROUPS, jnp.where(lane == 1, i2 - N_GROUPS, 0))
    gate_ref[...] = jnp.where(lane == 0, gtop * p1, jnp.where(lane == 1, gtop * p2, 0.0))


def _outproj_router(merged, x, w_o, ln_g, ln_b, w_router, b_router, alpha, tm=256):
    m, d = x.shape
    tm = min(tm, m)
    tile = pl.BlockSpec((tm, d), lambda i: (i, 0))
    vec = pl.BlockSpec((1, d), lambda i: (0, 0))
    small = pl.BlockSpec((tm, LANES), lambda i: (i, 0))
    return pl.pallas_call(
        functools.partial(_outproj_router_kernel, alpha),
        grid=(m // tm,),
        in_specs=[tile, tile, pl.BlockSpec((d, d), lambda i: (0, 0)), vec, vec,
                  pl.BlockSpec((2, d, LANES), lambda i: (0, 0, 0)), pl.BlockSpec((1, LANES), lambda i: (0, 0))],
        out_specs=[tile, pl.BlockSpec((tm * (d // LANES), LANES), lambda i: (i, 0)), small, small],
        out_shape=[jax.ShapeDtypeStruct((m, d), F32), jax.ShapeDtypeStruct((m * (d // LANES), LANES), F32),
                   jax.ShapeDtypeStruct((m, LANES), I32), jax.ShapeDtypeStruct((m, LANES), F32)],
        compiler_params=_cparams(("parallel",)),
        name="outproj_ln_router",
    )(merged, x, w_o, ln_g.reshape(1, d), ln_b.reshape(1, d), w_router, b_router)


def _onehots(eid_ref):
    tm = eid_ref.shape[1]
    e_iota = lax.broadcasted_iota(I32, (N_EXPERTS, tm), 0)
    oh0 = (eid_ref[0:1, :] == e_iota).astype(F32)
    oh1 = (eid_ref[1:2, :] == e_iota).astype(F32)
    return oh0, oh1


def _count_kernel(eid_ref, cnt_ref):
    @pl.when(pl.program_id(0) == 0)
    def _():
        cnt_ref[...] = jnp.zeros_like(cnt_ref)

    oh0, oh1 = _onehots(eid_ref)
    cnt_ref[...] += jnp.sum(oh0 + oh1, axis=1, keepdims=True)


def _slot_kernel(eid_ref, pstart_ref, dest_ref, run_ref):
    @pl.when(pl.program_id(0) == 0)
    def _():
        run_ref[...] = jnp.zeros_like(run_ref)

    tm = eid_ref.shape[1]
    oh0, oh1 = _onehots(eid_ref)
    both = oh0 + oh1
    earlier = (lax.broadcasted_iota(I32, (tm, tm), 0) < lax.broadcasted_iota(I32, (tm, tm), 1)).astype(BF16)
    pre = jnp.dot(both.astype(BF16), earlier, preferred_element_type=F32)
    base = pre + run_ref[...] + pstart_ref[...]
    dest_ref[0:1, :] = jnp.sum(oh0 * base, axis=0, keepdims=True).astype(I32)
    dest_ref[1:2, :] = jnp.sum(oh1 * base, axis=0, keepdims=True).astype(I32)
    run_ref[...] += jnp.sum(both, axis=1, keepdims=True)


def _expert_counts(eid_t):
    m = eid_t.shape[1]
    tm = min(ROUTE_TILE, m)
    return pl.pallas_call(
        _count_kernel,
        grid=(m // tm,),
        in_specs=[pl.BlockSpec((TOP_K, tm), lambda i: (0, i))],
        out_specs=pl.BlockSpec((N_EXPERTS, 1), lambda i: (0, 0)),
        out_shape=jax.ShapeDtypeStruct((N_EXPERTS, 1), F32),
        compiler_params=_cparams(("arbitrary",)),
        name="expert_counts",
    )(eid_t)


def _expert_slots(eid_t, pstart):
    m = eid_t.shape[1]
    tm = min(ROUTE_TILE, m)
    return pl.pallas_call(
        _slot_kernel,
        grid=(m // tm,),
        in_specs=[pl.BlockSpec((TOP_K, tm), lambda i: (0, i)),
                  pl.BlockSpec((N_EXPERTS, 1), lambda i: (0, 0))],
        out_specs=pl.BlockSpec((TOP_K, tm), lambda i: (0, i)),
        out_shape=jax.ShapeDtypeStruct((TOP_K, m), I32),
        scratch_shapes=[pltpu.VMEM((N_EXPERTS, 1), F32)],
        compiler_params=_cparams(("arbitrary",)),
        name="expert_slots",
    )(eid_t, pstart)


def _row_slab(ref, row, slab):
    return ref.at[pl.ds(pl.multiple_of(row * slab, slab), slab), :]


def _dispatch_kernel(dest_ref, h_ref, xb_in_ref, xb_ref, sem):
    del xb_in_ref
    tm = DISPATCH_TILE
    slab = h_ref.shape[0] // tm
    t0 = pl.program_id(0) * tm

    def copy(n, k):
        return pltpu.make_async_copy(_row_slab(h_ref, n, slab),
                                     _row_slab(xb_ref, dest_ref[TOP_K * (t0 + n) + k], slab), sem)

    def start(n, carry):
        for k in range(TOP_K):
            copy(n, k).start()
        return carry

    lax.fori_loop(0, tm, start, 0, unroll=8)
    for k in range(TOP_K):
        pltpu.make_async_copy(h_ref, xb_ref.at[pl.ds(0, h_ref.shape[0]), :], sem).wait()


DISPATCH_TILE = 512


def _dispatch(dest_flat, h_rows, m, n_rows):
    slab = h_rows.shape[0] // m
    assert m % DISPATCH_TILE == 0
    grid_spec = pltpu.PrefetchScalarGridSpec(
        num_scalar_prefetch=1,
        grid=(m // DISPATCH_TILE,),
        in_specs=[pl.BlockSpec((DISPATCH_TILE * slab, LANES), lambda i, dr: (i, 0)),
                  pl.BlockSpec(memory_space=pl.ANY)],
        out_specs=pl.BlockSpec(memory_space=pl.ANY),
        scratch_shapes=[pltpu.SemaphoreType.DMA(())],
    )
    return pl.pallas_call(
        _dispatch_kernel,
        grid_spec=grid_spec,
        out_shape=jax.ShapeDtypeStruct((n_rows * slab, LANES), F32),
        input_output_aliases={2: 0},
        compiler_params=pltpu.CompilerParams(dimension_semantics=("arbitrary",), has_side_effects=True),
        name="moe_dispatch",
    )(dest_flat, h_rows, jnp.zeros((n_rows * slab, LANES), F32))


def _expert_kernel(item_e_ref, item_row_ref, item_nsub_ref, *refs):
    wg_refs = refs[:MOE_GU_PIECES]
    wu_refs = refs[MOE_GU_PIECES:2 * MOE_GU_PIECES]
    wd_refs = refs[2 * MOE_GU_PIECES:2 * MOE_GU_PIECES + MOE_DOWN_PIECES]
    xb_ref, yb_ref, xin_ref, yout_ref, x_ref, acc_ref, wgu_ref, wdb_ref, sem_in, sem_out = (
        refs[2 * MOE_GU_PIECES + MOE_DOWN_PIECES:])
    del item_e_ref
    it = pl.program_id(0)
    f = pl.program_id(1)
    n_items = pl.num_programs(0)
    nf = pl.num_programs(1)
    kp = wg_refs[0].shape[1]
    kc = kp * MOE_GU_PIECES
    ff = wg_refs[0].shape[2]
    fp = wd_refs[0].shape[1]
    fc = fp * MOE_DOWN_PIECES
    nsub = item_nsub_ref[it]
    row0 = item_row_ref[it]
    nxt = jnp.minimum(it + 1, n_items - 1)
    nsub_next = jnp.where(it + 1 < n_items, item_nsub_ref[nxt], 0)
    prv = jnp.maximum(it - 1, 0)
    nsub_prev = jnp.where(it > 0, item_nsub_ref[prv], 0)
    rb = MOE_ROWS
    slab = x_ref.shape[0] * kc // LANES
    blk = rb * slab

    def stage_rows(ref, s):
        return ref.at[pl.ds(pl.multiple_of(s * blk, blk), blk), :]

    def hbm_rows(ref, item_row, s):
        return ref.at[pl.ds(pl.multiple_of((item_row + s * rb) * slab, blk), blk), :]

    def in_copy(item_row, s):
        return pltpu.make_async_copy(hbm_rows(xb_ref, item_row, s), stage_rows(xin_ref, s), sem_in)

    def out_copy(item_row, s):
        return pltpu.make_async_copy(stage_rows(yout_ref, s), hbm_rows(yb_ref, item_row, s), sem_out)

    def loop(n, fn):
        def body(s, carry):
            fn(s)
            return carry
        lax.fori_loop(0, n, body, 0)

    @pl.when(nsub > 0)
    def _():
        @pl.when(f == 0)
        def _():
            @pl.when(it == 0)
            def _():
                loop(nsub, lambda s: in_copy(row0, s).start())
            loop(nsub, lambda s: in_copy(row0, s).wait())

            def to_matrix(s):
                rows = pl.ds(pl.multiple_of(s * rb, rb), rb)
                base = pl.multiple_of(s * blk, blk)
                for j in range(slab):
                    c0 = (j * LANES) % kc
                    x_ref[(j * LANES) // kc, rows, c0:c0 + LANES] = (
                        xin_ref[pl.ds(base + j, rb, stride=slab), :].astype(BF16))
            loop(nsub, to_matrix)
            loop(nsub_next, lambda s: in_copy(item_row_ref[nxt], s).start())

        for q in range(MOE_GU_PIECES):
            wgu_ref[q * kp:(q + 1) * kp, :ff] = wg_refs[q][0].astype(BF16)
            wgu_ref[q * kp:(q + 1) * kp, ff:] = wu_refs[q][0].astype(BF16)
        for q in range(MOE_DOWN_PIECES):
            wdb_ref[pl.ds(pl.multiple_of(f * fc + q * fp, 2 * SUBLANES), fp), :] = wd_refs[q][0].astype(BF16)

        def gate_up(start, size):
            rows = pl.ds(pl.multiple_of(start, rb), size)
            part = jnp.dot(x_ref[f, rows, :], wgu_ref[...], preferred_element_type=F32)
            acc_ref[rows, :] = jnp.where(f > 0, acc_ref[rows, :], 0.0) + part

        def pair(p, carry):
            gate_up(p * (2 * rb), 2 * rb)
            return carry
        lax.fori_loop(0, nsub // 2, pair, 0)

        @pl.when(nsub % 2 == 1)
        def _():
            gate_up((nsub - 1) * rb, rb)

        @pl.when(f == nf - 1)
        def _():
            loop(nsub_prev, lambda s: out_copy(item_row_ref[prv], s).wait())

            def down(start, size):
                rows = pl.ds(pl.multiple_of(start, rb), size)
                gate = acc_ref[rows, :ff]
                act = (gate * _sigmoid(gate) * acc_ref[rows, ff:]).astype(BF16)
                y = jnp.dot(act, wdb_ref[...], preferred_element_type=F32)
                base = pl.multiple_of(start * slab, blk)
                for j in range(slab):
                    yout_ref[pl.ds(base + j, size, stride=slab), :] = y[:, j * LANES:(j + 1) * LANES]

            def down_pair(p, carry):
                down(p * (2 * rb), 2 * rb)
                return carry
            lax.fori_loop(0, nsub // 2, down_pair, 0)

            @pl.when(nsub % 2 == 1)
            def _():
                down((nsub - 1) * rb, rb)

            loop(nsub, lambda s: out_copy(row0, s).start())

            @pl.when(nsub_next == 0)
            def _():
                loop(nsub, lambda s: out_copy(row0, s).wait())


def _experts(item_e, item_row, item_nsub, w_gate, w_up, w_down, xb):
    d = w_gate.shape[1]
    ff = w_gate.shape[2]
    nk = MOE_K_STEPS
    kc = d // nk
    fc = ff // nk
    assert d % nk == 0 and ff % nk == 0 and kc % LANES == 0 and fc % (2 * SUBLANES) == 0 and ff % LANES == 0
    n_items = item_e.shape[0]
    rows_max = MOE_ROWS * MOE_SUBS_PER_ITEM
    slab = d // LANES

    kp = kc // MOE_GU_PIECES
    fp = fc // MOE_DOWN_PIECES
    assert kc % MOE_GU_PIECES == 0 and fc % MOE_DOWN_PIECES == 0 and kp % (2 * SUBLANES) == 0 and fp % (2 * SUBLANES) == 0

    def piece(rows, cols, pieces, q):
        def index(it, k, ie, ir, ns):
            return ie[it], jnp.where(ns[it] > 0, k, nk - 1) * pieces + q, 0
        return pl.BlockSpec((1, rows, cols), index)

    grid_spec = pltpu.PrefetchScalarGridSpec(
        num_scalar_prefetch=3,
        grid=(n_items, nk),
        in_specs=[
            *[piece(kp, ff, MOE_GU_PIECES, q) for q in range(MOE_GU_PIECES)],
            *[piece(kp, ff, MOE_GU_PIECES, q) for q in range(MOE_GU_PIECES)],
            *[piece(fp, d, MOE_DOWN_PIECES, q) for q in range(MOE_DOWN_PIECES)],
            pl.BlockSpec(memory_space=pl.ANY),
        ],
        out_specs=pl.BlockSpec(memory_space=pl.ANY),
        scratch_shapes=[
            pltpu.VMEM((rows_max * slab, LANES), F32),
            pltpu.VMEM((rows_max * slab, LANES), F32),
            pltpu.VMEM((nk, rows_max, kc), BF16),
            pltpu.VMEM((rows_max, 2 * ff), F32),
            pltpu.VMEM((kc, 2 * ff), BF16),
            pltpu.VMEM((ff, d), BF16),
            pltpu.SemaphoreType.DMA(()),
            pltpu.SemaphoreType.DMA(()),
        ],
    )
    return pl.pallas_call(
        _expert_kernel,
        grid_spec=grid_spec,
        out_shape=jax.ShapeDtypeStruct(xb.shape, F32),
        input_output_aliases={3 + 2 * MOE_GU_PIECES + MOE_DOWN_PIECES: 0},
        compiler_params=pltpu.CompilerParams(dimension_semantics=("arbitrary", "arbitrary"),
                                             vmem_limit_bytes=VMEM_LIMIT_BYTES, has_side_effects=True),
        name="moe_experts",
    )(item_e, item_row, item_nsub, *([w_gate] * MOE_GU_PIECES), *([w_up] * MOE_GU_PIECES),
      *([w_down] * MOE_DOWN_PIECES), xb)


COMBINE_TILE = 256


def _combine_kernel(alpha, dest_ref, yb_ref, h_ref, gate_ref, g_ref, b_ref, o_ref, buf_ref, sem):
    tm = COMBINE_TILE
    i = pl.program_id(0)
    slab = buf_ref.shape[2] // tm

    def issue(tile, half):
        def start(n, carry):
            for k in range(TOP_K):
                pltpu.make_async_copy(_row_slab(yb_ref, dest_ref[TOP_K * (tile * tm + n) + k], slab),
                                      _row_slab(buf_ref.at[half, k], n, slab), sem.at[half]).start()
            return carry
        lax.fori_loop(0, tm, start, 0, unroll=8)

    @pl.when(i == 0)
    def _():
        issue(0, 0)

    @pl.when(i + 1 < pl.num_programs(0))
    def _():
        issue(i + 1, (i + 1) % 2)

    half = i % 2
    for k in range(TOP_K):
        pltpu.make_async_copy(yb_ref.at[pl.ds(0, buf_ref.shape[2]), :], buf_ref.at[half, k], sem.at[half]).wait()
    gate = gate_ref[...]
    g0 = gate[:, 0:1]
    g1 = gate[:, 1:2]
    col = lambda k, j: buf_ref[half, k, pl.ds(j, tm, stride=slab), :]
    ff = jnp.concatenate([g0 * col(0, j) + g1 * col(1, j) for j in range(slab)], axis=1)
    o_ref[...] = _layer_norm(alpha * h_ref[...] + ff, g_ref[...], b_ref[...])


def _combine(dest_flat, yb, h, gates, ln_g, ln_b, alpha):
    m, d = h.shape
    tm = COMBINE_TILE
    assert m % tm == 0
    grid_spec = pltpu.PrefetchScalarGridSpec(
        num_scalar_prefetch=1,
        grid=(m // tm,),
        in_specs=[pl.BlockSpec(memory_space=pl.ANY),
                  pl.BlockSpec((tm, d), lambda i, dr: (i, 0)),
                  pl.BlockSpec((tm, LANES), lambda i, dr: (i, 0)),
                  pl.BlockSpec((1, d), lambda i, dr: (0, 0)),
                  pl.BlockSpec((1, d), lambda i, dr: (0, 0))],
        out_specs=pl.BlockSpec((tm, d), lambda i, dr: (i, 0)),
        scratch_shapes=[pltpu.VMEM((2, TOP_K, tm * (d // LANES), LANES), F32), pltpu.SemaphoreType.DMA((2,))],
    )
    return pl.pallas_call(
        functools.partial(_combine_kernel, alpha),
        grid_spec=grid_spec,
        out_shape=jax.ShapeDtypeStruct((m, d), F32),
        compiler_params=_cparams(("arbitrary",)),
        name="moe_combine_ln",
    )(dest_flat, yb, h, gates, ln_g.reshape(1, d), ln_b.reshape(1, d))


def _moe_tables(counts):
    nsub_e = (counts + MOE_ROWS - 1) // MOE_ROWS
    pstart = (jnp.cumsum(nsub_e) - nsub_e) * MOE_ROWS
    nitem_e = (nsub_e + MOE_SUBS_PER_ITEM - 1) // MOE_SUBS_PER_ITEM
    item_end = jnp.cumsum(nitem_e)
    return nsub_e, pstart, nitem_e, item_end


def _moe(h, h_rows, eid, gates, w_gate, w_up, w_down, ln_g, ln_b, alpha):
    m, d = h.shape
    n_assign = m * TOP_K
    n_blocks = (n_assign + N_EXPERTS * (MOE_ROWS - 1) + MOE_ROWS - 1) // MOE_ROWS
    n_rows = n_blocks * MOE_ROWS
    n_items = N_EXPERTS + n_assign // (MOE_ROWS * MOE_SUBS_PER_ITEM)

    eid_t = eid[:, :TOP_K].T
    counts = _expert_counts(eid_t)[:, 0].astype(I32)
    nsub_e, pstart, nitem_e, item_end = _moe_tables(counts)
    dest_t = _expert_slots(eid_t, pstart.astype(F32).reshape(N_EXPERTS, 1))
    dest_flat = dest_t.T.reshape(-1)

    it = jnp.arange(n_items, dtype=I32)
    total_items = item_end[-1]
    item_e = jnp.minimum(jnp.sum(item_end[None, :] <= it[:, None], axis=1).astype(I32), N_EXPERTS - 1)
    j = it - (item_end - nitem_e)[item_e]
    used = it < total_items
    last_e = item_e[jnp.maximum(total_items - 1, 0)]
    item_nsub = jnp.where(used, jnp.clip(nsub_e[item_e] - j * MOE_SUBS_PER_ITEM, 0, MOE_SUBS_PER_ITEM), 0).astype(I32)
    item_row = jnp.where(used, pstart[item_e] + j * (MOE_ROWS * MOE_SUBS_PER_ITEM), 0).astype(I32)
    item_e = jnp.where(used, item_e, last_e).astype(I32)

    xb = _dispatch(dest_flat, h_rows, m, n_rows)
    yb = _experts(item_e, item_row, item_nsub, w_gate, w_up, w_down, xb)
    return _combine(dest_flat, yb, h, gates, ln_g, ln_b, alpha)


def _pad_cols(w, n):
    return jnp.pad(w, ((0, 0), (0, n - w.shape[1])))


def _pad_rows(w, n):
    return jnp.pad(w, ((0, n - w.shape[0]), (0, 0)))


def kernel(x, w_in, attn_sinks, rw_mu_rkv, rw_mu_wag, rw_w0, rw_w1, rw_w2, rw_a0, rw_a1, rw_a2, rw_g1, rw_g2, rw_k_k, rw_k_a, rw_r_k, rw_lnx_w, rw_lnx_b, p_attn, p_rwkv, w_o, ln1_g, ln1_b, w_group, b_group, w_expert, b_expert, w_gate, w_up, w_down, ln2_g, ln2_b):
    b, t, d = x.shape
    depth = w_in.shape[0]
    m = b * t
    alpha = (2.0 * depth) ** 0.25
    cosb, sinb = _rope_tables(t)
    qkv_w = ATTN_Q_W + 2 * ATTN_KV_W
    rkv_w = 3 * RWKV_W
    h = x
    for l in range(depth):
        hf = h.reshape(m, d)
        hb = hf.astype(BF16)
        w_in_b = w_in[l].astype(BF16)
        qkv = _matmul_cols(hb, w_in_b, 0, qkv_w, F32)
        rkv = _matmul_cols(hb, w_in_b, qkv_w, rkv_w, F32)
        gates = _matmul_cols(hb, w_in_b, qkv_w + rkv_w, 2 * d, F32)

        y_a = _attention(qkv.reshape(b, t, qkv_w), attn_sinks[l], cosb, sinb)

        lora = lambda w, n: _pad_cols(w, n).astype(BF16)
        lorb = lambda w, n: _pad_rows(w, n).astype(BF16)
        n_w = -(-rw_w1.shape[2] // LANES) * LANES
        n_a = -(-rw_a1.shape[2] // LANES) * LANES
        n_g = -(-rw_g1.shape[2] // LANES) * LANES
        r_, k_, v_, lw_, cum_, kk_, ag_, g_ = _rwkv_prep(
            h, rkv.reshape(b, t, rkv_w), rw_mu_rkv[l], rw_mu_wag[l], rw_w0[l],
            lora(rw_w1[l], n_w), lorb(rw_w2[l], n_w), rw_a0[l], lora(rw_a1[l], n_a), lorb(rw_a2[l], n_a),
            lora(rw_g1[l], n_g), lorb(rw_g2[l], n_g), rw_k_k[l], rw_k_a[l])
        y_r = _wkv_scan(r_, k_, v_, lw_, cum_, kk_, ag_, rw_r_k[l], rw_lnx_w[l], rw_lnx_b[l])

        merged = _merge(y_a.reshape(m, ATTN_Q_W), y_r.reshape(m, RWKV_W), g_.reshape(m, RWKV_W), gates,
                        p_attn[l].astype(BF16), p_rwkv[l].astype(BF16))
        w_router = _pad_cols(jnp.concatenate([w_group[l], w_expert[l]], axis=1), LANES)
        w_router_hi = w_router.astype(BF16)
        w_router = jnp.stack([w_router_hi, (w_router - w_router_hi.astype(F32)).astype(BF16)])
        b_router = _pad_cols(jnp.concatenate([b_group[l], b_expert[l]])[None, :], LANES)
        h1, h1_rows, eid, gate = _outproj_router(merged, hf, w_o[l].astype(BF16), ln1_g[l], ln1_b[l],
                                        w_router, b_router, alpha)
        h2 = _moe(h1, h1_rows, eid, gate, w_gate[l], w_up[l], w_down[l], ln2_g[l], ln2_b[l], alpha)
        h = h2.reshape(b, t, d)
    return h
```

```python
import functools

import jax
import jax.numpy as jnp
from jax import lax
from jax.experimental import pallas as pl
from jax.experimental.pallas import tpu as pltpu

F32 = jnp.float32
BF16 = jnp.bfloat16
I32 = jnp.int32

HEAD_DIM = 64
ATTN_Q_HEADS = 16
ATTN_KV_HEADS = 4
ATTN_GROUP = ATTN_Q_HEADS // ATTN_KV_HEADS
ATTN_Q_W = ATTN_Q_HEADS * HEAD_DIM
ATTN_KV_W = ATTN_KV_HEADS * HEAD_DIM
WINDOW = 128
ROPE_THETA = 10000.0
RWKV_HEADS = 16
RWKV_N = 64
RWKV_W = RWKV_HEADS * RWKV_N
RWKV_GN_EPS = 64e-5
N_GROUPS = 8
EXPERTS_PER_GROUP = 8
N_EXPERTS = N_GROUPS * EXPERTS_PER_GROUP
TOP_K = 2
LN_EPS = 1e-5

LANES = 128
SUBLANES = 8
VMEM_LIMIT_BYTES = 56 * 1024 * 1024

WKV_CHUNK = 64
WKV_CHUNKS_PER_STEP = 4
WKV_HEADS_PER_STEP = 8
MOE_ROWS = 128
MOE_SUBS_PER_ITEM = 4
MOE_K_STEPS = 4
MOE_GU_PIECES = 1
MOE_DOWN_PIECES = 1
ROUTE_TILE = 512


def _cparams(sem, vmem=VMEM_LIMIT_BYTES):
    return pltpu.CompilerParams(dimension_semantics=sem, vmem_limit_bytes=vmem)


def _sigmoid(x):
    return 1.0 / (1.0 + jnp.exp(-x))


def _dot(a, b):
    return jnp.dot(a.astype(BF16), b.astype(BF16), preferred_element_type=F32)


def _dot_nt(a, b):
    return lax.dot_general(a.astype(BF16), b.astype(BF16), (((1,), (1,)), ((), ())),
                           preferred_element_type=F32)


def _layer_norm(t, g, b):
    mu = jnp.mean(t, axis=-1, keepdims=True)
    d = t - mu
    var = jnp.mean(d * d, axis=-1, keepdims=True)
    return d * lax.rsqrt(var + LN_EPS) * g + b


def _matmul_kernel(a_ref, b_ref, o_ref, bb_ref):
    @pl.when(pl.program_id(1) == 0)
    def _():
        bb_ref[...] = b_ref[...].astype(BF16)

    o_ref[...] = jnp.dot(a_ref[...], bb_ref[...], preferred_element_type=F32).astype(o_ref.dtype)


def _matmul_cols(a, b, col0, ncols, out_dtype, tm=1024, tn=512):
    m, k = a.shape
    tm = min(tm, m)
    cb = col0 // tn
    assert col0 % tn == 0 and ncols % tn == 0 and m % tm == 0
    return pl.pallas_call(
        _matmul_kernel,
        grid=(ncols // tn, m // tm),
        in_specs=[pl.BlockSpec((tm, k), lambda j, i: (i, 0)),
                  pl.BlockSpec((k, tn), lambda j, i: (0, j + cb))],
        out_specs=pl.BlockSpec((tm, tn), lambda j, i: (i, j)),
        out_shape=jax.ShapeDtypeStruct((m, ncols), out_dtype),
        scratch_shapes=[pltpu.VMEM((k, tn), BF16)],
        compiler_params=_cparams(("arbitrary", "arbitrary")),
        name="inproj_matmul",
    )(a, b)


def _rope(x, cosb, sinb):
    half = HEAD_DIM // 2
    lane = lax.broadcasted_iota(I32, cosb.shape, 1)
    first_half = (lane % HEAD_DIM) < half
    outs = []
    for g in range(x.shape[1] // LANES):
        xg = x[:, g * LANES:(g + 1) * LANES]
        partner = jnp.where(first_half, pltpu.roll(xg, LANES - half, axis=1), pltpu.roll(xg, half, axis=1))
        outs.append(xg * cosb + partner * sinb)
    return outs


def _attn_kernel(sinks_ref, q_ref, kc_ref, kp_ref, vc_ref, vp_ref, cosc_ref, sinc_ref, cosp_ref, sinp_ref, o_ref):
    blk = pl.program_id(1)
    tq = q_ref.shape[1]
    qg = _rope(q_ref[0], cosc_ref[...], sinc_ref[...])
    kcg = _rope(kc_ref[0], cosc_ref[...], sinc_ref[...])
    kpg = _rope(kp_ref[0], cosp_ref[...], sinp_ref[...])
    vc = vc_ref[0]
    vp = vp_ref[0]

    def head(groups, h):
        g = groups[h // 2]
        return g[:, (h % 2) * HEAD_DIM:(h % 2 + 1) * HEAD_DIM]

    rows = ATTN_GROUP * tq
    qi = lax.broadcasted_iota(I32, (rows, 2 * tq), 0) % tq
    kj = lax.broadcasted_iota(I32, (rows, 2 * tq), 1)
    dist = qi + tq - kj
    valid = (dist >= 0) & (dist < WINDOW) & ((blk > 0) | (kj >= tq))
    rid = lax.broadcasted_iota(I32, (rows, 1), 0) // tq
    scale = HEAD_DIM ** -0.5
    for kvh in range(ATTN_KV_HEADS):
        qh = jnp.concatenate([head(qg, kvh * ATTN_GROUP + g) for g in range(ATTN_GROUP)], axis=0)
        kw = jnp.concatenate([head(kpg, kvh), head(kcg, kvh)], axis=0)
        vw = jnp.concatenate([vp[:, kvh * HEAD_DIM:(kvh + 1) * HEAD_DIM],
                              vc[:, kvh * HEAD_DIM:(kvh + 1) * HEAD_DIM]], axis=0)
        s = _dot_nt(qh, kw) * scale
        s = jnp.where(valid, s, -jnp.inf)
        sink = jnp.zeros((rows, 1), F32)
        for g in range(ATTN_GROUP):
            sink = jnp.where(rid == g, sinks_ref[kvh * ATTN_GROUP + g], sink)
        m = jnp.maximum(jnp.max(s, axis=-1, keepdims=True), sink)
        e = jnp.exp(s - m)
        denom = jnp.sum(e, axis=-1, keepdims=True) + jnp.exp(sink - m)
        o = _dot(e, vw) / denom
        for g in range(ATTN_GROUP):
            hq = kvh * ATTN_GROUP + g
            o_ref[0, :, hq * HEAD_DIM:(hq + 1) * HEAD_DIM] = o[g * tq:(g + 1) * tq]


def _attention(qkv, sinks, cosb, sinb):
    b, t, _ = qkv.shape
    tq = WINDOW
    nb = t // tq
    kcol = ATTN_Q_W // ATTN_KV_W
    prev = lambda i: jnp.maximum(i - 1, 0)
    grid_spec = pltpu.PrefetchScalarGridSpec(
        num_scalar_prefetch=0,
        grid=(b, nb),
        in_specs=[
            pl.BlockSpec(memory_space=pltpu.SMEM),
            pl.BlockSpec((1, tq, ATTN_Q_W), lambda bi, i: (bi, i, 0)),
            pl.BlockSpec((1, tq, ATTN_KV_W), lambda bi, i: (bi, i, kcol)),
            pl.BlockSpec((1, tq, ATTN_KV_W), lambda bi, i: (bi, prev(i), kcol)),
            pl.BlockSpec((1, tq, ATTN_KV_W), lambda bi, i: (bi, i, kcol + 1)),
            pl.BlockSpec((1, tq, ATTN_KV_W), lambda bi, i: (bi, prev(i), kcol + 1)),
            pl.BlockSpec((tq, LANES), lambda bi, i: (i, 0)),
            pl.BlockSpec((tq, LANES), lambda bi, i: (i, 0)),
            pl.BlockSpec((tq, LANES), lambda bi, i: (prev(i), 0)),
            pl.BlockSpec((tq, LANES), lambda bi, i: (prev(i), 0)),
        ],
        out_specs=pl.BlockSpec((1, tq, ATTN_Q_W), lambda bi, i: (bi, i, 0)),
    )
    return pl.pallas_call(
        _attn_kernel,
        grid_spec=grid_spec,
        out_shape=jax.ShapeDtypeStruct((b, t, ATTN_Q_W), F32),
        compiler_params=_cparams(("parallel", "arbitrary")),
        name="swa_attention",
    )(sinks, qkv, qkv, qkv, qkv, qkv, cosb, sinb, cosb, sinb)


def _rope_tables(t):
    inv = 1.0 / (ROPE_THETA ** (jnp.arange(0, HEAD_DIM, 2, dtype=F32) / HEAD_DIM))
    ang = jnp.arange(t, dtype=F32)[:, None] * inv[None, :]
    cos, sin = jnp.cos(ang), jnp.sin(ang)
    reps = LANES // HEAD_DIM
    cosb = jnp.tile(jnp.concatenate([cos, cos], axis=-1), (1, reps))
    sinb = jnp.tile(jnp.concatenate([-sin, sin], axis=-1), (1, reps))
    return cosb, sinb


def _rwkv_prep_kernel(h_ref, hp_ref, r_ref, k_ref, v_ref, rp_ref, kp_ref, vp_ref,
                      mu_rkv_ref, mu_wag_ref, w0_ref, w1_ref, w2_ref, a0_ref, a1_ref, a2_ref,
                      g1_ref, g2_ref, kk_ref, ka_ref,
                      ro_ref, ko_ref, vo_ref, lwo_ref, cumo_ref, kko_ref, ago_ref, go_ref):
    first = pl.program_id(1) == 0
    last_row = SUBLANES - 1

    def shifted(cur, prev_ref):
        prev_row = jnp.where(first, 0.0, prev_ref[0, last_row:last_row + 1, :])
        rowid = lax.broadcasted_iota(I32, cur.shape, 0)
        return jnp.where(rowid == 0, prev_row, pltpu.roll(cur, 1, axis=0))

    h = h_ref[0]
    xx = shifted(h, hp_ref) - h
    xw = h + xx * mu_wag_ref[0:1, :]
    xa = h + xx * mu_wag_ref[1:2, :]
    xg = h + xx * mu_wag_ref[2:3, :]
    w_raw = w0_ref[...] + _dot(jnp.tanh(_dot(xw, w1_ref[...])), w2_ref[...])
    neg = -w_raw
    softplus = jnp.maximum(neg, 0.0) + jnp.log1p(jnp.exp(-jnp.abs(neg)))
    w = -softplus - 0.5
    lw = -jnp.exp(w)
    tm = lw.shape[0]
    row = lax.broadcasted_iota(I32, (tm, tm), 0)
    col = lax.broadcasted_iota(I32, (tm, tm), 1)
    tri = ((row >= col) & (row // WKV_CHUNK == col // WKV_CHUNK)).astype(BF16)
    cum = sum(jnp.dot(tri, piece, preferred_element_type=F32) for piece in _split3(lw))
    ag = _sigmoid(a0_ref[...] + _dot(_dot(xa, a1_ref[...]), a2_ref[...]))
    go_ref[0] = _dot(_sigmoid(_dot(xg, g1_ref[...])), g2_ref[...])

    r = r_ref[0]
    k = k_ref[0]
    v = v_ref[0]
    r = r + (shifted(r, rp_ref) - r) * mu_rkv_ref[0:1, :]
    k = k + (shifted(k, kp_ref) - k) * mu_rkv_ref[1:2, :]
    v = v + (shifted(v, vp_ref) - v) * mu_rkv_ref[2:3, :]
    kk = k * kk_ref[...]
    k = k * (1.0 + (ag - 1.0) * ka_ref[...])
    for hd in range(RWKV_HEADS):
        sl = slice(hd * RWKV_N, (hd + 1) * RWKV_N)
        ro_ref[0, hd] = r[:, sl]
        ko_ref[0, hd] = k[:, sl]
        vo_ref[0, hd] = v[:, sl]
        lwo_ref[0, hd] = lw[:, sl]
        cumo_ref[0, hd] = cum[:, sl]
        kko_ref[0, hd] = kk[:, sl]
        ago_ref[0, hd] = ag[:, sl]


def _rwkv_prep(h, rkv, mu_rkv, mu_wag, w0, w1, w2, a0, a1, a2, g1, g2, k_k, k_a, tm=256):
    b, t, d = h.shape
    tm = min(tm, t)
    c = RWKV_W
    spt = tm // SUBLANES
    prevblk = lambda i: jnp.maximum(i * spt - 1, 0)
    full = lambda arr: pl.BlockSpec(arr.shape, lambda bi, i: (0,) * arr.ndim)
    row = lambda arr: arr.reshape(1, -1)
    w0, a0, k_k, k_a = row(w0), row(a0), row(k_k), row(k_a)
    in_specs = [
        pl.BlockSpec((1, tm, d), lambda bi, i: (bi, i, 0)),
        pl.BlockSpec((1, SUBLANES, d), lambda bi, i: (bi, prevblk(i), 0)),
        pl.BlockSpec((1, tm, c), lambda bi, i: (bi, i, 0)),
        pl.BlockSpec((1, tm, c), lambda bi, i: (bi, i, 1)),
        pl.BlockSpec((1, tm, c), lambda bi, i: (bi, i, 2)),
        pl.BlockSpec((1, SUBLANES, c), lambda bi, i: (bi, prevblk(i), 0)),
        pl.BlockSpec((1, SUBLANES, c), lambda bi, i: (bi, prevblk(i), 1)),
        pl.BlockSpec((1, SUBLANES, c), lambda bi, i: (bi, prevblk(i), 2)),
    ] + [full(a) for a in (mu_rkv, mu_wag, w0, w1, w2, a0, a1, a2, g1, g2, k_k, k_a)]
    hm = jax.ShapeDtypeStruct((b, RWKV_HEADS, t, RWKV_N), F32)
    hm_spec = pl.BlockSpec((1, RWKV_HEADS, tm, RWKV_N), lambda bi, i: (bi, 0, i, 0))
    return pl.pallas_call(
        _rwkv_prep_kernel,
        grid=(b, t // tm),
        in_specs=in_specs,
        out_specs=[hm_spec] * 7 + [pl.BlockSpec((1, tm, c), lambda bi, i: (bi, i, 0))],
        out_shape=[hm] * 7 + [jax.ShapeDtypeStruct((b, t, c), F32)],
        compiler_params=_cparams(("parallel", "arbitrary")),
        name="rwkv_prep",
    )(h, h, rkv, rkv, rkv, rkv, rkv, rkv, mu_rkv, mu_wag, w0, w1, w2, a0, a1, a2, g1, g2, k_k, k_a)


def _split3(x):
    hi = x.astype(BF16)
    r1 = x - hi.astype(F32)
    mid = r1.astype(BF16)
    lo = (r1 - mid.astype(F32)).astype(BF16)
    return hi, mid, lo


def _bmm(a, b):
    return jnp.einsum("gmk,gkn->gmn", a.astype(BF16), b.astype(BF16), preferred_element_type=F32)


def _bmm_nt(a, b):
    return jnp.einsum("gmk,gnk->gmn", a.astype(BF16), b.astype(BF16), preferred_element_type=F32)


def _wkv_kernel(r_ref, k_ref, v_ref, lw_ref, cum_ref, kk_ref, ag_ref, rk_ref, lnw_ref, lnb_ref, y_ref,
                s_ref, st_ref):
    c = WKV_CHUNK
    n = RWKV_N
    hb, tc = r_ref.shape[1], r_ref.shape[2]
    nc = tc // c
    g = hb * nc

    @pl.when(pl.program_id(2) == 0)
    def _():
        s_ref[...] = jnp.zeros_like(s_ref)

    chunks = lambda ref: ref[0].reshape(g, c, n)
    r, k, v, lw, cum, kk, ag = (chunks(ref) for ref in (r_ref, k_ref, v_ref, lw_ref, cum_ref, kk_ref, ag_ref))
    kk = kk / jnp.maximum(jnp.sqrt(jnp.sum(kk * kk, axis=-1, keepdims=True)), 1e-12)
    dinc = jnp.exp(cum)
    dinv = jnp.exp(-cum)
    rt = r * dinc
    kt = k * dinv
    at = -kk * jnp.exp(cum - lw)
    bt = kk * ag * dinv
    d_chunk = dinc[:, c - 1:c, :]

    row = lax.broadcasted_iota(I32, (1, c, c), 1)
    col = lax.broadcasted_iota(I32, (1, c, c), 2)
    strict = row > col
    incl = row >= col
    eye = jnp.broadcast_to((row == col).astype(F32), (g, c, c))
    p = _bmm_nt(jnp.concatenate([at, rt], axis=1), jnp.concatenate([bt, kt], axis=1))
    a_ab = jnp.where(strict, p[:, :c, :c], 0.0)
    a_ak = jnp.where(strict, p[:, :c, c:], 0.0)
    a_rb = jnp.where(incl, p[:, c:, :c], 0.0)
    a_rk = jnp.where(incl, p[:, c:, c:], 0.0)
    x = _bmm(a_ab, a_ab)
    tinv = eye + a_ab
    levels = c.bit_length() - 2
    for j in range(1, levels):
        both = _bmm(jnp.concatenate([x, tinv], axis=1), x)
        x = both[:, :c]
        tinv = tinv + both[:, c:]
    tinv = tinv + _bmm(tinv, x)
    z = _bmm(a_ak, v)
    ta = _bmm(tinv, jnp.concatenate([at, z], axis=2))
    ry = _bmm(a_rb, ta)
    rp = rt + ry[:, :, :n]
    yv = ry[:, :, n:] + _bmm(a_rk, v)
    vt = _bmm_nt(eye, v)
    att = _bmm_nt(eye, at)
    tt = _bmm_nt(jnp.concatenate([att, _bmm_nt(vt, a_ak)], axis=1), tinv)
    mn = _bmm(tt, bt)
    moff = mn[:, :n].reshape(hb, nc, n, n)
    n2 = (mn[:, n:] + _bmm(vt, kt)).reshape(hb, nc, n, n)
    dch = d_chunk.reshape(hb, nc, 1, n)

    s = s_ref[...]
    for ci in range(nc):
        st_ref[:, ci] = s
        s = (s + _bmm(s, moff[:, ci]) + n2[:, ci]) * dch[:, ci]
    s_ref[...] = s

    y = _bmm_nt(rp, st_ref[...].reshape(g, n, n)) + yv
    mu = jnp.mean(y, axis=-1, keepdims=True)
    yc = y - mu
    var = jnp.mean(yc * yc, axis=-1, keepdims=True)
    y = (yc * lax.rsqrt(var + RWKV_GN_EPS)).reshape(hb, tc, n) * lnw_ref[...] + lnb_ref[...]
    bonus = jnp.sum((r * k).reshape(hb, tc, n) * rk_ref[...], axis=-1, keepdims=True)
    y = y + bonus * v.reshape(hb, tc, n)
    y_ref[0] = jnp.concatenate([y[hd] for hd in range(hb)], axis=1)


def _wkv_scan(r, k, v, lw, cum, kk, ag, r_k, lnx_w, lnx_b):
    b, hh, t, n = r.shape
    tc = min(WKV_CHUNK * WKV_CHUNKS_PER_STEP, t)
    hb = WKV_HEADS_PER_STEP
    blk = pl.BlockSpec((1, hb, tc, n), lambda bi, hi, i: (bi, hi, i, 0))
    par = pl.BlockSpec((hb, 1, n), lambda bi, hi, i: (hi, 0, 0))
    per_head = lambda arr: arr.reshape(hh, 1, n)
    return pl.pallas_call(
        _wkv_kernel,
        grid=(b, hh // hb, t // tc),
        in_specs=[blk] * 7 + [par] * 3,
        out_specs=pl.BlockSpec((1, tc, hb * n), lambda bi, hi, i: (bi, i, hi)),
        out_shape=jax.ShapeDtypeStruct((b, t, hh * n), F32),
        scratch_shapes=[pltpu.VMEM((hb, n, n), F32), pltpu.VMEM((hb, tc // WKV_CHUNK, n, n), F32)],
        compiler_params=_cparams(("parallel", "parallel", "arbitrary")),
        name="wkv7_scan",
    )(r, k, v, lw, cum, kk, ag, per_head(r_k), per_head(lnx_w), per_head(lnx_b))


def _merge_kernel(ya_ref, yr_ref, g_ref, ga_ref, gr_ref, pa_ref, pr_ref, o_ref):
    ma = _dot(ya_ref[...], pa_ref[...])
    mr = _dot(yr_ref[...] * g_ref[...], pr_ref[...])
    o_ref[...] = (_sigmoid(ga_ref[...]) * ma + _sigmoid(gr_ref[...]) * mr).astype(o_ref.dtype)


def _merge(ya, yr, g, gates, p_attn, p_rwkv, tm=256):
    m, c = ya.shape
    d = p_attn.shape[1]
    tm = min(tm, m)
    tile = pl.BlockSpec((tm, c), lambda i: (i, 0))
    return pl.pallas_call(
        _merge_kernel,
        grid=(m // tm,),
        in_specs=[tile, tile, tile,
                  pl.BlockSpec((tm, d), lambda i: (i, 0)),
                  pl.BlockSpec((tm, d), lambda i: (i, 1)),
                  pl.BlockSpec((c, d), lambda i: (0, 0)),
                  pl.BlockSpec((c, d), lambda i: (0, 0))],
        out_specs=pl.BlockSpec((tm, d), lambda i: (i, 0)),
        out_shape=jax.ShapeDtypeStruct((m, d), BF16),
        compiler_params=_cparams(("parallel",)),
        name="gated_merge",
    )(ya, yr, g, gates, gates, p_attn, p_rwkv)


def _outproj_router_kernel(alpha, mg_ref, x_ref, wo_ref, g_ref, b_ref, wr_ref, br_ref,
                           h_ref, hrow_ref, eid_ref, gate_ref):
    mix = jnp.dot(mg_ref[...], wo_ref[...], preferred_element_type=F32)
    h = _layer_norm(alpha * x_ref[...] + mix, g_ref[...], b_ref[...])
    h_ref[...] = h
    slab = h.shape[1] // LANES
    for j in range(slab):
        hrow_ref[pl.ds(j, h.shape[0], stride=slab), :] = h[:, j * LANES:(j + 1) * LANES]
    h_hi = h.astype(BF16)
    h_lo = (h - h_hi.astype(F32)).astype(BF16)
    logits = (jnp.dot(h_hi, wr_ref[0], preferred_element_type=F32)
              + jnp.dot(h_lo, wr_ref[0], preferred_element_type=F32)
              + jnp.dot(h_hi, wr_ref[1], preferred_element_type=F32)) + br_ref[...]
    lane = lax.broadcasted_iota(I32, logits.shape, 1)
    ninf = -jnp.inf
    big = jnp.int32(2 * LANES)
    glog = jnp.where(lane < N_GROUPS, logits, ninf)
    gmax = jnp.max(glog, axis=-1, keepdims=True)
    gidx = jnp.min(jnp.where(glog == gmax, lane, big), axis=-1, keepdims=True)
    gtop = 1.0 / jnp.sum(jnp.exp(glog - gmax), axis=-1, keepdims=True)
    eg = (lane - N_GROUPS) // EXPERTS_PER_GROUP
    in_group = (lane >= N_GROUPS) & (lane < N_GROUPS + N_EXPERTS) & (eg == gidx)
    el = jnp.where(in_group, logits, ninf)
    m1 = jnp.max(el, axis=-1, keepdims=True)
    i1 = jnp.min(jnp.where(el == m1, lane, big), axis=-1, keepdims=True)
    el2 = jnp.where(lane == i1, ninf, el)
    m2 = jnp.max(el2, axis=-1, keepdims=True)
    i2 = jnp.min(jnp.where(el2 == m2, lane, big), axis=-1, keepdims=True)
    t = jnp.exp(m2 - m1)
    p1 = 1.0 / (1.0 + t)
    p2 = t / (1.0 + t)
    eid_ref[...] = jnp.where(lane == 0, i1 - N_GROUPS, jnp.where(lane == 1, i2 - N_GROUPS, 0))
    gate_ref[...] = jnp.where(lane == 0, gtop * p1, jnp.where(lane == 1, gtop * p2, 0.0))


def _outproj_router(merged, x, w_o, ln_g, ln_b, w_router, b_router, alpha, tm=256):
    m, d = x.shape
    tm = min(tm, m)
    tile = pl.BlockSpec((tm, d), lambda i: (i, 0))
    vec = pl.BlockSpec((1, d), lambda i: (0, 0))
    small = pl.BlockSpec((tm, LANES), lambda i: (i, 0))
    return pl.pallas_call(
        functools.partial(_outproj_router_kernel, alpha),
        grid=(m // tm,),
        in_specs=[tile, tile, pl.BlockSpec((d, d), lambda i: (0, 0)), vec, vec,
                  pl.BlockSpec((2, d, LANES), lambda i: (0, 0, 0)), pl.BlockSpec((1, LANES), lambda i: (0, 0))],
        out_specs=[tile, pl.BlockSpec((tm * (d // LANES), LANES), lambda i: (i, 0)), small, small],
        out_shape=[jax.ShapeDtypeStruct((m, d), F32), jax.ShapeDtypeStruct((m * (d // LANES), LANES), F32),
                   jax.ShapeDtypeStruct((m, LANES), I32), jax.ShapeDtypeStruct((m, LANES), F32)],
        compiler_params=_cparams(("parallel",)),
        name="outproj_ln_router",
    )(merged, x, w_o, ln_g.reshape(1, d), ln_b.reshape(1, d), w_router, b_router)


def _onehots(eid_ref):
    tm = eid_ref.shape[1]
    e_iota = lax.broadcasted_iota(I32, (N_EXPERTS, tm), 0)
    oh0 = (eid_ref[0:1, :] == e_iota).astype(F32)
    oh1 = (eid_ref[1:2, :] == e_iota).astype(F32)
    return oh0, oh1


def _count_kernel(eid_ref, cnt_ref):
    @pl.when(pl.program_id(0) == 0)
    def _():
        cnt_ref[...] = jnp.zeros_like(cnt_ref)

    oh0, oh1 = _onehots(eid_ref)
    cnt_ref[...] += jnp.sum(oh0 + oh1, axis=1, keepdims=True)


def _slot_kernel(eid_ref, pstart_ref, dest_ref, run_ref):
    @pl.when(pl.program_id(0) == 0)
    def _():
        run_ref[...] = jnp.zeros_like(run_ref)

    tm = eid_ref.shape[1]
    oh0, oh1 = _onehots(eid_ref)
    both = oh0 + oh1
    earlier = (lax.broadcasted_iota(I32, (tm, tm), 0) < lax.broadcasted_iota(I32, (tm, tm), 1)).astype(BF16)
    pre = jnp.dot(both.astype(BF16), earlier, preferred_element_type=F32)
    base = pre + run_ref[...] + pstart_ref[...]
    dest_ref[0:1, :] = jnp.sum(oh0 * base, axis=0, keepdims=True).astype(I32)
    dest_ref[1:2, :] = jnp.sum(oh1 * base, axis=0, keepdims=True).astype(I32)
    run_ref[...] += jnp.sum(both, axis=1, keepdims=True)


def _expert_counts(eid_t):
    m = eid_t.shape[1]
    tm = min(ROUTE_TILE, m)
    return pl.pallas_call(
        _count_kernel,
        grid=(m // tm,),
        in_specs=[pl.BlockSpec((TOP_K, tm), lambda i: (0, i))],
        out_specs=pl.BlockSpec((N_EXPERTS, 1), lambda i: (0, 0)),
        out_shape=jax.ShapeDtypeStruct((N_EXPERTS, 1), F32),
        compiler_params=_cparams(("arbitrary",)),
        name="expert_counts",
    )(eid_t)


def _expert_slots(eid_t, pstart):
    m = eid_t.shape[1]
    tm = min(ROUTE_TILE, m)
    return pl.pallas_call(
        _slot_kernel,
        grid=(m // tm,),
        in_specs=[pl.BlockSpec((TOP_K, tm), lambda i: (0, i)),
                  pl.BlockSpec((N_EXPERTS, 1), lambda i: (0, 0))],
        out_specs=pl.BlockSpec((TOP_K, tm), lambda i: (0, i)),
        out_shape=jax.ShapeDtypeStruct((TOP_K, m), I32),
        scratch_shapes=[pltpu.VMEM((N_EXPERTS, 1), F32)],
        compiler_params=_cparams(("arbitrary",)),
        name="expert_slots",
    )(eid_t, pstart)


def _row_slab(ref, row, slab):
    return ref.at[pl.ds(pl.multiple_of(row * slab, slab), slab), :]


def _dispatch_kernel(dest_ref, h_ref, xb_in_ref, xb_ref, sem):
    del xb_in_ref
    tm = DISPATCH_TILE
    slab = h_ref.shape[0] // tm
    t0 = pl.program_id(0) * tm

    def copy(n, k):
        return pltpu.make_async_copy(_row_slab(h_ref, n, slab),
                                     _row_slab(xb_ref, dest_ref[TOP_K * (t0 + n) + k], slab), sem)

    def start(n, carry):
        for k in range(TOP_K):
            copy(n, k).start()
        return carry

    lax.fori_loop(0, tm, start, 0, unroll=8)
    for k in range(TOP_K):
        pltpu.make_async_copy(h_ref, xb_ref.at[pl.ds(0, h_ref.shape[0]), :], sem).wait()


DISPATCH_TILE = 512


def _dispatch(dest_flat, h_rows, m, n_rows):
    slab = h_rows.shape[0] // m
    assert m % DISPATCH_TILE == 0
    grid_spec = pltpu.PrefetchScalarGridSpec(
        num_scalar_prefetch=1,
        grid=(m // DISPATCH_TILE,),
        in_specs=[pl.BlockSpec((DISPATCH_TILE * slab, LANES), lambda i, dr: (i, 0)),
                  pl.BlockSpec(memory_space=pl.ANY)],
        out_specs=pl.BlockSpec(memory_space=pl.ANY),
        scratch_shapes=[pltpu.SemaphoreType.DMA(())],
    )
    return pl.pallas_call(
        _dispatch_kernel,
        grid_spec=grid_spec,
        out_shape=jax.ShapeDtypeStruct((n_rows * slab, LANES), F32),
        input_output_aliases={2: 0},
        compiler_params=pltpu.CompilerParams(dimension_semantics=("arbitrary",), has_side_effects=True),
        name="moe_dispatch",
    )(dest_flat, h_rows, jnp.zeros((n_rows * slab, LANES), F32))


def _expert_kernel(item_e_ref, item_row_ref, item_nsub_ref, *refs):
    wg_refs = refs[:MOE_GU_PIECES]
    wu_refs = refs[MOE_GU_PIECES:2 * MOE_GU_PIECES]
    wd_refs = refs[2 * MOE_GU_PIECES:2 * MOE_GU_PIECES + MOE_DOWN_PIECES]
    xb_ref, yb_ref, xin_ref, yout_ref, x_ref, acc_ref, wgu_ref, wdb_ref, sem_in, sem_out = (
        refs[2 * MOE_GU_PIECES + MOE_DOWN_PIECES:])
    del item_e_ref
    it = pl.program_id(0)
    f = pl.program_id(1)
    n_items = pl.num_programs(0)
    nf = pl.num_programs(1)
    kp = wg_refs[0].shape[1]
    kc = kp * MOE_GU_PIECES
    ff = wg_refs[0].shape[2]
    fp = wd_refs[0].shape[1]
    fc = fp * MOE_DOWN_PIECES
    nsub = item_nsub_ref[it]
    row0 = item_row_ref[it]
    nxt = jnp.minimum(it + 1, n_items - 1)
    nsub_next = jnp.where(it + 1 < n_items, item_nsub_ref[nxt], 0)
    prv = jnp.maximum(it - 1, 0)
    nsub_prev = jnp.where(it > 0, item_nsub_ref[prv], 0)
    rb = MOE_ROWS
    slab = x_ref.shape[0] * kc // LANES
    blk = rb * slab

    def stage_rows(ref, s):
        return ref.at[pl.ds(pl.multiple_of(s * blk, blk), blk), :]

    def hbm_rows(ref, item_row, s):
        return ref.at[pl.ds(pl.multiple_of((item_row + s * rb) * slab, blk), blk), :]

    def in_copy(item_row, s):
        return pltpu.make_async_copy(hbm_rows(xb_ref, item_row, s), stage_rows(xin_ref, s), sem_in)

    def out_copy(item_row, s):
        return pltpu.make_async_copy(stage_rows(yout_ref, s), hbm_rows(yb_ref, item_row, s), sem_out)

    def loop(n, fn):
        def body(s, carry):
            fn(s)
            return carry
        lax.fori_loop(0, n, body, 0)

    @pl.when(nsub > 0)
    def _():
        @pl.when(f == 0)
        def _():
            @pl.when(it == 0)
            def _():
                x_ref[...] = jnp.zeros_like(x_ref)
                loop(nsub, lambda s: in_copy(row0, s).start())
            loop(nsub, lambda s: in_copy(row0, s).wait())

            def to_matrix(s):
                rows = pl.ds(pl.multiple_of(s * rb, rb), rb)
                base = pl.multiple_of(s * blk, blk)
                for j in range(slab):
                    c0 = (j * LANES) % kc
                    x_ref[(j * LANES) // kc, rows, c0:c0 + LANES] = (
                        xin_ref[pl.ds(base + j, rb, stride=slab), :].astype(BF16))
            loop(nsub, to_matrix)
            loop(nsub_next, lambda s: in_copy(item_row_ref[nxt], s).start())

        for q in range(MOE_GU_PIECES):
            wgu_ref[q * kp:(q + 1) * kp, :ff] = wg_refs[q][0].astype(BF16)
            wgu_ref[q * kp:(q + 1) * kp, ff:] = wu_refs[q][0].astype(BF16)
        for q in range(MOE_DOWN_PIECES):
            wdb_ref[pl.ds(pl.multiple_of(f * fc + q * fp, 2 * SUBLANES), fp), :] = wd_refs[q][0].astype(BF16)

        def gate_up(start, size):
            rows = pl.ds(pl.multiple_of(start, rb), size)
            part = jnp.dot(x_ref[f, rows, :], wgu_ref[...], preferred_element_type=F32)
            acc_ref[rows, :] = jnp.where(f > 0, acc_ref[rows, :], 0.0) + part

        gate_up(0, 2 * rb)

        def pair(p, carry):
            gate_up(p * (2 * rb), 2 * rb)
            return carry
        lax.fori_loop(1, nsub // 2, pair, 0)

        @pl.when((nsub % 2 == 1) & (nsub > 1))
        def _():
            gate_up((nsub - 1) * rb, rb)

        @pl.when(f == nf - 1)
        def _():
            loop(nsub_prev, lambda s: out_copy(item_row_ref[prv], s).wait())

            def down(start, size):
                rows = pl.ds(pl.multiple_of(start, rb), size)
                gate = acc_ref[rows, :ff]
                act = (gate * _sigmoid(gate) * acc_ref[rows, ff:]).astype(BF16)
                y = jnp.dot(act, wdb_ref[...], preferred_element_type=F32)
                base = pl.multiple_of(start * slab, blk)
                for j in range(slab):
                    yout_ref[pl.ds(base + j, size, stride=slab), :] = y[:, j * LANES:(j + 1) * LANES]

            down(0, 2 * rb)

            def down_pair(p, carry):
                down(p * (2 * rb), 2 * rb)
                return carry
            lax.fori_loop(1, nsub // 2, down_pair, 0)

            @pl.when((nsub % 2 == 1) & (nsub > 1))
            def _():
                down((nsub - 1) * rb, rb)

            loop(nsub, lambda s: out_copy(row0, s).start())

            @pl.when(nsub_next == 0)
            def _():
                loop(nsub, lambda s: out_copy(row0, s).wait())


def _experts(item_e, item_row, item_nsub, w_gate, w_up, w_down, xb):
    d = w_gate.shape[1]
    ff = w_gate.shape[2]
    nk = MOE_K_STEPS
    kc = d // nk
    fc = ff // nk
    assert d % nk == 0 and ff % nk == 0 and kc % LANES == 0 and fc % (2 * SUBLANES) == 0 and ff % LANES == 0
    n_items = item_e.shape[0]
    rows_max = MOE_ROWS * MOE_SUBS_PER_ITEM
    slab = d // LANES

    kp = kc // MOE_GU_PIECES
    fp = fc // MOE_DOWN_PIECES
    assert kc % MOE_GU_PIECES == 0 and fc % MOE_DOWN_PIECES == 0 and kp % (2 * SUBLANES) == 0 and fp % (2 * SUBLANES) == 0

    def piece(rows, cols, pieces, q):
        def index(it, k, ie, ir, ns):
            return ie[it], jnp.where(ns[it] > 0, k, nk - 1) * pieces + q, 0
        return pl.BlockSpec((1, rows, cols), index)

    grid_spec = pltpu.PrefetchScalarGridSpec(
        num_scalar_prefetch=3,
        grid=(n_items, nk),
        in_specs=[
            *[piece(kp, ff, MOE_GU_PIECES, q) for q in range(MOE_GU_PIECES)],
            *[piece(kp, ff, MOE_GU_PIECES, q) for q in range(MOE_GU_PIECES)],
            *[piece(fp, d, MOE_DOWN_PIECES, q) for q in range(MOE_DOWN_PIECES)],
            pl.BlockSpec(memory_space=pl.ANY),
        ],
        out_specs=pl.BlockSpec(memory_space=pl.ANY),
        scratch_shapes=[
            pltpu.VMEM((rows_max * slab, LANES), F32),
            pltpu.VMEM((rows_max * slab, LANES), F32),
            pltpu.VMEM((nk, rows_max, kc), BF16),
            pltpu.VMEM((rows_max, 2 * ff), F32),
            pltpu.VMEM((kc, 2 * ff), BF16),
            pltpu.VMEM((ff, d), BF16),
            pltpu.SemaphoreType.DMA(()),
            pltpu.SemaphoreType.DMA(()),
        ],
    )
    return pl.pallas_call(
        _expert_kernel,
        grid_spec=grid_spec,
        out_shape=jax.ShapeDtypeStruct(xb.shape, F32),
        input_output_aliases={3 + 2 * MOE_GU_PIECES + MOE_DOWN_PIECES: 0},
        compiler_params=pltpu.CompilerParams(dimension_semantics=("arbitrary", "arbitrary"),
                                             vmem_limit_bytes=VMEM_LIMIT_BYTES, has_side_effects=True),
        name="moe_experts",
    )(item_e, item_row, item_nsub, *([w_gate] * MOE_GU_PIECES), *([w_up] * MOE_GU_PIECES),
      *([w_down] * MOE_DOWN_PIECES), xb)


COMBINE_TILE = 256


def _combine_kernel(alpha, dest_ref, yb_ref, h_ref, gate_ref, g_ref, b_ref, o_ref, buf_ref, sem):
    tm = COMBINE_TILE
    i = pl.program_id(0)
    slab = buf_ref.shape[2] // tm

    def issue(tile, half):
        def start(n, carry):
            for k in range(TOP_K):
                pltpu.make_async_copy(_row_slab(yb_ref, dest_ref[TOP_K * (tile * tm + n) + k], slab),
                                      _row_slab(buf_ref.at[half, k], n, slab), sem.at[half]).start()
            return carry
        lax.fori_loop(0, tm, start, 0, unroll=8)

    @pl.when(i == 0)
    def _():
        issue(0, 0)

    @pl.when(i + 1 < pl.num_programs(0))
    def _():
        issue(i + 1, (i + 1) % 2)

    half = i % 2
    for k in range(TOP_K):
        pltpu.make_async_copy(yb_ref.at[pl.ds(0, buf_ref.shape[2]), :], buf_ref.at[half, k], sem.at[half]).wait()
    gate = gate_ref[...]
    g0 = gate[:, 0:1]
    g1 = gate[:, 1:2]
    col = lambda k, j: buf_ref[half, k, pl.ds(j, tm, stride=slab), :]
    ff = jnp.concatenate([g0 * col(0, j) + g1 * col(1, j) for j in range(slab)], axis=1)
    o_ref[...] = _layer_norm(alpha * h_ref[...] + ff, g_ref[...], b_ref[...])


def _combine(dest_flat, yb, h, gates, ln_g, ln_b, alpha):
    m, d = h.shape
    tm = COMBINE_TILE
    assert m % tm == 0
    grid_spec = pltpu.PrefetchScalarGridSpec(
        num_scalar_prefetch=1,
        grid=(m // tm,),
        in_specs=[pl.BlockSpec(memory_space=pl.ANY),
                  pl.BlockSpec((tm, d), lambda i, dr: (i, 0)),
                  pl.BlockSpec((tm, LANES), lambda i, dr: (i, 0)),
                  pl.BlockSpec((1, d), lambda i, dr: (0, 0)),
                  pl.BlockSpec((1, d), lambda i, dr: (0, 0))],
        out_specs=pl.BlockSpec((tm, d), lambda i, dr: (i, 0)),
        scratch_shapes=[pltpu.VMEM((2, TOP_K, tm * (d // LANES), LANES), F32), pltpu.SemaphoreType.DMA((2,))],
    )
    return pl.pallas_call(
        functools.partial(_combine_kernel, alpha),
        grid_spec=grid_spec,
        out_shape=jax.ShapeDtypeStruct((m, d), F32),
        compiler_params=_cparams(("arbitrary",)),
        name="moe_combine_ln",
    )(dest_flat, yb, h, gates, ln_g.reshape(1, d), ln_b.reshape(1, d))


def _moe_tables(counts):
    nsub_e = (counts + MOE_ROWS - 1) // MOE_ROWS
    pstart = (jnp.cumsum(nsub_e) - nsub_e) * MOE_ROWS
    nitem_e = (nsub_e + MOE_SUBS_PER_ITEM - 1) // MOE_SUBS_PER_ITEM
    item_end = jnp.cumsum(nitem_e)
    return nsub_e, pstart, nitem_e, item_end


def _moe(h, h_rows, eid, gates, w_gate, w_up, w_down, ln_g, ln_b, alpha):
    m, d = h.shape
    n_assign = m * TOP_K
    n_blocks = (n_assign + N_EXPERTS * (MOE_ROWS - 1) + MOE_ROWS - 1) // MOE_ROWS
    n_rows = n_blocks * MOE_ROWS
    n_items = N_EXPERTS + n_assign // (MOE_ROWS * MOE_SUBS_PER_ITEM)

    eid_t = eid[:, :TOP_K].T
    counts = _expert_counts(eid_t)[:, 0].astype(I32)
    nsub_e, pstart, nitem_e, item_end = _moe_tables(counts)
    dest_t = _expert_slots(eid_t, pstart.astype(F32).reshape(N_EXPERTS, 1))
    dest_flat = dest_t.T.reshape(-1)

    it = jnp.arange(n_items, dtype=I32)
    total_items = item_end[-1]
    item_e = jnp.minimum(jnp.sum(item_end[None, :] <= it[:, None], axis=1).astype(I32), N_EXPERTS - 1)
    j = it - (item_end - nitem_e)[item_e]
    used = it < total_items
    last_e = item_e[jnp.maximum(total_items - 1, 0)]
    item_nsub = jnp.where(used, jnp.clip(nsub_e[item_e] - j * MOE_SUBS_PER_ITEM, 0, MOE_SUBS_PER_ITEM), 0).astype(I32)
    item_row = jnp.where(used, pstart[item_e] + j * (MOE_ROWS * MOE_SUBS_PER_ITEM), 0).astype(I32)
    item_e = jnp.where(used, item_e, last_e).astype(I32)

    xb = _dispatch(dest_flat, h_rows, m, n_rows)
    yb = _experts(item_e, item_row, item_nsub, w_gate, w_up, w_down, xb)
    return _combine(dest_flat, yb, h, gates, ln_g, ln_b, alpha)


def _pad_cols(w, n):
    return jnp.pad(w, ((0, 0), (0, n - w.shape[1])))


def _pad_rows(w, n):
    return jnp.pad(w, ((0, n - w.shape[0]), (0, 0)))


def kernel(x, w_in, attn_sinks, rw_mu_rkv, rw_mu_wag, rw_w0, rw_w1, rw_w2, rw_a0, rw_a1, rw_a2, rw_g1, rw_g2, rw_k_k, rw_k_a, rw_r_k, rw_lnx_w, rw_lnx_b, p_attn, p_rwkv, w_o, ln1_g, ln1_b, w_group, b_group, w_expert, b_expert, w_gate, w_up, w_down, ln2_g, ln2_b):
    b, t, d = x.shape
    depth = w_in.shape[0]
    m = b * t
    alpha = (2.0 * depth) ** 0.25
    cosb, sinb = _rope_tables(t)
    qkv_w = ATTN_Q_W + 2 * ATTN_KV_W
    rkv_w = 3 * RWKV_W
    h = x
    for l in range(depth):
        hf = h.reshape(m, d)
        hb = hf.astype(BF16)
        qkv = _matmul_cols(hb, w_in[l], 0, qkv_w, F32)
        rkv = _matmul_cols(hb, w_in[l], qkv_w, rkv_w, F32)
        gates = _matmul_cols(hb, w_in[l], qkv_w + rkv_w, 2 * d, F32)

        y_a = _attention(qkv.reshape(b, t, qkv_w), attn_sinks[l], cosb, sinb)

        lora = lambda w, n: _pad_cols(w, n).astype(BF16)
        lorb = lambda w, n: _pad_rows(w, n).astype(BF16)
        n_w = -(-rw_w1.shape[2] // LANES) * LANES
        n_a = -(-rw_a1.shape[2] // LANES) * LANES
        n_g = -(-rw_g1.shape[2] // LANES) * LANES
        r_, k_, v_, lw_, cum_, kk_, ag_, g_ = _rwkv_prep(
            h, rkv.reshape(b, t, rkv_w), rw_mu_rkv[l], rw_mu_wag[l], rw_w0[l],
            lora(rw_w1[l], n_w), lorb(rw_w2[l], n_w), rw_a0[l], lora(rw_a1[l], n_a), lorb(rw_a2[l], n_a),
            lora(rw_g1[l], n_g), lorb(rw_g2[l], n_g), rw_k_k[l], rw_k_a[l])
        y_r = _wkv_scan(r_, k_, v_, lw_, cum_, kk_, ag_, rw_r_k[l], rw_lnx_w[l], rw_lnx_b[l])

        merged = _merge(y_a.reshape(m, ATTN_Q_W), y_r.reshape(m, RWKV_W), g_.reshape(m, RWKV_W), gates,
                        p_attn[l].astype(BF16), p_rwkv[l].astype(BF16))
        w_router = _pad_cols(jnp.concatenate([w_group[l], w_expert[l]], axis=1), LANES)
        w_router_hi = w_router.astype(BF16)
        w_router = jnp.stack([w_router_hi, (w_router - w_router_hi.astype(F32)).astype(BF16)])
        b_router = _pad_cols(jnp.concatenate([b_group[l], b_expert[l]])[None, :], LANES)
        h1, h1_rows, eid, gate = _outproj_router(merged, hf, w_o[l].astype(BF16), ln1_g[l], ln1_b[l],
                                        w_router, b_router, alpha)
        h2 = _moe(h1, h1_rows, eid, gate, w_gate[l], w_up[l], w_down[l], ln2_g[l], ln2_b[l], alpha)
        h = h2.reshape(b, t, d)
    return h
```

```python
import functools

import jax
import jax.numpy as jnp
from jax import lax
from jax.experimental import pallas as pl
from jax.experimental.pallas import tpu as pltpu

F32 = jnp.float32
BF16 = jnp.bfloat16
I32 = jnp.int32

HEAD_DIM = 64
ATTN_Q_HEADS = 16
ATTN_KV_HEADS = 4
ATTN_GROUP = ATTN_Q_HEADS // ATTN_KV_HEADS
ATTN_Q_W = ATTN_Q_HEADS * HEAD_DIM
ATTN_KV_W = ATTN_KV_HEADS * HEAD_DIM
WINDOW = 128
ROPE_THETA = 10000.0
RWKV_HEADS = 16
RWKV_N = 64
RWKV_W = RWKV_HEADS * RWKV_N
RWKV_GN_EPS = 64e-5
N_GROUPS = 8
EXPERTS_PER_GROUP = 8
N_EXPERTS = N_GROUPS * EXPERTS_PER_GROUP
TOP_K = 2
LN_EPS = 1e-5

LANES = 128
SUBLANES = 8
VMEM_LIMIT_BYTES = 56 * 1024 * 1024

WKV_CHUNK = 64
WKV_CHUNKS_PER_STEP = 4
WKV_HEADS_PER_STEP = 8
MOE_ROWS = 128
MOE_SUBS_PER_ITEM = 4
MOE_K_STEPS = 4
MOE_GU_PIECES = 4
MOE_DOWN_PIECES = 2
ROUTE_TILE = 512


def _cparams(sem, vmem=VMEM_LIMIT_BYTES):
    return pltpu.CompilerParams(dimension_semantics=sem, vmem_limit_bytes=vmem)


def _sigmoid(x):
    return 1.0 / (1.0 + jnp.exp(-x))


def _dot(a, b):
    return jnp.dot(a.astype(BF16), b.astype(BF16), preferred_element_type=F32)


def _dot_nt(a, b):
    return lax.dot_general(a.astype(BF16), b.astype(BF16), (((1,), (1,)), ((), ())),
                           preferred_element_type=F32)


def _layer_norm(t, g, b):
    mu = jnp.mean(t, axis=-1, keepdims=True)
    d = t - mu
    var = jnp.mean(d * d, axis=-1, keepdims=True)
    return d * lax.rsqrt(var + LN_EPS) * g + b


def _matmul_kernel(a_ref, b_ref, o_ref, bb_ref):
    @pl.when(pl.program_id(1) == 0)
    def _():
        bb_ref[...] = b_ref[...].astype(BF16)

    o_ref[...] = jnp.dot(a_ref[...], bb_ref[...], preferred_element_type=F32).astype(o_ref.dtype)


def _matmul_cols(a, b, col0, ncols, out_dtype, tm=1024, tn=512):
    m, k = a.shape
    tm = min(tm, m)
    cb = col0 // tn
    assert col0 % tn == 0 and ncols % tn == 0 and m % tm == 0
    return pl.pallas_call(
        _matmul_kernel,
        grid=(ncols // tn, m // tm),
        in_specs=[pl.BlockSpec((tm, k), lambda j, i: (i, 0)),
                  pl.BlockSpec((k, tn), lambda j, i: (0, j + cb))],
        out_specs=pl.BlockSpec((tm, tn), lambda j, i: (i, j)),
        out_shape=jax.ShapeDtypeStruct((m, ncols), out_dtype),
        scratch_shapes=[pltpu.VMEM((k, tn), BF16)],
        compiler_params=_cparams(("arbitrary", "arbitrary")),
        name="inproj_matmul",
    )(a, b)


def _rope(x, cosb, sinb):
    half = HEAD_DIM // 2
    lane = lax.broadcasted_iota(I32, cosb.shape, 1)
    first_half = (lane % HEAD_DIM) < half
    outs = []
    for g in range(x.shape[1] // LANES):
        xg = x[:, g * LANES:(g + 1) * LANES]
        partner = jnp.where(first_half, pltpu.roll(xg, LANES - half, axis=1), pltpu.roll(xg, half, axis=1))
        outs.append(xg * cosb + partner * sinb)
    return outs


def _attn_kernel(sinks_ref, q_ref, kc_ref, kp_ref, vc_ref, vp_ref, cosc_ref, sinc_ref, cosp_ref, sinp_ref, o_ref):
    blk = pl.program_id(1)
    tq = q_ref.shape[1]
    qg = _rope(q_ref[0], cosc_ref[...], sinc_ref[...])
    kcg = _rope(kc_ref[0], cosc_ref[...], sinc_ref[...])
    kpg = _rope(kp_ref[0], cosp_ref[...], sinp_ref[...])
    vc = vc_ref[0]
    vp = vp_ref[0]

    def head(groups, h):
        g = groups[h // 2]
        return g[:, (h % 2) * HEAD_DIM:(h % 2 + 1) * HEAD_DIM]

    rows = ATTN_GROUP * tq
    qi = lax.broadcasted_iota(I32, (rows, 2 * tq), 0) % tq
    kj = lax.broadcasted_iota(I32, (rows, 2 * tq), 1)
    dist = qi + tq - kj
    valid = (dist >= 0) & (dist < WINDOW) & ((blk > 0) | (kj >= tq))
    rid = lax.broadcasted_iota(I32, (rows, 1), 0) // tq
    scale = HEAD_DIM ** -0.5
    for kvh in range(ATTN_KV_HEADS):
        qh = jnp.concatenate([head(qg, kvh * ATTN_GROUP + g) for g in range(ATTN_GROUP)], axis=0)
        kw = jnp.concatenate([head(kpg, kvh), head(kcg, kvh)], axis=0)
        vw = jnp.concatenate([vp[:, kvh * HEAD_DIM:(kvh + 1) * HEAD_DIM],
                              vc[:, kvh * HEAD_DIM:(kvh + 1) * HEAD_DIM]], axis=0)
        s = _dot_nt(qh, kw) * scale
        s = jnp.where(valid, s, -jnp.inf)
        sink = jnp.zeros((rows, 1), F32)
        for g in range(ATTN_GROUP):
            sink = jnp.where(rid == g, sinks_ref[kvh * ATTN_GROUP + g], sink)
        m = jnp.maximum(jnp.max(s, axis=-1, keepdims=True), sink)
        e = jnp.exp(s - m)
        denom = jnp.sum(e, axis=-1, keepdims=True) + jnp.exp(sink - m)
        o = _dot(e, vw) / denom
        for g in range(ATTN_GROUP):
            hq = kvh * ATTN_GROUP + g
            o_ref[0, :, hq * HEAD_DIM:(hq + 1) * HEAD_DIM] = o[g * tq:(g + 1) * tq]


def _attention(qkv, sinks, cosb, sinb):
    b, t, _ = qkv.shape
    tq = WINDOW
    nb = t // tq
    kcol = ATTN_Q_W // ATTN_KV_W
    prev = lambda i: jnp.maximum(i - 1, 0)
    grid_spec = pltpu.PrefetchScalarGridSpec(
        num_scalar_prefetch=0,
        grid=(b, nb),
        in_specs=[
            pl.BlockSpec(memory_space=pltpu.SMEM),
            pl.BlockSpec((1, tq, ATTN_Q_W), lambda bi, i: (bi, i, 0)),
            pl.BlockSpec((1, tq, ATTN_KV_W), lambda bi, i: (bi, i, kcol)),
            pl.BlockSpec((1, tq, ATTN_KV_W), lambda bi, i: (bi, prev(i), kcol)),
            pl.BlockSpec((1, tq, ATTN_KV_W), lambda bi, i: (bi, i, kcol + 1)),
            pl.BlockSpec((1, tq, ATTN_KV_W), lambda bi, i: (bi, prev(i), kcol + 1)),
            pl.BlockSpec((tq, LANES), lambda bi, i: (i, 0)),
            pl.BlockSpec((tq, LANES), lambda bi, i: (i, 0)),
            pl.BlockSpec((tq, LANES), lambda bi, i: (prev(i), 0)),
            pl.BlockSpec((tq, LANES), lambda bi, i: (prev(i), 0)),
        ],
        out_specs=pl.BlockSpec((1, tq, ATTN_Q_W), lambda bi, i: (bi, i, 0)),
    )
    return pl.pallas_call(
        _attn_kernel,
        grid_spec=grid_spec,
        out_shape=jax.ShapeDtypeStruct((b, t, ATTN_Q_W), F32),
        compiler_params=_cparams(("parallel", "arbitrary")),
        name="swa_attention",
    )(sinks, qkv, qkv, qkv, qkv, qkv, cosb, sinb, cosb, sinb)


def _rope_tables(t):
    inv = 1.0 / (ROPE_THETA ** (jnp.arange(0, HEAD_DIM, 2, dtype=F32) / HEAD_DIM))
    ang = jnp.arange(t, dtype=F32)[:, None] * inv[None, :]
    cos, sin = jnp.cos(ang), jnp.sin(ang)
    reps = LANES // HEAD_DIM
    cosb = jnp.tile(jnp.concatenate([cos, cos], axis=-1), (1, reps))
    sinb = jnp.tile(jnp.concatenate([-sin, sin], axis=-1), (1, reps))
    return cosb, sinb


def _rwkv_prep_kernel(h_ref, hp_ref, r_ref, k_ref, v_ref, rp_ref, kp_ref, vp_ref,
                      mu_rkv_ref, mu_wag_ref, w0_ref, w1_ref, w2_ref, a0_ref, a1_ref, a2_ref,
                      g1_ref, g2_ref, kk_ref, ka_ref,
                      ro_ref, ko_ref, vo_ref, lwo_ref, cumo_ref, kko_ref, ago_ref, go_ref):
    first = pl.program_id(1) == 0
    last_row = SUBLANES - 1

    def shifted(cur, prev_ref):
        prev_row = jnp.where(first, 0.0, prev_ref[0, last_row:last_row + 1, :])
        rowid = lax.broadcasted_iota(I32, cur.shape, 0)
        return jnp.where(rowid == 0, prev_row, pltpu.roll(cur, 1, axis=0))

    h = h_ref[0]
    xx = shifted(h, hp_ref) - h
    xw = h + xx * mu_wag_ref[0:1, :]
    xa = h + xx * mu_wag_ref[1:2, :]
    xg = h + xx * mu_wag_ref[2:3, :]
    w_raw = w0_ref[...] + _dot(jnp.tanh(_dot(xw, w1_ref[...])), w2_ref[...])
    neg = -w_raw
    softplus = jnp.maximum(neg, 0.0) + jnp.log1p(jnp.exp(-jnp.abs(neg)))
    w = -softplus - 0.5
    lw = -jnp.exp(w)
    tm = lw.shape[0]
    row = lax.broadcasted_iota(I32, (tm, tm), 0)
    col = lax.broadcasted_iota(I32, (tm, tm), 1)
    tri = ((row >= col) & (row // WKV_CHUNK == col // WKV_CHUNK)).astype(BF16)
    cum = sum(jnp.dot(tri, piece, preferred_element_type=F32) for piece in _split3(lw))
    ag = _sigmoid(a0_ref[...] + _dot(_dot(xa, a1_ref[...]), a2_ref[...]))
    go_ref[0] = _dot(_sigmoid(_dot(xg, g1_ref[...])), g2_ref[...])

    r = r_ref[0]
    k = k_ref[0]
    v = v_ref[0]
    r = r + (shifted(r, rp_ref) - r) * mu_rkv_ref[0:1, :]
    k = k + (shifted(k, kp_ref) - k) * mu_rkv_ref[1:2, :]
    v = v + (shifted(v, vp_ref) - v) * mu_rkv_ref[2:3, :]
    kk = k * kk_ref[...]
    k = k * (1.0 + (ag - 1.0) * ka_ref[...])
    for hd in range(RWKV_HEADS):
        sl = slice(hd * RWKV_N, (hd + 1) * RWKV_N)
        ro_ref[0, hd] = r[:, sl]
        ko_ref[0, hd] = k[:, sl]
        vo_ref[0, hd] = v[:, sl]
        lwo_ref[0, hd] = lw[:, sl]
        cumo_ref[0, hd] = cum[:, sl]
        kko_ref[0, hd] = kk[:, sl]
        ago_ref[0, hd] = ag[:, sl]


def _rwkv_prep(h, rkv, mu_rkv, mu_wag, w0, w1, w2, a0, a1, a2, g1, g2, k_k, k_a, tm=256):
    b, t, d = h.shape
    tm = min(tm, t)
    c = RWKV_W
    spt = tm // SUBLANES
    prevblk = lambda i: jnp.maximum(i * spt - 1, 0)
    full = lambda arr: pl.BlockSpec(arr.shape, lambda bi, i: (0,) * arr.ndim)
    row = lambda arr: arr.reshape(1, -1)
    w0, a0, k_k, k_a = row(w0), row(a0), row(k_k), row(k_a)
    in_specs = [
        pl.BlockSpec((1, tm, d), lambda bi, i: (bi, i, 0)),
        pl.BlockSpec((1, SUBLANES, d), lambda bi, i: (bi, prevblk(i), 0)),
        pl.BlockSpec((1, tm, c), lambda bi, i: (bi, i, 0)),
        pl.BlockSpec((1, tm, c), lambda bi, i: (bi, i, 1)),
        pl.BlockSpec((1, tm, c), lambda bi, i: (bi, i, 2)),
        pl.BlockSpec((1, SUBLANES, c), lambda bi, i: (bi, prevblk(i), 0)),
        pl.BlockSpec((1, SUBLANES, c), lambda bi, i: (bi, prevblk(i), 1)),
        pl.BlockSpec((1, SUBLANES, c), lambda bi, i: (bi, prevblk(i), 2)),
    ] + [full(a) for a in (mu_rkv, mu_wag, w0, w1, w2, a0, a1, a2, g1, g2, k_k, k_a)]
    hm = jax.ShapeDtypeStruct((b, RWKV_HEADS, t, RWKV_N), F32)
    hm_spec = pl.BlockSpec((1, RWKV_HEADS, tm, RWKV_N), lambda bi, i: (bi, 0, i, 0))
    return pl.pallas_call(
        _rwkv_prep_kernel,
        grid=(b, t // tm),
        in_specs=in_specs,
        out_specs=[hm_spec] * 7 + [pl.BlockSpec((1, tm, c), lambda bi, i: (bi, i, 0))],
        out_shape=[hm] * 7 + [jax.ShapeDtypeStruct((b, t, c), F32)],
        compiler_params=_cparams(("parallel", "arbitrary")),
        name="rwkv_prep",
    )(h, h, rkv, rkv, rkv, rkv, rkv, rkv, mu_rkv, mu_wag, w0, w1, w2, a0, a1, a2, g1, g2, k_k, k_a)


def _split3(x):
    hi = x.astype(BF16)
    r1 = x - hi.astype(F32)
    mid = r1.astype(BF16)
    lo = (r1 - mid.astype(F32)).astype(BF16)
    return hi, mid, lo


def _bmm(a, b):
    return jnp.einsum("gmk,gkn->gmn", a.astype(BF16), b.astype(BF16), preferred_element_type=F32)


def _bmm_nt(a, b):
    return jnp.einsum("gmk,gnk->gmn", a.astype(BF16), b.astype(BF16), preferred_element_type=F32)


def _wkv_kernel(r_ref, k_ref, v_ref, lw_ref, cum_ref, kk_ref, ag_ref, rk_ref, lnw_ref, lnb_ref, y_ref,
                s_ref, st_ref):
    c = WKV_CHUNK
    n = RWKV_N
    hb, tc = r_ref.shape[1], r_ref.shape[2]
    nc = tc // c
    g = hb * nc

    @pl.when(pl.program_id(2) == 0)
    def _():
        s_ref[...] = jnp.zeros_like(s_ref)

    chunks = lambda ref: ref[0].reshape(g, c, n)
    r, k, v, lw, cum, kk, ag = (chunks(ref) for ref in (r_ref, k_ref, v_ref, lw_ref, cum_ref, kk_ref, ag_ref))
    kk = kk / jnp.maximum(jnp.sqrt(jnp.sum(kk * kk, axis=-1, keepdims=True)), 1e-12)
    dinc = jnp.exp(cum)
    dinv = jnp.exp(-cum)
    rt = r * dinc
    kt = k * dinv
    at = -kk * jnp.exp(cum - lw)
    bt = kk * ag * dinv
    d_chunk = dinc[:, c - 1:c, :]

    row = lax.broadcasted_iota(I32, (1, c, c), 1)
    col = lax.broadcasted_iota(I32, (1, c, c), 2)
    strict = row > col
    incl = row >= col
    eye = jnp.broadcast_to((row == col).astype(F32), (g, c, c))
    p = _bmm_nt(jnp.concatenate([at, rt], axis=1), jnp.concatenate([bt, kt], axis=1))
    a_ab = jnp.where(strict, p[:, :c, :c], 0.0)
    a_ak = jnp.where(strict, p[:, :c, c:], 0.0)
    a_rb = jnp.where(incl, p[:, c:, :c], 0.0)
    a_rk = jnp.where(incl, p[:, c:, c:], 0.0)
    x = _bmm(a_ab, a_ab)
    tinv = eye + a_ab
    levels = c.bit_length() - 2
    for j in range(1, levels):
        both = _bmm(jnp.concatenate([x, tinv], axis=1), x)
        x = both[:, :c]
        tinv = tinv + both[:, c:]
    tinv = tinv + _bmm(tinv, x)
    z = _bmm(a_ak, v)
    ta = _bmm(tinv, jnp.concatenate([at, z], axis=2))
    ry = _bmm(a_rb, ta)
    rp = rt + ry[:, :, :n]
    yv = ry[:, :, n:] + _bmm(a_rk, v)
    vt = _bmm_nt(eye, v)
    att = _bmm_nt(eye, at)
    tt = _bmm_nt(jnp.concatenate([att, _bmm_nt(vt, a_ak)], axis=1), tinv)
    mn = _bmm(tt, bt)
    moff = mn[:, :n].reshape(hb, nc, n, n)
    n2 = (mn[:, n:] + _bmm(vt, kt)).reshape(hb, nc, n, n)
    dch = d_chunk.reshape(hb, nc, 1, n)

    s = s_ref[...]
    for ci in range(nc):
        st_ref[:, ci] = s
        s = (s + _bmm(s, moff[:, ci]) + n2[:, ci]) * dch[:, ci]
    s_ref[...] = s

    y = _bmm_nt(rp, st_ref[...].reshape(g, n, n)) + yv
    mu = jnp.mean(y, axis=-1, keepdims=True)
    yc = y - mu
    var = jnp.mean(yc * yc, axis=-1, keepdims=True)
    y = (yc * lax.rsqrt(var + RWKV_GN_EPS)).reshape(hb, tc, n) * lnw_ref[...] + lnb_ref[...]
    bonus = jnp.sum((r * k).reshape(hb, tc, n) * rk_ref[...], axis=-1, keepdims=True)
    y = y + bonus * v.reshape(hb, tc, n)
    y_ref[0] = jnp.concatenate([y[hd] for hd in range(hb)], axis=1)


def _wkv_scan(r, k, v, lw, cum, kk, ag, r_k, lnx_w, lnx_b):
    b, hh, t, n = r.shape
    tc = min(WKV_CHUNK * WKV_CHUNKS_PER_STEP, t)
    hb = WKV_HEADS_PER_STEP
    blk = pl.BlockSpec((1, hb, tc, n), lambda bi, hi, i: (bi, hi, i, 0))
    par = pl.BlockSpec((hb, 1, n), lambda bi, hi, i: (hi, 0, 0))
    per_head = lambda arr: arr.reshape(hh, 1, n)
    return pl.pallas_call(
        _wkv_kernel,
        grid=(b, hh // hb, t // tc),
        in_specs=[blk] * 7 + [par] * 3,
        out_specs=pl.BlockSpec((1, tc, hb * n), lambda bi, hi, i: (bi, i, hi)),
        out_shape=jax.ShapeDtypeStruct((b, t, hh * n), F32),
        scratch_shapes=[pltpu.VMEM((hb, n, n), F32), pltpu.VMEM((hb, tc // WKV_CHUNK, n, n), F32)],
        compiler_params=_cparams(("parallel", "parallel", "arbitrary")),
        name="wkv7_scan",
    )(r, k, v, lw, cum, kk, ag, per_head(r_k), per_head(lnx_w), per_head(lnx_b))


def _merge_kernel(ya_ref, yr_ref, g_ref, ga_ref, gr_ref, pa_ref, pr_ref, o_ref):
    ma = _dot(ya_ref[...], pa_ref[...])
    mr = _dot(yr_ref[...] * g_ref[...], pr_ref[...])
    o_ref[...] = (_sigmoid(ga_ref[...]) * ma + _sigmoid(gr_ref[...]) * mr).astype(o_ref.dtype)


def _merge(ya, yr, g, gates, p_attn, p_rwkv, tm=256):
    m, c = ya.shape
    d = p_attn.shape[1]
    tm = min(tm, m)
    tile = pl.BlockSpec((tm, c), lambda i: (i, 0))
    return pl.pallas_call(
        _merge_kernel,
        grid=(m // tm,),
        in_specs=[tile, tile, tile,
                  pl.BlockSpec((tm, d), lambda i: (i, 0)),
                  pl.BlockSpec((tm, d), lambda i: (i, 1)),
                  pl.BlockSpec((c, d), lambda i: (0, 0)),
                  pl.BlockSpec((c, d), lambda i: (0, 0))],
        out_specs=pl.BlockSpec((tm, d), lambda i: (i, 0)),
        out_shape=jax.ShapeDtypeStruct((m, d), BF16),
        compiler_params=_cparams(("parallel",)),
        name="gated_merge",
    )(ya, yr, g, gates, gates, p_attn, p_rwkv)


def _outproj_router_kernel(alpha, mg_ref, x_ref, wo_ref, g_ref, b_ref, wr_ref, br_ref,
                           h_ref, hrow_ref, eid_ref, gate_ref):
    mix = jnp.dot(mg_ref[...], wo_ref[...], preferred_element_type=F32)
    h = _layer_norm(alpha * x_ref[...] + mix, g_ref[...], b_ref[...])
    h_ref[...] = h
    slab = h.shape[1] // LANES
    for j in range(slab):
        hrow_ref[pl.ds(j, h.shape[0], stride=slab), :] = h[:, j * LANES:(j + 1) * LANES]
    h_hi = h.astype(BF16)
    h_lo = (h - h_hi.astype(F32)).astype(BF16)
    logits = (jnp.dot(h_hi, wr_ref[0], preferred_element_type=F32)
              + jnp.dot(h_lo, wr_ref[0], preferred_element_type=F32)
              + jnp.dot(h_hi, wr_ref[1], preferred_element_type=F32)) + br_ref[...]
    lane = lax.broadcasted_iota(I32, logits.shape, 1)
    ninf = -jnp.inf
    big = jnp.int32(2 * LANES)
    glog = jnp.where(lane < N_GROUPS, logits, ninf)
    gmax = jnp.max(glog, axis=-1, keepdims=True)
    gidx = jnp.min(jnp.where(glog == gmax, lane, big), axis=-1, keepdims=True)
    gtop = 1.0 / jnp.sum(jnp.exp(glog - gmax), axis=-1, keepdims=True)
    eg = (lane - N_GROUPS) // EXPERTS_PER_GROUP
    in_group = (lane >= N_GROUPS) & (lane < N_GROUPS + N_EXPERTS) & (eg == gidx)
    el = jnp.where(in_group, logits, ninf)
    m1 = jnp.max(el, axis=-1, keepdims=True)
    i1 = jnp.min(jnp.where(el == m1, lane, big), axis=-1, keepdims=True)
    el2 = jnp.where(lane == i1, ninf, el)
    m2 = jnp.max(el2, axis=-1, keepdims=True)
    i2 = jnp.min(jnp.where(el2 == m2, lane, big), axis=-1, keepdims=True)
    t = jnp.exp(m2 - m1)
    p1 = 1.0 / (1.0 + t)
    p2 = t / (1.0 + t)
    eid_ref[...] = jnp.where(lane == 0, i1 - N_GROUPS, jnp.where(lane == 1, i2 - N_GROUPS, 0))
    gate_ref[...] = jnp.where(lane == 0, gtop * p1, jnp.where(lane == 1, gtop * p2, 0.0))


def _outproj_router(merged, x, w_o, ln_g, ln_b, w_router, b_router, alpha, tm=256):
    m, d = x.shape
    tm = min(tm, m)
    tile = pl.BlockSpec((tm, d), lambda i: (i, 0))
    vec = pl.BlockSpec((1, d), lambda i: (0, 0))
    small = pl.BlockSpec((tm, LANES), lambda i: (i, 0))
    return pl.pallas_call(
        functools.partial(_outproj_router_kernel, alpha),
        grid=(m // tm,),
        in_specs=[tile, tile, pl.BlockSpec((d, d), lambda i: (0, 0)), vec, vec,
                  pl.BlockSpec((2, d, LANES), lambda i: (0, 0, 0)), pl.BlockSpec((1, LANES), lambda i: (0, 0))],
        out_specs=[tile, pl.BlockSpec((tm * (d // LANES), LANES), lambda i: (i, 0)), small, small],
        out_shape=[jax.ShapeDtypeStruct((m, d), F32), jax.ShapeDtypeStruct((m * (d // LANES), LANES), F32),
                   jax.ShapeDtypeStruct((m, LANES), I32), jax.ShapeDtypeStruct((m, LANES), F32)],
        compiler_params=_cparams(("parallel",)),
        name="outproj_ln_router",
    )(merged, x, w_o, ln_g.reshape(1, d), ln_b.reshape(1, d), w_router, b_router)


def _onehots(eid_ref):
    tm = eid_ref.shape[1]
    e_iota = lax.broadcasted_iota(I32, (N_EXPERTS, tm), 0)
    oh0 = (eid_ref[0:1, :] == e_iota).astype(F32)
    oh1 = (eid_ref[1:2, :] == e_iota).astype(F32)
    return oh0, oh1


def _count_kernel(eid_ref, cnt_ref):
    @pl.when(pl.program_id(0) == 0)
    def _():
        cnt_ref[...] = jnp.zeros_like(cnt_ref)

    oh0, oh1 = _onehots(eid_ref)
    cnt_ref[...] += jnp.sum(oh0 + oh1, axis=1, keepdims=True)


def _slot_kernel(eid_ref, pstart_ref, dest_ref, run_ref):
    @pl.when(pl.program_id(0) == 0)
    def _():
        run_ref[...] = jnp.zeros_like(run_ref)

    tm = eid_ref.shape[1]
    oh0, oh1 = _onehots(eid_ref)
    both = oh0 + oh1
    earlier = (lax.broadcasted_iota(I32, (tm, tm), 0) < lax.broadcasted_iota(I32, (tm, tm), 1)).astype(BF16)
    pre = jnp.dot(both.astype(BF16), earlier, preferred_element_type=F32)
    base = pre + run_ref[...] + pstart_ref[...]
    dest_ref[0:1, :] = jnp.sum(oh0 * base, axis=0, keepdims=True).astype(I32)
    dest_ref[1:2, :] = jnp.sum(oh1 * base, axis=0, keepdims=True).astype(I32)
    run_ref[...] += jnp.sum(both, axis=1, keepdims=True)


def _expert_counts(eid_t):
    m = eid_t.shape[1]
    tm = min(ROUTE_TILE, m)
    return pl.pallas_call(
        _count_kernel,
        grid=(m // tm,),
        in_specs=[pl.BlockSpec((TOP_K, tm), lambda i: (0, i))],
        out_specs=pl.BlockSpec((N_EXPERTS, 1), lambda i: (0, 0)),
        out_shape=jax.ShapeDtypeStruct((N_EXPERTS, 1), F32),
        compiler_params=_cparams(("arbitrary",)),
        name="expert_counts",
    )(eid_t)


def _expert_slots(eid_t, pstart):
    m = eid_t.shape[1]
    tm = min(ROUTE_TILE, m)
    return pl.pallas_call(
        _slot_kernel,
        grid=(m // tm,),
        in_specs=[pl.BlockSpec((TOP_K, tm), lambda i: (0, i)),
                  pl.BlockSpec((N_EXPERTS, 1), lambda i: (0, 0))],
        out_specs=pl.BlockSpec((TOP_K, tm), lambda i: (0, i)),
        out_shape=jax.ShapeDtypeStruct((TOP_K, m), I32),
        scratch_shapes=[pltpu.VMEM((N_EXPERTS, 1), F32)],
        compiler_params=_cparams(("arbitrary",)),
        name="expert_slots",
    )(eid_t, pstart)


def _row_slab(ref, row, slab):
    return ref.at[pl.ds(pl.multiple_of(row * slab, slab), slab), :]


def _dispatch_kernel(dest_ref, h_ref, xb_in_ref, xb_ref, sem):
    del xb_in_ref
    tm = DISPATCH_TILE
    slab = h_ref.shape[0] // tm
    t0 = pl.program_id(0) * tm

    def copy(n, k):
        return pltpu.make_async_copy(_row_slab(h_ref, n, slab),
                                     _row_slab(xb_ref, dest_ref[TOP_K * (t0 + n) + k], slab), sem)

    def start(n, carry):
        for k in range(TOP_K):
            copy(n, k).start()
        return carry

    lax.fori_loop(0, tm, start, 0, unroll=8)
    for k in range(TOP_K):
        pltpu.make_async_copy(h_ref, xb_ref.at[pl.ds(0, h_ref.shape[0]), :], sem).wait()


DISPATCH_TILE = 512


def _dispatch(dest_flat, h_rows, m, n_rows):
    slab = h_rows.shape[0] // m
    assert m % DISPATCH_TILE == 0
    grid_spec = pltpu.PrefetchScalarGridSpec(
        num_scalar_prefetch=1,
        grid=(m // DISPATCH_TILE,),
        in_specs=[pl.BlockSpec((DISPATCH_TILE * slab, LANES), lambda i, dr: (i, 0)),
                  pl.BlockSpec(memory_space=pl.ANY)],
        out_specs=pl.BlockSpec(memory_space=pl.ANY),
        scratch_shapes=[pltpu.SemaphoreType.DMA(())],
    )
    return pl.pallas_call(
        _dispatch_kernel,
        grid_spec=grid_spec,
        out_shape=jax.ShapeDtypeStruct((n_rows * slab, LANES), F32),
        input_output_aliases={2: 0},
        compiler_params=pltpu.CompilerParams(dimension_semantics=("arbitrary",), has_side_effects=True),
        name="moe_dispatch",
    )(dest_flat, h_rows, jnp.zeros((n_rows * slab, LANES), F32))


def _expert_kernel(item_e_ref, item_row_ref, item_nsub_ref, *refs):
    wg_refs = refs[:MOE_GU_PIECES]
    wu_refs = refs[MOE_GU_PIECES:2 * MOE_GU_PIECES]
    wd_refs = refs[2 * MOE_GU_PIECES:2 * MOE_GU_PIECES + MOE_DOWN_PIECES]
    xb_ref, yb_ref, xin_ref, yout_ref, x_ref, acc_ref, wgu_ref, wdb_ref, sem_in, sem_out = (
        refs[2 * MOE_GU_PIECES + MOE_DOWN_PIECES:])
    del item_e_ref
    it = pl.program_id(0)
    f = pl.program_id(1)
    n_items = pl.num_programs(0)
    nf = pl.num_programs(1)
    kp = wg_refs[0].shape[1]
    kc = kp * MOE_GU_PIECES
    ff = wg_refs[0].shape[2]
    fp = wd_refs[0].shape[1]
    fc = fp * MOE_DOWN_PIECES
    nsub = item_nsub_ref[it]
    row0 = item_row_ref[it]
    nxt = jnp.minimum(it + 1, n_items - 1)
    nsub_next = jnp.where(it + 1 < n_items, item_nsub_ref[nxt], 0)
    prv = jnp.maximum(it - 1, 0)
    nsub_prev = jnp.where(it > 0, item_nsub_ref[prv], 0)
    rb = MOE_ROWS
    slab = x_ref.shape[0] * kc // LANES
    blk = rb * slab

    def stage_rows(ref, s):
        return ref.at[pl.ds(pl.multiple_of(s * blk, blk), blk), :]

    def hbm_rows(ref, item_row, s):
        return ref.at[pl.ds(pl.multiple_of((item_row + s * rb) * slab, blk), blk), :]

    def in_copy(item_row, s):
        return pltpu.make_async_copy(hbm_rows(xb_ref, item_row, s), stage_rows(xin_ref, s), sem_in)

    def out_copy(item_row, s):
        return pltpu.make_async_copy(stage_rows(yout_ref, s), hbm_rows(yb_ref, item_row, s), sem_out)

    def loop(n, fn):
        def body(s, carry):
            fn(s)
            return carry
        lax.fori_loop(0, n, body, 0)

    @pl.when(nsub > 0)
    def _():
        @pl.when(f == 0)
        def _():
            @pl.when(it == 0)
            def _():
                x_ref[...] = jnp.zeros_like(x_ref)
                loop(nsub, lambda s: in_copy(row0, s).start())
            loop(nsub, lambda s: in_copy(row0, s).wait())

            def to_matrix(s):
                rows = pl.ds(pl.multiple_of(s * rb, rb), rb)
                base = pl.multiple_of(s * blk, blk)
                for j in range(slab):
                    c0 = (j * LANES) % kc
                    x_ref[(j * LANES) // kc, rows, c0:c0 + LANES] = (
                        xin_ref[pl.ds(base + j, rb, stride=slab), :].astype(BF16))
            loop(nsub, to_matrix)
            loop(nsub_next, lambda s: in_copy(item_row_ref[nxt], s).start())

        for q in range(MOE_GU_PIECES):
            wgu_ref[q * kp:(q + 1) * kp, :ff] = wg_refs[q][0].astype(BF16)
            wgu_ref[q * kp:(q + 1) * kp, ff:] = wu_refs[q][0].astype(BF16)
        for q in range(MOE_DOWN_PIECES):
            wdb_ref[pl.ds(pl.multiple_of(f * fc + q * fp, 2 * SUBLANES), fp), :] = wd_refs[q][0].astype(BF16)

        def gate_up(start, size):
            rows = pl.ds(pl.multiple_of(start, rb), size)
            part = jnp.dot(x_ref[f, rows, :], wgu_ref[...], preferred_element_type=F32)
            acc_ref[rows, :] = jnp.where(f > 0, acc_ref[rows, :], 0.0) + part

        gate_up(0, 2 * rb)

        def pair(p, carry):
            gate_up(p * (2 * rb), 2 * rb)
            return carry
        lax.fori_loop(1, nsub // 2, pair, 0)

        @pl.when((nsub % 2 == 1) & (nsub > 1))
        def _():
            gate_up((nsub - 1) * rb, rb)

        @pl.when(f == nf - 1)
        def _():
            loop(nsub_prev, lambda s: out_copy(item_row_ref[prv], s).wait())

            def down(start, size):
                rows = pl.ds(pl.multiple_of(start, rb), size)
                gate = acc_ref[rows, :ff]
                act = (gate * _sigmoid(gate) * acc_ref[rows, ff:]).astype(BF16)
                y = jnp.dot(act, wdb_ref[...], preferred_element_type=F32)
                base = pl.multiple_of(start * slab, blk)
                for j in range(slab):
                    yout_ref[pl.ds(base + j, size, stride=slab), :] = y[:, j * LANES:(j + 1) * LANES]

            down(0, 2 * rb)

            def down_pair(p, carry):
                down(p * (2 * rb), 2 * rb)
                return carry
            lax.fori_loop(1, nsub // 2, down_pair, 0)

            @pl.when((nsub % 2 == 1) & (nsub > 1))
            def _():
                down((nsub - 1) * rb, rb)

            loop(nsub, lambda s: out_copy(row0, s).start())

            @pl.when(nsub_next == 0)
            def _():
                loop(nsub, lambda s: out_copy(row0, s).wait())


def _experts(item_e, item_row, item_nsub, w_gate, w_up, w_down, xb):
    d = w_gate.shape[1]
    ff = w_gate.shape[2]
    nk = MOE_K_STEPS
    kc = d // nk
    fc = ff // nk
    assert d % nk == 0 and ff % nk == 0 and kc % LANES == 0 and fc % (2 * SUBLANES) == 0 and ff % LANES == 0
    n_items = item_e.shape[0]
    rows_max = MOE_ROWS * MOE_SUBS_PER_ITEM
    slab = d // LANES

    kp = kc // MOE_GU_PIECES
    fp = fc // MOE_DOWN_PIECES
    assert kc % MOE_GU_PIECES == 0 and fc % MOE_DOWN_PIECES == 0 and kp % (2 * SUBLANES) == 0 and fp % (2 * SUBLANES) == 0

    def piece(rows, cols, pieces, q):
        def index(it, k, ie, ir, ns):
            return ie[it], jnp.where(ns[it] > 0, k, nk - 1) * pieces + q, 0
        return pl.BlockSpec((1, rows, cols), index)

    grid_spec = pltpu.PrefetchScalarGridSpec(
        num_scalar_prefetch=3,
        grid=(n_items, nk),
        in_specs=[
            *[piece(kp, ff, MOE_GU_PIECES, q) for q in range(MOE_GU_PIECES)],
            *[piece(kp, ff, MOE_GU_PIECES, q) for q in range(MOE_GU_PIECES)],
            *[piece(fp, d, MOE_DOWN_PIECES, q) for q in range(MOE_DOWN_PIECES)],
            pl.BlockSpec(memory_space=pl.ANY),
        ],
        out_specs=pl.BlockSpec(memory_space=pl.ANY),
        scratch_shapes=[
            pltpu.VMEM((rows_max * slab, LANES), F32),
            pltpu.VMEM((rows_max * slab, LANES), F32),
            pltpu.VMEM((nk, rows_max, kc), BF16),
            pltpu.VMEM((rows_max, 2 * ff), F32),
            pltpu.VMEM((kc, 2 * ff), BF16),
            pltpu.VMEM((ff, d), BF16),
            pltpu.SemaphoreType.DMA(()),
            pltpu.SemaphoreType.DMA(()),
        ],
    )
    return pl.pallas_call(
        _expert_kernel,
        grid_spec=grid_spec,
        out_shape=jax.ShapeDtypeStruct(xb.shape, F32),
        input_output_aliases={3 + 2 * MOE_GU_PIECES + MOE_DOWN_PIECES: 0},
        compiler_params=pltpu.CompilerParams(dimension_semantics=("arbitrary", "arbitrary"),
                                             vmem_limit_bytes=VMEM_LIMIT_BYTES, has_side_effects=True),
        name="moe_experts",
    )(item_e, item_row, item_nsub, *([w_gate] * MOE_GU_PIECES), *([w_up] * MOE_GU_PIECES),
      *([w_down] * MOE_DOWN_PIECES), xb)


COMBINE_TILE = 256


def _combine_kernel(alpha, dest_ref, yb_ref, h_ref, gate_ref, g_ref, b_ref, o_ref, buf_ref, sem):
    tm = COMBINE_TILE
    i = pl.program_id(0)
    slab = buf_ref.shape[2] // tm

    def issue(tile, half):
        def start(n, carry):
            for k in range(TOP_K):
                pltpu.make_async_copy(_row_slab(yb_ref, dest_ref[TOP_K * (tile * tm + n) + k], slab),
                                      _row_slab(buf_ref.at[half, k], n, slab), sem.at[half]).start()
            return carry
        lax.fori_loop(0, tm, start, 0, unroll=8)

    @pl.when(i == 0)
    def _():
        issue(0, 0)

    @pl.when(i + 1 < pl.num_programs(0))
    def _():
        issue(i + 1, (i + 1) % 2)

    half = i % 2
    for k in range(TOP_K):
        pltpu.make_async_copy(yb_ref.at[pl.ds(0, buf_ref.shape[2]), :], buf_ref.at[half, k], sem.at[half]).wait()
    gate = gate_ref[...]
    g0 = gate[:, 0:1]
    g1 = gate[:, 1:2]
    col = lambda k, j: buf_ref[half, k, pl.ds(j, tm, stride=slab), :]
    ff = jnp.concatenate([g0 * col(0, j) + g1 * col(1, j) for j in range(slab)], axis=1)
    o_ref[...] = _layer_norm(alpha * h_ref[...] + ff, g_ref[...], b_ref[...])


def _combine(dest_flat, yb, h, gates, ln_g, ln_b, alpha):
    m, d = h.shape
    tm = COMBINE_TILE
    assert m % tm == 0
    grid_spec = pltpu.PrefetchScalarGridSpec(
        num_scalar_prefetch=1,
        grid=(m // tm,),
        in_specs=[pl.BlockSpec(memory_space=pl.ANY),
                  pl.BlockSpec((tm, d), lambda i, dr: (i, 0)),
                  pl.BlockSpec((tm, LANES), lambda i, dr: (i, 0)),
                  pl.BlockSpec((1, d), lambda i, dr: (0, 0)),
                  pl.BlockSpec((1, d), lambda i, dr: (0, 0))],
        out_specs=pl.BlockSpec((tm, d), lambda i, dr: (i, 0)),
        scratch_shapes=[pltpu.VMEM((2, TOP_K, tm * (d // LANES), LANES), F32), pltpu.SemaphoreType.DMA((2,))],
    )
    return pl.pallas_call(
        functools.partial(_combine_kernel, alpha),
        grid_spec=grid_spec,
        out_shape=jax.ShapeDtypeStruct((m, d), F32),
        compiler_params=_cparams(("arbitrary",)),
        name="moe_combine_ln",
    )(dest_flat, yb, h, gates, ln_g.reshape(1, d), ln_b.reshape(1, d))


def _moe_tables(counts):
    nsub_e = (counts + MOE_ROWS - 1) // MOE_ROWS
    pstart = (jnp.cumsum(nsub_e) - nsub_e) * MOE_ROWS
    nitem_e = (nsub_e + MOE_SUBS_PER_ITEM - 1) // MOE_SUBS_PER_ITEM
    item_end = jnp.cumsum(nitem_e)
    return nsub_e, pstart, nitem_e, item_end


def _moe(h, h_rows, eid, gates, w_gate, w_up, w_down, ln_g, ln_b, alpha):
    m, d = h.shape
    n_assign = m * TOP_K
    n_blocks = (n_assign + N_EXPERTS * (MOE_ROWS - 1) + MOE_ROWS - 1) // MOE_ROWS
    n_rows = n_blocks * MOE_ROWS
    n_items = N_EXPERTS + n_assign // (MOE_ROWS * MOE_SUBS_PER_ITEM)

    eid_t = eid[:, :TOP_K].T
    counts = _expert_counts(eid_t)[:, 0].astype(I32)
    nsub_e, pstart, nitem_e, item_end = _moe_tables(counts)
    dest_t = _expert_slots(eid_t, pstart.astype(F32).reshape(N_EXPERTS, 1))
    dest_flat = dest_t.T.reshape(-1)

    it = jnp.arange(n_items, dtype=I32)
    total_items = item_end[-1]
    item_e = jnp.minimum(jnp.sum(item_end[None, :] <= it[:, None], axis=1).astype(I32), N_EXPERTS - 1)
    j = it - (item_end - nitem_e)[item_e]
    used = it < total_items
    last_e = item_e[jnp.maximum(total_items - 1, 0)]
    item_nsub = jnp.where(used, jnp.clip(nsub_e[item_e] - j * MOE_SUBS_PER_ITEM, 0, MOE_SUBS_PER_ITEM), 0).astype(I32)
    item_row = jnp.where(used, pstart[item_e] + j * (MOE_ROWS * MOE_SUBS_PER_ITEM), 0).astype(I32)
    item_e = jnp.where(used, item_e, last_e).astype(I32)

    xb = _dispatch(dest_flat, h_rows, m, n_rows)
    yb = _experts(item_e, item_row, item_nsub, w_gate, w_up, w_down, xb)
    return _combine(dest_flat, yb, h, gates, ln_g, ln_b, alpha)


def _pad_cols(w, n):
    return jnp.pad(w, ((0, 0), (0, n - w.shape[1])))


def _pad_rows(w, n):
    return jnp.pad(w, ((0, n - w.shape[0]), (0, 0)))


def kernel(x, w_in, attn_sinks, rw_mu_rkv, rw_mu_wag, rw_w0, rw_w1, rw_w2, rw_a0, rw_a1, rw_a2, rw_g1, rw_g2, rw_k_k, rw_k_a, rw_r_k, rw_lnx_w, rw_lnx_b, p_attn, p_rwkv, w_o, ln1_g, ln1_b, w_group, b_group, w_expert, b_expert, w_gate, w_up, w_down, ln2_g, ln2_b):
    b, t, d = x.shape
    depth = w_in.shape[0]
    m = b * t
    alpha = (2.0 * depth) ** 0.25
    cosb, sinb = _rope_tables(t)
    qkv_w = ATTN_Q_W + 2 * ATTN_KV_W
    rkv_w = 3 * RWKV_W
    h = x
    for l in range(depth):
        hf = h.reshape(m, d)
        hb = hf.astype(BF16)
        qkv = _matmul_cols(hb, w_in[l], 0, qkv_w, F32)
        rkv = _matmul_cols(hb, w_in[l], qkv_w, rkv_w, F32)
        gates = _matmul_cols(hb, w_in[l], qkv_w + rkv_w, 2 * d, F32)

        y_a = _attention(qkv.reshape(b, t, qkv_w), attn_sinks[l], cosb, sinb)

        lora = lambda w, n: _pad_cols(w, n).astype(BF16)
        lorb = lambda w, n: _pad_rows(w, n).astype(BF16)
        n_w = -(-rw_w1.shape[2] // LANES) * LANES
        n_a = -(-rw_a1.shape[2] // LANES) * LANES
        n_g = -(-rw_g1.shape[2] // LANES) * LANES
        r_, k_, v_, lw_, cum_, kk_, ag_, g_ = _rwkv_prep(
            h, rkv.reshape(b, t, rkv_w), rw_mu_rkv[l], rw_mu_wag[l], rw_w0[l],
            lora(rw_w1[l], n_w), lorb(rw_w2[l], n_w), rw_a0[l], lora(rw_a1[l], n_a), lorb(rw_a2[l], n_a),
            lora(rw_g1[l], n_g), lorb(rw_g2[l], n_g), rw_k_k[l], rw_k_a[l])
        y_r = _wkv_scan(r_, k_, v_, lw_, cum_, kk_, ag_, rw_r_k[l], rw_lnx_w[l], rw_lnx_b[l])

        merged = _merge(y_a.reshape(m, ATTN_Q_W), y_r.reshape(m, RWKV_W), g_.reshape(m, RWKV_W), gates,
                        p_attn[l].astype(BF16), p_rwkv[l].astype(BF16))
        w_router = _pad_cols(jnp.concatenate([w_group[l], w_expert[l]], axis=1), LANES)
        w_router_hi = w_router.astype(BF16)
        w_router = jnp.stack([w_router_hi, (w_router - w_router_hi.astype(F32)).astype(BF16)])
        b_router = _pad_cols(jnp.concatenate([b_group[l], b_expert[l]])[None, :], LANES)
        h1, h1_rows, eid, gate = _outproj_router(merged, hf, w_o[l].astype(BF16), ln1_g[l], ln1_b[l],
                                        w_router, b_router, alpha)
        h2 = _moe(h1, h1_rows, eid, gate, w_gate[l], w_up[l], w_down[l], ln2_g[l], ln2_b[l], alpha)
        h = h2.reshape(b, t, d)
    return h
```

```python
import functools

import jax
import jax.numpy as jnp
from jax import lax
from jax.experimental import pallas as pl
from jax.experimental.pallas import tpu as pltpu

F32 = jnp.float32
BF16 = jnp.bfloat16
I32 = jnp.int32

HEAD_DIM = 64
ATTN_Q_HEADS = 16
ATTN_KV_HEADS = 4
ATTN_GROUP = ATTN_Q_HEADS // ATTN_KV_HEADS
ATTN_Q_W = ATTN_Q_HEADS * HEAD_DIM
ATTN_KV_W = ATTN_KV_HEADS * HEAD_DIM
WINDOW = 128
ROPE_THETA = 10000.0
RWKV_HEADS = 16
RWKV_N = 64
RWKV_W = RWKV_HEADS * RWKV_N
RWKV_GN_EPS = 64e-5
N_GROUPS = 8
EXPERTS_PER_GROUP = 8
N_EXPERTS = N_GROUPS * EXPERTS_PER_GROUP
TOP_K = 2
LN_EPS = 1e-5

LANES = 128
SUBLANES = 8
VMEM_LIMIT_BYTES = 56 * 1024 * 1024

WKV_CHUNK = 64
WKV_CHUNKS_PER_STEP = 4
WKV_HEADS_PER_STEP = 8
MOE_ROWS = 128
MOE_SUBS_PER_ITEM = 4
MOE_K_STEPS = 4
MOE_GU_PIECES = 1
MOE_DOWN_PIECES = 1
ROUTE_TILE = 512


def _cparams(sem, vmem=VMEM_LIMIT_BYTES):
    return pltpu.CompilerParams(dimension_semantics=sem, vmem_limit_bytes=vmem)


def _sigmoid(x):
    return 1.0 / (1.0 + jnp.exp(-x))


def _dot(a, b):
    return jnp.dot(a.astype(BF16), b.astype(BF16), preferred_element_type=F32)


def _dot_nt(a, b):
    return lax.dot_general(a.astype(BF16), b.astype(BF16), (((1,), (1,)), ((), ())),
                           preferred_element_type=F32)


def _layer_norm(t, g, b):
    mu = jnp.mean(t, axis=-1, keepdims=True)
    d = t - mu
    var = jnp.mean(d * d, axis=-1, keepdims=True)
    return d * lax.rsqrt(var + LN_EPS) * g + b


def _matmul_kernel(a_ref, b_ref, o_ref, bb_ref):
    @pl.when(pl.program_id(1) == 0)
    def _():
        bb_ref[...] = b_ref[...].astype(BF16)

    o_ref[...] = jnp.dot(a_ref[...], bb_ref[...], preferred_element_type=F32).astype(o_ref.dtype)


def _matmul_cols(a, b, col0, ncols, out_dtype, tm=2048, tn=512):
    m, k = a.shape
    tm = min(tm, m)
    cb = col0 // tn
    assert col0 % tn == 0 and ncols % tn == 0 and m % tm == 0
    return pl.pallas_call(
        _matmul_kernel,
        grid=(ncols // tn, m // tm),
        in_specs=[pl.BlockSpec((tm, k), lambda j, i: (i, 0)),
                  pl.BlockSpec((k, tn), lambda j, i: (0, j + cb))],
        out_specs=pl.BlockSpec((tm, tn), lambda j, i: (i, j)),
        out_shape=jax.ShapeDtypeStruct((m, ncols), out_dtype),
        scratch_shapes=[pltpu.VMEM((k, tn), BF16)],
        compiler_params=_cparams(("arbitrary", "arbitrary")),
        name="inproj_matmul",
    )(a, b)


def _rope(x, cosb, sinb):
    half = HEAD_DIM // 2
    lane = lax.broadcasted_iota(I32, cosb.shape, 1)
    first_half = (lane % HEAD_DIM) < half
    outs = []
    for g in range(x.shape[1] // LANES):
        xg = x[:, g * LANES:(g + 1) * LANES]
        partner = jnp.where(first_half, pltpu.roll(xg, LANES - half, axis=1), pltpu.roll(xg, half, axis=1))
        outs.append(xg * cosb + partner * sinb)
    return outs


def _attn_kernel(sinks_ref, q_ref, kc_ref, kp_ref, vc_ref, vp_ref, cosc_ref, sinc_ref, cosp_ref, sinp_ref, o_ref):
    blk = pl.program_id(1)
    tq = q_ref.shape[1]
    qg = _rope(q_ref[0], cosc_ref[...], sinc_ref[...])
    kcg = _rope(kc_ref[0], cosc_ref[...], sinc_ref[...])
    kpg = _rope(kp_ref[0], cosp_ref[...], sinp_ref[...])
    vc = vc_ref[0]
    vp = vp_ref[0]

    def head(groups, h):
        g = groups[h // 2]
        return g[:, (h % 2) * HEAD_DIM:(h % 2 + 1) * HEAD_DIM]

    rows = ATTN_GROUP * tq
    qi = lax.broadcasted_iota(I32, (rows, 2 * tq), 0) % tq
    kj = lax.broadcasted_iota(I32, (rows, 2 * tq), 1)
    dist = qi + tq - kj
    valid = (dist >= 0) & (dist < WINDOW) & ((blk > 0) | (kj >= tq))
    rid = lax.broadcasted_iota(I32, (rows, 1), 0) // tq
    scale = HEAD_DIM ** -0.5
    for kvh in range(ATTN_KV_HEADS):
        qh = jnp.concatenate([head(qg, kvh * ATTN_GROUP + g) for g in range(ATTN_GROUP)], axis=0)
        kw = jnp.concatenate([head(kpg, kvh), head(kcg, kvh)], axis=0)
        vw = jnp.concatenate([vp[:, kvh * HEAD_DIM:(kvh + 1) * HEAD_DIM],
                              vc[:, kvh * HEAD_DIM:(kvh + 1) * HEAD_DIM]], axis=0)
        s = _dot_nt(qh, kw) * scale
        s = jnp.where(valid, s, -jnp.inf)
        sink = jnp.zeros((rows, 1), F32)
        for g in range(ATTN_GROUP):
            sink = jnp.where(rid == g, sinks_ref[kvh * ATTN_GROUP + g], sink)
        m = jnp.maximum(jnp.max(s, axis=-1, keepdims=True), sink)
        e = jnp.exp(s - m)
        denom = jnp.sum(e, axis=-1, keepdims=True) + jnp.exp(sink - m)
        o = _dot(e, vw) / denom
        for g in range(ATTN_GROUP):
            hq = kvh * ATTN_GROUP + g
            o_ref[0, :, hq * HEAD_DIM:(hq + 1) * HEAD_DIM] = o[g * tq:(g + 1) * tq]


def _attention(qkv, sinks, cosb, sinb):
    b, t, _ = qkv.shape
    tq = WINDOW
    nb = t // tq
    kcol = ATTN_Q_W // ATTN_KV_W
    prev = lambda i: jnp.maximum(i - 1, 0)
    grid_spec = pltpu.PrefetchScalarGridSpec(
        num_scalar_prefetch=0,
        grid=(b, nb),
        in_specs=[
            pl.BlockSpec(memory_space=pltpu.SMEM),
            pl.BlockSpec((1, tq, ATTN_Q_W), lambda bi, i: (bi, i, 0)),
            pl.BlockSpec((1, tq, ATTN_KV_W), lambda bi, i: (bi, i, kcol)),
            pl.BlockSpec((1, tq, ATTN_KV_W), lambda bi, i: (bi, prev(i), kcol)),
            pl.BlockSpec((1, tq, ATTN_KV_W), lambda bi, i: (bi, i, kcol + 1)),
            pl.BlockSpec((1, tq, ATTN_KV_W), lambda bi, i: (bi, prev(i), kcol + 1)),
            pl.BlockSpec((tq, LANES), lambda bi, i: (i, 0)),
            pl.BlockSpec((tq, LANES), lambda bi, i: (i, 0)),
            pl.BlockSpec((tq, LANES), lambda bi, i: (prev(i), 0)),
            pl.BlockSpec((tq, LANES), lambda bi, i: (prev(i), 0)),
        ],
        out_specs=pl.BlockSpec((1, tq, ATTN_Q_W), lambda bi, i: (bi, i, 0)),
    )
    return pl.pallas_call(
        _attn_kernel,
        grid_spec=grid_spec,
        out_shape=jax.ShapeDtypeStruct((b, t, ATTN_Q_W), F32),
        compiler_params=_cparams(("parallel", "arbitrary")),
        name="swa_attention",
    )(sinks, qkv, qkv, qkv, qkv, qkv, cosb, sinb, cosb, sinb)


def _rope_tables(t):
    inv = 1.0 / (ROPE_THETA ** (jnp.arange(0, HEAD_DIM, 2, dtype=F32) / HEAD_DIM))
    ang = jnp.arange(t, dtype=F32)[:, None] * inv[None, :]
    cos, sin = jnp.cos(ang), jnp.sin(ang)
    reps = LANES // HEAD_DIM
    cosb = jnp.tile(jnp.concatenate([cos, cos], axis=-1), (1, reps))
    sinb = jnp.tile(jnp.concatenate([-sin, sin], axis=-1), (1, reps))
    return cosb, sinb


def _rwkv_prep_kernel(h_ref, hp_ref, r_ref, k_ref, v_ref, rp_ref, kp_ref, vp_ref,
                      mu_rkv_ref, mu_wag_ref, w0_ref, w1_ref, w2_ref, a0_ref, a1_ref, a2_ref,
                      g1_ref, g2_ref, kk_ref, ka_ref,
                      ro_ref, ko_ref, vo_ref, lwo_ref, cumo_ref, kko_ref, ago_ref, go_ref):
    first = pl.program_id(1) == 0
    last_row = SUBLANES - 1

    def shifted(cur, prev_ref):
        prev_row = jnp.where(first, 0.0, prev_ref[0, last_row:last_row + 1, :])
        rowid = lax.broadcasted_iota(I32, cur.shape, 0)
        return jnp.where(rowid == 0, prev_row, pltpu.roll(cur, 1, axis=0))

    h = h_ref[0]
    xx = shifted(h, hp_ref) - h
    xw = h + xx * mu_wag_ref[0:1, :]
    xa = h + xx * mu_wag_ref[1:2, :]
    xg = h + xx * mu_wag_ref[2:3, :]
    w_raw = w0_ref[...] + _dot(jnp.tanh(_dot(xw, w1_ref[...])), w2_ref[...])
    neg = -w_raw
    softplus = jnp.maximum(neg, 0.0) + jnp.log1p(jnp.exp(-jnp.abs(neg)))
    w = -softplus - 0.5
    lw = -jnp.exp(w)
    tm = lw.shape[0]
    row = lax.broadcasted_iota(I32, (tm, tm), 0)
    col = lax.broadcasted_iota(I32, (tm, tm), 1)
    tri = ((row >= col) & (row // WKV_CHUNK == col // WKV_CHUNK)).astype(BF16)
    cum = sum(jnp.dot(tri, piece, preferred_element_type=F32) for piece in _split3(lw))
    ag = _sigmoid(a0_ref[...] + _dot(_dot(xa, a1_ref[...]), a2_ref[...]))
    go_ref[0] = _dot(_sigmoid(_dot(xg, g1_ref[...])), g2_ref[...])

    r = r_ref[0]
    k = k_ref[0]
    v = v_ref[0]
    r = r + (shifted(r, rp_ref) - r) * mu_rkv_ref[0:1, :]
    k = k + (shifted(k, kp_ref) - k) * mu_rkv_ref[1:2, :]
    v = v + (shifted(v, vp_ref) - v) * mu_rkv_ref[2:3, :]
    kk = k * kk_ref[...]
    k = k * (1.0 + (ag - 1.0) * ka_ref[...])
    nchunk = tm // WKV_CHUNK
    for hd in range(RWKV_HEADS):
        sl = slice(hd * RWKV_N, (hd + 1) * RWKV_N)
        for ref, val in ((ro_ref, r), (ko_ref, k), (vo_ref, v), (lwo_ref, lw), (cumo_ref, cum), (kko_ref, kk),
                         (ago_ref, ag)):
            ref[0, :, hd] = val[:, sl].reshape(nchunk, WKV_CHUNK, RWKV_N)


def _rwkv_prep(h, rkv, mu_rkv, mu_wag, w0, w1, w2, a0, a1, a2, g1, g2, k_k, k_a, tm=256):
    b, t, d = h.shape
    tm = min(tm, t)
    c = RWKV_W
    spt = tm // SUBLANES
    prevblk = lambda i: jnp.maximum(i * spt - 1, 0)
    full = lambda arr: pl.BlockSpec(arr.shape, lambda bi, i: (0,) * arr.ndim)
    row = lambda arr: arr.reshape(1, -1)
    w0, a0, k_k, k_a = row(w0), row(a0), row(k_k), row(k_a)
    in_specs = [
        pl.BlockSpec((1, tm, d), lambda bi, i: (bi, i, 0)),
        pl.BlockSpec((1, SUBLANES, d), lambda bi, i: (bi, prevblk(i), 0)),
        pl.BlockSpec((1, tm, c), lambda bi, i: (bi, i, 0)),
        pl.BlockSpec((1, tm, c), lambda bi, i: (bi, i, 1)),
        pl.BlockSpec((1, tm, c), lambda bi, i: (bi, i, 2)),
        pl.BlockSpec((1, SUBLANES, c), lambda bi, i: (bi, prevblk(i), 0)),
        pl.BlockSpec((1, SUBLANES, c), lambda bi, i: (bi, prevblk(i), 1)),
        pl.BlockSpec((1, SUBLANES, c), lambda bi, i: (bi, prevblk(i), 2)),
    ] + [full(a) for a in (mu_rkv, mu_wag, w0, w1, w2, a0, a1, a2, g1, g2, k_k, k_a)]
    assert tm % WKV_CHUNK == 0
    hm = jax.ShapeDtypeStruct((b, t // WKV_CHUNK, RWKV_HEADS, WKV_CHUNK, RWKV_N), F32)
    hm_spec = pl.BlockSpec((1, tm // WKV_CHUNK, RWKV_HEADS, WKV_CHUNK, RWKV_N), lambda bi, i: (bi, i, 0, 0, 0))
    return pl.pallas_call(
        _rwkv_prep_kernel,
        grid=(b, t // tm),
        in_specs=in_specs,
        out_specs=[hm_spec] * 7 + [pl.BlockSpec((1, tm, c), lambda bi, i: (bi, i, 0))],
        out_shape=[hm] * 7 + [jax.ShapeDtypeStruct((b, t, c), F32)],
        compiler_params=_cparams(("parallel", "arbitrary")),
        name="rwkv_prep",
    )(h, h, rkv, rkv, rkv, rkv, rkv, rkv, mu_rkv, mu_wag, w0, w1, w2, a0, a1, a2, g1, g2, k_k, k_a)


def _split3(x):
    hi = x.astype(BF16)
    r1 = x - hi.astype(F32)
    mid = r1.astype(BF16)
    lo = (r1 - mid.astype(F32)).astype(BF16)
    return hi, mid, lo


def _bmm(a, b):
    return jnp.einsum("gmk,gkn->gmn", a.astype(BF16), b.astype(BF16), preferred_element_type=F32)


def _bmm_nt(a, b):
    return jnp.einsum("gmk,gnk->gmn", a.astype(BF16), b.astype(BF16), preferred_element_type=F32)


def _wkv_kernel(r_ref, k_ref, v_ref, lw_ref, cum_ref, kk_ref, ag_ref, rk_ref, lnw_ref, lnb_ref, y_ref,
                s_ref, st_ref):
    c = WKV_CHUNK
    n = RWKV_N
    nc, hb = r_ref.shape[1], r_ref.shape[2]
    tc = nc * c
    g = nc * hb

    @pl.when(pl.program_id(2) == 0)
    def _():
        s_ref[...] = jnp.zeros_like(s_ref)

    chunks = lambda ref: ref[0].reshape(g, c, n)
    r, k, v, lw, cum, kk, ag = (chunks(ref) for ref in (r_ref, k_ref, v_ref, lw_ref, cum_ref, kk_ref, ag_ref))
    kk = kk / jnp.maximum(jnp.sqrt(jnp.sum(kk * kk, axis=-1, keepdims=True)), 1e-12)
    dinc = jnp.exp(cum)
    dinv = jnp.exp(-cum)
    rt = r * dinc
    kt = k * dinv
    at = -kk * jnp.exp(cum - lw)
    bt = kk * ag * dinv
    d_chunk = dinc[:, c - 1:c, :]

    row = lax.broadcasted_iota(I32, (1, c, c), 1)
    col = lax.broadcasted_iota(I32, (1, c, c), 2)
    strict = row > col
    incl = row >= col
    eye = jnp.broadcast_to((row == col).astype(F32), (g, c, c))
    p = _bmm_nt(jnp.concatenate([at, rt], axis=1), jnp.concatenate([bt, kt], axis=1))
    a_ab = jnp.where(strict, p[:, :c, :c], 0.0)
    a_ak = jnp.where(strict, p[:, :c, c:], 0.0)
    a_rb = jnp.where(incl, p[:, c:, :c], 0.0)
    a_rk = jnp.where(incl, p[:, c:, c:], 0.0)
    x = _bmm(a_ab, a_ab)
    tinv = eye + a_ab
    levels = c.bit_length() - 2
    for j in range(1, levels):
        both = _bmm(jnp.concatenate([x, tinv], axis=1), x)
        x = both[:, :c]
        tinv = tinv + both[:, c:]
    tinv = tinv + _bmm(tinv, x)
    z = _bmm(a_ak, v)
    ta = _bmm(tinv, jnp.concatenate([at, z], axis=2))
    ry = _bmm(a_rb, ta)
    rp = rt + ry[:, :, :n]
    yv = ry[:, :, n:] + _bmm(a_rk, v)
    vt = _bmm_nt(eye, v)
    att = _bmm_nt(eye, at)
    tt = _bmm_nt(jnp.concatenate([att, _bmm_nt(vt, a_ak)], axis=1), tinv)
    mn = _bmm(tt, bt)
    moff = mn[:, :n].reshape(nc, hb, n, n)
    n2 = (mn[:, n:] + _bmm(vt, kt)).reshape(nc, hb, n, n)
    dch = d_chunk.reshape(nc, hb, 1, n)

    s = s_ref[...]
    for ci in range(nc):
        st_ref[ci] = s
        s = (s + _bmm(s, moff[ci]) + n2[ci]) * dch[ci]
    s_ref[...] = s

    y = _bmm_nt(rp, st_ref[...].reshape(g, n, n)) + yv
    mu = jnp.mean(y, axis=-1, keepdims=True)
    yc = y - mu
    var = jnp.mean(yc * yc, axis=-1, keepdims=True)
    per_head = lambda t: t.reshape(nc, hb, c, n)
    y = per_head(yc * lax.rsqrt(var + RWKV_GN_EPS)) * lnw_ref[...] + lnb_ref[...]
    bonus = jnp.sum(per_head(r * k) * rk_ref[...], axis=-1, keepdims=True)
    y = y + bonus * per_head(v)
    y_ref[0] = jnp.concatenate([jnp.concatenate([y[ci, hd] for hd in range(hb)], axis=1) for ci in range(nc)],
                               axis=0)


def _wkv_scan(r, k, v, lw, cum, kk, ag, r_k, lnx_w, lnx_b):
    b, nchunks, hh, c, n = r.shape
    t = nchunks * c
    nc = min(WKV_CHUNKS_PER_STEP, nchunks)
    tc = nc * c
    hb = WKV_HEADS_PER_STEP
    blk = pl.BlockSpec((1, nc, hb, c, n), lambda bi, hi, i: (bi, i, hi, 0, 0))
    par = pl.BlockSpec((hb, 1, n), lambda bi, hi, i: (hi, 0, 0))
    per_head = lambda arr: arr.reshape(hh, 1, n)
    return pl.pallas_call(
        _wkv_kernel,
        grid=(b, hh // hb, t // tc),
        in_specs=[blk] * 7 + [par] * 3,
        out_specs=pl.BlockSpec((1, tc, hb * n), lambda bi, hi, i: (bi, i, hi)),
        out_shape=jax.ShapeDtypeStruct((b, t, hh * n), F32),
        scratch_shapes=[pltpu.VMEM((hb, n, n), F32), pltpu.VMEM((nc, hb, n, n), F32)],
        compiler_params=_cparams(("parallel", "parallel", "arbitrary")),
        name="wkv7_scan",
    )(r, k, v, lw, cum, kk, ag, per_head(r_k), per_head(lnx_w), per_head(lnx_b))


def _merge_kernel(ya_ref, yr_ref, g_ref, ga_ref, gr_ref, pa_ref, pr_ref, o_ref):
    ma = _dot(ya_ref[...], pa_ref[...])
    mr = _dot(yr_ref[...] * g_ref[...], pr_ref[...])
    o_ref[...] = (_sigmoid(ga_ref[...]) * ma + _sigmoid(gr_ref[...]) * mr).astype(o_ref.dtype)


def _merge(ya, yr, g, gates, p_attn, p_rwkv, tm=256):
    m, c = ya.shape
    d = p_attn.shape[1]
    tm = min(tm, m)
    tile = pl.BlockSpec((tm, c), lambda i: (i, 0))
    return pl.pallas_call(
        _merge_kernel,
        grid=(m // tm,),
        in_specs=[tile, tile, tile,
                  pl.BlockSpec((tm, d), lambda i: (i, 0)),
                  pl.BlockSpec((tm, d), lambda i: (i, 1)),
                  pl.BlockSpec((c, d), lambda i: (0, 0)),
                  pl.BlockSpec((c, d), lambda i: (0, 0))],
        out_specs=pl.BlockSpec((tm, d), lambda i: (i, 0)),
        out_shape=jax.ShapeDtypeStruct((m, d), BF16),
        compiler_params=_cparams(("parallel",)),
        name="gated_merge",
    )(ya, yr, g, gates, gates, p_attn, p_rwkv)


def _outproj_router_kernel(alpha, mg_ref, x_ref, wo_ref, g_ref, b_ref, wr_ref, br_ref,
                           h_ref, hrow_ref, eid_ref, gate_ref):
    mix = jnp.dot(mg_ref[...], wo_ref[...], preferred_element_type=F32)
    h = _layer_norm(alpha * x_ref[...] + mix, g_ref[...], b_ref[...])
    h_ref[...] = h
    slab = h.shape[1] // LANES
    for j in range(slab):
        hrow_ref[pl.ds(j, h.shape[0], stride=slab), :] = h[:, j * LANES:(j + 1) * LANES]
    h_hi = h.astype(BF16)
    h_lo = (h - h_hi.astype(F32)).astype(BF16)
    logits = (jnp.dot(h_hi, wr_ref[0], preferred_element_type=F32)
              + jnp.dot(h_lo, wr_ref[0], preferred_element_type=F32)
              + jnp.dot(h_hi, wr_ref[1], preferred_element_type=F32)) + br_ref[...]
    lane = lax.broadcasted_iota(I32, logits.shape, 1)
    ninf = -jnp.inf
    big = jnp.int32(2 * LANES)
    glog = jnp.where(lane < N_GROUPS, logits, ninf)
    gmax = jnp.max(glog, axis=-1, keepdims=True)
    gidx = jnp.min(jnp.where(glog == gmax, lane, big), axis=-1, keepdims=True)
    gtop = 1.0 / jnp.sum(jnp.exp(glog - gmax), axis=-1, keepdims=True)
    eg = (lane - N_GROUPS) // EXPERTS_PER_GROUP
    in_group = (lane >= N_GROUPS) & (lane < N_GROUPS + N_EXPERTS) & (eg == gidx)
    el = jnp.where(in_group, logits, ninf)
    m1 = jnp.max(el, axis=-1, keepdims=True)
    i1 = jnp.min(jnp.where(el == m1, lane, big), axis=-1, keepdims=True)
    el2 = jnp.where(lane == i1, ninf, el)
    m2 = jnp.max(el2, axis=-1, keepdims=True)
    i2 = jnp.min(jnp.where(el2 == m2, lane, big), axis=-1, keepdims=True)
    t = jnp.exp(m2 - m1)
    p1 = 1.0 / (1.0 + t)
    p2 = t / (1.0 + t)
    eid_ref[...] = jnp.where(lane == 0, i1 - N_GROUPS, jnp.where(lane == 1, i2 - N_GROUPS, 0))
    gate_ref[...] = jnp.where(lane == 0, gtop * p1, jnp.where(lane == 1, gtop * p2, 0.0))


def _outproj_router(merged, x, w_o, ln_g, ln_b, w_router, b_router, alpha, tm=256):
    m, d = x.shape
    tm = min(tm, m)
    tile = pl.BlockSpec((tm, d), lambda i: (i, 0))
    vec = pl.BlockSpec((1, d), lambda i: (0, 0))
    small = pl.BlockSpec((tm, LANES), lambda i: (i, 0))
    return pl.pallas_call(
        functools.partial(_outproj_router_kernel, alpha),
        grid=(m // tm,),
        in_specs=[tile, tile, pl.BlockSpec((d, d), lambda i: (0, 0)), vec, vec,
                  pl.BlockSpec((2, d, LANES), lambda i: (0, 0, 0)), pl.BlockSpec((1, LANES), lambda i: (0, 0))],
        out_specs=[tile, pl.BlockSpec((tm * (d // LANES), LANES), lambda i: (i, 0)), small, small],
        out_shape=[jax.ShapeDtypeStruct((m, d), F32), jax.ShapeDtypeStruct((m * (d // LANES), LANES), F32),
                   jax.ShapeDtypeStruct((m, LANES), I32), jax.ShapeDtypeStruct((m, LANES), F32)],
        compiler_params=_cparams(("parallel",)),
        name="outproj_ln_router",
    )(merged, x, w_o, ln_g.reshape(1, d), ln_b.reshape(1, d), w_router, b_router)


def _onehots(eid_ref):
    tm = eid_ref.shape[1]
    e_iota = lax.broadcasted_iota(I32, (N_EXPERTS, tm), 0)
    oh0 = (eid_ref[0:1, :] == e_iota).astype(F32)
    oh1 = (eid_ref[1:2, :] == e_iota).astype(F32)
    return oh0, oh1


def _count_kernel(eid_ref, cnt_ref):
    @pl.when(pl.program_id(0) == 0)
    def _():
        cnt_ref[...] = jnp.zeros_like(cnt_ref)

    oh0, oh1 = _onehots(eid_ref)
    cnt_ref[...] += jnp.sum(oh0 + oh1, axis=1, keepdims=True)


def _slot_kernel(eid_ref, pstart_ref, dest_ref, run_ref):
    @pl.when(pl.program_id(0) == 0)
    def _():
        run_ref[...] = jnp.zeros_like(run_ref)

    tm = eid_ref.shape[1]
    oh0, oh1 = _onehots(eid_ref)
    both = oh0 + oh1
    earlier = (lax.broadcasted_iota(I32, (tm, tm), 0) < lax.broadcasted_iota(I32, (tm, tm), 1)).astype(BF16)
    pre = jnp.dot(both.astype(BF16), earlier, preferred_element_type=F32)
    base = pre + run_ref[...] + pstart_ref[...]
    dest_ref[0:1, :] = jnp.sum(oh0 * base, axis=0, keepdims=True).astype(I32)
    dest_ref[1:2, :] = jnp.sum(oh1 * base, axis=0, keepdims=True).astype(I32)
    run_ref[...] += jnp.sum(both, axis=1, keepdims=True)


def _expert_counts(eid_t):
    m = eid_t.shape[1]
    tm = min(ROUTE_TILE, m)
    return pl.pallas_call(
        _count_kernel,
        grid=(m // tm,),
        in_specs=[pl.BlockSpec((TOP_K, tm), lambda i: (0, i))],
        out_specs=pl.BlockSpec((N_EXPERTS, 1), lambda i: (0, 0)),
        out_shape=jax.ShapeDtypeStruct((N_EXPERTS, 1), F32),
        compiler_params=_cparams(("arbitrary",)),
        name="expert_counts",
    )(eid_t)


def _expert_slots(eid_t, pstart):
    m = eid_t.shape[1]
    tm = min(ROUTE_TILE, m)
    return pl.pallas_call(
        _slot_kernel,
        grid=(m // tm,),
        in_specs=[pl.BlockSpec((TOP_K, tm), lambda i: (0, i)),
                  pl.BlockSpec((N_EXPERTS, 1), lambda i: (0, 0))],
        out_specs=pl.BlockSpec((TOP_K, tm), lambda i: (0, i)),
        out_shape=jax.ShapeDtypeStruct((TOP_K, m), I32),
        scratch_shapes=[pltpu.VMEM((N_EXPERTS, 1), F32)],
        compiler_params=_cparams(("arbitrary",)),
        name="expert_slots",
    )(eid_t, pstart)


def _row_slab(ref, row, slab):
    return ref.at[pl.ds(pl.multiple_of(row * slab, slab), slab), :]


def _dispatch_kernel(dest_ref, h_ref, xb_in_ref, xb_ref, sem):
    del xb_in_ref
    tm = DISPATCH_TILE
    slab = h_ref.shape[0] // tm
    t0 = pl.program_id(0) * tm

    def copy(n, k):
        return pltpu.make_async_copy(_row_slab(h_ref, n, slab),
                                     _row_slab(xb_ref, dest_ref[TOP_K * (t0 + n) + k], slab), sem)

    def start(n, carry):
        for k in range(TOP_K):
            copy(n, k).start()
        return carry

    lax.fori_loop(0, tm, start, 0, unroll=8)
    for k in range(TOP_K):
        pltpu.make_async_copy(h_ref, xb_ref.at[pl.ds(0, h_ref.shape[0]), :], sem).wait()


DISPATCH_TILE = 512


def _dispatch(dest_flat, h_rows, m, n_rows):
    slab = h_rows.shape[0] // m
    assert m % DISPATCH_TILE == 0
    grid_spec = pltpu.PrefetchScalarGridSpec(
        num_scalar_prefetch=1,
        grid=(m // DISPATCH_TILE,),
        in_specs=[pl.BlockSpec((DISPATCH_TILE * slab, LANES), lambda i, dr: (i, 0)),
                  pl.BlockSpec(memory_space=pl.ANY)],
        out_specs=pl.BlockSpec(memory_space=pl.ANY),
        scratch_shapes=[pltpu.SemaphoreType.DMA(())],
    )
    return pl.pallas_call(
        _dispatch_kernel,
        grid_spec=grid_spec,
        out_shape=jax.ShapeDtypeStruct((n_rows * slab, LANES), F32),
        input_output_aliases={2: 0},
        compiler_params=pltpu.CompilerParams(dimension_semantics=("arbitrary",), has_side_effects=True),
        name="moe_dispatch",
    )(dest_flat, h_rows, jnp.zeros((n_rows * slab, LANES), F32))


def _expert_kernel(item_e_ref, item_row_ref, item_nsub_ref, *refs):
    wg_refs = refs[:MOE_GU_PIECES]
    wu_refs = refs[MOE_GU_PIECES:2 * MOE_GU_PIECES]
    wd_refs = refs[2 * MOE_GU_PIECES:2 * MOE_GU_PIECES + MOE_DOWN_PIECES]
    xb_ref, yb_ref, xin_ref, yout_ref, x_ref, acc_ref, wgu_ref, wdb_ref, sem_in, sem_out = (
        refs[2 * MOE_GU_PIECES + MOE_DOWN_PIECES:])
    del item_e_ref
    it = pl.program_id(0)
    f = pl.program_id(1)
    n_items = pl.num_programs(0)
    nf = pl.num_programs(1)
    kp = wg_refs[0].shape[1]
    kc = kp * MOE_GU_PIECES
    ff = wg_refs[0].shape[2]
    fp = wd_refs[0].shape[1]
    fc = fp * MOE_DOWN_PIECES
    nsub = item_nsub_ref[it]
    row0 = item_row_ref[it]
    nxt = jnp.minimum(it + 1, n_items - 1)
    nsub_next = jnp.where(it + 1 < n_items, item_nsub_ref[nxt], 0)
    prv = jnp.maximum(it - 1, 0)
    nsub_prev = jnp.where(it > 0, item_nsub_ref[prv], 0)
    rb = MOE_ROWS
    slab = x_ref.shape[0] * kc // LANES
    blk = rb * slab

    def stage_rows(ref, s):
        return ref.at[pl.ds(pl.multiple_of(s * blk, blk), blk), :]

    def hbm_rows(ref, item_row, s):
        return ref.at[pl.ds(pl.multiple_of((item_row + s * rb) * slab, blk), blk), :]

    def in_copy(item_row, s):
        return pltpu.make_async_copy(hbm_rows(xb_ref, item_row, s), stage_rows(xin_ref, s), sem_in)

    def out_copy(item_row, s):
        return pltpu.make_async_copy(stage_rows(yout_ref, s), hbm_rows(yb_ref, item_row, s), sem_out)

    def loop(n, fn):
        def body(s, carry):
            fn(s)
            return carry
        lax.fori_loop(0, n, body, 0)

    @pl.when(nsub > 0)
    def _():
        @pl.when(f == 0)
        def _():
            @pl.when(it == 0)
            def _():
                x_ref[...] = jnp.zeros_like(x_ref)
                loop(nsub, lambda s: in_copy(row0, s).start())
            loop(nsub, lambda s: in_copy(row0, s).wait())

            def to_matrix(s):
                rows = pl.ds(pl.multiple_of(s * rb, rb), rb)
                base = pl.multiple_of(s * blk, blk)
                for j in range(slab):
                    c0 = (j * LANES) % kc
                    x_ref[(j * LANES) // kc, rows, c0:c0 + LANES] = (
                        xin_ref[pl.ds(base + j, rb, stride=slab), :].astype(BF16))
            loop(nsub, to_matrix)
            loop(nsub_next, lambda s: in_copy(item_row_ref[nxt], s).start())

        for q in range(MOE_GU_PIECES):
            wgu_ref[q * kp:(q + 1) * kp, :ff] = wg_refs[q][0].astype(BF16)
            wgu_ref[q * kp:(q + 1) * kp, ff:] = wu_refs[q][0].astype(BF16)
        for q in range(MOE_DOWN_PIECES):
            wdb_ref[pl.ds(pl.multiple_of(f * fc + q * fp, 2 * SUBLANES), fp), :] = wd_refs[q][0].astype(BF16)

        def gate_up(start, size):
            rows = pl.ds(pl.multiple_of(start, rb), size)
            part = jnp.dot(x_ref[f, rows, :], wgu_ref[...], preferred_element_type=F32)
            acc_ref[rows, :] = jnp.where(f > 0, acc_ref[rows, :], 0.0) + part

        gate_up(0, 2 * rb)

        def pair(p, carry):
            gate_up(p * (2 * rb), 2 * rb)
            return carry
        lax.fori_loop(1, nsub // 2, pair, 0)

        @pl.when((nsub % 2 == 1) & (nsub > 1))
        def _():
            gate_up((nsub - 1) * rb, rb)

        @pl.when(f == nf - 1)
        def _():
            loop(nsub_prev, lambda s: out_copy(item_row_ref[prv], s).wait())

            def down(start, size):
                rows = pl.ds(pl.multiple_of(start, rb), size)
                gate = acc_ref[rows, :ff]
                act = (gate * _sigmoid(gate) * acc_ref[rows, ff:]).astype(BF16)
                y = jnp.dot(act, wdb_ref[...], preferred_element_type=F32)
                base = pl.multiple_of(start * slab, blk)
                for j in range(slab):
                    yout_ref[pl.ds(base + j, size, stride=slab), :] = y[:, j * LANES:(j + 1) * LANES]

            down(0, 2 * rb)

            def down_pair(p, carry):
                down(p * (2 * rb), 2 * rb)
                return carry
            lax.fori_loop(1, nsub // 2, down_pair, 0)

            @pl.when((nsub % 2 == 1) & (nsub > 1))
            def _():
                down((nsub - 1) * rb, rb)

            loop(nsub, lambda s: out_copy(row0, s).start())

            @pl.when(nsub_next == 0)
            def _():
                loop(nsub, lambda s: out_copy(row0, s).wait())


def _experts(item_e, item_row, item_nsub, w_gate, w_up, w_down, xb):
    d = w_gate.shape[1]
    ff = w_gate.shape[2]
    nk = MOE_K_STEPS
    kc = d // nk
    fc = ff // nk
    assert d % nk == 0 and ff % nk == 0 and kc % LANES == 0 and fc % (2 * SUBLANES) == 0 and ff % LANES == 0
    n_items = item_e.shape[0]
    rows_max = MOE_ROWS * MOE_SUBS_PER_ITEM
    slab = d // LANES

    kp = kc // MOE_GU_PIECES
    fp = fc // MOE_DOWN_PIECES
    assert kc % MOE_GU_PIECES == 0 and fc % MOE_DOWN_PIECES == 0 and kp % (2 * SUBLANES) == 0 and fp % (2 * SUBLANES) == 0

    def piece(rows, cols, pieces, q):
        def index(it, k, ie, ir, ns):
            return ie[it], jnp.where(ns[it] > 0, k, nk - 1) * pieces + q, 0
        return pl.BlockSpec((1, rows, cols), index)

    grid_spec = pltpu.PrefetchScalarGridSpec(
        num_scalar_prefetch=3,
        grid=(n_items, nk),
        in_specs=[
            *[piece(kp, ff, MOE_GU_PIECES, q) for q in range(MOE_GU_PIECES)],
            *[piece(kp, ff, MOE_GU_PIECES, q) for q in range(MOE_GU_PIECES)],
            *[piece(fp, d, MOE_DOWN_PIECES, q) for q in range(MOE_DOWN_PIECES)],
            pl.BlockSpec(memory_space=pl.ANY),
        ],
        out_specs=pl.BlockSpec(memory_space=pl.ANY),
        scratch_shapes=[
            pltpu.VMEM((rows_max * slab, LANES), F32),
            pltpu.VMEM((rows_max * slab, LANES), F32),
            pltpu.VMEM((nk, rows_max, kc), BF16),
            pltpu.VMEM((rows_max, 2 * ff), F32),
            pltpu.VMEM((kc, 2 * ff), BF16),
            pltpu.VMEM((ff, d), BF16),
            pltpu.SemaphoreType.DMA(()),
            pltpu.SemaphoreType.DMA(()),
        ],
    )
    return pl.pallas_call(
        _expert_kernel,
        grid_spec=grid_spec,
        out_shape=jax.ShapeDtypeStruct(xb.shape, F32),
        input_output_aliases={3 + 2 * MOE_GU_PIECES + MOE_DOWN_PIECES: 0},
        compiler_params=pltpu.CompilerParams(dimension_semantics=("arbitrary", "arbitrary"),
                                             vmem_limit_bytes=VMEM_LIMIT_BYTES, has_side_effects=True),
        name="moe_experts",
    )(item_e, item_row, item_nsub, *([w_gate] * MOE_GU_PIECES), *([w_up] * MOE_GU_PIECES),
      *([w_down] * MOE_DOWN_PIECES), xb)


COMBINE_TILE = 256


def _combine_kernel(alpha, dest_ref, yb_ref, h_ref, gate_ref, g_ref, b_ref, o_ref, buf_ref, sem):
    tm = COMBINE_TILE
    i = pl.program_id(0)
    slab = buf_ref.shape[2] // tm

    def issue(tile, half):
        def start(n, carry):
            for k in range(TOP_K):
                pltpu.make_async_copy(_row_slab(yb_ref, dest_ref[TOP_K * (tile * tm + n) + k], slab),
                                      _row_slab(buf_ref.at[half, k], n, slab), sem.at[half]).start()
            return carry
        lax.fori_loop(0, tm, start, 0, unroll=8)

    @pl.when(i == 0)
    def _():
        issue(0, 0)

    @pl.when(i + 1 < pl.num_programs(0))
    def _():
        issue(i + 1, (i + 1) % 2)

    half = i % 2
    for k in range(TOP_K):
        pltpu.make_async_copy(yb_ref.at[pl.ds(0, buf_ref.shape[2]), :], buf_ref.at[half, k], sem.at[half]).wait()
    gate = gate_ref[...]
    g0 = gate[:, 0:1]
    g1 = gate[:, 1:2]
    col = lambda k, j: buf_ref[half, k, pl.ds(j, tm, stride=slab), :]
    ff = jnp.concatenate([g0 * col(0, j) + g1 * col(1, j) for j in range(slab)], axis=1)
    o_ref[...] = _layer_norm(alpha * h_ref[...] + ff, g_ref[...], b_ref[...])


def _combine(dest_flat, yb, h, gates, ln_g, ln_b, alpha):
    m, d = h.shape
    tm = COMBINE_TILE
    assert m % tm == 0
    grid_spec = pltpu.PrefetchScalarGridSpec(
        num_scalar_prefetch=1,
        grid=(m // tm,),
        in_specs=[pl.BlockSpec(memory_space=pl.ANY),
                  pl.BlockSpec((tm, d), lambda i, dr: (i, 0)),
                  pl.BlockSpec((tm, LANES), lambda i, dr: (i, 0)),
                  pl.BlockSpec((1, d), lambda i, dr: (0, 0)),
                  pl.BlockSpec((1, d), lambda i, dr: (0, 0))],
        out_specs=pl.BlockSpec((tm, d), lambda i, dr: (i, 0)),
        scratch_shapes=[pltpu.VMEM((2, TOP_K, tm * (d // LANES), LANES), F32), pltpu.SemaphoreType.DMA((2,))],
    )
    return pl.pallas_call(
        functools.partial(_combine_kernel, alpha),
        grid_spec=grid_spec,
        out_shape=jax.ShapeDtypeStruct((m, d), F32),
        compiler_params=_cparams(("arbitrary",)),
        name="moe_combine_ln",
    )(dest_flat, yb, h, gates, ln_g.reshape(1, d), ln_b.reshape(1, d))


def _moe_tables(counts):
    nsub_e = (counts + MOE_ROWS - 1) // MOE_ROWS
    pstart = (jnp.cumsum(nsub_e) - nsub_e) * MOE_ROWS
    nitem_e = (nsub_e + MOE_SUBS_PER_ITEM - 1) // MOE_SUBS_PER_ITEM
    item_end = jnp.cumsum(nitem_e)
    return nsub_e, pstart, nitem_e, item_end


def _moe(h, h_rows, eid, gates, w_gate, w_up, w_down, ln_g, ln_b, alpha):
    m, d = h.shape
    n_assign = m * TOP_K
    n_blocks = (n_assign + N_EXPERTS * (MOE_ROWS - 1) + MOE_ROWS - 1) // MOE_ROWS
    n_rows = n_blocks * MOE_ROWS
    n_items = N_EXPERTS + n_assign // (MOE_ROWS * MOE_SUBS_PER_ITEM)

    eid_t = eid[:, :TOP_K].T
    counts = _expert_counts(eid_t)[:, 0].astype(I32)
    nsub_e, pstart, nitem_e, item_end = _moe_tables(counts)
    dest_t = _expert_slots(eid_t, pstart.astype(F32).reshape(N_EXPERTS, 1))
    dest_flat = dest_t.T.reshape(-1)

    it = jnp.arange(n_items, dtype=I32)
    total_items = item_end[-1]
    item_e = jnp.minimum(jnp.sum(item_end[None, :] <= it[:, None], axis=1).astype(I32), N_EXPERTS - 1)
    j = it - (item_end - nitem_e)[item_e]
    used = it < total_items
    last_e = item_e[jnp.maximum(total_items - 1, 0)]
    item_nsub = jnp.where(used, jnp.clip(nsub_e[item_e] - j * MOE_SUBS_PER_ITEM, 0, MOE_SUBS_PER_ITEM), 0).astype(I32)
    item_row = jnp.where(used, pstart[item_e] + j * (MOE_ROWS * MOE_SUBS_PER_ITEM), 0).astype(I32)
    item_e = jnp.where(used, item_e, last_e).astype(I32)

    xb = _dispatch(dest_flat, h_rows, m, n_rows)
    yb = _experts(item_e, item_row, item_nsub, w_gate, w_up, w_down, xb)
    return _combine(dest_flat, yb, h, gates, ln_g, ln_b, alpha)


def _pad_cols(w, n):
    return jnp.pad(w, ((0, 0), (0, n - w.shape[1])))


def _pad_rows(w, n):
    return jnp.pad(w, ((0, n - w.shape[0]), (0, 0)))


def kernel(x, w_in, attn_sinks, rw_mu_rkv, rw_mu_wag, rw_w0, rw_w1, rw_w2, rw_a0, rw_a1, rw_a2, rw_g1, rw_g2, rw_k_k, rw_k_a, rw_r_k, rw_lnx_w, rw_lnx_b, p_attn, p_rwkv, w_o, ln1_g, ln1_b, w_group, b_group, w_expert, b_expert, w_gate, w_up, w_down, ln2_g, ln2_b):
    b, t, d = x.shape
    depth = w_in.shape[0]
    m = b * t
    alpha = (2.0 * depth) ** 0.25
    cosb, sinb = _rope_tables(t)
    qkv_w = ATTN_Q_W + 2 * ATTN_KV_W
    rkv_w = 3 * RWKV_W
    h = x
    for l in range(depth):
        hf = h.reshape(m, d)
        hb = hf.astype(BF16)
        qkv = _matmul_cols(hb, w_in[l], 0, qkv_w, F32)
        rkv = _matmul_cols(hb, w_in[l], qkv_w, rkv_w, F32)
        gates = _matmul_cols(hb, w_in[l], qkv_w + rkv_w, 2 * d, F32)

        y_a = _attention(qkv.reshape(b, t, qkv_w), attn_sinks[l], cosb, sinb)

        lora = lambda w, n: _pad_cols(w, n).astype(BF16)
        lorb = lambda w, n: _pad_rows(w, n).astype(BF16)
        n_w = -(-rw_w1.shape[2] // LANES) * LANES
        n_a = -(-rw_a1.shape[2] // LANES) * LANES
        n_g = -(-rw_g1.shape[2] // LANES) * LANES
        r_, k_, v_, lw_, cum_, kk_, ag_, g_ = _rwkv_prep(
            h, rkv.reshape(b, t, rkv_w), rw_mu_rkv[l], rw_mu_wag[l], rw_w0[l],
            lora(rw_w1[l], n_w), lorb(rw_w2[l], n_w), rw_a0[l], lora(rw_a1[l], n_a), lorb(rw_a2[l], n_a),
            lora(rw_g1[l], n_g), lorb(rw_g2[l], n_g), rw_k_k[l], rw_k_a[l])
        y_r = _wkv_scan(r_, k_, v_, lw_, cum_, kk_, ag_, rw_r_k[l], rw_lnx_w[l], rw_lnx_b[l])

        merged = _merge(y_a.reshape(m, ATTN_Q_W), y_r.reshape(m, RWKV_W), g_.reshape(m, RWKV_W), gates,
                        p_attn[l].astype(BF16), p_rwkv[l].astype(BF16))
        w_router = _pad_cols(jnp.concatenate([w_group[l], w_expert[l]], axis=1), LANES)
        w_router_hi = w_router.astype(BF16)
        w_router = jnp.stack([w_router_hi, (w_router - w_router_hi.astype(F32)).astype(BF16)])
        b_router = _pad_cols(jnp.concatenate([b_group[l], b_expert[l]])[None, :], LANES)
        h1, h1_rows, eid, gate = _outproj_router(merged, hf, w_o[l].astype(BF16), ln1_g[l], ln1_b[l],
                                        w_router, b_router, alpha)
        h2 = _moe(h1, h1_rows, eid, gate, w_gate[l], w_up[l], w_down[l], ln2_g[l], ln2_b[l], alpha)
        h = h2.reshape(b, t, d)
    return h
```

```python
import functools

import jax
import jax.numpy as jnp
from jax import lax
from jax.experimental import pallas as pl
from jax.experimental.pallas import tpu as pltpu

F32 = jnp.float32
BF16 = jnp.bfloat16
I32 = jnp.int32

HEAD_DIM = 64
ATTN_Q_HEADS = 16
ATTN_KV_HEADS = 4
ATTN_GROUP = ATTN_Q_HEADS // ATTN_KV_HEADS
ATTN_Q_W = ATTN_Q_HEADS * HEAD_DIM
ATTN_KV_W = ATTN_KV_HEADS * HEAD_DIM
WINDOW = 128
ROPE_THETA = 10000.0
RWKV_HEADS = 16
RWKV_N = 64
RWKV_W = RWKV_HEADS * RWKV_N
RWKV_GN_EPS = 64e-5
N_GROUPS = 8
EXPERTS_PER_GROUP = 8
N_EXPERTS = N_GROUPS * EXPERTS_PER_GROUP
TOP_K = 2
LN_EPS = 1e-5

LANES = 128
SUBLANES = 8
VMEM_LIMIT_BYTES = 56 * 1024 * 1024

WKV_CHUNK = 64
WKV_CHUNKS_PER_STEP = 4
WKV_HEADS_PER_STEP = 8
MOE_ROWS = 128
MOE_SUBS_PER_ITEM = 4
MOE_K_STEPS = 4
MOE_GU_PIECES = 1
MOE_DOWN_PIECES = 1
ROUTE_TILE = 512


def _cparams(sem, vmem=VMEM_LIMIT_BYTES):
    return pltpu.CompilerParams(dimension_semantics=sem, vmem_limit_bytes=vmem)


def _sigmoid(x):
    return 1.0 / (1.0 + jnp.exp(-x))


def _dot(a, b):
    return jnp.dot(a.astype(BF16), b.astype(BF16), preferred_element_type=F32)


def _dot_nt(a, b):
    return lax.dot_general(a.astype(BF16), b.astype(BF16), (((1,), (1,)), ((), ())),
                           preferred_element_type=F32)


def _layer_norm(t, g, b):
    mu = jnp.mean(t, axis=-1, keepdims=True)
    d = t - mu
    var = jnp.mean(d * d, axis=-1, keepdims=True)
    return d * lax.rsqrt(var + LN_EPS) * g + b


def _matmul_kernel(a_ref, b_ref, o_ref):
    o_ref[...] = jnp.dot(a_ref[...], b_ref[...].astype(BF16), preferred_element_type=F32).astype(o_ref.dtype)


def _matmul_cols(a, b, col0, ncols, out_dtype, tm=2048, tn=512):
    m, k = a.shape
    tm = min(tm, m)
    cb = col0 // tn
    assert col0 % tn == 0 and ncols % tn == 0 and m % tm == 0
    return pl.pallas_call(
        _matmul_kernel,
        grid=(m // tm, ncols // tn),
        in_specs=[pl.BlockSpec((tm, k), lambda i, j: (i, 0)),
                  pl.BlockSpec((k, tn), lambda i, j: (0, j + cb))],
        out_specs=pl.BlockSpec((tm, tn), lambda i, j: (i, j)),
        out_shape=jax.ShapeDtypeStruct((m, ncols), out_dtype),
        compiler_params=_cparams(("parallel", "arbitrary")),
        name="inproj_matmul",
    )(a, b)


def _rope(x, cosb, sinb):
    half = HEAD_DIM // 2
    lane = lax.broadcasted_iota(I32, cosb.shape, 1)
    first_half = (lane % HEAD_DIM) < half
    outs = []
    for g in range(x.shape[1] // LANES):
        xg = x[:, g * LANES:(g + 1) * LANES]
        partner = jnp.where(first_half, pltpu.roll(xg, LANES - half, axis=1), pltpu.roll(xg, half, axis=1))
        outs.append(xg * cosb + partner * sinb)
    return outs


def _attn_kernel(sinks_ref, q_ref, kc_ref, kp_ref, vc_ref, vp_ref, cosc_ref, sinc_ref, cosp_ref, sinp_ref, o_ref):
    blk = pl.program_id(1)
    tq = q_ref.shape[1]
    qg = _rope(q_ref[0], cosc_ref[...], sinc_ref[...])
    kcg = _rope(kc_ref[0], cosc_ref[...], sinc_ref[...])
    kpg = _rope(kp_ref[0], cosp_ref[...], sinp_ref[...])
    vc = vc_ref[0]
    vp = vp_ref[0]

    def head(groups, h):
        g = groups[h // 2]
        return g[:, (h % 2) * HEAD_DIM:(h % 2 + 1) * HEAD_DIM]

    rows = ATTN_GROUP * tq
    qi = lax.broadcasted_iota(I32, (rows, 2 * tq), 0) % tq
    kj = lax.broadcasted_iota(I32, (rows, 2 * tq), 1)
    dist = qi + tq - kj
    valid = (dist >= 0) & (dist < WINDOW) & ((blk > 0) | (kj >= tq))
    rid = lax.broadcasted_iota(I32, (rows, 1), 0) // tq
    scale = HEAD_DIM ** -0.5
    for kvh in range(ATTN_KV_HEADS):
        qh = jnp.concatenate([head(qg, kvh * ATTN_GROUP + g) for g in range(ATTN_GROUP)], axis=0)
        kw = jnp.concatenate([head(kpg, kvh), head(kcg, kvh)], axis=0)
        vw = jnp.concatenate([vp[:, kvh * HEAD_DIM:(kvh + 1) * HEAD_DIM],
                              vc[:, kvh * HEAD_DIM:(kvh + 1) * HEAD_DIM]], axis=0)
        s = _dot_nt(qh, kw) * scale
        s = jnp.where(valid, s, -jnp.inf)
        sink = jnp.zeros((rows, 1), F32)
        for g in range(ATTN_GROUP):
            sink = jnp.where(rid == g, sinks_ref[kvh * ATTN_GROUP + g], sink)
        m = jnp.maximum(jnp.max(s, axis=-1, keepdims=True), sink)
        e = jnp.exp(s - m)
        denom = jnp.sum(e, axis=-1, keepdims=True) + jnp.exp(sink - m)
        o = _dot(e, vw) / denom
        for g in range(ATTN_GROUP):
            hq = kvh * ATTN_GROUP + g
            o_ref[0, :, hq * HEAD_DIM:(hq + 1) * HEAD_DIM] = o[g * tq:(g + 1) * tq].astype(o_ref.dtype)


def _attention(qkv, sinks, cosb, sinb):
    b, t, _ = qkv.shape
    tq = WINDOW
    nb = t // tq
    kcol = ATTN_Q_W // ATTN_KV_W
    prev = lambda i: jnp.maximum(i - 1, 0)
    grid_spec = pltpu.PrefetchScalarGridSpec(
        num_scalar_prefetch=0,
        grid=(b, nb),
        in_specs=[
            pl.BlockSpec(memory_space=pltpu.SMEM),
            pl.BlockSpec((1, tq, ATTN_Q_W), lambda bi, i: (bi, i, 0)),
            pl.BlockSpec((1, tq, ATTN_KV_W), lambda bi, i: (bi, i, kcol)),
            pl.BlockSpec((1, tq, ATTN_KV_W), lambda bi, i: (bi, prev(i), kcol)),
            pl.BlockSpec((1, tq, ATTN_KV_W), lambda bi, i: (bi, i, kcol + 1)),
            pl.BlockSpec((1, tq, ATTN_KV_W), lambda bi, i: (bi, prev(i), kcol + 1)),
            pl.BlockSpec((tq, LANES), lambda bi, i: (i, 0)),
            pl.BlockSpec((tq, LANES), lambda bi, i: (i, 0)),
            pl.BlockSpec((tq, LANES), lambda bi, i: (prev(i), 0)),
            pl.BlockSpec((tq, LANES), lambda bi, i: (prev(i), 0)),
        ],
        out_specs=pl.BlockSpec((1, tq, ATTN_Q_W), lambda bi, i: (bi, i, 0)),
    )
    return pl.pallas_call(
        _attn_kernel,
        grid_spec=grid_spec,
        out_shape=jax.ShapeDtypeStruct((b, t, ATTN_Q_W), BF16),
        compiler_params=_cparams(("parallel", "arbitrary")),
        name="swa_attention",
    )(sinks, qkv, qkv, qkv, qkv, qkv, cosb, sinb, cosb, sinb)


def _rope_tables(t):
    inv = 1.0 / (ROPE_THETA ** (jnp.arange(0, HEAD_DIM, 2, dtype=F32) / HEAD_DIM))
    ang = jnp.arange(t, dtype=F32)[:, None] * inv[None, :]
    cos, sin = jnp.cos(ang), jnp.sin(ang)
    reps = LANES // HEAD_DIM
    cosb = jnp.tile(jnp.concatenate([cos, cos], axis=-1), (1, reps))
    sinb = jnp.tile(jnp.concatenate([-sin, sin], axis=-1), (1, reps))
    return cosb, sinb


def _rwkv_prep_kernel(h_ref, hp_ref, r_ref, k_ref, v_ref, rp_ref, kp_ref, vp_ref,
                      mu_rkv_ref, mu_wag_ref, w0_ref, w1_ref, w2_ref, a0_ref, a1_ref, a2_ref,
                      g1_ref, g2_ref, kk_ref, ka_ref,
                      ro_ref, ko_ref, vo_ref, lwo_ref, cumo_ref, kko_ref, ago_ref, go_ref):
    first = pl.program_id(1) == 0
    last_row = SUBLANES - 1

    def shifted(cur, prev_ref):
        prev_row = jnp.where(first, 0.0, prev_ref[0, last_row:last_row + 1, :])
        rowid = lax.broadcasted_iota(I32, cur.shape, 0)
        return jnp.where(rowid == 0, prev_row, pltpu.roll(cur, 1, axis=0))

    h = h_ref[0]
    xx = shifted(h, hp_ref) - h
    xw = h + xx * mu_wag_ref[0:1, :]
    xa = h + xx * mu_wag_ref[1:2, :]
    xg = h + xx * mu_wag_ref[2:3, :]
    w_raw = w0_ref[...] + _dot(jnp.tanh(_dot(xw, w1_ref[...])), w2_ref[...])
    neg = -w_raw
    softplus = jnp.maximum(neg, 0.0) + jnp.log1p(jnp.exp(-jnp.abs(neg)))
    w = -softplus - 0.5
    lw = -jnp.exp(w)
    tm = lw.shape[0]
    row = lax.broadcasted_iota(I32, (tm, tm), 0)
    col = lax.broadcasted_iota(I32, (tm, tm), 1)
    tri = ((row >= col) & (row // WKV_CHUNK == col // WKV_CHUNK)).astype(BF16)
    cum = sum(jnp.dot(tri, piece, preferred_element_type=F32) for piece in _split3(lw))
    ag = _sigmoid(a0_ref[...] + _dot(_dot(xa, a1_ref[...]), a2_ref[...]))
    go_ref[0] = _dot(_sigmoid(_dot(xg, g1_ref[...])), g2_ref[...])

    r = r_ref[0]
    k = k_ref[0]
    v = v_ref[0]
    r = r + (shifted(r, rp_ref) - r) * mu_rkv_ref[0:1, :]
    k = k + (shifted(k, kp_ref) - k) * mu_rkv_ref[1:2, :]
    v = v + (shifted(v, vp_ref) - v) * mu_rkv_ref[2:3, :]
    kk = k * kk_ref[...]
    k = k * (1.0 + (ag - 1.0) * ka_ref[...])
    nchunk = tm // WKV_CHUNK
    for hd in range(RWKV_HEADS):
        sl = slice(hd * RWKV_N, (hd + 1) * RWKV_N)
        for ref, val in ((ro_ref, r), (ko_ref, k), (vo_ref, v), (lwo_ref, lw), (cumo_ref, cum), (kko_ref, kk),
                         (ago_ref, ag)):
            ref[0, :, hd] = val[:, sl].reshape(nchunk, WKV_CHUNK, RWKV_N)


def _rwkv_prep(h, rkv, mu_rkv, mu_wag, w0, w1, w2, a0, a1, a2, g1, g2, k_k, k_a, tm=256):
    b, t, d = h.shape
    tm = min(tm, t)
    c = RWKV_W
    spt = tm // SUBLANES
    prevblk = lambda i: jnp.maximum(i * spt - 1, 0)
    full = lambda arr: pl.BlockSpec(arr.shape, lambda bi, i: (0,) * arr.ndim)
    row = lambda arr: arr.reshape(1, -1)
    w0, a0, k_k, k_a = row(w0), row(a0), row(k_k), row(k_a)
    in_specs = [
        pl.BlockSpec((1, tm, d), lambda bi, i: (bi, i, 0)),
        pl.BlockSpec((1, SUBLANES, d), lambda bi, i: (bi, prevblk(i), 0)),
        pl.BlockSpec((1, tm, c), lambda bi, i: (bi, i, 0)),
        pl.BlockSpec((1, tm, c), lambda bi, i: (bi, i, 1)),
        pl.BlockSpec((1, tm, c), lambda bi, i: (bi, i, 2)),
        pl.BlockSpec((1, SUBLANES, c), lambda bi, i: (bi, prevblk(i), 0)),
        pl.BlockSpec((1, SUBLANES, c), lambda bi, i: (bi, prevblk(i), 1)),
        pl.BlockSpec((1, SUBLANES, c), lambda bi, i: (bi, prevblk(i), 2)),
    ] + [full(a) for a in (mu_rkv, mu_wag, w0, w1, w2, a0, a1, a2, g1, g2, k_k, k_a)]
    assert tm % WKV_CHUNK == 0
    hm = jax.ShapeDtypeStruct((b, t // WKV_CHUNK, RWKV_HEADS, WKV_CHUNK, RWKV_N), F32)
    hm_spec = pl.BlockSpec((1, tm // WKV_CHUNK, RWKV_HEADS, WKV_CHUNK, RWKV_N), lambda bi, i: (bi, i, 0, 0, 0))
    return pl.pallas_call(
        _rwkv_prep_kernel,
        grid=(b, t // tm),
        in_specs=in_specs,
        out_specs=[hm_spec] * 7 + [pl.BlockSpec((1, tm, c), lambda bi, i: (bi, i, 0))],
        out_shape=[hm] * 7 + [jax.ShapeDtypeStruct((b, t, c), F32)],
        compiler_params=_cparams(("parallel", "arbitrary")),
        name="rwkv_prep",
    )(h, h, rkv, rkv, rkv, rkv, rkv, rkv, mu_rkv, mu_wag, w0, w1, w2, a0, a1, a2, g1, g2, k_k, k_a)


def _split3(x):
    hi = x.astype(BF16)
    r1 = x - hi.astype(F32)
    mid = r1.astype(BF16)
    lo = (r1 - mid.astype(F32)).astype(BF16)
    return hi, mid, lo


def _bmm(a, b):
    return jnp.einsum("gmk,gkn->gmn", a.astype(BF16), b.astype(BF16), preferred_element_type=F32)


def _bmm_nt(a, b):
    return jnp.einsum("gmk,gnk->gmn", a.astype(BF16), b.astype(BF16), preferred_element_type=F32)


def _wkv_kernel(r_ref, k_ref, v_ref, lw_ref, cum_ref, kk_ref, ag_ref, g_ref, rk_ref, lnw_ref, lnb_ref, y_ref,
                s_ref, st_ref):
    c = WKV_CHUNK
    n = RWKV_N
    nc, hb = r_ref.shape[1], r_ref.shape[2]
    tc = nc * c
    g = nc * hb

    @pl.when(pl.program_id(2) == 0)
    def _():
        s_ref[...] = jnp.zeros_like(s_ref)

    chunks = lambda ref: ref[0].reshape(g, c, n)
    r, k, v, lw, cum, kk, ag = (chunks(ref) for ref in (r_ref, k_ref, v_ref, lw_ref, cum_ref, kk_ref, ag_ref))
    kk = kk / jnp.maximum(jnp.sqrt(jnp.sum(kk * kk, axis=-1, keepdims=True)), 1e-12)
    dinc = jnp.exp(cum)
    dinv = jnp.exp(-cum)
    rt = r * dinc
    kt = k * dinv
    at = -kk * jnp.exp(cum - lw)
    bt = kk * ag * dinv
    d_chunk = dinc[:, c - 1:c, :]

    row = lax.broadcasted_iota(I32, (1, c, c), 1)
    col = lax.broadcasted_iota(I32, (1, c, c), 2)
    strict = row > col
    incl = row >= col
    eye = jnp.broadcast_to((row == col).astype(F32), (g, c, c))
    p = _bmm_nt(jnp.concatenate([at, rt], axis=1), jnp.concatenate([bt, kt], axis=1))
    a_ab = jnp.where(strict, p[:, :c, :c], 0.0)
    a_ak = jnp.where(strict, p[:, :c, c:], 0.0)
    a_rb = jnp.where(incl, p[:, c:, :c], 0.0)
    a_rk = jnp.where(incl, p[:, c:, c:], 0.0)
    x = _bmm(a_ab, a_ab)
    tinv = eye + a_ab
    levels = c.bit_length() - 2
    for j in range(1, levels):
        both = _bmm(jnp.concatenate([x, tinv], axis=1), x)
        x = both[:, :c]
        tinv = tinv + both[:, c:]
    tinv = tinv + _bmm(tinv, x)
    z = _bmm(a_ak, v)
    ta = _bmm(tinv, jnp.concatenate([at, z], axis=2))
    ry = _bmm(a_rb, ta)
    rp = rt + ry[:, :, :n]
    yv = ry[:, :, n:] + _bmm(a_rk, v)
    vt = _bmm_nt(eye, v)
    att = _bmm_nt(eye, at)
    tt = _bmm_nt(jnp.concatenate([att, _bmm_nt(vt, a_ak)], axis=1), tinv)
    mn = _bmm(tt, bt)
    moff = mn[:, :n].reshape(nc, hb, n, n)
    n2 = (mn[:, n:] + _bmm(vt, kt)).reshape(nc, hb, n, n)
    dch = d_chunk.reshape(nc, hb, 1, n)

    s = s_ref[...]
    for ci in range(nc):
        st_ref[ci] = s
        s = (s + _bmm(s, moff[ci]) + n2[ci]) * dch[ci]
    s_ref[...] = s

    y = _bmm_nt(rp, st_ref[...].reshape(g, n, n)) + yv
    mu = jnp.mean(y, axis=-1, keepdims=True)
    yc = y - mu
    var = jnp.mean(yc * yc, axis=-1, keepdims=True)
    per_head = lambda t: t.reshape(nc, hb, c, n)
    y = per_head(yc * lax.rsqrt(var + RWKV_GN_EPS)) * lnw_ref[...] + lnb_ref[...]
    bonus = jnp.sum(per_head(r * k) * rk_ref[...], axis=-1, keepdims=True)
    y = y + bonus * per_head(v)
    y = jnp.concatenate([jnp.concatenate([y[ci, hd] for hd in range(hb)], axis=1) for ci in range(nc)], axis=0)
    y_ref[0] = (y * g_ref[0]).astype(y_ref.dtype)


def _wkv_scan(r, k, v, lw, cum, kk, ag, g, r_k, lnx_w, lnx_b):
    b, nchunks, hh, c, n = r.shape
    t = nchunks * c
    nc = min(WKV_CHUNKS_PER_STEP, nchunks)
    tc = nc * c
    hb = WKV_HEADS_PER_STEP
    blk = pl.BlockSpec((1, nc, hb, c, n), lambda bi, hi, i: (bi, i, hi, 0, 0))
    tok = pl.BlockSpec((1, tc, hb * n), lambda bi, hi, i: (bi, i, hi))
    par = pl.BlockSpec((hb, 1, n), lambda bi, hi, i: (hi, 0, 0))
    per_head = lambda arr: arr.reshape(hh, 1, n)
    return pl.pallas_call(
        _wkv_kernel,
        grid=(b, hh // hb, t // tc),
        in_specs=[blk] * 7 + [tok] + [par] * 3,
        out_specs=tok,
        out_shape=jax.ShapeDtypeStruct((b, t, hh * n), BF16),
        scratch_shapes=[pltpu.VMEM((hb, n, n), F32), pltpu.VMEM((nc, hb, n, n), F32)],
        compiler_params=_cparams(("parallel", "parallel", "arbitrary")),
        name="wkv7_scan",
    )(r, k, v, lw, cum, kk, ag, g, per_head(r_k), per_head(lnx_w), per_head(lnx_b))


def _merge_kernel(ya_ref, yr_ref, ga_ref, gr_ref, pa_ref, pr_ref, o_ref):
    ma = jnp.dot(ya_ref[...], pa_ref[...], preferred_element_type=F32)
    mr = jnp.dot(yr_ref[...], pr_ref[...], preferred_element_type=F32)
    o_ref[...] = (_sigmoid(ga_ref[...].astype(F32)) * ma + _sigmoid(gr_ref[...].astype(F32)) * mr).astype(o_ref.dtype)


def _merge(ya, yr, gates, p_attn, p_rwkv, tm=256):
    m, c = ya.shape
    d = p_attn.shape[1]
    tm = min(tm, m)
    tile = pl.BlockSpec((tm, c), lambda i: (i, 0))
    return pl.pallas_call(
        _merge_kernel,
        grid=(m // tm,),
        in_specs=[tile, tile,
                  pl.BlockSpec((tm, d), lambda i: (i, 0)),
                  pl.BlockSpec((tm, d), lambda i: (i, 1)),
                  pl.BlockSpec((c, d), lambda i: (0, 0)),
                  pl.BlockSpec((c, d), lambda i: (0, 0))],
        out_specs=pl.BlockSpec((tm, d), lambda i: (i, 0)),
        out_shape=jax.ShapeDtypeStruct((m, d), BF16),
        compiler_params=_cparams(("parallel",)),
        name="gated_merge",
    )(ya, yr, gates, gates, p_attn, p_rwkv)


def _outproj_router_kernel(alpha, mg_ref, x_ref, wo_ref, g_ref, b_ref, wr_ref, br_ref,
                           h_ref, hrow_ref, eid_ref, gate_ref):
    mix = jnp.dot(mg_ref[...], wo_ref[...], preferred_element_type=F32)
    h = _layer_norm(alpha * x_ref[...] + mix, g_ref[...], b_ref[...])
    h_ref[...] = h
    slab = h.shape[1] // LANES
    for j in range(slab):
        hrow_ref[pl.ds(j, h.shape[0], stride=slab), :] = h[:, j * LANES:(j + 1) * LANES]
    h_hi = h.astype(BF16)
    h_lo = (h - h_hi.astype(F32)).astype(BF16)
    logits = (jnp.dot(h_hi, wr_ref[0], preferred_element_type=F32)
              + jnp.dot(h_lo, wr_ref[0], preferred_element_type=F32)
              + jnp.dot(h_hi, wr_ref[1], preferred_element_type=F32)) + br_ref[...]
    lane = lax.broadcasted_iota(I32, logits.shape, 1)
    ninf = -jnp.inf
    big = jnp.int32(2 * LANES)
    glog = jnp.where(lane < N_GROUPS, logits, ninf)
    gmax = jnp.max(glog, axis=-1, keepdims=True)
    gidx = jnp.min(jnp.where(glog == gmax, lane, big), axis=-1, keepdims=True)
    gtop = 1.0 / jnp.sum(jnp.exp(glog - gmax), axis=-1, keepdims=True)
    eg = (lane - N_GROUPS) // EXPERTS_PER_GROUP
    in_group = (lane >= N_GROUPS) & (lane < N_GROUPS + N_EXPERTS) & (eg == gidx)
    el = jnp.where(in_group, logits, ninf)
    m1 = jnp.max(el, axis=-1, keepdims=True)
    i1 = jnp.min(jnp.where(el == m1, lane, big), axis=-1, keepdims=True)
    el2 = jnp.where(lane == i1, ninf, el)
    m2 = jnp.max(el2, axis=-1, keepdims=True)
    i2 = jnp.min(jnp.where(el2 == m2, lane, big), axis=-1, keepdims=True)
    t = jnp.exp(m2 - m1)
    p1 = 1.0 / (1.0 + t)
    p2 = t / (1.0 + t)
    eid_ref[...] = jnp.where(lane == 0, i1 - N_GROUPS, jnp.where(lane == 1, i2 - N_GROUPS, 0))
    gate_ref[...] = jnp.where(lane == 0, gtop * p1, jnp.where(lane == 1, gtop * p2, 0.0))


def _outproj_router(merged, x, w_o, ln_g, ln_b, w_router, b_router, alpha, tm=256):
    m, d = x.shape
    tm = min(tm, m)
    tile = pl.BlockSpec((tm, d), lambda i: (i, 0))
    vec = pl.BlockSpec((1, d), lambda i: (0, 0))
    small = pl.BlockSpec((tm, LANES), lambda i: (i, 0))
    return pl.pallas_call(
        functools.partial(_outproj_router_kernel, alpha),
        grid=(m // tm,),
        in_specs=[tile, tile, pl.BlockSpec((d, d), lambda i: (0, 0)), vec, vec,
                  pl.BlockSpec((2, d, LANES), lambda i: (0, 0, 0)), pl.BlockSpec((1, LANES), lambda i: (0, 0))],
        out_specs=[tile, pl.BlockSpec((tm * (d // LANES), LANES), lambda i: (i, 0)), small, small],
        out_shape=[jax.ShapeDtypeStruct((m, d), F32), jax.ShapeDtypeStruct((m * (d // LANES), LANES), F32),
                   jax.ShapeDtypeStruct((m, LANES), I32), jax.ShapeDtypeStruct((m, LANES), F32)],
        compiler_params=_cparams(("parallel",)),
        name="outproj_ln_router",
    )(merged, x, w_o, ln_g.reshape(1, d), ln_b.reshape(1, d), w_router, b_router)


def _onehots(eid_ref):
    tm = eid_ref.shape[1]
    e_iota = lax.broadcasted_iota(I32, (N_EXPERTS, tm), 0)
    oh0 = (eid_ref[0:1, :] == e_iota).astype(F32)
    oh1 = (eid_ref[1:2, :] == e_iota).astype(F32)
    return oh0, oh1


def _count_kernel(eid_ref, cnt_ref):
    @pl.when(pl.program_id(0) == 0)
    def _():
        cnt_ref[...] = jnp.zeros_like(cnt_ref)

    oh0, oh1 = _onehots(eid_ref)
    cnt_ref[...] += jnp.sum(oh0 + oh1, axis=1, keepdims=True)


def _slot_kernel(eid_ref, pstart_ref, dest_ref, run_ref):
    @pl.when(pl.program_id(0) == 0)
    def _():
        run_ref[...] = jnp.zeros_like(run_ref)

    tm = eid_ref.shape[1]
    oh0, oh1 = _onehots(eid_ref)
    both = oh0 + oh1
    earlier = (lax.broadcasted_iota(I32, (tm, tm), 0) < lax.broadcasted_iota(I32, (tm, tm), 1)).astype(BF16)
    pre = jnp.dot(both.astype(BF16), earlier, preferred_element_type=F32)
    base = pre + run_ref[...] + pstart_ref[...]
    dest_ref[0:1, :] = jnp.sum(oh0 * base, axis=0, keepdims=True).astype(I32)
    dest_ref[1:2, :] = jnp.sum(oh1 * base, axis=0, keepdims=True).astype(I32)
    run_ref[...] += jnp.sum(both, axis=1, keepdims=True)


def _expert_counts(eid_t):
    m = eid_t.shape[1]
    tm = min(ROUTE_TILE, m)
    return pl.pallas_call(
        _count_kernel,
        grid=(m // tm,),
        in_specs=[pl.BlockSpec((TOP_K, tm), lambda i: (0, i))],
        out_specs=pl.BlockSpec((N_EXPERTS, 1), lambda i: (0, 0)),
        out_shape=jax.ShapeDtypeStruct((N_EXPERTS, 1), F32),
        compiler_params=_cparams(("arbitrary",)),
        name="expert_counts",
    )(eid_t)


def _expert_slots(eid_t, pstart):
    m = eid_t.shape[1]
    tm = min(ROUTE_TILE, m)
    return pl.pallas_call(
        _slot_kernel,
        grid=(m // tm,),
        in_specs=[pl.BlockSpec((TOP_K, tm), lambda i: (0, i)),
                  pl.BlockSpec((N_EXPERTS, 1), lambda i: (0, 0))],
        out_specs=pl.BlockSpec((TOP_K, tm), lambda i: (0, i)),
        out_shape=jax.ShapeDtypeStruct((TOP_K, m), I32),
        scratch_shapes=[pltpu.VMEM((N_EXPERTS, 1), F32)],
        compiler_params=_cparams(("arbitrary",)),
        name="expert_slots",
    )(eid_t, pstart)


def _row_slab(ref, row, slab):
    return ref.at[pl.ds(pl.multiple_of(row * slab, slab), slab), :]


def _dispatch_kernel(dest_ref, h_ref, xb_in_ref, xb_ref, sem):
    del xb_in_ref
    tm = DISPATCH_TILE
    slab = h_ref.shape[0] // tm
    t0 = pl.program_id(0) * tm

    def copy(n, k):
        return pltpu.make_async_copy(_row_slab(h_ref, n, slab),
                                     _row_slab(xb_ref, dest_ref[TOP_K * (t0 + n) + k], slab), sem)

    def start(n, carry):
        for k in range(TOP_K):
            copy(n, k).start()
        return carry

    lax.fori_loop(0, tm, start, 0, unroll=8)
    for k in range(TOP_K):
        pltpu.make_async_copy(h_ref, xb_ref.at[pl.ds(0, h_ref.shape[0]), :], sem).wait()


DISPATCH_TILE = 512


def _dispatch(dest_flat, h_rows, m, n_rows):
    slab = h_rows.shape[0] // m
    assert m % DISPATCH_TILE == 0
    grid_spec = pltpu.PrefetchScalarGridSpec(
        num_scalar_prefetch=1,
        grid=(m // DISPATCH_TILE,),
        in_specs=[pl.BlockSpec((DISPATCH_TILE * slab, LANES), lambda i, dr: (i, 0)),
                  pl.BlockSpec(memory_space=pl.ANY)],
        out_specs=pl.BlockSpec(memory_space=pl.ANY),
        scratch_shapes=[pltpu.SemaphoreType.DMA(())],
    )
    return pl.pallas_call(
        _dispatch_kernel,
        grid_spec=grid_spec,
        out_shape=jax.ShapeDtypeStruct((n_rows * slab, LANES), F32),
        input_output_aliases={2: 0},
        compiler_params=pltpu.CompilerParams(dimension_semantics=("arbitrary",), has_side_effects=True),
        name="moe_dispatch",
    )(dest_flat, h_rows, jnp.zeros((n_rows * slab, LANES), F32))


def _expert_kernel(item_e_ref, item_row_ref, item_nsub_ref, *refs):
    wg_refs = refs[:MOE_GU_PIECES]
    wu_refs = refs[MOE_GU_PIECES:2 * MOE_GU_PIECES]
    wd_refs = refs[2 * MOE_GU_PIECES:2 * MOE_GU_PIECES + MOE_DOWN_PIECES]
    xb_ref, yb_ref, xin_ref, yout_ref, x_ref, acc_ref, wgu_ref, wdb_ref, sem_in, sem_out = (
        refs[2 * MOE_GU_PIECES + MOE_DOWN_PIECES:])
    del item_e_ref
    it = pl.program_id(0)
    f = pl.program_id(1)
    n_items = pl.num_programs(0)
    nf = pl.num_programs(1)
    kp = wg_refs[0].shape[1]
    kc = kp * MOE_GU_PIECES
    ff = wg_refs[0].shape[2]
    fp = wd_refs[0].shape[1]
    fc = fp * MOE_DOWN_PIECES
    nsub = item_nsub_ref[it]
    row0 = item_row_ref[it]
    nxt = jnp.minimum(it + 1, n_items - 1)
    nsub_next = jnp.where(it + 1 < n_items, item_nsub_ref[nxt], 0)
    prv = jnp.maximum(it - 1, 0)
    nsub_prev = jnp.where(it > 0, item_nsub_ref[prv], 0)
    rb = MOE_ROWS
    slab = x_ref.shape[0] * kc // LANES
    blk = rb * slab

    def stage_rows(ref, s):
        return ref.at[pl.ds(pl.multiple_of(s * blk, blk), blk), :]

    def hbm_rows(ref, item_row, s):
        return ref.at[pl.ds(pl.multiple_of((item_row + s * rb) * slab, blk), blk), :]

    def in_copy(item_row, s):
        return pltpu.make_async_copy(hbm_rows(xb_ref, item_row, s), stage_rows(xin_ref, s), sem_in)

    def out_copy(item_row, s):
        return pltpu.make_async_copy(stage_rows(yout_ref, s), hbm_rows(yb_ref, item_row, s), sem_out)

    def loop(n, fn):
        def body(s, carry):
            fn(s)
            return carry
        lax.fori_loop(0, n, body, 0)

    @pl.when(nsub > 0)
    def _():
        @pl.when(f == 0)
        def _():
            @pl.when(it == 0)
            def _():
                x_ref[...] = jnp.zeros_like(x_ref)
                loop(nsub, lambda s: in_copy(row0, s).start())
            loop(nsub, lambda s: in_copy(row0, s).wait())

            def to_matrix(s):
                rows = pl.ds(pl.multiple_of(s * rb, rb), rb)
                base = pl.multiple_of(s * blk, blk)
                for j in range(slab):
                    c0 = (j * LANES) % kc
                    x_ref[(j * LANES) // kc, rows, c0:c0 + LANES] = (
                        xin_ref[pl.ds(base + j, rb, stride=slab), :].astype(BF16))
            loop(nsub, to_matrix)
            loop(nsub_next, lambda s: in_copy(item_row_ref[nxt], s).start())

        for q in range(MOE_GU_PIECES):
            wgu_ref[q * kp:(q + 1) * kp, :ff] = wg_refs[q][0].astype(BF16)
            wgu_ref[q * kp:(q + 1) * kp, ff:] = wu_refs[q][0].astype(BF16)
        for q in range(MOE_DOWN_PIECES):
            wdb_ref[pl.ds(pl.multiple_of(f * fc + q * fp, 2 * SUBLANES), fp), :] = wd_refs[q][0].astype(BF16)

        def gate_up(start, size):
            rows = pl.ds(pl.multiple_of(start, rb), size)
            part = jnp.dot(x_ref[f, rows, :], wgu_ref[...], preferred_element_type=F32)
            acc_ref[rows, :] = jnp.where(f > 0, acc_ref[rows, :], 0.0) + part

        gate_up(0, 2 * rb)

        def pair(p, carry):
            gate_up(p * (2 * rb), 2 * rb)
            return carry
        lax.fori_loop(1, nsub // 2, pair, 0)

        @pl.when((nsub % 2 == 1) & (nsub > 1))
        def _():
            gate_up((nsub - 1) * rb, rb)

        @pl.when(f == nf - 1)
        def _():
            loop(nsub_prev, lambda s: out_copy(item_row_ref[prv], s).wait())

            def down(start, size):
                rows = pl.ds(pl.multiple_of(start, rb), size)
                gate = acc_ref[rows, :ff]
                act = (gate * _sigmoid(gate) * acc_ref[rows, ff:]).astype(BF16)
                y = jnp.dot(act, wdb_ref[...], preferred_element_type=F32)
                base = pl.multiple_of(start * slab, blk)
                for j in range(slab):
                    yout_ref[pl.ds(base + j, size, stride=slab), :] = y[:, j * LANES:(j + 1) * LANES]

            down(0, 2 * rb)

            def down_pair(p, carry):
                down(p * (2 * rb), 2 * rb)
                return carry
            lax.fori_loop(1, nsub // 2, down_pair, 0)

            @pl.when((nsub % 2 == 1) & (nsub > 1))
            def _():
                down((nsub - 1) * rb, rb)

            loop(nsub, lambda s: out_copy(row0, s).start())

            @pl.when(nsub_next == 0)
            def _():
                loop(nsub, lambda s: out_copy(row0, s).wait())


def _experts(item_e, item_row, item_nsub, w_gate, w_up, w_down, xb):
    d = w_gate.shape[1]
    ff = w_gate.shape[2]
    nk = MOE_K_STEPS
    kc = d // nk
    fc = ff // nk
    assert d % nk == 0 and ff % nk == 0 and kc % LANES == 0 and fc % (2 * SUBLANES) == 0 and ff % LANES == 0
    n_items = item_e.shape[0]
    rows_max = MOE_ROWS * MOE_SUBS_PER_ITEM
    slab = d // LANES

    kp = kc // MOE_GU_PIECES
    fp = fc // MOE_DOWN_PIECES
    assert kc % MOE_GU_PIECES == 0 and fc % MOE_DOWN_PIECES == 0 and kp % (2 * SUBLANES) == 0 and fp % (2 * SUBLANES) == 0

    def piece(rows, cols, pieces, q):
        def index(it, k, ie, ir, ns):
            return ie[it], jnp.where(ns[it] > 0, k, nk - 1) * pieces + q, 0
        return pl.BlockSpec((1, rows, cols), index)

    grid_spec = pltpu.PrefetchScalarGridSpec(
        num_scalar_prefetch=3,
        grid=(n_items, nk),
        in_specs=[
            *[piece(kp, ff, MOE_GU_PIECES, q) for q in range(MOE_GU_PIECES)],
            *[piece(kp, ff, MOE_GU_PIECES, q) for q in range(MOE_GU_PIECES)],
            *[piece(fp, d, MOE_DOWN_PIECES, q) for q in range(MOE_DOWN_PIECES)],
            pl.BlockSpec(memory_space=pl.ANY),
        ],
        out_specs=pl.BlockSpec(memory_space=pl.ANY),
        scratch_shapes=[
            pltpu.VMEM((rows_max * slab, LANES), F32),
            pltpu.VMEM((rows_max * slab, LANES), F32),
            pltpu.VMEM((nk, rows_max, kc), BF16),
            pltpu.VMEM((rows_max, 2 * ff), F32),
            pltpu.VMEM((kc, 2 * ff), BF16),
            pltpu.VMEM((ff, d), BF16),
            pltpu.SemaphoreType.DMA(()),
            pltpu.SemaphoreType.DMA(()),
        ],
    )
    return pl.pallas_call(
        _expert_kernel,
        grid_spec=grid_spec,
        out_shape=jax.ShapeDtypeStruct(xb.shape, F32),
        input_output_aliases={3 + 2 * MOE_GU_PIECES + MOE_DOWN_PIECES: 0},
        compiler_params=pltpu.CompilerParams(dimension_semantics=("arbitrary", "arbitrary"),
                                             vmem_limit_bytes=VMEM_LIMIT_BYTES, has_side_effects=True),
        name="moe_experts",
    )(item_e, item_row, item_nsub, *([w_gate] * MOE_GU_PIECES), *([w_up] * MOE_GU_PIECES),
      *([w_down] * MOE_DOWN_PIECES), xb)


COMBINE_TILE = 256


def _combine_kernel(alpha, dest_ref, yb_ref, h_ref, gate_ref, g_ref, b_ref, o_ref, buf_ref, sem):
    tm = COMBINE_TILE
    i = pl.program_id(0)
    slab = buf_ref.shape[2] // tm

    def issue(tile, half):
        def start(n, carry):
            for k in range(TOP_K):
                pltpu.make_async_copy(_row_slab(yb_ref, dest_ref[TOP_K * (tile * tm + n) + k], slab),
                                      _row_slab(buf_ref.at[half, k], n, slab), sem.at[half]).start()
            return carry
        lax.fori_loop(0, tm, start, 0, unroll=8)

    @pl.when(i == 0)
    def _():
        issue(0, 0)

    @pl.when(i + 1 < pl.num_programs(0))
    def _():
        issue(i + 1, (i + 1) % 2)

    half = i % 2
    for k in range(TOP_K):
        pltpu.make_async_copy(yb_ref.at[pl.ds(0, buf_ref.shape[2]), :], buf_ref.at[half, k], sem.at[half]).wait()
    gate = gate_ref[...]
    g0 = gate[:, 0:1]
    g1 = gate[:, 1:2]
    col = lambda k, j: buf_ref[half, k, pl.ds(j, tm, stride=slab), :]
    ff = jnp.concatenate([g0 * col(0, j) + g1 * col(1, j) for j in range(slab)], axis=1)
    o_ref[...] = _layer_norm(alpha * h_ref[...] + ff, g_ref[...], b_ref[...])


def _combine(dest_flat, yb, h, gates, ln_g, ln_b, alpha):
    m, d = h.shape
    tm = COMBINE_TILE
    assert m % tm == 0
    grid_spec = pltpu.PrefetchScalarGridSpec(
        num_scalar_prefetch=1,
        grid=(m // tm,),
        in_specs=[pl.BlockSpec(memory_space=pl.ANY),
                  pl.BlockSpec((tm, d), lambda i, dr: (i, 0)),
                  pl.BlockSpec((tm, LANES), lambda i, dr: (i, 0)),
                  pl.BlockSpec((1, d), lambda i, dr: (0, 0)),
                  pl.BlockSpec((1, d), lambda i, dr: (0, 0))],
        out_specs=pl.BlockSpec((tm, d), lambda i, dr: (i, 0)),
        scratch_shapes=[pltpu.VMEM((2, TOP_K, tm * (d // LANES), LANES), F32), pltpu.SemaphoreType.DMA((2,))],
    )
    return pl.pallas_call(
        functools.partial(_combine_kernel, alpha),
        grid_spec=grid_spec,
        out_shape=jax.ShapeDtypeStruct((m, d), F32),
        compiler_params=_cparams(("arbitrary",)),
        name="moe_combine_ln",
    )(dest_flat, yb, h, gates, ln_g.reshape(1, d), ln_b.reshape(1, d))


def _moe_tables(counts):
    nsub_e = (counts + MOE_ROWS - 1) // MOE_ROWS
    pstart = (jnp.cumsum(nsub_e) - nsub_e) * MOE_ROWS
    nitem_e = (nsub_e + MOE_SUBS_PER_ITEM - 1) // MOE_SUBS_PER_ITEM
    item_end = jnp.cumsum(nitem_e)
    return nsub_e, pstart, nitem_e, item_end


def _moe(h, h_rows, eid, gates, w_gate, w_up, w_down, ln_g, ln_b, alpha):
    m, d = h.shape
    n_assign = m * TOP_K
    n_blocks = (n_assign + N_EXPERTS * (MOE_ROWS - 1) + MOE_ROWS - 1) // MOE_ROWS
    n_rows = n_blocks * MOE_ROWS
    n_items = N_EXPERTS + n_assign // (MOE_ROWS * MOE_SUBS_PER_ITEM)

    eid_t = eid[:, :TOP_K].T
    counts = _expert_counts(eid_t)[:, 0].astype(I32)
    nsub_e, pstart, nitem_e, item_end = _moe_tables(counts)
    dest_t = _expert_slots(eid_t, pstart.astype(F32).reshape(N_EXPERTS, 1))
    dest_flat = dest_t.T.reshape(-1)

    it = jnp.arange(n_items, dtype=I32)
    total_items = item_end[-1]
    item_e = jnp.minimum(jnp.sum(item_end[None, :] <= it[:, None], axis=1).astype(I32), N_EXPERTS - 1)
    j = it - (item_end - nitem_e)[item_e]
    used = it < total_items
    last_e = item_e[jnp.maximum(total_items - 1, 0)]
    item_nsub = jnp.where(used, jnp.clip(nsub_e[item_e] - j * MOE_SUBS_PER_ITEM, 0, MOE_SUBS_PER_ITEM), 0).astype(I32)
    item_row = jnp.where(used, pstart[item_e] + j * (MOE_ROWS * MOE_SUBS_PER_ITEM), 0).astype(I32)
    item_e = jnp.where(used, item_e, last_e).astype(I32)

    xb = _dispatch(dest_flat, h_rows, m, n_rows)
    yb = _experts(item_e, item_row, item_nsub, w_gate, w_up, w_down, xb)
    return _combine(dest_flat, yb, h, gates, ln_g, ln_b, alpha)


def _pad_cols(w, n):
    return jnp.pad(w, ((0, 0), (0, n - w.shape[1])))


def _pad_rows(w, n):
    return jnp.pad(w, ((0, n - w.shape[0]), (0, 0)))


def kernel(x, w_in, attn_sinks, rw_mu_rkv, rw_mu_wag, rw_w0, rw_w1, rw_w2, rw_a0, rw_a1, rw_a2, rw_g1, rw_g2, rw_k_k, rw_k_a, rw_r_k, rw_lnx_w, rw_lnx_b, p_attn, p_rwkv, w_o, ln1_g, ln1_b, w_group, b_group, w_expert, b_expert, w_gate, w_up, w_down, ln2_g, ln2_b):
    b, t, d = x.shape
    depth = w_in.shape[0]
    m = b * t
    alpha = (2.0 * depth) ** 0.25
    cosb, sinb = _rope_tables(t)
    qkv_w = ATTN_Q_W + 2 * ATTN_KV_W
    rkv_w = 3 * RWKV_W
    h = x
    for l in range(depth):
        hf = h.reshape(m, d)
        hb = hf.astype(BF16)
        qkv = _matmul_cols(hb, w_in[l], 0, qkv_w, F32)
        rkv = _matmul_cols(hb, w_in[l], qkv_w, rkv_w, F32)
        gates = _matmul_cols(hb, w_in[l], qkv_w + rkv_w, 2 * d, BF16)

        y_a = _attention(qkv.reshape(b, t, qkv_w), attn_sinks[l], cosb, sinb)

        lora = lambda w, n: _pad_cols(w, n).astype(BF16)
        lorb = lambda w, n: _pad_rows(w, n).astype(BF16)
        n_w = -(-rw_w1.shape[2] // LANES) * LANES
        n_a = -(-rw_a1.shape[2] // LANES) * LANES
        n_g = -(-rw_g1.shape[2] // LANES) * LANES
        r_, k_, v_, lw_, cum_, kk_, ag_, g_ = _rwkv_prep(
            h, rkv.reshape(b, t, rkv_w), rw_mu_rkv[l], rw_mu_wag[l], rw_w0[l],
            lora(rw_w1[l], n_w), lorb(rw_w2[l], n_w), rw_a0[l], lora(rw_a1[l], n_a), lorb(rw_a2[l], n_a),
            lora(rw_g1[l], n_g), lorb(rw_g2[l], n_g), rw_k_k[l], rw_k_a[l])
        y_r = _wkv_scan(r_, k_, v_, lw_, cum_, kk_, ag_, g_, rw_r_k[l], rw_lnx_w[l], rw_lnx_b[l])

        merged = _merge(y_a.reshape(m, ATTN_Q_W), y_r.reshape(m, RWKV_W), gates,
                        p_attn[l].astype(BF16), p_rwkv[l].astype(BF16))
        w_router = _pad_cols(jnp.concatenate([w_group[l], w_expert[l]], axis=1), LANES)
        w_router_hi = w_router.astype(BF16)
        w_router = jnp.stack([w_router_hi, (w_router - w_router_hi.astype(F32)).astype(BF16)])
        b_router = _pad_cols(jnp.concatenate([b_group[l], b_expert[l]])[None, :], LANES)
        h1, h1_rows, eid, gate = _outproj_router(merged, hf, w_o[l].astype(BF16), ln1_g[l], ln1_b[l],
                                        w_router, b_router, alpha)
        h2 = _moe(h1, h1_rows, eid, gate, w_gate[l], w_up[l], w_down[l], ln2_g[l], ln2_b[l], alpha)
        h = h2.reshape(b, t, d)
    return h
```

```python
import functools

import jax
import jax.numpy as jnp
from jax import lax
from jax.experimental import pallas as pl
from jax.experimental.pallas import tpu as pltpu

F32 = jnp.float32
BF16 = jnp.bfloat16
I32 = jnp.int32

HEAD_DIM = 64
ATTN_Q_HEADS = 16
ATTN_KV_HEADS = 4
ATTN_GROUP = ATTN_Q_HEADS // ATTN_KV_HEADS
ATTN_Q_W = ATTN_Q_HEADS * HEAD_DIM
ATTN_KV_W = ATTN_KV_HEADS * HEAD_DIM
WINDOW = 128
ROPE_THETA = 10000.0
RWKV_HEADS = 16
RWKV_N = 64
RWKV_W = RWKV_HEADS * RWKV_N
RWKV_GN_EPS = 64e-5
N_GROUPS = 8
EXPERTS_PER_GROUP = 8
N_EXPERTS = N_GROUPS * EXPERTS_PER_GROUP
TOP_K = 2
LN_EPS = 1e-5

LANES = 128
SUBLANES = 8
VMEM_LIMIT_BYTES = 56 * 1024 * 1024

WKV_CHUNK = 64
WKV_CHUNKS_PER_STEP = 4
WKV_HEADS_PER_STEP = 8
MOE_ROWS = 128
MOE_SUBS_PER_ITEM = 4
MOE_K_STEPS = 4
MOE_GU_PIECES = 1
MOE_DOWN_PIECES = 1
ROUTE_TILE = 512


def _cparams(sem, vmem=VMEM_LIMIT_BYTES):
    return pltpu.CompilerParams(dimension_semantics=sem, vmem_limit_bytes=vmem)


def _sigmoid(x):
    return 1.0 / (1.0 + jnp.exp(-x))


def _dot(a, b):
    return jnp.dot(a.astype(BF16), b.astype(BF16), preferred_element_type=F32)


def _dot_nt(a, b):
    return lax.dot_general(a.astype(BF16), b.astype(BF16), (((1,), (1,)), ((), ())),
                           preferred_element_type=F32)


def _layer_norm(t, g, b):
    mu = jnp.mean(t, axis=-1, keepdims=True)
    d = t - mu
    var = jnp.mean(d * d, axis=-1, keepdims=True)
    return d * lax.rsqrt(var + LN_EPS) * g + b


def _matmul_kernel(a_ref, b_ref, o_ref):
    o_ref[...] = jnp.dot(a_ref[...], b_ref[...].astype(BF16), preferred_element_type=F32).astype(o_ref.dtype)


def _matmul_cols(a, b, col0, ncols, out_dtype, tm=2048, tn=512):
    m, k = a.shape
    tm = min(tm, m)
    cb = col0 // tn
    assert col0 % tn == 0 and ncols % tn == 0 and m % tm == 0
    return pl.pallas_call(
        _matmul_kernel,
        grid=(m // tm, ncols // tn),
        in_specs=[pl.BlockSpec((tm, k), lambda i, j: (i, 0)),
                  pl.BlockSpec((k, tn), lambda i, j: (0, j + cb))],
        out_specs=pl.BlockSpec((tm, tn), lambda i, j: (i, j)),
        out_shape=jax.ShapeDtypeStruct((m, ncols), out_dtype),
        compiler_params=_cparams(("parallel", "arbitrary")),
        name="inproj_matmul",
    )(a, b)


def _rope(x, cosb, sinb):
    half = HEAD_DIM // 2
    lane = lax.broadcasted_iota(I32, cosb.shape, 1)
    first_half = (lane % HEAD_DIM) < half
    outs = []
    for g in range(x.shape[1] // LANES):
        xg = x[:, g * LANES:(g + 1) * LANES]
        partner = jnp.where(first_half, pltpu.roll(xg, LANES - half, axis=1), pltpu.roll(xg, half, axis=1))
        outs.append(xg * cosb + partner * sinb)
    return outs


def _attn_kernel(sinks_ref, q_ref, kc_ref, kp_ref, vc_ref, vp_ref, cosc_ref, sinc_ref, cosp_ref, sinp_ref, o_ref):
    blk = pl.program_id(1)
    tq = q_ref.shape[1]
    qg = _rope(q_ref[0], cosc_ref[...], sinc_ref[...])
    kcg = _rope(kc_ref[0], cosc_ref[...], sinc_ref[...])
    kpg = _rope(kp_ref[0], cosp_ref[...], sinp_ref[...])
    vc = vc_ref[0]
    vp = vp_ref[0]

    def head(groups, h):
        g = groups[h // 2]
        return g[:, (h % 2) * HEAD_DIM:(h % 2 + 1) * HEAD_DIM]

    rows = ATTN_GROUP * tq
    qi = lax.broadcasted_iota(I32, (rows, 2 * tq), 0) % tq
    kj = lax.broadcasted_iota(I32, (rows, 2 * tq), 1)
    dist = qi + tq - kj
    valid = (dist >= 0) & (dist < WINDOW) & ((blk > 0) | (kj >= tq))
    rid = lax.broadcasted_iota(I32, (rows, 1), 0) // tq
    scale = HEAD_DIM ** -0.5
    for kvh in range(ATTN_KV_HEADS):
        qh = jnp.concatenate([head(qg, kvh * ATTN_GROUP + g) for g in range(ATTN_GROUP)], axis=0)
        kw = jnp.concatenate([head(kpg, kvh), head(kcg, kvh)], axis=0)
        vw = jnp.concatenate([vp[:, kvh * HEAD_DIM:(kvh + 1) * HEAD_DIM],
                              vc[:, kvh * HEAD_DIM:(kvh + 1) * HEAD_DIM]], axis=0)
        s = _dot_nt(qh, kw) * scale
        s = jnp.where(valid, s, -jnp.inf)
        sink = jnp.zeros((rows, 1), F32)
        for g in range(ATTN_GROUP):
            sink = jnp.where(rid == g, sinks_ref[kvh * ATTN_GROUP + g], sink)
        m = jnp.maximum(jnp.max(s, axis=-1, keepdims=True), sink)
        e = jnp.exp(s - m)
        denom = jnp.sum(e, axis=-1, keepdims=True) + jnp.exp(sink - m)
        o = _dot(e, vw) / denom
        for g in range(ATTN_GROUP):
            hq = kvh * ATTN_GROUP + g
            o_ref[0, :, hq * HEAD_DIM:(hq + 1) * HEAD_DIM] = o[g * tq:(g + 1) * tq].astype(o_ref.dtype)


def _attention(qkv, sinks, cosb, sinb):
    b, t, _ = qkv.shape
    tq = WINDOW
    nb = t // tq
    kcol = ATTN_Q_W // ATTN_KV_W
    prev = lambda i: jnp.maximum(i - 1, 0)
    grid_spec = pltpu.PrefetchScalarGridSpec(
        num_scalar_prefetch=0,
        grid=(b, nb),
        in_specs=[
            pl.BlockSpec(memory_space=pltpu.SMEM),
            pl.BlockSpec((1, tq, ATTN_Q_W), lambda bi, i: (bi, i, 0)),
            pl.BlockSpec((1, tq, ATTN_KV_W), lambda bi, i: (bi, i, kcol)),
            pl.BlockSpec((1, tq, ATTN_KV_W), lambda bi, i: (bi, prev(i), kcol)),
            pl.BlockSpec((1, tq, ATTN_KV_W), lambda bi, i: (bi, i, kcol + 1)),
            pl.BlockSpec((1, tq, ATTN_KV_W), lambda bi, i: (bi, prev(i), kcol + 1)),
            pl.BlockSpec((tq, LANES), lambda bi, i: (i, 0)),
            pl.BlockSpec((tq, LANES), lambda bi, i: (i, 0)),
            pl.BlockSpec((tq, LANES), lambda bi, i: (prev(i), 0)),
            pl.BlockSpec((tq, LANES), lambda bi, i: (prev(i), 0)),
        ],
        out_specs=pl.BlockSpec((1, tq, ATTN_Q_W), lambda bi, i: (bi, i, 0)),
    )
    return pl.pallas_call(
        _attn_kernel,
        grid_spec=grid_spec,
        out_shape=jax.ShapeDtypeStruct((b, t, ATTN_Q_W), BF16),
        compiler_params=_cparams(("parallel", "arbitrary")),
        name="swa_attention",
    )(sinks, qkv, qkv, qkv, qkv, qkv, cosb, sinb, cosb, sinb)


def _rope_tables(t):
    inv = 1.0 / (ROPE_THETA ** (jnp.arange(0, HEAD_DIM, 2, dtype=F32) / HEAD_DIM))
    ang = jnp.arange(t, dtype=F32)[:, None] * inv[None, :]
    cos, sin = jnp.cos(ang), jnp.sin(ang)
    reps = LANES // HEAD_DIM
    cosb = jnp.tile(jnp.concatenate([cos, cos], axis=-1), (1, reps))
    sinb = jnp.tile(jnp.concatenate([-sin, sin], axis=-1), (1, reps))
    return cosb, sinb


def _rwkv_prep_kernel(h_ref, hp_ref, r_ref, k_ref, v_ref, rp_ref, kp_ref, vp_ref,
                      mu_rkv_ref, mu_wag_ref, w0_ref, w1_ref, w2_ref, a0_ref, a1_ref, a2_ref,
                      g1_ref, g2_ref,
                      ro_ref, ko_ref, vo_ref, lwo_ref, cumo_ref, ago_ref, go_ref):
    first = pl.program_id(1) == 0
    last_row = SUBLANES - 1

    def shifted(cur, prev_ref):
        prev_row = jnp.where(first, 0.0, prev_ref[0, last_row:last_row + 1, :])
        rowid = lax.broadcasted_iota(I32, cur.shape, 0)
        return jnp.where(rowid == 0, prev_row, pltpu.roll(cur, 1, axis=0))

    h = h_ref[0]
    xx = shifted(h, hp_ref) - h
    xw = h + xx * mu_wag_ref[0:1, :]
    xa = h + xx * mu_wag_ref[1:2, :]
    xg = h + xx * mu_wag_ref[2:3, :]
    w_raw = w0_ref[...] + _dot(jnp.tanh(_dot(xw, w1_ref[...])), w2_ref[...])
    neg = -w_raw
    softplus = jnp.maximum(neg, 0.0) + jnp.log1p(jnp.exp(-jnp.abs(neg)))
    w = -softplus - 0.5
    lw = -jnp.exp(w)
    tm = lw.shape[0]
    row = lax.broadcasted_iota(I32, (tm, tm), 0)
    col = lax.broadcasted_iota(I32, (tm, tm), 1)
    tri = ((row >= col) & (row // WKV_CHUNK == col // WKV_CHUNK)).astype(BF16)
    cum = sum(jnp.dot(tri, piece, preferred_element_type=F32) for piece in _split3(lw))
    ag = _sigmoid(a0_ref[...] + _dot(_dot(xa, a1_ref[...]), a2_ref[...]))
    go_ref[0] = _dot(_sigmoid(_dot(xg, g1_ref[...])), g2_ref[...])

    r = r_ref[0]
    k = k_ref[0]
    v = v_ref[0]
    r = r + (shifted(r, rp_ref) - r) * mu_rkv_ref[0:1, :]
    k = k + (shifted(k, kp_ref) - k) * mu_rkv_ref[1:2, :]
    v = v + (shifted(v, vp_ref) - v) * mu_rkv_ref[2:3, :]
    nchunk = tm // WKV_CHUNK
    for hd in range(RWKV_HEADS):
        sl = slice(hd * RWKV_N, (hd + 1) * RWKV_N)
        for ref, val in ((ro_ref, r), (ko_ref, k), (vo_ref, v), (lwo_ref, lw), (cumo_ref, cum), (ago_ref, ag)):
            ref[0, :, hd] = val[:, sl].reshape(nchunk, WKV_CHUNK, RWKV_N).astype(ref.dtype)


def _rwkv_prep(h, rkv, mu_rkv, mu_wag, w0, w1, w2, a0, a1, a2, g1, g2, tm=256):
    b, t, d = h.shape
    tm = min(tm, t)
    c = RWKV_W
    spt = tm // SUBLANES
    prevblk = lambda i: jnp.maximum(i * spt - 1, 0)
    full = lambda arr: pl.BlockSpec(arr.shape, lambda bi, i: (0,) * arr.ndim)
    row = lambda arr: arr.reshape(1, -1)
    w0, a0 = row(w0), row(a0)
    in_specs = [
        pl.BlockSpec((1, tm, d), lambda bi, i: (bi, i, 0)),
        pl.BlockSpec((1, SUBLANES, d), lambda bi, i: (bi, prevblk(i), 0)),
        pl.BlockSpec((1, tm, c), lambda bi, i: (bi, i, 0)),
        pl.BlockSpec((1, tm, c), lambda bi, i: (bi, i, 1)),
        pl.BlockSpec((1, tm, c), lambda bi, i: (bi, i, 2)),
        pl.BlockSpec((1, SUBLANES, c), lambda bi, i: (bi, prevblk(i), 0)),
        pl.BlockSpec((1, SUBLANES, c), lambda bi, i: (bi, prevblk(i), 1)),
        pl.BlockSpec((1, SUBLANES, c), lambda bi, i: (bi, prevblk(i), 2)),
    ] + [full(a) for a in (mu_rkv, mu_wag, w0, w1, w2, a0, a1, a2, g1, g2)]
    assert tm % WKV_CHUNK == 0
    hm = lambda dt: jax.ShapeDtypeStruct((b, t // WKV_CHUNK, RWKV_HEADS, WKV_CHUNK, RWKV_N), dt)
    hm_spec = pl.BlockSpec((1, tm // WKV_CHUNK, RWKV_HEADS, WKV_CHUNK, RWKV_N), lambda bi, i: (bi, i, 0, 0, 0))
    return pl.pallas_call(
        _rwkv_prep_kernel,
        grid=(b, t // tm),
        in_specs=in_specs,
        out_specs=[hm_spec] * 6 + [pl.BlockSpec((1, tm, c), lambda bi, i: (bi, i, 0))],
        out_shape=[hm(BF16), hm(BF16), hm(BF16), hm(F32), hm(F32), hm(BF16), jax.ShapeDtypeStruct((b, t, c), F32)],
        compiler_params=_cparams(("parallel", "arbitrary")),
        name="rwkv_prep",
    )(h, h, rkv, rkv, rkv, rkv, rkv, rkv, mu_rkv, mu_wag, w0, w1, w2, a0, a1, a2, g1, g2)


def _split3(x):
    hi = x.astype(BF16)
    r1 = x - hi.astype(F32)
    mid = r1.astype(BF16)
    lo = (r1 - mid.astype(F32)).astype(BF16)
    return hi, mid, lo


def _bmm(a, b):
    return jnp.einsum("gmk,gkn->gmn", a.astype(BF16), b.astype(BF16), preferred_element_type=F32)


def _bmm_nt(a, b):
    return jnp.einsum("gmk,gnk->gmn", a.astype(BF16), b.astype(BF16), preferred_element_type=F32)


def _bmm_tn(a, b):
    return jnp.einsum("gtm,gtn->gmn", a.astype(BF16), b.astype(BF16), preferred_element_type=F32)


def _wkv_kernel(r_ref, k_ref, v_ref, lw_ref, cum_ref, ag_ref, g_ref, kk_ref, ka_ref, rk_ref, lnw_ref, lnb_ref,
                y_ref, s_ref, st_ref):
    c = WKV_CHUNK
    n = RWKV_N
    nc, hb = r_ref.shape[1], r_ref.shape[2]
    tc = nc * c
    g = nc * hb

    @pl.when(pl.program_id(2) == 0)
    def _():
        s_ref[...] = jnp.zeros_like(s_ref)

    chunks = lambda ref: ref[0].reshape(g, c, n).astype(F32)
    r, k, v, lw, cum, ag = (chunks(ref) for ref in (r_ref, k_ref, v_ref, lw_ref, cum_ref, ag_ref))
    per_head = lambda t: t.reshape(nc, hb, c, n)
    kk = (per_head(k) * kk_ref[...]).reshape(g, c, n)
    k = (per_head(k) * (1.0 + (per_head(ag) - 1.0) * ka_ref[...])).reshape(g, c, n)
    kk = kk / jnp.maximum(jnp.sqrt(jnp.sum(kk * kk, axis=-1, keepdims=True)), 1e-12)
    dinc = jnp.exp(cum)
    dinv = jnp.exp(-cum)
    rt = r * dinc
    kt = k * dinv
    at = -kk * jnp.exp(cum - lw)
    bt = kk * ag * dinv
    d_chunk = dinc[:, c - 1:c, :]

    row = lax.broadcasted_iota(I32, (1, c, c), 1)
    col = lax.broadcasted_iota(I32, (1, c, c), 2)
    strict = row > col
    incl = row >= col
    eye = jnp.broadcast_to((row == col).astype(F32), (g, c, c))
    p = _bmm_nt(jnp.concatenate([at, rt], axis=1), jnp.concatenate([bt, kt], axis=1))
    a_ab = jnp.where(strict, p[:, :c, :c], 0.0)
    a_ak = jnp.where(strict, p[:, :c, c:], 0.0)
    a_rb = jnp.where(incl, p[:, c:, :c], 0.0)
    a_rk = jnp.where(incl, p[:, c:, c:], 0.0)
    x = _bmm(a_ab, a_ab)
    tinv = eye + a_ab
    levels = c.bit_length() - 2
    for j in range(1, levels):
        both = _bmm(jnp.concatenate([x, tinv], axis=1), x)
        x = both[:, :c]
        tinv = tinv + both[:, c:]
    tinv = tinv + _bmm(tinv, x)
    z = _bmm(a_ak, v)
    ta = _bmm(tinv, jnp.concatenate([at, z], axis=2))
    ry = _bmm(a_rb, ta)
    rp = rt + ry[:, :, :n]
    yv = ry[:, :, n:] + _bmm(a_rk, v)
    moff = _bmm_tn(ta[:, :, :n], bt).reshape(nc, hb, n, n)
    n2 = _bmm_tn(jnp.concatenate([ta[:, :, n:], v], axis=1),
                 jnp.concatenate([bt, kt], axis=1)).reshape(nc, hb, n, n)
    dch = d_chunk.reshape(nc, hb, 1, n)

    s = s_ref[...]
    for ci in range(nc):
        st_ref[ci] = s
        s = (s + _bmm(s, moff[ci]) + n2[ci]) * dch[ci]
    s_ref[...] = s

    y = _bmm_nt(rp, st_ref[...].reshape(g, n, n)) + yv
    mu = jnp.mean(y, axis=-1, keepdims=True)
    yc = y - mu
    var = jnp.mean(yc * yc, axis=-1, keepdims=True)
    y = per_head(yc * lax.rsqrt(var + RWKV_GN_EPS)) * lnw_ref[...] + lnb_ref[...]
    bonus = jnp.sum(per_head(r * k) * rk_ref[...], axis=-1, keepdims=True)
    y = y + bonus * per_head(v)
    y = jnp.concatenate([jnp.concatenate([y[ci, hd] for hd in range(hb)], axis=1) for ci in range(nc)], axis=0)
    y_ref[0] = (y * g_ref[0]).astype(y_ref.dtype)


def _wkv_scan(r, k, v, lw, cum, ag, g, k_k, k_a, r_k, lnx_w, lnx_b):
    b, nchunks, hh, c, n = r.shape
    t = nchunks * c
    nc = min(WKV_CHUNKS_PER_STEP, nchunks)
    tc = nc * c
    hb = WKV_HEADS_PER_STEP
    blk = pl.BlockSpec((1, nc, hb, c, n), lambda bi, hi, i: (bi, i, hi, 0, 0))
    tok = pl.BlockSpec((1, tc, hb * n), lambda bi, hi, i: (bi, i, hi))
    par = pl.BlockSpec((hb, 1, n), lambda bi, hi, i: (hi, 0, 0))
    per_head = lambda arr: arr.reshape(hh, 1, n)
    return pl.pallas_call(
        _wkv_kernel,
        grid=(b, hh // hb, t // tc),
        in_specs=[blk] * 6 + [tok] + [par] * 5,
        out_specs=tok,
        out_shape=jax.ShapeDtypeStruct((b, t, hh * n), BF16),
        scratch_shapes=[pltpu.VMEM((hb, n, n), F32), pltpu.VMEM((nc, hb, n, n), F32)],
        compiler_params=_cparams(("parallel", "parallel", "arbitrary")),
        name="wkv7_scan",
    )(r, k, v, lw, cum, ag, g, per_head(k_k), per_head(k_a), per_head(r_k), per_head(lnx_w), per_head(lnx_b))


def _merge_kernel(ya_ref, yr_ref, ga_ref, gr_ref, pa_ref, pr_ref, o_ref):
    ma = jnp.dot(ya_ref[...], pa_ref[...], preferred_element_type=F32)
    mr = jnp.dot(yr_ref[...], pr_ref[...], preferred_element_type=F32)
    o_ref[...] = (_sigmoid(ga_ref[...].astype(F32)) * ma + _sigmoid(gr_ref[...].astype(F32)) * mr).astype(o_ref.dtype)


def _merge(ya, yr, gates, p_attn, p_rwkv, tm=256):
    m, c = ya.shape
    d = p_attn.shape[1]
    tm = min(tm, m)
    tile = pl.BlockSpec((tm, c), lambda i: (i, 0))
    return pl.pallas_call(
        _merge_kernel,
        grid=(m // tm,),
        in_specs=[tile, tile,
                  pl.BlockSpec((tm, d), lambda i: (i, 0)),
                  pl.BlockSpec((tm, d), lambda i: (i, 1)),
                  pl.BlockSpec((c, d), lambda i: (0, 0)),
                  pl.BlockSpec((c, d), lambda i: (0, 0))],
        out_specs=pl.BlockSpec((tm, d), lambda i: (i, 0)),
        out_shape=jax.ShapeDtypeStruct((m, d), BF16),
        compiler_params=_cparams(("parallel",)),
        name="gated_merge",
    )(ya, yr, gates, gates, p_attn, p_rwkv)


def _outproj_router_kernel(alpha, mg_ref, x_ref, wo_ref, g_ref, b_ref, wr_ref, br_ref,
                           h_ref, hrow_ref, eid_ref, gate_ref):
    mix = jnp.dot(mg_ref[...], wo_ref[...], preferred_element_type=F32)
    h = _layer_norm(alpha * x_ref[...] + mix, g_ref[...], b_ref[...])
    h_ref[...] = h
    slab = h.shape[1] // LANES
    for j in range(slab):
        hrow_ref[pl.ds(j, h.shape[0], stride=slab), :] = h[:, j * LANES:(j + 1) * LANES]
    h_hi = h.astype(BF16)
    h_lo = (h - h_hi.astype(F32)).astype(BF16)
    logits = (jnp.dot(h_hi, wr_ref[0], preferred_element_type=F32)
              + jnp.dot(h_lo, wr_ref[0], preferred_element_type=F32)
              + jnp.dot(h_hi, wr_ref[1], preferred_element_type=F32)) + br_ref[...]
    lane = lax.broadcasted_iota(I32, logits.shape, 1)
    ninf = -jnp.inf
    big = jnp.int32(2 * LANES)
    glog = jnp.where(lane < N_GROUPS, logits, ninf)
    gmax = jnp.max(glog, axis=-1, keepdims=True)
    gidx = jnp.min(jnp.where(glog == gmax, lane, big), axis=-1, keepdims=True)
    gtop = 1.0 / jnp.sum(jnp.exp(glog - gmax), axis=-1, keepdims=True)
    eg = (lane - N_GROUPS) // EXPERTS_PER_GROUP
    in_group = (lane >= N_GROUPS) & (lane < N_GROUPS + N_EXPERTS) & (eg == gidx)
    el = jnp.where(in_group, logits, ninf)
    m1 = jnp.max(el, axis=-1, keepdims=True)
    i1 = jnp.min(jnp.where(el == m1, lane, big), axis=-1, keepdims=True)
    el2 = jnp.where(lane == i1, ninf, el)
    m2 = jnp.max(el2, axis=-1, keepdims=True)
    i2 = jnp.min(jnp.where(el2 == m2, lane, big), axis=-1, keepdims=True)
    t = jnp.exp(m2 - m1)
    p1 = 1.0 / (1.0 + t)
    p2 = t / (1.0 + t)
    eid_ref[...] = jnp.where(lane == 0, i1 - N_GROUPS, jnp.where(lane == 1, i2 - N_GROUPS, 0))
    gate_ref[...] = jnp.where(lane == 0, gtop * p1, jnp.where(lane == 1, gtop * p2, 0.0))


def _outproj_router(merged, x, w_o, ln_g, ln_b, w_router, b_router, alpha, tm=256):
    m, d = x.shape
    tm = min(tm, m)
    tile = pl.BlockSpec((tm, d), lambda i: (i, 0))
    vec = pl.BlockSpec((1, d), lambda i: (0, 0))
    small = pl.BlockSpec((tm, LANES), lambda i: (i, 0))
    return pl.pallas_call(
        functools.partial(_outproj_router_kernel, alpha),
        grid=(m // tm,),
        in_specs=[tile, tile, pl.BlockSpec((d, d), lambda i: (0, 0)), vec, vec,
                  pl.BlockSpec((2, d, LANES), lambda i: (0, 0, 0)), pl.BlockSpec((1, LANES), lambda i: (0, 0))],
        out_specs=[tile, pl.BlockSpec((tm * (d // LANES), LANES), lambda i: (i, 0)), small, small],
        out_shape=[jax.ShapeDtypeStruct((m, d), F32), jax.ShapeDtypeStruct((m * (d // LANES), LANES), F32),
                   jax.ShapeDtypeStruct((m, LANES), I32), jax.ShapeDtypeStruct((m, LANES), F32)],
        compiler_params=_cparams(("parallel",)),
        name="outproj_ln_router",
    )(merged, x, w_o, ln_g.reshape(1, d), ln_b.reshape(1, d), w_router, b_router)


def _onehots(eid_ref):
    tm = eid_ref.shape[1]
    e_iota = lax.broadcasted_iota(I32, (N_EXPERTS, tm), 0)
    oh0 = (eid_ref[0:1, :] == e_iota).astype(F32)
    oh1 = (eid_ref[1:2, :] == e_iota).astype(F32)
    return oh0, oh1


def _count_kernel(eid_ref, cnt_ref):
    @pl.when(pl.program_id(0) == 0)
    def _():
        cnt_ref[...] = jnp.zeros_like(cnt_ref)

    oh0, oh1 = _onehots(eid_ref)
    cnt_ref[...] += jnp.sum(oh0 + oh1, axis=1, keepdims=True)


def _slot_kernel(eid_ref, pstart_ref, dest_ref, run_ref):
    @pl.when(pl.program_id(0) == 0)
    def _():
        run_ref[...] = jnp.zeros_like(run_ref)

    tm = eid_ref.shape[1]
    oh0, oh1 = _onehots(eid_ref)
    both = oh0 + oh1
    earlier = (lax.broadcasted_iota(I32, (tm, tm), 0) < lax.broadcasted_iota(I32, (tm, tm), 1)).astype(BF16)
    pre = jnp.dot(both.astype(BF16), earlier, preferred_element_type=F32)
    base = pre + run_ref[...] + pstart_ref[...]
    dest_ref[0:1, :] = jnp.sum(oh0 * base, axis=0, keepdims=True).astype(I32)
    dest_ref[1:2, :] = jnp.sum(oh1 * base, axis=0, keepdims=True).astype(I32)
    run_ref[...] += jnp.sum(both, axis=1, keepdims=True)


def _expert_counts(eid_t):
    m = eid_t.shape[1]
    tm = min(ROUTE_TILE, m)
    return pl.pallas_call(
        _count_kernel,
        grid=(m // tm,),
        in_specs=[pl.BlockSpec((TOP_K, tm), lambda i: (0, i))],
        out_specs=pl.BlockSpec((N_EXPERTS, 1), lambda i: (0, 0)),
        out_shape=jax.ShapeDtypeStruct((N_EXPERTS, 1), F32),
        compiler_params=_cparams(("arbitrary",)),
        name="expert_counts",
    )(eid_t)


def _expert_slots(eid_t, pstart):
    m = eid_t.shape[1]
    tm = min(ROUTE_TILE, m)
    return pl.pallas_call(
        _slot_kernel,
        grid=(m // tm,),
        in_specs=[pl.BlockSpec((TOP_K, tm), lambda i: (0, i)),
                  pl.BlockSpec((N_EXPERTS, 1), lambda i: (0, 0))],
        out_specs=pl.BlockSpec((TOP_K, tm), lambda i: (0, i)),
        out_shape=jax.ShapeDtypeStruct((TOP_K, m), I32),
        scratch_shapes=[pltpu.VMEM((N_EXPERTS, 1), F32)],
        compiler_params=_cparams(("arbitrary",)),
        name="expert_slots",
    )(eid_t, pstart)


def _row_slab(ref, row, slab):
    return ref.at[pl.ds(pl.multiple_of(row * slab, slab), slab), :]


def _dispatch_kernel(dest_ref, h_ref, xb_in_ref, xb_ref, sem):
    del xb_in_ref
    tm = DISPATCH_TILE
    slab = h_ref.shape[0] // tm
    t0 = pl.program_id(0) * tm

    def copy(n, k):
        return pltpu.make_async_copy(_row_slab(h_ref, n, slab),
                                     _row_slab(xb_ref, dest_ref[TOP_K * (t0 + n) + k], slab), sem)

    def start(n, carry):
        for k in range(TOP_K):
            copy(n, k).start()
        return carry

    lax.fori_loop(0, tm, start, 0, unroll=8)
    for k in range(TOP_K):
        pltpu.make_async_copy(h_ref, xb_ref.at[pl.ds(0, h_ref.shape[0]), :], sem).wait()


DISPATCH_TILE = 512


def _dispatch(dest_flat, h_rows, m, n_rows):
    slab = h_rows.shape[0] // m
    assert m % DISPATCH_TILE == 0
    grid_spec = pltpu.PrefetchScalarGridSpec(
        num_scalar_prefetch=1,
        grid=(m // DISPATCH_TILE,),
        in_specs=[pl.BlockSpec((DISPATCH_TILE * slab, LANES), lambda i, dr: (i, 0)),
                  pl.BlockSpec(memory_space=pl.ANY)],
        out_specs=pl.BlockSpec(memory_space=pl.ANY),
        scratch_shapes=[pltpu.SemaphoreType.DMA(())],
    )
    return pl.pallas_call(
        _dispatch_kernel,
        grid_spec=grid_spec,
        out_shape=jax.ShapeDtypeStruct((n_rows * slab, LANES), F32),
        input_output_aliases={2: 0},
        compiler_params=pltpu.CompilerParams(dimension_semantics=("arbitrary",), has_side_effects=True),
        name="moe_dispatch",
    )(dest_flat, h_rows, jnp.zeros((n_rows * slab, LANES), F32))


def _expert_kernel(item_e_ref, item_row_ref, item_nsub_ref, *refs):
    wg_refs = refs[:MOE_GU_PIECES]
    wu_refs = refs[MOE_GU_PIECES:2 * MOE_GU_PIECES]
    wd_refs = refs[2 * MOE_GU_PIECES:2 * MOE_GU_PIECES + MOE_DOWN_PIECES]
    xb_ref, yb_ref, xin_ref, yout_ref, x_ref, acc_ref, wgu_ref, wdb_ref, sem_in, sem_out = (
        refs[2 * MOE_GU_PIECES + MOE_DOWN_PIECES:])
    del item_e_ref
    it = pl.program_id(0)
    f = pl.program_id(1)
    n_items = pl.num_programs(0)
    nf = pl.num_programs(1)
    kp = wg_refs[0].shape[1]
    kc = kp * MOE_GU_PIECES
    ff = wg_refs[0].shape[2]
    fp = wd_refs[0].shape[1]
    fc = fp * MOE_DOWN_PIECES
    nsub = item_nsub_ref[it]
    row0 = item_row_ref[it]
    nxt = jnp.minimum(it + 1, n_items - 1)
    nsub_next = jnp.where(it + 1 < n_items, item_nsub_ref[nxt], 0)
    prv = jnp.maximum(it - 1, 0)
    nsub_prev = jnp.where(it > 0, item_nsub_ref[prv], 0)
    rb = MOE_ROWS
    slab = x_ref.shape[0] * kc // LANES
    blk = rb * slab

    def stage_rows(ref, s):
        return ref.at[pl.ds(pl.multiple_of(s * blk, blk), blk), :]

    def hbm_rows(ref, item_row, s):
        return ref.at[pl.ds(pl.multiple_of((item_row + s * rb) * slab, blk), blk), :]

    def in_copy(item_row, s):
        return pltpu.make_async_copy(hbm_rows(xb_ref, item_row, s), stage_rows(xin_ref, s), sem_in)

    def out_copy(item_row, s):
        return pltpu.make_async_copy(stage_rows(yout_ref, s), hbm_rows(yb_ref, item_row, s), sem_out)

    def loop(n, fn):
        def body(s, carry):
            fn(s)
            return carry
        lax.fori_loop(0, n, body, 0)

    @pl.when(nsub > 0)
    def _():
        @pl.when(f == 0)
        def _():
            @pl.when(it == 0)
            def _():
                x_ref[...] = jnp.zeros_like(x_ref)
                loop(nsub, lambda s: in_copy(row0, s).start())
            loop(nsub, lambda s: in_copy(row0, s).wait())

            def to_matrix(s):
                rows = pl.ds(pl.multiple_of(s * rb, rb), rb)
                base = pl.multiple_of(s * blk, blk)
                for j in range(slab):
                    c0 = (j * LANES) % kc
                    x_ref[(j * LANES) // kc, rows, c0:c0 + LANES] = (
                        xin_ref[pl.ds(base + j, rb, stride=slab), :].astype(BF16))
            loop(nsub, to_matrix)
            loop(nsub_next, lambda s: in_copy(item_row_ref[nxt], s).start())

        for q in range(MOE_GU_PIECES):
            wgu_ref[q * kp:(q + 1) * kp, :ff] = wg_refs[q][0].astype(BF16)
            wgu_ref[q * kp:(q + 1) * kp, ff:] = wu_refs[q][0].astype(BF16)
        for q in range(MOE_DOWN_PIECES):
            wdb_ref[pl.ds(pl.multiple_of(f * fc + q * fp, 2 * SUBLANES), fp), :] = wd_refs[q][0].astype(BF16)

        def gate_up(start, size):
            rows = pl.ds(pl.multiple_of(start, rb), size)
            part = jnp.dot(x_ref[f, rows, :], wgu_ref[...], preferred_element_type=F32)
            acc_ref[rows, :] = jnp.where(f > 0, acc_ref[rows, :], 0.0) + part

        gate_up(0, 2 * rb)

        def pair(p, carry):
            gate_up(p * (2 * rb), 2 * rb)
            return carry
        lax.fori_loop(1, nsub // 2, pair, 0)

        @pl.when((nsub % 2 == 1) & (nsub > 1))
        def _():
            gate_up((nsub - 1) * rb, rb)

        @pl.when(f == nf - 1)
        def _():
            loop(nsub_prev, lambda s: out_copy(item_row_ref[prv], s).wait())

            def down(start, size):
                rows = pl.ds(pl.multiple_of(start, rb), size)
                gate = acc_ref[rows, :ff]
                act = (gate * _sigmoid(gate) * acc_ref[rows, ff:]).astype(BF16)
                y = jnp.dot(act, wdb_ref[...], preferred_element_type=F32)
                base = pl.multiple_of(start * slab, blk)
                for j in range(slab):
                    yout_ref[pl.ds(base + j, size, stride=slab), :] = y[:, j * LANES:(j + 1) * LANES]

            down(0, 2 * rb)

            def down_pair(p, carry):
                down(p * (2 * rb), 2 * rb)
                return carry
            lax.fori_loop(1, nsub // 2, down_pair, 0)

            @pl.when((nsub % 2 == 1) & (nsub > 1))
            def _():
                down((nsub - 1) * rb, rb)

            loop(nsub, lambda s: out_copy(row0, s).start())

            @pl.when(nsub_next == 0)
            def _():
                loop(nsub, lambda s: out_copy(row0, s).wait())


def _experts(item_e, item_row, item_nsub, w_gate, w_up, w_down, xb):
    d = w_gate.shape[1]
    ff = w_gate.shape[2]
    nk = MOE_K_STEPS
    kc = d // nk
    fc = ff // nk
    assert d % nk == 0 and ff % nk == 0 and kc % LANES == 0 and fc % (2 * SUBLANES) == 0 and ff % LANES == 0
    n_items = item_e.shape[0]
    rows_max = MOE_ROWS * MOE_SUBS_PER_ITEM
    slab = d // LANES

    kp = kc // MOE_GU_PIECES
    fp = fc // MOE_DOWN_PIECES
    assert kc % MOE_GU_PIECES == 0 and fc % MOE_DOWN_PIECES == 0 and kp % (2 * SUBLANES) == 0 and fp % (2 * SUBLANES) == 0

    def piece(rows, cols, pieces, q):
        def index(it, k, ie, ir, ns):
            return ie[it], jnp.where(ns[it] > 0, k, nk - 1) * pieces + q, 0
        return pl.BlockSpec((1, rows, cols), index)

    grid_spec = pltpu.PrefetchScalarGridSpec(
        num_scalar_prefetch=3,
        grid=(n_items, nk),
        in_specs=[
            *[piece(kp, ff, MOE_GU_PIECES, q) for q in range(MOE_GU_PIECES)],
            *[piece(kp, ff, MOE_GU_PIECES, q) for q in range(MOE_GU_PIECES)],
            *[piece(fp, d, MOE_DOWN_PIECES, q) for q in range(MOE_DOWN_PIECES)],
            pl.BlockSpec(memory_space=pl.ANY),
        ],
        out_specs=pl.BlockSpec(memory_space=pl.ANY),
        scratch_shapes=[
            pltpu.VMEM((rows_max * slab, LANES), F32),
            pltpu.VMEM((rows_max * slab, LANES), F32),
            pltpu.VMEM((nk, rows_max, kc), BF16),
            pltpu.VMEM((rows_max, 2 * ff), F32),
            pltpu.VMEM((kc, 2 * ff), BF16),
            pltpu.VMEM((ff, d), BF16),
            pltpu.SemaphoreType.DMA(()),
            pltpu.SemaphoreType.DMA(()),
        ],
    )
    return pl.pallas_call(
        _expert_kernel,
        grid_spec=grid_spec,
        out_shape=jax.ShapeDtypeStruct(xb.shape, F32),
        input_output_aliases={3 + 2 * MOE_GU_PIECES + MOE_DOWN_PIECES: 0},
        compiler_params=pltpu.CompilerParams(dimension_semantics=("arbitrary", "arbitrary"),
                                             vmem_limit_bytes=VMEM_LIMIT_BYTES, has_side_effects=True),
        name="moe_experts",
    )(item_e, item_row, item_nsub, *([w_gate] * MOE_GU_PIECES), *([w_up] * MOE_GU_PIECES),
      *([w_down] * MOE_DOWN_PIECES), xb)


COMBINE_TILE = 256


def _combine_kernel(alpha, dest_ref, yb_ref, h_ref, gate_ref, g_ref, b_ref, o_ref, buf_ref, sem):
    tm = COMBINE_TILE
    i = pl.program_id(0)
    slab = buf_ref.shape[2] // tm

    def issue(tile, half):
        def start(n, carry):
            for k in range(TOP_K):
                pltpu.make_async_copy(_row_slab(yb_ref, dest_ref[TOP_K * (tile * tm + n) + k], slab),
                                      _row_slab(buf_ref.at[half, k], n, slab), sem.at[half]).start()
            return carry
        lax.fori_loop(0, tm, start, 0, unroll=8)

    @pl.when(i == 0)
    def _():
        issue(0, 0)

    @pl.when(i + 1 < pl.num_programs(0))
    def _():
        issue(i + 1, (i + 1) % 2)

    half = i % 2
    for k in range(TOP_K):
        pltpu.make_async_copy(yb_ref.at[pl.ds(0, buf_ref.shape[2]), :], buf_ref.at[half, k], sem.at[half]).wait()
    gate = gate_ref[...]
    g0 = gate[:, 0:1]
    g1 = gate[:, 1:2]
    col = lambda k, j: buf_ref[half, k, pl.ds(j, tm, stride=slab), :]
    ff = jnp.concatenate([g0 * col(0, j) + g1 * col(1, j) for j in range(slab)], axis=1)
    o_ref[...] = _layer_norm(alpha * h_ref[...] + ff, g_ref[...], b_ref[...])


def _combine(dest_flat, yb, h, gates, ln_g, ln_b, alpha):
    m, d = h.shape
    tm = COMBINE_TILE
    assert m % tm == 0
    grid_spec = pltpu.PrefetchScalarGridSpec(
        num_scalar_prefetch=1,
        grid=(m // tm,),
        in_specs=[pl.BlockSpec(memory_space=pl.ANY),
                  pl.BlockSpec((tm, d), lambda i, dr: (i, 0)),
                  pl.BlockSpec((tm, LANES), lambda i, dr: (i, 0)),
                  pl.BlockSpec((1, d), lambda i, dr: (0, 0)),
                  pl.BlockSpec((1, d), lambda i, dr: (0, 0))],
        out_specs=pl.BlockSpec((tm, d), lambda i, dr: (i, 0)),
        scratch_shapes=[pltpu.VMEM((2, TOP_K, tm * (d // LANES), LANES), F32), pltpu.SemaphoreType.DMA((2,))],
    )
    return pl.pallas_call(
        functools.partial(_combine_kernel, alpha),
        grid_spec=grid_spec,
        out_shape=jax.ShapeDtypeStruct((m, d), F32),
        compiler_params=_cparams(("arbitrary",)),
        name="moe_combine_ln",
    )(dest_flat, yb, h, gates, ln_g.reshape(1, d), ln_b.reshape(1, d))


def _moe_tables(counts):
    nsub_e = (counts + MOE_ROWS - 1) // MOE_ROWS
    pstart = (jnp.cumsum(nsub_e) - nsub_e) * MOE_ROWS
    nitem_e = (nsub_e + MOE_SUBS_PER_ITEM - 1) // MOE_SUBS_PER_ITEM
    item_end = jnp.cumsum(nitem_e)
    return nsub_e, pstart, nitem_e, item_end


def _moe(h, h_rows, eid, gates, w_gate, w_up, w_down, ln_g, ln_b, alpha):
    m, d = h.shape
    n_assign = m * TOP_K
    n_blocks = (n_assign + N_EXPERTS * (MOE_ROWS - 1) + MOE_ROWS - 1) // MOE_ROWS
    n_rows = n_blocks * MOE_ROWS
    n_items = N_EXPERTS + n_assign // (MOE_ROWS * MOE_SUBS_PER_ITEM)

    eid_t = eid[:, :TOP_K].T
    counts = _expert_counts(eid_t)[:, 0].astype(I32)
    nsub_e, pstart, nitem_e, item_end = _moe_tables(counts)
    dest_t = _expert_slots(eid_t, pstart.astype(F32).reshape(N_EXPERTS, 1))
    dest_flat = dest_t.T.reshape(-1)

    it = jnp.arange(n_items, dtype=I32)
    total_items = item_end[-1]
    item_e = jnp.minimum(jnp.sum(item_end[None, :] <= it[:, None], axis=1).astype(I32), N_EXPERTS - 1)
    j = it - (item_end - nitem_e)[item_e]
    used = it < total_items
    last_e = item_e[jnp.maximum(total_items - 1, 0)]
    item_nsub = jnp.where(used, jnp.clip(nsub_e[item_e] - j * MOE_SUBS_PER_ITEM, 0, MOE_SUBS_PER_ITEM), 0).astype(I32)
    item_row = jnp.where(used, pstart[item_e] + j * (MOE_ROWS * MOE_SUBS_PER_ITEM), 0).astype(I32)
    item_e = jnp.where(used, item_e, last_e).astype(I32)

    xb = _dispatch(dest_flat, h_rows, m, n_rows)
    yb = _experts(item_e, item_row, item_nsub, w_gate, w_up, w_down, xb)
    return _combine(dest_flat, yb, h, gates, ln_g, ln_b, alpha)


def _pad_cols(w, n):
    return jnp.pad(w, ((0, 0), (0, n - w.shape[1])))


def _pad_rows(w, n):
    return jnp.pad(w, ((0, n - w.shape[0]), (0, 0)))


def kernel(x, w_in, attn_sinks, rw_mu_rkv, rw_mu_wag, rw_w0, rw_w1, rw_w2, rw_a0, rw_a1, rw_a2, rw_g1, rw_g2, rw_k_k, rw_k_a, rw_r_k, rw_lnx_w, rw_lnx_b, p_attn, p_rwkv, w_o, ln1_g, ln1_b, w_group, b_group, w_expert, b_expert, w_gate, w_up, w_down, ln2_g, ln2_b):
    b, t, d = x.shape
    depth = w_in.shape[0]
    m = b * t
    alpha = (2.0 * depth) ** 0.25
    cosb, sinb = _rope_tables(t)
    qkv_w = ATTN_Q_W + 2 * ATTN_KV_W
    rkv_w = 3 * RWKV_W
    h = x
    for l in range(depth):
        hf = h.reshape(m, d)
        hb = hf.astype(BF16)
        qkv = _matmul_cols(hb, w_in[l], 0, qkv_w, F32)
        rkv = _matmul_cols(hb, w_in[l], qkv_w, rkv_w, F32)
        gates = _matmul_cols(hb, w_in[l], qkv_w + rkv_w, 2 * d, BF16)

        y_a = _attention(qkv.reshape(b, t, qkv_w), attn_sinks[l], cosb, sinb)

        lora = lambda w, n: _pad_cols(w, n).astype(BF16)
        lorb = lambda w, n: _pad_rows(w, n).astype(BF16)
        n_w = -(-rw_w1.shape[2] // LANES) * LANES
        n_a = -(-rw_a1.shape[2] // LANES) * LANES
        n_g = -(-rw_g1.shape[2] // LANES) * LANES
        r_, k_, v_, lw_, cum_, ag_, g_ = _rwkv_prep(
            h, rkv.reshape(b, t, rkv_w), rw_mu_rkv[l], rw_mu_wag[l], rw_w0[l],
            lora(rw_w1[l], n_w), lorb(rw_w2[l], n_w), rw_a0[l], lora(rw_a1[l], n_a), lorb(rw_a2[l], n_a),
            lora(rw_g1[l], n_g), lorb(rw_g2[l], n_g))
        y_r = _wkv_scan(r_, k_, v_, lw_, cum_, ag_, g_, rw_k_k[l], rw_k_a[l], rw_r_k[l], rw_lnx_w[l], rw_lnx_b[l])

        merged = _merge(y_a.reshape(m, ATTN_Q_W), y_r.reshape(m, RWKV_W), gates,
                        p_attn[l].astype(BF16), p_rwkv[l].astype(BF16))
        w_router = _pad_cols(jnp.concatenate([w_group[l], w_expert[l]], axis=1), LANES)
        w_router_hi = w_router.astype(BF16)
        w_router = jnp.stack([w_router_hi, (w_router - w_router_hi.astype(F32)).astype(BF16)])
        b_router = _pad_cols(jnp.concatenate([b_group[l], b_expert[l]])[None, :], LANES)
        h1, h1_rows, eid, gate = _outproj_router(merged, hf, w_o[l].astype(BF16), ln1_g[l], ln1_b[l],
                                        w_router, b_router, alpha)
        h2 = _moe(h1, h1_rows, eid, gate, w_gate[l], w_up[l], w_down[l], ln2_g[l], ln2_b[l], alpha)
        h = h2.reshape(b, t, d)
    return h
```

```python
import functools

import jax
import jax.numpy as jnp
from jax import lax
from jax.experimental import pallas as pl
from jax.experimental.pallas import tpu as pltpu

F32 = jnp.float32
BF16 = jnp.bfloat16
I32 = jnp.int32

HEAD_DIM = 64
ATTN_Q_HEADS = 16
ATTN_KV_HEADS = 4
ATTN_GROUP = ATTN_Q_HEADS // ATTN_KV_HEADS
ATTN_Q_W = ATTN_Q_HEADS * HEAD_DIM
ATTN_KV_W = ATTN_KV_HEADS * HEAD_DIM
WINDOW = 128
ROPE_THETA = 10000.0
RWKV_HEADS = 16
RWKV_N = 64
RWKV_W = RWKV_HEADS * RWKV_N
RWKV_GN_EPS = 64e-5
N_GROUPS = 8
EXPERTS_PER_GROUP = 8
N_EXPERTS = N_GROUPS * EXPERTS_PER_GROUP
TOP_K = 2
LN_EPS = 1e-5

LANES = 128
SUBLANES = 8
VMEM_LIMIT_BYTES = 56 * 1024 * 1024

WKV_CHUNK = 64
WKV_CHUNKS_PER_STEP = 4
WKV_HEADS_PER_STEP = 8
MOE_ROWS = 128
MOE_SUBS_PER_ITEM = 4
MOE_K_STEPS = 4
MOE_GU_PIECES = 1
MOE_DOWN_PIECES = 1
ROUTE_TILE = 512


def _cparams(sem, vmem=VMEM_LIMIT_BYTES):
    return pltpu.CompilerParams(dimension_semantics=sem, vmem_limit_bytes=vmem)


def _sigmoid(x):
    return 1.0 / (1.0 + jnp.exp(-x))


def _dot(a, b):
    return jnp.dot(a.astype(BF16), b.astype(BF16), preferred_element_type=F32)


def _dot_nt(a, b):
    return lax.dot_general(a.astype(BF16), b.astype(BF16), (((1,), (1,)), ((), ())),
                           preferred_element_type=F32)


def _layer_norm(t, g, b):
    mu = jnp.mean(t, axis=-1, keepdims=True)
    d = t - mu
    var = jnp.mean(d * d, axis=-1, keepdims=True)
    return d * lax.rsqrt(var + LN_EPS) * g + b


def _matmul_kernel(a_ref, b_ref, o_ref):
    o_ref[...] = jnp.dot(a_ref[...], b_ref[...].astype(BF16), preferred_element_type=F32).astype(o_ref.dtype)


def _matmul_cols(a, b, col0, ncols, out_dtype, tm=2048, tn=512):
    m, k = a.shape
    tm = min(tm, m)
    cb = col0 // tn
    assert col0 % tn == 0 and ncols % tn == 0 and m % tm == 0
    return pl.pallas_call(
        _matmul_kernel,
        grid=(m // tm, ncols // tn),
        in_specs=[pl.BlockSpec((tm, k), lambda i, j: (i, 0)),
                  pl.BlockSpec((k, tn), lambda i, j: (0, j + cb))],
        out_specs=pl.BlockSpec((tm, tn), lambda i, j: (i, j)),
        out_shape=jax.ShapeDtypeStruct((m, ncols), out_dtype),
        compiler_params=_cparams(("parallel", "arbitrary")),
        name="inproj_matmul",
    )(a, b)


def _rope(x, cosb, sinb):
    half = HEAD_DIM // 2
    lane = lax.broadcasted_iota(I32, cosb.shape, 1)
    first_half = (lane % HEAD_DIM) < half
    outs = []
    for g in range(x.shape[1] // LANES):
        xg = x[:, g * LANES:(g + 1) * LANES]
        partner = jnp.where(first_half, pltpu.roll(xg, LANES - half, axis=1), pltpu.roll(xg, half, axis=1))
        outs.append(xg * cosb + partner * sinb)
    return outs


def _attn_kernel(sinks_ref, q_ref, kc_ref, kp_ref, vc_ref, vp_ref, cosc_ref, sinc_ref, cosp_ref, sinp_ref, o_ref):
    blk = pl.program_id(1)
    tq = q_ref.shape[1]
    qg = _rope(q_ref[0], cosc_ref[...], sinc_ref[...])
    kcg = _rope(kc_ref[0], cosc_ref[...], sinc_ref[...])
    kpg = _rope(kp_ref[0], cosp_ref[...], sinp_ref[...])
    vc = vc_ref[0]
    vp = vp_ref[0]

    def head(groups, h):
        g = groups[h // 2]
        return g[:, (h % 2) * HEAD_DIM:(h % 2 + 1) * HEAD_DIM]

    rows = ATTN_GROUP * tq
    qi = lax.broadcasted_iota(I32, (rows, 2 * tq), 0) % tq
    kj = lax.broadcasted_iota(I32, (rows, 2 * tq), 1)
    dist = qi + tq - kj
    valid = (dist >= 0) & (dist < WINDOW) & ((blk > 0) | (kj >= tq))
    rid = lax.broadcasted_iota(I32, (rows, 1), 0) // tq
    scale = HEAD_DIM ** -0.5
    for kvh in range(ATTN_KV_HEADS):
        qh = jnp.concatenate([head(qg, kvh * ATTN_GROUP + g) for g in range(ATTN_GROUP)], axis=0)
        kw = jnp.concatenate([head(kpg, kvh), head(kcg, kvh)], axis=0)
        vw = jnp.concatenate([vp[:, kvh * HEAD_DIM:(kvh + 1) * HEAD_DIM],
                              vc[:, kvh * HEAD_DIM:(kvh + 1) * HEAD_DIM]], axis=0)
        s = _dot_nt(qh, kw) * scale
        s = jnp.where(valid, s, -jnp.inf)
        sink = jnp.zeros((rows, 1), F32)
        for g in range(ATTN_GROUP):
            sink = jnp.where(rid == g, sinks_ref[kvh * ATTN_GROUP + g], sink)
        m = jnp.maximum(jnp.max(s, axis=-1, keepdims=True), sink)
        e = jnp.exp(s - m)
        denom = jnp.sum(e, axis=-1, keepdims=True) + jnp.exp(sink - m)
        o = _dot(e, vw) / denom
        for g in range(ATTN_GROUP):
            hq = kvh * ATTN_GROUP + g
            o_ref[0, :, hq * HEAD_DIM:(hq + 1) * HEAD_DIM] = o[g * tq:(g + 1) * tq].astype(o_ref.dtype)


def _attention(qkv, sinks, cosb, sinb):
    b, t, _ = qkv.shape
    tq = WINDOW
    nb = t // tq
    kcol = ATTN_Q_W // ATTN_KV_W
    prev = lambda i: jnp.maximum(i - 1, 0)
    grid_spec = pltpu.PrefetchScalarGridSpec(
        num_scalar_prefetch=0,
        grid=(b, nb),
        in_specs=[
            pl.BlockSpec(memory_space=pltpu.SMEM),
            pl.BlockSpec((1, tq, ATTN_Q_W), lambda bi, i: (bi, i, 0)),
            pl.BlockSpec((1, tq, ATTN_KV_W), lambda bi, i: (bi, i, kcol)),
            pl.BlockSpec((1, tq, ATTN_KV_W), lambda bi, i: (bi, prev(i), kcol)),
            pl.BlockSpec((1, tq, ATTN_KV_W), lambda bi, i: (bi, i, kcol + 1)),
            pl.BlockSpec((1, tq, ATTN_KV_W), lambda bi, i: (bi, prev(i), kcol + 1)),
            pl.BlockSpec((tq, LANES), lambda bi, i: (i, 0)),
            pl.BlockSpec((tq, LANES), lambda bi, i: (i, 0)),
            pl.BlockSpec((tq, LANES), lambda bi, i: (prev(i), 0)),
            pl.BlockSpec((tq, LANES), lambda bi, i: (prev(i), 0)),
        ],
        out_specs=pl.BlockSpec((1, tq, ATTN_Q_W), lambda bi, i: (bi, i, 0)),
    )
    return pl.pallas_call(
        _attn_kernel,
        grid_spec=grid_spec,
        out_shape=jax.ShapeDtypeStruct((b, t, ATTN_Q_W), BF16),
        compiler_params=_cparams(("parallel", "arbitrary")),
        name="swa_attention",
    )(sinks, qkv, qkv, qkv, qkv, qkv, cosb, sinb, cosb, sinb)


def _rope_tables(t):
    inv = 1.0 / (ROPE_THETA ** (jnp.arange(0, HEAD_DIM, 2, dtype=F32) / HEAD_DIM))
    ang = jnp.arange(t, dtype=F32)[:, None] * inv[None, :]
    cos, sin = jnp.cos(ang), jnp.sin(ang)
    reps = LANES // HEAD_DIM
    cosb = jnp.tile(jnp.concatenate([cos, cos], axis=-1), (1, reps))
    sinb = jnp.tile(jnp.concatenate([-sin, sin], axis=-1), (1, reps))
    return cosb, sinb


def _rwkv_prep_kernel(h_ref, hp_ref, r_ref, k_ref, v_ref, rp_ref, kp_ref, vp_ref,
                      mu_rkv_ref, mu_wag_ref, w0_ref, w1_ref, w2_ref, a0_ref, a1_ref, a2_ref,
                      g1_ref, g2_ref,
                      ro_ref, ko_ref, vo_ref, lwo_ref, cumo_ref, ago_ref, go_ref):
    first = pl.program_id(1) == 0
    last_row = SUBLANES - 1

    def shifted(cur, prev_ref):
        prev_row = jnp.where(first, 0.0, prev_ref[0, last_row:last_row + 1, :])
        rowid = lax.broadcasted_iota(I32, cur.shape, 0)
        return jnp.where(rowid == 0, prev_row, pltpu.roll(cur, 1, axis=0))

    h = h_ref[0]
    xx = shifted(h, hp_ref) - h
    xw = h + xx * mu_wag_ref[0:1, :]
    xa = h + xx * mu_wag_ref[1:2, :]
    xg = h + xx * mu_wag_ref[2:3, :]
    w_raw = w0_ref[...] + _dot(jnp.tanh(_dot(xw, w1_ref[...])), w2_ref[...])
    neg = -w_raw
    softplus = jnp.maximum(neg, 0.0) + jnp.log1p(jnp.exp(-jnp.abs(neg)))
    w = -softplus - 0.5
    lw = -jnp.exp(w)
    tm = lw.shape[0]
    row = lax.broadcasted_iota(I32, (tm, tm), 0)
    col = lax.broadcasted_iota(I32, (tm, tm), 1)
    tri = ((row >= col) & (row // WKV_CHUNK == col // WKV_CHUNK)).astype(BF16)
    cum = sum(jnp.dot(tri, piece, preferred_element_type=F32) for piece in _split3(lw))
    ag = _sigmoid(a0_ref[...] + _dot(_dot(xa, a1_ref[...]), a2_ref[...]))
    go_ref[0] = _dot(_sigmoid(_dot(xg, g1_ref[...])), g2_ref[...])

    r = r_ref[0]
    k = k_ref[0]
    v = v_ref[0]
    r = r + (shifted(r, rp_ref) - r) * mu_rkv_ref[0:1, :]
    k = k + (shifted(k, kp_ref) - k) * mu_rkv_ref[1:2, :]
    v = v + (shifted(v, vp_ref) - v) * mu_rkv_ref[2:3, :]
    nchunk = tm // WKV_CHUNK
    for hd in range(RWKV_HEADS):
        sl = slice(hd * RWKV_N, (hd + 1) * RWKV_N)
        for ref, val in ((ro_ref, r), (ko_ref, k), (vo_ref, v), (lwo_ref, lw), (cumo_ref, cum), (ago_ref, ag)):
            ref[0, :, hd] = val[:, sl].reshape(nchunk, WKV_CHUNK, RWKV_N).astype(ref.dtype)


def _rwkv_prep(h, rkv, mu_rkv, mu_wag, w0, w1, w2, a0, a1, a2, g1, g2, tm=256):
    b, t, d = h.shape
    tm = min(tm, t)
    c = RWKV_W
    spt = tm // SUBLANES
    prevblk = lambda i: jnp.maximum(i * spt - 1, 0)
    full = lambda arr: pl.BlockSpec(arr.shape, lambda bi, i: (0,) * arr.ndim)
    row = lambda arr: arr.reshape(1, -1)
    w0, a0 = row(w0), row(a0)
    in_specs = [
        pl.BlockSpec((1, tm, d), lambda bi, i: (bi, i, 0)),
        pl.BlockSpec((1, SUBLANES, d), lambda bi, i: (bi, prevblk(i), 0)),
        pl.BlockSpec((1, tm, c), lambda bi, i: (bi, i, 0)),
        pl.BlockSpec((1, tm, c), lambda bi, i: (bi, i, 1)),
        pl.BlockSpec((1, tm, c), lambda bi, i: (bi, i, 2)),
        pl.BlockSpec((1, SUBLANES, c), lambda bi, i: (bi, prevblk(i), 0)),
        pl.BlockSpec((1, SUBLANES, c), lambda bi, i: (bi, prevblk(i), 1)),
        pl.BlockSpec((1, SUBLANES, c), lambda bi, i: (bi, prevblk(i), 2)),
    ] + [full(a) for a in (mu_rkv, mu_wag, w0, w1, w2, a0, a1, a2, g1, g2)]
    assert tm % WKV_CHUNK == 0
    hm = lambda dt: jax.ShapeDtypeStruct((b, t // WKV_CHUNK, RWKV_HEADS, WKV_CHUNK, RWKV_N), dt)
    hm_spec = pl.BlockSpec((1, tm // WKV_CHUNK, RWKV_HEADS, WKV_CHUNK, RWKV_N), lambda bi, i: (bi, i, 0, 0, 0))
    return pl.pallas_call(
        _rwkv_prep_kernel,
        grid=(b, t // tm),
        in_specs=in_specs,
        out_specs=[hm_spec] * 6 + [pl.BlockSpec((1, tm, c), lambda bi, i: (bi, i, 0))],
        out_shape=[hm(BF16), hm(BF16), hm(BF16), hm(F32), hm(F32), hm(BF16), jax.ShapeDtypeStruct((b, t, c), F32)],
        compiler_params=_cparams(("parallel", "arbitrary")),
        name="rwkv_prep",
    )(h, h, rkv, rkv, rkv, rkv, rkv, rkv, mu_rkv, mu_wag, w0, w1, w2, a0, a1, a2, g1, g2)


def _split3(x):
    hi = x.astype(BF16)
    r1 = x - hi.astype(F32)
    mid = r1.astype(BF16)
    lo = (r1 - mid.astype(F32)).astype(BF16)
    return hi, mid, lo


def _bmm(a, b):
    return jnp.einsum("gmk,gkn->gmn", a.astype(BF16), b.astype(BF16), preferred_element_type=F32)


def _bmm_nt(a, b):
    return jnp.einsum("gmk,gnk->gmn", a.astype(BF16), b.astype(BF16), preferred_element_type=F32)


def _bmm_tn(a, b):
    return jnp.einsum("gtm,gtn->gmn", a.astype(BF16), b.astype(BF16), preferred_element_type=F32)


def _wkv_kernel(r_ref, k_ref, v_ref, lw_ref, cum_ref, ag_ref, g_ref, kk_ref, ka_ref, rk_ref, lnw_ref, lnb_ref,
                y_ref, s_ref, st_ref):
    c = WKV_CHUNK
    n = RWKV_N
    nc, hb = r_ref.shape[1], r_ref.shape[2]
    tc = nc * c
    g = nc * hb

    @pl.when(pl.program_id(2) == 0)
    def _():
        s_ref[...] = jnp.zeros_like(s_ref)

    chunks = lambda ref: ref[0].reshape(g, c, n).astype(F32)
    r, k, v, lw, cum, ag = (chunks(ref) for ref in (r_ref, k_ref, v_ref, lw_ref, cum_ref, ag_ref))
    per_head = lambda t: t.reshape(nc, hb, c, n)
    kk = (per_head(k) * kk_ref[...]).reshape(g, c, n)
    k = (per_head(k) * (1.0 + (per_head(ag) - 1.0) * ka_ref[...])).reshape(g, c, n)
    kk = kk / jnp.maximum(jnp.sqrt(jnp.sum(kk * kk, axis=-1, keepdims=True)), 1e-12)
    dinc = jnp.exp(cum)
    dinv = jnp.exp(-cum)
    rt = r * dinc
    kt = k * dinv
    at = -kk * jnp.exp(cum - lw)
    bt = kk * ag * dinv
    d_chunk = dinc[:, c - 1:c, :]

    row = lax.broadcasted_iota(I32, (1, c, c), 1)
    col = lax.broadcasted_iota(I32, (1, c, c), 2)
    strict = row > col
    incl = row >= col
    eye = jnp.broadcast_to((row == col).astype(F32), (g, c, c))
    p = _bmm_nt(jnp.concatenate([at, rt], axis=1), jnp.concatenate([bt, kt], axis=1))
    a_ab = jnp.where(strict, p[:, :c, :c], 0.0)
    a_ak = jnp.where(strict, p[:, :c, c:], 0.0)
    a_rb = jnp.where(incl, p[:, c:, :c], 0.0)
    a_rk = jnp.where(incl, p[:, c:, c:], 0.0)
    x = _bmm(a_ab, a_ab)
    tinv = eye + a_ab
    levels = c.bit_length() - 2
    for j in range(1, levels):
        both = _bmm(jnp.concatenate([x, tinv], axis=1), x)
        x = both[:, :c]
        tinv = tinv + both[:, c:]
    tinv = tinv + _bmm(tinv, x)
    z = _bmm(a_ak, v)
    ta = _bmm(tinv, jnp.concatenate([at, z], axis=2))
    ry = _bmm(a_rb, ta)
    rp = rt + ry[:, :, :n]
    yv = ry[:, :, n:] + _bmm(a_rk, v)
    moff = _bmm_tn(ta[:, :, :n], bt).reshape(nc, hb, n, n)
    n2 = _bmm_tn(jnp.concatenate([ta[:, :, n:], v], axis=1),
                 jnp.concatenate([bt, kt], axis=1)).reshape(nc, hb, n, n)
    dch = d_chunk.reshape(nc, hb, 1, n)

    s = s_ref[...]
    for ci in range(nc):
        st_ref[ci] = s
        s = (s + _bmm(s, moff[ci]) + n2[ci]) * dch[ci]
    s_ref[...] = s

    y = _bmm_nt(rp, st_ref[...].reshape(g, n, n)) + yv
    mu = jnp.mean(y, axis=-1, keepdims=True)
    yc = y - mu
    var = jnp.mean(yc * yc, axis=-1, keepdims=True)
    y = per_head(yc * lax.rsqrt(var + RWKV_GN_EPS)) * lnw_ref[...] + lnb_ref[...]
    bonus = jnp.sum(per_head(r * k) * rk_ref[...], axis=-1, keepdims=True)
    y = y + bonus * per_head(v)
    y = jnp.concatenate([jnp.concatenate([y[ci, hd] for hd in range(hb)], axis=1) for ci in range(nc)], axis=0)
    y_ref[0] = (y * g_ref[0]).astype(y_ref.dtype)


def _wkv_scan(r, k, v, lw, cum, ag, g, k_k, k_a, r_k, lnx_w, lnx_b):
    b, nchunks, hh, c, n = r.shape
    t = nchunks * c
    nc = min(WKV_CHUNKS_PER_STEP, nchunks)
    tc = nc * c
    hb = WKV_HEADS_PER_STEP
    blk = pl.BlockSpec((1, nc, hb, c, n), lambda bi, hi, i: (bi, i, hi, 0, 0))
    tok = pl.BlockSpec((1, tc, hb * n), lambda bi, hi, i: (bi, i, hi))
    par = pl.BlockSpec((hb, 1, n), lambda bi, hi, i: (hi, 0, 0))
    per_head = lambda arr: arr.reshape(hh, 1, n)
    return pl.pallas_call(
        _wkv_kernel,
        grid=(b, hh // hb, t // tc),
        in_specs=[blk] * 6 + [tok] + [par] * 5,
        out_specs=tok,
        out_shape=jax.ShapeDtypeStruct((b, t, hh * n), BF16),
        scratch_shapes=[pltpu.VMEM((hb, n, n), F32), pltpu.VMEM((nc, hb, n, n), F32)],
        compiler_params=_cparams(("parallel", "parallel", "arbitrary")),
        name="wkv7_scan",
    )(r, k, v, lw, cum, ag, g, per_head(k_k), per_head(k_a), per_head(r_k), per_head(lnx_w), per_head(lnx_b))


def _merge_kernel(ya_ref, yr_ref, ga_ref, gr_ref, pa_ref, pr_ref, o_ref):
    ma = jnp.dot(ya_ref[...], pa_ref[...], preferred_element_type=F32)
    mr = jnp.dot(yr_ref[...], pr_ref[...], preferred_element_type=F32)
    o_ref[...] = (_sigmoid(ga_ref[...].astype(F32)) * ma + _sigmoid(gr_ref[...].astype(F32)) * mr).astype(o_ref.dtype)


def _merge(ya, yr, gates, p_attn, p_rwkv, tm=256):
    m, c = ya.shape
    d = p_attn.shape[1]
    tm = min(tm, m)
    tile = pl.BlockSpec((tm, c), lambda i: (i, 0))
    return pl.pallas_call(
        _merge_kernel,
        grid=(m // tm,),
        in_specs=[tile, tile,
                  pl.BlockSpec((tm, d), lambda i: (i, 0)),
                  pl.BlockSpec((tm, d), lambda i: (i, 1)),
                  pl.BlockSpec((c, d), lambda i: (0, 0)),
                  pl.BlockSpec((c, d), lambda i: (0, 0))],
        out_specs=pl.BlockSpec((tm, d), lambda i: (i, 0)),
        out_shape=jax.ShapeDtypeStruct((m, d), BF16),
        compiler_params=_cparams(("parallel",)),
        name="gated_merge",
    )(ya, yr, gates, gates, p_attn, p_rwkv)


def _outproj_router_kernel(alpha, mg_ref, x_ref, wo_ref, g_ref, b_ref, wr_ref, br_ref,
                           h_ref, hrow_ref, eid_ref, gate_ref):
    mix = jnp.dot(mg_ref[...], wo_ref[...], preferred_element_type=F32)
    h = _layer_norm(alpha * x_ref[...] + mix, g_ref[...], b_ref[...])
    h_ref[...] = h
    slab = h.shape[1] // LANES
    for j in range(slab):
        hrow_ref[pl.ds(j, h.shape[0], stride=slab), :] = h[:, j * LANES:(j + 1) * LANES]
    h_hi = h.astype(BF16)
    h_lo = (h - h_hi.astype(F32)).astype(BF16)
    logits = (jnp.dot(h_hi, wr_ref[0], preferred_element_type=F32)
              + jnp.dot(h_lo, wr_ref[0], preferred_element_type=F32)
              + jnp.dot(h_hi, wr_ref[1], preferred_element_type=F32)) + br_ref[...]
    lane = lax.broadcasted_iota(I32, logits.shape, 1)
    ninf = -jnp.inf
    big = jnp.int32(2 * LANES)
    glog = jnp.where(lane < N_GROUPS, logits, ninf)
    gmax = jnp.max(glog, axis=-1, keepdims=True)
    gidx = jnp.min(jnp.where(glog == gmax, lane, big), axis=-1, keepdims=True)
    gtop = 1.0 / jnp.sum(jnp.exp(glog - gmax), axis=-1, keepdims=True)
    eg = (lane - N_GROUPS) // EXPERTS_PER_GROUP
    in_group = (lane >= N_GROUPS) & (lane < N_GROUPS + N_EXPERTS) & (eg == gidx)
    el = jnp.where(in_group, logits, ninf)
    m1 = jnp.max(el, axis=-1, keepdims=True)
    i1 = jnp.min(jnp.where(el == m1, lane, big), axis=-1, keepdims=True)
    el2 = jnp.where(lane == i1, ninf, el)
    m2 = jnp.max(el2, axis=-1, keepdims=True)
    i2 = jnp.min(jnp.where(el2 == m2, lane, big), axis=-1, keepdims=True)
    t = jnp.exp(m2 - m1)
    p1 = 1.0 / (1.0 + t)
    p2 = t / (1.0 + t)
    eid_ref[...] = jnp.where(lane == 0, i1 - N_GROUPS, jnp.where(lane == 1, i2 - N_GROUPS, 0))
    gate_ref[...] = jnp.where(lane == 0, gtop * p1, jnp.where(lane == 1, gtop * p2, 0.0))


def _outproj_router(merged, x, w_o, ln_g, ln_b, w_router, b_router, alpha, tm=256):
    m, d = x.shape
    tm = min(tm, m)
    tile = pl.BlockSpec((tm, d), lambda i: (i, 0))
    vec = pl.BlockSpec((1, d), lambda i: (0, 0))
    small = pl.BlockSpec((tm, LANES), lambda i: (i, 0))
    return pl.pallas_call(
        functools.partial(_outproj_router_kernel, alpha),
        grid=(m // tm,),
        in_specs=[tile, tile, pl.BlockSpec((d, d), lambda i: (0, 0)), vec, vec,
                  pl.BlockSpec((2, d, LANES), lambda i: (0, 0, 0)), pl.BlockSpec((1, LANES), lambda i: (0, 0))],
        out_specs=[tile, pl.BlockSpec((tm * (d // LANES), LANES), lambda i: (i, 0)), small, small],
        out_shape=[jax.ShapeDtypeStruct((m, d), F32), jax.ShapeDtypeStruct((m * (d // LANES), LANES), F32),
                   jax.ShapeDtypeStruct((m, LANES), I32), jax.ShapeDtypeStruct((m, LANES), F32)],
        compiler_params=_cparams(("parallel",)),
        name="outproj_ln_router",
    )(merged, x, w_o, ln_g.reshape(1, d), ln_b.reshape(1, d), w_router, b_router)


def _onehots(eid_ref):
    tm = eid_ref.shape[1]
    e_iota = lax.broadcasted_iota(I32, (N_EXPERTS, tm), 0)
    oh0 = (eid_ref[0:1, :] == e_iota).astype(F32)
    oh1 = (eid_ref[1:2, :] == e_iota).astype(F32)
    return oh0, oh1


def _count_kernel(eid_ref, cnt_ref):
    @pl.when(pl.program_id(0) == 0)
    def _():
        cnt_ref[...] = jnp.zeros_like(cnt_ref)

    oh0, oh1 = _onehots(eid_ref)
    cnt_ref[...] += jnp.sum(oh0 + oh1, axis=1, keepdims=True)


def _slot_kernel(eid_ref, pstart_ref, dest_ref, run_ref):
    @pl.when(pl.program_id(0) == 0)
    def _():
        run_ref[...] = jnp.zeros_like(run_ref)

    tm = eid_ref.shape[1]
    oh0, oh1 = _onehots(eid_ref)
    both = oh0 + oh1
    earlier = (lax.broadcasted_iota(I32, (tm, tm), 0) < lax.broadcasted_iota(I32, (tm, tm), 1)).astype(BF16)
    pre = jnp.dot(both.astype(BF16), earlier, preferred_element_type=F32)
    base = pre + run_ref[...] + pstart_ref[...]
    dest_ref[0:1, :] = jnp.sum(oh0 * base, axis=0, keepdims=True).astype(I32)
    dest_ref[1:2, :] = jnp.sum(oh1 * base, axis=0, keepdims=True).astype(I32)
    run_ref[...] += jnp.sum(both, axis=1, keepdims=True)


def _expert_counts(eid_t):
    m = eid_t.shape[1]
    tm = min(ROUTE_TILE, m)
    return pl.pallas_call(
        _count_kernel,
        grid=(m // tm,),
        in_specs=[pl.BlockSpec((TOP_K, tm), lambda i: (0, i))],
        out_specs=pl.BlockSpec((N_EXPERTS, 1), lambda i: (0, 0)),
        out_shape=jax.ShapeDtypeStruct((N_EXPERTS, 1), F32),
        compiler_params=_cparams(("arbitrary",)),
        name="expert_counts",
    )(eid_t)


def _expert_slots(eid_t, pstart):
    m = eid_t.shape[1]
    tm = min(ROUTE_TILE, m)
    return pl.pallas_call(
        _slot_kernel,
        grid=(m // tm,),
        in_specs=[pl.BlockSpec((TOP_K, tm), lambda i: (0, i)),
                  pl.BlockSpec((N_EXPERTS, 1), lambda i: (0, 0))],
        out_specs=pl.BlockSpec((TOP_K, tm), lambda i: (0, i)),
        out_shape=jax.ShapeDtypeStruct((TOP_K, m), I32),
        scratch_shapes=[pltpu.VMEM((N_EXPERTS, 1), F32)],
        compiler_params=_cparams(("arbitrary",)),
        name="expert_slots",
    )(eid_t, pstart)


def _row_slab(ref, row, slab):
    return ref.at[pl.ds(pl.multiple_of(row * slab, slab), slab), :]


def _expert_kernel(item_e_ref, item_row_ref, item_nsub_ref, item_nvalid_ref, row_tok_ref, row_out_ref, *refs):
    wg_refs = refs[:MOE_GU_PIECES]
    wu_refs = refs[MOE_GU_PIECES:2 * MOE_GU_PIECES]
    wd_refs = refs[2 * MOE_GU_PIECES:2 * MOE_GU_PIECES + MOE_DOWN_PIECES]
    h_ref, ys_ref, xin_ref, yout_ref, x_ref, acc_ref, wgu_ref, wdb_ref, sem_in, sem_out = (
        refs[2 * MOE_GU_PIECES + MOE_DOWN_PIECES:])
    del item_e_ref
    it = pl.program_id(0)
    f = pl.program_id(1)
    n_items = pl.num_programs(0)
    nf = pl.num_programs(1)
    kp = wg_refs[0].shape[1]
    kc = kp * MOE_GU_PIECES
    ff = wg_refs[0].shape[2]
    fp = wd_refs[0].shape[1]
    fc = fp * MOE_DOWN_PIECES
    nsub = item_nsub_ref[it]
    row0 = item_row_ref[it]
    nxt = jnp.minimum(it + 1, n_items - 1)
    nsub_next = jnp.where(it + 1 < n_items, item_nsub_ref[nxt], 0)
    prv = jnp.maximum(it - 1, 0)
    nvalid_prev = jnp.where(it > 0, item_nvalid_ref[prv], 0)
    rb = MOE_ROWS
    slab = x_ref.shape[0] * kc // LANES
    blk = rb * slab

    def stage_rows(ref, s):
        return ref.at[pl.ds(pl.multiple_of(s * blk, blk), blk), :]

    def loop(n, fn):
        def body(s, carry):
            fn(s)
            return carry
        lax.fori_loop(0, n, body, 0)

    def block_rows(fn):
        def block(s):
            def body(r, carry):
                fn(s * rb + r)
                return carry
            lax.fori_loop(0, rb, body, 0, unroll=8)
        return block

    def gather_start(item_row, n_blocks):
        def row(q):
            pltpu.make_async_copy(_row_slab(h_ref, row_tok_ref[item_row + q], slab),
                                  _row_slab(xin_ref, q, slab), sem_in).start()
        loop(n_blocks, block_rows(row))

    def gather_wait(n_blocks):
        loop(n_blocks, lambda s: pltpu.make_async_copy(stage_rows(h_ref, s), stage_rows(xin_ref, s), sem_in).wait())

    def scatter_start(item_row, n_blocks):
        def row(q):
            slot = row_out_ref[item_row + q]

            @pl.when(slot >= 0)
            def _():
                pltpu.make_async_copy(_row_slab(yout_ref, q, slab), _row_slab(ys_ref, slot, slab), sem_out).start()
        loop(n_blocks, block_rows(row))

    def scatter_wait(n_valid):
        loop(n_valid, lambda q: pltpu.make_async_copy(_row_slab(yout_ref, 0, slab), _row_slab(ys_ref, 0, slab),
                                                      sem_out).wait())

    @pl.when(nsub > 0)
    def _():
        @pl.when(f == 0)
        def _():
            @pl.when(it == 0)
            def _():
                x_ref[...] = jnp.zeros_like(x_ref)
                gather_start(row0, nsub)
            gather_wait(nsub)

            def to_matrix(s):
                rows = pl.ds(pl.multiple_of(s * rb, rb), rb)
                base = pl.multiple_of(s * blk, blk)
                for j in range(slab):
                    c0 = (j * LANES) % kc
                    x_ref[(j * LANES) // kc, rows, c0:c0 + LANES] = (
                        xin_ref[pl.ds(base + j, rb, stride=slab), :].astype(BF16))
            loop(nsub, to_matrix)
            gather_start(item_row_ref[nxt], nsub_next)

        for q in range(MOE_GU_PIECES):
            wgu_ref[q * kp:(q + 1) * kp, :ff] = wg_refs[q][0].astype(BF16)
            wgu_ref[q * kp:(q + 1) * kp, ff:] = wu_refs[q][0].astype(BF16)
        for q in range(MOE_DOWN_PIECES):
            wdb_ref[pl.ds(pl.multiple_of(f * fc + q * fp, 2 * SUBLANES), fp), :] = wd_refs[q][0].astype(BF16)

        def gate_up(start, size):
            rows = pl.ds(pl.multiple_of(start, rb), size)
            part = jnp.dot(x_ref[f, rows, :], wgu_ref[...], preferred_element_type=F32)
            acc_ref[rows, :] = jnp.where(f > 0, acc_ref[rows, :], 0.0) + part

        gate_up(0, 2 * rb)

        def pair(p, carry):
            gate_up(p * (2 * rb), 2 * rb)
            return carry
        lax.fori_loop(1, nsub // 2, pair, 0)

        @pl.when((nsub % 2 == 1) & (nsub > 1))
        def _():
            gate_up((nsub - 1) * rb, rb)

        @pl.when(f == nf - 1)
        def _():
            scatter_wait(nvalid_prev)

            def down(start, size):
                rows = pl.ds(pl.multiple_of(start, rb), size)
                gate = acc_ref[rows, :ff]
                act = (gate * _sigmoid(gate) * acc_ref[rows, ff:]).astype(BF16)
                y = jnp.dot(act, wdb_ref[...], preferred_element_type=F32)
                base = pl.multiple_of(start * slab, blk)
                for j in range(slab):
                    yout_ref[pl.ds(base + j, size, stride=slab), :] = y[:, j * LANES:(j + 1) * LANES]

            down(0, 2 * rb)

            def down_pair(p, carry):
                down(p * (2 * rb), 2 * rb)
                return carry
            lax.fori_loop(1, nsub // 2, down_pair, 0)

            @pl.when((nsub % 2 == 1) & (nsub > 1))
            def _():
                down((nsub - 1) * rb, rb)

            scatter_start(row0, nsub)

            @pl.when(nsub_next == 0)
            def _():
                scatter_wait(item_nvalid_ref[it])


def _experts(item_e, item_row, item_nsub, item_nvalid, row_tok, row_out, w_gate, w_up, w_down, h_rows, n_assign):
    d = w_gate.shape[1]
    ff = w_gate.shape[2]
    nk = MOE_K_STEPS
    kc = d // nk
    fc = ff // nk
    assert d % nk == 0 and ff % nk == 0 and kc % LANES == 0 and fc % (2 * SUBLANES) == 0 and ff % LANES == 0
    n_items = item_e.shape[0]
    rows_max = MOE_ROWS * MOE_SUBS_PER_ITEM
    slab = d // LANES

    kp = kc // MOE_GU_PIECES
    fp = fc // MOE_DOWN_PIECES
    assert kc % MOE_GU_PIECES == 0 and fc % MOE_DOWN_PIECES == 0 and kp % (2 * SUBLANES) == 0 and fp % (2 * SUBLANES) == 0

    def piece(rows, cols, pieces, q):
        def index(it, k, ie, ir, ns, nv, rt, ro):
            return ie[it], jnp.where(ns[it] > 0, k, nk - 1) * pieces + q, 0
        return pl.BlockSpec((1, rows, cols), index)

    grid_spec = pltpu.PrefetchScalarGridSpec(
        num_scalar_prefetch=6,
        grid=(n_items, nk),
        in_specs=[
            *[piece(kp, ff, MOE_GU_PIECES, q) for q in range(MOE_GU_PIECES)],
            *[piece(kp, ff, MOE_GU_PIECES, q) for q in range(MOE_GU_PIECES)],
            *[piece(fp, d, MOE_DOWN_PIECES, q) for q in range(MOE_DOWN_PIECES)],
            pl.BlockSpec(memory_space=pl.ANY),
        ],
        out_specs=pl.BlockSpec(memory_space=pl.ANY),
        scratch_shapes=[
            pltpu.VMEM((rows_max * slab, LANES), F32),
            pltpu.VMEM((rows_max * slab, LANES), F32),
            pltpu.VMEM((nk, rows_max, kc), BF16),
            pltpu.VMEM((rows_max, 2 * ff), F32),
            pltpu.VMEM((kc, 2 * ff), BF16),
            pltpu.VMEM((ff, d), BF16),
            pltpu.SemaphoreType.DMA(()),
            pltpu.SemaphoreType.DMA(()),
        ],
    )
    return pl.pallas_call(
        _expert_kernel,
        grid_spec=grid_spec,
        out_shape=jax.ShapeDtypeStruct((n_assign * slab, LANES), F32),
        compiler_params=pltpu.CompilerParams(dimension_semantics=("arbitrary", "arbitrary"),
                                             vmem_limit_bytes=VMEM_LIMIT_BYTES, has_side_effects=True),
        name="moe_experts",
    )(item_e, item_row, item_nsub, item_nvalid, row_tok, row_out, *([w_gate] * MOE_GU_PIECES),
      *([w_up] * MOE_GU_PIECES), *([w_down] * MOE_DOWN_PIECES), h_rows)


COMBINE_TILE = 256


def _combine_kernel(alpha, ys_ref, h_ref, gate_ref, g_ref, b_ref, o_ref):
    tm = h_ref.shape[0]
    slab = ys_ref.shape[0] // (TOP_K * tm)
    gate = gate_ref[...]
    g0 = gate[:, 0:1]
    g1 = gate[:, 1:2]
    col = lambda k, j: ys_ref[pl.ds(k * slab + j, tm, stride=TOP_K * slab), :]
    ff = jnp.concatenate([g0 * col(0, j) + g1 * col(1, j) for j in range(slab)], axis=1)
    o_ref[...] = _layer_norm(alpha * h_ref[...] + ff, g_ref[...], b_ref[...])


def _combine(ys, h, gates, ln_g, ln_b, alpha):
    m, d = h.shape
    tm = min(COMBINE_TILE, m)
    slab = d // LANES
    assert m % tm == 0
    return pl.pallas_call(
        functools.partial(_combine_kernel, alpha),
        grid=(m // tm,),
        in_specs=[pl.BlockSpec((tm * TOP_K * slab, LANES), lambda i: (i, 0)),
                  pl.BlockSpec((tm, d), lambda i: (i, 0)),
                  pl.BlockSpec((tm, LANES), lambda i: (i, 0)),
                  pl.BlockSpec((1, d), lambda i: (0, 0)),
                  pl.BlockSpec((1, d), lambda i: (0, 0))],
        out_specs=pl.BlockSpec((tm, d), lambda i: (i, 0)),
        out_shape=jax.ShapeDtypeStruct((m, d), F32),
        compiler_params=_cparams(("parallel",)),
        name="moe_combine_ln",
    )(ys, h, gates, ln_g.reshape(1, d), ln_b.reshape(1, d))


def _moe_tables(counts):
    nsub_e = (counts + MOE_ROWS - 1) // MOE_ROWS
    pstart = (jnp.cumsum(nsub_e) - nsub_e) * MOE_ROWS
    nitem_e = (nsub_e + MOE_SUBS_PER_ITEM - 1) // MOE_SUBS_PER_ITEM
    item_end = jnp.cumsum(nitem_e)
    return nsub_e, pstart, nitem_e, item_end


def _moe(h, h_rows, eid, gates, w_gate, w_up, w_down, ln_g, ln_b, alpha):
    m, d = h.shape
    n_assign = m * TOP_K
    n_blocks = (n_assign + N_EXPERTS * (MOE_ROWS - 1) + MOE_ROWS - 1) // MOE_ROWS
    n_rows = n_blocks * MOE_ROWS
    n_items = N_EXPERTS + n_assign // (MOE_ROWS * MOE_SUBS_PER_ITEM)

    eid_t = eid[:, :TOP_K].T
    counts = _expert_counts(eid_t)[:, 0].astype(I32)
    nsub_e, pstart, nitem_e, item_end = _moe_tables(counts)
    dest_t = _expert_slots(eid_t, pstart.astype(F32).reshape(N_EXPERTS, 1))
    dest_flat = dest_t.T.reshape(-1)

    it = jnp.arange(n_items, dtype=I32)
    total_items = item_end[-1]
    item_e = jnp.minimum(jnp.sum(item_end[None, :] <= it[:, None], axis=1).astype(I32), N_EXPERTS - 1)
    j = it - (item_end - nitem_e)[item_e]
    used = it < total_items
    last_e = item_e[jnp.maximum(total_items - 1, 0)]
    item_nsub = jnp.where(used, jnp.clip(nsub_e[item_e] - j * MOE_SUBS_PER_ITEM, 0, MOE_SUBS_PER_ITEM), 0).astype(I32)
    item_row = jnp.where(used, pstart[item_e] + j * (MOE_ROWS * MOE_SUBS_PER_ITEM), 0).astype(I32)
    item_e = jnp.where(used, item_e, last_e).astype(I32)
    rows_per_item = MOE_ROWS * MOE_SUBS_PER_ITEM
    item_nvalid = jnp.where(used, jnp.clip(counts[item_e] - j * rows_per_item, 0, rows_per_item), 0).astype(I32)

    row_out = jnp.full((n_rows,), -1, I32).at[dest_flat].set(jnp.arange(n_assign, dtype=I32))
    row_tok = jnp.maximum(row_out, 0) // TOP_K

    ys = _experts(item_e, item_row, item_nsub, item_nvalid, row_tok, row_out, w_gate, w_up, w_down, h_rows, n_assign)
    return _combine(ys, h, gates, ln_g, ln_b, alpha)


def _pad_cols(w, n):
    return jnp.pad(w, ((0, 0), (0, n - w.shape[1])))


def _pad_rows(w, n):
    return jnp.pad(w, ((0, n - w.shape[0]), (0, 0)))


def kernel(x, w_in, attn_sinks, rw_mu_rkv, rw_mu_wag, rw_w0, rw_w1, rw_w2, rw_a0, rw_a1, rw_a2, rw_g1, rw_g2, rw_k_k, rw_k_a, rw_r_k, rw_lnx_w, rw_lnx_b, p_attn, p_rwkv, w_o, ln1_g, ln1_b, w_group, b_group, w_expert, b_expert, w_gate, w_up, w_down, ln2_g, ln2_b):
    b, t, d = x.shape
    depth = w_in.shape[0]
    m = b * t
    alpha = (2.0 * depth) ** 0.25
    cosb, sinb = _rope_tables(t)
    qkv_w = ATTN_Q_W + 2 * ATTN_KV_W
    rkv_w = 3 * RWKV_W
    h = x
    for l in range(depth):
        hf = h.reshape(m, d)
        hb = hf.astype(BF16)
        qkv = _matmul_cols(hb, w_in[l], 0, qkv_w, F32)
        rkv = _matmul_cols(hb, w_in[l], qkv_w, rkv_w, F32)
        gates = _matmul_cols(hb, w_in[l], qkv_w + rkv_w, 2 * d, BF16)

        y_a = _attention(qkv.reshape(b, t, qkv_w), attn_sinks[l], cosb, sinb)

        lora = lambda w, n: _pad_cols(w, n).astype(BF16)
        lorb = lambda w, n: _pad_rows(w, n).astype(BF16)
        n_w = -(-rw_w1.shape[2] // LANES) * LANES
        n_a = -(-rw_a1.shape[2] // LANES) * LANES
        n_g = -(-rw_g1.shape[2] // LANES) * LANES
        r_, k_, v_, lw_, cum_, ag_, g_ = _rwkv_prep(
            h, rkv.reshape(b, t, rkv_w), rw_mu_rkv[l], rw_mu_wag[l], rw_w0[l],
            lora(rw_w1[l], n_w), lorb(rw_w2[l], n_w), rw_a0[l], lora(rw_a1[l], n_a), lorb(rw_a2[l], n_a),
            lora(rw_g1[l], n_g), lorb(rw_g2[l], n_g))
        y_r = _wkv_scan(r_, k_, v_, lw_, cum_, ag_, g_, rw_k_k[l], rw_k_a[l], rw_r_k[l], rw_lnx_w[l], rw_lnx_b[l])

        merged = _merge(y_a.reshape(m, ATTN_Q_W), y_r.reshape(m, RWKV_W), gates,
                        p_attn[l].astype(BF16), p_rwkv[l].astype(BF16))
        w_router = _pad_cols(jnp.concatenate([w_group[l], w_expert[l]], axis=1), LANES)
        w_router_hi = w_router.astype(BF16)
        w_router = jnp.stack([w_router_hi, (w_router - w_router_hi.astype(F32)).astype(BF16)])
        b_router = _pad_cols(jnp.concatenate([b_group[l], b_expert[l]])[None, :], LANES)
        h1, h1_rows, eid, gate = _outproj_router(merged, hf, w_o[l].astype(BF16), ln1_g[l], ln1_b[l],
                                        w_router, b_router, alpha)
        h2 = _moe(h1, h1_rows, eid, gate, w_gate[l], w_up[l], w_down[l], ln2_g[l], ln2_b[l], alpha)
        h = h2.reshape(b, t, d)
    return h
```

```python
import functools

import jax
import jax.numpy as jnp
from jax import lax
from jax.experimental import pallas as pl
from jax.experimental.pallas import tpu as pltpu

F32 = jnp.float32
BF16 = jnp.bfloat16
I32 = jnp.int32

HEAD_DIM = 64
ATTN_Q_HEADS = 16
ATTN_KV_HEADS = 4
ATTN_GROUP = ATTN_Q_HEADS // ATTN_KV_HEADS
ATTN_Q_W = ATTN_Q_HEADS * HEAD_DIM
ATTN_KV_W = ATTN_KV_HEADS * HEAD_DIM
WINDOW = 128
ROPE_THETA = 10000.0
RWKV_HEADS = 16
RWKV_N = 64
RWKV_W = RWKV_HEADS * RWKV_N
RWKV_GN_EPS = 64e-5
N_GROUPS = 8
EXPERTS_PER_GROUP = 8
N_EXPERTS = N_GROUPS * EXPERTS_PER_GROUP
TOP_K = 2
LN_EPS = 1e-5

LANES = 128
SUBLANES = 8
VMEM_LIMIT_BYTES = 56 * 1024 * 1024

WKV_CHUNK = 64
WKV_CHUNKS_PER_STEP = 4
WKV_HEADS_PER_STEP = 8
MOE_ROWS = 128
MOE_SUBS_PER_ITEM = 4
MOE_K_STEPS = 4
MOE_GU_PIECES = 1
MOE_DOWN_PIECES = 1
ROUTE_TILE = 512


def _cparams(sem, vmem=VMEM_LIMIT_BYTES):
    return pltpu.CompilerParams(dimension_semantics=sem, vmem_limit_bytes=vmem)


def _sigmoid(x):
    return 1.0 / (1.0 + jnp.exp(-x))


def _dot(a, b):
    return jnp.dot(a.astype(BF16), b.astype(BF16), preferred_element_type=F32)


def _dot_nt(a, b):
    return lax.dot_general(a.astype(BF16), b.astype(BF16), (((1,), (1,)), ((), ())),
                           preferred_element_type=F32)


def _layer_norm(t, g, b):
    mu = jnp.mean(t, axis=-1, keepdims=True)
    d = t - mu
    var = jnp.mean(d * d, axis=-1, keepdims=True)
    return d * lax.rsqrt(var + LN_EPS) * g + b


def _matmul_kernel(a_ref, b_ref, o_ref):
    o_ref[...] = jnp.dot(a_ref[...], b_ref[...].astype(BF16), preferred_element_type=F32).astype(o_ref.dtype)


def _matmul_cols(a, b, col0, ncols, out_dtype, tm=2048, tn=512):
    m, k = a.shape
    tm = min(tm, m)
    cb = col0 // tn
    assert col0 % tn == 0 and ncols % tn == 0 and m % tm == 0
    return pl.pallas_call(
        _matmul_kernel,
        grid=(m // tm, ncols // tn),
        in_specs=[pl.BlockSpec((tm, k), lambda i, j: (i, 0)),
                  pl.BlockSpec((k, tn), lambda i, j: (0, j + cb))],
        out_specs=pl.BlockSpec((tm, tn), lambda i, j: (i, j)),
        out_shape=jax.ShapeDtypeStruct((m, ncols), out_dtype),
        compiler_params=_cparams(("parallel", "arbitrary")),
        name="inproj_matmul",
    )(a, b)


def _rope(x, cosb, sinb):
    half = HEAD_DIM // 2
    lane = lax.broadcasted_iota(I32, cosb.shape, 1)
    first_half = (lane % HEAD_DIM) < half
    outs = []
    for g in range(x.shape[1] // LANES):
        xg = x[:, g * LANES:(g + 1) * LANES]
        partner = jnp.where(first_half, pltpu.roll(xg, LANES - half, axis=1), pltpu.roll(xg, half, axis=1))
        outs.append(xg * cosb + partner * sinb)
    return outs


def _attn_kernel(sinks_ref, q_ref, kc_ref, kp_ref, vc_ref, vp_ref, cosc_ref, sinc_ref, cosp_ref, sinp_ref, o_ref):
    blk = pl.program_id(1)
    tq = q_ref.shape[1]
    qg = _rope(q_ref[0], cosc_ref[...], sinc_ref[...])
    kcg = _rope(kc_ref[0], cosc_ref[...], sinc_ref[...])
    kpg = _rope(kp_ref[0], cosp_ref[...], sinp_ref[...])
    vc = vc_ref[0]
    vp = vp_ref[0]

    def head(groups, h):
        g = groups[h // 2]
        return g[:, (h % 2) * HEAD_DIM:(h % 2 + 1) * HEAD_DIM]

    rows = ATTN_GROUP * tq
    qi = lax.broadcasted_iota(I32, (rows, 2 * tq), 0) % tq
    kj = lax.broadcasted_iota(I32, (rows, 2 * tq), 1)
    dist = qi + tq - kj
    valid = (dist >= 0) & (dist < WINDOW) & ((blk > 0) | (kj >= tq))
    rid = lax.broadcasted_iota(I32, (rows, 1), 0) // tq
    scale = HEAD_DIM ** -0.5
    for kvh in range(ATTN_KV_HEADS):
        qh = jnp.concatenate([head(qg, kvh * ATTN_GROUP + g) for g in range(ATTN_GROUP)], axis=0)
        kw = jnp.concatenate([head(kpg, kvh), head(kcg, kvh)], axis=0)
        vw = jnp.concatenate([vp[:, kvh * HEAD_DIM:(kvh + 1) * HEAD_DIM],
                              vc[:, kvh * HEAD_DIM:(kvh + 1) * HEAD_DIM]], axis=0)
        s = _dot_nt(qh, kw) * scale
        s = jnp.where(valid, s, -jnp.inf)
        sink = jnp.zeros((rows, 1), F32)
        for g in range(ATTN_GROUP):
            sink = jnp.where(rid == g, sinks_ref[kvh * ATTN_GROUP + g], sink)
        m = jnp.maximum(jnp.max(s, axis=-1, keepdims=True), sink)
        e = jnp.exp(s - m)
        denom = jnp.sum(e, axis=-1, keepdims=True) + jnp.exp(sink - m)
        o = _dot(e, vw) / denom
        for g in range(ATTN_GROUP):
            hq = kvh * ATTN_GROUP + g
            o_ref[0, :, hq * HEAD_DIM:(hq + 1) * HEAD_DIM] = o[g * tq:(g + 1) * tq].astype(o_ref.dtype)


def _attention(qkv, sinks, cosb, sinb):
    b, t, _ = qkv.shape
    tq = WINDOW
    nb = t // tq
    kcol = ATTN_Q_W // ATTN_KV_W
    prev = lambda i: jnp.maximum(i - 1, 0)
    grid_spec = pltpu.PrefetchScalarGridSpec(
        num_scalar_prefetch=0,
        grid=(b, nb),
        in_specs=[
            pl.BlockSpec(memory_space=pltpu.SMEM),
            pl.BlockSpec((1, tq, ATTN_Q_W), lambda bi, i: (bi, i, 0)),
            pl.BlockSpec((1, tq, ATTN_KV_W), lambda bi, i: (bi, i, kcol)),
            pl.BlockSpec((1, tq, ATTN_KV_W), lambda bi, i: (bi, prev(i), kcol)),
            pl.BlockSpec((1, tq, ATTN_KV_W), lambda bi, i: (bi, i, kcol + 1)),
            pl.BlockSpec((1, tq, ATTN_KV_W), lambda bi, i: (bi, prev(i), kcol + 1)),
            pl.BlockSpec((tq, LANES), lambda bi, i: (i, 0)),
            pl.BlockSpec((tq, LANES), lambda bi, i: (i, 0)),
            pl.BlockSpec((tq, LANES), lambda bi, i: (prev(i), 0)),
            pl.BlockSpec((tq, LANES), lambda bi, i: (prev(i), 0)),
        ],
        out_specs=pl.BlockSpec((1, tq, ATTN_Q_W), lambda bi, i: (bi, i, 0)),
    )
    return pl.pallas_call(
        _attn_kernel,
        grid_spec=grid_spec,
        out_shape=jax.ShapeDtypeStruct((b, t, ATTN_Q_W), BF16),
        compiler_params=_cparams(("parallel", "arbitrary")),
        name="swa_attention",
    )(sinks, qkv, qkv, qkv, qkv, qkv, cosb, sinb, cosb, sinb)


def _rope_tables(t):
    inv = 1.0 / (ROPE_THETA ** (jnp.arange(0, HEAD_DIM, 2, dtype=F32) / HEAD_DIM))
    ang = jnp.arange(t, dtype=F32)[:, None] * inv[None, :]
    cos, sin = jnp.cos(ang), jnp.sin(ang)
    reps = LANES // HEAD_DIM
    cosb = jnp.tile(jnp.concatenate([cos, cos], axis=-1), (1, reps))
    sinb = jnp.tile(jnp.concatenate([-sin, sin], axis=-1), (1, reps))
    return cosb, sinb


def _rwkv_prep_kernel(h_ref, hp_ref, r_ref, k_ref, v_ref, rp_ref, kp_ref, vp_ref,
                      mu_rkv_ref, mu_wag_ref, w0_ref, w1_ref, w2_ref, a0_ref, a1_ref, a2_ref,
                      g1_ref, g2_ref,
                      ro_ref, ko_ref, vo_ref, lwo_ref, cumo_ref, ago_ref, go_ref):
    first = pl.program_id(1) == 0

    def shifted(cur, prev_ref):
        last_row = prev_ref.shape[1] - 1
        prev_row = jnp.where(first, 0.0, prev_ref[0, last_row:last_row + 1, :].astype(F32))
        rowid = lax.broadcasted_iota(I32, cur.shape, 0)
        return jnp.where(rowid == 0, prev_row, pltpu.roll(cur, 1, axis=0))

    h = h_ref[0]
    xx = shifted(h, hp_ref) - h
    xw = h + xx * mu_wag_ref[0:1, :]
    xa = h + xx * mu_wag_ref[1:2, :]
    xg = h + xx * mu_wag_ref[2:3, :]
    w_raw = w0_ref[...] + _dot(jnp.tanh(_dot(xw, w1_ref[...])), w2_ref[...])
    neg = -w_raw
    softplus = jnp.maximum(neg, 0.0) + jnp.log1p(jnp.exp(-jnp.abs(neg)))
    w = -softplus - 0.5
    lw = -jnp.exp(w)
    tm = lw.shape[0]
    row = lax.broadcasted_iota(I32, (tm, tm), 0)
    col = lax.broadcasted_iota(I32, (tm, tm), 1)
    tri = ((row >= col) & (row // WKV_CHUNK == col // WKV_CHUNK)).astype(BF16)
    cum = sum(jnp.dot(tri, piece, preferred_element_type=F32) for piece in _split3(lw))
    ag = _sigmoid(a0_ref[...] + _dot(_dot(xa, a1_ref[...]), a2_ref[...]))
    go_ref[0] = _dot(_sigmoid(_dot(xg, g1_ref[...])), g2_ref[...])

    r = r_ref[0].astype(F32)
    k = k_ref[0].astype(F32)
    v = v_ref[0].astype(F32)
    r = r + (shifted(r, rp_ref) - r) * mu_rkv_ref[0:1, :]
    k = k + (shifted(k, kp_ref) - k) * mu_rkv_ref[1:2, :]
    v = v + (shifted(v, vp_ref) - v) * mu_rkv_ref[2:3, :]
    nchunk = tm // WKV_CHUNK
    for hd in range(RWKV_HEADS):
        sl = slice(hd * RWKV_N, (hd + 1) * RWKV_N)
        for ref, val in ((ro_ref, r), (ko_ref, k), (vo_ref, v), (lwo_ref, lw), (cumo_ref, cum), (ago_ref, ag)):
            ref[0, :, hd] = val[:, sl].reshape(nchunk, WKV_CHUNK, RWKV_N).astype(ref.dtype)


def _rwkv_prep(h, rkv, mu_rkv, mu_wag, w0, w1, w2, a0, a1, a2, g1, g2, tm=256):
    b, t, d = h.shape
    tm = min(tm, t)
    c = RWKV_W
    prevblk = lambda i, rows: jnp.maximum(i * (tm // rows) - 1, 0)
    rkv_rows = SUBLANES * (4 // rkv.dtype.itemsize)
    full = lambda arr: pl.BlockSpec(arr.shape, lambda bi, i: (0,) * arr.ndim)
    row = lambda arr: arr.reshape(1, -1)
    w0, a0 = row(w0), row(a0)
    in_specs = [
        pl.BlockSpec((1, tm, d), lambda bi, i: (bi, i, 0)),
        pl.BlockSpec((1, SUBLANES, d), lambda bi, i: (bi, prevblk(i, SUBLANES), 0)),
        pl.BlockSpec((1, tm, c), lambda bi, i: (bi, i, 0)),
        pl.BlockSpec((1, tm, c), lambda bi, i: (bi, i, 1)),
        pl.BlockSpec((1, tm, c), lambda bi, i: (bi, i, 2)),
        pl.BlockSpec((1, rkv_rows, c), lambda bi, i: (bi, prevblk(i, rkv_rows), 0)),
        pl.BlockSpec((1, rkv_rows, c), lambda bi, i: (bi, prevblk(i, rkv_rows), 1)),
        pl.BlockSpec((1, rkv_rows, c), lambda bi, i: (bi, prevblk(i, rkv_rows), 2)),
    ] + [full(a) for a in (mu_rkv, mu_wag, w0, w1, w2, a0, a1, a2, g1, g2)]
    assert tm % WKV_CHUNK == 0
    hm = lambda dt: jax.ShapeDtypeStruct((b, t // WKV_CHUNK, RWKV_HEADS, WKV_CHUNK, RWKV_N), dt)
    hm_spec = pl.BlockSpec((1, tm // WKV_CHUNK, RWKV_HEADS, WKV_CHUNK, RWKV_N), lambda bi, i: (bi, i, 0, 0, 0))
    return pl.pallas_call(
        _rwkv_prep_kernel,
        grid=(b, t // tm),
        in_specs=in_specs,
        out_specs=[hm_spec] * 6 + [pl.BlockSpec((1, tm, c), lambda bi, i: (bi, i, 0))],
        out_shape=[hm(BF16), hm(BF16), hm(BF16), hm(F32), hm(F32), hm(BF16), jax.ShapeDtypeStruct((b, t, c), F32)],
        compiler_params=_cparams(("parallel", "arbitrary")),
        name="rwkv_prep",
    )(h, h, rkv, rkv, rkv, rkv, rkv, rkv, mu_rkv, mu_wag, w0, w1, w2, a0, a1, a2, g1, g2)


def _split3(x):
    hi = x.astype(BF16)
    r1 = x - hi.astype(F32)
    mid = r1.astype(BF16)
    lo = (r1 - mid.astype(F32)).astype(BF16)
    return hi, mid, lo


def _bmm(a, b):
    return jnp.einsum("gmk,gkn->gmn", a.astype(BF16), b.astype(BF16), preferred_element_type=F32)


def _bmm_nt(a, b):
    return jnp.einsum("gmk,gnk->gmn", a.astype(BF16), b.astype(BF16), preferred_element_type=F32)


def _bmm_tn(a, b):
    return jnp.einsum("gtm,gtn->gmn", a.astype(BF16), b.astype(BF16), preferred_element_type=F32)


def _wkv_kernel(r_ref, k_ref, v_ref, lw_ref, cum_ref, ag_ref, g_ref, kk_ref, ka_ref, rk_ref, lnw_ref, lnb_ref,
                y_ref, s_ref, st_ref):
    c = WKV_CHUNK
    n = RWKV_N
    nc, hb = r_ref.shape[1], r_ref.shape[2]
    tc = nc * c
    g = nc * hb

    @pl.when(pl.program_id(2) == 0)
    def _():
        s_ref[...] = jnp.zeros_like(s_ref)

    chunks = lambda ref: ref[0].reshape(g, c, n).astype(F32)
    r, k, v, lw, cum, ag = (chunks(ref) for ref in (r_ref, k_ref, v_ref, lw_ref, cum_ref, ag_ref))
    per_head = lambda t: t.reshape(nc, hb, c, n)
    kk = (per_head(k) * kk_ref[...]).reshape(g, c, n)
    k = (per_head(k) * (1.0 + (per_head(ag) - 1.0) * ka_ref[...])).reshape(g, c, n)
    kk = kk / jnp.maximum(jnp.sqrt(jnp.sum(kk * kk, axis=-1, keepdims=True)), 1e-12)
    dinc = jnp.exp(cum)
    dinv = jnp.exp(-cum)
    rt = r * dinc
    kt = k * dinv
    at = -kk * jnp.exp(cum - lw)
    bt = kk * ag * dinv
    d_chunk = dinc[:, c - 1:c, :]

    row = lax.broadcasted_iota(I32, (1, c, c), 1)
    col = lax.broadcasted_iota(I32, (1, c, c), 2)
    strict = row > col
    incl = row >= col
    eye = jnp.broadcast_to((row == col).astype(F32), (g, c, c))
    p = _bmm_nt(jnp.concatenate([at, rt], axis=1), jnp.concatenate([bt, kt], axis=1))
    a_ab = jnp.where(strict, p[:, :c, :c], 0.0)
    a_ak = jnp.where(strict, p[:, :c, c:], 0.0)
    a_rb = jnp.where(incl, p[:, c:, :c], 0.0)
    a_rk = jnp.where(incl, p[:, c:, c:], 0.0)
    x = _bmm(a_ab, a_ab)
    tinv = eye + a_ab
    levels = c.bit_length() - 2
    for j in range(1, levels):
        both = _bmm(jnp.concatenate([x, tinv], axis=1), x)
        x = both[:, :c]
        tinv = tinv + both[:, c:]
    tinv = tinv + _bmm(tinv, x)
    z = _bmm(a_ak, v)
    ta = _bmm(tinv, jnp.concatenate([at, z], axis=2))
    ry = _bmm(a_rb, ta)
    rp = rt + ry[:, :, :n]
    yv = ry[:, :, n:] + _bmm(a_rk, v)
    moff = _bmm_tn(ta[:, :, :n], bt).reshape(nc, hb, n, n)
    n2 = _bmm_tn(jnp.concatenate([ta[:, :, n:], v], axis=1),
                 jnp.concatenate([bt, kt], axis=1)).reshape(nc, hb, n, n)
    dch = d_chunk.reshape(nc, hb, 1, n)

    s = s_ref[...]
    for ci in range(nc):
        st_ref[ci] = s
        s = (s + _bmm(s, moff[ci]) + n2[ci]) * dch[ci]
    s_ref[...] = s

    y = _bmm_nt(rp, st_ref[...].reshape(g, n, n)) + yv
    mu = jnp.mean(y, axis=-1, keepdims=True)
    yc = y - mu
    var = jnp.mean(yc * yc, axis=-1, keepdims=True)
    y = per_head(yc * lax.rsqrt(var + RWKV_GN_EPS)) * lnw_ref[...] + lnb_ref[...]
    bonus = jnp.sum(per_head(r * k) * rk_ref[...], axis=-1, keepdims=True)
    y = y + bonus * per_head(v)
    y = jnp.concatenate([jnp.concatenate([y[ci, hd] for hd in range(hb)], axis=1) for ci in range(nc)], axis=0)
    y_ref[0] = (y * g_ref[0]).astype(y_ref.dtype)


def _wkv_scan(r, k, v, lw, cum, ag, g, k_k, k_a, r_k, lnx_w, lnx_b):
    b, nchunks, hh, c, n = r.shape
    t = nchunks * c
    nc = min(WKV_CHUNKS_PER_STEP, nchunks)
    tc = nc * c
    hb = WKV_HEADS_PER_STEP
    blk = pl.BlockSpec((1, nc, hb, c, n), lambda bi, hi, i: (bi, i, hi, 0, 0))
    tok = pl.BlockSpec((1, tc, hb * n), lambda bi, hi, i: (bi, i, hi))
    par = pl.BlockSpec((hb, 1, n), lambda bi, hi, i: (hi, 0, 0))
    per_head = lambda arr: arr.reshape(hh, 1, n)
    return pl.pallas_call(
        _wkv_kernel,
        grid=(b, hh // hb, t // tc),
        in_specs=[blk] * 6 + [tok] + [par] * 5,
        out_specs=tok,
        out_shape=jax.ShapeDtypeStruct((b, t, hh * n), BF16),
        scratch_shapes=[pltpu.VMEM((hb, n, n), F32), pltpu.VMEM((nc, hb, n, n), F32)],
        compiler_params=_cparams(("parallel", "parallel", "arbitrary")),
        name="wkv7_scan",
    )(r, k, v, lw, cum, ag, g, per_head(k_k), per_head(k_a), per_head(r_k), per_head(lnx_w), per_head(lnx_b))


def _merge_kernel(ya_ref, yr_ref, ga_ref, gr_ref, pa_ref, pr_ref, o_ref):
    ma = jnp.dot(ya_ref[...], pa_ref[...], preferred_element_type=F32)
    mr = jnp.dot(yr_ref[...], pr_ref[...], preferred_element_type=F32)
    o_ref[...] = (_sigmoid(ga_ref[...].astype(F32)) * ma + _sigmoid(gr_ref[...].astype(F32)) * mr).astype(o_ref.dtype)


def _merge(ya, yr, gates, p_attn, p_rwkv, tm=256):
    m, c = ya.shape
    d = p_attn.shape[1]
    tm = min(tm, m)
    tile = pl.BlockSpec((tm, c), lambda i: (i, 0))
    return pl.pallas_call(
        _merge_kernel,
        grid=(m // tm,),
        in_specs=[tile, tile,
                  pl.BlockSpec((tm, d), lambda i: (i, 0)),
                  pl.BlockSpec((tm, d), lambda i: (i, 1)),
                  pl.BlockSpec((c, d), lambda i: (0, 0)),
                  pl.BlockSpec((c, d), lambda i: (0, 0))],
        out_specs=pl.BlockSpec((tm, d), lambda i: (i, 0)),
        out_shape=jax.ShapeDtypeStruct((m, d), BF16),
        compiler_params=_cparams(("parallel",)),
        name="gated_merge",
    )(ya, yr, gates, gates, p_attn, p_rwkv)


def _outproj_router_kernel(alpha, mg_ref, x_ref, wo_ref, g_ref, b_ref, wr_ref, br_ref,
                           h_ref, hrow_ref, eid_ref, gate_ref):
    mix = jnp.dot(mg_ref[...], wo_ref[...], preferred_element_type=F32)
    h = _layer_norm(alpha * x_ref[...] + mix, g_ref[...], b_ref[...])
    h_ref[...] = h
    slab = h.shape[1] // LANES
    for j in range(slab):
        hrow_ref[pl.ds(j, h.shape[0], stride=slab), :] = h[:, j * LANES:(j + 1) * LANES]
    h_hi = h.astype(BF16)
    h_lo = (h - h_hi.astype(F32)).astype(BF16)
    logits = (jnp.dot(h_hi, wr_ref[0], preferred_element_type=F32)
              + jnp.dot(h_lo, wr_ref[0], preferred_element_type=F32)
              + jnp.dot(h_hi, wr_ref[1], preferred_element_type=F32)) + br_ref[...]
    lane = lax.broadcasted_iota(I32, logits.shape, 1)
    ninf = -jnp.inf
    big = jnp.int32(2 * LANES)
    glog = jnp.where(lane < N_GROUPS, logits, ninf)
    gmax = jnp.max(glog, axis=-1, keepdims=True)
    gidx = jnp.min(jnp.where(glog == gmax, lane, big), axis=-1, keepdims=True)
    gtop = 1.0 / jnp.sum(jnp.exp(glog - gmax), axis=-1, keepdims=True)
    eg = (lane - N_GROUPS) // EXPERTS_PER_GROUP
    in_group = (lane >= N_GROUPS) & (lane < N_GROUPS + N_EXPERTS) & (eg == gidx)
    el = jnp.where(in_group, logits, ninf)
    m1 = jnp.max(el, axis=-1, keepdims=True)
    i1 = jnp.min(jnp.where(el == m1, lane, big), axis=-1, keepdims=True)
    el2 = jnp.where(lane == i1, ninf, el)
    m2 = jnp.max(el2, axis=-1, keepdims=True)
    i2 = jnp.min(jnp.where(el2 == m2, lane, big), axis=-1, keepdims=True)
    t = jnp.exp(m2 - m1)
    p1 = 1.0 / (1.0 + t)
    p2 = t / (1.0 + t)
    eid_ref[...] = jnp.where(lane == 0, i1 - N_GROUPS, jnp.where(lane == 1, i2 - N_GROUPS, 0))
    gate_ref[...] = jnp.where(lane == 0, gtop * p1, jnp.where(lane == 1, gtop * p2, 0.0))


def _outproj_router(merged, x, w_o, ln_g, ln_b, w_router, b_router, alpha, tm=512):
    m, d = x.shape
    tm = min(tm, m)
    tile = pl.BlockSpec((tm, d), lambda i: (i, 0))
    vec = pl.BlockSpec((1, d), lambda i: (0, 0))
    small = pl.BlockSpec((tm, LANES), lambda i: (i, 0))
    return pl.pallas_call(
        functools.partial(_outproj_router_kernel, alpha),
        grid=(m // tm,),
        in_specs=[tile, tile, pl.BlockSpec((d, d), lambda i: (0, 0)), vec, vec,
                  pl.BlockSpec((2, d, LANES), lambda i: (0, 0, 0)), pl.BlockSpec((1, LANES), lambda i: (0, 0))],
        out_specs=[tile, pl.BlockSpec((tm * (d // LANES), LANES), lambda i: (i, 0)), small, small],
        out_shape=[jax.ShapeDtypeStruct((m, d), F32), jax.ShapeDtypeStruct((m * (d // LANES), LANES), F32),
                   jax.ShapeDtypeStruct((m, LANES), I32), jax.ShapeDtypeStruct((m, LANES), F32)],
        compiler_params=_cparams(("parallel",)),
        name="outproj_ln_router",
    )(merged, x, w_o, ln_g.reshape(1, d), ln_b.reshape(1, d), w_router, b_router)


def _onehots(eid_ref):
    tm = eid_ref.shape[1]
    e_iota = lax.broadcasted_iota(I32, (N_EXPERTS, tm), 0)
    oh0 = (eid_ref[0:1, :] == e_iota).astype(F32)
    oh1 = (eid_ref[1:2, :] == e_iota).astype(F32)
    return oh0, oh1


def _count_kernel(eid_ref, cnt_ref):
    @pl.when(pl.program_id(0) == 0)
    def _():
        cnt_ref[...] = jnp.zeros_like(cnt_ref)

    oh0, oh1 = _onehots(eid_ref)
    cnt_ref[...] += jnp.sum(oh0 + oh1, axis=1, keepdims=True)


def _slot_kernel(eid_ref, pstart_ref, dest_ref, run_ref):
    @pl.when(pl.program_id(0) == 0)
    def _():
        run_ref[...] = jnp.zeros_like(run_ref)

    tm = eid_ref.shape[1]
    oh0, oh1 = _onehots(eid_ref)
    both = oh0 + oh1
    earlier = (lax.broadcasted_iota(I32, (tm, tm), 0) < lax.broadcasted_iota(I32, (tm, tm), 1)).astype(BF16)
    pre = jnp.dot(both.astype(BF16), earlier, preferred_element_type=F32)
    base = pre + run_ref[...] + pstart_ref[...]
    dest_ref[0:1, :] = jnp.sum(oh0 * base, axis=0, keepdims=True).astype(I32)
    dest_ref[1:2, :] = jnp.sum(oh1 * base, axis=0, keepdims=True).astype(I32)
    run_ref[...] += jnp.sum(both, axis=1, keepdims=True)


def _expert_counts(eid_t):
    m = eid_t.shape[1]
    tm = min(ROUTE_TILE, m)
    return pl.pallas_call(
        _count_kernel,
        grid=(m // tm,),
        in_specs=[pl.BlockSpec((TOP_K, tm), lambda i: (0, i))],
        out_specs=pl.BlockSpec((N_EXPERTS, 1), lambda i: (0, 0)),
        out_shape=jax.ShapeDtypeStruct((N_EXPERTS, 1), F32),
        compiler_params=_cparams(("arbitrary",)),
        name="expert_counts",
    )(eid_t)


def _expert_slots(eid_t, pstart):
    m = eid_t.shape[1]
    tm = min(ROUTE_TILE, m)
    return pl.pallas_call(
        _slot_kernel,
        grid=(m // tm,),
        in_specs=[pl.BlockSpec((TOP_K, tm), lambda i: (0, i)),
                  pl.BlockSpec((N_EXPERTS, 1), lambda i: (0, 0))],
        out_specs=pl.BlockSpec((TOP_K, tm), lambda i: (0, i)),
        out_shape=jax.ShapeDtypeStruct((TOP_K, m), I32),
        scratch_shapes=[pltpu.VMEM((N_EXPERTS, 1), F32)],
        compiler_params=_cparams(("arbitrary",)),
        name="expert_slots",
    )(eid_t, pstart)


def _row_slab(ref, row, slab):
    return ref.at[pl.ds(pl.multiple_of(row * slab, slab), slab), :]


def _dispatch_kernel(dest_ref, h_ref, xb_in_ref, xb_ref, sem):
    del xb_in_ref
    tm = DISPATCH_TILE
    slab = h_ref.shape[0] // tm
    t0 = pl.program_id(0) * tm

    def copy(n, k):
        return pltpu.make_async_copy(_row_slab(h_ref, n, slab),
                                     _row_slab(xb_ref, dest_ref[TOP_K * (t0 + n) + k], slab), sem)

    def start(n, carry):
        for k in range(TOP_K):
            copy(n, k).start()
        return carry

    lax.fori_loop(0, tm, start, 0, unroll=8)
    for k in range(TOP_K):
        pltpu.make_async_copy(h_ref, xb_ref.at[pl.ds(0, h_ref.shape[0]), :], sem).wait()


DISPATCH_TILE = 512


def _dispatch(dest_flat, h_rows, m, n_rows):
    slab = h_rows.shape[0] // m
    assert m % DISPATCH_TILE == 0
    grid_spec = pltpu.PrefetchScalarGridSpec(
        num_scalar_prefetch=1,
        grid=(m // DISPATCH_TILE,),
        in_specs=[pl.BlockSpec((DISPATCH_TILE * slab, LANES), lambda i, dr: (i, 0)),
                  pl.BlockSpec(memory_space=pl.ANY)],
        out_specs=pl.BlockSpec(memory_space=pl.ANY),
        scratch_shapes=[pltpu.SemaphoreType.DMA(())],
    )
    return pl.pallas_call(
        _dispatch_kernel,
        grid_spec=grid_spec,
        out_shape=jax.ShapeDtypeStruct((n_rows * slab, LANES), F32),
        input_output_aliases={2: 0},
        compiler_params=pltpu.CompilerParams(dimension_semantics=("arbitrary",), has_side_effects=True),
        name="moe_dispatch",
    )(dest_flat, h_rows, jnp.zeros((n_rows * slab, LANES), F32))


def _expert_kernel(item_e_ref, item_row_ref, item_nsub_ref, *refs):
    wg_refs = refs[:MOE_GU_PIECES]
    wu_refs = refs[MOE_GU_PIECES:2 * MOE_GU_PIECES]
    wd_refs = refs[2 * MOE_GU_PIECES:2 * MOE_GU_PIECES + MOE_DOWN_PIECES]
    xb_ref, yb_ref, xin_ref, yout_ref, x_ref, acc_ref, wgu_ref, wdb_ref, sem_in, sem_out = (
        refs[2 * MOE_GU_PIECES + MOE_DOWN_PIECES:])
    del item_e_ref
    it = pl.program_id(0)
    f = pl.program_id(1)
    n_items = pl.num_programs(0)
    nf = pl.num_programs(1)
    kp = wg_refs[0].shape[1]
    kc = kp * MOE_GU_PIECES
    ff = wg_refs[0].shape[2]
    fp = wd_refs[0].shape[1]
    fc = fp * MOE_DOWN_PIECES
    nsub = item_nsub_ref[it]
    row0 = item_row_ref[it]
    nxt = jnp.minimum(it + 1, n_items - 1)
    nsub_next = jnp.where(it + 1 < n_items, item_nsub_ref[nxt], 0)
    prv = jnp.maximum(it - 1, 0)
    nsub_prev = jnp.where(it > 0, item_nsub_ref[prv], 0)
    rb = MOE_ROWS
    slab = x_ref.shape[0] * kc // LANES
    blk = rb * slab

    def stage_rows(ref, s):
        return ref.at[pl.ds(pl.multiple_of(s * blk, blk), blk), :]

    def hbm_rows(ref, item_row, s):
        return ref.at[pl.ds(pl.multiple_of((item_row + s * rb) * slab, blk), blk), :]

    def in_copy(item_row, s):
        return pltpu.make_async_copy(hbm_rows(xb_ref, item_row, s), stage_rows(xin_ref, s), sem_in)

    def out_copy(item_row, s):
        return pltpu.make_async_copy(stage_rows(yout_ref, s), hbm_rows(yb_ref, item_row, s), sem_out)

    def loop(n, fn):
        def body(s, carry):
            fn(s)
            return carry
        lax.fori_loop(0, n, body, 0)

    @pl.when(nsub > 0)
    def _():
        @pl.when(f == 0)
        def _():
            @pl.when(it == 0)
            def _():
                x_ref[...] = jnp.zeros_like(x_ref)
                loop(nsub, lambda s: in_copy(row0, s).start())
            loop(nsub, lambda s: in_copy(row0, s).wait())

            def to_matrix(s):
                rows = pl.ds(pl.multiple_of(s * rb, rb), rb)
                base = pl.multiple_of(s * blk, blk)
                for j in range(slab):
                    c0 = (j * LANES) % kc
                    x_ref[(j * LANES) // kc, rows, c0:c0 + LANES] = (
                        xin_ref[pl.ds(base + j, rb, stride=slab), :].astype(BF16))
            loop(nsub, to_matrix)
            loop(nsub_next, lambda s: in_copy(item_row_ref[nxt], s).start())

        for q in range(MOE_GU_PIECES):
            wgu_ref[q * kp:(q + 1) * kp, :ff] = wg_refs[q][0].astype(BF16)
            wgu_ref[q * kp:(q + 1) * kp, ff:] = wu_refs[q][0].astype(BF16)
        for q in range(MOE_DOWN_PIECES):
            wdb_ref[pl.ds(pl.multiple_of(f * fc + q * fp, 2 * SUBLANES), fp), :] = wd_refs[q][0].astype(BF16)

        def gate_up(start, size):
            rows = pl.ds(pl.multiple_of(start, rb), size)
            part = jnp.dot(x_ref[f, rows, :], wgu_ref[...], preferred_element_type=F32)
            acc_ref[rows, :] = jnp.where(f > 0, acc_ref[rows, :], 0.0) + part

        gate_up(0, 2 * rb)

        def pair(p, carry):
            gate_up(p * (2 * rb), 2 * rb)
            return carry
        lax.fori_loop(1, nsub // 2, pair, 0)

        @pl.when((nsub % 2 == 1) & (nsub > 1))
        def _():
            gate_up((nsub - 1) * rb, rb)

        @pl.when(f == nf - 1)
        def _():
            loop(nsub_prev, lambda s: out_copy(item_row_ref[prv], s).wait())

            def down(start, size):
                rows = pl.ds(pl.multiple_of(start, rb), size)
                gate = acc_ref[rows, :ff]
                act = (gate * _sigmoid(gate) * acc_ref[rows, ff:]).astype(BF16)
                y = jnp.dot(act, wdb_ref[...], preferred_element_type=F32)
                base = pl.multiple_of(start * slab, blk)
                for j in range(slab):
                    yout_ref[pl.ds(base + j, size, stride=slab), :] = y[:, j * LANES:(j + 1) * LANES]

            down(0, 2 * rb)

            def down_pair(p, carry):
                down(p * (2 * rb), 2 * rb)
                return carry
            lax.fori_loop(1, nsub // 2, down_pair, 0)

            @pl.when((nsub % 2 == 1) & (nsub > 1))
            def _():
                down((nsub - 1) * rb, rb)

            loop(nsub, lambda s: out_copy(row0, s).start())

            @pl.when(nsub_next == 0)
            def _():
                loop(nsub, lambda s: out_copy(row0, s).wait())


def _experts(item_e, item_row, item_nsub, w_gate, w_up, w_down, xb):
    d = w_gate.shape[1]
    ff = w_gate.shape[2]
    nk = MOE_K_STEPS
    kc = d // nk
    fc = ff // nk
    assert d % nk == 0 and ff % nk == 0 and kc % LANES == 0 and fc % (2 * SUBLANES) == 0 and ff % LANES == 0
    n_items = item_e.shape[0]
    rows_max = MOE_ROWS * MOE_SUBS_PER_ITEM
    slab = d // LANES

    kp = kc // MOE_GU_PIECES
    fp = fc // MOE_DOWN_PIECES
    assert kc % MOE_GU_PIECES == 0 and fc % MOE_DOWN_PIECES == 0 and kp % (2 * SUBLANES) == 0 and fp % (2 * SUBLANES) == 0

    def piece(rows, cols, pieces, q):
        def index(it, k, ie, ir, ns):
            return ie[it], jnp.where(ns[it] > 0, k, nk - 1) * pieces + q, 0
        return pl.BlockSpec((1, rows, cols), index)

    grid_spec = pltpu.PrefetchScalarGridSpec(
        num_scalar_prefetch=3,
        grid=(n_items, nk),
        in_specs=[
            *[piece(kp, ff, MOE_GU_PIECES, q) for q in range(MOE_GU_PIECES)],
            *[piece(kp, ff, MOE_GU_PIECES, q) for q in range(MOE_GU_PIECES)],
            *[piece(fp, d, MOE_DOWN_PIECES, q) for q in range(MOE_DOWN_PIECES)],
            pl.BlockSpec(memory_space=pl.ANY),
        ],
        out_specs=pl.BlockSpec(memory_space=pl.ANY),
        scratch_shapes=[
            pltpu.VMEM((rows_max * slab, LANES), F32),
            pltpu.VMEM((rows_max * slab, LANES), F32),
            pltpu.VMEM((nk, rows_max, kc), BF16),
            pltpu.VMEM((rows_max, 2 * ff), F32),
            pltpu.VMEM((kc, 2 * ff), BF16),
            pltpu.VMEM((ff, d), BF16),
            pltpu.SemaphoreType.DMA(()),
            pltpu.SemaphoreType.DMA(()),
        ],
    )
    return pl.pallas_call(
        _expert_kernel,
        grid_spec=grid_spec,
        out_shape=jax.ShapeDtypeStruct(xb.shape, F32),
        input_output_aliases={3 + 2 * MOE_GU_PIECES + MOE_DOWN_PIECES: 0},
        compiler_params=pltpu.CompilerParams(dimension_semantics=("arbitrary", "arbitrary"),
                                             vmem_limit_bytes=VMEM_LIMIT_BYTES, has_side_effects=True),
        name="moe_experts",
    )(item_e, item_row, item_nsub, *([w_gate] * MOE_GU_PIECES), *([w_up] * MOE_GU_PIECES),
      *([w_down] * MOE_DOWN_PIECES), xb)


COMBINE_TILE = 256


def _combine_kernel(alpha, dest_ref, yb_ref, h_ref, gate_ref, g_ref, b_ref, o_ref, buf_ref, sem):
    tm = COMBINE_TILE
    i = pl.program_id(0)
    slab = buf_ref.shape[2] // tm

    def issue(tile, half):
        def start(n, carry):
            for k in range(TOP_K):
                pltpu.make_async_copy(_row_slab(yb_ref, dest_ref[TOP_K * (tile * tm + n) + k], slab),
                                      _row_slab(buf_ref.at[half, k], n, slab), sem.at[half]).start()
            return carry
        lax.fori_loop(0, tm, start, 0, unroll=8)

    @pl.when(i == 0)
    def _():
        issue(0, 0)

    @pl.when(i + 1 < pl.num_programs(0))
    def _():
        issue(i + 1, (i + 1) % 2)

    half = i % 2
    for k in range(TOP_K):
        pltpu.make_async_copy(yb_ref.at[pl.ds(0, buf_ref.shape[2]), :], buf_ref.at[half, k], sem.at[half]).wait()
    gate = gate_ref[...]
    g0 = gate[:, 0:1]
    g1 = gate[:, 1:2]
    col = lambda k, j: buf_ref[half, k, pl.ds(j, tm, stride=slab), :]
    ff = jnp.concatenate([g0 * col(0, j) + g1 * col(1, j) for j in range(slab)], axis=1)
    o_ref[...] = _layer_norm(alpha * h_ref[...] + ff, g_ref[...], b_ref[...])


def _combine(dest_flat, yb, h, gates, ln_g, ln_b, alpha):
    m, d = h.shape
    tm = COMBINE_TILE
    assert m % tm == 0
    grid_spec = pltpu.PrefetchScalarGridSpec(
        num_scalar_prefetch=1,
        grid=(m // tm,),
        in_specs=[pl.BlockSpec(memory_space=pl.ANY),
                  pl.BlockSpec((tm, d), lambda i, dr: (i, 0)),
                  pl.BlockSpec((tm, LANES), lambda i, dr: (i, 0)),
                  pl.BlockSpec((1, d), lambda i, dr: (0, 0)),
                  pl.BlockSpec((1, d), lambda i, dr: (0, 0))],
        out_specs=pl.BlockSpec((tm, d), lambda i, dr: (i, 0)),
        scratch_shapes=[pltpu.VMEM((2, TOP_K, tm * (d // LANES), LANES), F32), pltpu.SemaphoreType.DMA((2,))],
    )
    return pl.pallas_call(
        functools.partial(_combine_kernel, alpha),
        grid_spec=grid_spec,
        out_shape=jax.ShapeDtypeStruct((m, d), F32),
        compiler_params=_cparams(("arbitrary",)),
        name="moe_combine_ln",
    )(dest_flat, yb, h, gates, ln_g.reshape(1, d), ln_b.reshape(1, d))


def _moe_tables(counts):
    nsub_e = (counts + MOE_ROWS - 1) // MOE_ROWS
    pstart = (jnp.cumsum(nsub_e) - nsub_e) * MOE_ROWS
    nitem_e = (nsub_e + MOE_SUBS_PER_ITEM - 1) // MOE_SUBS_PER_ITEM
    item_end = jnp.cumsum(nitem_e)
    return nsub_e, pstart, nitem_e, item_end


def _moe(h, h_rows, eid, gates, w_gate, w_up, w_down, ln_g, ln_b, alpha):
    m, d = h.shape
    n_assign = m * TOP_K
    n_blocks = (n_assign + N_EXPERTS * (MOE_ROWS - 1) + MOE_ROWS - 1) // MOE_ROWS
    n_rows = n_blocks * MOE_ROWS
    n_items = N_EXPERTS + n_assign // (MOE_ROWS * MOE_SUBS_PER_ITEM)

    eid_t = eid[:, :TOP_K].T
    counts = _expert_counts(eid_t)[:, 0].astype(I32)
    nsub_e, pstart, nitem_e, item_end = _moe_tables(counts)
    dest_t = _expert_slots(eid_t, pstart.astype(F32).reshape(N_EXPERTS, 1))
    dest_flat = dest_t.T.reshape(-1)

    it = jnp.arange(n_items, dtype=I32)
    total_items = item_end[-1]
    item_e = jnp.minimum(jnp.sum(item_end[None, :] <= it[:, None], axis=1).astype(I32), N_EXPERTS - 1)
    j = it - (item_end - nitem_e)[item_e]
    used = it < total_items
    last_e = item_e[jnp.maximum(total_items - 1, 0)]
    item_nsub = jnp.where(used, jnp.clip(nsub_e[item_e] - j * MOE_SUBS_PER_ITEM, 0, MOE_SUBS_PER_ITEM), 0).astype(I32)
    item_row = jnp.where(used, pstart[item_e] + j * (MOE_ROWS * MOE_SUBS_PER_ITEM), 0).astype(I32)
    item_e = jnp.where(used, item_e, last_e).astype(I32)

    xb = _dispatch(dest_flat, h_rows, m, n_rows)
    yb = _experts(item_e, item_row, item_nsub, w_gate, w_up, w_down, xb)
    return _combine(dest_flat, yb, h, gates, ln_g, ln_b, alpha)


def _pad_cols(w, n):
    return jnp.pad(w, ((0, 0), (0, n - w.shape[1])))


def _pad_rows(w, n):
    return jnp.pad(w, ((0, n - w.shape[0]), (0, 0)))


def kernel(x, w_in, attn_sinks, rw_mu_rkv, rw_mu_wag, rw_w0, rw_w1, rw_w2, rw_a0, rw_a1, rw_a2, rw_g1, rw_g2, rw_k_k, rw_k_a, rw_r_k, rw_lnx_w, rw_lnx_b, p_attn, p_rwkv, w_o, ln1_g, ln1_b, w_group, b_group, w_expert, b_expert, w_gate, w_up, w_down, ln2_g, ln2_b):
    b, t, d = x.shape
    depth = w_in.shape[0]
    m = b * t
    alpha = (2.0 * depth) ** 0.25
    cosb, sinb = _rope_tables(t)
    qkv_w = ATTN_Q_W + 2 * ATTN_KV_W
    rkv_w = 3 * RWKV_W
    h = x
    for l in range(depth):
        hf = h.reshape(m, d)
        hb = hf.astype(BF16)
        qkv = _matmul_cols(hb, w_in[l], 0, qkv_w, F32)
        rkv = _matmul_cols(hb, w_in[l], qkv_w, rkv_w, BF16)
        gates = _matmul_cols(hb, w_in[l], qkv_w + rkv_w, 2 * d, BF16)

        y_a = _attention(qkv.reshape(b, t, qkv_w), attn_sinks[l], cosb, sinb)

        lora = lambda w, n: _pad_cols(w, n).astype(BF16)
        lorb = lambda w, n: _pad_rows(w, n).astype(BF16)
        n_w = -(-rw_w1.shape[2] // LANES) * LANES
        n_a = -(-rw_a1.shape[2] // LANES) * LANES
        n_g = -(-rw_g1.shape[2] // LANES) * LANES
        r_, k_, v_, lw_, cum_, ag_, g_ = _rwkv_prep(
            h, rkv.reshape(b, t, rkv_w), rw_mu_rkv[l], rw_mu_wag[l], rw_w0[l],
            lora(rw_w1[l], n_w), lorb(rw_w2[l], n_w), rw_a0[l], lora(rw_a1[l], n_a), lorb(rw_a2[l], n_a),
            lora(rw_g1[l], n_g), lorb(rw_g2[l], n_g))
        y_r = _wkv_scan(r_, k_, v_, lw_, cum_, ag_, g_, rw_k_k[l], rw_k_a[l], rw_r_k[l], rw_lnx_w[l], rw_lnx_b[l])

        merged = _merge(y_a.reshape(m, ATTN_Q_W), y_r.reshape(m, RWKV_W), gates,
                        p_attn[l].astype(BF16), p_rwkv[l].astype(BF16))
        w_router = _pad_cols(jnp.concatenate([w_group[l], w_expert[l]], axis=1), LANES)
        w_router_hi = w_router.astype(BF16)
        w_router = jnp.stack([w_router_hi, (w_router - w_router_hi.astype(F32)).astype(BF16)])
        b_router = _pad_cols(jnp.concatenate([b_group[l], b_expert[l]])[None, :], LANES)
        h1, h1_rows, eid, gate = _outproj_router(merged, hf, w_o[l].astype(BF16), ln1_g[l], ln1_b[l],
                                        w_router, b_router, alpha)
        h2 = _moe(h1, h1_rows, eid, gate, w_gate[l], w_up[l], w_down[l], ln2_g[l], ln2_b[l], alpha)
        h = h2.reshape(b, t, d)
    return h
```

```python
import functools

import jax
import jax.numpy as jnp
from jax import lax
from jax.experimental import pallas as pl
from jax.experimental.pallas import tpu as pltpu

F32 = jnp.float32
BF16 = jnp.bfloat16
I32 = jnp.int32

HEAD_DIM = 64
ATTN_Q_HEADS = 16
ATTN_KV_HEADS = 4
ATTN_GROUP = ATTN_Q_HEADS // ATTN_KV_HEADS
ATTN_Q_W = ATTN_Q_HEADS * HEAD_DIM
ATTN_KV_W = ATTN_KV_HEADS * HEAD_DIM
WINDOW = 128
ROPE_THETA = 10000.0
RWKV_HEADS = 16
RWKV_N = 64
RWKV_W = RWKV_HEADS * RWKV_N
RWKV_GN_EPS = 64e-5
N_GROUPS = 8
EXPERTS_PER_GROUP = 8
N_EXPERTS = N_GROUPS * EXPERTS_PER_GROUP
TOP_K = 2
LN_EPS = 1e-5

LANES = 128
SUBLANES = 8
VMEM_LIMIT_BYTES = 56 * 1024 * 1024

WKV_CHUNK = 64
WKV_CHUNKS_PER_STEP = 4
WKV_HEADS_PER_STEP = 8
MOE_ROWS = 128
MOE_SUBS_PER_ITEM = 4
MOE_K_STEPS = 4
MOE_GU_PIECES = 1
MOE_DOWN_PIECES = 1
ROUTE_TILE = 512


def _cparams(sem, vmem=VMEM_LIMIT_BYTES):
    return pltpu.CompilerParams(dimension_semantics=sem, vmem_limit_bytes=vmem)


def _sigmoid(x):
    return 1.0 / (1.0 + jnp.exp(-x))


def _dot(a, b):
    return jnp.dot(a.astype(BF16), b.astype(BF16), preferred_element_type=F32)


def _dot_nt(a, b):
    return lax.dot_general(a.astype(BF16), b.astype(BF16), (((1,), (1,)), ((), ())),
                           preferred_element_type=F32)


def _layer_norm(t, g, b):
    mu = jnp.mean(t, axis=-1, keepdims=True)
    d = t - mu
    var = jnp.mean(d * d, axis=-1, keepdims=True)
    return d * lax.rsqrt(var + LN_EPS) * g + b


def _matmul_kernel(a_ref, b_ref, o_ref):
    o_ref[...] = jnp.dot(a_ref[...], b_ref[...].astype(BF16), preferred_element_type=F32).astype(o_ref.dtype)


def _matmul_cols(a, b, col0, ncols, out_dtype, tm=2048, tn=512):
    m, k = a.shape
    tm = min(tm, m)
    cb = col0 // tn
    assert col0 % tn == 0 and ncols % tn == 0 and m % tm == 0
    return pl.pallas_call(
        _matmul_kernel,
        grid=(m // tm, ncols // tn),
        in_specs=[pl.BlockSpec((tm, k), lambda i, j: (i, 0)),
                  pl.BlockSpec((k, tn), lambda i, j: (0, j + cb))],
        out_specs=pl.BlockSpec((tm, tn), lambda i, j: (i, j)),
        out_shape=jax.ShapeDtypeStruct((m, ncols), out_dtype),
        compiler_params=_cparams(("parallel", "arbitrary")),
        name="inproj_matmul",
    )(a, b)


def _rope(x, cosb, sinb):
    half = HEAD_DIM // 2
    lane = lax.broadcasted_iota(I32, cosb.shape, 1)
    first_half = (lane % HEAD_DIM) < half
    outs = []
    for g in range(x.shape[1] // LANES):
        xg = x[:, g * LANES:(g + 1) * LANES]
        partner = jnp.where(first_half, pltpu.roll(xg, LANES - half, axis=1), pltpu.roll(xg, half, axis=1))
        outs.append(xg * cosb + partner * sinb)
    return outs


def _attn_kernel(sinks_ref, q_ref, kc_ref, kp_ref, vc_ref, vp_ref, cosc_ref, sinc_ref, cosp_ref, sinp_ref, o_ref):
    blk = pl.program_id(1)
    tq = q_ref.shape[1]
    qg = _rope(q_ref[0], cosc_ref[...], sinc_ref[...])
    kcg = _rope(kc_ref[0], cosc_ref[...], sinc_ref[...])
    kpg = _rope(kp_ref[0], cosp_ref[...], sinp_ref[...])
    vc = vc_ref[0]
    vp = vp_ref[0]

    def head(groups, h):
        g = groups[h // 2]
        return g[:, (h % 2) * HEAD_DIM:(h % 2 + 1) * HEAD_DIM]

    rows = ATTN_GROUP * tq
    qi = lax.broadcasted_iota(I32, (rows, 2 * tq), 0) % tq
    kj = lax.broadcasted_iota(I32, (rows, 2 * tq), 1)
    dist = qi + tq - kj
    valid = (dist >= 0) & (dist < WINDOW) & ((blk > 0) | (kj >= tq))
    rid = lax.broadcasted_iota(I32, (rows, 1), 0) // tq
    scale = HEAD_DIM ** -0.5
    for kvh in range(ATTN_KV_HEADS):
        qh = jnp.concatenate([head(qg, kvh * ATTN_GROUP + g) for g in range(ATTN_GROUP)], axis=0)
        kw = jnp.concatenate([head(kpg, kvh), head(kcg, kvh)], axis=0)
        vw = jnp.concatenate([vp[:, kvh * HEAD_DIM:(kvh + 1) * HEAD_DIM],
                              vc[:, kvh * HEAD_DIM:(kvh + 1) * HEAD_DIM]], axis=0)
        s = _dot_nt(qh, kw) * scale
        s = jnp.where(valid, s, -jnp.inf)
        sink = jnp.zeros((rows, 1), F32)
        for g in range(ATTN_GROUP):
            sink = jnp.where(rid == g, sinks_ref[kvh * ATTN_GROUP + g], sink)
        m = jnp.maximum(jnp.max(s, axis=-1, keepdims=True), sink)
        e = jnp.exp(s - m)
        denom = jnp.sum(e, axis=-1, keepdims=True) + jnp.exp(sink - m)
        o = _dot(e, vw) / denom
        for g in range(ATTN_GROUP):
            hq = kvh * ATTN_GROUP + g
            o_ref[0, :, hq * HEAD_DIM:(hq + 1) * HEAD_DIM] = o[g * tq:(g + 1) * tq].astype(o_ref.dtype)


def _attention(qkv, sinks, cosb, sinb):
    b, t, _ = qkv.shape
    tq = WINDOW
    nb = t // tq
    kcol = ATTN_Q_W // ATTN_KV_W
    prev = lambda i: jnp.maximum(i - 1, 0)
    grid_spec = pltpu.PrefetchScalarGridSpec(
        num_scalar_prefetch=0,
        grid=(b, nb),
        in_specs=[
            pl.BlockSpec(memory_space=pltpu.SMEM),
            pl.BlockSpec((1, tq, ATTN_Q_W), lambda bi, i: (bi, i, 0)),
            pl.BlockSpec((1, tq, ATTN_KV_W), lambda bi, i: (bi, i, kcol)),
            pl.BlockSpec((1, tq, ATTN_KV_W), lambda bi, i: (bi, prev(i), kcol)),
            pl.BlockSpec((1, tq, ATTN_KV_W), lambda bi, i: (bi, i, kcol + 1)),
            pl.BlockSpec((1, tq, ATTN_KV_W), lambda bi, i: (bi, prev(i), kcol + 1)),
            pl.BlockSpec((tq, LANES), lambda bi, i: (i, 0)),
            pl.BlockSpec((tq, LANES), lambda bi, i: (i, 0)),
            pl.BlockSpec((tq, LANES), lambda bi, i: (prev(i), 0)),
            pl.BlockSpec((tq, LANES), lambda bi, i: (prev(i), 0)),
        ],
        out_specs=pl.BlockSpec((1, tq, ATTN_Q_W), lambda bi, i: (bi, i, 0)),
    )
    return pl.pallas_call(
        _attn_kernel,
        grid_spec=grid_spec,
        out_shape=jax.ShapeDtypeStruct((b, t, ATTN_Q_W), BF16),
        compiler_params=_cparams(("parallel", "arbitrary")),
        name="swa_attention",
    )(sinks, qkv, qkv, qkv, qkv, qkv, cosb, sinb, cosb, sinb)


def _rope_tables(t):
    inv = 1.0 / (ROPE_THETA ** (jnp.arange(0, HEAD_DIM, 2, dtype=F32) / HEAD_DIM))
    ang = jnp.arange(t, dtype=F32)[:, None] * inv[None, :]
    cos, sin = jnp.cos(ang), jnp.sin(ang)
    reps = LANES // HEAD_DIM
    cosb = jnp.tile(jnp.concatenate([cos, cos], axis=-1), (1, reps))
    sinb = jnp.tile(jnp.concatenate([-sin, sin], axis=-1), (1, reps))
    return cosb, sinb


def _rwkv_prep_kernel(h_ref, hp_ref, r_ref, k_ref, v_ref, rp_ref, kp_ref, vp_ref,
                      mu_rkv_ref, mu_wag_ref, w0_ref, w1_ref, w2_ref, a0_ref, a1_ref, a2_ref,
                      g1_ref, g2_ref,
                      ro_ref, ko_ref, vo_ref, lwo_ref, cumo_ref, ago_ref, go_ref):
    first = pl.program_id(1) == 0

    def shifted(cur, prev_ref):
        last_row = prev_ref.shape[1] - 1
        prev_row = jnp.where(first, 0.0, prev_ref[0, last_row:last_row + 1, :].astype(F32))
        rowid = lax.broadcasted_iota(I32, cur.shape, 0)
        return jnp.where(rowid == 0, prev_row, pltpu.roll(cur, 1, axis=0))

    h = h_ref[0]
    xx = shifted(h, hp_ref) - h
    xw = h + xx * mu_wag_ref[0:1, :]
    xa = h + xx * mu_wag_ref[1:2, :]
    xg = h + xx * mu_wag_ref[2:3, :]
    w_raw = w0_ref[...] + _dot(jnp.tanh(_dot(xw, w1_ref[...])), w2_ref[...])
    neg = -w_raw
    softplus = jnp.maximum(neg, 0.0) + jnp.log1p(jnp.exp(-jnp.abs(neg)))
    w = -softplus - 0.5
    lw = -jnp.exp(w)
    tm = lw.shape[0]
    row = lax.broadcasted_iota(I32, (tm, tm), 0)
    col = lax.broadcasted_iota(I32, (tm, tm), 1)
    tri = ((row >= col) & (row // WKV_CHUNK == col // WKV_CHUNK)).astype(BF16)
    cum = sum(jnp.dot(tri, piece, preferred_element_type=F32) for piece in _split3(lw))
    ag = _sigmoid(a0_ref[...] + _dot(_dot(xa, a1_ref[...]), a2_ref[...]))
    go_ref[0] = _dot(_sigmoid(_dot(xg, g1_ref[...])), g2_ref[...])

    r = r_ref[0].astype(F32)
    k = k_ref[0].astype(F32)
    v = v_ref[0].astype(F32)
    r = r + (shifted(r, rp_ref) - r) * mu_rkv_ref[0:1, :]
    k = k + (shifted(k, kp_ref) - k) * mu_rkv_ref[1:2, :]
    v = v + (shifted(v, vp_ref) - v) * mu_rkv_ref[2:3, :]
    nchunk = tm // WKV_CHUNK
    for hd in range(RWKV_HEADS):
        sl = slice(hd * RWKV_N, (hd + 1) * RWKV_N)
        for ref, val in ((ro_ref, r), (ko_ref, k), (vo_ref, v), (lwo_ref, lw), (cumo_ref, cum), (ago_ref, ag)):
            ref[0, :, hd] = val[:, sl].reshape(nchunk, WKV_CHUNK, RWKV_N).astype(ref.dtype)


def _rwkv_prep(h, rkv, mu_rkv, mu_wag, w0, w1, w2, a0, a1, a2, g1, g2, tm=256):
    b, t, d = h.shape
    tm = min(tm, t)
    c = RWKV_W
    prevblk = lambda i, rows: jnp.maximum(i * (tm // rows) - 1, 0)
    rkv_rows = SUBLANES * (4 // rkv.dtype.itemsize)
    full = lambda arr: pl.BlockSpec(arr.shape, lambda bi, i: (0,) * arr.ndim)
    row = lambda arr: arr.reshape(1, -1)
    w0, a0 = row(w0), row(a0)
    in_specs = [
        pl.BlockSpec((1, tm, d), lambda bi, i: (bi, i, 0)),
        pl.BlockSpec((1, SUBLANES, d), lambda bi, i: (bi, prevblk(i, SUBLANES), 0)),
        pl.BlockSpec((1, tm, c), lambda bi, i: (bi, i, 0)),
        pl.BlockSpec((1, tm, c), lambda bi, i: (bi, i, 1)),
        pl.BlockSpec((1, tm, c), lambda bi, i: (bi, i, 2)),
        pl.BlockSpec((1, rkv_rows, c), lambda bi, i: (bi, prevblk(i, rkv_rows), 0)),
        pl.BlockSpec((1, rkv_rows, c), lambda bi, i: (bi, prevblk(i, rkv_rows), 1)),
        pl.BlockSpec((1, rkv_rows, c), lambda bi, i: (bi, prevblk(i, rkv_rows), 2)),
    ] + [full(a) for a in (mu_rkv, mu_wag, w0, w1, w2, a0, a1, a2, g1, g2)]
    assert tm % WKV_CHUNK == 0
    hm = lambda dt: jax.ShapeDtypeStruct((b, t // WKV_CHUNK, RWKV_HEADS, WKV_CHUNK, RWKV_N), dt)
    hm_spec = pl.BlockSpec((1, tm // WKV_CHUNK, RWKV_HEADS, WKV_CHUNK, RWKV_N), lambda bi, i: (bi, i, 0, 0, 0))
    return pl.pallas_call(
        _rwkv_prep_kernel,
        grid=(b, t // tm),
        in_specs=in_specs,
        out_specs=[hm_spec] * 6 + [pl.BlockSpec((1, tm, c), lambda bi, i: (bi, i, 0))],
        out_shape=[hm(BF16), hm(BF16), hm(BF16), hm(F32), hm(F32), hm(BF16), jax.ShapeDtypeStruct((b, t, c), F32)],
        compiler_params=_cparams(("parallel", "arbitrary")),
        name="rwkv_prep",
    )(h, h, rkv, rkv, rkv, rkv, rkv, rkv, mu_rkv, mu_wag, w0, w1, w2, a0, a1, a2, g1, g2)


def _split3(x):
    hi = x.astype(BF16)
    r1 = x - hi.astype(F32)
    mid = r1.astype(BF16)
    lo = (r1 - mid.astype(F32)).astype(BF16)
    return hi, mid, lo


def _bmm(a, b):
    return jnp.einsum("gmk,gkn->gmn", a.astype(BF16), b.astype(BF16), preferred_element_type=F32)


def _bmm_nt(a, b):
    return jnp.einsum("gmk,gnk->gmn", a.astype(BF16), b.astype(BF16), preferred_element_type=F32)


def _bmm_tn(a, b):
    return jnp.einsum("gtm,gtn->gmn", a.astype(BF16), b.astype(BF16), preferred_element_type=F32)


def _wkv_kernel(r_ref, k_ref, v_ref, lw_ref, cum_ref, ag_ref, g_ref, kk_ref, ka_ref, rk_ref, lnw_ref, lnb_ref,
                y_ref, s_ref, st_ref):
    c = WKV_CHUNK
    n = RWKV_N
    nc, hb = r_ref.shape[1], r_ref.shape[2]
    tc = nc * c
    g = nc * hb

    @pl.when(pl.program_id(2) == 0)
    def _():
        s_ref[...] = jnp.zeros_like(s_ref)

    chunks = lambda ref: ref[0].reshape(g, c, n).astype(F32)
    r, k, v, lw, cum, ag = (chunks(ref) for ref in (r_ref, k_ref, v_ref, lw_ref, cum_ref, ag_ref))
    per_head = lambda t: t.reshape(nc, hb, c, n)
    kk = (per_head(k) * kk_ref[...]).reshape(g, c, n)
    k = (per_head(k) * (1.0 + (per_head(ag) - 1.0) * ka_ref[...])).reshape(g, c, n)
    kk = kk / jnp.maximum(jnp.sqrt(jnp.sum(kk * kk, axis=-1, keepdims=True)), 1e-12)
    dinc = jnp.exp(cum)
    dinv = jnp.exp(-cum)
    rt = r * dinc
    kt = k * dinv
    at = -kk * jnp.exp(cum - lw)
    bt = kk * ag * dinv
    d_chunk = dinc[:, c - 1:c, :]

    row = lax.broadcasted_iota(I32, (1, c, c), 1)
    col = lax.broadcasted_iota(I32, (1, c, c), 2)
    strict = row > col
    incl = row >= col
    eye = jnp.broadcast_to((row == col).astype(F32), (g, c, c))
    p = _bmm_nt(jnp.concatenate([at, rt], axis=1), jnp.concatenate([bt, kt], axis=1))
    a_ab = jnp.where(strict, p[:, :c, :c], 0.0)
    a_ak = jnp.where(strict, p[:, :c, c:], 0.0)
    a_rb = jnp.where(incl, p[:, c:, :c], 0.0)
    a_rk = jnp.where(incl, p[:, c:, c:], 0.0)
    x = _bmm(a_ab, a_ab)
    tinv = eye + a_ab
    levels = c.bit_length() - 2
    for j in range(1, levels):
        both = _bmm(jnp.concatenate([x, tinv], axis=1), x)
        x = both[:, :c]
        tinv = tinv + both[:, c:]
    tinv = tinv + _bmm(tinv, x)
    z = _bmm(a_ak, v)
    ta = _bmm(tinv, jnp.concatenate([at, z], axis=2))
    ry = _bmm(a_rb, ta)
    rp = rt + ry[:, :, :n]
    yv = ry[:, :, n:] + _bmm(a_rk, v)
    moff = _bmm_tn(ta[:, :, :n], bt).reshape(nc, hb, n, n)
    n2 = _bmm_tn(jnp.concatenate([ta[:, :, n:], v], axis=1),
                 jnp.concatenate([bt, kt], axis=1)).reshape(nc, hb, n, n)
    dch = d_chunk.reshape(nc, hb, 1, n)

    s = s_ref[...]
    for ci in range(nc):
        st_ref[ci] = s
        s = (s + _bmm(s, moff[ci]) + n2[ci]) * dch[ci]
    s_ref[...] = s

    y = _bmm_nt(rp, st_ref[...].reshape(g, n, n)) + yv
    mu = jnp.mean(y, axis=-1, keepdims=True)
    yc = y - mu
    var = jnp.mean(yc * yc, axis=-1, keepdims=True)
    y = per_head(yc * lax.rsqrt(var + RWKV_GN_EPS)) * lnw_ref[...] + lnb_ref[...]
    bonus = jnp.sum(per_head(r * k) * rk_ref[...], axis=-1, keepdims=True)
    y = y + bonus * per_head(v)
    y = jnp.concatenate([jnp.concatenate([y[ci, hd] for hd in range(hb)], axis=1) for ci in range(nc)], axis=0)
    y_ref[0] = (y * g_ref[0]).astype(y_ref.dtype)


def _wkv_scan(r, k, v, lw, cum, ag, g, k_k, k_a, r_k, lnx_w, lnx_b):
    b, nchunks, hh, c, n = r.shape
    t = nchunks * c
    nc = min(WKV_CHUNKS_PER_STEP, nchunks)
    tc = nc * c
    hb = WKV_HEADS_PER_STEP
    blk = pl.BlockSpec((1, nc, hb, c, n), lambda bi, hi, i: (bi, i, hi, 0, 0))
    tok = pl.BlockSpec((1, tc, hb * n), lambda bi, hi, i: (bi, i, hi))
    par = pl.BlockSpec((hb, 1, n), lambda bi, hi, i: (hi, 0, 0))
    per_head = lambda arr: arr.reshape(hh, 1, n)
    return pl.pallas_call(
        _wkv_kernel,
        grid=(b, hh // hb, t // tc),
        in_specs=[blk] * 6 + [tok] + [par] * 5,
        out_specs=tok,
        out_shape=jax.ShapeDtypeStruct((b, t, hh * n), BF16),
        scratch_shapes=[pltpu.VMEM((hb, n, n), F32), pltpu.VMEM((nc, hb, n, n), F32)],
        compiler_params=_cparams(("parallel", "parallel", "arbitrary")),
        name="wkv7_scan",
    )(r, k, v, lw, cum, ag, g, per_head(k_k), per_head(k_a), per_head(r_k), per_head(lnx_w), per_head(lnx_b))


def _merge_kernel(ya_ref, yr_ref, ga_ref, gr_ref, pa_ref, pr_ref, o_ref):
    ma = jnp.dot(ya_ref[...], pa_ref[...], preferred_element_type=F32)
    mr = jnp.dot(yr_ref[...], pr_ref[...], preferred_element_type=F32)
    o_ref[...] = (_sigmoid(ga_ref[...].astype(F32)) * ma + _sigmoid(gr_ref[...].astype(F32)) * mr).astype(o_ref.dtype)


def _merge(ya, yr, gates, p_attn, p_rwkv, tm=256):
    m, c = ya.shape
    d = p_attn.shape[1]
    tm = min(tm, m)
    tile = pl.BlockSpec((tm, c), lambda i: (i, 0))
    return pl.pallas_call(
        _merge_kernel,
        grid=(m // tm,),
        in_specs=[tile, tile,
                  pl.BlockSpec((tm, d), lambda i: (i, 0)),
                  pl.BlockSpec((tm, d), lambda i: (i, 1)),
                  pl.BlockSpec((c, d), lambda i: (0, 0)),
                  pl.BlockSpec((c, d), lambda i: (0, 0))],
        out_specs=pl.BlockSpec((tm, d), lambda i: (i, 0)),
        out_shape=jax.ShapeDtypeStruct((m, d), BF16),
        compiler_params=_cparams(("parallel",)),
        name="gated_merge",
    )(ya, yr, gates, gates, p_attn, p_rwkv)


def _outproj_router_kernel(alpha, mg_ref, x_ref, wo_ref, g_ref, b_ref, wr_ref, br_ref,
                           h_ref, hrow_ref, eid_ref, gate_ref):
    mix = jnp.dot(mg_ref[...], wo_ref[...], preferred_element_type=F32)
    h = _layer_norm(alpha * x_ref[...] + mix, g_ref[...], b_ref[...])
    h_ref[...] = h
    slab = h.shape[1] // LANES
    for j in range(slab):
        hrow_ref[pl.ds(j, h.shape[0], stride=slab), :] = h[:, j * LANES:(j + 1) * LANES]
    h_hi = h.astype(BF16)
    h_lo = (h - h_hi.astype(F32)).astype(BF16)
    logits = (jnp.dot(h_hi, wr_ref[0], preferred_element_type=F32)
              + jnp.dot(h_lo, wr_ref[0], preferred_element_type=F32)
              + jnp.dot(h_hi, wr_ref[1], preferred_element_type=F32)) + br_ref[...]
    lane = lax.broadcasted_iota(I32, logits.shape, 1)
    ninf = -jnp.inf
    big = jnp.int32(2 * LANES)
    glog = jnp.where(lane < N_GROUPS, logits, ninf)
    gmax = jnp.max(glog, axis=-1, keepdims=True)
    gidx = jnp.min(jnp.where(glog == gmax, lane, big), axis=-1, keepdims=True)
    gtop = 1.0 / jnp.sum(jnp.exp(glog - gmax), axis=-1, keepdims=True)
    eg = (lane - N_GROUPS) // EXPERTS_PER_GROUP
    in_group = (lane >= N_GROUPS) & (lane < N_GROUPS + N_EXPERTS) & (eg == gidx)
    el = jnp.where(in_group, logits, ninf)
    m1 = jnp.max(el, axis=-1, keepdims=True)
    i1 = jnp.min(jnp.where(el == m1, lane, big), axis=-1, keepdims=True)
    el2 = jnp.where(lane == i1, ninf, el)
    m2 = jnp.max(el2, axis=-1, keepdims=True)
    i2 = jnp.min(jnp.where(el2 == m2, lane, big), axis=-1, keepdims=True)
    t = jnp.exp(m2 - m1)
    p1 = 1.0 / (1.0 + t)
    p2 = t / (1.0 + t)
    eid_ref[...] = jnp.where(lane == 0, i1 - N_GROUPS, jnp.where(lane == 1, i2 - N_GROUPS, 0))
    gate_ref[...] = jnp.where(lane == 0, gtop * p1, jnp.where(lane == 1, gtop * p2, 0.0))


def _outproj_router(merged, x, w_o, ln_g, ln_b, w_router, b_router, alpha, tm=512):
    m, d = x.shape
    tm = min(tm, m)
    tile = pl.BlockSpec((tm, d), lambda i: (i, 0))
    vec = pl.BlockSpec((1, d), lambda i: (0, 0))
    small = pl.BlockSpec((tm, LANES), lambda i: (i, 0))
    return pl.pallas_call(
        functools.partial(_outproj_router_kernel, alpha),
        grid=(m // tm,),
        in_specs=[tile, tile, pl.BlockSpec((d, d), lambda i: (0, 0)), vec, vec,
                  pl.BlockSpec((2, d, LANES), lambda i: (0, 0, 0)), pl.BlockSpec((1, LANES), lambda i: (0, 0))],
        out_specs=[tile, pl.BlockSpec((tm * (d // LANES), LANES), lambda i: (i, 0)), small, small],
        out_shape=[jax.ShapeDtypeStruct((m, d), F32), jax.ShapeDtypeStruct((m * (d // LANES), LANES), F32),
                   jax.ShapeDtypeStruct((m, LANES), I32), jax.ShapeDtypeStruct((m, LANES), F32)],
        compiler_params=_cparams(("parallel",)),
        name="outproj_ln_router",
    )(merged, x, w_o, ln_g.reshape(1, d), ln_b.reshape(1, d), w_router, b_router)


def _onehots(eid_ref):
    tm = eid_ref.shape[1]
    e_iota = lax.broadcasted_iota(I32, (N_EXPERTS, tm), 0)
    oh0 = (eid_ref[0:1, :] == e_iota).astype(F32)
    oh1 = (eid_ref[1:2, :] == e_iota).astype(F32)
    return oh0, oh1


def _count_kernel(eid_ref, cnt_ref):
    @pl.when(pl.program_id(0) == 0)
    def _():
        cnt_ref[...] = jnp.zeros_like(cnt_ref)

    oh0, oh1 = _onehots(eid_ref)
    cnt_ref[...] += jnp.sum(oh0 + oh1, axis=1, keepdims=True)


def _slot_kernel(eid_ref, pstart_ref, dest_ref, run_ref):
    @pl.when(pl.program_id(0) == 0)
    def _():
        run_ref[...] = jnp.zeros_like(run_ref)

    tm = eid_ref.shape[1]
    oh0, oh1 = _onehots(eid_ref)
    both = oh0 + oh1
    earlier = (lax.broadcasted_iota(I32, (tm, tm), 0) < lax.broadcasted_iota(I32, (tm, tm), 1)).astype(BF16)
    pre = jnp.dot(both.astype(BF16), earlier, preferred_element_type=F32)
    base = pre + run_ref[...] + pstart_ref[...]
    dest_ref[0:1, :] = jnp.sum(oh0 * base, axis=0, keepdims=True).astype(I32)
    dest_ref[1:2, :] = jnp.sum(oh1 * base, axis=0, keepdims=True).astype(I32)
    run_ref[...] += jnp.sum(both, axis=1, keepdims=True)


def _expert_counts(eid_t):
    m = eid_t.shape[1]
    tm = min(ROUTE_TILE, m)
    return pl.pallas_call(
        _count_kernel,
        grid=(m // tm,),
        in_specs=[pl.BlockSpec((TOP_K, tm), lambda i: (0, i))],
        out_specs=pl.BlockSpec((N_EXPERTS, 1), lambda i: (0, 0)),
        out_shape=jax.ShapeDtypeStruct((N_EXPERTS, 1), F32),
        compiler_params=_cparams(("arbitrary",)),
        name="expert_counts",
    )(eid_t)


def _expert_slots(eid_t, pstart):
    m = eid_t.shape[1]
    tm = min(ROUTE_TILE, m)
    return pl.pallas_call(
        _slot_kernel,
        grid=(m // tm,),
        in_specs=[pl.BlockSpec((TOP_K, tm), lambda i: (0, i)),
                  pl.BlockSpec((N_EXPERTS, 1), lambda i: (0, 0))],
        out_specs=pl.BlockSpec((TOP_K, tm), lambda i: (0, i)),
        out_shape=jax.ShapeDtypeStruct((TOP_K, m), I32),
        scratch_shapes=[pltpu.VMEM((N_EXPERTS, 1), F32)],
        compiler_params=_cparams(("arbitrary",)),
        name="expert_slots",
    )(eid_t, pstart)


def _row_slab(ref, row, slab):
    return ref.at[pl.ds(pl.multiple_of(row * slab, slab), slab), :]


def _dispatch_kernel(dest_ref, pad_row_ref, pad_len_ref, tail_ref, h_ref, xb_ref, zero_ref, sem, zero_sem):
    tm = DISPATCH_TILE
    slab = h_ref.shape[0] // tm
    rb = MOE_ROWS
    step = pl.program_id(0)
    t0 = step * tm

    def zero_rows(row, n_rows):
        return pltpu.make_async_copy(zero_ref.at[pl.ds(0, n_rows * slab), :],
                                     xb_ref.at[pl.ds(pl.multiple_of(row * slab, slab), n_rows * slab), :], zero_sem)

    def zero_fill(op):
        def pad_run(e, carry):
            row = pad_row_ref[e]
            length = pad_len_ref[e]
            piece = rb // 2
            while piece >= 1:
                take = (length & piece) != 0

                @pl.when(take)
                def _(row=row, piece=piece):
                    op(zero_rows(row, piece))
                row = row + jnp.where(take, piece, 0)
                piece //= 2
            return carry
        lax.fori_loop(0, pad_len_ref.shape[0], pad_run, 0)

        def tail_block(tb, carry):
            op(zero_rows(tail_ref[0] + tb * rb, rb))
            return carry
        lax.fori_loop(0, tail_ref[1], tail_block, 0)

    @pl.when(step == 0)
    def _():
        zero_ref[...] = jnp.zeros_like(zero_ref)
        zero_fill(lambda c: c.start())

    def copy(n, k):
        return pltpu.make_async_copy(_row_slab(h_ref, n, slab),
                                     _row_slab(xb_ref, dest_ref[TOP_K * (t0 + n) + k], slab), sem)

    def start(n, carry):
        for k in range(TOP_K):
            copy(n, k).start()
        return carry

    lax.fori_loop(0, tm, start, 0, unroll=8)
    for k in range(TOP_K):
        pltpu.make_async_copy(h_ref, xb_ref.at[pl.ds(0, h_ref.shape[0]), :], sem).wait()

    @pl.when(step == pl.num_programs(0) - 1)
    def _():
        zero_fill(lambda c: c.wait())


DISPATCH_TILE = 512


def _dispatch(dest_flat, pad_row, pad_len, tail, h_rows, m, n_rows):
    slab = h_rows.shape[0] // m
    assert m % DISPATCH_TILE == 0
    grid_spec = pltpu.PrefetchScalarGridSpec(
        num_scalar_prefetch=4,
        grid=(m // DISPATCH_TILE,),
        in_specs=[pl.BlockSpec((DISPATCH_TILE * slab, LANES), lambda i, *_: (i, 0))],
        out_specs=pl.BlockSpec(memory_space=pl.ANY),
        scratch_shapes=[pltpu.VMEM((MOE_ROWS * slab, LANES), F32), pltpu.SemaphoreType.DMA(()),
                        pltpu.SemaphoreType.DMA(())],
    )
    return pl.pallas_call(
        _dispatch_kernel,
        grid_spec=grid_spec,
        out_shape=jax.ShapeDtypeStruct((n_rows * slab, LANES), F32),
        compiler_params=pltpu.CompilerParams(dimension_semantics=("arbitrary",), has_side_effects=True),
        name="moe_dispatch",
    )(dest_flat, pad_row, pad_len, tail, h_rows)


def _expert_kernel(item_e_ref, item_row_ref, item_nsub_ref, *refs):
    wg_refs = refs[:MOE_GU_PIECES]
    wu_refs = refs[MOE_GU_PIECES:2 * MOE_GU_PIECES]
    wd_refs = refs[2 * MOE_GU_PIECES:2 * MOE_GU_PIECES + MOE_DOWN_PIECES]
    xb_ref, yb_ref, xin_ref, yout_ref, x_ref, acc_ref, wgu_ref, wdb_ref, sem_in, sem_out = (
        refs[2 * MOE_GU_PIECES + MOE_DOWN_PIECES:])
    del item_e_ref
    it = pl.program_id(0)
    f = pl.program_id(1)
    n_items = pl.num_programs(0)
    nf = pl.num_programs(1)
    kp = wg_refs[0].shape[1]
    kc = kp * MOE_GU_PIECES
    ff = wg_refs[0].shape[2]
    fp = wd_refs[0].shape[1]
    fc = fp * MOE_DOWN_PIECES
    nsub = item_nsub_ref[it]
    row0 = item_row_ref[it]
    nxt = jnp.minimum(it + 1, n_items - 1)
    nsub_next = jnp.where(it + 1 < n_items, item_nsub_ref[nxt], 0)
    prv = jnp.maximum(it - 1, 0)
    nsub_prev = jnp.where(it > 0, item_nsub_ref[prv], 0)
    rb = MOE_ROWS
    slab = x_ref.shape[0] * kc // LANES
    blk = rb * slab

    def stage_rows(ref, s):
        return ref.at[pl.ds(pl.multiple_of(s * blk, blk), blk), :]

    def hbm_rows(ref, item_row, s):
        return ref.at[pl.ds(pl.multiple_of((item_row + s * rb) * slab, blk), blk), :]

    def in_copy(item_row, s):
        return pltpu.make_async_copy(hbm_rows(xb_ref, item_row, s), stage_rows(xin_ref, s), sem_in)

    def out_copy(item_row, s):
        return pltpu.make_async_copy(stage_rows(yout_ref, s), hbm_rows(yb_ref, item_row, s), sem_out)

    def loop(n, fn):
        def body(s, carry):
            fn(s)
            return carry
        lax.fori_loop(0, n, body, 0)

    @pl.when(nsub > 0)
    def _():
        @pl.when(f == 0)
        def _():
            @pl.when(it == 0)
            def _():
                x_ref[...] = jnp.zeros_like(x_ref)
                loop(nsub, lambda s: in_copy(row0, s).start())
            loop(nsub, lambda s: in_copy(row0, s).wait())

            def to_matrix(s):
                rows = pl.ds(pl.multiple_of(s * rb, rb), rb)
                base = pl.multiple_of(s * blk, blk)
                for j in range(slab):
                    c0 = (j * LANES) % kc
                    x_ref[(j * LANES) // kc, rows, c0:c0 + LANES] = (
                        xin_ref[pl.ds(base + j, rb, stride=slab), :].astype(BF16))
            loop(nsub, to_matrix)
            loop(nsub_next, lambda s: in_copy(item_row_ref[nxt], s).start())

        for q in range(MOE_GU_PIECES):
            wgu_ref[q * kp:(q + 1) * kp, :ff] = wg_refs[q][0].astype(BF16)
            wgu_ref[q * kp:(q + 1) * kp, ff:] = wu_refs[q][0].astype(BF16)
        for q in range(MOE_DOWN_PIECES):
            wdb_ref[pl.ds(pl.multiple_of(f * fc + q * fp, 2 * SUBLANES), fp), :] = wd_refs[q][0].astype(BF16)

        def gate_up(start, size):
            rows = pl.ds(pl.multiple_of(start, rb), size)
            part = jnp.dot(x_ref[f, rows, :], wgu_ref[...], preferred_element_type=F32)
            acc_ref[rows, :] = jnp.where(f > 0, acc_ref[rows, :], 0.0) + part

        gate_up(0, 2 * rb)

        def pair(p, carry):
            gate_up(p * (2 * rb), 2 * rb)
            return carry
        lax.fori_loop(1, nsub // 2, pair, 0)

        @pl.when((nsub % 2 == 1) & (nsub > 1))
        def _():
            gate_up((nsub - 1) * rb, rb)

        @pl.when(f == nf - 1)
        def _():
            loop(nsub_prev, lambda s: out_copy(item_row_ref[prv], s).wait())

            def down(start, size):
                rows = pl.ds(pl.multiple_of(start, rb), size)
                gate = acc_ref[rows, :ff]
                act = (gate * _sigmoid(gate) * acc_ref[rows, ff:]).astype(BF16)
                y = jnp.dot(act, wdb_ref[...], preferred_element_type=F32)
                base = pl.multiple_of(start * slab, blk)
                for j in range(slab):
                    yout_ref[pl.ds(base + j, size, stride=slab), :] = y[:, j * LANES:(j + 1) * LANES]

            down(0, 2 * rb)

            def down_pair(p, carry):
                down(p * (2 * rb), 2 * rb)
                return carry
            lax.fori_loop(1, nsub // 2, down_pair, 0)

            @pl.when((nsub % 2 == 1) & (nsub > 1))
            def _():
                down((nsub - 1) * rb, rb)

            loop(nsub, lambda s: out_copy(row0, s).start())

            @pl.when(nsub_next == 0)
            def _():
                loop(nsub, lambda s: out_copy(row0, s).wait())


def _experts(item_e, item_row, item_nsub, w_gate, w_up, w_down, xb):
    d = w_gate.shape[1]
    ff = w_gate.shape[2]
    nk = MOE_K_STEPS
    kc = d // nk
    fc = ff // nk
    assert d % nk == 0 and ff % nk == 0 and kc % LANES == 0 and fc % (2 * SUBLANES) == 0 and ff % LANES == 0
    n_items = item_e.shape[0]
    rows_max = MOE_ROWS * MOE_SUBS_PER_ITEM
    slab = d // LANES

    kp = kc // MOE_GU_PIECES
    fp = fc // MOE_DOWN_PIECES
    assert kc % MOE_GU_PIECES == 0 and fc % MOE_DOWN_PIECES == 0 and kp % (2 * SUBLANES) == 0 and fp % (2 * SUBLANES) == 0

    def piece(rows, cols, pieces, q):
        def index(it, k, ie, ir, ns):
            return ie[it], jnp.where(ns[it] > 0, k, nk - 1) * pieces + q, 0
        return pl.BlockSpec((1, rows, cols), index)

    grid_spec = pltpu.PrefetchScalarGridSpec(
        num_scalar_prefetch=3,
        grid=(n_items, nk),
        in_specs=[
            *[piece(kp, ff, MOE_GU_PIECES, q) for q in range(MOE_GU_PIECES)],
            *[piece(kp, ff, MOE_GU_PIECES, q) for q in range(MOE_GU_PIECES)],
            *[piece(fp, d, MOE_DOWN_PIECES, q) for q in range(MOE_DOWN_PIECES)],
            pl.BlockSpec(memory_space=pl.ANY),
        ],
        out_specs=pl.BlockSpec(memory_space=pl.ANY),
        scratch_shapes=[
            pltpu.VMEM((rows_max * slab, LANES), F32),
            pltpu.VMEM((rows_max * slab, LANES), F32),
            pltpu.VMEM((nk, rows_max, kc), BF16),
            pltpu.VMEM((rows_max, 2 * ff), F32),
            pltpu.VMEM((kc, 2 * ff), BF16),
            pltpu.VMEM((ff, d), BF16),
            pltpu.SemaphoreType.DMA(()),
            pltpu.SemaphoreType.DMA(()),
        ],
    )
    return pl.pallas_call(
        _expert_kernel,
        grid_spec=grid_spec,
        out_shape=jax.ShapeDtypeStruct(xb.shape, F32),
        input_output_aliases={3 + 2 * MOE_GU_PIECES + MOE_DOWN_PIECES: 0},
        compiler_params=pltpu.CompilerParams(dimension_semantics=("arbitrary", "arbitrary"),
                                             vmem_limit_bytes=VMEM_LIMIT_BYTES, has_side_effects=True),
        name="moe_experts",
    )(item_e, item_row, item_nsub, *([w_gate] * MOE_GU_PIECES), *([w_up] * MOE_GU_PIECES),
      *([w_down] * MOE_DOWN_PIECES), xb)


COMBINE_TILE = 256


def _combine_kernel(alpha, dest_ref, yb_ref, h_ref, gate_ref, g_ref, b_ref, o_ref, buf_ref, sem):
    tm = COMBINE_TILE
    i = pl.program_id(0)
    slab = buf_ref.shape[2] // tm

    def issue(tile, half):
        def start(n, carry):
            for k in range(TOP_K):
                pltpu.make_async_copy(_row_slab(yb_ref, dest_ref[TOP_K * (tile * tm + n) + k], slab),
                                      _row_slab(buf_ref.at[half, k], n, slab), sem.at[half]).start()
            return carry
        lax.fori_loop(0, tm, start, 0, unroll=8)

    @pl.when(i == 0)
    def _():
        issue(0, 0)

    @pl.when(i + 1 < pl.num_programs(0))
    def _():
        issue(i + 1, (i + 1) % 2)

    half = i % 2
    for k in range(TOP_K):
        pltpu.make_async_copy(yb_ref.at[pl.ds(0, buf_ref.shape[2]), :], buf_ref.at[half, k], sem.at[half]).wait()
    gate = gate_ref[...]
    g0 = gate[:, 0:1]
    g1 = gate[:, 1:2]
    col = lambda k, j: buf_ref[half, k, pl.ds(j, tm, stride=slab), :]
    ff = jnp.concatenate([g0 * col(0, j) + g1 * col(1, j) for j in range(slab)], axis=1)
    o_ref[...] = _layer_norm(alpha * h_ref[...] + ff, g_ref[...], b_ref[...])


def _combine(dest_flat, yb, h, gates, ln_g, ln_b, alpha):
    m, d = h.shape
    tm = COMBINE_TILE
    assert m % tm == 0
    grid_spec = pltpu.PrefetchScalarGridSpec(
        num_scalar_prefetch=1,
        grid=(m // tm,),
        in_specs=[pl.BlockSpec(memory_space=pl.ANY),
                  pl.BlockSpec((tm, d), lambda i, dr: (i, 0)),
                  pl.BlockSpec((tm, LANES), lambda i, dr: (i, 0)),
                  pl.BlockSpec((1, d), lambda i, dr: (0, 0)),
                  pl.BlockSpec((1, d), lambda i, dr: (0, 0))],
        out_specs=pl.BlockSpec((tm, d), lambda i, dr: (i, 0)),
        scratch_shapes=[pltpu.VMEM((2, TOP_K, tm * (d // LANES), LANES), F32), pltpu.SemaphoreType.DMA((2,))],
    )
    return pl.pallas_call(
        functools.partial(_combine_kernel, alpha),
        grid_spec=grid_spec,
        out_shape=jax.ShapeDtypeStruct((m, d), F32),
        compiler_params=_cparams(("arbitrary",)),
        name="moe_combine_ln",
    )(dest_flat, yb, h, gates, ln_g.reshape(1, d), ln_b.reshape(1, d))


def _moe_tables(counts):
    nsub_e = (counts + MOE_ROWS - 1) // MOE_ROWS
    pstart = (jnp.cumsum(nsub_e) - nsub_e) * MOE_ROWS
    nitem_e = (nsub_e + MOE_SUBS_PER_ITEM - 1) // MOE_SUBS_PER_ITEM
    item_end = jnp.cumsum(nitem_e)
    return nsub_e, pstart, nitem_e, item_end


def _moe(h, h_rows, eid, gates, w_gate, w_up, w_down, ln_g, ln_b, alpha):
    m, d = h.shape
    n_assign = m * TOP_K
    n_blocks = (n_assign + N_EXPERTS * (MOE_ROWS - 1) + MOE_ROWS - 1) // MOE_ROWS
    n_rows = n_blocks * MOE_ROWS
    n_items = N_EXPERTS + n_assign // (MOE_ROWS * MOE_SUBS_PER_ITEM)

    eid_t = eid[:, :TOP_K].T
    counts = _expert_counts(eid_t)[:, 0].astype(I32)
    nsub_e, pstart, nitem_e, item_end = _moe_tables(counts)
    dest_t = _expert_slots(eid_t, pstart.astype(F32).reshape(N_EXPERTS, 1))
    dest_flat = dest_t.T.reshape(-1)

    it = jnp.arange(n_items, dtype=I32)
    total_items = item_end[-1]
    item_e = jnp.minimum(jnp.sum(item_end[None, :] <= it[:, None], axis=1).astype(I32), N_EXPERTS - 1)
    j = it - (item_end - nitem_e)[item_e]
    used = it < total_items
    last_e = item_e[jnp.maximum(total_items - 1, 0)]
    item_nsub = jnp.where(used, jnp.clip(nsub_e[item_e] - j * MOE_SUBS_PER_ITEM, 0, MOE_SUBS_PER_ITEM), 0).astype(I32)
    item_row = jnp.where(used, pstart[item_e] + j * (MOE_ROWS * MOE_SUBS_PER_ITEM), 0).astype(I32)
    item_e = jnp.where(used, item_e, last_e).astype(I32)

    pad_row = (pstart + counts).astype(I32)
    pad_len = (nsub_e * MOE_ROWS - counts).astype(I32)
    used_rows = jnp.sum(nsub_e) * MOE_ROWS
    tail = jnp.stack([used_rows, (n_rows - used_rows) // MOE_ROWS]).astype(I32)
    xb = _dispatch(dest_flat, pad_row, pad_len, tail, h_rows, m, n_rows)
    yb = _experts(item_e, item_row, item_nsub, w_gate, w_up, w_down, xb)
    return _combine(dest_flat, yb, h, gates, ln_g, ln_b, alpha)


def _pad_cols(w, n):
    return jnp.pad(w, ((0, 0), (0, n - w.shape[1])))


def _pad_rows(w, n):
    return jnp.pad(w, ((0, n - w.shape[0]), (0, 0)))


def kernel(x, w_in, attn_sinks, rw_mu_rkv, rw_mu_wag, rw_w0, rw_w1, rw_w2, rw_a0, rw_a1, rw_a2, rw_g1, rw_g2, rw_k_k, rw_k_a, rw_r_k, rw_lnx_w, rw_lnx_b, p_attn, p_rwkv, w_o, ln1_g, ln1_b, w_group, b_group, w_expert, b_expert, w_gate, w_up, w_down, ln2_g, ln2_b):
    b, t, d = x.shape
    depth = w_in.shape[0]
    m = b * t
    alpha = (2.0 * depth) ** 0.25
    cosb, sinb = _rope_tables(t)
    qkv_w = ATTN_Q_W + 2 * ATTN_KV_W
    rkv_w = 3 * RWKV_W
    h = x
    for l in range(depth):
        hf = h.reshape(m, d)
        hb = hf.astype(BF16)
        qkv = _matmul_cols(hb, w_in[l], 0, qkv_w, F32)
        rkv = _matmul_cols(hb, w_in[l], qkv_w, rkv_w, BF16)
        gates = _matmul_cols(hb, w_in[l], qkv_w + rkv_w, 2 * d, BF16)

        y_a = _attention(qkv.reshape(b, t, qkv_w), attn_sinks[l], cosb, sinb)

        lora = lambda w, n: _pad_cols(w, n).astype(BF16)
        lorb = lambda w, n: _pad_rows(w, n).astype(BF16)
        n_w = -(-rw_w1.shape[2] // LANES) * LANES
        n_a = -(-rw_a1.shape[2] // LANES) * LANES
        n_g = -(-rw_g1.shape[2] // LANES) * LANES
        r_, k_, v_, lw_, cum_, ag_, g_ = _rwkv_prep(
            h, rkv.reshape(b, t, rkv_w), rw_mu_rkv[l], rw_mu_wag[l], rw_w0[l],
            lora(rw_w1[l], n_w), lorb(rw_w2[l], n_w), rw_a0[l], lora(rw_a1[l], n_a), lorb(rw_a2[l], n_a),
            lora(rw_g1[l], n_g), lorb(rw_g2[l], n_g))
        y_r = _wkv_scan(r_, k_, v_, lw_, cum_, ag_, g_, rw_k_k[l], rw_k_a[l], rw_r_k[l], rw_lnx_w[l], rw_lnx_b[l])

        merged = _merge(y_a.reshape(m, ATTN_Q_W), y_r.reshape(m, RWKV_W), gates,
                        p_attn[l].astype(BF16), p_rwkv[l].astype(BF16))
        w_router = _pad_cols(jnp.concatenate([w_group[l], w_expert[l]], axis=1), LANES)
        w_router_hi = w_router.astype(BF16)
        w_router = jnp.stack([w_router_hi, (w_router - w_router_hi.astype(F32)).astype(BF16)])
        b_router = _pad_cols(jnp.concatenate([b_group[l], b_expert[l]])[None, :], LANES)
        h1, h1_rows, eid, gate = _outproj_router(merged, hf, w_o[l].astype(BF16), ln1_g[l], ln1_b[l],
                                        w_router, b_router, alpha)
        h2 = _moe(h1, h1_rows, eid, gate, w_gate[l], w_up[l], w_down[l], ln2_g[l], ln2_b[l], alpha)
        h = h2.reshape(b, t, d)
    return h
```

```python
import functools

import jax
import jax.numpy as jnp
from jax import lax
from jax.experimental import pallas as pl
from jax.experimental.pallas import tpu as pltpu

F32 = jnp.float32
BF16 = jnp.bfloat16
I32 = jnp.int32

HEAD_DIM = 64
ATTN_Q_HEADS = 16
ATTN_KV_HEADS = 4
ATTN_GROUP = ATTN_Q_HEADS // ATTN_KV_HEADS
ATTN_Q_W = ATTN_Q_HEADS * HEAD_DIM
ATTN_KV_W = ATTN_KV_HEADS * HEAD_DIM
WINDOW = 128
ROPE_THETA = 10000.0
RWKV_HEADS = 16
RWKV_N = 64
RWKV_W = RWKV_HEADS * RWKV_N
RWKV_GN_EPS = 64e-5
N_GROUPS = 8
EXPERTS_PER_GROUP = 8
N_EXPERTS = N_GROUPS * EXPERTS_PER_GROUP
TOP_K = 2
LN_EPS = 1e-5

LANES = 128
SUBLANES = 8
VMEM_LIMIT_BYTES = 56 * 1024 * 1024

WKV_CHUNK = 64
WKV_CHUNKS_PER_STEP = 4
WKV_HEADS_PER_STEP = 8
MOE_ROWS = 128
MOE_SUBS_PER_ITEM = 4
MOE_K_STEPS = 4
MOE_WEIGHT_SLOTS = 3
ROUTE_TILE = 512


def _cparams(sem, vmem=VMEM_LIMIT_BYTES):
    return pltpu.CompilerParams(dimension_semantics=sem, vmem_limit_bytes=vmem)


def _sigmoid(x):
    return 1.0 / (1.0 + jnp.exp(-x))


def _dot(a, b):
    return jnp.dot(a.astype(BF16), b.astype(BF16), preferred_element_type=F32)


def _dot_nt(a, b):
    return lax.dot_general(a.astype(BF16), b.astype(BF16), (((1,), (1,)), ((), ())),
                           preferred_element_type=F32)


def _layer_norm(t, g, b):
    mu = jnp.mean(t, axis=-1, keepdims=True)
    d = t - mu
    var = jnp.mean(d * d, axis=-1, keepdims=True)
    return d * lax.rsqrt(var + LN_EPS) * g + b


def _matmul_kernel(a_ref, b_ref, o_ref):
    o_ref[...] = jnp.dot(a_ref[...], b_ref[...].astype(BF16), preferred_element_type=F32).astype(o_ref.dtype)


def _matmul_cols(a, b, col0, ncols, out_dtype, tm=2048, tn=512):
    m, k = a.shape
    tm = min(tm, m)
    cb = col0 // tn
    assert col0 % tn == 0 and ncols % tn == 0 and m % tm == 0
    return pl.pallas_call(
        _matmul_kernel,
        grid=(m // tm, ncols // tn),
        in_specs=[pl.BlockSpec((tm, k), lambda i, j: (i, 0)),
                  pl.BlockSpec((k, tn), lambda i, j: (0, j + cb))],
        out_specs=pl.BlockSpec((tm, tn), lambda i, j: (i, j)),
        out_shape=jax.ShapeDtypeStruct((m, ncols), out_dtype),
        compiler_params=_cparams(("parallel", "arbitrary")),
        name="inproj_matmul",
    )(a, b)


def _rope(x, cosb, sinb):
    half = HEAD_DIM // 2
    lane = lax.broadcasted_iota(I32, cosb.shape, 1)
    first_half = (lane % HEAD_DIM) < half
    outs = []
    for g in range(x.shape[1] // LANES):
        xg = x[:, g * LANES:(g + 1) * LANES]
        partner = jnp.where(first_half, pltpu.roll(xg, LANES - half, axis=1), pltpu.roll(xg, half, axis=1))
        outs.append(xg * cosb + partner * sinb)
    return outs


def _attn_kernel(sinks_ref, q_ref, kc_ref, kp_ref, vc_ref, vp_ref, cosc_ref, sinc_ref, cosp_ref, sinp_ref, o_ref):
    blk = pl.program_id(1)
    tq = q_ref.shape[1]
    qg = _rope(q_ref[0], cosc_ref[...], sinc_ref[...])
    kcg = _rope(kc_ref[0], cosc_ref[...], sinc_ref[...])
    kpg = _rope(kp_ref[0], cosp_ref[...], sinp_ref[...])
    vc = vc_ref[0]
    vp = vp_ref[0]

    def head(groups, h):
        g = groups[h // 2]
        return g[:, (h % 2) * HEAD_DIM:(h % 2 + 1) * HEAD_DIM]

    rows = ATTN_GROUP * tq
    qi = lax.broadcasted_iota(I32, (rows, 2 * tq), 0) % tq
    kj = lax.broadcasted_iota(I32, (rows, 2 * tq), 1)
    dist = qi + tq - kj
    valid = (dist >= 0) & (dist < WINDOW) & ((blk > 0) | (kj >= tq))
    rid = lax.broadcasted_iota(I32, (rows, 1), 0) // tq
    scale = HEAD_DIM ** -0.5
    for kvh in range(ATTN_KV_HEADS):
        qh = jnp.concatenate([head(qg, kvh * ATTN_GROUP + g) for g in range(ATTN_GROUP)], axis=0)
        kw = jnp.concatenate([head(kpg, kvh), head(kcg, kvh)], axis=0)
        vw = jnp.concatenate([vp[:, kvh * HEAD_DIM:(kvh + 1) * HEAD_DIM],
                              vc[:, kvh * HEAD_DIM:(kvh + 1) * HEAD_DIM]], axis=0)
        s = _dot_nt(qh, kw) * scale
        s = jnp.where(valid, s, -jnp.inf)
        sink = jnp.zeros((rows, 1), F32)
        for g in range(ATTN_GROUP):
            sink = jnp.where(rid == g, sinks_ref[kvh * ATTN_GROUP + g], sink)
        m = jnp.maximum(jnp.max(s, axis=-1, keepdims=True), sink)
        e = jnp.exp(s - m)
        denom = jnp.sum(e, axis=-1, keepdims=True) + jnp.exp(sink - m)
        o = _dot(e, vw) / denom
        for g in range(ATTN_GROUP):
            hq = kvh * ATTN_GROUP + g
            o_ref[0, :, hq * HEAD_DIM:(hq + 1) * HEAD_DIM] = o[g * tq:(g + 1) * tq].astype(o_ref.dtype)


def _attention(qkv, sinks, cosb, sinb):
    b, t, _ = qkv.shape
    tq = WINDOW
    nb = t // tq
    kcol = ATTN_Q_W // ATTN_KV_W
    prev = lambda i: jnp.maximum(i - 1, 0)
    grid_spec = pltpu.PrefetchScalarGridSpec(
        num_scalar_prefetch=0,
        grid=(b, nb),
        in_specs=[
            pl.BlockSpec(memory_space=pltpu.SMEM),
            pl.BlockSpec((1, tq, ATTN_Q_W), lambda bi, i: (bi, i, 0)),
            pl.BlockSpec((1, tq, ATTN_KV_W), lambda bi, i: (bi, i, kcol)),
            pl.BlockSpec((1, tq, ATTN_KV_W), lambda bi, i: (bi, prev(i), kcol)),
            pl.BlockSpec((1, tq, ATTN_KV_W), lambda bi, i: (bi, i, kcol + 1)),
            pl.BlockSpec((1, tq, ATTN_KV_W), lambda bi, i: (bi, prev(i), kcol + 1)),
            pl.BlockSpec((tq, LANES), lambda bi, i: (i, 0)),
            pl.BlockSpec((tq, LANES), lambda bi, i: (i, 0)),
            pl.BlockSpec((tq, LANES), lambda bi, i: (prev(i), 0)),
            pl.BlockSpec((tq, LANES), lambda bi, i: (prev(i), 0)),
        ],
        out_specs=pl.BlockSpec((1, tq, ATTN_Q_W), lambda bi, i: (bi, i, 0)),
    )
    return pl.pallas_call(
        _attn_kernel,
        grid_spec=grid_spec,
        out_shape=jax.ShapeDtypeStruct((b, t, ATTN_Q_W), BF16),
        compiler_params=_cparams(("parallel", "arbitrary")),
        name="swa_attention",
    )(sinks, qkv, qkv, qkv, qkv, qkv, cosb, sinb, cosb, sinb)


def _rope_tables(t):
    inv = 1.0 / (ROPE_THETA ** (jnp.arange(0, HEAD_DIM, 2, dtype=F32) / HEAD_DIM))
    ang = jnp.arange(t, dtype=F32)[:, None] * inv[None, :]
    cos, sin = jnp.cos(ang), jnp.sin(ang)
    reps = LANES // HEAD_DIM
    cosb = jnp.tile(jnp.concatenate([cos, cos], axis=-1), (1, reps))
    sinb = jnp.tile(jnp.concatenate([-sin, sin], axis=-1), (1, reps))
    return cosb, sinb


def _rwkv_prep_kernel(h_ref, hp_ref, r_ref, k_ref, v_ref, rp_ref, kp_ref, vp_ref,
                      mu_rkv_ref, mu_wag_ref, w0_ref, w1_ref, w2_ref, a0_ref, a1_ref, a2_ref,
                      g1_ref, g2_ref,
                      ro_ref, ko_ref, vo_ref, lwo_ref, cumo_ref, ago_ref, go_ref):
    first = pl.program_id(1) == 0

    def shifted(cur, prev_ref):
        last_row = prev_ref.shape[1] - 1
        prev_row = jnp.where(first, 0.0, prev_ref[0, last_row:last_row + 1, :].astype(F32))
        rowid = lax.broadcasted_iota(I32, cur.shape, 0)
        return jnp.where(rowid == 0, prev_row, pltpu.roll(cur, 1, axis=0))

    h = h_ref[0]
    xx = shifted(h, hp_ref) - h
    xw = h + xx * mu_wag_ref[0:1, :]
    xa = h + xx * mu_wag_ref[1:2, :]
    xg = h + xx * mu_wag_ref[2:3, :]
    w_raw = w0_ref[...] + _dot(jnp.tanh(_dot(xw, w1_ref[...])), w2_ref[...])
    neg = -w_raw
    softplus = jnp.maximum(neg, 0.0) + jnp.log1p(jnp.exp(-jnp.abs(neg)))
    w = -softplus - 0.5
    lw = -jnp.exp(w)
    tm = lw.shape[0]
    row = lax.broadcasted_iota(I32, (tm, tm), 0)
    col = lax.broadcasted_iota(I32, (tm, tm), 1)
    tri = ((row >= col) & (row // WKV_CHUNK == col // WKV_CHUNK)).astype(BF16)
    cum = sum(jnp.dot(tri, piece, preferred_element_type=F32) for piece in _split3(lw))
    ag = _sigmoid(a0_ref[...] + _dot(_dot(xa, a1_ref[...]), a2_ref[...]))
    go_ref[0] = _dot(_sigmoid(_dot(xg, g1_ref[...])), g2_ref[...])

    r = r_ref[0].astype(F32)
    k = k_ref[0].astype(F32)
    v = v_ref[0].astype(F32)
    r = r + (shifted(r, rp_ref) - r) * mu_rkv_ref[0:1, :]
    k = k + (shifted(k, kp_ref) - k) * mu_rkv_ref[1:2, :]
    v = v + (shifted(v, vp_ref) - v) * mu_rkv_ref[2:3, :]
    nchunk = tm // WKV_CHUNK
    for hd in range(RWKV_HEADS):
        sl = slice(hd * RWKV_N, (hd + 1) * RWKV_N)
        for ref, val in ((ro_ref, r), (ko_ref, k), (vo_ref, v), (lwo_ref, lw), (cumo_ref, cum), (ago_ref, ag)):
            ref[0, :, hd] = val[:, sl].reshape(nchunk, WKV_CHUNK, RWKV_N).astype(ref.dtype)


def _rwkv_prep(h, rkv, mu_rkv, mu_wag, w0, w1, w2, a0, a1, a2, g1, g2, tm=256):
    b, t, d = h.shape
    tm = min(tm, t)
    c = RWKV_W
    prevblk = lambda i, rows: jnp.maximum(i * (tm // rows) - 1, 0)
    rkv_rows = SUBLANES * (4 // rkv.dtype.itemsize)
    full = lambda arr: pl.BlockSpec(arr.shape, lambda bi, i: (0,) * arr.ndim)
    row = lambda arr: arr.reshape(1, -1)
    w0, a0 = row(w0), row(a0)
    in_specs = [
        pl.BlockSpec((1, tm, d), lambda bi, i: (bi, i, 0)),
        pl.BlockSpec((1, SUBLANES, d), lambda bi, i: (bi, prevblk(i, SUBLANES), 0)),
        pl.BlockSpec((1, tm, c), lambda bi, i: (bi, i, 0)),
        pl.BlockSpec((1, tm, c), lambda bi, i: (bi, i, 1)),
        pl.BlockSpec((1, tm, c), lambda bi, i: (bi, i, 2)),
        pl.BlockSpec((1, rkv_rows, c), lambda bi, i: (bi, prevblk(i, rkv_rows), 0)),
        pl.BlockSpec((1, rkv_rows, c), lambda bi, i: (bi, prevblk(i, rkv_rows), 1)),
        pl.BlockSpec((1, rkv_rows, c), lambda bi, i: (bi, prevblk(i, rkv_rows), 2)),
    ] + [full(a) for a in (mu_rkv, mu_wag, w0, w1, w2, a0, a1, a2, g1, g2)]
    assert tm % WKV_CHUNK == 0
    hm = lambda dt: jax.ShapeDtypeStruct((b, t // WKV_CHUNK, RWKV_HEADS, WKV_CHUNK, RWKV_N), dt)
    hm_spec = pl.BlockSpec((1, tm // WKV_CHUNK, RWKV_HEADS, WKV_CHUNK, RWKV_N), lambda bi, i: (bi, i, 0, 0, 0))
    return pl.pallas_call(
        _rwkv_prep_kernel,
        grid=(b, t // tm),
        in_specs=in_specs,
        out_specs=[hm_spec] * 6 + [pl.BlockSpec((1, tm, c), lambda bi, i: (bi, i, 0))],
        out_shape=[hm(BF16), hm(BF16), hm(BF16), hm(F32), hm(F32), hm(BF16), jax.ShapeDtypeStruct((b, t, c), F32)],
        compiler_params=_cparams(("parallel", "arbitrary")),
        name="rwkv_prep",
    )(h, h, rkv, rkv, rkv, rkv, rkv, rkv, mu_rkv, mu_wag, w0, w1, w2, a0, a1, a2, g1, g2)


def _split3(x):
    hi = x.astype(BF16)
    r1 = x - hi.astype(F32)
    mid = r1.astype(BF16)
    lo = (r1 - mid.astype(F32)).astype(BF16)
    return hi, mid, lo


def _bmm(a, b):
    return jnp.einsum("gmk,gkn->gmn", a.astype(BF16), b.astype(BF16), preferred_element_type=F32)


def _bmm_nt(a, b):
    return jnp.einsum("gmk,gnk->gmn", a.astype(BF16), b.astype(BF16), preferred_element_type=F32)


def _bmm_tn(a, b):
    return jnp.einsum("gtm,gtn->gmn", a.astype(BF16), b.astype(BF16), preferred_element_type=F32)


def _wkv_kernel(r_ref, k_ref, v_ref, lw_ref, cum_ref, ag_ref, g_ref, kk_ref, ka_ref, rk_ref, lnw_ref, lnb_ref,
                y_ref, s_ref, st_ref):
    c = WKV_CHUNK
    n = RWKV_N
    nc, hb = r_ref.shape[1], r_ref.shape[2]
    tc = nc * c
    g = nc * hb

    @pl.when(pl.program_id(2) == 0)
    def _():
        s_ref[...] = jnp.zeros_like(s_ref)

    chunks = lambda ref: ref[0].reshape(g, c, n).astype(F32)
    r, k, v, lw, cum, ag = (chunks(ref) for ref in (r_ref, k_ref, v_ref, lw_ref, cum_ref, ag_ref))
    per_head = lambda t: t.reshape(nc, hb, c, n)
    kk = (per_head(k) * kk_ref[...]).reshape(g, c, n)
    k = (per_head(k) * (1.0 + (per_head(ag) - 1.0) * ka_ref[...])).reshape(g, c, n)
    kk = kk / jnp.maximum(jnp.sqrt(jnp.sum(kk * kk, axis=-1, keepdims=True)), 1e-12)
    dinc = jnp.exp(cum)
    dinv = jnp.exp(-cum)
    rt = r * dinc
    kt = k * dinv
    at = -kk * jnp.exp(cum - lw)
    bt = kk * ag * dinv
    d_chunk = dinc[:, c - 1:c, :]

    row = lax.broadcasted_iota(I32, (1, c, c), 1)
    col = lax.broadcasted_iota(I32, (1, c, c), 2)
    strict = row > col
    incl = row >= col
    eye = jnp.broadcast_to((row == col).astype(F32), (g, c, c))
    p = _bmm_nt(jnp.concatenate([at, rt], axis=1), jnp.concatenate([bt, kt], axis=1))
    a_ab = jnp.where(strict, p[:, :c, :c], 0.0)
    a_ak = jnp.where(strict, p[:, :c, c:], 0.0)
    a_rb = jnp.where(incl, p[:, c:, :c], 0.0)
    a_rk = jnp.where(incl, p[:, c:, c:], 0.0)
    x = _bmm(a_ab, a_ab)
    tinv = eye + a_ab
    levels = c.bit_length() - 2
    for j in range(1, levels):
        both = _bmm(jnp.concatenate([x, tinv], axis=1), x)
        x = both[:, :c]
        tinv = tinv + both[:, c:]
    tinv = tinv + _bmm(tinv, x)
    z = _bmm(a_ak, v)
    ta = _bmm(tinv, jnp.concatenate([at, z], axis=2))
    ry = _bmm(a_rb, ta)
    rp = rt + ry[:, :, :n]
    yv = ry[:, :, n:] + _bmm(a_rk, v)
    moff = _bmm_tn(ta[:, :, :n], bt).reshape(nc, hb, n, n)
    n2 = _bmm_tn(jnp.concatenate([ta[:, :, n:], v], axis=1),
                 jnp.concatenate([bt, kt], axis=1)).reshape(nc, hb, n, n)
    dch = d_chunk.reshape(nc, hb, 1, n)

    s = s_ref[...]
    for ci in range(nc):
        st_ref[ci] = s
        s = (s + _bmm(s, moff[ci]) + n2[ci]) * dch[ci]
    s_ref[...] = s

    y = _bmm_nt(rp, st_ref[...].reshape(g, n, n)) + yv
    mu = jnp.mean(y, axis=-1, keepdims=True)
    yc = y - mu
    var = jnp.mean(yc * yc, axis=-1, keepdims=True)
    y = per_head(yc * lax.rsqrt(var + RWKV_GN_EPS)) * lnw_ref[...] + lnb_ref[...]
    bonus = jnp.sum(per_head(r * k) * rk_ref[...], axis=-1, keepdims=True)
    y = y + bonus * per_head(v)
    y = jnp.concatenate([jnp.concatenate([y[ci, hd] for hd in range(hb)], axis=1) for ci in range(nc)], axis=0)
    y_ref[0] = (y * g_ref[0]).astype(y_ref.dtype)


def _wkv_scan(r, k, v, lw, cum, ag, g, k_k, k_a, r_k, lnx_w, lnx_b):
    b, nchunks, hh, c, n = r.shape
    t = nchunks * c
    nc = min(WKV_CHUNKS_PER_STEP, nchunks)
    tc = nc * c
    hb = WKV_HEADS_PER_STEP
    blk = pl.BlockSpec((1, nc, hb, c, n), lambda bi, hi, i: (bi, i, hi, 0, 0))
    tok = pl.BlockSpec((1, tc, hb * n), lambda bi, hi, i: (bi, i, hi))
    par = pl.BlockSpec((hb, 1, n), lambda bi, hi, i: (hi, 0, 0))
    per_head = lambda arr: arr.reshape(hh, 1, n)
    return pl.pallas_call(
        _wkv_kernel,
        grid=(b, hh // hb, t // tc),
        in_specs=[blk] * 6 + [tok] + [par] * 5,
        out_specs=tok,
        out_shape=jax.ShapeDtypeStruct((b, t, hh * n), BF16),
        scratch_shapes=[pltpu.VMEM((hb, n, n), F32), pltpu.VMEM((nc, hb, n, n), F32)],
        compiler_params=_cparams(("parallel", "parallel", "arbitrary")),
        name="wkv7_scan",
    )(r, k, v, lw, cum, ag, g, per_head(k_k), per_head(k_a), per_head(r_k), per_head(lnx_w), per_head(lnx_b))


def _merge_kernel(ya_ref, yr_ref, ga_ref, gr_ref, pa_ref, pr_ref, o_ref):
    ma = jnp.dot(ya_ref[...], pa_ref[...], preferred_element_type=F32)
    mr = jnp.dot(yr_ref[...], pr_ref[...], preferred_element_type=F32)
    o_ref[...] = (_sigmoid(ga_ref[...].astype(F32)) * ma + _sigmoid(gr_ref[...].astype(F32)) * mr).astype(o_ref.dtype)


def _merge(ya, yr, gates, p_attn, p_rwkv, tm=256):
    m, c = ya.shape
    d = p_attn.shape[1]
    tm = min(tm, m)
    tile = pl.BlockSpec((tm, c), lambda i: (i, 0))
    return pl.pallas_call(
        _merge_kernel,
        grid=(m // tm,),
        in_specs=[tile, tile,
                  pl.BlockSpec((tm, d), lambda i: (i, 0)),
                  pl.BlockSpec((tm, d), lambda i: (i, 1)),
                  pl.BlockSpec((c, d), lambda i: (0, 0)),
                  pl.BlockSpec((c, d), lambda i: (0, 0))],
        out_specs=pl.BlockSpec((tm, d), lambda i: (i, 0)),
        out_shape=jax.ShapeDtypeStruct((m, d), BF16),
        compiler_params=_cparams(("parallel",)),
        name="gated_merge",
    )(ya, yr, gates, gates, p_attn, p_rwkv)


def _outproj_router_kernel(alpha, mg_ref, x_ref, wo_ref, g_ref, b_ref, wr_ref, br_ref,
                           h_ref, hrow_ref, eid_ref, gate_ref):
    mix = jnp.dot(mg_ref[...], wo_ref[...], preferred_element_type=F32)
    h = _layer_norm(alpha * x_ref[...] + mix, g_ref[...], b_ref[...])
    h_ref[...] = h
    slab = h.shape[1] // LANES
    for j in range(slab):
        hrow_ref[pl.ds(j, h.shape[0], stride=slab), :] = h[:, j * LANES:(j + 1) * LANES]
    h_hi = h.astype(BF16)
    h_lo = (h - h_hi.astype(F32)).astype(BF16)
    logits = (jnp.dot(h_hi, wr_ref[0], preferred_element_type=F32)
              + jnp.dot(h_lo, wr_ref[0], preferred_element_type=F32)
              + jnp.dot(h_hi, wr_ref[1], preferred_element_type=F32)) + br_ref[...]
    lane = lax.broadcasted_iota(I32, logits.shape, 1)
    ninf = -jnp.inf
    big = jnp.int32(2 * LANES)
    glog = jnp.where(lane < N_GROUPS, logits, ninf)
    gmax = jnp.max(glog, axis=-1, keepdims=True)
    gidx = jnp.min(jnp.where(glog == gmax, lane, big), axis=-1, keepdims=True)
    gtop = 1.0 / jnp.sum(jnp.exp(glog - gmax), axis=-1, keepdims=True)
    eg = (lane - N_GROUPS) // EXPERTS_PER_GROUP
    in_group = (lane >= N_GROUPS) & (lane < N_GROUPS + N_EXPERTS) & (eg == gidx)
    el = jnp.where(in_group, logits, ninf)
    m1 = jnp.max(el, axis=-1, keepdims=True)
    i1 = jnp.min(jnp.where(el == m1, lane, big), axis=-1, keepdims=True)
    el2 = jnp.where(lane == i1, ninf, el)
    m2 = jnp.max(el2, axis=-1, keepdims=True)
    i2 = jnp.min(jnp.where(el2 == m2, lane, big), axis=-1, keepdims=True)
    t = jnp.exp(m2 - m1)
    p1 = 1.0 / (1.0 + t)
    p2 = t / (1.0 + t)
    eid_ref[...] = jnp.where(lane == 0, i1 - N_GROUPS, jnp.where(lane == 1, i2 - N_GROUPS, 0))
    gate_ref[...] = jnp.where(lane == 0, gtop * p1, jnp.where(lane == 1, gtop * p2, 0.0))


def _outproj_router(merged, x, w_o, ln_g, ln_b, w_router, b_router, alpha, tm=512):
    m, d = x.shape
    tm = min(tm, m)
    tile = pl.BlockSpec((tm, d), lambda i: (i, 0))
    vec = pl.BlockSpec((1, d), lambda i: (0, 0))
    small = pl.BlockSpec((tm, LANES), lambda i: (i, 0))
    return pl.pallas_call(
        functools.partial(_outproj_router_kernel, alpha),
        grid=(m // tm,),
        in_specs=[tile, tile, pl.BlockSpec((d, d), lambda i: (0, 0)), vec, vec,
                  pl.BlockSpec((2, d, LANES), lambda i: (0, 0, 0)), pl.BlockSpec((1, LANES), lambda i: (0, 0))],
        out_specs=[tile, pl.BlockSpec((tm * (d // LANES), LANES), lambda i: (i, 0)), small, small],
        out_shape=[jax.ShapeDtypeStruct((m, d), F32), jax.ShapeDtypeStruct((m * (d // LANES), LANES), F32),
                   jax.ShapeDtypeStruct((m, LANES), I32), jax.ShapeDtypeStruct((m, LANES), F32)],
        compiler_params=_cparams(("parallel",)),
        name="outproj_ln_router",
    )(merged, x, w_o, ln_g.reshape(1, d), ln_b.reshape(1, d), w_router, b_router)


def _onehots(eid_ref):
    tm = eid_ref.shape[1]
    e_iota = lax.broadcasted_iota(I32, (N_EXPERTS, tm), 0)
    oh0 = (eid_ref[0:1, :] == e_iota).astype(F32)
    oh1 = (eid_ref[1:2, :] == e_iota).astype(F32)
    return oh0, oh1


def _count_kernel(eid_ref, cnt_ref):
    @pl.when(pl.program_id(0) == 0)
    def _():
        cnt_ref[...] = jnp.zeros_like(cnt_ref)

    oh0, oh1 = _onehots(eid_ref)
    cnt_ref[...] += jnp.sum(oh0 + oh1, axis=1, keepdims=True)


def _slot_kernel(eid_ref, pstart_ref, dest_ref, run_ref):
    @pl.when(pl.program_id(0) == 0)
    def _():
        run_ref[...] = jnp.zeros_like(run_ref)

    tm = eid_ref.shape[1]
    oh0, oh1 = _onehots(eid_ref)
    both = oh0 + oh1
    earlier = (lax.broadcasted_iota(I32, (tm, tm), 0) < lax.broadcasted_iota(I32, (tm, tm), 1)).astype(BF16)
    pre = jnp.dot(both.astype(BF16), earlier, preferred_element_type=F32)
    base = pre + run_ref[...] + pstart_ref[...]
    dest_ref[0:1, :] = jnp.sum(oh0 * base, axis=0, keepdims=True).astype(I32)
    dest_ref[1:2, :] = jnp.sum(oh1 * base, axis=0, keepdims=True).astype(I32)
    run_ref[...] += jnp.sum(both, axis=1, keepdims=True)


def _expert_counts(eid_t):
    m = eid_t.shape[1]
    tm = min(ROUTE_TILE, m)
    return pl.pallas_call(
        _count_kernel,
        grid=(m // tm,),
        in_specs=[pl.BlockSpec((TOP_K, tm), lambda i: (0, i))],
        out_specs=pl.BlockSpec((N_EXPERTS, 1), lambda i: (0, 0)),
        out_shape=jax.ShapeDtypeStruct((N_EXPERTS, 1), F32),
        compiler_params=_cparams(("arbitrary",)),
        name="expert_counts",
    )(eid_t)


def _expert_slots(eid_t, pstart):
    m = eid_t.shape[1]
    tm = min(ROUTE_TILE, m)
    return pl.pallas_call(
        _slot_kernel,
        grid=(m // tm,),
        in_specs=[pl.BlockSpec((TOP_K, tm), lambda i: (0, i)),
                  pl.BlockSpec((N_EXPERTS, 1), lambda i: (0, 0))],
        out_specs=pl.BlockSpec((TOP_K, tm), lambda i: (0, i)),
        out_shape=jax.ShapeDtypeStruct((TOP_K, m), I32),
        scratch_shapes=[pltpu.VMEM((N_EXPERTS, 1), F32)],
        compiler_params=_cparams(("arbitrary",)),
        name="expert_slots",
    )(eid_t, pstart)


def _row_slab(ref, row, slab):
    return ref.at[pl.ds(pl.multiple_of(row * slab, slab), slab), :]


def _dispatch_kernel(dest_ref, pad_row_ref, pad_len_ref, tail_ref, h_ref, xb_ref, zero_ref, sem, zero_sem):
    tm = DISPATCH_TILE
    slab = h_ref.shape[0] // tm
    rb = MOE_ROWS
    step = pl.program_id(0)
    t0 = step * tm

    def zero_rows(row, n_rows):
        return pltpu.make_async_copy(zero_ref.at[pl.ds(0, n_rows * slab), :],
                                     xb_ref.at[pl.ds(pl.multiple_of(row * slab, slab), n_rows * slab), :], zero_sem)

    def zero_fill(op):
        def pad_run(e, carry):
            row = pad_row_ref[e]
            length = pad_len_ref[e]
            piece = rb // 2
            while piece >= 1:
                take = (length & piece) != 0

                @pl.when(take)
                def _(row=row, piece=piece):
                    op(zero_rows(row, piece))
                row = row + jnp.where(take, piece, 0)
                piece //= 2
            return carry
        lax.fori_loop(0, pad_len_ref.shape[0], pad_run, 0)

        def tail_block(tb, carry):
            op(zero_rows(tail_ref[0] + tb * rb, rb))
            return carry
        lax.fori_loop(0, tail_ref[1], tail_block, 0)

    @pl.when(step == 0)
    def _():
        zero_ref[...] = jnp.zeros_like(zero_ref)
        zero_fill(lambda c: c.start())

    def copy(n, k):
        return pltpu.make_async_copy(_row_slab(h_ref, n, slab),
                                     _row_slab(xb_ref, dest_ref[TOP_K * (t0 + n) + k], slab), sem)

    def start(n, carry):
        for k in range(TOP_K):
            copy(n, k).start()
        return carry

    lax.fori_loop(0, tm, start, 0, unroll=8)
    for k in range(TOP_K):
        pltpu.make_async_copy(h_ref, xb_ref.at[pl.ds(0, h_ref.shape[0]), :], sem).wait()

    @pl.when(step == pl.num_programs(0) - 1)
    def _():
        zero_fill(lambda c: c.wait())


DISPATCH_TILE = 512


def _dispatch(dest_flat, pad_row, pad_len, tail, h_rows, m, n_rows):
    slab = h_rows.shape[0] // m
    assert m % DISPATCH_TILE == 0
    grid_spec = pltpu.PrefetchScalarGridSpec(
        num_scalar_prefetch=4,
        grid=(m // DISPATCH_TILE,),
        in_specs=[pl.BlockSpec((DISPATCH_TILE * slab, LANES), lambda i, *_: (i, 0))],
        out_specs=pl.BlockSpec(memory_space=pl.ANY),
        scratch_shapes=[pltpu.VMEM((MOE_ROWS * slab, LANES), F32), pltpu.SemaphoreType.DMA(()),
                        pltpu.SemaphoreType.DMA(())],
    )
    return pl.pallas_call(
        _dispatch_kernel,
        grid_spec=grid_spec,
        out_shape=jax.ShapeDtypeStruct((n_rows * slab, LANES), F32),
        compiler_params=pltpu.CompilerParams(dimension_semantics=("arbitrary",), has_side_effects=True),
        name="moe_dispatch",
    )(dest_flat, pad_row, pad_len, tail, h_rows)


def _expert_kernel(item_e_ref, item_row_ref, item_nsub_ref, wg_hbm, wu_hbm, wd_hbm, xb_ref, yb_ref,
                   wg_buf, wu_buf, wd_buf, xin_ref, yout_ref, x_ref, acc_ref, wgu_ref, wdb_ref, sem_w, sem_in, sem_out):
    it = pl.program_id(0)
    f = pl.program_id(1)
    n_items = pl.num_programs(0)
    nf = pl.num_programs(1)
    kc = wg_buf.shape[1]
    ff = wg_buf.shape[2]
    fc = wd_buf.shape[1]
    nsub = item_nsub_ref[it]
    row0 = item_row_ref[it]
    nxt = jnp.minimum(it + 1, n_items - 1)
    nsub_next = jnp.where(it + 1 < n_items, item_nsub_ref[nxt], 0)
    prv = jnp.maximum(it - 1, 0)
    nsub_prev = jnp.where(it > 0, item_nsub_ref[prv], 0)
    rb = MOE_ROWS
    slab = x_ref.shape[0] * kc // LANES
    blk = rb * slab

    def stage_rows(ref, s):
        return ref.at[pl.ds(pl.multiple_of(s * blk, blk), blk), :]

    def hbm_rows(ref, item_row, s):
        return ref.at[pl.ds(pl.multiple_of((item_row + s * rb) * slab, blk), blk), :]

    def in_copy(item_row, s):
        return pltpu.make_async_copy(hbm_rows(xb_ref, item_row, s), stage_rows(xin_ref, s), sem_in)

    def out_copy(item_row, s):
        return pltpu.make_async_copy(stage_rows(yout_ref, s), hbm_rows(yb_ref, item_row, s), sem_out)

    def loop(n, fn):
        def body(s, carry):
            fn(s)
            return carry
        lax.fori_loop(0, n, body, 0)

    def weight_copies(step):
        item = step // nf
        k = step % nf
        e = item_e_ref[jnp.minimum(item, n_items - 1)]
        slot = step % MOE_WEIGHT_SLOTS
        return (pltpu.make_async_copy(wg_hbm.at[e, pl.ds(pl.multiple_of(k * kc, kc), kc), :], wg_buf.at[slot],
                                      sem_w.at[slot]),
                pltpu.make_async_copy(wu_hbm.at[e, pl.ds(pl.multiple_of(k * kc, kc), kc), :], wu_buf.at[slot],
                                      sem_w.at[slot]),
                pltpu.make_async_copy(wd_hbm.at[e, pl.ds(pl.multiple_of(k * fc, SUBLANES), fc), :], wd_buf.at[slot],
                                      sem_w.at[slot]))

    def fetch_weights(step):
        item = step // nf
        exists = (item < n_items) & (item_nsub_ref[jnp.minimum(item, n_items - 1)] > 0)

        @pl.when(exists)
        def _():
            for c in weight_copies(step):
                c.start()

    @pl.when(nsub > 0)
    def _():
        this_step = it * nf + f

        @pl.when(this_step == 0)
        def _():
            for ahead in range(MOE_WEIGHT_SLOTS - 1):
                fetch_weights(ahead)
        fetch_weights(this_step + MOE_WEIGHT_SLOTS - 1)

        @pl.when(f == 0)
        def _():
            @pl.when(it == 0)
            def _():
                x_ref[...] = jnp.zeros_like(x_ref)
                loop(nsub, lambda s: in_copy(row0, s).start())
            loop(nsub, lambda s: in_copy(row0, s).wait())

            def to_matrix(s):
                rows = pl.ds(pl.multiple_of(s * rb, rb), rb)
                base = pl.multiple_of(s * blk, blk)
                for j in range(slab):
                    c0 = (j * LANES) % kc
                    x_ref[(j * LANES) // kc, rows, c0:c0 + LANES] = (
                        xin_ref[pl.ds(base + j, rb, stride=slab), :].astype(BF16))
            loop(nsub, to_matrix)
            loop(nsub_next, lambda s: in_copy(item_row_ref[nxt], s).start())

        for c in weight_copies(this_step):
            c.wait()
        slot = this_step % MOE_WEIGHT_SLOTS
        wgu_ref[:, :ff] = wg_buf[slot].astype(BF16)
        wgu_ref[:, ff:] = wu_buf[slot].astype(BF16)
        wdb_ref[pl.ds(pl.multiple_of(f * fc, 2 * SUBLANES), fc), :] = wd_buf[slot].astype(BF16)

        def gate_up(start, size):
            rows = pl.ds(pl.multiple_of(start, rb), size)
            part = jnp.dot(x_ref[f, rows, :], wgu_ref[...], preferred_element_type=F32)
            acc_ref[rows, :] = jnp.where(f > 0, acc_ref[rows, :], 0.0) + part

        gate_up(0, 2 * rb)

        def pair(p, carry):
            gate_up(p * (2 * rb), 2 * rb)
            return carry
        lax.fori_loop(1, nsub // 2, pair, 0)

        @pl.when((nsub % 2 == 1) & (nsub > 1))
        def _():
            gate_up((nsub - 1) * rb, rb)

        @pl.when(f == nf - 1)
        def _():
            loop(nsub_prev, lambda s: out_copy(item_row_ref[prv], s).wait())

            def down(start, size):
                rows = pl.ds(pl.multiple_of(start, rb), size)
                gate = acc_ref[rows, :ff]
                act = (gate * _sigmoid(gate) * acc_ref[rows, ff:]).astype(BF16)
                y = jnp.dot(act, wdb_ref[...], preferred_element_type=F32)
                base = pl.multiple_of(start * slab, blk)
                for j in range(slab):
                    yout_ref[pl.ds(base + j, size, stride=slab), :] = y[:, j * LANES:(j + 1) * LANES]

            down(0, 2 * rb)

            def down_pair(p, carry):
                down(p * (2 * rb), 2 * rb)
                return carry
            lax.fori_loop(1, nsub // 2, down_pair, 0)

            @pl.when((nsub % 2 == 1) & (nsub > 1))
            def _():
                down((nsub - 1) * rb, rb)

            loop(nsub, lambda s: out_copy(row0, s).start())

            @pl.when(nsub_next == 0)
            def _():
                loop(nsub, lambda s: out_copy(row0, s).wait())


def _experts(item_e, item_row, item_nsub, w_gate, w_up, w_down, xb):
    d = w_gate.shape[1]
    ff = w_gate.shape[2]
    nk = MOE_K_STEPS
    kc = d // nk
    fc = ff // nk
    assert d % nk == 0 and ff % nk == 0 and kc % LANES == 0 and fc % (2 * SUBLANES) == 0 and ff % LANES == 0
    n_items = item_e.shape[0]
    rows_max = MOE_ROWS * MOE_SUBS_PER_ITEM
    slab = d // LANES

    grid_spec = pltpu.PrefetchScalarGridSpec(
        num_scalar_prefetch=3,
        grid=(n_items, nk),
        in_specs=[pl.BlockSpec(memory_space=pl.ANY)] * 4,
        out_specs=pl.BlockSpec(memory_space=pl.ANY),
        scratch_shapes=[
            pltpu.VMEM((MOE_WEIGHT_SLOTS, kc, ff), F32),
            pltpu.VMEM((MOE_WEIGHT_SLOTS, kc, ff), F32),
            pltpu.VMEM((MOE_WEIGHT_SLOTS, fc, d), F32),
            pltpu.VMEM((rows_max * slab, LANES), F32),
            pltpu.VMEM((rows_max * slab, LANES), F32),
            pltpu.VMEM((nk, rows_max, kc), BF16),
            pltpu.VMEM((rows_max, 2 * ff), F32),
            pltpu.VMEM((kc, 2 * ff), BF16),
            pltpu.VMEM((ff, d), BF16),
            pltpu.SemaphoreType.DMA((MOE_WEIGHT_SLOTS,)),
            pltpu.SemaphoreType.DMA(()),
            pltpu.SemaphoreType.DMA(()),
        ],
    )
    return pl.pallas_call(
        _expert_kernel,
        grid_spec=grid_spec,
        out_shape=jax.ShapeDtypeStruct(xb.shape, F32),
        input_output_aliases={6: 0},
        compiler_params=pltpu.CompilerParams(dimension_semantics=("arbitrary", "arbitrary"),
                                             vmem_limit_bytes=VMEM_LIMIT_BYTES, has_side_effects=True),
        name="moe_experts",
    )(item_e, item_row, item_nsub, w_gate, w_up, w_down, xb)


COMBINE_TILE = 256


def _combine_kernel(alpha, dest_ref, yb_ref, h_ref, gate_ref, g_ref, b_ref, o_ref, buf_ref, sem):
    tm = COMBINE_TILE
    i = pl.program_id(0)
    slab = buf_ref.shape[2] // tm

    def issue(tile, half):
        def start(n, carry):
            for k in range(TOP_K):
                pltpu.make_async_copy(_row_slab(yb_ref, dest_ref[TOP_K * (tile * tm + n) + k], slab),
                                      _row_slab(buf_ref.at[half, k], n, slab), sem.at[half]).start()
            return carry
        lax.fori_loop(0, tm, start, 0, unroll=8)

    @pl.when(i == 0)
    def _():
        issue(0, 0)

    @pl.when(i + 1 < pl.num_programs(0))
    def _():
        issue(i + 1, (i + 1) % 2)

    half = i % 2
    for k in range(TOP_K):
        pltpu.make_async_copy(yb_ref.at[pl.ds(0, buf_ref.shape[2]), :], buf_ref.at[half, k], sem.at[half]).wait()
    gate = gate_ref[...]
    g0 = gate[:, 0:1]
    g1 = gate[:, 1:2]
    col = lambda k, j: buf_ref[half, k, pl.ds(j, tm, stride=slab), :]
    ff = jnp.concatenate([g0 * col(0, j) + g1 * col(1, j) for j in range(slab)], axis=1)
    o_ref[...] = _layer_norm(alpha * h_ref[...] + ff, g_ref[...], b_ref[...])


def _combine(dest_flat, yb, h, gates, ln_g, ln_b, alpha):
    m, d = h.shape
    tm = COMBINE_TILE
    assert m % tm == 0
    grid_spec = pltpu.PrefetchScalarGridSpec(
        num_scalar_prefetch=1,
        grid=(m // tm,),
        in_specs=[pl.BlockSpec(memory_space=pl.ANY),
                  pl.BlockSpec((tm, d), lambda i, dr: (i, 0)),
                  pl.BlockSpec((tm, LANES), lambda i, dr: (i, 0)),
                  pl.BlockSpec((1, d), lambda i, dr: (0, 0)),
                  pl.BlockSpec((1, d), lambda i, dr: (0, 0))],
        out_specs=pl.BlockSpec((tm, d), lambda i, dr: (i, 0)),
        scratch_shapes=[pltpu.VMEM((2, TOP_K, tm * (d // LANES), LANES), F32), pltpu.SemaphoreType.DMA((2,))],
    )
    return pl.pallas_call(
        functools.partial(_combine_kernel, alpha),
        grid_spec=grid_spec,
        out_shape=jax.ShapeDtypeStruct((m, d), F32),
        compiler_params=_cparams(("arbitrary",)),
        name="moe_combine_ln",
    )(dest_flat, yb, h, gates, ln_g.reshape(1, d), ln_b.reshape(1, d))


def _moe_tables(counts):
    nsub_e = (counts + MOE_ROWS - 1) // MOE_ROWS
    pstart = (jnp.cumsum(nsub_e) - nsub_e) * MOE_ROWS
    nitem_e = (nsub_e + MOE_SUBS_PER_ITEM - 1) // MOE_SUBS_PER_ITEM
    item_end = jnp.cumsum(nitem_e)
    return nsub_e, pstart, nitem_e, item_end


def _moe(h, h_rows, eid, gates, w_gate, w_up, w_down, ln_g, ln_b, alpha):
    m, d = h.shape
    n_assign = m * TOP_K
    n_blocks = (n_assign + N_EXPERTS * (MOE_ROWS - 1) + MOE_ROWS - 1) // MOE_ROWS
    n_rows = n_blocks * MOE_ROWS
    n_items = N_EXPERTS + n_assign // (MOE_ROWS * MOE_SUBS_PER_ITEM)

    eid_t = eid[:, :TOP_K].T
    counts = _expert_counts(eid_t)[:, 0].astype(I32)
    nsub_e, pstart, nitem_e, item_end = _moe_tables(counts)
    dest_t = _expert_slots(eid_t, pstart.astype(F32).reshape(N_EXPERTS, 1))
    dest_flat = dest_t.T.reshape(-1)

    it = jnp.arange(n_items, dtype=I32)
    total_items = item_end[-1]
    item_e = jnp.minimum(jnp.sum(item_end[None, :] <= it[:, None], axis=1).astype(I32), N_EXPERTS - 1)
    j = it - (item_end - nitem_e)[item_e]
    used = it < total_items
    last_e = item_e[jnp.maximum(total_items - 1, 0)]
    item_nsub = jnp.where(used, jnp.clip(nsub_e[item_e] - j * MOE_SUBS_PER_ITEM, 0, MOE_SUBS_PER_ITEM), 0).astype(I32)
    item_row = jnp.where(used, pstart[item_e] + j * (MOE_ROWS * MOE_SUBS_PER_ITEM), 0).astype(I32)
    item_e = jnp.where(used, item_e, last_e).astype(I32)

    pad_row = (pstart + counts).astype(I32)
    pad_len = (nsub_e * MOE_ROWS - counts).astype(I32)
    used_rows = jnp.sum(nsub_e) * MOE_ROWS
    tail = jnp.stack([used_rows, (n_rows - used_rows) // MOE_ROWS]).astype(I32)
    xb = _dispatch(dest_flat, pad_row, pad_len, tail, h_rows, m, n_rows)
    yb = _experts(item_e, item_row, item_nsub, w_gate, w_up, w_down, xb)
    return _combine(dest_flat, yb, h, gates, ln_g, ln_b, alpha)


def _pad_cols(w, n):
    return jnp.pad(w, ((0, 0), (0, n - w.shape[1])))


def _pad_rows(w, n):
    return jnp.pad(w, ((0, n - w.shape[0]), (0, 0)))


def kernel(x, w_in, attn_sinks, rw_mu_rkv, rw_mu_wag, rw_w0, rw_w1, rw_w2, rw_a0, rw_a1, rw_a2, rw_g1, rw_g2, rw_k_k, rw_k_a, rw_r_k, rw_lnx_w, rw_lnx_b, p_attn, p_rwkv, w_o, ln1_g, ln1_b, w_group, b_group, w_expert, b_expert, w_gate, w_up, w_down, ln2_g, ln2_b):
    b, t, d = x.shape
    depth = w_in.shape[0]
    m = b * t
    alpha = (2.0 * depth) ** 0.25
    cosb, sinb = _rope_tables(t)
    qkv_w = ATTN_Q_W + 2 * ATTN_KV_W
    rkv_w = 3 * RWKV_W
    h = x
    for l in range(depth):
        hf = h.reshape(m, d)
        hb = hf.astype(BF16)
        qkv = _matmul_cols(hb, w_in[l], 0, qkv_w, F32)
        rkv = _matmul_cols(hb, w_in[l], qkv_w, rkv_w, BF16)
        gates = _matmul_cols(hb, w_in[l], qkv_w + rkv_w, 2 * d, BF16)

        y_a = _attention(qkv.reshape(b, t, qkv_w), attn_sinks[l], cosb, sinb)

        lora = lambda w, n: _pad_cols(w, n).astype(BF16)
        lorb = lambda w, n: _pad_rows(w, n).astype(BF16)
        n_w = -(-rw_w1.shape[2] // LANES) * LANES
        n_a = -(-rw_a1.shape[2] // LANES) * LANES
        n_g = -(-rw_g1.shape[2] // LANES) * LANES
        r_, k_, v_, lw_, cum_, ag_, g_ = _rwkv_prep(
            h, rkv.reshape(b, t, rkv_w), rw_mu_rkv[l], rw_mu_wag[l], rw_w0[l],
            lora(rw_w1[l], n_w), lorb(rw_w2[l], n_w), rw_a0[l], lora(rw_a1[l], n_a), lorb(rw_a2[l], n_a),
            lora(rw_g1[l], n_g), lorb(rw_g2[l], n_g))
        y_r = _wkv_scan(r_, k_, v_, lw_, cum_, ag_, g_, rw_k_k[l], rw_k_a[l], rw_r_k[l], rw_lnx_w[l], rw_lnx_b[l])

        merged = _merge(y_a.reshape(m, ATTN_Q_W), y_r.reshape(m, RWKV_W), gates,
                        p_attn[l].astype(BF16), p_rwkv[l].astype(BF16))
        w_router = _pad_cols(jnp.concatenate([w_group[l], w_expert[l]], axis=1), LANES)
        w_router_hi = w_router.astype(BF16)
        w_router = jnp.stack([w_router_hi, (w_router - w_router_hi.astype(F32)).astype(BF16)])
        b_router = _pad_cols(jnp.concatenate([b_group[l], b_expert[l]])[None, :], LANES)
        h1, h1_rows, eid, gate = _outproj_router(merged, hf, w_o[l].astype(BF16), ln1_g[l], ln1_b[l],
                                        w_router, b_router, alpha)
        h2 = _moe(h1, h1_rows, eid, gate, w_gate[l], w_up[l], w_down[l], ln2_g[l], ln2_b[l], alpha)
        h = h2.reshape(b, t, d)
    return h
```

```python
import functools

import jax
import jax.numpy as jnp
from jax import lax
from jax.experimental import pallas as pl
from jax.experimental.pallas import tpu as pltpu

F32 = jnp.float32
BF16 = jnp.bfloat16
I32 = jnp.int32

HEAD_DIM = 64
ATTN_Q_HEADS = 16
ATTN_KV_HEADS = 4
ATTN_GROUP = ATTN_Q_HEADS // ATTN_KV_HEADS
ATTN_Q_W = ATTN_Q_HEADS * HEAD_DIM
ATTN_KV_W = ATTN_KV_HEADS * HEAD_DIM
WINDOW = 128
ROPE_THETA = 10000.0
RWKV_HEADS = 16
RWKV_N = 64
RWKV_W = RWKV_HEADS * RWKV_N
RWKV_GN_EPS = 64e-5
N_GROUPS = 8
EXPERTS_PER_GROUP = 8
N_EXPERTS = N_GROUPS * EXPERTS_PER_GROUP
TOP_K = 2
LN_EPS = 1e-5

LANES = 128
SUBLANES = 8
VMEM_LIMIT_BYTES = 56 * 1024 * 1024

WKV_CHUNK = 64
WKV_CHUNKS_PER_STEP = 4
WKV_HEADS_PER_STEP = 8
MOE_ROWS = 128
MOE_SUBS_PER_ITEM = 4
MOE_K_STEPS = 4
MOE_WEIGHT_SLOTS = 3
ROUTE_TILE = 512


def _cparams(sem, vmem=VMEM_LIMIT_BYTES):
    return pltpu.CompilerParams(dimension_semantics=sem, vmem_limit_bytes=vmem)


def _sigmoid(x):
    return 1.0 / (1.0 + jnp.exp(-x))


def _dot(a, b):
    return jnp.dot(a.astype(BF16), b.astype(BF16), preferred_element_type=F32)


def _dot_nt(a, b):
    return lax.dot_general(a.astype(BF16), b.astype(BF16), (((1,), (1,)), ((), ())),
                           preferred_element_type=F32)


def _layer_norm(t, g, b):
    mu = jnp.mean(t, axis=-1, keepdims=True)
    d = t - mu
    var = jnp.mean(d * d, axis=-1, keepdims=True)
    return d * lax.rsqrt(var + LN_EPS) * g + b


def _matmul_kernel(a_ref, b_ref, o_ref):
    o_ref[...] = jnp.dot(a_ref[...], b_ref[...].astype(BF16), preferred_element_type=F32).astype(o_ref.dtype)


def _matmul_cols(a, b, col0, ncols, out_dtype, tm=2048, tn=512):
    m, k = a.shape
    tm = min(tm, m)
    cb = col0 // tn
    assert col0 % tn == 0 and ncols % tn == 0 and m % tm == 0
    return pl.pallas_call(
        _matmul_kernel,
        grid=(m // tm, ncols // tn),
        in_specs=[pl.BlockSpec((tm, k), lambda i, j: (i, 0)),
                  pl.BlockSpec((k, tn), lambda i, j: (0, j + cb))],
        out_specs=pl.BlockSpec((tm, tn), lambda i, j: (i, j)),
        out_shape=jax.ShapeDtypeStruct((m, ncols), out_dtype),
        compiler_params=_cparams(("parallel", "arbitrary")),
        name="inproj_matmul",
    )(a, b)


def _rope(x, cosb, sinb):
    half = HEAD_DIM // 2
    lane = lax.broadcasted_iota(I32, cosb.shape, 1)
    first_half = (lane % HEAD_DIM) < half
    outs = []
    for g in range(x.shape[1] // LANES):
        xg = x[:, g * LANES:(g + 1) * LANES]
        partner = jnp.where(first_half, pltpu.roll(xg, LANES - half, axis=1), pltpu.roll(xg, half, axis=1))
        outs.append(xg * cosb + partner * sinb)
    return outs


def _attn_kernel(sinks_ref, q_ref, kc_ref, kp_ref, vc_ref, vp_ref, cosc_ref, sinc_ref, cosp_ref, sinp_ref, o_ref):
    blk = pl.program_id(1)
    tq = q_ref.shape[1]
    qg = _rope(q_ref[0], cosc_ref[...], sinc_ref[...])
    kcg = _rope(kc_ref[0], cosc_ref[...], sinc_ref[...])
    kpg = _rope(kp_ref[0], cosp_ref[...], sinp_ref[...])
    vc = vc_ref[0]
    vp = vp_ref[0]

    def head(groups, h):
        g = groups[h // 2]
        return g[:, (h % 2) * HEAD_DIM:(h % 2 + 1) * HEAD_DIM]

    rows = ATTN_GROUP * tq
    qi = lax.broadcasted_iota(I32, (rows, 2 * tq), 0) % tq
    kj = lax.broadcasted_iota(I32, (rows, 2 * tq), 1)
    dist = qi + tq - kj
    valid = (dist >= 0) & (dist < WINDOW) & ((blk > 0) | (kj >= tq))
    rid = lax.broadcasted_iota(I32, (rows, 1), 0) // tq
    scale = HEAD_DIM ** -0.5
    for kvh in range(ATTN_KV_HEADS):
        qh = jnp.concatenate([head(qg, kvh * ATTN_GROUP + g) for g in range(ATTN_GROUP)], axis=0)
        kw = jnp.concatenate([head(kpg, kvh), head(kcg, kvh)], axis=0)
        vw = jnp.concatenate([vp[:, kvh * HEAD_DIM:(kvh + 1) * HEAD_DIM],
                              vc[:, kvh * HEAD_DIM:(kvh + 1) * HEAD_DIM]], axis=0)
        s = _dot_nt(qh, kw) * scale
        s = jnp.where(valid, s, -jnp.inf)
        sink = jnp.zeros((rows, 1), F32)
        for g in range(ATTN_GROUP):
            sink = jnp.where(rid == g, sinks_ref[kvh * ATTN_GROUP + g], sink)
        m = jnp.maximum(jnp.max(s, axis=-1, keepdims=True), sink)
        e = jnp.exp(s - m)
        denom = jnp.sum(e, axis=-1, keepdims=True) + jnp.exp(sink - m)
        o = _dot(e, vw) / denom
        for g in range(ATTN_GROUP):
            hq = kvh * ATTN_GROUP + g
            o_ref[0, :, hq * HEAD_DIM:(hq + 1) * HEAD_DIM] = o[g * tq:(g + 1) * tq].astype(o_ref.dtype)


def _attention(qkv, sinks, cosb, sinb):
    b, t, _ = qkv.shape
    tq = WINDOW
    nb = t // tq
    kcol = ATTN_Q_W // ATTN_KV_W
    prev = lambda i: jnp.maximum(i - 1, 0)
    grid_spec = pltpu.PrefetchScalarGridSpec(
        num_scalar_prefetch=0,
        grid=(b, nb),
        in_specs=[
            pl.BlockSpec(memory_space=pltpu.SMEM),
            pl.BlockSpec((1, tq, ATTN_Q_W), lambda bi, i: (bi, i, 0)),
            pl.BlockSpec((1, tq, ATTN_KV_W), lambda bi, i: (bi, i, kcol)),
            pl.BlockSpec((1, tq, ATTN_KV_W), lambda bi, i: (bi, prev(i), kcol)),
            pl.BlockSpec((1, tq, ATTN_KV_W), lambda bi, i: (bi, i, kcol + 1)),
            pl.BlockSpec((1, tq, ATTN_KV_W), lambda bi, i: (bi, prev(i), kcol + 1)),
            pl.BlockSpec((tq, LANES), lambda bi, i: (i, 0)),
            pl.BlockSpec((tq, LANES), lambda bi, i: (i, 0)),
            pl.BlockSpec((tq, LANES), lambda bi, i: (prev(i), 0)),
            pl.BlockSpec((tq, LANES), lambda bi, i: (prev(i), 0)),
        ],
        out_specs=pl.BlockSpec((1, tq, ATTN_Q_W), lambda bi, i: (bi, i, 0)),
    )
    return pl.pallas_call(
        _attn_kernel,
        grid_spec=grid_spec,
        out_shape=jax.ShapeDtypeStruct((b, t, ATTN_Q_W), BF16),
        compiler_params=_cparams(("parallel", "arbitrary")),
        name="swa_attention",
    )(sinks, qkv, qkv, qkv, qkv, qkv, cosb, sinb, cosb, sinb)


def _rope_tables(t):
    inv = 1.0 / (ROPE_THETA ** (jnp.arange(0, HEAD_DIM, 2, dtype=F32) / HEAD_DIM))
    ang = jnp.arange(t, dtype=F32)[:, None] * inv[None, :]
    cos, sin = jnp.cos(ang), jnp.sin(ang)
    reps = LANES // HEAD_DIM
    cosb = jnp.tile(jnp.concatenate([cos, cos], axis=-1), (1, reps))
    sinb = jnp.tile(jnp.concatenate([-sin, sin], axis=-1), (1, reps))
    return cosb, sinb


def _rwkv_prep_kernel(h_ref, hp_ref, r_ref, k_ref, v_ref, rp_ref, kp_ref, vp_ref,
                      mu_rkv_ref, mu_wag_ref, w0_ref, w1_ref, w2_ref, a0_ref, a1_ref, a2_ref,
                      g1_ref, g2_ref,
                      ro_ref, ko_ref, vo_ref, lwo_ref, cumo_ref, ago_ref, go_ref):
    first = pl.program_id(1) == 0

    def shifted(cur, prev_ref):
        last_row = prev_ref.shape[1] - 1
        prev_row = jnp.where(first, 0.0, prev_ref[0, last_row:last_row + 1, :].astype(F32))
        rowid = lax.broadcasted_iota(I32, cur.shape, 0)
        return jnp.where(rowid == 0, prev_row, pltpu.roll(cur, 1, axis=0))

    h = h_ref[0]
    xx = shifted(h, hp_ref) - h
    xw = h + xx * mu_wag_ref[0:1, :]
    xa = h + xx * mu_wag_ref[1:2, :]
    xg = h + xx * mu_wag_ref[2:3, :]
    w_raw = w0_ref[...] + _dot(jnp.tanh(_dot(xw, w1_ref[...])), w2_ref[...])
    neg = -w_raw
    softplus = jnp.maximum(neg, 0.0) + jnp.log1p(jnp.exp(-jnp.abs(neg)))
    w = -softplus - 0.5
    lw = -jnp.exp(w)
    tm = lw.shape[0]
    row = lax.broadcasted_iota(I32, (tm, tm), 0)
    col = lax.broadcasted_iota(I32, (tm, tm), 1)
    tri = ((row >= col) & (row // WKV_CHUNK == col // WKV_CHUNK)).astype(BF16)
    cum = sum(jnp.dot(tri, piece, preferred_element_type=F32) for piece in _split3(lw))
    ag = _sigmoid(a0_ref[...] + _dot(_dot(xa, a1_ref[...]), a2_ref[...]))
    go_ref[0] = _dot(_sigmoid(_dot(xg, g1_ref[...])), g2_ref[...])

    r = r_ref[0].astype(F32)
    k = k_ref[0].astype(F32)
    v = v_ref[0].astype(F32)
    r = r + (shifted(r, rp_ref) - r) * mu_rkv_ref[0:1, :]
    k = k + (shifted(k, kp_ref) - k) * mu_rkv_ref[1:2, :]
    v = v + (shifted(v, vp_ref) - v) * mu_rkv_ref[2:3, :]
    nchunk = tm // WKV_CHUNK
    for hd in range(RWKV_HEADS):
        sl = slice(hd * RWKV_N, (hd + 1) * RWKV_N)
        for ref, val in ((ro_ref, r), (ko_ref, k), (vo_ref, v), (lwo_ref, lw), (cumo_ref, cum), (ago_ref, ag)):
            ref[0, :, hd] = val[:, sl].reshape(nchunk, WKV_CHUNK, RWKV_N).astype(ref.dtype)


def _rwkv_prep(h, rkv, mu_rkv, mu_wag, w0, w1, w2, a0, a1, a2, g1, g2, tm=256):
    b, t, d = h.shape
    tm = min(tm, t)
    c = RWKV_W
    prevblk = lambda i, rows: jnp.maximum(i * (tm // rows) - 1, 0)
    rkv_rows = SUBLANES * (4 // rkv.dtype.itemsize)
    full = lambda arr: pl.BlockSpec(arr.shape, lambda bi, i: (0,) * arr.ndim)
    row = lambda arr: arr.reshape(1, -1)
    w0, a0 = row(w0), row(a0)
    in_specs = [
        pl.BlockSpec((1, tm, d), lambda bi, i: (bi, i, 0)),
        pl.BlockSpec((1, SUBLANES, d), lambda bi, i: (bi, prevblk(i, SUBLANES), 0)),
        pl.BlockSpec((1, tm, c), lambda bi, i: (bi, i, 0)),
        pl.BlockSpec((1, tm, c), lambda bi, i: (bi, i, 1)),
        pl.BlockSpec((1, tm, c), lambda bi, i: (bi, i, 2)),
        pl.BlockSpec((1, rkv_rows, c), lambda bi, i: (bi, prevblk(i, rkv_rows), 0)),
        pl.BlockSpec((1, rkv_rows, c), lambda bi, i: (bi, prevblk(i, rkv_rows), 1)),
        pl.BlockSpec((1, rkv_rows, c), lambda bi, i: (bi, prevblk(i, rkv_rows), 2)),
    ] + [full(a) for a in (mu_rkv, mu_wag, w0, w1, w2, a0, a1, a2, g1, g2)]
    assert tm % WKV_CHUNK == 0
    hm = lambda dt: jax.ShapeDtypeStruct((b, t // WKV_CHUNK, RWKV_HEADS, WKV_CHUNK, RWKV_N), dt)
    hm_spec = pl.BlockSpec((1, tm // WKV_CHUNK, RWKV_HEADS, WKV_CHUNK, RWKV_N), lambda bi, i: (bi, i, 0, 0, 0))
    return pl.pallas_call(
        _rwkv_prep_kernel,
        grid=(b, t // tm),
        in_specs=in_specs,
        out_specs=[hm_spec] * 6 + [pl.BlockSpec((1, tm, c), lambda bi, i: (bi, i, 0))],
        out_shape=[hm(BF16), hm(BF16), hm(BF16), hm(F32), hm(F32), hm(BF16), jax.ShapeDtypeStruct((b, t, c), F32)],
        compiler_params=_cparams(("parallel", "arbitrary")),
        name="rwkv_prep",
    )(h, h, rkv, rkv, rkv, rkv, rkv, rkv, mu_rkv, mu_wag, w0, w1, w2, a0, a1, a2, g1, g2)


def _split3(x):
    hi = x.astype(BF16)
    r1 = x - hi.astype(F32)
    mid = r1.astype(BF16)
    lo = (r1 - mid.astype(F32)).astype(BF16)
    return hi, mid, lo


def _bmm(a, b):
    return jnp.einsum("gmk,gkn->gmn", a.astype(BF16), b.astype(BF16), preferred_element_type=F32)


def _bmm_nt(a, b):
    return jnp.einsum("gmk,gnk->gmn", a.astype(BF16), b.astype(BF16), preferred_element_type=F32)


def _bmm_tn(a, b):
    return jnp.einsum("gtm,gtn->gmn", a.astype(BF16), b.astype(BF16), preferred_element_type=F32)


def _wkv_kernel(r_ref, k_ref, v_ref, lw_ref, cum_ref, ag_ref, g_ref, kk_ref, ka_ref, rk_ref, lnw_ref, lnb_ref,
                y_ref, s_ref, st_ref):
    c = WKV_CHUNK
    n = RWKV_N
    nc, hb = r_ref.shape[1], r_ref.shape[2]
    tc = nc * c
    g = nc * hb

    @pl.when(pl.program_id(2) == 0)
    def _():
        s_ref[...] = jnp.zeros_like(s_ref)

    chunks = lambda ref: ref[0].reshape(g, c, n).astype(F32)
    r, k, v, lw, cum, ag = (chunks(ref) for ref in (r_ref, k_ref, v_ref, lw_ref, cum_ref, ag_ref))
    per_head = lambda t: t.reshape(nc, hb, c, n)
    kk = (per_head(k) * kk_ref[...]).reshape(g, c, n)
    k = (per_head(k) * (1.0 + (per_head(ag) - 1.0) * ka_ref[...])).reshape(g, c, n)
    kk = kk / jnp.maximum(jnp.sqrt(jnp.sum(kk * kk, axis=-1, keepdims=True)), 1e-12)
    dinc = jnp.exp(cum)
    dinv = jnp.exp(-cum)
    rt = r * dinc
    kt = k * dinv
    at = -kk * jnp.exp(cum - lw)
    bt = kk * ag * dinv
    d_chunk = dinc[:, c - 1:c, :]

    row = lax.broadcasted_iota(I32, (1, c, c), 1)
    col = lax.broadcasted_iota(I32, (1, c, c), 2)
    strict = row > col
    incl = row >= col
    eye = jnp.broadcast_to((row == col).astype(F32), (g, c, c))
    p = _bmm_nt(jnp.concatenate([at, rt], axis=1), jnp.concatenate([bt, kt], axis=1))
    a_ab = jnp.where(strict, p[:, :c, :c], 0.0)
    a_ak = jnp.where(strict, p[:, :c, c:], 0.0)
    a_rb = jnp.where(incl, p[:, c:, :c], 0.0)
    a_rk = jnp.where(incl, p[:, c:, c:], 0.0)
    x = _bmm(a_ab, a_ab)
    tinv = eye + a_ab
    levels = c.bit_length() - 2
    for j in range(1, levels):
        both = _bmm(jnp.concatenate([x, tinv], axis=1), x)
        x = both[:, :c]
        tinv = tinv + both[:, c:]
    tinv = tinv + _bmm(tinv, x)
    z = _bmm(a_ak, v)
    ta = _bmm(tinv, jnp.concatenate([at, z], axis=2))
    ry = _bmm(a_rb, ta)
    rp = rt + ry[:, :, :n]
    yv = ry[:, :, n:] + _bmm(a_rk, v)
    moff = _bmm_tn(ta[:, :, :n], bt).reshape(nc, hb, n, n)
    n2 = _bmm_tn(jnp.concatenate([ta[:, :, n:], v], axis=1),
                 jnp.concatenate([bt, kt], axis=1)).reshape(nc, hb, n, n)
    dch = d_chunk.reshape(nc, hb, 1, n)

    s = s_ref[...]
    for ci in range(nc):
        st_ref[ci] = s
        s = (s + _bmm(s, moff[ci]) + n2[ci]) * dch[ci]
    s_ref[...] = s

    y = _bmm_nt(rp, st_ref[...].reshape(g, n, n)) + yv
    mu = jnp.mean(y, axis=-1, keepdims=True)
    yc = y - mu
    var = jnp.mean(yc * yc, axis=-1, keepdims=True)
    y = per_head(yc * lax.rsqrt(var + RWKV_GN_EPS)) * lnw_ref[...] + lnb_ref[...]
    bonus = jnp.sum(per_head(r * k) * rk_ref[...], axis=-1, keepdims=True)
    y = y + bonus * per_head(v)
    y = jnp.concatenate([jnp.concatenate([y[ci, hd] for hd in range(hb)], axis=1) for ci in range(nc)], axis=0)
    y_ref[0] = (y * g_ref[0]).astype(y_ref.dtype)


def _wkv_scan(r, k, v, lw, cum, ag, g, k_k, k_a, r_k, lnx_w, lnx_b):
    b, nchunks, hh, c, n = r.shape
    t = nchunks * c
    nc = min(WKV_CHUNKS_PER_STEP, nchunks)
    tc = nc * c
    hb = WKV_HEADS_PER_STEP
    blk = pl.BlockSpec((1, nc, hb, c, n), lambda bi, hi, i: (bi, i, hi, 0, 0))
    tok = pl.BlockSpec((1, tc, hb * n), lambda bi, hi, i: (bi, i, hi))
    par = pl.BlockSpec((hb, 1, n), lambda bi, hi, i: (hi, 0, 0))
    per_head = lambda arr: arr.reshape(hh, 1, n)
    return pl.pallas_call(
        _wkv_kernel,
        grid=(b, hh // hb, t // tc),
        in_specs=[blk] * 6 + [tok] + [par] * 5,
        out_specs=tok,
        out_shape=jax.ShapeDtypeStruct((b, t, hh * n), BF16),
        scratch_shapes=[pltpu.VMEM((hb, n, n), F32), pltpu.VMEM((nc, hb, n, n), F32)],
        compiler_params=_cparams(("parallel", "parallel", "arbitrary")),
        name="wkv7_scan",
    )(r, k, v, lw, cum, ag, g, per_head(k_k), per_head(k_a), per_head(r_k), per_head(lnx_w), per_head(lnx_b))


def _merge_kernel(ya_ref, yr_ref, ga_ref, gr_ref, pa_ref, pr_ref, o_ref):
    ma = jnp.dot(ya_ref[...], pa_ref[...], preferred_element_type=F32)
    mr = jnp.dot(yr_ref[...], pr_ref[...], preferred_element_type=F32)
    o_ref[...] = (_sigmoid(ga_ref[...].astype(F32)) * ma + _sigmoid(gr_ref[...].astype(F32)) * mr).astype(o_ref.dtype)


def _merge(ya, yr, gates, p_attn, p_rwkv, tm=512):
    m, c = ya.shape
    d = p_attn.shape[1]
    tm = min(tm, m)
    tile = pl.BlockSpec((tm, c), lambda i: (i, 0))
    return pl.pallas_call(
        _merge_kernel,
        grid=(m // tm,),
        in_specs=[tile, tile,
                  pl.BlockSpec((tm, d), lambda i: (i, 0)),
                  pl.BlockSpec((tm, d), lambda i: (i, 1)),
                  pl.BlockSpec((c, d), lambda i: (0, 0)),
                  pl.BlockSpec((c, d), lambda i: (0, 0))],
        out_specs=pl.BlockSpec((tm, d), lambda i: (i, 0)),
        out_shape=jax.ShapeDtypeStruct((m, d), BF16),
        compiler_params=_cparams(("parallel",)),
        name="gated_merge",
    )(ya, yr, gates, gates, p_attn, p_rwkv)


def _outproj_router_kernel(alpha, mg_ref, x_ref, wo_ref, g_ref, b_ref, wr_ref, br_ref,
                           h_ref, hrow_ref, eid_ref, gate_ref):
    mix = jnp.dot(mg_ref[...], wo_ref[...], preferred_element_type=F32)
    h = _layer_norm(alpha * x_ref[...] + mix, g_ref[...], b_ref[...])
    h_ref[...] = h
    slab = h.shape[1] // LANES
    for j in range(slab):
        hrow_ref[pl.ds(j, h.shape[0], stride=slab), :] = h[:, j * LANES:(j + 1) * LANES]
    h_hi = h.astype(BF16)
    h_lo = (h - h_hi.astype(F32)).astype(BF16)
    logits = (jnp.dot(h_hi, wr_ref[0], preferred_element_type=F32)
              + jnp.dot(h_lo, wr_ref[0], preferred_element_type=F32)
              + jnp.dot(h_hi, wr_ref[1], preferred_element_type=F32)) + br_ref[...]
    lane = lax.broadcasted_iota(I32, logits.shape, 1)
    ninf = -jnp.inf
    big = jnp.int32(2 * LANES)
    glog = jnp.where(lane < N_GROUPS, logits, ninf)
    gmax = jnp.max(glog, axis=-1, keepdims=True)
    gidx = jnp.min(jnp.where(glog == gmax, lane, big), axis=-1, keepdims=True)
    gtop = 1.0 / jnp.sum(jnp.exp(glog - gmax), axis=-1, keepdims=True)
    eg = (lane - N_GROUPS) // EXPERTS_PER_GROUP
    in_group = (lane >= N_GROUPS) & (lane < N_GROUPS + N_EXPERTS) & (eg == gidx)
    el = jnp.where(in_group, logits, ninf)
    m1 = jnp.max(el, axis=-1, keepdims=True)
    i1 = jnp.min(jnp.where(el == m1, lane, big), axis=-1, keepdims=True)
    el2 = jnp.where(lane == i1, ninf, el)
    m2 = jnp.max(el2, axis=-1, keepdims=True)
    i2 = jnp.min(jnp.where(el2 == m2, lane, big), axis=-1, keepdims=True)
    t = jnp.exp(m2 - m1)
    p1 = 1.0 / (1.0 + t)
    p2 = t / (1.0 + t)
    eid_ref[...] = jnp.where(lane == 0, i1 - N_GROUPS, jnp.where(lane == 1, i2 - N_GROUPS, 0))
    gate_ref[...] = jnp.where(lane == 0, gtop * p1, jnp.where(lane == 1, gtop * p2, 0.0))


def _outproj_router(merged, x, w_o, ln_g, ln_b, w_router, b_router, alpha, tm=512):
    m, d = x.shape
    tm = min(tm, m)
    tile = pl.BlockSpec((tm, d), lambda i: (i, 0))
    vec = pl.BlockSpec((1, d), lambda i: (0, 0))
    small = pl.BlockSpec((tm, LANES), lambda i: (i, 0))
    return pl.pallas_call(
        functools.partial(_outproj_router_kernel, alpha),
        grid=(m // tm,),
        in_specs=[tile, tile, pl.BlockSpec((d, d), lambda i: (0, 0)), vec, vec,
                  pl.BlockSpec((2, d, LANES), lambda i: (0, 0, 0)), pl.BlockSpec((1, LANES), lambda i: (0, 0))],
        out_specs=[tile, pl.BlockSpec((tm * (d // LANES), LANES), lambda i: (i, 0)), small, small],
        out_shape=[jax.ShapeDtypeStruct((m, d), F32), jax.ShapeDtypeStruct((m * (d // LANES), LANES), F32),
                   jax.ShapeDtypeStruct((m, LANES), I32), jax.ShapeDtypeStruct((m, LANES), F32)],
        compiler_params=_cparams(("parallel",)),
        name="outproj_ln_router",
    )(merged, x, w_o, ln_g.reshape(1, d), ln_b.reshape(1, d), w_router, b_router)


def _onehots(eid_ref):
    tm = eid_ref.shape[1]
    e_iota = lax.broadcasted_iota(I32, (N_EXPERTS, tm), 0)
    oh0 = (eid_ref[0:1, :] == e_iota).astype(F32)
    oh1 = (eid_ref[1:2, :] == e_iota).astype(F32)
    return oh0, oh1


def _count_kernel(eid_ref, cnt_ref):
    @pl.when(pl.program_id(0) == 0)
    def _():
        cnt_ref[...] = jnp.zeros_like(cnt_ref)

    oh0, oh1 = _onehots(eid_ref)
    cnt_ref[...] += jnp.sum(oh0 + oh1, axis=1, keepdims=True)


def _slot_kernel(eid_ref, pstart_ref, dest_ref, run_ref):
    @pl.when(pl.program_id(0) == 0)
    def _():
        run_ref[...] = jnp.zeros_like(run_ref)

    tm = eid_ref.shape[1]
    oh0, oh1 = _onehots(eid_ref)
    both = oh0 + oh1
    earlier = (lax.broadcasted_iota(I32, (tm, tm), 0) < lax.broadcasted_iota(I32, (tm, tm), 1)).astype(BF16)
    pre = jnp.dot(both.astype(BF16), earlier, preferred_element_type=F32)
    base = pre + run_ref[...] + pstart_ref[...]
    dest_ref[0:1, :] = jnp.sum(oh0 * base, axis=0, keepdims=True).astype(I32)
    dest_ref[1:2, :] = jnp.sum(oh1 * base, axis=0, keepdims=True).astype(I32)
    run_ref[...] += jnp.sum(both, axis=1, keepdims=True)


def _expert_counts(eid_t):
    m = eid_t.shape[1]
    tm = min(ROUTE_TILE, m)
    return pl.pallas_call(
        _count_kernel,
        grid=(m // tm,),
        in_specs=[pl.BlockSpec((TOP_K, tm), lambda i: (0, i))],
        out_specs=pl.BlockSpec((N_EXPERTS, 1), lambda i: (0, 0)),
        out_shape=jax.ShapeDtypeStruct((N_EXPERTS, 1), F32),
        compiler_params=_cparams(("arbitrary",)),
        name="expert_counts",
    )(eid_t)


def _expert_slots(eid_t, pstart):
    m = eid_t.shape[1]
    tm = min(ROUTE_TILE, m)
    return pl.pallas_call(
        _slot_kernel,
        grid=(m // tm,),
        in_specs=[pl.BlockSpec((TOP_K, tm), lambda i: (0, i)),
                  pl.BlockSpec((N_EXPERTS, 1), lambda i: (0, 0))],
        out_specs=pl.BlockSpec((TOP_K, tm), lambda i: (0, i)),
        out_shape=jax.ShapeDtypeStruct((TOP_K, m), I32),
        scratch_shapes=[pltpu.VMEM((N_EXPERTS, 1), F32)],
        compiler_params=_cparams(("arbitrary",)),
        name="expert_slots",
    )(eid_t, pstart)


def _row_slab(ref, row, slab):
    return ref.at[pl.ds(pl.multiple_of(row * slab, slab), slab), :]


def _dispatch_kernel(dest_ref, pad_row_ref, pad_len_ref, tail_ref, h_ref, xb_ref, zero_ref, sem, zero_sem):
    tm = DISPATCH_TILE
    slab = h_ref.shape[0] // tm
    rb = MOE_ROWS
    step = pl.program_id(0)
    t0 = step * tm

    def zero_rows(row, n_rows):
        return pltpu.make_async_copy(zero_ref.at[pl.ds(0, n_rows * slab), :],
                                     xb_ref.at[pl.ds(pl.multiple_of(row * slab, slab), n_rows * slab), :], zero_sem)

    def zero_fill(op):
        def pad_run(e, carry):
            row = pad_row_ref[e]
            length = pad_len_ref[e]
            piece = rb // 2
            while piece >= 1:
                take = (length & piece) != 0

                @pl.when(take)
                def _(row=row, piece=piece):
                    op(zero_rows(row, piece))
                row = row + jnp.where(take, piece, 0)
                piece //= 2
            return carry
        lax.fori_loop(0, pad_len_ref.shape[0], pad_run, 0)

        def tail_block(tb, carry):
            op(zero_rows(tail_ref[0] + tb * rb, rb))
            return carry
        lax.fori_loop(0, tail_ref[1], tail_block, 0)

    @pl.when(step == 0)
    def _():
        zero_ref[...] = jnp.zeros_like(zero_ref)
        zero_fill(lambda c: c.start())

    def copy(n, k):
        return pltpu.make_async_copy(_row_slab(h_ref, n, slab),
                                     _row_slab(xb_ref, dest_ref[TOP_K * (t0 + n) + k], slab), sem)

    def start(n, carry):
        for k in range(TOP_K):
            copy(n, k).start()
        return carry

    lax.fori_loop(0, tm, start, 0, unroll=8)
    for k in range(TOP_K):
        pltpu.make_async_copy(h_ref, xb_ref.at[pl.ds(0, h_ref.shape[0]), :], sem).wait()

    @pl.when(step == pl.num_programs(0) - 1)
    def _():
        zero_fill(lambda c: c.wait())


DISPATCH_TILE = 1024


def _dispatch(dest_flat, pad_row, pad_len, tail, h_rows, m, n_rows):
    slab = h_rows.shape[0] // m
    assert m % DISPATCH_TILE == 0
    grid_spec = pltpu.PrefetchScalarGridSpec(
        num_scalar_prefetch=4,
        grid=(m // DISPATCH_TILE,),
        in_specs=[pl.BlockSpec((DISPATCH_TILE * slab, LANES), lambda i, *_: (i, 0))],
        out_specs=pl.BlockSpec(memory_space=pl.ANY),
        scratch_shapes=[pltpu.VMEM((MOE_ROWS * slab, LANES), F32), pltpu.SemaphoreType.DMA(()),
                        pltpu.SemaphoreType.DMA(())],
    )
    return pl.pallas_call(
        _dispatch_kernel,
        grid_spec=grid_spec,
        out_shape=jax.ShapeDtypeStruct((n_rows * slab, LANES), F32),
        compiler_params=pltpu.CompilerParams(dimension_semantics=("arbitrary",), has_side_effects=True),
        name="moe_dispatch",
    )(dest_flat, pad_row, pad_len, tail, h_rows)


def _expert_kernel(item_e_ref, item_row_ref, item_nsub_ref, wg_hbm, wu_hbm, wd_hbm, xb_ref, yb_ref,
                   wg_buf, wu_buf, wd_buf, xin_ref, yout_ref, x_ref, acc_ref, wgu_ref, wdb_ref, sem_w, sem_in, sem_out):
    it = pl.program_id(0)
    f = pl.program_id(1)
    n_items = pl.num_programs(0)
    nf = pl.num_programs(1)
    kc = wg_buf.shape[1]
    ff = wg_buf.shape[2]
    fc = wd_buf.shape[1]
    nsub = item_nsub_ref[it]
    row0 = item_row_ref[it]
    nxt = jnp.minimum(it + 1, n_items - 1)
    nsub_next = jnp.where(it + 1 < n_items, item_nsub_ref[nxt], 0)
    prv = jnp.maximum(it - 1, 0)
    nsub_prev = jnp.where(it > 0, item_nsub_ref[prv], 0)
    rb = MOE_ROWS
    slab = x_ref.shape[0] * kc // LANES
    blk = rb * slab

    def stage_rows(ref, s):
        return ref.at[pl.ds(pl.multiple_of(s * blk, blk), blk), :]

    def hbm_rows(ref, item_row, s):
        return ref.at[pl.ds(pl.multiple_of((item_row + s * rb) * slab, blk), blk), :]

    def in_copy(item_row, s):
        return pltpu.make_async_copy(hbm_rows(xb_ref, item_row, s), stage_rows(xin_ref, s), sem_in)

    def out_copy(item_row, s):
        return pltpu.make_async_copy(stage_rows(yout_ref, s), hbm_rows(yb_ref, item_row, s), sem_out)

    def loop(n, fn):
        def body(s, carry):
            fn(s)
            return carry
        lax.fori_loop(0, n, body, 0)

    def weight_copies(step):
        item = step // nf
        k = step % nf
        e = item_e_ref[jnp.minimum(item, n_items - 1)]
        slot = step % MOE_WEIGHT_SLOTS
        return (pltpu.make_async_copy(wg_hbm.at[e, pl.ds(pl.multiple_of(k * kc, kc), kc), :], wg_buf.at[slot],
                                      sem_w.at[slot]),
                pltpu.make_async_copy(wu_hbm.at[e, pl.ds(pl.multiple_of(k * kc, kc), kc), :], wu_buf.at[slot],
                                      sem_w.at[slot]),
                pltpu.make_async_copy(wd_hbm.at[e, pl.ds(pl.multiple_of(k * fc, SUBLANES), fc), :], wd_buf.at[slot],
                                      sem_w.at[slot]))

    def fetch_weights(step):
        item = step // nf
        exists = (item < n_items) & (item_nsub_ref[jnp.minimum(item, n_items - 1)] > 0)

        @pl.when(exists)
        def _():
            for c in weight_copies(step):
                c.start()

    @pl.when(nsub > 0)
    def _():
        this_step = it * nf + f

        @pl.when(this_step == 0)
        def _():
            for ahead in range(MOE_WEIGHT_SLOTS - 1):
                fetch_weights(ahead)
        fetch_weights(this_step + MOE_WEIGHT_SLOTS - 1)

        @pl.when(f == 0)
        def _():
            @pl.when(it == 0)
            def _():
                x_ref[...] = jnp.zeros_like(x_ref)
                loop(nsub, lambda s: in_copy(row0, s).start())
            loop(nsub, lambda s: in_copy(row0, s).wait())

            def to_matrix(s):
                rows = pl.ds(pl.multiple_of(s * rb, rb), rb)
                base = pl.multiple_of(s * blk, blk)
                for j in range(slab):
                    c0 = (j * LANES) % kc
                    x_ref[(j * LANES) // kc, rows, c0:c0 + LANES] = (
                        xin_ref[pl.ds(base + j, rb, stride=slab), :].astype(BF16))
            loop(nsub, to_matrix)
            loop(nsub_next, lambda s: in_copy(item_row_ref[nxt], s).start())

        for c in weight_copies(this_step):
            c.wait()
        slot = this_step % MOE_WEIGHT_SLOTS
        wgu_ref[:, :ff] = wg_buf[slot].astype(BF16)
        wgu_ref[:, ff:] = wu_buf[slot].astype(BF16)
        wdb_ref[pl.ds(pl.multiple_of(f * fc, 2 * SUBLANES), fc), :] = wd_buf[slot].astype(BF16)

        def gate_up(start, size):
            rows = pl.ds(pl.multiple_of(start, rb), size)
            part = jnp.dot(x_ref[f, rows, :], wgu_ref[...], preferred_element_type=F32)
            acc_ref[rows, :] = jnp.where(f > 0, acc_ref[rows, :], 0.0) + part

        gate_up(0, 2 * rb)

        def pair(p, carry):
            gate_up(p * (2 * rb), 2 * rb)
            return carry
        lax.fori_loop(1, nsub // 2, pair, 0)

        @pl.when((nsub % 2 == 1) & (nsub > 1))
        def _():
            gate_up((nsub - 1) * rb, rb)

        @pl.when(f == nf - 1)
        def _():
            loop(nsub_prev, lambda s: out_copy(item_row_ref[prv], s).wait())

            def down(start, size):
                rows = pl.ds(pl.multiple_of(start, rb), size)
                gate = acc_ref[rows, :ff]
                act = (gate * _sigmoid(gate) * acc_ref[rows, ff:]).astype(BF16)
                y = jnp.dot(act, wdb_ref[...], preferred_element_type=F32)
                base = pl.multiple_of(start * slab, blk)
                for j in range(slab):
                    yout_ref[pl.ds(base + j, size, stride=slab), :] = y[:, j * LANES:(j + 1) * LANES]

            down(0, 2 * rb)

            def down_pair(p, carry):
                down(p * (2 * rb), 2 * rb)
                return carry
            lax.fori_loop(1, nsub // 2, down_pair, 0)

            @pl.when((nsub % 2 == 1) & (nsub > 1))
            def _():
                down((nsub - 1) * rb, rb)

            loop(nsub, lambda s: out_copy(row0, s).start())

            @pl.when(nsub_next == 0)
            def _():
                loop(nsub, lambda s: out_copy(row0, s).wait())


def _experts(item_e, item_row, item_nsub, w_gate, w_up, w_down, xb):
    d = w_gate.shape[1]
    ff = w_gate.shape[2]
    nk = MOE_K_STEPS
    kc = d // nk
    fc = ff // nk
    assert d % nk == 0 and ff % nk == 0 and kc % LANES == 0 and fc % (2 * SUBLANES) == 0 and ff % LANES == 0
    n_items = item_e.shape[0]
    rows_max = MOE_ROWS * MOE_SUBS_PER_ITEM
    slab = d // LANES

    grid_spec = pltpu.PrefetchScalarGridSpec(
        num_scalar_prefetch=3,
        grid=(n_items, nk),
        in_specs=[pl.BlockSpec(memory_space=pl.ANY)] * 4,
        out_specs=pl.BlockSpec(memory_space=pl.ANY),
        scratch_shapes=[
            pltpu.VMEM((MOE_WEIGHT_SLOTS, kc, ff), F32),
            pltpu.VMEM((MOE_WEIGHT_SLOTS, kc, ff), F32),
            pltpu.VMEM((MOE_WEIGHT_SLOTS, fc, d), F32),
            pltpu.VMEM((rows_max * slab, LANES), F32),
            pltpu.VMEM((rows_max * slab, LANES), F32),
            pltpu.VMEM((nk, rows_max, kc), BF16),
            pltpu.VMEM((rows_max, 2 * ff), F32),
            pltpu.VMEM((kc, 2 * ff), BF16),
            pltpu.VMEM((ff, d), BF16),
            pltpu.SemaphoreType.DMA((MOE_WEIGHT_SLOTS,)),
            pltpu.SemaphoreType.DMA(()),
            pltpu.SemaphoreType.DMA(()),
        ],
    )
    return pl.pallas_call(
        _expert_kernel,
        grid_spec=grid_spec,
        out_shape=jax.ShapeDtypeStruct(xb.shape, F32),
        input_output_aliases={6: 0},
        compiler_params=pltpu.CompilerParams(dimension_semantics=("arbitrary", "arbitrary"),
                                             vmem_limit_bytes=VMEM_LIMIT_BYTES, has_side_effects=True),
        name="moe_experts",
    )(item_e, item_row, item_nsub, w_gate, w_up, w_down, xb)


COMBINE_TILE = 512


def _combine_kernel(alpha, dest_ref, yb_ref, h_ref, gate_ref, g_ref, b_ref, o_ref, buf_ref, sem):
    tm = COMBINE_TILE
    i = pl.program_id(0)
    slab = buf_ref.shape[2] // tm

    def issue(tile, half):
        def start(n, carry):
            for k in range(TOP_K):
                pltpu.make_async_copy(_row_slab(yb_ref, dest_ref[TOP_K * (tile * tm + n) + k], slab),
                                      _row_slab(buf_ref.at[half, k], n, slab), sem.at[half]).start()
            return carry
        lax.fori_loop(0, tm, start, 0, unroll=8)

    @pl.when(i == 0)
    def _():
        issue(0, 0)

    @pl.when(i + 1 < pl.num_programs(0))
    def _():
        issue(i + 1, (i + 1) % 2)

    half = i % 2
    for k in range(TOP_K):
        pltpu.make_async_copy(yb_ref.at[pl.ds(0, buf_ref.shape[2]), :], buf_ref.at[half, k], sem.at[half]).wait()
    gate = gate_ref[...]
    g0 = gate[:, 0:1]
    g1 = gate[:, 1:2]
    col = lambda k, j: buf_ref[half, k, pl.ds(j, tm, stride=slab), :]
    ff = jnp.concatenate([g0 * col(0, j) + g1 * col(1, j) for j in range(slab)], axis=1)
    o_ref[...] = _layer_norm(alpha * h_ref[...] + ff, g_ref[...], b_ref[...])


def _combine(dest_flat, yb, h, gates, ln_g, ln_b, alpha):
    m, d = h.shape
    tm = COMBINE_TILE
    assert m % tm == 0
    grid_spec = pltpu.PrefetchScalarGridSpec(
        num_scalar_prefetch=1,
        grid=(m // tm,),
        in_specs=[pl.BlockSpec(memory_space=pl.ANY),
                  pl.BlockSpec((tm, d), lambda i, dr: (i, 0)),
                  pl.BlockSpec((tm, LANES), lambda i, dr: (i, 0)),
                  pl.BlockSpec((1, d), lambda i, dr: (0, 0)),
                  pl.BlockSpec((1, d), lambda i, dr: (0, 0))],
        out_specs=pl.BlockSpec((tm, d), lambda i, dr: (i, 0)),
        scratch_shapes=[pltpu.VMEM((2, TOP_K, tm * (d // LANES), LANES), F32), pltpu.SemaphoreType.DMA((2,))],
    )
    return pl.pallas_call(
        functools.partial(_combine_kernel, alpha),
        grid_spec=grid_spec,
        out_shape=jax.ShapeDtypeStruct((m, d), F32),
        compiler_params=_cparams(("arbitrary",)),
        name="moe_combine_ln",
    )(dest_flat, yb, h, gates, ln_g.reshape(1, d), ln_b.reshape(1, d))


def _moe_tables(counts):
    nsub_e = (counts + MOE_ROWS - 1) // MOE_ROWS
    pstart = (jnp.cumsum(nsub_e) - nsub_e) * MOE_ROWS
    nitem_e = (nsub_e + MOE_SUBS_PER_ITEM - 1) // MOE_SUBS_PER_ITEM
    item_end = jnp.cumsum(nitem_e)
    return nsub_e, pstart, nitem_e, item_end


def _moe(h, h_rows, eid, gates, w_gate, w_up, w_down, ln_g, ln_b, alpha):
    m, d = h.shape
    n_assign = m * TOP_K
    n_blocks = (n_assign + N_EXPERTS * (MOE_ROWS - 1) + MOE_ROWS - 1) // MOE_ROWS
    n_rows = n_blocks * MOE_ROWS
    n_items = N_EXPERTS + n_assign // (MOE_ROWS * MOE_SUBS_PER_ITEM)

    eid_t = eid[:, :TOP_K].T
    counts = _expert_counts(eid_t)[:, 0].astype(I32)
    nsub_e, pstart, nitem_e, item_end = _moe_tables(counts)
    dest_t = _expert_slots(eid_t, pstart.astype(F32).reshape(N_EXPERTS, 1))
    dest_flat = dest_t.T.reshape(-1)

    it = jnp.arange(n_items, dtype=I32)
    total_items = item_end[-1]
    item_e = jnp.minimum(jnp.sum(item_end[None, :] <= it[:, None], axis=1).astype(I32), N_EXPERTS - 1)
    j = it - (item_end - nitem_e)[item_e]
    used = it < total_items
    last_e = item_e[jnp.maximum(total_items - 1, 0)]
    item_nsub = jnp.where(used, jnp.clip(nsub_e[item_e] - j * MOE_SUBS_PER_ITEM, 0, MOE_SUBS_PER_ITEM), 0).astype(I32)
    item_row = jnp.where(used, pstart[item_e] + j * (MOE_ROWS * MOE_SUBS_PER_ITEM), 0).astype(I32)
    item_e = jnp.where(used, item_e, last_e).astype(I32)

    pad_row = (pstart + counts).astype(I32)
    pad_len = (nsub_e * MOE_ROWS - counts).astype(I32)
    used_rows = jnp.sum(nsub_e) * MOE_ROWS
    tail = jnp.stack([used_rows, (n_rows - used_rows) // MOE_ROWS]).astype(I32)
    xb = _dispatch(dest_flat, pad_row, pad_len, tail, h_rows, m, n_rows)
    yb = _experts(item_e, item_row, item_nsub, w_gate, w_up, w_down, xb)
    return _combine(dest_flat, yb, h, gates, ln_g, ln_b, alpha)


def _pad_cols(w, n):
    return jnp.pad(w, ((0, 0), (0, n - w.shape[1])))


def _pad_rows(w, n):
    return jnp.pad(w, ((0, n - w.shape[0]), (0, 0)))


def kernel(x, w_in, attn_sinks, rw_mu_rkv, rw_mu_wag, rw_w0, rw_w1, rw_w2, rw_a0, rw_a1, rw_a2, rw_g1, rw_g2, rw_k_k, rw_k_a, rw_r_k, rw_lnx_w, rw_lnx_b, p_attn, p_rwkv, w_o, ln1_g, ln1_b, w_group, b_group, w_expert, b_expert, w_gate, w_up, w_down, ln2_g, ln2_b):
    b, t, d = x.shape
    depth = w_in.shape[0]
    m = b * t
    alpha = (2.0 * depth) ** 0.25
    cosb, sinb = _rope_tables(t)
    qkv_w = ATTN_Q_W + 2 * ATTN_KV_W
    rkv_w = 3 * RWKV_W
    h = x
    for l in range(depth):
        hf = h.reshape(m, d)
        hb = hf.astype(BF16)
        qkv = _matmul_cols(hb, w_in[l], 0, qkv_w, F32)
        rkv = _matmul_cols(hb, w_in[l], qkv_w, rkv_w, BF16)
        gates = _matmul_cols(hb, w_in[l], qkv_w + rkv_w, 2 * d, BF16)

        y_a = _attention(qkv.reshape(b, t, qkv_w), attn_sinks[l], cosb, sinb)

        lora = lambda w, n: _pad_cols(w, n).astype(BF16)
        lorb = lambda w, n: _pad_rows(w, n).astype(BF16)
        n_w = -(-rw_w1.shape[2] // LANES) * LANES
        n_a = -(-rw_a1.shape[2] // LANES) * LANES
        n_g = -(-rw_g1.shape[2] // LANES) * LANES
        r_, k_, v_, lw_, cum_, ag_, g_ = _rwkv_prep(
            h, rkv.reshape(b, t, rkv_w), rw_mu_rkv[l], rw_mu_wag[l], rw_w0[l],
            lora(rw_w1[l], n_w), lorb(rw_w2[l], n_w), rw_a0[l], lora(rw_a1[l], n_a), lorb(rw_a2[l], n_a),
            lora(rw_g1[l], n_g), lorb(rw_g2[l], n_g))
        y_r = _wkv_scan(r_, k_, v_, lw_, cum_, ag_, g_, rw_k_k[l], rw_k_a[l], rw_r_k[l], rw_lnx_w[l], rw_lnx_b[l])

        merged = _merge(y_a.reshape(m, ATTN_Q_W), y_r.reshape(m, RWKV_W), gates,
                        p_attn[l].astype(BF16), p_rwkv[l].astype(BF16))
        w_router = _pad_cols(jnp.concatenate([w_group[l], w_expert[l]], axis=1), LANES)
        w_router_hi = w_router.astype(BF16)
        w_router = jnp.stack([w_router_hi, (w_router - w_router_hi.astype(F32)).astype(BF16)])
        b_router = _pad_cols(jnp.concatenate([b_group[l], b_expert[l]])[None, :], LANES)
        h1, h1_rows, eid, gate = _outproj_router(merged, hf, w_o[l].astype(BF16), ln1_g[l], ln1_b[l],
                                        w_router, b_router, alpha)
        h2 = _moe(h1, h1_rows, eid, gate, w_gate[l], w_up[l], w_down[l], ln2_g[l], ln2_b[l], alpha)
        h = h2.reshape(b, t, d)
    return h
```

```python
import functools

import jax
import jax.numpy as jnp
from jax import lax
from jax.experimental import pallas as pl
from jax.experimental.pallas import tpu as pltpu

F32 = jnp.float32
BF16 = jnp.bfloat16
I32 = jnp.int32

HEAD_DIM = 64
ATTN_Q_HEADS = 16
ATTN_KV_HEADS = 4
ATTN_GROUP = ATTN_Q_HEADS // ATTN_KV_HEADS
ATTN_Q_W = ATTN_Q_HEADS * HEAD_DIM
ATTN_KV_W = ATTN_KV_HEADS * HEAD_DIM
WINDOW = 128
ROPE_THETA = 10000.0
RWKV_HEADS = 16
RWKV_N = 64
RWKV_W = RWKV_HEADS * RWKV_N
RWKV_GN_EPS = 64e-5
N_GROUPS = 8
EXPERTS_PER_GROUP = 8
N_EXPERTS = N_GROUPS * EXPERTS_PER_GROUP
TOP_K = 2
LN_EPS = 1e-5

LANES = 128
SUBLANES = 8
VMEM_LIMIT_BYTES = 56 * 1024 * 1024

WKV_CHUNK = 64
WKV_CHUNKS_PER_STEP = 4
WKV_HEADS_PER_STEP = 16
MOE_ROWS = 128
MOE_SUBS_PER_ITEM = 4
MOE_K_STEPS = 4
MOE_WEIGHT_SLOTS = 3
ROUTE_TILE = 512


def _cparams(sem, vmem=VMEM_LIMIT_BYTES):
    return pltpu.CompilerParams(dimension_semantics=sem, vmem_limit_bytes=vmem)


def _sigmoid(x):
    return 1.0 / (1.0 + jnp.exp(-x))


def _dot(a, b):
    return jnp.dot(a.astype(BF16), b.astype(BF16), preferred_element_type=F32)


def _dot_nt(a, b):
    return lax.dot_general(a.astype(BF16), b.astype(BF16), (((1,), (1,)), ((), ())),
                           preferred_element_type=F32)


def _layer_norm(t, g, b):
    mu = jnp.mean(t, axis=-1, keepdims=True)
    d = t - mu
    var = jnp.mean(d * d, axis=-1, keepdims=True)
    return d * lax.rsqrt(var + LN_EPS) * g + b


def _matmul_kernel(a_ref, b_ref, o_ref):
    o_ref[...] = jnp.dot(a_ref[...], b_ref[...].astype(BF16), preferred_element_type=F32).astype(o_ref.dtype)


def _matmul_cols(a, b, col0, ncols, out_dtype, tm=2048, tn=512):
    m, k = a.shape
    tm = min(tm, m)
    cb = col0 // tn
    assert col0 % tn == 0 and ncols % tn == 0 and m % tm == 0
    return pl.pallas_call(
        _matmul_kernel,
        grid=(m // tm, ncols // tn),
        in_specs=[pl.BlockSpec((tm, k), lambda i, j: (i, 0)),
                  pl.BlockSpec((k, tn), lambda i, j: (0, j + cb))],
        out_specs=pl.BlockSpec((tm, tn), lambda i, j: (i, j)),
        out_shape=jax.ShapeDtypeStruct((m, ncols), out_dtype),
        compiler_params=_cparams(("parallel", "arbitrary")),
        name="inproj_matmul",
    )(a, b)


def _rope(x, cosb, sinb):
    half = HEAD_DIM // 2
    lane = lax.broadcasted_iota(I32, cosb.shape, 1)
    first_half = (lane % HEAD_DIM) < half
    outs = []
    for g in range(x.shape[1] // LANES):
        xg = x[:, g * LANES:(g + 1) * LANES]
        partner = jnp.where(first_half, pltpu.roll(xg, LANES - half, axis=1), pltpu.roll(xg, half, axis=1))
        outs.append(xg * cosb + partner * sinb)
    return outs


def _attn_kernel(sinks_ref, q_ref, kc_ref, kp_ref, vc_ref, vp_ref, cosc_ref, sinc_ref, cosp_ref, sinp_ref, o_ref):
    blk = pl.program_id(1)
    tq = q_ref.shape[1]
    qg = _rope(q_ref[0], cosc_ref[...], sinc_ref[...])
    kcg = _rope(kc_ref[0], cosc_ref[...], sinc_ref[...])
    kpg = _rope(kp_ref[0], cosp_ref[...], sinp_ref[...])
    vc = vc_ref[0]
    vp = vp_ref[0]

    def head(groups, h):
        g = groups[h // 2]
        return g[:, (h % 2) * HEAD_DIM:(h % 2 + 1) * HEAD_DIM]

    rows = ATTN_GROUP * tq
    qi = lax.broadcasted_iota(I32, (rows, 2 * tq), 0) % tq
    kj = lax.broadcasted_iota(I32, (rows, 2 * tq), 1)
    dist = qi + tq - kj
    valid = (dist >= 0) & (dist < WINDOW) & ((blk > 0) | (kj >= tq))
    rid = lax.broadcasted_iota(I32, (rows, 1), 0) // tq
    scale = HEAD_DIM ** -0.5
    for kvh in range(ATTN_KV_HEADS):
        qh = jnp.concatenate([head(qg, kvh * ATTN_GROUP + g) for g in range(ATTN_GROUP)], axis=0)
        kw = jnp.concatenate([head(kpg, kvh), head(kcg, kvh)], axis=0)
        vw = jnp.concatenate([vp[:, kvh * HEAD_DIM:(kvh + 1) * HEAD_DIM],
                              vc[:, kvh * HEAD_DIM:(kvh + 1) * HEAD_DIM]], axis=0)
        s = _dot_nt(qh, kw) * scale
        s = jnp.where(valid, s, -jnp.inf)
        sink = jnp.zeros((rows, 1), F32)
        for g in range(ATTN_GROUP):
            sink = jnp.where(rid == g, sinks_ref[kvh * ATTN_GROUP + g], sink)
        m = jnp.maximum(jnp.max(s, axis=-1, keepdims=True), sink)
        e = jnp.exp(s - m)
        denom = jnp.sum(e, axis=-1, keepdims=True) + jnp.exp(sink - m)
        o = _dot(e, vw) / denom
        for g in range(ATTN_GROUP):
            hq = kvh * ATTN_GROUP + g
            o_ref[0, :, hq * HEAD_DIM:(hq + 1) * HEAD_DIM] = o[g * tq:(g + 1) * tq].astype(o_ref.dtype)


def _attention(qkv, sinks, cosb, sinb):
    b, t, _ = qkv.shape
    tq = WINDOW
    nb = t // tq
    kcol = ATTN_Q_W // ATTN_KV_W
    prev = lambda i: jnp.maximum(i - 1, 0)
    grid_spec = pltpu.PrefetchScalarGridSpec(
        num_scalar_prefetch=0,
        grid=(b, nb),
        in_specs=[
            pl.BlockSpec(memory_space=pltpu.SMEM),
            pl.BlockSpec((1, tq, ATTN_Q_W), lambda bi, i: (bi, i, 0)),
            pl.BlockSpec((1, tq, ATTN_KV_W), lambda bi, i: (bi, i, kcol)),
            pl.BlockSpec((1, tq, ATTN_KV_W), lambda bi, i: (bi, prev(i), kcol)),
            pl.BlockSpec((1, tq, ATTN_KV_W), lambda bi, i: (bi, i, kcol + 1)),
            pl.BlockSpec((1, tq, ATTN_KV_W), lambda bi, i: (bi, prev(i), kcol + 1)),
            pl.BlockSpec((tq, LANES), lambda bi, i: (i, 0)),
            pl.BlockSpec((tq, LANES), lambda bi, i: (i, 0)),
            pl.BlockSpec((tq, LANES), lambda bi, i: (prev(i), 0)),
            pl.BlockSpec((tq, LANES), lambda bi, i: (prev(i), 0)),
        ],
        out_specs=pl.BlockSpec((1, tq, ATTN_Q_W), lambda bi, i: (bi, i, 0)),
    )
    return pl.pallas_call(
        _attn_kernel,
        grid_spec=grid_spec,
        out_shape=jax.ShapeDtypeStruct((b, t, ATTN_Q_W), BF16),
        compiler_params=_cparams(("parallel", "arbitrary")),
        name="swa_attention",
    )(sinks, qkv, qkv, qkv, qkv, qkv, cosb, sinb, cosb, sinb)


def _rope_tables(t):
    inv = 1.0 / (ROPE_THETA ** (jnp.arange(0, HEAD_DIM, 2, dtype=F32) / HEAD_DIM))
    ang = jnp.arange(t, dtype=F32)[:, None] * inv[None, :]
    cos, sin = jnp.cos(ang), jnp.sin(ang)
    reps = LANES // HEAD_DIM
    cosb = jnp.tile(jnp.concatenate([cos, cos], axis=-1), (1, reps))
    sinb = jnp.tile(jnp.concatenate([-sin, sin], axis=-1), (1, reps))
    return cosb, sinb


def _rwkv_prep_kernel(h_ref, hp_ref, r_ref, k_ref, v_ref, rp_ref, kp_ref, vp_ref,
                      mu_rkv_ref, mu_wag_ref, w0_ref, w1_ref, w2_ref, a0_ref, a1_ref, a2_ref,
                      g1_ref, g2_ref,
                      ro_ref, ko_ref, vo_ref, lwo_ref, cumo_ref, ago_ref, go_ref):
    first = pl.program_id(1) == 0

    def shifted(cur, prev_ref):
        last_row = prev_ref.shape[1] - 1
        prev_row = jnp.where(first, 0.0, prev_ref[0, last_row:last_row + 1, :].astype(F32))
        rowid = lax.broadcasted_iota(I32, cur.shape, 0)
        return jnp.where(rowid == 0, prev_row, pltpu.roll(cur, 1, axis=0))

    h = h_ref[0]
    xx = shifted(h, hp_ref) - h
    xw = h + xx * mu_wag_ref[0:1, :]
    xa = h + xx * mu_wag_ref[1:2, :]
    xg = h + xx * mu_wag_ref[2:3, :]
    w_raw = w0_ref[...] + _dot(jnp.tanh(_dot(xw, w1_ref[...])), w2_ref[...])
    neg = -w_raw
    softplus = jnp.maximum(neg, 0.0) + jnp.log1p(jnp.exp(-jnp.abs(neg)))
    w = -softplus - 0.5
    lw = -jnp.exp(w)
    tm = lw.shape[0]
    row = lax.broadcasted_iota(I32, (tm, tm), 0)
    col = lax.broadcasted_iota(I32, (tm, tm), 1)
    tri = ((row >= col) & (row // WKV_CHUNK == col // WKV_CHUNK)).astype(BF16)
    cum = sum(jnp.dot(tri, piece, preferred_element_type=F32) for piece in _split3(lw))
    ag = _sigmoid(a0_ref[...] + _dot(_dot(xa, a1_ref[...]), a2_ref[...]))
    go_ref[0] = _dot(_sigmoid(_dot(xg, g1_ref[...])), g2_ref[...])

    r = r_ref[0].astype(F32)
    k = k_ref[0].astype(F32)
    v = v_ref[0].astype(F32)
    r = r + (shifted(r, rp_ref) - r) * mu_rkv_ref[0:1, :]
    k = k + (shifted(k, kp_ref) - k) * mu_rkv_ref[1:2, :]
    v = v + (shifted(v, vp_ref) - v) * mu_rkv_ref[2:3, :]
    nchunk = tm // WKV_CHUNK
    for hd in range(RWKV_HEADS):
        sl = slice(hd * RWKV_N, (hd + 1) * RWKV_N)
        for ref, val in ((ro_ref, r), (ko_ref, k), (vo_ref, v), (lwo_ref, lw), (cumo_ref, cum), (ago_ref, ag)):
            ref[0, :, hd] = val[:, sl].reshape(nchunk, WKV_CHUNK, RWKV_N).astype(ref.dtype)


def _rwkv_prep(h, rkv, mu_rkv, mu_wag, w0, w1, w2, a0, a1, a2, g1, g2, tm=256):
    b, t, d = h.shape
    tm = min(tm, t)
    c = RWKV_W
    prevblk = lambda i, rows: jnp.maximum(i * (tm // rows) - 1, 0)
    rkv_rows = SUBLANES * (4 // rkv.dtype.itemsize)
    full = lambda arr: pl.BlockSpec(arr.shape, lambda bi, i: (0,) * arr.ndim)
    row = lambda arr: arr.reshape(1, -1)
    w0, a0 = row(w0), row(a0)
    in_specs = [
        pl.BlockSpec((1, tm, d), lambda bi, i: (bi, i, 0)),
        pl.BlockSpec((1, SUBLANES, d), lambda bi, i: (bi, prevblk(i, SUBLANES), 0)),
        pl.BlockSpec((1, tm, c), lambda bi, i: (bi, i, 0)),
        pl.BlockSpec((1, tm, c), lambda bi, i: (bi, i, 1)),
        pl.BlockSpec((1, tm, c), lambda bi, i: (bi, i, 2)),
        pl.BlockSpec((1, rkv_rows, c), lambda bi, i: (bi, prevblk(i, rkv_rows), 0)),
        pl.BlockSpec((1, rkv_rows, c), lambda bi, i: (bi, prevblk(i, rkv_rows), 1)),
        pl.BlockSpec((1, rkv_rows, c), lambda bi, i: (bi, prevblk(i, rkv_rows), 2)),
    ] + [full(a) for a in (mu_rkv, mu_wag, w0, w1, w2, a0, a1, a2, g1, g2)]
    assert tm % WKV_CHUNK == 0
    hm = lambda dt: jax.ShapeDtypeStruct((b, t // WKV_CHUNK, RWKV_HEADS, WKV_CHUNK, RWKV_N), dt)
    hm_spec = pl.BlockSpec((1, tm // WKV_CHUNK, RWKV_HEADS, WKV_CHUNK, RWKV_N), lambda bi, i: (bi, i, 0, 0, 0))
    return pl.pallas_call(
        _rwkv_prep_kernel,
        grid=(b, t // tm),
        in_specs=in_specs,
        out_specs=[hm_spec] * 6 + [pl.BlockSpec((1, tm, c), lambda bi, i: (bi, i, 0))],
        out_shape=[hm(BF16), hm(BF16), hm(BF16), hm(F32), hm(F32), hm(BF16), jax.ShapeDtypeStruct((b, t, c), F32)],
        compiler_params=_cparams(("parallel", "arbitrary")),
        name="rwkv_prep",
    )(h, h, rkv, rkv, rkv, rkv, rkv, rkv, mu_rkv, mu_wag, w0, w1, w2, a0, a1, a2, g1, g2)


def _split3(x):
    hi = x.astype(BF16)
    r1 = x - hi.astype(F32)
    mid = r1.astype(BF16)
    lo = (r1 - mid.astype(F32)).astype(BF16)
    return hi, mid, lo


def _bmm(a, b):
    return jnp.einsum("gmk,gkn->gmn", a.astype(BF16), b.astype(BF16), preferred_element_type=F32)


def _bmm_nt(a, b):
    return jnp.einsum("gmk,gnk->gmn", a.astype(BF16), b.astype(BF16), preferred_element_type=F32)


def _bmm_tn(a, b):
    return jnp.einsum("gtm,gtn->gmn", a.astype(BF16), b.astype(BF16), preferred_element_type=F32)


def _wkv_kernel(r_ref, k_ref, v_ref, lw_ref, cum_ref, ag_ref, g_ref, kk_ref, ka_ref, rk_ref, lnw_ref, lnb_ref,
                y_ref, s_ref, st_ref):
    c = WKV_CHUNK
    n = RWKV_N
    nc, hb = r_ref.shape[1], r_ref.shape[2]
    tc = nc * c
    g = nc * hb

    @pl.when(pl.program_id(2) == 0)
    def _():
        s_ref[...] = jnp.zeros_like(s_ref)

    chunks = lambda ref: ref[0].reshape(g, c, n).astype(F32)
    r, k, v, lw, cum, ag = (chunks(ref) for ref in (r_ref, k_ref, v_ref, lw_ref, cum_ref, ag_ref))
    per_head = lambda t: t.reshape(nc, hb, c, n)
    kk = (per_head(k) * kk_ref[...]).reshape(g, c, n)
    k = (per_head(k) * (1.0 + (per_head(ag) - 1.0) * ka_ref[...])).reshape(g, c, n)
    kk = kk / jnp.maximum(jnp.sqrt(jnp.sum(kk * kk, axis=-1, keepdims=True)), 1e-12)
    dinc = jnp.exp(cum)
    dinv = jnp.exp(-cum)
    rt = r * dinc
    kt = k * dinv
    at = -kk * jnp.exp(cum - lw)
    bt = kk * ag * dinv
    d_chunk = dinc[:, c - 1:c, :]

    row = lax.broadcasted_iota(I32, (1, c, c), 1)
    col = lax.broadcasted_iota(I32, (1, c, c), 2)
    strict = row > col
    incl = row >= col
    eye = jnp.broadcast_to((row == col).astype(F32), (g, c, c))
    p = _bmm_nt(jnp.concatenate([at, rt], axis=1), jnp.concatenate([bt, kt], axis=1))
    a_ab = jnp.where(strict, p[:, :c, :c], 0.0)
    a_ak = jnp.where(strict, p[:, :c, c:], 0.0)
    a_rb = jnp.where(incl, p[:, c:, :c], 0.0)
    a_rk = jnp.where(incl, p[:, c:, c:], 0.0)
    x = _bmm(a_ab, a_ab)
    tinv = eye + a_ab
    levels = c.bit_length() - 2
    for j in range(1, levels):
        both = _bmm(jnp.concatenate([x, tinv], axis=1), x)
        x = both[:, :c]
        tinv = tinv + both[:, c:]
    tinv = tinv + _bmm(tinv, x)
    z = _bmm(a_ak, v)
    ta = _bmm(tinv, jnp.concatenate([at, z], axis=2))
    ry = _bmm(a_rb, ta)
    rp = rt + ry[:, :, :n]
    yv = ry[:, :, n:] + _bmm(a_rk, v)
    moff = _bmm_tn(ta[:, :, :n], bt).reshape(nc, hb, n, n)
    n2 = _bmm_tn(jnp.concatenate([ta[:, :, n:], v], axis=1),
                 jnp.concatenate([bt, kt], axis=1)).reshape(nc, hb, n, n)
    dch = d_chunk.reshape(nc, hb, 1, n)

    s = s_ref[...]
    for ci in range(nc):
        st_ref[ci] = s
        s = (s + _bmm(s, moff[ci]) + n2[ci]) * dch[ci]
    s_ref[...] = s

    y = _bmm_nt(rp, st_ref[...].reshape(g, n, n)) + yv
    mu = jnp.mean(y, axis=-1, keepdims=True)
    yc = y - mu
    var = jnp.mean(yc * yc, axis=-1, keepdims=True)
    y = per_head(yc * lax.rsqrt(var + RWKV_GN_EPS)) * lnw_ref[...] + lnb_ref[...]
    bonus = jnp.sum(per_head(r * k) * rk_ref[...], axis=-1, keepdims=True)
    y = y + bonus * per_head(v)
    y = jnp.concatenate([jnp.concatenate([y[ci, hd] for hd in range(hb)], axis=1) for ci in range(nc)], axis=0)
    y_ref[0] = (y * g_ref[0]).astype(y_ref.dtype)


def _wkv_scan(r, k, v, lw, cum, ag, g, k_k, k_a, r_k, lnx_w, lnx_b):
    b, nchunks, hh, c, n = r.shape
    t = nchunks * c
    nc = min(WKV_CHUNKS_PER_STEP, nchunks)
    tc = nc * c
    hb = WKV_HEADS_PER_STEP
    blk = pl.BlockSpec((1, nc, hb, c, n), lambda bi, hi, i: (bi, i, hi, 0, 0))
    tok = pl.BlockSpec((1, tc, hb * n), lambda bi, hi, i: (bi, i, hi))
    par = pl.BlockSpec((hb, 1, n), lambda bi, hi, i: (hi, 0, 0))
    per_head = lambda arr: arr.reshape(hh, 1, n)
    return pl.pallas_call(
        _wkv_kernel,
        grid=(b, hh // hb, t // tc),
        in_specs=[blk] * 6 + [tok] + [par] * 5,
        out_specs=tok,
        out_shape=jax.ShapeDtypeStruct((b, t, hh * n), BF16),
        scratch_shapes=[pltpu.VMEM((hb, n, n), F32), pltpu.VMEM((nc, hb, n, n), F32)],
        compiler_params=_cparams(("parallel", "parallel", "arbitrary")),
        name="wkv7_scan",
    )(r, k, v, lw, cum, ag, g, per_head(k_k), per_head(k_a), per_head(r_k), per_head(lnx_w), per_head(lnx_b))


def _merge_kernel(ya_ref, yr_ref, ga_ref, gr_ref, pa_ref, pr_ref, o_ref):
    ma = jnp.dot(ya_ref[...], pa_ref[...], preferred_element_type=F32)
    mr = jnp.dot(yr_ref[...], pr_ref[...], preferred_element_type=F32)
    o_ref[...] = (_sigmoid(ga_ref[...].astype(F32)) * ma + _sigmoid(gr_ref[...].astype(F32)) * mr).astype(o_ref.dtype)


def _merge(ya, yr, gates, p_attn, p_rwkv, tm=256):
    m, c = ya.shape
    d = p_attn.shape[1]
    tm = min(tm, m)
    tile = pl.BlockSpec((tm, c), lambda i: (i, 0))
    return pl.pallas_call(
        _merge_kernel,
        grid=(m // tm,),
        in_specs=[tile, tile,
                  pl.BlockSpec((tm, d), lambda i: (i, 0)),
                  pl.BlockSpec((tm, d), lambda i: (i, 1)),
                  pl.BlockSpec((c, d), lambda i: (0, 0)),
                  pl.BlockSpec((c, d), lambda i: (0, 0))],
        out_specs=pl.BlockSpec((tm, d), lambda i: (i, 0)),
        out_shape=jax.ShapeDtypeStruct((m, d), BF16),
        compiler_params=_cparams(("parallel",)),
        name="gated_merge",
    )(ya, yr, gates, gates, p_attn, p_rwkv)


def _outproj_router_kernel(alpha, mg_ref, x_ref, wo_ref, g_ref, b_ref, wr_ref, br_ref,
                           h_ref, hrow_ref, eid_ref, gate_ref):
    mix = jnp.dot(mg_ref[...], wo_ref[...], preferred_element_type=F32)
    h = _layer_norm(alpha * x_ref[...] + mix, g_ref[...], b_ref[...])
    h_ref[...] = h
    slab = h.shape[1] // LANES
    for j in range(slab):
        hrow_ref[pl.ds(j, h.shape[0], stride=slab), :] = h[:, j * LANES:(j + 1) * LANES]
    h_hi = h.astype(BF16)
    h_lo = (h - h_hi.astype(F32)).astype(BF16)
    logits = (jnp.dot(h_hi, wr_ref[0], preferred_element_type=F32)
              + jnp.dot(h_lo, wr_ref[0], preferred_element_type=F32)
              + jnp.dot(h_hi, wr_ref[1], preferred_element_type=F32)) + br_ref[...]
    lane = lax.broadcasted_iota(I32, logits.shape, 1)
    ninf = -jnp.inf
    big = jnp.int32(2 * LANES)
    glog = jnp.where(lane < N_GROUPS, logits, ninf)
    gmax = jnp.max(glog, axis=-1, keepdims=True)
    gidx = jnp.min(jnp.where(glog == gmax, lane, big), axis=-1, keepdims=True)
    gtop = 1.0 / jnp.sum(jnp.exp(glog - gmax), axis=-1, keepdims=True)
    eg = (lane - N_GROUPS) // EXPERTS_PER_GROUP
    in_group = (lane >= N_GROUPS) & (lane < N_GROUPS + N_EXPERTS) & (eg == gidx)
    el = jnp.where(in_group, logits, ninf)
    m1 = jnp.max(el, axis=-1, keepdims=True)
    i1 = jnp.min(jnp.where(el == m1, lane, big), axis=-1, keepdims=True)
    el2 = jnp.where(lane == i1, ninf, el)
    m2 = jnp.max(el2, axis=-1, keepdims=True)
    i2 = jnp.min(jnp.where(el2 == m2, lane, big), axis=-1, keepdims=True)
    t = jnp.exp(m2 - m1)
    p1 = 1.0 / (1.0 + t)
    p2 = t / (1.0 + t)
    eid_ref[...] = jnp.where(lane == 0, i1 - N_GROUPS, jnp.where(lane == 1, i2 - N_GROUPS, 0))
    gate_ref[...] = jnp.where(lane == 0, gtop * p1, jnp.where(lane == 1, gtop * p2, 0.0))


def _outproj_router(merged, x, w_o, ln_g, ln_b, w_router, b_router, alpha, tm=512):
    m, d = x.shape
    tm = min(tm, m)
    tile = pl.BlockSpec((tm, d), lambda i: (i, 0))
    vec = pl.BlockSpec((1, d), lambda i: (0, 0))
    small = pl.BlockSpec((tm, LANES), lambda i: (i, 0))
    return pl.pallas_call(
        functools.partial(_outproj_router_kernel, alpha),
        grid=(m // tm,),
        in_specs=[tile, tile, pl.BlockSpec((d, d), lambda i: (0, 0)), vec, vec,
                  pl.BlockSpec((2, d, LANES), lambda i: (0, 0, 0)), pl.BlockSpec((1, LANES), lambda i: (0, 0))],
        out_specs=[tile, pl.BlockSpec((tm * (d // LANES), LANES), lambda i: (i, 0)), small, small],
        out_shape=[jax.ShapeDtypeStruct((m, d), F32), jax.ShapeDtypeStruct((m * (d // LANES), LANES), F32),
                   jax.ShapeDtypeStruct((m, LANES), I32), jax.ShapeDtypeStruct((m, LANES), F32)],
        compiler_params=_cparams(("parallel",)),
        name="outproj_ln_router",
    )(merged, x, w_o, ln_g.reshape(1, d), ln_b.reshape(1, d), w_router, b_router)


def _onehots(eid_ref):
    tm = eid_ref.shape[1]
    e_iota = lax.broadcasted_iota(I32, (N_EXPERTS, tm), 0)
    oh0 = (eid_ref[0:1, :] == e_iota).astype(F32)
    oh1 = (eid_ref[1:2, :] == e_iota).astype(F32)
    return oh0, oh1


def _count_kernel(eid_ref, cnt_ref):
    @pl.when(pl.program_id(0) == 0)
    def _():
        cnt_ref[...] = jnp.zeros_like(cnt_ref)

    oh0, oh1 = _onehots(eid_ref)
    cnt_ref[...] += jnp.sum(oh0 + oh1, axis=1, keepdims=True)


def _slot_kernel(eid_ref, pstart_ref, dest_ref, run_ref):
    @pl.when(pl.program_id(0) == 0)
    def _():
        run_ref[...] = jnp.zeros_like(run_ref)

    tm = eid_ref.shape[1]
    oh0, oh1 = _onehots(eid_ref)
    both = oh0 + oh1
    earlier = (lax.broadcasted_iota(I32, (tm, tm), 0) < lax.broadcasted_iota(I32, (tm, tm), 1)).astype(BF16)
    pre = jnp.dot(both.astype(BF16), earlier, preferred_element_type=F32)
    base = pre + run_ref[...] + pstart_ref[...]
    dest_ref[0:1, :] = jnp.sum(oh0 * base, axis=0, keepdims=True).astype(I32)
    dest_ref[1:2, :] = jnp.sum(oh1 * base, axis=0, keepdims=True).astype(I32)
    run_ref[...] += jnp.sum(both, axis=1, keepdims=True)


def _expert_counts(eid_t):
    m = eid_t.shape[1]
    tm = min(ROUTE_TILE, m)
    return pl.pallas_call(
        _count_kernel,
        grid=(m // tm,),
        in_specs=[pl.BlockSpec((TOP_K, tm), lambda i: (0, i))],
        out_specs=pl.BlockSpec((N_EXPERTS, 1), lambda i: (0, 0)),
        out_shape=jax.ShapeDtypeStruct((N_EXPERTS, 1), F32),
        compiler_params=_cparams(("arbitrary",)),
        name="expert_counts",
    )(eid_t)


def _expert_slots(eid_t, pstart):
    m = eid_t.shape[1]
    tm = min(ROUTE_TILE, m)
    return pl.pallas_call(
        _slot_kernel,
        grid=(m // tm,),
        in_specs=[pl.BlockSpec((TOP_K, tm), lambda i: (0, i)),
                  pl.BlockSpec((N_EXPERTS, 1), lambda i: (0, 0))],
        out_specs=pl.BlockSpec((TOP_K, tm), lambda i: (0, i)),
        out_shape=jax.ShapeDtypeStruct((TOP_K, m), I32),
        scratch_shapes=[pltpu.VMEM((N_EXPERTS, 1), F32)],
        compiler_params=_cparams(("arbitrary",)),
        name="expert_slots",
    )(eid_t, pstart)


def _row_slab(ref, row, slab):
    return ref.at[pl.ds(pl.multiple_of(row * slab, slab), slab), :]


def _dispatch_kernel(dest_ref, pad_row_ref, pad_len_ref, tail_ref, h_ref, xb_ref, zero_ref, sem, zero_sem):
    tm = DISPATCH_TILE
    slab = h_ref.shape[0] // tm
    rb = MOE_ROWS
    step = pl.program_id(0)
    t0 = step * tm

    def zero_rows(row, n_rows):
        return pltpu.make_async_copy(zero_ref.at[pl.ds(0, n_rows * slab), :],
                                     xb_ref.at[pl.ds(pl.multiple_of(row * slab, slab), n_rows * slab), :], zero_sem)

    def zero_fill(op):
        def pad_run(e, carry):
            row = pad_row_ref[e]
            length = pad_len_ref[e]
            piece = rb // 2
            while piece >= 1:
                take = (length & piece) != 0

                @pl.when(take)
                def _(row=row, piece=piece):
                    op(zero_rows(row, piece))
                row = row + jnp.where(take, piece, 0)
                piece //= 2
            return carry
        lax.fori_loop(0, pad_len_ref.shape[0], pad_run, 0)

        def tail_block(tb, carry):
            op(zero_rows(tail_ref[0] + tb * rb, rb))
            return carry
        lax.fori_loop(0, tail_ref[1], tail_block, 0)

    @pl.when(step == 0)
    def _():
        zero_ref[...] = jnp.zeros_like(zero_ref)
        zero_fill(lambda c: c.start())

    def copy(n, k):
        return pltpu.make_async_copy(_row_slab(h_ref, n, slab),
                                     _row_slab(xb_ref, dest_ref[TOP_K * (t0 + n) + k], slab), sem)

    def start(n, carry):
        for k in range(TOP_K):
            copy(n, k).start()
        return carry

    lax.fori_loop(0, tm, start, 0, unroll=8)
    for k in range(TOP_K):
        pltpu.make_async_copy(h_ref, xb_ref.at[pl.ds(0, h_ref.shape[0]), :], sem).wait()

    @pl.when(step == pl.num_programs(0) - 1)
    def _():
        zero_fill(lambda c: c.wait())


DISPATCH_TILE = 512


def _dispatch(dest_flat, pad_row, pad_len, tail, h_rows, m, n_rows):
    slab = h_rows.shape[0] // m
    assert m % DISPATCH_TILE == 0
    grid_spec = pltpu.PrefetchScalarGridSpec(
        num_scalar_prefetch=4,
        grid=(m // DISPATCH_TILE,),
        in_specs=[pl.BlockSpec((DISPATCH_TILE * slab, LANES), lambda i, *_: (i, 0))],
        out_specs=pl.BlockSpec(memory_space=pl.ANY),
        scratch_shapes=[pltpu.VMEM((MOE_ROWS * slab, LANES), F32), pltpu.SemaphoreType.DMA(()),
                        pltpu.SemaphoreType.DMA(())],
    )
    return pl.pallas_call(
        _dispatch_kernel,
        grid_spec=grid_spec,
        out_shape=jax.ShapeDtypeStruct((n_rows * slab, LANES), F32),
        compiler_params=pltpu.CompilerParams(dimension_semantics=("arbitrary",), has_side_effects=True),
        name="moe_dispatch",
    )(dest_flat, pad_row, pad_len, tail, h_rows)


def _expert_kernel(item_e_ref, item_row_ref, item_nsub_ref, wg_hbm, wu_hbm, wd_hbm, xb_ref, yb_ref,
                   wg_buf, wu_buf, wd_buf, xin_ref, yout_ref, x_ref, acc_ref, wgu_ref, wdb_ref, sem_w, sem_in, sem_out):
    it = pl.program_id(0)
    f = pl.program_id(1)
    n_items = pl.num_programs(0)
    nf = pl.num_programs(1)
    kc = wg_buf.shape[1]
    ff = wg_buf.shape[2]
    fc = wd_buf.shape[1]
    nsub = item_nsub_ref[it]
    row0 = item_row_ref[it]
    nxt = jnp.minimum(it + 1, n_items - 1)
    nsub_next = jnp.where(it + 1 < n_items, item_nsub_ref[nxt], 0)
    prv = jnp.maximum(it - 1, 0)
    nsub_prev = jnp.where(it > 0, item_nsub_ref[prv], 0)
    rb = MOE_ROWS
    slab = x_ref.shape[0] * kc // LANES
    blk = rb * slab

    def stage_rows(ref, s):
        return ref.at[pl.ds(pl.multiple_of(s * blk, blk), blk), :]

    def hbm_rows(ref, item_row, s):
        return ref.at[pl.ds(pl.multiple_of((item_row + s * rb) * slab, blk), blk), :]

    def in_copy(item_row, s):
        return pltpu.make_async_copy(hbm_rows(xb_ref, item_row, s), stage_rows(xin_ref, s), sem_in)

    def out_copy(item_row, s):
        return pltpu.make_async_copy(stage_rows(yout_ref, s), hbm_rows(yb_ref, item_row, s), sem_out)

    def loop(n, fn):
        def body(s, carry):
            fn(s)
            return carry
        lax.fori_loop(0, n, body, 0)

    def weight_copies(step):
        item = step // nf
        k = step % nf
        e = item_e_ref[jnp.minimum(item, n_items - 1)]
        slot = step % MOE_WEIGHT_SLOTS
        return (pltpu.make_async_copy(wg_hbm.at[e, pl.ds(pl.multiple_of(k * kc, kc), kc), :], wg_buf.at[slot],
                                      sem_w.at[slot]),
                pltpu.make_async_copy(wu_hbm.at[e, pl.ds(pl.multiple_of(k * kc, kc), kc), :], wu_buf.at[slot],
                                      sem_w.at[slot]),
                pltpu.make_async_copy(wd_hbm.at[e, pl.ds(pl.multiple_of(k * fc, SUBLANES), fc), :], wd_buf.at[slot],
                                      sem_w.at[slot]))

    def fetch_weights(step):
        item = step // nf
        exists = (item < n_items) & (item_nsub_ref[jnp.minimum(item, n_items - 1)] > 0)

        @pl.when(exists)
        def _():
            for c in weight_copies(step):
                c.start()

    @pl.when(nsub > 0)
    def _():
        this_step = it * nf + f

        @pl.when(this_step == 0)
        def _():
            for ahead in range(MOE_WEIGHT_SLOTS - 1):
                fetch_weights(ahead)
        fetch_weights(this_step + MOE_WEIGHT_SLOTS - 1)

        @pl.when(f == 0)
        def _():
            @pl.when(it == 0)
            def _():
                x_ref[...] = jnp.zeros_like(x_ref)
                loop(nsub, lambda s: in_copy(row0, s).start())
            loop(nsub, lambda s: in_copy(row0, s).wait())

            def to_matrix(s):
                rows = pl.ds(pl.multiple_of(s * rb, rb), rb)
                base = pl.multiple_of(s * blk, blk)
                for j in range(slab):
                    c0 = (j * LANES) % kc
                    x_ref[(j * LANES) // kc, rows, c0:c0 + LANES] = (
                        xin_ref[pl.ds(base + j, rb, stride=slab), :].astype(BF16))
            loop(nsub, to_matrix)
            loop(nsub_next, lambda s: in_copy(item_row_ref[nxt], s).start())

        for c in weight_copies(this_step):
            c.wait()
        slot = this_step % MOE_WEIGHT_SLOTS
        wgu_ref[:, :ff] = wg_buf[slot].astype(BF16)
        wgu_ref[:, ff:] = wu_buf[slot].astype(BF16)
        wdb_ref[pl.ds(pl.multiple_of(f * fc, 2 * SUBLANES), fc), :] = wd_buf[slot].astype(BF16)

        def gate_up(start, size):
            rows = pl.ds(pl.multiple_of(start, rb), size)
            part = jnp.dot(x_ref[f, rows, :], wgu_ref[...], preferred_element_type=F32)
            acc_ref[rows, :] = jnp.where(f > 0, acc_ref[rows, :], 0.0) + part

        gate_up(0, 2 * rb)

        def pair(p, carry):
            gate_up(p * (2 * rb), 2 * rb)
            return carry
        lax.fori_loop(1, nsub // 2, pair, 0)

        @pl.when((nsub % 2 == 1) & (nsub > 1))
        def _():
            gate_up((nsub - 1) * rb, rb)

        @pl.when(f == nf - 1)
        def _():
            loop(nsub_prev, lambda s: out_copy(item_row_ref[prv], s).wait())

            def down(start, size):
                rows = pl.ds(pl.multiple_of(start, rb), size)
                gate = acc_ref[rows, :ff]
                act = (gate * _sigmoid(gate) * acc_ref[rows, ff:]).astype(BF16)
                y = jnp.dot(act, wdb_ref[...], preferred_element_type=F32)
                base = pl.multiple_of(start * slab, blk)
                for j in range(slab):
                    yout_ref[pl.ds(base + j, size, stride=slab), :] = y[:, j * LANES:(j + 1) * LANES]

            down(0, 2 * rb)

            def down_pair(p, carry):
                down(p * (2 * rb), 2 * rb)
                return carry
            lax.fori_loop(1, nsub // 2, down_pair, 0)

            @pl.when((nsub % 2 == 1) & (nsub > 1))
            def _():
                down((nsub - 1) * rb, rb)

            loop(nsub, lambda s: out_copy(row0, s).start())

            @pl.when(nsub_next == 0)
            def _():
                loop(nsub, lambda s: out_copy(row0, s).wait())


def _experts(item_e, item_row, item_nsub, w_gate, w_up, w_down, xb):
    d = w_gate.shape[1]
    ff = w_gate.shape[2]
    nk = MOE_K_STEPS
    kc = d // nk
    fc = ff // nk
    assert d % nk == 0 and ff % nk == 0 and kc % LANES == 0 and fc % (2 * SUBLANES) == 0 and ff % LANES == 0
    n_items = item_e.shape[0]
    rows_max = MOE_ROWS * MOE_SUBS_PER_ITEM
    slab = d // LANES

    grid_spec = pltpu.PrefetchScalarGridSpec(
        num_scalar_prefetch=3,
        grid=(n_items, nk),
        in_specs=[pl.BlockSpec(memory_space=pl.ANY)] * 4,
        out_specs=pl.BlockSpec(memory_space=pl.ANY),
        scratch_shapes=[
            pltpu.VMEM((MOE_WEIGHT_SLOTS, kc, ff), F32),
            pltpu.VMEM((MOE_WEIGHT_SLOTS, kc, ff), F32),
            pltpu.VMEM((MOE_WEIGHT_SLOTS, fc, d), F32),
            pltpu.VMEM((rows_max * slab, LANES), F32),
            pltpu.VMEM((rows_max * slab, LANES), F32),
            pltpu.VMEM((nk, rows_max, kc), BF16),
            pltpu.VMEM((rows_max, 2 * ff), F32),
            pltpu.VMEM((kc, 2 * ff), BF16),
            pltpu.VMEM((ff, d), BF16),
            pltpu.SemaphoreType.DMA((MOE_WEIGHT_SLOTS,)),
            pltpu.SemaphoreType.DMA(()),
            pltpu.SemaphoreType.DMA(()),
        ],
    )
    return pl.pallas_call(
        _expert_kernel,
        grid_spec=grid_spec,
        out_shape=jax.ShapeDtypeStruct(xb.shape, F32),
        input_output_aliases={6: 0},
        compiler_params=pltpu.CompilerParams(dimension_semantics=("arbitrary", "arbitrary"),
                                             vmem_limit_bytes=VMEM_LIMIT_BYTES, has_side_effects=True),
        name="moe_experts",
    )(item_e, item_row, item_nsub, w_gate, w_up, w_down, xb)


COMBINE_TILE = 256


def _combine_kernel(alpha, dest_ref, yb_ref, h_ref, gate_ref, g_ref, b_ref, o_ref, buf_ref, sem):
    tm = COMBINE_TILE
    i = pl.program_id(0)
    slab = buf_ref.shape[2] // tm

    def issue(tile, half):
        def start(n, carry):
            for k in range(TOP_K):
                pltpu.make_async_copy(_row_slab(yb_ref, dest_ref[TOP_K * (tile * tm + n) + k], slab),
                                      _row_slab(buf_ref.at[half, k], n, slab), sem.at[half]).start()
            return carry
        lax.fori_loop(0, tm, start, 0, unroll=8)

    @pl.when(i == 0)
    def _():
        issue(0, 0)

    @pl.when(i + 1 < pl.num_programs(0))
    def _():
        issue(i + 1, (i + 1) % 2)

    half = i % 2
    for k in range(TOP_K):
        pltpu.make_async_copy(yb_ref.at[pl.ds(0, buf_ref.shape[2]), :], buf_ref.at[half, k], sem.at[half]).wait()
    gate = gate_ref[...]
    g0 = gate[:, 0:1]
    g1 = gate[:, 1:2]
    col = lambda k, j: buf_ref[half, k, pl.ds(j, tm, stride=slab), :]
    ff = jnp.concatenate([g0 * col(0, j) + g1 * col(1, j) for j in range(slab)], axis=1)
    o_ref[...] = _layer_norm(alpha * h_ref[...] + ff, g_ref[...], b_ref[...])


def _combine(dest_flat, yb, h, gates, ln_g, ln_b, alpha):
    m, d = h.shape
    tm = COMBINE_TILE
    assert m % tm == 0
    grid_spec = pltpu.PrefetchScalarGridSpec(
        num_scalar_prefetch=1,
        grid=(m // tm,),
        in_specs=[pl.BlockSpec(memory_space=pl.ANY),
                  pl.BlockSpec((tm, d), lambda i, dr: (i, 0)),
                  pl.BlockSpec((tm, LANES), lambda i, dr: (i, 0)),
                  pl.BlockSpec((1, d), lambda i, dr: (0, 0)),
                  pl.BlockSpec((1, d), lambda i, dr: (0, 0))],
        out_specs=pl.BlockSpec((tm, d), lambda i, dr: (i, 0)),
        scratch_shapes=[pltpu.VMEM((2, TOP_K, tm * (d // LANES), LANES), F32), pltpu.SemaphoreType.DMA((2,))],
    )
    return pl.pallas_call(
        functools.partial(_combine_kernel, alpha),
        grid_spec=grid_spec,
        out_shape=jax.ShapeDtypeStruct((m, d), F32),
        compiler_params=_cparams(("arbitrary",)),
        name="moe_combine_ln",
    )(dest_flat, yb, h, gates, ln_g.reshape(1, d), ln_b.reshape(1, d))


def _moe_tables(counts):
    nsub_e = (counts + MOE_ROWS - 1) // MOE_ROWS
    pstart = (jnp.cumsum(nsub_e) - nsub_e) * MOE_ROWS
    nitem_e = (nsub_e + MOE_SUBS_PER_ITEM - 1) // MOE_SUBS_PER_ITEM
    item_end = jnp.cumsum(nitem_e)
    return nsub_e, pstart, nitem_e, item_end


def _moe(h, h_rows, eid, gates, w_gate, w_up, w_down, ln_g, ln_b, alpha):
    m, d = h.shape
    n_assign = m * TOP_K
    n_blocks = (n_assign + N_EXPERTS * (MOE_ROWS - 1) + MOE_ROWS - 1) // MOE_ROWS
    n_rows = n_blocks * MOE_ROWS
    n_items = N_EXPERTS + n_assign // (MOE_ROWS * MOE_SUBS_PER_ITEM)

    eid_t = eid[:, :TOP_K].T
    counts = _expert_counts(eid_t)[:, 0].astype(I32)
    nsub_e, pstart, nitem_e, item_end = _moe_tables(counts)
    dest_t = _expert_slots(eid_t, pstart.astype(F32).reshape(N_EXPERTS, 1))
    dest_flat = dest_t.T.reshape(-1)

    it = jnp.arange(n_items, dtype=I32)
    total_items = item_end[-1]
    item_e = jnp.minimum(jnp.sum(item_end[None, :] <= it[:, None], axis=1).astype(I32), N_EXPERTS - 1)
    j = it - (item_end - nitem_e)[item_e]
    used = it < total_items
    last_e = item_e[jnp.maximum(total_items - 1, 0)]
    item_nsub = jnp.where(used, jnp.clip(nsub_e[item_e] - j * MOE_SUBS_PER_ITEM, 0, MOE_SUBS_PER_ITEM), 0).astype(I32)
    item_row = jnp.where(used, pstart[item_e] + j * (MOE_ROWS * MOE_SUBS_PER_ITEM), 0).astype(I32)
    item_e = jnp.where(used, item_e, last_e).astype(I32)

    pad_row = (pstart + counts).astype(I32)
    pad_len = (nsub_e * MOE_ROWS - counts).astype(I32)
    used_rows = jnp.sum(nsub_e) * MOE_ROWS
    tail = jnp.stack([used_rows, (n_rows - used_rows) // MOE_ROWS]).astype(I32)
    xb = _dispatch(dest_flat, pad_row, pad_len, tail, h_rows, m, n_rows)
    yb = _experts(item_e, item_row, item_nsub, w_gate, w_up, w_down, xb)
    return _combine(dest_flat, yb, h, gates, ln_g, ln_b, alpha)


def _pad_cols(w, n):
    return jnp.pad(w, ((0, 0), (0, n - w.shape[1])))


def _pad_rows(w, n):
    return jnp.pad(w, ((0, n - w.shape[0]), (0, 0)))


def kernel(x, w_in, attn_sinks, rw_mu_rkv, rw_mu_wag, rw_w0, rw_w1, rw_w2, rw_a0, rw_a1, rw_a2, rw_g1, rw_g2, rw_k_k, rw_k_a, rw_r_k, rw_lnx_w, rw_lnx_b, p_attn, p_rwkv, w_o, ln1_g, ln1_b, w_group, b_group, w_expert, b_expert, w_gate, w_up, w_down, ln2_g, ln2_b):
    b, t, d = x.shape
    depth = w_in.shape[0]
    m = b * t
    alpha = (2.0 * depth) ** 0.25
    cosb, sinb = _rope_tables(t)
    qkv_w = ATTN_Q_W + 2 * ATTN_KV_W
    rkv_w = 3 * RWKV_W
    h = x
    for l in range(depth):
        hf = h.reshape(m, d)
        hb = hf.astype(BF16)
        qkv = _matmul_cols(hb, w_in[l], 0, qkv_w, F32)
        rkv = _matmul_cols(hb, w_in[l], qkv_w, rkv_w, BF16)
        gates = _matmul_cols(hb, w_in[l], qkv_w + rkv_w, 2 * d, BF16)

        y_a = _attention(qkv.reshape(b, t, qkv_w), attn_sinks[l], cosb, sinb)

        lora = lambda w, n: _pad_cols(w, n).astype(BF16)
        lorb = lambda w, n: _pad_rows(w, n).astype(BF16)
        n_w = -(-rw_w1.shape[2] // LANES) * LANES
        n_a = -(-rw_a1.shape[2] // LANES) * LANES
        n_g = -(-rw_g1.shape[2] // LANES) * LANES
        r_, k_, v_, lw_, cum_, ag_, g_ = _rwkv_prep(
            h, rkv.reshape(b, t, rkv_w), rw_mu_rkv[l], rw_mu_wag[l], rw_w0[l],
            lora(rw_w1[l], n_w), lorb(rw_w2[l], n_w), rw_a0[l], lora(rw_a1[l], n_a), lorb(rw_a2[l], n_a),
            lora(rw_g1[l], n_g), lorb(rw_g2[l], n_g))
        y_r = _wkv_scan(r_, k_, v_, lw_, cum_, ag_, g_, rw_k_k[l], rw_k_a[l], rw_r_k[l], rw_lnx_w[l], rw_lnx_b[l])

        merged = _merge(y_a.reshape(m, ATTN_Q_W), y_r.reshape(m, RWKV_W), gates,
                        p_attn[l].astype(BF16), p_rwkv[l].astype(BF16))
        w_router = _pad_cols(jnp.concatenate([w_group[l], w_expert[l]], axis=1), LANES)
        w_router_hi = w_router.astype(BF16)
        w_router = jnp.stack([w_router_hi, (w_router - w_router_hi.astype(F32)).astype(BF16)])
        b_router = _pad_cols(jnp.concatenate([b_group[l], b_expert[l]])[None, :], LANES)
        h1, h1_rows, eid, gate = _outproj_router(merged, hf, w_o[l].astype(BF16), ln1_g[l], ln1_b[l],
                                        w_router, b_router, alpha)
        h2 = _moe(h1, h1_rows, eid, gate, w_gate[l], w_up[l], w_down[l], ln2_g[l], ln2_b[l], alpha)
        h = h2.reshape(b, t, d)
    return h
```

```python
import functools

import numpy as np
import jax
import jax.numpy as jnp
from jax import lax
from jax.experimental import pallas as pl
from jax.experimental.pallas import tpu as pltpu

F32 = jnp.float32
BF16 = jnp.bfloat16
I32 = jnp.int32

HEAD_DIM = 64
ATTN_Q_HEADS = 16
ATTN_KV_HEADS = 4
ATTN_GROUP = ATTN_Q_HEADS // ATTN_KV_HEADS
ATTN_Q_W = ATTN_Q_HEADS * HEAD_DIM
ATTN_KV_W = ATTN_KV_HEADS * HEAD_DIM
WINDOW = 128
ROPE_THETA = 10000.0
RWKV_HEADS = 16
RWKV_N = 64
RWKV_W = RWKV_HEADS * RWKV_N
RWKV_GN_EPS = 64e-5
N_GROUPS = 8
EXPERTS_PER_GROUP = 8
N_EXPERTS = N_GROUPS * EXPERTS_PER_GROUP
TOP_K = 2
LN_EPS = 1e-5

LANES = 128
SUBLANES = 8
VMEM_LIMIT_BYTES = 56 * 1024 * 1024

WKV_CHUNK = 64
WKV_CHUNKS_PER_STEP = 4
WKV_HEADS_PER_STEP = 16
MOE_ROWS = 128
MOE_SUBS_PER_ITEM = 4
MOE_K_STEPS = 4
MOE_WEIGHT_SLOTS = 3
ROUTE_TILE = 512


def _cparams(sem, vmem=VMEM_LIMIT_BYTES):
    return pltpu.CompilerParams(dimension_semantics=sem, vmem_limit_bytes=vmem)


def _sigmoid(x):
    return 1.0 / (1.0 + jnp.exp(-x))


def _dot(a, b):
    return jnp.dot(a.astype(BF16), b.astype(BF16), preferred_element_type=F32)


def _dot_nt(a, b):
    return lax.dot_general(a.astype(BF16), b.astype(BF16), (((1,), (1,)), ((), ())),
                           preferred_element_type=F32)


def _layer_norm(t, g, b):
    mu = jnp.mean(t, axis=-1, keepdims=True)
    d = t - mu
    var = jnp.mean(d * d, axis=-1, keepdims=True)
    return d * lax.rsqrt(var + LN_EPS) * g + b


def _matmul_kernel(a_ref, b_ref, o_ref):
    o_ref[...] = jnp.dot(a_ref[...], b_ref[...].astype(BF16), preferred_element_type=F32).astype(o_ref.dtype)


def _matmul_cols(a, b, col0, ncols, out_dtype, tm=2048, tn=512):
    m, k = a.shape
    tm = min(tm, m)
    cb = col0 // tn
    assert col0 % tn == 0 and ncols % tn == 0 and m % tm == 0
    return pl.pallas_call(
        _matmul_kernel,
        grid=(m // tm, ncols // tn),
        in_specs=[pl.BlockSpec((tm, k), lambda i, j: (i, 0)),
                  pl.BlockSpec((k, tn), lambda i, j: (0, j + cb))],
        out_specs=pl.BlockSpec((tm, tn), lambda i, j: (i, j)),
        out_shape=jax.ShapeDtypeStruct((m, ncols), out_dtype),
        compiler_params=_cparams(("parallel", "arbitrary")),
        name="inproj_matmul",
    )(a, b)


def _rope(x, cosb, sinb):
    half = HEAD_DIM // 2
    lane = lax.broadcasted_iota(I32, cosb.shape, 1)
    first_half = (lane % HEAD_DIM) < half
    outs = []
    for g in range(x.shape[1] // LANES):
        xg = x[:, g * LANES:(g + 1) * LANES]
        partner = jnp.where(first_half, pltpu.roll(xg, LANES - half, axis=1), pltpu.roll(xg, half, axis=1))
        outs.append(xg * cosb + partner * sinb)
    return outs


def _attn_kernel(sinks_ref, q_ref, kc_ref, kp_ref, vc_ref, vp_ref, cosc_ref, sinc_ref, cosp_ref, sinp_ref, bias_ref,
                 o_ref):
    tq = q_ref.shape[1]
    qg = _rope(q_ref[0], cosc_ref[...], sinc_ref[...])
    kcg = _rope(kc_ref[0], cosc_ref[...], sinc_ref[...])
    kpg = _rope(kp_ref[0], cosp_ref[...], sinp_ref[...])
    vc = vc_ref[0]
    vp = vp_ref[0]

    def head(groups, h):
        g = groups[h // 2]
        return g[:, (h % 2) * HEAD_DIM:(h % 2 + 1) * HEAD_DIM]

    rows = ATTN_GROUP * tq
    bias = bias_ref[0]
    rid = lax.broadcasted_iota(I32, (rows, 1), 0) // tq
    scale = HEAD_DIM ** -0.5
    for kvh in range(ATTN_KV_HEADS):
        qh = jnp.concatenate([head(qg, kvh * ATTN_GROUP + g) for g in range(ATTN_GROUP)], axis=0)
        kw = jnp.concatenate([head(kpg, kvh), head(kcg, kvh)], axis=0)
        vw = jnp.concatenate([vp[:, kvh * HEAD_DIM:(kvh + 1) * HEAD_DIM],
                              vc[:, kvh * HEAD_DIM:(kvh + 1) * HEAD_DIM]], axis=0)
        s = _dot_nt(qh, kw) * scale + bias
        sink = jnp.zeros((rows, 1), F32)
        for g in range(ATTN_GROUP):
            sink = jnp.where(rid == g, sinks_ref[kvh * ATTN_GROUP + g], sink)
        m = jnp.maximum(jnp.max(s, axis=-1, keepdims=True), sink)
        e = jnp.exp(s - m)
        denom = jnp.sum(e, axis=-1, keepdims=True) + jnp.exp(sink - m)
        o = _dot(e, vw) / denom
        for g in range(ATTN_GROUP):
            hq = kvh * ATTN_GROUP + g
            o_ref[0, :, hq * HEAD_DIM:(hq + 1) * HEAD_DIM] = o[g * tq:(g + 1) * tq].astype(o_ref.dtype)


def _attention(qkv, sinks, cosb, sinb):
    b, t, _ = qkv.shape
    tq = WINDOW
    nb = t // tq
    kcol = ATTN_Q_W // ATTN_KV_W
    prev = lambda i: jnp.maximum(i - 1, 0)
    grid_spec = pltpu.PrefetchScalarGridSpec(
        num_scalar_prefetch=0,
        grid=(b, nb),
        in_specs=[
            pl.BlockSpec(memory_space=pltpu.SMEM),
            pl.BlockSpec((1, tq, ATTN_Q_W), lambda bi, i: (bi, i, 0)),
            pl.BlockSpec((1, tq, ATTN_KV_W), lambda bi, i: (bi, i, kcol)),
            pl.BlockSpec((1, tq, ATTN_KV_W), lambda bi, i: (bi, prev(i), kcol)),
            pl.BlockSpec((1, tq, ATTN_KV_W), lambda bi, i: (bi, i, kcol + 1)),
            pl.BlockSpec((1, tq, ATTN_KV_W), lambda bi, i: (bi, prev(i), kcol + 1)),
            pl.BlockSpec((tq, LANES), lambda bi, i: (i, 0)),
            pl.BlockSpec((tq, LANES), lambda bi, i: (i, 0)),
            pl.BlockSpec((tq, LANES), lambda bi, i: (prev(i), 0)),
            pl.BlockSpec((tq, LANES), lambda bi, i: (prev(i), 0)),
            pl.BlockSpec((1, ATTN_GROUP * tq, 2 * tq), lambda bi, i: (jnp.minimum(i, 1), 0, 0)),
        ],
        out_specs=pl.BlockSpec((1, tq, ATTN_Q_W), lambda bi, i: (bi, i, 0)),
    )
    return pl.pallas_call(
        _attn_kernel,
        grid_spec=grid_spec,
        out_shape=jax.ShapeDtypeStruct((b, t, ATTN_Q_W), BF16),
        compiler_params=_cparams(("parallel", "arbitrary")),
        name="swa_attention",
    )(sinks, qkv, qkv, qkv, qkv, qkv, cosb, sinb, cosb, sinb, _band_bias(tq))


def _band_bias(tq):
    qi = np.arange(ATTN_GROUP * tq)[:, None] % tq
    kj = np.arange(2 * tq)[None, :]
    dist = qi + tq - kj
    band = (dist >= 0) & (dist < WINDOW)
    first = band & (kj >= tq)
    return jnp.asarray(np.where(np.stack([first, band]), 0.0, -np.inf), F32)


def _rope_tables(t):
    inv = 1.0 / (ROPE_THETA ** (np.arange(0, HEAD_DIM, 2, dtype=np.float64) / HEAD_DIM))
    ang = np.arange(t, dtype=np.float64)[:, None] * inv[None, :]
    cos, sin = np.cos(ang), np.sin(ang)
    reps = LANES // HEAD_DIM
    cosb = np.tile(np.concatenate([cos, cos], axis=-1), (1, reps))
    sinb = np.tile(np.concatenate([-sin, sin], axis=-1), (1, reps))
    return jnp.asarray(cosb, F32), jnp.asarray(sinb, F32)


def _rwkv_prep_kernel(h_ref, hp_ref, r_ref, k_ref, v_ref, rp_ref, kp_ref, vp_ref,
                      mu_rkv_ref, mu_wag_ref, w0_ref, w1_ref, w2_ref, a0_ref, a1_ref, a2_ref,
                      g1_ref, g2_ref,
                      ro_ref, ko_ref, vo_ref, lwo_ref, cumo_ref, ago_ref, go_ref):
    first = pl.program_id(1) == 0

    def shifted(cur, prev_ref):
        last_row = prev_ref.shape[1] - 1
        prev_row = jnp.where(first, 0.0, prev_ref[0, last_row:last_row + 1, :].astype(F32))
        rowid = lax.broadcasted_iota(I32, cur.shape, 0)
        return jnp.where(rowid == 0, prev_row, pltpu.roll(cur, 1, axis=0))

    h = h_ref[0]
    xx = shifted(h, hp_ref) - h
    xw = h + xx * mu_wag_ref[0:1, :]
    xa = h + xx * mu_wag_ref[1:2, :]
    xg = h + xx * mu_wag_ref[2:3, :]
    w_raw = w0_ref[...] + _dot(jnp.tanh(_dot(xw, w1_ref[...])), w2_ref[...])
    neg = -w_raw
    softplus = jnp.maximum(neg, 0.0) + jnp.log1p(jnp.exp(-jnp.abs(neg)))
    w = -softplus - 0.5
    lw = -jnp.exp(w)
    tm = lw.shape[0]
    row = lax.broadcasted_iota(I32, (tm, tm), 0)
    col = lax.broadcasted_iota(I32, (tm, tm), 1)
    tri = ((row >= col) & (row // WKV_CHUNK == col // WKV_CHUNK)).astype(BF16)
    cum = sum(jnp.dot(tri, piece, preferred_element_type=F32) for piece in _split3(lw))
    ag = _sigmoid(a0_ref[...] + _dot(_dot(xa, a1_ref[...]), a2_ref[...]))
    go_ref[0] = _dot(_sigmoid(_dot(xg, g1_ref[...])), g2_ref[...])

    r = r_ref[0].astype(F32)
    k = k_ref[0].astype(F32)
    v = v_ref[0].astype(F32)
    r = r + (shifted(r, rp_ref) - r) * mu_rkv_ref[0:1, :]
    k = k + (shifted(k, kp_ref) - k) * mu_rkv_ref[1:2, :]
    v = v + (shifted(v, vp_ref) - v) * mu_rkv_ref[2:3, :]
    nchunk = tm // WKV_CHUNK
    for hd in range(RWKV_HEADS):
        sl = slice(hd * RWKV_N, (hd + 1) * RWKV_N)
        for ref, val in ((ro_ref, r), (ko_ref, k), (vo_ref, v), (lwo_ref, lw), (cumo_ref, cum), (ago_ref, ag)):
            ref[0, :, hd] = val[:, sl].reshape(nchunk, WKV_CHUNK, RWKV_N).astype(ref.dtype)


def _rwkv_prep(h, rkv, mu_rkv, mu_wag, w0, w1, w2, a0, a1, a2, g1, g2, tm=256):
    b, t, d = h.shape
    tm = min(tm, t)
    c = RWKV_W
    prevblk = lambda i, rows: jnp.maximum(i * (tm // rows) - 1, 0)
    rkv_rows = SUBLANES * (4 // rkv.dtype.itemsize)
    full = lambda arr: pl.BlockSpec(arr.shape, lambda bi, i: (0,) * arr.ndim)
    row = lambda arr: arr.reshape(1, -1)
    w0, a0 = row(w0), row(a0)
    in_specs = [
        pl.BlockSpec((1, tm, d), lambda bi, i: (bi, i, 0)),
        pl.BlockSpec((1, SUBLANES, d), lambda bi, i: (bi, prevblk(i, SUBLANES), 0)),
        pl.BlockSpec((1, tm, c), lambda bi, i: (bi, i, 0)),
        pl.BlockSpec((1, tm, c), lambda bi, i: (bi, i, 1)),
        pl.BlockSpec((1, tm, c), lambda bi, i: (bi, i, 2)),
        pl.BlockSpec((1, rkv_rows, c), lambda bi, i: (bi, prevblk(i, rkv_rows), 0)),
        pl.BlockSpec((1, rkv_rows, c), lambda bi, i: (bi, prevblk(i, rkv_rows), 1)),
        pl.BlockSpec((1, rkv_rows, c), lambda bi, i: (bi, prevblk(i, rkv_rows), 2)),
    ] + [full(a) for a in (mu_rkv, mu_wag, w0, w1, w2, a0, a1, a2, g1, g2)]
    assert tm % WKV_CHUNK == 0
    hm = lambda dt: jax.ShapeDtypeStruct((b, t // WKV_CHUNK, RWKV_HEADS, WKV_CHUNK, RWKV_N), dt)
    hm_spec = pl.BlockSpec((1, tm // WKV_CHUNK, RWKV_HEADS, WKV_CHUNK, RWKV_N), lambda bi, i: (bi, i, 0, 0, 0))
    return pl.pallas_call(
        _rwkv_prep_kernel,
        grid=(b, t // tm),
        in_specs=in_specs,
        out_specs=[hm_spec] * 6 + [pl.BlockSpec((1, tm, c), lambda bi, i: (bi, i, 0))],
        out_shape=[hm(BF16), hm(BF16), hm(BF16), hm(F32), hm(F32), hm(BF16), jax.ShapeDtypeStruct((b, t, c), F32)],
        compiler_params=_cparams(("parallel", "arbitrary")),
        name="rwkv_prep",
    )(h, h, rkv, rkv, rkv, rkv, rkv, rkv, mu_rkv, mu_wag, w0, w1, w2, a0, a1, a2, g1, g2)


def _split3(x):
    hi = x.astype(BF16)
    r1 = x - hi.astype(F32)
    mid = r1.astype(BF16)
    lo = (r1 - mid.astype(F32)).astype(BF16)
    return hi, mid, lo


def _bmm(a, b):
    return jnp.einsum("gmk,gkn->gmn", a.astype(BF16), b.astype(BF16), preferred_element_type=F32)


def _bmm_nt(a, b):
    return jnp.einsum("gmk,gnk->gmn", a.astype(BF16), b.astype(BF16), preferred_element_type=F32)


def _bmm_tn(a, b):
    return jnp.einsum("gtm,gtn->gmn", a.astype(BF16), b.astype(BF16), preferred_element_type=F32)


def _wkv_kernel(r_ref, k_ref, v_ref, lw_ref, cum_ref, ag_ref, g_ref, kk_ref, ka_ref, rk_ref, lnw_ref, lnb_ref,
                y_ref, s_ref, st_ref):
    c = WKV_CHUNK
    n = RWKV_N
    nc, hb = r_ref.shape[1], r_ref.shape[2]
    tc = nc * c
    g = nc * hb

    @pl.when(pl.program_id(2) == 0)
    def _():
        s_ref[...] = jnp.zeros_like(s_ref)

    chunks = lambda ref: ref[0].reshape(g, c, n).astype(F32)
    r, k, v, lw, cum, ag = (chunks(ref) for ref in (r_ref, k_ref, v_ref, lw_ref, cum_ref, ag_ref))
    per_head = lambda t: t.reshape(nc, hb, c, n)
    kk = (per_head(k) * kk_ref[...]).reshape(g, c, n)
    k = (per_head(k) * (1.0 + (per_head(ag) - 1.0) * ka_ref[...])).reshape(g, c, n)
    kk = kk / jnp.maximum(jnp.sqrt(jnp.sum(kk * kk, axis=-1, keepdims=True)), 1e-12)
    dinc = jnp.exp(cum)
    dinv = jnp.exp(-cum)
    rt = r * dinc
    kt = k * dinv
    at = -kk * jnp.exp(cum - lw)
    bt = kk * ag * dinv
    d_chunk = dinc[:, c - 1:c, :]

    row = lax.broadcasted_iota(I32, (1, c, c), 1)
    col = lax.broadcasted_iota(I32, (1, c, c), 2)
    strict = row > col
    incl = row >= col
    eye = jnp.broadcast_to((row == col).astype(F32), (g, c, c))
    p = _bmm_nt(jnp.concatenate([at, rt], axis=1), jnp.concatenate([bt, kt], axis=1))
    a_ab = jnp.where(strict, p[:, :c, :c], 0.0)
    a_ak = jnp.where(strict, p[:, :c, c:], 0.0)
    a_rb = jnp.where(incl, p[:, c:, :c], 0.0)
    a_rk = jnp.where(incl, p[:, c:, c:], 0.0)
    x = _bmm(a_ab, a_ab)
    tinv = eye + a_ab
    levels = c.bit_length() - 2
    for j in range(1, levels):
        both = _bmm(jnp.concatenate([x, tinv], axis=1), x)
        x = both[:, :c]
        tinv = tinv + both[:, c:]
    tinv = tinv + _bmm(tinv, x)
    z = _bmm(a_ak, v)
    ta = _bmm(tinv, jnp.concatenate([at, z], axis=2))
    ry = _bmm(a_rb, ta)
    rp = rt + ry[:, :, :n]
    yv = ry[:, :, n:] + _bmm(a_rk, v)
    moff = _bmm_tn(ta[:, :, :n], bt).reshape(nc, hb, n, n)
    n2 = _bmm_tn(jnp.concatenate([ta[:, :, n:], v], axis=1),
                 jnp.concatenate([bt, kt], axis=1)).reshape(nc, hb, n, n)
    dch = d_chunk.reshape(nc, hb, 1, n)

    s = s_ref[...]
    for ci in range(nc):
        st_ref[ci] = s
        s = (s + _bmm(s, moff[ci]) + n2[ci]) * dch[ci]
    s_ref[...] = s

    y = _bmm_nt(rp, st_ref[...].reshape(g, n, n)) + yv
    mu = jnp.mean(y, axis=-1, keepdims=True)
    yc = y - mu
    var = jnp.mean(yc * yc, axis=-1, keepdims=True)
    y = per_head(yc * lax.rsqrt(var + RWKV_GN_EPS)) * lnw_ref[...] + lnb_ref[...]
    bonus = jnp.sum(per_head(r * k) * rk_ref[...], axis=-1, keepdims=True)
    y = y + bonus * per_head(v)
    y = jnp.concatenate([jnp.concatenate([y[ci, hd] for hd in range(hb)], axis=1) for ci in range(nc)], axis=0)
    y_ref[0] = (y * g_ref[0]).astype(y_ref.dtype)


def _wkv_scan(r, k, v, lw, cum, ag, g, k_k, k_a, r_k, lnx_w, lnx_b):
    b, nchunks, hh, c, n = r.shape
    t = nchunks * c
    nc = min(WKV_CHUNKS_PER_STEP, nchunks)
    tc = nc * c
    hb = WKV_HEADS_PER_STEP
    blk = pl.BlockSpec((1, nc, hb, c, n), lambda bi, hi, i: (bi, i, hi, 0, 0))
    tok = pl.BlockSpec((1, tc, hb * n), lambda bi, hi, i: (bi, i, hi))
    par = pl.BlockSpec((hb, 1, n), lambda bi, hi, i: (hi, 0, 0))
    per_head = lambda arr: arr.reshape(hh, 1, n)
    return pl.pallas_call(
        _wkv_kernel,
        grid=(b, hh // hb, t // tc),
        in_specs=[blk] * 6 + [tok] + [par] * 5,
        out_specs=tok,
        out_shape=jax.ShapeDtypeStruct((b, t, hh * n), BF16),
        scratch_shapes=[pltpu.VMEM((hb, n, n), F32), pltpu.VMEM((nc, hb, n, n), F32)],
        compiler_params=_cparams(("parallel", "parallel", "arbitrary")),
        name="wkv7_scan",
    )(r, k, v, lw, cum, ag, g, per_head(k_k), per_head(k_a), per_head(r_k), per_head(lnx_w), per_head(lnx_b))


def _merge_kernel(ya_ref, yr_ref, ga_ref, gr_ref, pa_ref, pr_ref, o_ref):
    ma = jnp.dot(ya_ref[...], pa_ref[...], preferred_element_type=F32)
    mr = jnp.dot(yr_ref[...], pr_ref[...], preferred_element_type=F32)
    o_ref[...] = (_sigmoid(ga_ref[...].astype(F32)) * ma + _sigmoid(gr_ref[...].astype(F32)) * mr).astype(o_ref.dtype)


def _merge(ya, yr, gates, p_attn, p_rwkv, tm=256):
    m, c = ya.shape
    d = p_attn.shape[1]
    tm = min(tm, m)
    tile = pl.BlockSpec((tm, c), lambda i: (i, 0))
    return pl.pallas_call(
        _merge_kernel,
        grid=(m // tm,),
        in_specs=[tile, tile,
                  pl.BlockSpec((tm, d), lambda i: (i, 0)),
                  pl.BlockSpec((tm, d), lambda i: (i, 1)),
                  pl.BlockSpec((c, d), lambda i: (0, 0)),
                  pl.BlockSpec((c, d), lambda i: (0, 0))],
        out_specs=pl.BlockSpec((tm, d), lambda i: (i, 0)),
        out_shape=jax.ShapeDtypeStruct((m, d), BF16),
        compiler_params=_cparams(("parallel",)),
        name="gated_merge",
    )(ya, yr, gates, gates, p_attn, p_rwkv)


def _outproj_router_kernel(alpha, mg_ref, x_ref, wo_ref, g_ref, b_ref, wr_ref, br_ref,
                           h_ref, hrow_ref, eid_ref, gate_ref):
    mix = jnp.dot(mg_ref[...], wo_ref[...], preferred_element_type=F32)
    h = _layer_norm(alpha * x_ref[...] + mix, g_ref[...], b_ref[...])
    h_ref[...] = h
    slab = h.shape[1] // LANES
    for j in range(slab):
        hrow_ref[pl.ds(j, h.shape[0], stride=slab), :] = h[:, j * LANES:(j + 1) * LANES]
    h_hi = h.astype(BF16)
    h_lo = (h - h_hi.astype(F32)).astype(BF16)
    logits = (jnp.dot(h_hi, wr_ref[0], preferred_element_type=F32)
              + jnp.dot(h_lo, wr_ref[0], preferred_element_type=F32)
              + jnp.dot(h_hi, wr_ref[1], preferred_element_type=F32)) + br_ref[...]
    lane = lax.broadcasted_iota(I32, logits.shape, 1)
    ninf = -jnp.inf
    big = jnp.int32(2 * LANES)
    glog = jnp.where(lane < N_GROUPS, logits, ninf)
    gmax = jnp.max(glog, axis=-1, keepdims=True)
    gidx = jnp.min(jnp.where(glog == gmax, lane, big), axis=-1, keepdims=True)
    gtop = 1.0 / jnp.sum(jnp.exp(glog - gmax), axis=-1, keepdims=True)
    eg = (lane - N_GROUPS) // EXPERTS_PER_GROUP
    in_group = (lane >= N_GROUPS) & (lane < N_GROUPS + N_EXPERTS) & (eg == gidx)
    el = jnp.where(in_group, logits, ninf)
    m1 = jnp.max(el, axis=-1, keepdims=True)
    i1 = jnp.min(jnp.where(el == m1, lane, big), axis=-1, keepdims=True)
    el2 = jnp.where(lane == i1, ninf, el)
    m2 = jnp.max(el2, axis=-1, keepdims=True)
    i2 = jnp.min(jnp.where(el2 == m2, lane, big), axis=-1, keepdims=True)
    t = jnp.exp(m2 - m1)
    p1 = 1.0 / (1.0 + t)
    p2 = t / (1.0 + t)
    eid_ref[...] = jnp.where(lane == 0, i1 - N_GROUPS, jnp.where(lane == 1, i2 - N_GROUPS, 0))
    gate_ref[...] = jnp.where(lane == 0, gtop * p1, jnp.where(lane == 1, gtop * p2, 0.0))


def _outproj_router(merged, x, w_o, ln_g, ln_b, w_router, b_router, alpha, tm=512):
    m, d = x.shape
    tm = min(tm, m)
    tile = pl.BlockSpec((tm, d), lambda i: (i, 0))
    vec = pl.BlockSpec((1, d), lambda i: (0, 0))
    small = pl.BlockSpec((tm, LANES), lambda i: (i, 0))
    return pl.pallas_call(
        functools.partial(_outproj_router_kernel, alpha),
        grid=(m // tm,),
        in_specs=[tile, tile, pl.BlockSpec((d, d), lambda i: (0, 0)), vec, vec,
                  pl.BlockSpec((2, d, LANES), lambda i: (0, 0, 0)), pl.BlockSpec((1, LANES), lambda i: (0, 0))],
        out_specs=[tile, pl.BlockSpec((tm * (d // LANES), LANES), lambda i: (i, 0)), small, small],
        out_shape=[jax.ShapeDtypeStruct((m, d), F32), jax.ShapeDtypeStruct((m * (d // LANES), LANES), F32),
                   jax.ShapeDtypeStruct((m, LANES), I32), jax.ShapeDtypeStruct((m, LANES), F32)],
        compiler_params=_cparams(("parallel",)),
        name="outproj_ln_router",
    )(merged, x, w_o, ln_g.reshape(1, d), ln_b.reshape(1, d), w_router, b_router)


def _onehots(eid_ref):
    tm = eid_ref.shape[1]
    e_iota = lax.broadcasted_iota(I32, (N_EXPERTS, tm), 0)
    oh0 = (eid_ref[0:1, :] == e_iota).astype(F32)
    oh1 = (eid_ref[1:2, :] == e_iota).astype(F32)
    return oh0, oh1


def _count_kernel(eid_ref, cnt_ref):
    @pl.when(pl.program_id(0) == 0)
    def _():
        cnt_ref[...] = jnp.zeros_like(cnt_ref)

    oh0, oh1 = _onehots(eid_ref)
    cnt_ref[...] += jnp.sum(oh0 + oh1, axis=1, keepdims=True)


def _slot_kernel(eid_ref, pstart_ref, dest_ref, run_ref):
    @pl.when(pl.program_id(0) == 0)
    def _():
        run_ref[...] = jnp.zeros_like(run_ref)

    tm = eid_ref.shape[1]
    oh0, oh1 = _onehots(eid_ref)
    both = oh0 + oh1
    earlier = (lax.broadcasted_iota(I32, (tm, tm), 0) < lax.broadcasted_iota(I32, (tm, tm), 1)).astype(BF16)
    pre = jnp.dot(both.astype(BF16), earlier, preferred_element_type=F32)
    base = pre + run_ref[...] + pstart_ref[...]
    dest_ref[0:1, :] = jnp.sum(oh0 * base, axis=0, keepdims=True).astype(I32)
    dest_ref[1:2, :] = jnp.sum(oh1 * base, axis=0, keepdims=True).astype(I32)
    run_ref[...] += jnp.sum(both, axis=1, keepdims=True)


def _expert_counts(eid_t):
    m = eid_t.shape[1]
    tm = min(ROUTE_TILE, m)
    return pl.pallas_call(
        _count_kernel,
        grid=(m // tm,),
        in_specs=[pl.BlockSpec((TOP_K, tm), lambda i: (0, i))],
        out_specs=pl.BlockSpec((N_EXPERTS, 1), lambda i: (0, 0)),
        out_shape=jax.ShapeDtypeStruct((N_EXPERTS, 1), F32),
        compiler_params=_cparams(("arbitrary",)),
        name="expert_counts",
    )(eid_t)


def _expert_slots(eid_t, pstart):
    m = eid_t.shape[1]
    tm = min(ROUTE_TILE, m)
    return pl.pallas_call(
        _slot_kernel,
        grid=(m // tm,),
        in_specs=[pl.BlockSpec((TOP_K, tm), lambda i: (0, i)),
                  pl.BlockSpec((N_EXPERTS, 1), lambda i: (0, 0))],
        out_specs=pl.BlockSpec((TOP_K, tm), lambda i: (0, i)),
        out_shape=jax.ShapeDtypeStruct((TOP_K, m), I32),
        scratch_shapes=[pltpu.VMEM((N_EXPERTS, 1), F32)],
        compiler_params=_cparams(("arbitrary",)),
        name="expert_slots",
    )(eid_t, pstart)


def _row_slab(ref, row, slab):
    return ref.at[pl.ds(pl.multiple_of(row * slab, slab), slab), :]


def _dispatch_kernel(dest_ref, pad_row_ref, pad_len_ref, tail_ref, h_ref, xb_ref, zero_ref, sem, zero_sem):
    tm = DISPATCH_TILE
    slab = h_ref.shape[0] // tm
    rb = MOE_ROWS
    step = pl.program_id(0)
    t0 = step * tm

    def zero_rows(row, n_rows):
        return pltpu.make_async_copy(zero_ref.at[pl.ds(0, n_rows * slab), :],
                                     xb_ref.at[pl.ds(pl.multiple_of(row * slab, slab), n_rows * slab), :], zero_sem)

    def zero_fill(op):
        def pad_run(e, carry):
            row = pad_row_ref[e]
            length = pad_len_ref[e]
            piece = rb // 2
            while piece >= 1:
                take = (length & piece) != 0

                @pl.when(take)
                def _(row=row, piece=piece):
                    op(zero_rows(row, piece))
                row = row + jnp.where(take, piece, 0)
                piece //= 2
            return carry
        lax.fori_loop(0, pad_len_ref.shape[0], pad_run, 0)

        def tail_block(tb, carry):
            op(zero_rows(tail_ref[0] + tb * rb, rb))
            return carry
        lax.fori_loop(0, tail_ref[1], tail_block, 0)

    @pl.when(step == 0)
    def _():
        zero_ref[...] = jnp.zeros_like(zero_ref)
        zero_fill(lambda c: c.start())

    def copy(n, k):
        return pltpu.make_async_copy(_row_slab(h_ref, n, slab),
                                     _row_slab(xb_ref, dest_ref[TOP_K * (t0 + n) + k], slab), sem)

    def start(n, carry):
        for k in range(TOP_K):
            copy(n, k).start()
        return carry

    lax.fori_loop(0, tm, start, 0, unroll=8)
    for k in range(TOP_K):
        pltpu.make_async_copy(h_ref, xb_ref.at[pl.ds(0, h_ref.shape[0]), :], sem).wait()

    @pl.when(step == pl.num_programs(0) - 1)
    def _():
        zero_fill(lambda c: c.wait())


DISPATCH_TILE = 512


def _dispatch(dest_flat, pad_row, pad_len, tail, h_rows, m, n_rows):
    slab = h_rows.shape[0] // m
    assert m % DISPATCH_TILE == 0
    grid_spec = pltpu.PrefetchScalarGridSpec(
        num_scalar_prefetch=4,
        grid=(m // DISPATCH_TILE,),
        in_specs=[pl.BlockSpec((DISPATCH_TILE * slab, LANES), lambda i, *_: (i, 0))],
        out_specs=pl.BlockSpec(memory_space=pl.ANY),
        scratch_shapes=[pltpu.VMEM((MOE_ROWS * slab, LANES), F32), pltpu.SemaphoreType.DMA(()),
                        pltpu.SemaphoreType.DMA(())],
    )
    return pl.pallas_call(
        _dispatch_kernel,
        grid_spec=grid_spec,
        out_shape=jax.ShapeDtypeStruct((n_rows * slab, LANES), F32),
        compiler_params=pltpu.CompilerParams(dimension_semantics=("arbitrary",), has_side_effects=True),
        name="moe_dispatch",
    )(dest_flat, pad_row, pad_len, tail, h_rows)


def _expert_kernel(item_e_ref, item_row_ref, item_nsub_ref, wg_hbm, wu_hbm, wd_hbm, xb_ref, yb_ref,
                   wg_buf, wu_buf, wd_buf, xin_ref, yout_ref, x_ref, acc_ref, wgu_ref, wdb_ref, sem_w, sem_in, sem_out):
    it = pl.program_id(0)
    f = pl.program_id(1)
    n_items = pl.num_programs(0)
    nf = pl.num_programs(1)
    kc = wg_buf.shape[1]
    ff = wg_buf.shape[2]
    fc = wd_buf.shape[1]
    nsub = item_nsub_ref[it]
    row0 = item_row_ref[it]
    nxt = jnp.minimum(it + 1, n_items - 1)
    nsub_next = jnp.where(it + 1 < n_items, item_nsub_ref[nxt], 0)
    prv = jnp.maximum(it - 1, 0)
    nsub_prev = jnp.where(it > 0, item_nsub_ref[prv], 0)
    rb = MOE_ROWS
    slab = x_ref.shape[0] * kc // LANES
    blk = rb * slab

    def stage_rows(ref, s):
        return ref.at[pl.ds(pl.multiple_of(s * blk, blk), blk), :]

    def hbm_rows(ref, item_row, s):
        return ref.at[pl.ds(pl.multiple_of((item_row + s * rb) * slab, blk), blk), :]

    def in_copy(item_row, s):
        return pltpu.make_async_copy(hbm_rows(xb_ref, item_row, s), stage_rows(xin_ref, s), sem_in)

    def out_copy(item_row, s):
        return pltpu.make_async_copy(stage_rows(yout_ref, s), hbm_rows(yb_ref, item_row, s), sem_out)

    def loop(n, fn):
        def body(s, carry):
            fn(s)
            return carry
        lax.fori_loop(0, n, body, 0)

    def weight_copies(step):
        item = step // nf
        k = step % nf
        e = item_e_ref[jnp.minimum(item, n_items - 1)]
        slot = step % MOE_WEIGHT_SLOTS
        return (pltpu.make_async_copy(wg_hbm.at[e, pl.ds(pl.multiple_of(k * kc, kc), kc), :], wg_buf.at[slot],
                                      sem_w.at[slot]),
                pltpu.make_async_copy(wu_hbm.at[e, pl.ds(pl.multiple_of(k * kc, kc), kc), :], wu_buf.at[slot],
                                      sem_w.at[slot]),
                pltpu.make_async_copy(wd_hbm.at[e, pl.ds(pl.multiple_of(k * fc, SUBLANES), fc), :], wd_buf.at[slot],
                                      sem_w.at[slot]))

    def fetch_weights(step):
        item = step // nf
        exists = (item < n_items) & (item_nsub_ref[jnp.minimum(item, n_items - 1)] > 0)

        @pl.when(exists)
        def _():
            for c in weight_copies(step):
                c.start()

    @pl.when(nsub > 0)
    def _():
        this_step = it * nf + f

        @pl.when(this_step == 0)
        def _():
            for ahead in range(MOE_WEIGHT_SLOTS - 1):
                fetch_weights(ahead)
        fetch_weights(this_step + MOE_WEIGHT_SLOTS - 1)

        @pl.when(f == 0)
        def _():
            @pl.when(it == 0)
            def _():
                x_ref[...] = jnp.zeros_like(x_ref)
                loop(nsub, lambda s: in_copy(row0, s).start())
            loop(nsub, lambda s: in_copy(row0, s).wait())

            def to_matrix(s):
                rows = pl.ds(pl.multiple_of(s * rb, rb), rb)
                base = pl.multiple_of(s * blk, blk)
                for j in range(slab):
                    c0 = (j * LANES) % kc
                    x_ref[(j * LANES) // kc, rows, c0:c0 + LANES] = (
                        xin_ref[pl.ds(base + j, rb, stride=slab), :].astype(BF16))
            loop(nsub, to_matrix)
            loop(nsub_next, lambda s: in_copy(item_row_ref[nxt], s).start())

        for c in weight_copies(this_step):
            c.wait()
        slot = this_step % MOE_WEIGHT_SLOTS
        wgu_ref[:, :ff] = wg_buf[slot].astype(BF16)
        wgu_ref[:, ff:] = wu_buf[slot].astype(BF16)
        wdb_ref[pl.ds(pl.multiple_of(f * fc, 2 * SUBLANES), fc), :] = wd_buf[slot].astype(BF16)

        def gate_up(start, size):
            rows = pl.ds(pl.multiple_of(start, rb), size)
            part = jnp.dot(x_ref[f, rows, :], wgu_ref[...], preferred_element_type=F32)
            acc_ref[rows, :] = jnp.where(f > 0, acc_ref[rows, :], 0.0) + part

        gate_up(0, 2 * rb)

        def pair(p, carry):
            gate_up(p * (2 * rb), 2 * rb)
            return carry
        lax.fori_loop(1, nsub // 2, pair, 0)

        @pl.when((nsub % 2 == 1) & (nsub > 1))
        def _():
            gate_up((nsub - 1) * rb, rb)

        @pl.when(f == nf - 1)
        def _():
            loop(nsub_prev, lambda s: out_copy(item_row_ref[prv], s).wait())

            def down(start, size):
                rows = pl.ds(pl.multiple_of(start, rb), size)
                gate = acc_ref[rows, :ff]
                act = (gate * _sigmoid(gate) * acc_ref[rows, ff:]).astype(BF16)
                y = jnp.dot(act, wdb_ref[...], preferred_element_type=F32)
                base = pl.multiple_of(start * slab, blk)
                for j in range(slab):
                    yout_ref[pl.ds(base + j, size, stride=slab), :] = y[:, j * LANES:(j + 1) * LANES]

            down(0, 2 * rb)

            def down_pair(p, carry):
                down(p * (2 * rb), 2 * rb)
                return carry
            lax.fori_loop(1, nsub // 2, down_pair, 0)

            @pl.when((nsub % 2 == 1) & (nsub > 1))
            def _():
                down((nsub - 1) * rb, rb)

            loop(nsub, lambda s: out_copy(row0, s).start())

            @pl.when(nsub_next == 0)
            def _():
                loop(nsub, lambda s: out_copy(row0, s).wait())


def _experts(item_e, item_row, item_nsub, w_gate, w_up, w_down, xb):
    d = w_gate.shape[1]
    ff = w_gate.shape[2]
    nk = MOE_K_STEPS
    kc = d // nk
    fc = ff // nk
    assert d % nk == 0 and ff % nk == 0 and kc % LANES == 0 and fc % (2 * SUBLANES) == 0 and ff % LANES == 0
    n_items = item_e.shape[0]
    rows_max = MOE_ROWS * MOE_SUBS_PER_ITEM
    slab = d // LANES

    grid_spec = pltpu.PrefetchScalarGridSpec(
        num_scalar_prefetch=3,
        grid=(n_items, nk),
        in_specs=[pl.BlockSpec(memory_space=pl.ANY)] * 4,
        out_specs=pl.BlockSpec(memory_space=pl.ANY),
        scratch_shapes=[
            pltpu.VMEM((MOE_WEIGHT_SLOTS, kc, ff), F32),
            pltpu.VMEM((MOE_WEIGHT_SLOTS, kc, ff), F32),
            pltpu.VMEM((MOE_WEIGHT_SLOTS, fc, d), F32),
            pltpu.VMEM((rows_max * slab, LANES), F32),
            pltpu.VMEM((rows_max * slab, LANES), F32),
            pltpu.VMEM((nk, rows_max, kc), BF16),
            pltpu.VMEM((rows_max, 2 * ff), F32),
            pltpu.VMEM((kc, 2 * ff), BF16),
            pltpu.VMEM((ff, d), BF16),
            pltpu.SemaphoreType.DMA((MOE_WEIGHT_SLOTS,)),
            pltpu.SemaphoreType.DMA(()),
            pltpu.SemaphoreType.DMA(()),
        ],
    )
    return pl.pallas_call(
        _expert_kernel,
        grid_spec=grid_spec,
        out_shape=jax.ShapeDtypeStruct(xb.shape, F32),
        input_output_aliases={6: 0},
        compiler_params=pltpu.CompilerParams(dimension_semantics=("arbitrary", "arbitrary"),
                                             vmem_limit_bytes=VMEM_LIMIT_BYTES, has_side_effects=True),
        name="moe_experts",
    )(item_e, item_row, item_nsub, w_gate, w_up, w_down, xb)


COMBINE_TILE = 256


def _combine_kernel(alpha, dest_ref, yb_ref, h_ref, gate_ref, g_ref, b_ref, o_ref, buf_ref, sem):
    tm = COMBINE_TILE
    i = pl.program_id(0)
    slab = buf_ref.shape[2] // tm

    def issue(tile, half):
        def start(n, carry):
            for k in range(TOP_K):
                pltpu.make_async_copy(_row_slab(yb_ref, dest_ref[TOP_K * (tile * tm + n) + k], slab),
                                      _row_slab(buf_ref.at[half, k], n, slab), sem.at[half]).start()
            return carry
        lax.fori_loop(0, tm, start, 0, unroll=8)

    @pl.when(i == 0)
    def _():
        issue(0, 0)

    @pl.when(i + 1 < pl.num_programs(0))
    def _():
        issue(i + 1, (i + 1) % 2)

    half = i % 2
    for k in range(TOP_K):
        pltpu.make_async_copy(yb_ref.at[pl.ds(0, buf_ref.shape[2]), :], buf_ref.at[half, k], sem.at[half]).wait()
    gate = gate_ref[...]
    g0 = gate[:, 0:1]
    g1 = gate[:, 1:2]
    col = lambda k, j: buf_ref[half, k, pl.ds(j, tm, stride=slab), :]
    ff = jnp.concatenate([g0 * col(0, j) + g1 * col(1, j) for j in range(slab)], axis=1)
    o_ref[...] = _layer_norm(alpha * h_ref[...] + ff, g_ref[...], b_ref[...])


def _combine(dest_flat, yb, h, gates, ln_g, ln_b, alpha):
    m, d = h.shape
    tm = COMBINE_TILE
    assert m % tm == 0
    grid_spec = pltpu.PrefetchScalarGridSpec(
        num_scalar_prefetch=1,
        grid=(m // tm,),
        in_specs=[pl.BlockSpec(memory_space=pl.ANY),
                  pl.BlockSpec((tm, d), lambda i, dr: (i, 0)),
                  pl.BlockSpec((tm, LANES), lambda i, dr: (i, 0)),
                  pl.BlockSpec((1, d), lambda i, dr: (0, 0)),
                  pl.BlockSpec((1, d), lambda i, dr: (0, 0))],
        out_specs=pl.BlockSpec((tm, d), lambda i, dr: (i, 0)),
        scratch_shapes=[pltpu.VMEM((2, TOP_K, tm * (d // LANES), LANES), F32), pltpu.SemaphoreType.DMA((2,))],
    )
    return pl.pallas_call(
        functools.partial(_combine_kernel, alpha),
        grid_spec=grid_spec,
        out_shape=jax.ShapeDtypeStruct((m, d), F32),
        compiler_params=_cparams(("arbitrary",)),
        name="moe_combine_ln",
    )(dest_flat, yb, h, gates, ln_g.reshape(1, d), ln_b.reshape(1, d))


def _moe_tables(counts):
    nsub_e = (counts + MOE_ROWS - 1) // MOE_ROWS
    pstart = (jnp.cumsum(nsub_e) - nsub_e) * MOE_ROWS
    nitem_e = (nsub_e + MOE_SUBS_PER_ITEM - 1) // MOE_SUBS_PER_ITEM
    item_end = jnp.cumsum(nitem_e)
    return nsub_e, pstart, nitem_e, item_end


def _moe(h, h_rows, eid, gates, w_gate, w_up, w_down, ln_g, ln_b, alpha):
    m, d = h.shape
    n_assign = m * TOP_K
    n_blocks = (n_assign + N_EXPERTS * (MOE_ROWS - 1) + MOE_ROWS - 1) // MOE_ROWS
    n_rows = n_blocks * MOE_ROWS
    n_items = N_EXPERTS + n_assign // (MOE_ROWS * MOE_SUBS_PER_ITEM)

    eid_t = eid[:, :TOP_K].T
    counts = _expert_counts(eid_t)[:, 0].astype(I32)
    nsub_e, pstart, nitem_e, item_end = _moe_tables(counts)
    dest_t = _expert_slots(eid_t, pstart.astype(F32).reshape(N_EXPERTS, 1))
    dest_flat = dest_t.T.reshape(-1)

    it = jnp.arange(n_items, dtype=I32)
    total_items = item_end[-1]
    item_e = jnp.minimum(jnp.sum(item_end[None, :] <= it[:, None], axis=1).astype(I32), N_EXPERTS - 1)
    j = it - (item_end - nitem_e)[item_e]
    used = it < total_items
    last_e = item_e[jnp.maximum(total_items - 1, 0)]
    item_nsub = jnp.where(used, jnp.clip(nsub_e[item_e] - j * MOE_SUBS_PER_ITEM, 0, MOE_SUBS_PER_ITEM), 0).astype(I32)
    item_row = jnp.where(used, pstart[item_e] + j * (MOE_ROWS * MOE_SUBS_PER_ITEM), 0).astype(I32)
    item_e = jnp.where(used, item_e, last_e).astype(I32)

    pad_row = (pstart + counts).astype(I32)
    pad_len = (nsub_e * MOE_ROWS - counts).astype(I32)
    used_rows = jnp.sum(nsub_e) * MOE_ROWS
    tail = jnp.stack([used_rows, (n_rows - used_rows) // MOE_ROWS]).astype(I32)
    xb = _dispatch(dest_flat, pad_row, pad_len, tail, h_rows, m, n_rows)
    yb = _experts(item_e, item_row, item_nsub, w_gate, w_up, w_down, xb)
    return _combine(dest_flat, yb, h, gates, ln_g, ln_b, alpha)


def _pad_cols(w, n):
    return jnp.pad(w, ((0, 0), (0, n - w.shape[1])))


def _pad_rows(w, n):
    return jnp.pad(w, ((0, n - w.shape[0]), (0, 0)))


def kernel(x, w_in, attn_sinks, rw_mu_rkv, rw_mu_wag, rw_w0, rw_w1, rw_w2, rw_a0, rw_a1, rw_a2, rw_g1, rw_g2, rw_k_k, rw_k_a, rw_r_k, rw_lnx_w, rw_lnx_b, p_attn, p_rwkv, w_o, ln1_g, ln1_b, w_group, b_group, w_expert, b_expert, w_gate, w_up, w_down, ln2_g, ln2_b):
    b, t, d = x.shape
    depth = w_in.shape[0]
    m = b * t
    alpha = (2.0 * depth) ** 0.25
    cosb, sinb = _rope_tables(t)
    qkv_w = ATTN_Q_W + 2 * ATTN_KV_W
    rkv_w = 3 * RWKV_W
    h = x
    for l in range(depth):
        hf = h.reshape(m, d)
        hb = hf.astype(BF16)
        qkv = _matmul_cols(hb, w_in[l], 0, qkv_w, F32)
        rkv = _matmul_cols(hb, w_in[l], qkv_w, rkv_w, BF16)
        gates = _matmul_cols(hb, w_in[l], qkv_w + rkv_w, 2 * d, BF16)

        y_a = _attention(qkv.reshape(b, t, qkv_w), attn_sinks[l], cosb, sinb)

        lora = lambda w, n: _pad_cols(w, n).astype(BF16)
        lorb = lambda w, n: _pad_rows(w, n).astype(BF16)
        n_w = -(-rw_w1.shape[2] // LANES) * LANES
        n_a = -(-rw_a1.shape[2] // LANES) * LANES
        n_g = -(-rw_g1.shape[2] // LANES) * LANES
        r_, k_, v_, lw_, cum_, ag_, g_ = _rwkv_prep(
            h, rkv.reshape(b, t, rkv_w), rw_mu_rkv[l], rw_mu_wag[l], rw_w0[l],
            lora(rw_w1[l], n_w), lorb(rw_w2[l], n_w), rw_a0[l], lora(rw_a1[l], n_a), lorb(rw_a2[l], n_a),
            lora(rw_g1[l], n_g), lorb(rw_g2[l], n_g))
        y_r = _wkv_scan(r_, k_, v_, lw_, cum_, ag_, g_, rw_k_k[l], rw_k_a[l], rw_r_k[l], rw_lnx_w[l], rw_lnx_b[l])

        merged = _merge(y_a.reshape(m, ATTN_Q_W), y_r.reshape(m, RWKV_W), gates,
                        p_attn[l].astype(BF16), p_rwkv[l].astype(BF16))
        w_router = _pad_cols(jnp.concatenate([w_group[l], w_expert[l]], axis=1), LANES)
        w_router_hi = w_router.astype(BF16)
        w_router = jnp.stack([w_router_hi, (w_router - w_router_hi.astype(F32)).astype(BF16)])
        b_router = _pad_cols(jnp.concatenate([b_group[l], b_expert[l]])[None, :], LANES)
        h1, h1_rows, eid, gate = _outproj_router(merged, hf, w_o[l].astype(BF16), ln1_g[l], ln1_b[l],
                                        w_router, b_router, alpha)
        h2 = _moe(h1, h1_rows, eid, gate, w_gate[l], w_up[l], w_down[l], ln2_g[l], ln2_b[l], alpha)
        h = h2.reshape(b, t, d)
    return h
```

```python
import functools

import numpy as np
import jax
import jax.numpy as jnp
from jax import lax
from jax.experimental import pallas as pl
from jax.experimental.pallas import tpu as pltpu

F32 = jnp.float32
BF16 = jnp.bfloat16
I32 = jnp.int32

HEAD_DIM = 64
ATTN_Q_HEADS = 16
ATTN_KV_HEADS = 4
ATTN_GROUP = ATTN_Q_HEADS // ATTN_KV_HEADS
ATTN_Q_W = ATTN_Q_HEADS * HEAD_DIM
ATTN_KV_W = ATTN_KV_HEADS * HEAD_DIM
WINDOW = 128
ROPE_THETA = 10000.0
RWKV_HEADS = 16
RWKV_N = 64
RWKV_W = RWKV_HEADS * RWKV_N
RWKV_GN_EPS = 64e-5
N_GROUPS = 8
EXPERTS_PER_GROUP = 8
N_EXPERTS = N_GROUPS * EXPERTS_PER_GROUP
TOP_K = 2
LN_EPS = 1e-5

LANES = 128
SUBLANES = 8
VMEM_LIMIT_BYTES = 56 * 1024 * 1024

WKV_CHUNK = 64
WKV_CHUNKS_PER_STEP = 4
WKV_HEADS_PER_STEP = 16
MOE_ROWS = 128
MOE_SUBS_PER_ITEM = 4
MOE_K_STEPS = 4
MOE_WEIGHT_SLOTS = 3
ROUTE_TILE = 512


def _cparams(sem, vmem=VMEM_LIMIT_BYTES):
    return pltpu.CompilerParams(dimension_semantics=sem, vmem_limit_bytes=vmem)


def _sigmoid(x):
    return 1.0 / (1.0 + jnp.exp(-x))


def _dot(a, b):
    return jnp.dot(a.astype(BF16), b.astype(BF16), preferred_element_type=F32)


def _dot_nt(a, b):
    return lax.dot_general(a.astype(BF16), b.astype(BF16), (((1,), (1,)), ((), ())),
                           preferred_element_type=F32)


def _layer_norm(t, g, b):
    mu = jnp.mean(t, axis=-1, keepdims=True)
    d = t - mu
    var = jnp.mean(d * d, axis=-1, keepdims=True)
    return d * lax.rsqrt(var + LN_EPS) * g + b


def _matmul_kernel(a_ref, b_ref, o_ref):
    o_ref[...] = jnp.dot(a_ref[...], b_ref[...].astype(BF16), preferred_element_type=F32).astype(o_ref.dtype)


def _matmul_cols(a, b, col0, ncols, out_dtype, tm=2048, tn=512):
    m, k = a.shape
    tm = min(tm, m)
    cb = col0 // tn
    assert col0 % tn == 0 and ncols % tn == 0 and m % tm == 0
    return pl.pallas_call(
        _matmul_kernel,
        grid=(m // tm, ncols // tn),
        in_specs=[pl.BlockSpec((tm, k), lambda i, j: (i, 0)),
                  pl.BlockSpec((k, tn), lambda i, j: (0, j + cb))],
        out_specs=pl.BlockSpec((tm, tn), lambda i, j: (i, j)),
        out_shape=jax.ShapeDtypeStruct((m, ncols), out_dtype),
        compiler_params=_cparams(("parallel", "arbitrary")),
        name="inproj_matmul",
    )(a, b)


def _rope(x, cosb, sinb):
    half = HEAD_DIM // 2
    lane = lax.broadcasted_iota(I32, cosb.shape, 1)
    first_half = (lane % HEAD_DIM) < half
    outs = []
    for g in range(x.shape[1] // LANES):
        xg = x[:, g * LANES:(g + 1) * LANES]
        partner = jnp.where(first_half, pltpu.roll(xg, LANES - half, axis=1), pltpu.roll(xg, half, axis=1))
        outs.append(xg * cosb + partner * sinb)
    return outs


def _attn_kernel(sinks_ref, q_ref, kc_ref, kp_ref, vc_ref, vp_ref, cosc_ref, sinc_ref, cosp_ref, sinp_ref, bias_ref,
                 o_ref):
    tq = q_ref.shape[1]
    qg = _rope(q_ref[0], cosc_ref[...], sinc_ref[...])
    kcg = _rope(kc_ref[0], cosc_ref[...], sinc_ref[...])
    kpg = _rope(kp_ref[0], cosp_ref[...], sinp_ref[...])
    vc = vc_ref[0]
    vp = vp_ref[0]

    def head(groups, h):
        g = groups[h // 2]
        return g[:, (h % 2) * HEAD_DIM:(h % 2 + 1) * HEAD_DIM]

    rows = ATTN_GROUP * tq
    bias = bias_ref[0]
    rid = lax.broadcasted_iota(I32, (rows, 1), 0) // tq
    scale = HEAD_DIM ** -0.5
    for kvh in range(ATTN_KV_HEADS):
        qh = jnp.concatenate([head(qg, kvh * ATTN_GROUP + g) for g in range(ATTN_GROUP)], axis=0)
        kw = jnp.concatenate([head(kpg, kvh), head(kcg, kvh)], axis=0)
        vw = jnp.concatenate([vp[:, kvh * HEAD_DIM:(kvh + 1) * HEAD_DIM],
                              vc[:, kvh * HEAD_DIM:(kvh + 1) * HEAD_DIM]], axis=0)
        s = _dot_nt(qh, kw) * scale + bias
        sink = jnp.zeros((rows, 1), F32)
        for g in range(ATTN_GROUP):
            sink = jnp.where(rid == g, sinks_ref[kvh * ATTN_GROUP + g], sink)
        m = jnp.maximum(jnp.max(s, axis=-1, keepdims=True), sink)
        e = jnp.exp(s - m)
        denom = jnp.sum(e, axis=-1, keepdims=True) + jnp.exp(sink - m)
        o = _dot(e, vw) / denom
        for g in range(ATTN_GROUP):
            hq = kvh * ATTN_GROUP + g
            o_ref[0, :, hq * HEAD_DIM:(hq + 1) * HEAD_DIM] = o[g * tq:(g + 1) * tq].astype(o_ref.dtype)


def _attention(qkv, sinks, cosb, sinb):
    b, t, _ = qkv.shape
    tq = WINDOW
    nb = t // tq
    kcol = ATTN_Q_W // ATTN_KV_W
    prev = lambda i: jnp.maximum(i - 1, 0)
    grid_spec = pltpu.PrefetchScalarGridSpec(
        num_scalar_prefetch=0,
        grid=(b, nb),
        in_specs=[
            pl.BlockSpec(memory_space=pltpu.SMEM),
            pl.BlockSpec((1, tq, ATTN_Q_W), lambda bi, i: (bi, i, 0)),
            pl.BlockSpec((1, tq, ATTN_KV_W), lambda bi, i: (bi, i, kcol)),
            pl.BlockSpec((1, tq, ATTN_KV_W), lambda bi, i: (bi, prev(i), kcol)),
            pl.BlockSpec((1, tq, ATTN_KV_W), lambda bi, i: (bi, i, kcol + 1)),
            pl.BlockSpec((1, tq, ATTN_KV_W), lambda bi, i: (bi, prev(i), kcol + 1)),
            pl.BlockSpec((tq, LANES), lambda bi, i: (i, 0)),
            pl.BlockSpec((tq, LANES), lambda bi, i: (i, 0)),
            pl.BlockSpec((tq, LANES), lambda bi, i: (prev(i), 0)),
            pl.BlockSpec((tq, LANES), lambda bi, i: (prev(i), 0)),
            pl.BlockSpec((1, ATTN_GROUP * tq, 2 * tq), lambda bi, i: (jnp.minimum(i, 1), 0, 0)),
        ],
        out_specs=pl.BlockSpec((1, tq, ATTN_Q_W), lambda bi, i: (bi, i, 0)),
    )
    return pl.pallas_call(
        _attn_kernel,
        grid_spec=grid_spec,
        out_shape=jax.ShapeDtypeStruct((b, t, ATTN_Q_W), BF16),
        compiler_params=_cparams(("parallel", "arbitrary")),
        name="swa_attention",
    )(sinks, qkv, qkv, qkv, qkv, qkv, cosb, sinb, cosb, sinb, _band_bias(tq))


def _band_bias(tq):
    qi = np.arange(ATTN_GROUP * tq)[:, None] % tq
    kj = np.arange(2 * tq)[None, :]
    dist = qi + tq - kj
    band = (dist >= 0) & (dist < WINDOW)
    first = band & (kj >= tq)
    return jnp.asarray(np.where(np.stack([first, band]), 0.0, -np.inf), F32)


def _rope_tables(t):
    inv = 1.0 / (ROPE_THETA ** (np.arange(0, HEAD_DIM, 2, dtype=np.float64) / HEAD_DIM))
    ang = np.arange(t, dtype=np.float64)[:, None] * inv[None, :]
    cos, sin = np.cos(ang), np.sin(ang)
    reps = LANES // HEAD_DIM
    cosb = np.tile(np.concatenate([cos, cos], axis=-1), (1, reps))
    sinb = np.tile(np.concatenate([-sin, sin], axis=-1), (1, reps))
    return jnp.asarray(cosb, F32), jnp.asarray(sinb, F32)


def _rwkv_prep_kernel(h_ref, hp_ref, r_ref, k_ref, v_ref, rp_ref, kp_ref, vp_ref,
                      mu_rkv_ref, mu_wag_ref, w0_ref, w1_ref, w2_ref, a0_ref, a1_ref, a2_ref,
                      g1_ref, g2_ref,
                      ro_ref, ko_ref, vo_ref, lwo_ref, cumo_ref, ago_ref, go_ref):
    first = pl.program_id(1) == 0

    def shifted(cur, prev_ref):
        last_row = prev_ref.shape[1] - 1
        prev_row = jnp.where(first, 0.0, prev_ref[0, last_row:last_row + 1, :].astype(F32))
        rowid = lax.broadcasted_iota(I32, cur.shape, 0)
        return jnp.where(rowid == 0, prev_row, pltpu.roll(cur, 1, axis=0))

    h = h_ref[0]
    xx = shifted(h, hp_ref) - h
    xw = h + xx * mu_wag_ref[0:1, :]
    xa = h + xx * mu_wag_ref[1:2, :]
    xg = h + xx * mu_wag_ref[2:3, :]
    w_raw = w0_ref[...] + _dot(jnp.tanh(_dot(xw, w1_ref[...])), w2_ref[...])
    neg = -w_raw
    softplus = jnp.maximum(neg, 0.0) + jnp.log1p(jnp.exp(-jnp.abs(neg)))
    w = -softplus - 0.5
    lw = -jnp.exp(w)
    tm = lw.shape[0]
    row = lax.broadcasted_iota(I32, (tm, tm), 0)
    col = lax.broadcasted_iota(I32, (tm, tm), 1)
    tri = ((row >= col) & (row // WKV_CHUNK == col // WKV_CHUNK)).astype(BF16)
    cum = sum(jnp.dot(tri, piece, preferred_element_type=F32) for piece in _split3(lw))
    ag = _sigmoid(a0_ref[...] + _dot(_dot(xa, a1_ref[...]), a2_ref[...]))
    go_ref[0] = _dot(_sigmoid(_dot(xg, g1_ref[...])), g2_ref[...])

    r = r_ref[0].astype(F32)
    k = k_ref[0].astype(F32)
    v = v_ref[0].astype(F32)
    r = r + (shifted(r, rp_ref) - r) * mu_rkv_ref[0:1, :]
    k = k + (shifted(k, kp_ref) - k) * mu_rkv_ref[1:2, :]
    v = v + (shifted(v, vp_ref) - v) * mu_rkv_ref[2:3, :]
    nchunk = tm // WKV_CHUNK
    for hd in range(RWKV_HEADS):
        sl = slice(hd * RWKV_N, (hd + 1) * RWKV_N)
        for ref, val in ((ro_ref, r), (ko_ref, k), (vo_ref, v), (lwo_ref, lw), (cumo_ref, cum), (ago_ref, ag)):
            ref[0, :, hd] = val[:, sl].reshape(nchunk, WKV_CHUNK, RWKV_N).astype(ref.dtype)


def _rwkv_prep(h, rkv, mu_rkv, mu_wag, w0, w1, w2, a0, a1, a2, g1, g2, tm=256):
    b, t, d = h.shape
    tm = min(tm, t)
    c = RWKV_W
    prevblk = lambda i, rows: jnp.maximum(i * (tm // rows) - 1, 0)
    rkv_rows = SUBLANES * (4 // rkv.dtype.itemsize)
    full = lambda arr: pl.BlockSpec(arr.shape, lambda bi, i: (0,) * arr.ndim)
    row = lambda arr: arr.reshape(1, -1)
    w0, a0 = row(w0), row(a0)
    in_specs = [
        pl.BlockSpec((1, tm, d), lambda bi, i: (bi, i, 0)),
        pl.BlockSpec((1, SUBLANES, d), lambda bi, i: (bi, prevblk(i, SUBLANES), 0)),
        pl.BlockSpec((1, tm, c), lambda bi, i: (bi, i, 0)),
        pl.BlockSpec((1, tm, c), lambda bi, i: (bi, i, 1)),
        pl.BlockSpec((1, tm, c), lambda bi, i: (bi, i, 2)),
        pl.BlockSpec((1, rkv_rows, c), lambda bi, i: (bi, prevblk(i, rkv_rows), 0)),
        pl.BlockSpec((1, rkv_rows, c), lambda bi, i: (bi, prevblk(i, rkv_rows), 1)),
        pl.BlockSpec((1, rkv_rows, c), lambda bi, i: (bi, prevblk(i, rkv_rows), 2)),
    ] + [full(a) for a in (mu_rkv, mu_wag, w0, w1, w2, a0, a1, a2, g1, g2)]
    assert tm % WKV_CHUNK == 0
    hm = lambda dt: jax.ShapeDtypeStruct((b, t // WKV_CHUNK, RWKV_HEADS, WKV_CHUNK, RWKV_N), dt)
    hm_spec = pl.BlockSpec((1, tm // WKV_CHUNK, RWKV_HEADS, WKV_CHUNK, RWKV_N), lambda bi, i: (bi, i, 0, 0, 0))
    return pl.pallas_call(
        _rwkv_prep_kernel,
        grid=(b, t // tm),
        in_specs=in_specs,
        out_specs=[hm_spec] * 6 + [pl.BlockSpec((1, tm, c), lambda bi, i: (bi, i, 0))],
        out_shape=[hm(BF16), hm(BF16), hm(BF16), hm(F32), hm(F32), hm(BF16), jax.ShapeDtypeStruct((b, t, c), F32)],
        compiler_params=_cparams(("parallel", "arbitrary")),
        name="rwkv_prep",
    )(h, h, rkv, rkv, rkv, rkv, rkv, rkv, mu_rkv, mu_wag, w0, w1, w2, a0, a1, a2, g1, g2)


def _split3(x):
    hi = x.astype(BF16)
    r1 = x - hi.astype(F32)
    mid = r1.astype(BF16)
    lo = (r1 - mid.astype(F32)).astype(BF16)
    return hi, mid, lo


def _bmm(a, b):
    return jnp.einsum("gmk,gkn->gmn", a.astype(BF16), b.astype(BF16), preferred_element_type=F32)


def _bmm_nt(a, b):
    return jnp.einsum("gmk,gnk->gmn", a.astype(BF16), b.astype(BF16), preferred_element_type=F32)


def _bmm_tn(a, b):
    return jnp.einsum("gtm,gtn->gmn", a.astype(BF16), b.astype(BF16), preferred_element_type=F32)


def _wkv_kernel(r_ref, k_ref, v_ref, lw_ref, cum_ref, ag_ref, g_ref, kk_ref, ka_ref, rk_ref, lnw_ref, lnb_ref,
                y_ref, s_ref, st_ref):
    c = WKV_CHUNK
    n = RWKV_N
    nc, hb = r_ref.shape[1], r_ref.shape[2]
    tc = nc * c
    g = nc * hb

    @pl.when(pl.program_id(2) == 0)
    def _():
        s_ref[...] = jnp.zeros_like(s_ref)

    chunks = lambda ref: ref[0].reshape(g, c, n).astype(F32)
    r, k, v, lw, cum, ag = (chunks(ref) for ref in (r_ref, k_ref, v_ref, lw_ref, cum_ref, ag_ref))
    per_head = lambda t: t.reshape(nc, hb, c, n)
    kk = (per_head(k) * kk_ref[...]).reshape(g, c, n)
    k = (per_head(k) * (1.0 + (per_head(ag) - 1.0) * ka_ref[...])).reshape(g, c, n)
    kk = kk / jnp.maximum(jnp.sqrt(jnp.sum(kk * kk, axis=-1, keepdims=True)), 1e-12)
    dinc = jnp.exp(cum)
    dinv = jnp.exp(-cum)
    rt = r * dinc
    kt = k * dinv
    at = -kk * jnp.exp(cum - lw)
    bt = kk * ag * dinv
    d_chunk = dinc[:, c - 1:c, :]

    row = lax.broadcasted_iota(I32, (1, c, c), 1)
    col = lax.broadcasted_iota(I32, (1, c, c), 2)
    strict = row > col
    incl = row >= col
    eye = jnp.broadcast_to((row == col).astype(F32), (g, c, c))
    p = _bmm_nt(jnp.concatenate([at, rt], axis=1), jnp.concatenate([bt, kt], axis=1))
    a_ab = jnp.where(strict, p[:, :c, :c], 0.0)
    a_ak = jnp.where(strict, p[:, :c, c:], 0.0)
    a_rb = jnp.where(incl, p[:, c:, :c], 0.0)
    a_rk = jnp.where(incl, p[:, c:, c:], 0.0)
    x = _bmm(a_ab, a_ab)
    tinv = eye + a_ab
    levels = c.bit_length() - 2
    for j in range(1, levels):
        both = _bmm(jnp.concatenate([x, tinv], axis=1), x)
        x = both[:, :c]
        tinv = tinv + both[:, c:]
    tinv = tinv + _bmm(tinv, x)
    z = _bmm(a_ak, v)
    ta = _bmm(tinv, jnp.concatenate([at, z], axis=2))
    ry = _bmm(a_rb, ta)
    rp = rt + ry[:, :, :n]
    yv = ry[:, :, n:] + _bmm(a_rk, v)
    moff = _bmm_tn(ta[:, :, :n], bt).reshape(nc, hb, n, n)
    n2 = _bmm_tn(jnp.concatenate([ta[:, :, n:], v], axis=1),
                 jnp.concatenate([bt, kt], axis=1)).reshape(nc, hb, n, n)
    dch = d_chunk.reshape(nc, hb, 1, n)

    s = s_ref[...]
    for ci in range(nc):
        st_ref[ci] = s
        s = (s + _bmm(s, moff[ci]) + n2[ci]) * dch[ci]
    s_ref[...] = s

    y = _bmm_nt(rp, st_ref[...].reshape(g, n, n)) + yv
    mu = jnp.mean(y, axis=-1, keepdims=True)
    yc = y - mu
    var = jnp.mean(yc * yc, axis=-1, keepdims=True)
    y = per_head(yc * lax.rsqrt(var + RWKV_GN_EPS)) * lnw_ref[...] + lnb_ref[...]
    bonus = jnp.sum(per_head(r * k) * rk_ref[...], axis=-1, keepdims=True)
    y = y + bonus * per_head(v)
    y = jnp.concatenate([jnp.concatenate([y[ci, hd] for hd in range(hb)], axis=1) for ci in range(nc)], axis=0)
    y_ref[0] = (y * g_ref[0]).astype(y_ref.dtype)


def _wkv_scan(r, k, v, lw, cum, ag, g, k_k, k_a, r_k, lnx_w, lnx_b):
    b, nchunks, hh, c, n = r.shape
    t = nchunks * c
    nc = min(WKV_CHUNKS_PER_STEP, nchunks)
    tc = nc * c
    hb = WKV_HEADS_PER_STEP
    blk = pl.BlockSpec((1, nc, hb, c, n), lambda bi, hi, i: (bi, i, hi, 0, 0))
    tok = pl.BlockSpec((1, tc, hb * n), lambda bi, hi, i: (bi, i, hi))
    par = pl.BlockSpec((hb, 1, n), lambda bi, hi, i: (hi, 0, 0))
    per_head = lambda arr: arr.reshape(hh, 1, n)
    return pl.pallas_call(
        _wkv_kernel,
        grid=(b, hh // hb, t // tc),
        in_specs=[blk] * 6 + [tok] + [par] * 5,
        out_specs=tok,
        out_shape=jax.ShapeDtypeStruct((b, t, hh * n), BF16),
        scratch_shapes=[pltpu.VMEM((hb, n, n), F32), pltpu.VMEM((nc, hb, n, n), F32)],
        compiler_params=_cparams(("parallel", "parallel", "arbitrary")),
        name="wkv7_scan",
    )(r, k, v, lw, cum, ag, g, per_head(k_k), per_head(k_a), per_head(r_k), per_head(lnx_w), per_head(lnx_b))


def _merge_kernel(ya_ref, yr_ref, ga_ref, gr_ref, pa_ref, pr_ref, o_ref):
    ma = jnp.dot(ya_ref[...], pa_ref[...], preferred_element_type=F32)
    mr = jnp.dot(yr_ref[...], pr_ref[...], preferred_element_type=F32)
    o_ref[...] = (_sigmoid(ga_ref[...].astype(F32)) * ma + _sigmoid(gr_ref[...].astype(F32)) * mr).astype(o_ref.dtype)


def _merge(ya, yr, gates, p_attn, p_rwkv, tm=256):
    m, c = ya.shape
    d = p_attn.shape[1]
    tm = min(tm, m)
    tile = pl.BlockSpec((tm, c), lambda i: (i, 0))
    return pl.pallas_call(
        _merge_kernel,
        grid=(m // tm,),
        in_specs=[tile, tile,
                  pl.BlockSpec((tm, d), lambda i: (i, 0)),
                  pl.BlockSpec((tm, d), lambda i: (i, 1)),
                  pl.BlockSpec((c, d), lambda i: (0, 0)),
                  pl.BlockSpec((c, d), lambda i: (0, 0))],
        out_specs=pl.BlockSpec((tm, d), lambda i: (i, 0)),
        out_shape=jax.ShapeDtypeStruct((m, d), BF16),
        compiler_params=_cparams(("parallel",)),
        name="gated_merge",
    )(ya, yr, gates, gates, p_attn, p_rwkv)


def _outproj_router_kernel(alpha, mg_ref, x_ref, wo_ref, g_ref, b_ref, wr_ref, br_ref,
                           h_ref, hrow_ref, eid_ref, gate_ref):
    mix = jnp.dot(mg_ref[...], wo_ref[...], preferred_element_type=F32)
    h = _layer_norm(alpha * x_ref[...] + mix, g_ref[...], b_ref[...])
    h_ref[...] = h
    slab = h.shape[1] // LANES
    for j in range(slab):
        hrow_ref[pl.ds(j, h.shape[0], stride=slab), :] = h[:, j * LANES:(j + 1) * LANES]
    h_hi = h.astype(BF16)
    h_lo = (h - h_hi.astype(F32)).astype(BF16)
    logits = (jnp.dot(h_hi, wr_ref[0], preferred_element_type=F32)
              + jnp.dot(h_lo, wr_ref[0], preferred_element_type=F32)
              + jnp.dot(h_hi, wr_ref[1], preferred_element_type=F32)) + br_ref[...]
    lane = lax.broadcasted_iota(I32, logits.shape, 1)
    ninf = -jnp.inf
    big = jnp.int32(2 * LANES)
    glog = jnp.where(lane < N_GROUPS, logits, ninf)
    gmax = jnp.max(glog, axis=-1, keepdims=True)
    gidx = jnp.min(jnp.where(glog == gmax, lane, big), axis=-1, keepdims=True)
    gtop = 1.0 / jnp.sum(jnp.exp(glog - gmax), axis=-1, keepdims=True)
    eg = (lane - N_GROUPS) // EXPERTS_PER_GROUP
    in_group = (lane >= N_GROUPS) & (lane < N_GROUPS + N_EXPERTS) & (eg == gidx)
    el = jnp.where(in_group, logits, ninf)
    m1 = jnp.max(el, axis=-1, keepdims=True)
    i1 = jnp.min(jnp.where(el == m1, lane, big), axis=-1, keepdims=True)
    el2 = jnp.where(lane == i1, ninf, el)
    m2 = jnp.max(el2, axis=-1, keepdims=True)
    i2 = jnp.min(jnp.where(el2 == m2, lane, big), axis=-1, keepdims=True)
    t = jnp.exp(m2 - m1)
    p1 = 1.0 / (1.0 + t)
    p2 = t / (1.0 + t)
    eid_ref[...] = jnp.where(lane == 0, i1 - N_GROUPS, jnp.where(lane == 1, i2 - N_GROUPS, 0))
    gate_ref[...] = jnp.where(lane == 0, gtop * p1, jnp.where(lane == 1, gtop * p2, 0.0))


def _outproj_router(merged, x, w_o, ln_g, ln_b, w_router, b_router, alpha, tm=512):
    m, d = x.shape
    tm = min(tm, m)
    tile = pl.BlockSpec((tm, d), lambda i: (i, 0))
    vec = pl.BlockSpec((1, d), lambda i: (0, 0))
    small = pl.BlockSpec((tm, LANES), lambda i: (i, 0))
    return pl.pallas_call(
        functools.partial(_outproj_router_kernel, alpha),
        grid=(m // tm,),
        in_specs=[tile, tile, pl.BlockSpec((d, d), lambda i: (0, 0)), vec, vec,
                  pl.BlockSpec((2, d, LANES), lambda i: (0, 0, 0)), pl.BlockSpec((1, LANES), lambda i: (0, 0))],
        out_specs=[tile, pl.BlockSpec((tm * (d // LANES), LANES), lambda i: (i, 0)), small, small],
        out_shape=[jax.ShapeDtypeStruct((m, d), F32), jax.ShapeDtypeStruct((m * (d // LANES), LANES), F32),
                   jax.ShapeDtypeStruct((m, LANES), I32), jax.ShapeDtypeStruct((m, LANES), F32)],
        compiler_params=_cparams(("parallel",)),
        name="outproj_ln_router",
    )(merged, x, w_o, ln_g.reshape(1, d), ln_b.reshape(1, d), w_router, b_router)


def _onehots(eid_ref):
    tm = eid_ref.shape[1]
    e_iota = lax.broadcasted_iota(I32, (N_EXPERTS, tm), 0)
    oh0 = (eid_ref[0:1, :] == e_iota).astype(F32)
    oh1 = (eid_ref[1:2, :] == e_iota).astype(F32)
    return oh0, oh1


def _count_kernel(eid_ref, cnt_ref):
    @pl.when(pl.program_id(0) == 0)
    def _():
        cnt_ref[...] = jnp.zeros_like(cnt_ref)

    oh0, oh1 = _onehots(eid_ref)
    cnt_ref[...] += jnp.sum(oh0 + oh1, axis=1, keepdims=True)


def _slot_kernel(eid_ref, pstart_ref, dest_ref, run_ref):
    @pl.when(pl.program_id(0) == 0)
    def _():
        run_ref[...] = jnp.zeros_like(run_ref)

    tm = eid_ref.shape[1]
    oh0, oh1 = _onehots(eid_ref)
    both = oh0 + oh1
    earlier = (lax.broadcasted_iota(I32, (tm, tm), 0) < lax.broadcasted_iota(I32, (tm, tm), 1)).astype(BF16)
    pre = jnp.dot(both.astype(BF16), earlier, preferred_element_type=F32)
    base = pre + run_ref[...] + pstart_ref[...]
    dest_ref[0:1, :] = jnp.sum(oh0 * base, axis=0, keepdims=True).astype(I32)
    dest_ref[1:2, :] = jnp.sum(oh1 * base, axis=0, keepdims=True).astype(I32)
    run_ref[...] += jnp.sum(both, axis=1, keepdims=True)


def _expert_counts(eid_t):
    m = eid_t.shape[1]
    tm = min(ROUTE_TILE, m)
    return pl.pallas_call(
        _count_kernel,
        grid=(m // tm,),
        in_specs=[pl.BlockSpec((TOP_K, tm), lambda i: (0, i))],
        out_specs=pl.BlockSpec((N_EXPERTS, 1), lambda i: (0, 0)),
        out_shape=jax.ShapeDtypeStruct((N_EXPERTS, 1), F32),
        compiler_params=_cparams(("arbitrary",)),
        name="expert_counts",
    )(eid_t)


def _expert_slots(eid_t, pstart):
    m = eid_t.shape[1]
    tm = min(ROUTE_TILE, m)
    return pl.pallas_call(
        _slot_kernel,
        grid=(m // tm,),
        in_specs=[pl.BlockSpec((TOP_K, tm), lambda i: (0, i)),
                  pl.BlockSpec((N_EXPERTS, 1), lambda i: (0, 0))],
        out_specs=pl.BlockSpec((TOP_K, tm), lambda i: (0, i)),
        out_shape=jax.ShapeDtypeStruct((TOP_K, m), I32),
        scratch_shapes=[pltpu.VMEM((N_EXPERTS, 1), F32)],
        compiler_params=_cparams(("arbitrary",)),
        name="expert_slots",
    )(eid_t, pstart)


def _row_slab(ref, row, slab):
    return ref.at[pl.ds(pl.multiple_of(row * slab, slab), slab), :]


def _dispatch_kernel(dest_ref, pad_row_ref, pad_len_ref, tail_ref, h_ref, xb_ref, zero_ref, sem, zero_sem):
    tm = DISPATCH_TILE
    slab = h_ref.shape[0] // tm
    rb = MOE_ROWS
    step = pl.program_id(0)
    t0 = step * tm

    def zero_rows(row, n_rows):
        return pltpu.make_async_copy(zero_ref.at[pl.ds(0, n_rows * slab), :],
                                     xb_ref.at[pl.ds(pl.multiple_of(row * slab, slab), n_rows * slab), :], zero_sem)

    def zero_fill(op):
        def pad_run(e, carry):
            row = pad_row_ref[e]
            length = pad_len_ref[e]
            piece = rb // 2
            while piece >= 1:
                take = (length & piece) != 0

                @pl.when(take)
                def _(row=row, piece=piece):
                    op(zero_rows(row, piece))
                row = row + jnp.where(take, piece, 0)
                piece //= 2
            return carry
        lax.fori_loop(0, pad_len_ref.shape[0], pad_run, 0)

        def tail_block(tb, carry):
            op(zero_rows(tail_ref[0] + tb * rb, rb))
            return carry
        lax.fori_loop(0, tail_ref[1], tail_block, 0)

    @pl.when(step == 0)
    def _():
        zero_ref[...] = jnp.zeros_like(zero_ref)
        zero_fill(lambda c: c.start())

    def copy(n, k):
        return pltpu.make_async_copy(_row_slab(h_ref, n, slab),
                                     _row_slab(xb_ref, dest_ref[TOP_K * (t0 + n) + k], slab), sem)

    def start(n, carry):
        for k in range(TOP_K):
            copy(n, k).start(priority=k % 2)
        return carry

    lax.fori_loop(0, tm, start, 0, unroll=8)
    for k in range(TOP_K):
        pltpu.make_async_copy(h_ref, xb_ref.at[pl.ds(0, h_ref.shape[0]), :], sem).wait()

    @pl.when(step == pl.num_programs(0) - 1)
    def _():
        zero_fill(lambda c: c.wait())


DISPATCH_TILE = 512


def _dispatch(dest_flat, pad_row, pad_len, tail, h_rows, m, n_rows):
    slab = h_rows.shape[0] // m
    assert m % DISPATCH_TILE == 0
    grid_spec = pltpu.PrefetchScalarGridSpec(
        num_scalar_prefetch=4,
        grid=(m // DISPATCH_TILE,),
        in_specs=[pl.BlockSpec((DISPATCH_TILE * slab, LANES), lambda i, *_: (i, 0))],
        out_specs=pl.BlockSpec(memory_space=pl.ANY),
        scratch_shapes=[pltpu.VMEM((MOE_ROWS * slab, LANES), F32), pltpu.SemaphoreType.DMA(()),
                        pltpu.SemaphoreType.DMA(())],
    )
    return pl.pallas_call(
        _dispatch_kernel,
        grid_spec=grid_spec,
        out_shape=jax.ShapeDtypeStruct((n_rows * slab, LANES), F32),
        compiler_params=pltpu.CompilerParams(dimension_semantics=("arbitrary",), has_side_effects=True),
        name="moe_dispatch",
    )(dest_flat, pad_row, pad_len, tail, h_rows)


def _expert_kernel(item_e_ref, item_row_ref, item_nsub_ref, wg_hbm, wu_hbm, wd_hbm, xb_ref, yb_ref,
                   wg_buf, wu_buf, wd_buf, xin_ref, yout_ref, x_ref, acc_ref, wgu_ref, wdb_ref, sem_w, sem_in, sem_out):
    it = pl.program_id(0)
    f = pl.program_id(1)
    n_items = pl.num_programs(0)
    nf = pl.num_programs(1)
    kc = wg_buf.shape[1]
    ff = wg_buf.shape[2]
    fc = wd_buf.shape[1]
    nsub = item_nsub_ref[it]
    row0 = item_row_ref[it]
    nxt = jnp.minimum(it + 1, n_items - 1)
    nsub_next = jnp.where(it + 1 < n_items, item_nsub_ref[nxt], 0)
    prv = jnp.maximum(it - 1, 0)
    nsub_prev = jnp.where(it > 0, item_nsub_ref[prv], 0)
    rb = MOE_ROWS
    slab = x_ref.shape[0] * kc // LANES
    blk = rb * slab

    def stage_rows(ref, s):
        return ref.at[pl.ds(pl.multiple_of(s * blk, blk), blk), :]

    def hbm_rows(ref, item_row, s):
        return ref.at[pl.ds(pl.multiple_of((item_row + s * rb) * slab, blk), blk), :]

    def in_copy(item_row, s):
        return pltpu.make_async_copy(hbm_rows(xb_ref, item_row, s), stage_rows(xin_ref, s), sem_in)

    def out_copy(item_row, s):
        return pltpu.make_async_copy(stage_rows(yout_ref, s), hbm_rows(yb_ref, item_row, s), sem_out)

    def loop(n, fn):
        def body(s, carry):
            fn(s)
            return carry
        lax.fori_loop(0, n, body, 0)

    def weight_copies(step):
        item = step // nf
        k = step % nf
        e = item_e_ref[jnp.minimum(item, n_items - 1)]
        slot = step % MOE_WEIGHT_SLOTS
        return (pltpu.make_async_copy(wg_hbm.at[e, pl.ds(pl.multiple_of(k * kc, kc), kc), :], wg_buf.at[slot],
                                      sem_w.at[slot]),
                pltpu.make_async_copy(wu_hbm.at[e, pl.ds(pl.multiple_of(k * kc, kc), kc), :], wu_buf.at[slot],
                                      sem_w.at[slot]),
                pltpu.make_async_copy(wd_hbm.at[e, pl.ds(pl.multiple_of(k * fc, SUBLANES), fc), :], wd_buf.at[slot],
                                      sem_w.at[slot]))

    def fetch_weights(step):
        item = step // nf
        exists = (item < n_items) & (item_nsub_ref[jnp.minimum(item, n_items - 1)] > 0)

        @pl.when(exists)
        def _():
            for c in weight_copies(step):
                c.start()

    @pl.when(nsub > 0)
    def _():
        this_step = it * nf + f

        @pl.when(this_step == 0)
        def _():
            for ahead in range(MOE_WEIGHT_SLOTS - 1):
                fetch_weights(ahead)
        fetch_weights(this_step + MOE_WEIGHT_SLOTS - 1)

        @pl.when(f == 0)
        def _():
            @pl.when(it == 0)
            def _():
                x_ref[...] = jnp.zeros_like(x_ref)
                loop(nsub, lambda s: in_copy(row0, s).start())
            loop(nsub, lambda s: in_copy(row0, s).wait())

            def to_matrix(s):
                rows = pl.ds(pl.multiple_of(s * rb, rb), rb)
                base = pl.multiple_of(s * blk, blk)
                for j in range(slab):
                    c0 = (j * LANES) % kc
                    x_ref[(j * LANES) // kc, rows, c0:c0 + LANES] = (
                        xin_ref[pl.ds(base + j, rb, stride=slab), :].astype(BF16))
            loop(nsub, to_matrix)
            loop(nsub_next, lambda s: in_copy(item_row_ref[nxt], s).start())

        for c in weight_copies(this_step):
            c.wait()
        slot = this_step % MOE_WEIGHT_SLOTS
        wgu_ref[:, :ff] = wg_buf[slot].astype(BF16)
        wgu_ref[:, ff:] = wu_buf[slot].astype(BF16)
        wdb_ref[pl.ds(pl.multiple_of(f * fc, 2 * SUBLANES), fc), :] = wd_buf[slot].astype(BF16)

        def gate_up(start, size):
            rows = pl.ds(pl.multiple_of(start, rb), size)
            part = jnp.dot(x_ref[f, rows, :], wgu_ref[...], preferred_element_type=F32)
            acc_ref[rows, :] = jnp.where(f > 0, acc_ref[rows, :], 0.0) + part

        gate_up(0, 2 * rb)

        def pair(p, carry):
            gate_up(p * (2 * rb), 2 * rb)
            return carry
        lax.fori_loop(1, nsub // 2, pair, 0)

        @pl.when((nsub % 2 == 1) & (nsub > 1))
        def _():
            gate_up((nsub - 1) * rb, rb)

        @pl.when(f == nf - 1)
        def _():
            loop(nsub_prev, lambda s: out_copy(item_row_ref[prv], s).wait())

            def down(start, size):
                rows = pl.ds(pl.multiple_of(start, rb), size)
                gate = acc_ref[rows, :ff]
                act = (gate * _sigmoid(gate) * acc_ref[rows, ff:]).astype(BF16)
                y = jnp.dot(act, wdb_ref[...], preferred_element_type=F32)
                base = pl.multiple_of(start * slab, blk)
                for j in range(slab):
                    yout_ref[pl.ds(base + j, size, stride=slab), :] = y[:, j * LANES:(j + 1) * LANES]

            down(0, 2 * rb)

            def down_pair(p, carry):
                down(p * (2 * rb), 2 * rb)
                return carry
            lax.fori_loop(1, nsub // 2, down_pair, 0)

            @pl.when((nsub % 2 == 1) & (nsub > 1))
            def _():
                down((nsub - 1) * rb, rb)

            loop(nsub, lambda s: out_copy(row0, s).start())

            @pl.when(nsub_next == 0)
            def _():
                loop(nsub, lambda s: out_copy(row0, s).wait())


def _experts(item_e, item_row, item_nsub, w_gate, w_up, w_down, xb):
    d = w_gate.shape[1]
    ff = w_gate.shape[2]
    nk = MOE_K_STEPS
    kc = d // nk
    fc = ff // nk
    assert d % nk == 0 and ff % nk == 0 and kc % LANES == 0 and fc % (2 * SUBLANES) == 0 and ff % LANES == 0
    n_items = item_e.shape[0]
    rows_max = MOE_ROWS * MOE_SUBS_PER_ITEM
    slab = d // LANES

    grid_spec = pltpu.PrefetchScalarGridSpec(
        num_scalar_prefetch=3,
        grid=(n_items, nk),
        in_specs=[pl.BlockSpec(memory_space=pl.ANY)] * 4,
        out_specs=pl.BlockSpec(memory_space=pl.ANY),
        scratch_shapes=[
            pltpu.VMEM((MOE_WEIGHT_SLOTS, kc, ff), F32),
            pltpu.VMEM((MOE_WEIGHT_SLOTS, kc, ff), F32),
            pltpu.VMEM((MOE_WEIGHT_SLOTS, fc, d), F32),
            pltpu.VMEM((rows_max * slab, LANES), F32),
            pltpu.VMEM((rows_max * slab, LANES), F32),
            pltpu.VMEM((nk, rows_max, kc), BF16),
            pltpu.VMEM((rows_max, 2 * ff), F32),
            pltpu.VMEM((kc, 2 * ff), BF16),
            pltpu.VMEM((ff, d), BF16),
            pltpu.SemaphoreType.DMA((MOE_WEIGHT_SLOTS,)),
            pltpu.SemaphoreType.DMA(()),
            pltpu.SemaphoreType.DMA(()),
        ],
    )
    return pl.pallas_call(
        _expert_kernel,
        grid_spec=grid_spec,
        out_shape=jax.ShapeDtypeStruct(xb.shape, F32),
        input_output_aliases={6: 0},
        compiler_params=pltpu.CompilerParams(dimension_semantics=("arbitrary", "arbitrary"),
                                             vmem_limit_bytes=VMEM_LIMIT_BYTES, has_side_effects=True),
        name="moe_experts",
    )(item_e, item_row, item_nsub, w_gate, w_up, w_down, xb)


COMBINE_TILE = 256


def _combine_kernel(alpha, dest_ref, yb_ref, h_ref, gate_ref, g_ref, b_ref, o_ref, buf_ref, sem):
    tm = COMBINE_TILE
    i = pl.program_id(0)
    slab = buf_ref.shape[2] // tm

    def issue(tile, half):
        def start(n, carry):
            for k in range(TOP_K):
                pltpu.make_async_copy(_row_slab(yb_ref, dest_ref[TOP_K * (tile * tm + n) + k], slab),
                                      _row_slab(buf_ref.at[half, k], n, slab), sem.at[half]).start(priority=k % 2)
            return carry
        lax.fori_loop(0, tm, start, 0, unroll=8)

    @pl.when(i == 0)
    def _():
        issue(0, 0)

    @pl.when(i + 1 < pl.num_programs(0))
    def _():
        issue(i + 1, (i + 1) % 2)

    half = i % 2
    for k in range(TOP_K):
        pltpu.make_async_copy(yb_ref.at[pl.ds(0, buf_ref.shape[2]), :], buf_ref.at[half, k], sem.at[half]).wait()
    gate = gate_ref[...]
    g0 = gate[:, 0:1]
    g1 = gate[:, 1:2]
    col = lambda k, j: buf_ref[half, k, pl.ds(j, tm, stride=slab), :]
    ff = jnp.concatenate([g0 * col(0, j) + g1 * col(1, j) for j in range(slab)], axis=1)
    o_ref[...] = _layer_norm(alpha * h_ref[...] + ff, g_ref[...], b_ref[...])


def _combine(dest_flat, yb, h, gates, ln_g, ln_b, alpha):
    m, d = h.shape
    tm = COMBINE_TILE
    assert m % tm == 0
    grid_spec = pltpu.PrefetchScalarGridSpec(
        num_scalar_prefetch=1,
        grid=(m // tm,),
        in_specs=[pl.BlockSpec(memory_space=pl.ANY),
                  pl.BlockSpec((tm, d), lambda i, dr: (i, 0)),
                  pl.BlockSpec((tm, LANES), lambda i, dr: (i, 0)),
                  pl.BlockSpec((1, d), lambda i, dr: (0, 0)),
                  pl.BlockSpec((1, d), lambda i, dr: (0, 0))],
        out_specs=pl.BlockSpec((tm, d), lambda i, dr: (i, 0)),
        scratch_shapes=[pltpu.VMEM((2, TOP_K, tm * (d // LANES), LANES), F32), pltpu.SemaphoreType.DMA((2,))],
    )
    return pl.pallas_call(
        functools.partial(_combine_kernel, alpha),
        grid_spec=grid_spec,
        out_shape=jax.ShapeDtypeStruct((m, d), F32),
        compiler_params=_cparams(("arbitrary",)),
        name="moe_combine_ln",
    )(dest_flat, yb, h, gates, ln_g.reshape(1, d), ln_b.reshape(1, d))


def _moe_tables(counts):
    nsub_e = (counts + MOE_ROWS - 1) // MOE_ROWS
    pstart = (jnp.cumsum(nsub_e) - nsub_e) * MOE_ROWS
    nitem_e = (nsub_e + MOE_SUBS_PER_ITEM - 1) // MOE_SUBS_PER_ITEM
    item_end = jnp.cumsum(nitem_e)
    return nsub_e, pstart, nitem_e, item_end


def _moe(h, h_rows, eid, gates, w_gate, w_up, w_down, ln_g, ln_b, alpha):
    m, d = h.shape
    n_assign = m * TOP_K
    n_blocks = (n_assign + N_EXPERTS * (MOE_ROWS - 1) + MOE_ROWS - 1) // MOE_ROWS
    n_rows = n_blocks * MOE_ROWS
    n_items = N_EXPERTS + n_assign // (MOE_ROWS * MOE_SUBS_PER_ITEM)

    eid_t = eid[:, :TOP_K].T
    counts = _expert_counts(eid_t)[:, 0].astype(I32)
    nsub_e, pstart, nitem_e, item_end = _moe_tables(counts)
    dest_t = _expert_slots(eid_t, pstart.astype(F32).reshape(N_EXPERTS, 1))
    dest_flat = dest_t.T.reshape(-1)

    it = jnp.arange(n_items, dtype=I32)
    total_items = item_end[-1]
    item_e = jnp.minimum(jnp.sum(item_end[None, :] <= it[:, None], axis=1).astype(I32), N_EXPERTS - 1)
    j = it - (item_end - nitem_e)[item_e]
    used = it < total_items
    last_e = item_e[jnp.maximum(total_items - 1, 0)]
    item_nsub = jnp.where(used, jnp.clip(nsub_e[item_e] - j * MOE_SUBS_PER_ITEM, 0, MOE_SUBS_PER_ITEM), 0).astype(I32)
    item_row = jnp.where(used, pstart[item_e] + j * (MOE_ROWS * MOE_SUBS_PER_ITEM), 0).astype(I32)
    item_e = jnp.where(used, item_e, last_e).astype(I32)

    pad_row = (pstart + counts).astype(I32)
    pad_len = (nsub_e * MOE_ROWS - counts).astype(I32)
    used_rows = jnp.sum(nsub_e) * MOE_ROWS
    tail = jnp.stack([used_rows, (n_rows - used_rows) // MOE_ROWS]).astype(I32)
    xb = _dispatch(dest_flat, pad_row, pad_len, tail, h_rows, m, n_rows)
    yb = _experts(item_e, item_row, item_nsub, w_gate, w_up, w_down, xb)
    return _combine(dest_flat, yb, h, gates, ln_g, ln_b, alpha)


def _pad_cols(w, n):
    return jnp.pad(w, ((0, 0), (0, n - w.shape[1])))


def _pad_rows(w, n):
    return jnp.pad(w, ((0, n - w.shape[0]), (0, 0)))


def kernel(x, w_in, attn_sinks, rw_mu_rkv, rw_mu_wag, rw_w0, rw_w1, rw_w2, rw_a0, rw_a1, rw_a2, rw_g1, rw_g2, rw_k_k, rw_k_a, rw_r_k, rw_lnx_w, rw_lnx_b, p_attn, p_rwkv, w_o, ln1_g, ln1_b, w_group, b_group, w_expert, b_expert, w_gate, w_up, w_down, ln2_g, ln2_b):
    b, t, d = x.shape
    depth = w_in.shape[0]
    m = b * t
    alpha = (2.0 * depth) ** 0.25
    cosb, sinb = _rope_tables(t)
    qkv_w = ATTN_Q_W + 2 * ATTN_KV_W
    rkv_w = 3 * RWKV_W
    h = x
    for l in range(depth):
        hf = h.reshape(m, d)
        hb = hf.astype(BF16)
        qkv = _matmul_cols(hb, w_in[l], 0, qkv_w, F32)
        rkv = _matmul_cols(hb, w_in[l], qkv_w, rkv_w, BF16)
        gates = _matmul_cols(hb, w_in[l], qkv_w + rkv_w, 2 * d, BF16)

        y_a = _attention(qkv.reshape(b, t, qkv_w), attn_sinks[l], cosb, sinb)

        lora = lambda w, n: _pad_cols(w, n).astype(BF16)
        lorb = lambda w, n: _pad_rows(w, n).astype(BF16)
        n_w = -(-rw_w1.shape[2] // LANES) * LANES
        n_a = -(-rw_a1.shape[2] // LANES) * LANES
        n_g = -(-rw_g1.shape[2] // LANES) * LANES
        r_, k_, v_, lw_, cum_, ag_, g_ = _rwkv_prep(
            h, rkv.reshape(b, t, rkv_w), rw_mu_rkv[l], rw_mu_wag[l], rw_w0[l],
            lora(rw_w1[l], n_w), lorb(rw_w2[l], n_w), rw_a0[l], lora(rw_a1[l], n_a), lorb(rw_a2[l], n_a),
            lora(rw_g1[l], n_g), lorb(rw_g2[l], n_g))
        y_r = _wkv_scan(r_, k_, v_, lw_, cum_, ag_, g_, rw_k_k[l], rw_k_a[l], rw_r_k[l], rw_lnx_w[l], rw_lnx_b[l])

        merged = _merge(y_a.reshape(m, ATTN_Q_W), y_r.reshape(m, RWKV_W), gates,
                        p_attn[l].astype(BF16), p_rwkv[l].astype(BF16))
        w_router = _pad_cols(jnp.concatenate([w_group[l], w_expert[l]], axis=1), LANES)
        w_router_hi = w_router.astype(BF16)
        w_router = jnp.stack([w_router_hi, (w_router - w_router_hi.astype(F32)).astype(BF16)])
        b_router = _pad_cols(jnp.concatenate([b_group[l], b_expert[l]])[None, :], LANES)
        h1, h1_rows, eid, gate = _outproj_router(merged, hf, w_o[l].astype(BF16), ln1_g[l], ln1_b[l],
                                        w_router, b_router, alpha)
        h2 = _moe(h1, h1_rows, eid, gate, w_gate[l], w_up[l], w_down[l], ln2_g[l], ln2_b[l], alpha)
        h = h2.reshape(b, t, d)
    return h
```

```python
import functools

import numpy as np
import jax
import jax.numpy as jnp
from jax import lax
from jax.experimental import pallas as pl
from jax.experimental.pallas import tpu as pltpu

F32 = jnp.float32
BF16 = jnp.bfloat16
I32 = jnp.int32

HEAD_DIM = 64
ATTN_Q_HEADS = 16
ATTN_KV_HEADS = 4
ATTN_GROUP = ATTN_Q_HEADS // ATTN_KV_HEADS
ATTN_Q_W = ATTN_Q_HEADS * HEAD_DIM
ATTN_KV_W = ATTN_KV_HEADS * HEAD_DIM
WINDOW = 128
ROPE_THETA = 10000.0
RWKV_HEADS = 16
RWKV_N = 64
RWKV_W = RWKV_HEADS * RWKV_N
RWKV_GN_EPS = 64e-5
N_GROUPS = 8
EXPERTS_PER_GROUP = 8
N_EXPERTS = N_GROUPS * EXPERTS_PER_GROUP
TOP_K = 2
LN_EPS = 1e-5

LANES = 128
SUBLANES = 8
VMEM_LIMIT_BYTES = 56 * 1024 * 1024

WKV_CHUNK = 64
WKV_CHUNKS_PER_STEP = 4
WKV_HEADS_PER_STEP = 16
MOE_ROWS = 128
MOE_SUBS_PER_ITEM = 4
MOE_K_STEPS = 4
MOE_WEIGHT_SLOTS = 3
ROUTE_TILE = 512


def _cparams(sem, vmem=VMEM_LIMIT_BYTES):
    return pltpu.CompilerParams(dimension_semantics=sem, vmem_limit_bytes=vmem)


def _sigmoid(x):
    return 1.0 / (1.0 + jnp.exp(-x))


def _dot(a, b):
    return jnp.dot(a.astype(BF16), b.astype(BF16), preferred_element_type=F32)


def _dot_nt(a, b):
    return lax.dot_general(a.astype(BF16), b.astype(BF16), (((1,), (1,)), ((), ())),
                           preferred_element_type=F32)


def _layer_norm(t, g, b):
    mu = jnp.mean(t, axis=-1, keepdims=True)
    d = t - mu
    var = jnp.mean(d * d, axis=-1, keepdims=True)
    return d * lax.rsqrt(var + LN_EPS) * g + b


def _matmul_kernel(a_ref, b_ref, o_ref):
    o_ref[...] = jnp.dot(a_ref[...], b_ref[...].astype(BF16), preferred_element_type=F32).astype(o_ref.dtype)


def _matmul_cols(a, b, col0, ncols, out_dtype, tm=2048, tn=512):
    m, k = a.shape
    tm = min(tm, m)
    cb = col0 // tn
    assert col0 % tn == 0 and ncols % tn == 0 and m % tm == 0
    return pl.pallas_call(
        _matmul_kernel,
        grid=(m // tm, ncols // tn),
        in_specs=[pl.BlockSpec((tm, k), lambda i, j: (i, 0)),
                  pl.BlockSpec((k, tn), lambda i, j: (0, j + cb))],
        out_specs=pl.BlockSpec((tm, tn), lambda i, j: (i, j)),
        out_shape=jax.ShapeDtypeStruct((m, ncols), out_dtype),
        compiler_params=_cparams(("parallel", "arbitrary")),
        name="inproj_matmul",
    )(a, b)


def _rope(x, cosb, sinb):
    half = HEAD_DIM // 2
    lane = lax.broadcasted_iota(I32, cosb.shape, 1)
    first_half = (lane % HEAD_DIM) < half
    outs = []
    for g in range(x.shape[1] // LANES):
        xg = x[:, g * LANES:(g + 1) * LANES]
        partner = jnp.where(first_half, pltpu.roll(xg, LANES - half, axis=1), pltpu.roll(xg, half, axis=1))
        outs.append(xg * cosb + partner * sinb)
    return outs


def _attn_kernel(sinks_ref, q_ref, kc_ref, kp_ref, vc_ref, vp_ref, cosc_ref, sinc_ref, cosp_ref, sinp_ref, bias_ref,
                 o_ref):
    tq = q_ref.shape[1]
    qg = _rope(q_ref[0], cosc_ref[...], sinc_ref[...])
    kcg = _rope(kc_ref[0], cosc_ref[...], sinc_ref[...])
    kpg = _rope(kp_ref[0], cosp_ref[...], sinp_ref[...])
    vc = vc_ref[0]
    vp = vp_ref[0]

    def head(groups, h):
        g = groups[h // 2]
        return g[:, (h % 2) * HEAD_DIM:(h % 2 + 1) * HEAD_DIM]

    rows = ATTN_GROUP * tq
    bias = bias_ref[0]
    rid = lax.broadcasted_iota(I32, (rows, 1), 0) // tq
    scale = HEAD_DIM ** -0.5
    for kvh in range(ATTN_KV_HEADS):
        qh = jnp.concatenate([head(qg, kvh * ATTN_GROUP + g) for g in range(ATTN_GROUP)], axis=0)
        kw = jnp.concatenate([head(kpg, kvh), head(kcg, kvh)], axis=0)
        vw = jnp.concatenate([vp[:, kvh * HEAD_DIM:(kvh + 1) * HEAD_DIM],
                              vc[:, kvh * HEAD_DIM:(kvh + 1) * HEAD_DIM]], axis=0)
        s = _dot_nt(qh, kw) * scale + bias
        sink = jnp.zeros((rows, 1), F32)
        for g in range(ATTN_GROUP):
            sink = jnp.where(rid == g, sinks_ref[kvh * ATTN_GROUP + g], sink)
        m = jnp.maximum(jnp.max(s, axis=-1, keepdims=True), sink)
        e = jnp.exp(s - m)
        denom = jnp.sum(e, axis=-1, keepdims=True) + jnp.exp(sink - m)
        o = _dot(e, vw) / denom
        for g in range(ATTN_GROUP):
            hq = kvh * ATTN_GROUP + g
            o_ref[0, :, hq * HEAD_DIM:(hq + 1) * HEAD_DIM] = o[g * tq:(g + 1) * tq].astype(o_ref.dtype)


def _attention(qkv, sinks, cosb, sinb):
    b, t, _ = qkv.shape
    tq = WINDOW
    nb = t // tq
    kcol = ATTN_Q_W // ATTN_KV_W
    prev = lambda i: jnp.maximum(i - 1, 0)
    grid_spec = pltpu.PrefetchScalarGridSpec(
        num_scalar_prefetch=0,
        grid=(b, nb),
        in_specs=[
            pl.BlockSpec(memory_space=pltpu.SMEM),
            pl.BlockSpec((1, tq, ATTN_Q_W), lambda bi, i: (bi, i, 0)),
            pl.BlockSpec((1, tq, ATTN_KV_W), lambda bi, i: (bi, i, kcol)),
            pl.BlockSpec((1, tq, ATTN_KV_W), lambda bi, i: (bi, prev(i), kcol)),
            pl.BlockSpec((1, tq, ATTN_KV_W), lambda bi, i: (bi, i, kcol + 1)),
            pl.BlockSpec((1, tq, ATTN_KV_W), lambda bi, i: (bi, prev(i), kcol + 1)),
            pl.BlockSpec((tq, LANES), lambda bi, i: (i, 0)),
            pl.BlockSpec((tq, LANES), lambda bi, i: (i, 0)),
            pl.BlockSpec((tq, LANES), lambda bi, i: (prev(i), 0)),
            pl.BlockSpec((tq, LANES), lambda bi, i: (prev(i), 0)),
            pl.BlockSpec((1, ATTN_GROUP * tq, 2 * tq), lambda bi, i: (jnp.minimum(i, 1), 0, 0)),
        ],
        out_specs=pl.BlockSpec((1, tq, ATTN_Q_W), lambda bi, i: (bi, i, 0)),
    )
    return pl.pallas_call(
        _attn_kernel,
        grid_spec=grid_spec,
        out_shape=jax.ShapeDtypeStruct((b, t, ATTN_Q_W), BF16),
        compiler_params=_cparams(("parallel", "arbitrary")),
        name="swa_attention",
    )(sinks, qkv, qkv, qkv, qkv, qkv, cosb, sinb, cosb, sinb, _band_bias(tq))


def _band_bias(tq):
    qi = np.arange(ATTN_GROUP * tq)[:, None] % tq
    kj = np.arange(2 * tq)[None, :]
    dist = qi + tq - kj
    band = (dist >= 0) & (dist < WINDOW)
    first = band & (kj >= tq)
    return jnp.asarray(np.where(np.stack([first, band]), 0.0, -np.inf), F32)


def _rope_tables(t):
    inv = 1.0 / (ROPE_THETA ** (np.arange(0, HEAD_DIM, 2, dtype=np.float64) / HEAD_DIM))
    ang = np.arange(t, dtype=np.float64)[:, None] * inv[None, :]
    cos, sin = np.cos(ang), np.sin(ang)
    reps = LANES // HEAD_DIM
    cosb = np.tile(np.concatenate([cos, cos], axis=-1), (1, reps))
    sinb = np.tile(np.concatenate([-sin, sin], axis=-1), (1, reps))
    return jnp.asarray(cosb, F32), jnp.asarray(sinb, F32)


def _rwkv_prep_kernel(h_ref, hp_ref, r_ref, k_ref, v_ref, rp_ref, kp_ref, vp_ref,
                      mu_rkv_ref, mu_wag_ref, w0_ref, w1_ref, w2_ref, a0_ref, a1_ref, a2_ref,
                      g1_ref, g2_ref,
                      ro_ref, ko_ref, vo_ref, lwo_ref, cumo_ref, ago_ref, go_ref):
    first = pl.program_id(1) == 0

    def shifted(cur, prev_ref):
        last_row = prev_ref.shape[1] - 1
        prev_row = jnp.where(first, 0.0, prev_ref[0, last_row:last_row + 1, :].astype(F32))
        rowid = lax.broadcasted_iota(I32, cur.shape, 0)
        return jnp.where(rowid == 0, prev_row, pltpu.roll(cur, 1, axis=0))

    h = h_ref[0]
    xx = shifted(h, hp_ref) - h
    xw = h + xx * mu_wag_ref[0:1, :]
    xa = h + xx * mu_wag_ref[1:2, :]
    xg = h + xx * mu_wag_ref[2:3, :]
    w_raw = w0_ref[...] + _dot(jnp.tanh(_dot(xw, w1_ref[...])), w2_ref[...])
    neg = -w_raw
    softplus = jnp.maximum(neg, 0.0) + jnp.log1p(jnp.exp(-jnp.abs(neg)))
    w = -softplus - 0.5
    lw = -jnp.exp(w)
    tm = lw.shape[0]
    row = lax.broadcasted_iota(I32, (tm, tm), 0)
    col = lax.broadcasted_iota(I32, (tm, tm), 1)
    tri = ((row >= col) & (row // WKV_CHUNK == col // WKV_CHUNK)).astype(BF16)
    cum = sum(jnp.dot(tri, piece, preferred_element_type=F32) for piece in _split3(lw))
    ag = _sigmoid(a0_ref[...] + _dot(_dot(xa, a1_ref[...]), a2_ref[...]))
    go_ref[0] = _dot(_sigmoid(_dot(xg, g1_ref[...])), g2_ref[...])

    r = r_ref[0].astype(F32)
    k = k_ref[0].astype(F32)
    v = v_ref[0].astype(F32)
    r = r + (shifted(r, rp_ref) - r) * mu_rkv_ref[0:1, :]
    k = k + (shifted(k, kp_ref) - k) * mu_rkv_ref[1:2, :]
    v = v + (shifted(v, vp_ref) - v) * mu_rkv_ref[2:3, :]
    nchunk = tm // WKV_CHUNK
    for hd in range(RWKV_HEADS):
        sl = slice(hd * RWKV_N, (hd + 1) * RWKV_N)
        for ref, val in ((ro_ref, r), (ko_ref, k), (vo_ref, v), (lwo_ref, lw), (cumo_ref, cum), (ago_ref, ag)):
            ref[0, :, hd] = val[:, sl].reshape(nchunk, WKV_CHUNK, RWKV_N).astype(ref.dtype)


def _rwkv_prep(h, rkv, mu_rkv, mu_wag, w0, w1, w2, a0, a1, a2, g1, g2, tm=256):
    b, t, d = h.shape
    tm = min(tm, t)
    c = RWKV_W
    prevblk = lambda i, rows: jnp.maximum(i * (tm // rows) - 1, 0)
    rkv_rows = SUBLANES * (4 // rkv.dtype.itemsize)
    full = lambda arr: pl.BlockSpec(arr.shape, lambda bi, i: (0,) * arr.ndim)
    row = lambda arr: arr.reshape(1, -1)
    w0, a0 = row(w0), row(a0)
    in_specs = [
        pl.BlockSpec((1, tm, d), lambda bi, i: (bi, i, 0)),
        pl.BlockSpec((1, SUBLANES, d), lambda bi, i: (bi, prevblk(i, SUBLANES), 0)),
        pl.BlockSpec((1, tm, c), lambda bi, i: (bi, i, 0)),
        pl.BlockSpec((1, tm, c), lambda bi, i: (bi, i, 1)),
        pl.BlockSpec((1, tm, c), lambda bi, i: (bi, i, 2)),
        pl.BlockSpec((1, rkv_rows, c), lambda bi, i: (bi, prevblk(i, rkv_rows), 0)),
        pl.BlockSpec((1, rkv_rows, c), lambda bi, i: (bi, prevblk(i, rkv_rows), 1)),
        pl.BlockSpec((1, rkv_rows, c), lambda bi, i: (bi, prevblk(i, rkv_rows), 2)),
    ] + [full(a) for a in (mu_rkv, mu_wag, w0, w1, w2, a0, a1, a2, g1, g2)]
    assert tm % WKV_CHUNK == 0
    hm = lambda dt: jax.ShapeDtypeStruct((b, t // WKV_CHUNK, RWKV_HEADS, WKV_CHUNK, RWKV_N), dt)
    hm_spec = pl.BlockSpec((1, tm // WKV_CHUNK, RWKV_HEADS, WKV_CHUNK, RWKV_N), lambda bi, i: (bi, i, 0, 0, 0))
    return pl.pallas_call(
        _rwkv_prep_kernel,
        grid=(b, t // tm),
        in_specs=in_specs,
        out_specs=[hm_spec] * 6 + [pl.BlockSpec((1, tm, c), lambda bi, i: (bi, i, 0))],
        out_shape=[hm(BF16), hm(BF16), hm(BF16), hm(F32), hm(F32), hm(BF16), jax.ShapeDtypeStruct((b, t, c), F32)],
        compiler_params=_cparams(("parallel", "arbitrary")),
        name="rwkv_prep",
    )(h, h, rkv, rkv, rkv, rkv, rkv, rkv, mu_rkv, mu_wag, w0, w1, w2, a0, a1, a2, g1, g2)


def _split3(x):
    hi = x.astype(BF16)
    r1 = x - hi.astype(F32)
    mid = r1.astype(BF16)
    lo = (r1 - mid.astype(F32)).astype(BF16)
    return hi, mid, lo


def _bmm(a, b):
    return jnp.einsum("gmk,gkn->gmn", a.astype(BF16), b.astype(BF16), preferred_element_type=F32)


def _bmm_nt(a, b):
    return jnp.einsum("gmk,gnk->gmn", a.astype(BF16), b.astype(BF16), preferred_element_type=F32)


def _bmm_tn(a, b):
    return jnp.einsum("gtm,gtn->gmn", a.astype(BF16), b.astype(BF16), preferred_element_type=F32)


def _wkv_kernel(r_ref, k_ref, v_ref, lw_ref, cum_ref, ag_ref, g_ref, kk_ref, ka_ref, rk_ref, lnw_ref, lnb_ref,
                y_ref, s_ref, st_ref):
    c = WKV_CHUNK
    n = RWKV_N
    nc, hb = r_ref.shape[1], r_ref.shape[2]
    tc = nc * c
    g = nc * hb

    @pl.when(pl.program_id(2) == 0)
    def _():
        s_ref[...] = jnp.zeros_like(s_ref)

    chunks = lambda ref: ref[0].reshape(g, c, n).astype(F32)
    r, k, v, lw, cum, ag = (chunks(ref) for ref in (r_ref, k_ref, v_ref, lw_ref, cum_ref, ag_ref))
    per_head = lambda t: t.reshape(nc, hb, c, n)
    kk = (per_head(k) * kk_ref[...]).reshape(g, c, n)
    k = (per_head(k) * (1.0 + (per_head(ag) - 1.0) * ka_ref[...])).reshape(g, c, n)
    kk = kk / jnp.maximum(jnp.sqrt(jnp.sum(kk * kk, axis=-1, keepdims=True)), 1e-12)
    dinc = jnp.exp(cum)
    dinv = jnp.exp(-cum)
    rt = r * dinc
    kt = k * dinv
    at = -kk * jnp.exp(cum - lw)
    bt = kk * ag * dinv
    d_chunk = dinc[:, c - 1:c, :]

    row = lax.broadcasted_iota(I32, (1, c, c), 1)
    col = lax.broadcasted_iota(I32, (1, c, c), 2)
    strict = row > col
    incl = row >= col
    eye = jnp.broadcast_to((row == col).astype(F32), (g, c, c))
    p = _bmm_nt(jnp.concatenate([at, rt], axis=1), jnp.concatenate([bt, kt], axis=1))
    a_ab = jnp.where(strict, p[:, :c, :c], 0.0)
    a_ak = jnp.where(strict, p[:, :c, c:], 0.0)
    a_rb = jnp.where(incl, p[:, c:, :c], 0.0)
    a_rk = jnp.where(incl, p[:, c:, c:], 0.0)
    x = _bmm(a_ab, a_ab)
    tinv = eye + a_ab
    levels = c.bit_length() - 2
    for j in range(1, levels):
        both = _bmm(jnp.concatenate([x, tinv], axis=1), x)
        x = both[:, :c]
        tinv = tinv + both[:, c:]
    tinv = tinv + _bmm(tinv, x)
    z = _bmm(a_ak, v)
    ta = _bmm(tinv, jnp.concatenate([at, z], axis=2))
    ry = _bmm(a_rb, ta)
    rp = rt + ry[:, :, :n]
    yv = ry[:, :, n:] + _bmm(a_rk, v)
    moff = _bmm_tn(ta[:, :, :n], bt).reshape(nc, hb, n, n)
    n2 = _bmm_tn(jnp.concatenate([ta[:, :, n:], v], axis=1),
                 jnp.concatenate([bt, kt], axis=1)).reshape(nc, hb, n, n)
    dch = d_chunk.reshape(nc, hb, 1, n)

    s = s_ref[...]
    for ci in range(nc):
        st_ref[ci] = s
        s = (s + _bmm(s, moff[ci]) + n2[ci]) * dch[ci]
    s_ref[...] = s

    y = _bmm_nt(rp, st_ref[...].reshape(g, n, n)) + yv
    mu = jnp.mean(y, axis=-1, keepdims=True)
    yc = y - mu
    var = jnp.mean(yc * yc, axis=-1, keepdims=True)
    y = per_head(yc * lax.rsqrt(var + RWKV_GN_EPS)) * lnw_ref[...] + lnb_ref[...]
    bonus = jnp.sum(per_head(r * k) * rk_ref[...], axis=-1, keepdims=True)
    y = y + bonus * per_head(v)
    y = jnp.concatenate([jnp.concatenate([y[ci, hd] for hd in range(hb)], axis=1) for ci in range(nc)], axis=0)
    y_ref[0] = (y * g_ref[0]).astype(y_ref.dtype)


def _wkv_scan(r, k, v, lw, cum, ag, g, k_k, k_a, r_k, lnx_w, lnx_b):
    b, nchunks, hh, c, n = r.shape
    t = nchunks * c
    nc = min(WKV_CHUNKS_PER_STEP, nchunks)
    tc = nc * c
    hb = WKV_HEADS_PER_STEP
    blk = pl.BlockSpec((1, nc, hb, c, n), lambda bi, hi, i: (bi, i, hi, 0, 0))
    tok = pl.BlockSpec((1, tc, hb * n), lambda bi, hi, i: (bi, i, hi))
    par = pl.BlockSpec((hb, 1, n), lambda bi, hi, i: (hi, 0, 0))
    per_head = lambda arr: arr.reshape(hh, 1, n)
    return pl.pallas_call(
        _wkv_kernel,
        grid=(b, hh // hb, t // tc),
        in_specs=[blk] * 6 + [tok] + [par] * 5,
        out_specs=tok,
        out_shape=jax.ShapeDtypeStruct((b, t, hh * n), BF16),
        scratch_shapes=[pltpu.VMEM((hb, n, n), F32), pltpu.VMEM((nc, hb, n, n), F32)],
        compiler_params=_cparams(("parallel", "parallel", "arbitrary")),
        name="wkv7_scan",
    )(r, k, v, lw, cum, ag, g, per_head(k_k), per_head(k_a), per_head(r_k), per_head(lnx_w), per_head(lnx_b))


def _merge_kernel(ya_ref, yr_ref, ga_ref, gr_ref, pa_ref, pr_ref, o_ref):
    ma = jnp.dot(ya_ref[...], pa_ref[...], preferred_element_type=F32)
    mr = jnp.dot(yr_ref[...], pr_ref[...], preferred_element_type=F32)
    o_ref[...] = (_sigmoid(ga_ref[...].astype(F32)) * ma + _sigmoid(gr_ref[...].astype(F32)) * mr).astype(o_ref.dtype)


def _merge(ya, yr, gates, p_attn, p_rwkv, tm=256):
    m, c = ya.shape
    d = p_attn.shape[1]
    tm = min(tm, m)
    tile = pl.BlockSpec((tm, c), lambda i: (i, 0))
    return pl.pallas_call(
        _merge_kernel,
        grid=(m // tm,),
        in_specs=[tile, tile,
                  pl.BlockSpec((tm, d), lambda i: (i, 0)),
                  pl.BlockSpec((tm, d), lambda i: (i, 1)),
                  pl.BlockSpec((c, d), lambda i: (0, 0)),
                  pl.BlockSpec((c, d), lambda i: (0, 0))],
        out_specs=pl.BlockSpec((tm, d), lambda i: (i, 0)),
        out_shape=jax.ShapeDtypeStruct((m, d), BF16),
        compiler_params=_cparams(("parallel",)),
        name="gated_merge",
    )(ya, yr, gates, gates, p_attn, p_rwkv)


def _outproj_router_kernel(alpha, mg_ref, x_ref, wo_ref, g_ref, b_ref, wr_ref, br_ref,
                           h_ref, hrow_ref, eid_ref, gate_ref):
    mix = jnp.dot(mg_ref[...], wo_ref[...], preferred_element_type=F32)
    h = _layer_norm(alpha * x_ref[...] + mix, g_ref[...], b_ref[...])
    h_ref[...] = h
    slab = h.shape[1] // LANES
    for j in range(slab):
        hrow_ref[pl.ds(j, h.shape[0], stride=slab), :] = h[:, j * LANES:(j + 1) * LANES]
    h_hi = h.astype(BF16)
    h_lo = (h - h_hi.astype(F32)).astype(BF16)
    logits = (jnp.dot(h_hi, wr_ref[0], preferred_element_type=F32)
              + jnp.dot(h_lo, wr_ref[0], preferred_element_type=F32)
              + jnp.dot(h_hi, wr_ref[1], preferred_element_type=F32)) + br_ref[...]
    lane = lax.broadcasted_iota(I32, logits.shape, 1)
    ninf = -jnp.inf
    big = jnp.int32(2 * LANES)
    glog = jnp.where(lane < N_GROUPS, logits, ninf)
    gmax = jnp.max(glog, axis=-1, keepdims=True)
    gidx = jnp.min(jnp.where(glog == gmax, lane, big), axis=-1, keepdims=True)
    gtop = 1.0 / jnp.sum(jnp.exp(glog - gmax), axis=-1, keepdims=True)
    eg = (lane - N_GROUPS) // EXPERTS_PER_GROUP
    in_group = (lane >= N_GROUPS) & (lane < N_GROUPS + N_EXPERTS) & (eg == gidx)
    el = jnp.where(in_group, logits, ninf)
    m1 = jnp.max(el, axis=-1, keepdims=True)
    i1 = jnp.min(jnp.where(el == m1, lane, big), axis=-1, keepdims=True)
    el2 = jnp.where(lane == i1, ninf, el)
    m2 = jnp.max(el2, axis=-1, keepdims=True)
    i2 = jnp.min(jnp.where(el2 == m2, lane, big), axis=-1, keepdims=True)
    t = jnp.exp(m2 - m1)
    p1 = 1.0 / (1.0 + t)
    p2 = t / (1.0 + t)
    eid_ref[...] = jnp.where(lane == 0, i1 - N_GROUPS, jnp.where(lane == 1, i2 - N_GROUPS, 0))
    gate_ref[...] = jnp.where(lane == 0, gtop * p1, jnp.where(lane == 1, gtop * p2, 0.0))


def _outproj_router(merged, x, w_o, ln_g, ln_b, w_router, b_router, alpha, tm=512):
    m, d = x.shape
    tm = min(tm, m)
    tile = pl.BlockSpec((tm, d), lambda i: (i, 0))
    vec = pl.BlockSpec((1, d), lambda i: (0, 0))
    small = pl.BlockSpec((tm, LANES), lambda i: (i, 0))
    return pl.pallas_call(
        functools.partial(_outproj_router_kernel, alpha),
        grid=(m // tm,),
        in_specs=[tile, tile, pl.BlockSpec((d, d), lambda i: (0, 0)), vec, vec,
                  pl.BlockSpec((2, d, LANES), lambda i: (0, 0, 0)), pl.BlockSpec((1, LANES), lambda i: (0, 0))],
        out_specs=[tile, pl.BlockSpec((tm * (d // LANES), LANES), lambda i: (i, 0)), small, small],
        out_shape=[jax.ShapeDtypeStruct((m, d), F32), jax.ShapeDtypeStruct((m * (d // LANES), LANES), F32),
                   jax.ShapeDtypeStruct((m, LANES), I32), jax.ShapeDtypeStruct((m, LANES), F32)],
        compiler_params=_cparams(("parallel",)),
        name="outproj_ln_router",
    )(merged, x, w_o, ln_g.reshape(1, d), ln_b.reshape(1, d), w_router, b_router)


def _onehots(eid_ref):
    tm = eid_ref.shape[1]
    e_iota = lax.broadcasted_iota(I32, (N_EXPERTS, tm), 0)
    oh0 = (eid_ref[0:1, :] == e_iota).astype(F32)
    oh1 = (eid_ref[1:2, :] == e_iota).astype(F32)
    return oh0, oh1


def _count_kernel(eid_ref, cnt_ref):
    @pl.when(pl.program_id(0) == 0)
    def _():
        cnt_ref[...] = jnp.zeros_like(cnt_ref)

    oh0, oh1 = _onehots(eid_ref)
    cnt_ref[...] += jnp.sum(oh0 + oh1, axis=1, keepdims=True)


def _slot_kernel(eid_ref, pstart_ref, dest_ref, run_ref):
    @pl.when(pl.program_id(0) == 0)
    def _():
        run_ref[...] = jnp.zeros_like(run_ref)

    tm = eid_ref.shape[1]
    oh0, oh1 = _onehots(eid_ref)
    both = oh0 + oh1
    earlier = (lax.broadcasted_iota(I32, (tm, tm), 0) < lax.broadcasted_iota(I32, (tm, tm), 1)).astype(BF16)
    pre = jnp.dot(both.astype(BF16), earlier, preferred_element_type=F32)
    base = pre + run_ref[...] + pstart_ref[...]
    dest_ref[0:1, :] = jnp.sum(oh0 * base, axis=0, keepdims=True).astype(I32)
    dest_ref[1:2, :] = jnp.sum(oh1 * base, axis=0, keepdims=True).astype(I32)
    run_ref[...] += jnp.sum(both, axis=1, keepdims=True)


def _expert_counts(eid_t):
    m = eid_t.shape[1]
    tm = min(ROUTE_TILE, m)
    return pl.pallas_call(
        _count_kernel,
        grid=(m // tm,),
        in_specs=[pl.BlockSpec((TOP_K, tm), lambda i: (0, i))],
        out_specs=pl.BlockSpec((N_EXPERTS, 1), lambda i: (0, 0)),
        out_shape=jax.ShapeDtypeStruct((N_EXPERTS, 1), F32),
        compiler_params=_cparams(("arbitrary",)),
        name="expert_counts",
    )(eid_t)


def _expert_slots(eid_t, pstart):
    m = eid_t.shape[1]
    tm = min(ROUTE_TILE, m)
    return pl.pallas_call(
        _slot_kernel,
        grid=(m // tm,),
        in_specs=[pl.BlockSpec((TOP_K, tm), lambda i: (0, i)),
                  pl.BlockSpec((N_EXPERTS, 1), lambda i: (0, 0))],
        out_specs=pl.BlockSpec((TOP_K, tm), lambda i: (0, i)),
        out_shape=jax.ShapeDtypeStruct((TOP_K, m), I32),
        scratch_shapes=[pltpu.VMEM((N_EXPERTS, 1), F32)],
        compiler_params=_cparams(("arbitrary",)),
        name="expert_slots",
    )(eid_t, pstart)


def _row_slab(ref, row, slab):
    return ref.at[pl.ds(pl.multiple_of(row * slab, slab), slab), :]


def _dispatch_kernel(dest_ref, pad_row_ref, pad_len_ref, tail_ref, h_ref, xb_ref, zero_ref, sem, zero_sem):
    tm = DISPATCH_TILE
    slab = h_ref.shape[0] // tm
    rb = MOE_ROWS
    step = pl.program_id(0)
    t0 = step * tm

    def zero_rows(row, n_rows):
        return pltpu.make_async_copy(zero_ref.at[pl.ds(0, n_rows * slab), :],
                                     xb_ref.at[pl.ds(pl.multiple_of(row * slab, slab), n_rows * slab), :], zero_sem)

    def zero_fill(op):
        def pad_run(e, carry):
            row = pad_row_ref[e]
            length = pad_len_ref[e]
            piece = rb // 2
            while piece >= 1:
                take = (length & piece) != 0

                @pl.when(take)
                def _(row=row, piece=piece):
                    op(zero_rows(row, piece))
                row = row + jnp.where(take, piece, 0)
                piece //= 2
            return carry
        lax.fori_loop(0, pad_len_ref.shape[0], pad_run, 0)

        def tail_block(tb, carry):
            op(zero_rows(tail_ref[0] + tb * rb, rb))
            return carry
        lax.fori_loop(0, tail_ref[1], tail_block, 0)

    @pl.when(step == 0)
    def _():
        zero_ref[...] = jnp.zeros_like(zero_ref)
        zero_fill(lambda c: c.start())

    def copy(n, k):
        return pltpu.make_async_copy(_row_slab(h_ref, n, slab),
                                     _row_slab(xb_ref, dest_ref[TOP_K * (t0 + n) + k], slab), sem)

    def start(n, carry):
        for k in range(TOP_K):
            copy(n, k).start(priority=k % 2)
        return carry

    lax.fori_loop(0, tm, start, 0, unroll=8)
    for k in range(TOP_K):
        pltpu.make_async_copy(h_ref, xb_ref.at[pl.ds(0, h_ref.shape[0]), :], sem).wait()

    @pl.when(step == pl.num_programs(0) - 1)
    def _():
        zero_fill(lambda c: c.wait())


DISPATCH_TILE = 512


def _dispatch(dest_flat, pad_row, pad_len, tail, h_rows, m, n_rows):
    slab = h_rows.shape[0] // m
    assert m % DISPATCH_TILE == 0
    grid_spec = pltpu.PrefetchScalarGridSpec(
        num_scalar_prefetch=4,
        grid=(m // DISPATCH_TILE,),
        in_specs=[pl.BlockSpec((DISPATCH_TILE * slab, LANES), lambda i, *_: (i, 0))],
        out_specs=pl.BlockSpec(memory_space=pl.ANY),
        scratch_shapes=[pltpu.VMEM((MOE_ROWS * slab, LANES), F32), pltpu.SemaphoreType.DMA(()),
                        pltpu.SemaphoreType.DMA(())],
    )
    return pl.pallas_call(
        _dispatch_kernel,
        grid_spec=grid_spec,
        out_shape=jax.ShapeDtypeStruct((n_rows * slab, LANES), F32),
        compiler_params=pltpu.CompilerParams(dimension_semantics=("arbitrary",), has_side_effects=True),
        name="moe_dispatch",
    )(dest_flat, pad_row, pad_len, tail, h_rows)


def _expert_kernel(item_e_ref, item_row_ref, item_nsub_ref, wg_hbm, wu_hbm, wd_hbm, xb_ref, yb_ref,
                   wg_buf, wu_buf, wd_buf, xin_ref, yout_ref, x_ref, acc_ref, wgu_ref, wdb_ref, sem_w, sem_in, sem_out):
    it = pl.program_id(0)
    f = pl.program_id(1)
    n_items = pl.num_programs(0)
    nf = pl.num_programs(1)
    kc = wg_buf.shape[1]
    ff = wg_buf.shape[2]
    fc = wd_buf.shape[1]
    nsub = item_nsub_ref[it]
    row0 = item_row_ref[it]
    nxt = jnp.minimum(it + 1, n_items - 1)
    nsub_next = jnp.where(it + 1 < n_items, item_nsub_ref[nxt], 0)
    prv = jnp.maximum(it - 1, 0)
    nsub_prev = jnp.where(it > 0, item_nsub_ref[prv], 0)
    rb = MOE_ROWS
    slab = x_ref.shape[0] * kc // LANES
    blk = rb * slab

    def stage_rows(ref, s):
        return ref.at[pl.ds(pl.multiple_of(s * blk, blk), blk), :]

    def hbm_rows(ref, item_row, s):
        return ref.at[pl.ds(pl.multiple_of((item_row + s * rb) * slab, blk), blk), :]

    def in_copy(item_row, s):
        return pltpu.make_async_copy(hbm_rows(xb_ref, item_row, s), stage_rows(xin_ref, s), sem_in)

    def out_copy(item_row, s):
        return pltpu.make_async_copy(stage_rows(yout_ref, s), hbm_rows(yb_ref, item_row, s), sem_out)

    def loop(n, fn):
        def body(s, carry):
            fn(s)
            return carry
        lax.fori_loop(0, n, body, 0)

    def weight_copies(step):
        item = step // nf
        k = step % nf
        e = item_e_ref[jnp.minimum(item, n_items - 1)]
        slot = step % MOE_WEIGHT_SLOTS
        return (pltpu.make_async_copy(wg_hbm.at[e, pl.ds(pl.multiple_of(k * kc, kc), kc), :], wg_buf.at[slot],
                                      sem_w.at[slot]),
                pltpu.make_async_copy(wu_hbm.at[e, pl.ds(pl.multiple_of(k * kc, kc), kc), :], wu_buf.at[slot],
                                      sem_w.at[slot]),
                pltpu.make_async_copy(wd_hbm.at[e, pl.ds(pl.multiple_of(k * fc, SUBLANES), fc), :], wd_buf.at[slot],
                                      sem_w.at[slot]))

    def fetch_weights(step):
        item = step // nf
        exists = (item < n_items) & (item_nsub_ref[jnp.minimum(item, n_items - 1)] > 0)

        @pl.when(exists)
        def _():
            for c in weight_copies(step):
                c.start(priority=1)

    @pl.when(nsub > 0)
    def _():
        this_step = it * nf + f

        @pl.when(this_step == 0)
        def _():
            for ahead in range(MOE_WEIGHT_SLOTS - 1):
                fetch_weights(ahead)
        fetch_weights(this_step + MOE_WEIGHT_SLOTS - 1)

        @pl.when(f == 0)
        def _():
            @pl.when(it == 0)
            def _():
                x_ref[...] = jnp.zeros_like(x_ref)
                loop(nsub, lambda s: in_copy(row0, s).start())
            loop(nsub, lambda s: in_copy(row0, s).wait())

            def to_matrix(s):
                rows = pl.ds(pl.multiple_of(s * rb, rb), rb)
                base = pl.multiple_of(s * blk, blk)
                for j in range(slab):
                    c0 = (j * LANES) % kc
                    x_ref[(j * LANES) // kc, rows, c0:c0 + LANES] = (
                        xin_ref[pl.ds(base + j, rb, stride=slab), :].astype(BF16))
            loop(nsub, to_matrix)
            loop(nsub_next, lambda s: in_copy(item_row_ref[nxt], s).start())

        for c in weight_copies(this_step):
            c.wait()
        slot = this_step % MOE_WEIGHT_SLOTS
        wgu_ref[:, :ff] = wg_buf[slot].astype(BF16)
        wgu_ref[:, ff:] = wu_buf[slot].astype(BF16)
        wdb_ref[pl.ds(pl.multiple_of(f * fc, 2 * SUBLANES), fc), :] = wd_buf[slot].astype(BF16)

        def gate_up(start, size):
            rows = pl.ds(pl.multiple_of(start, rb), size)
            part = jnp.dot(x_ref[f, rows, :], wgu_ref[...], preferred_element_type=F32)
            acc_ref[rows, :] = jnp.where(f > 0, acc_ref[rows, :], 0.0) + part

        gate_up(0, 2 * rb)

        def pair(p, carry):
            gate_up(p * (2 * rb), 2 * rb)
            return carry
        lax.fori_loop(1, nsub // 2, pair, 0)

        @pl.when((nsub % 2 == 1) & (nsub > 1))
        def _():
            gate_up((nsub - 1) * rb, rb)

        @pl.when(f == nf - 1)
        def _():
            loop(nsub_prev, lambda s: out_copy(item_row_ref[prv], s).wait())

            def down(start, size):
                rows = pl.ds(pl.multiple_of(start, rb), size)
                gate = acc_ref[rows, :ff]
                act = (gate * _sigmoid(gate) * acc_ref[rows, ff:]).astype(BF16)
                y = jnp.dot(act, wdb_ref[...], preferred_element_type=F32)
                base = pl.multiple_of(start * slab, blk)
                for j in range(slab):
                    yout_ref[pl.ds(base + j, size, stride=slab), :] = y[:, j * LANES:(j + 1) * LANES]

            down(0, 2 * rb)

            def down_pair(p, carry):
                down(p * (2 * rb), 2 * rb)
                return carry
            lax.fori_loop(1, nsub // 2, down_pair, 0)

            @pl.when((nsub % 2 == 1) & (nsub > 1))
            def _():
                down((nsub - 1) * rb, rb)

            loop(nsub, lambda s: out_copy(row0, s).start())

            @pl.when(nsub_next == 0)
            def _():
                loop(nsub, lambda s: out_copy(row0, s).wait())


def _experts(item_e, item_row, item_nsub, w_gate, w_up, w_down, xb):
    d = w_gate.shape[1]
    ff = w_gate.shape[2]
    nk = MOE_K_STEPS
    kc = d // nk
    fc = ff // nk
    assert d % nk == 0 and ff % nk == 0 and kc % LANES == 0 and fc % (2 * SUBLANES) == 0 and ff % LANES == 0
    n_items = item_e.shape[0]
    rows_max = MOE_ROWS * MOE_SUBS_PER_ITEM
    slab = d // LANES

    grid_spec = pltpu.PrefetchScalarGridSpec(
        num_scalar_prefetch=3,
        grid=(n_items, nk),
        in_specs=[pl.BlockSpec(memory_space=pl.ANY)] * 4,
        out_specs=pl.BlockSpec(memory_space=pl.ANY),
        scratch_shapes=[
            pltpu.VMEM((MOE_WEIGHT_SLOTS, kc, ff), F32),
            pltpu.VMEM((MOE_WEIGHT_SLOTS, kc, ff), F32),
            pltpu.VMEM((MOE_WEIGHT_SLOTS, fc, d), F32),
            pltpu.VMEM((rows_max * slab, LANES), F32),
            pltpu.VMEM((rows_max * slab, LANES), F32),
            pltpu.VMEM((nk, rows_max, kc), BF16),
            pltpu.VMEM((rows_max, 2 * ff), F32),
            pltpu.VMEM((kc, 2 * ff), BF16),
            pltpu.VMEM((ff, d), BF16),
            pltpu.SemaphoreType.DMA((MOE_WEIGHT_SLOTS,)),
            pltpu.SemaphoreType.DMA(()),
            pltpu.SemaphoreType.DMA(()),
        ],
    )
    return pl.pallas_call(
        _expert_kernel,
        grid_spec=grid_spec,
        out_shape=jax.ShapeDtypeStruct(xb.shape, F32),
        input_output_aliases={6: 0},
        compiler_params=pltpu.CompilerParams(dimension_semantics=("arbitrary", "arbitrary"),
                                             vmem_limit_bytes=VMEM_LIMIT_BYTES, has_side_effects=True),
        name="moe_experts",
    )(item_e, item_row, item_nsub, w_gate, w_up, w_down, xb)


COMBINE_TILE = 256


def _combine_kernel(alpha, dest_ref, yb_ref, h_ref, gate_ref, g_ref, b_ref, o_ref, buf_ref, sem):
    tm = COMBINE_TILE
    i = pl.program_id(0)
    slab = buf_ref.shape[2] // tm

    def issue(tile, half):
        def start(n, carry):
            for k in range(TOP_K):
                pltpu.make_async_copy(_row_slab(yb_ref, dest_ref[TOP_K * (tile * tm + n) + k], slab),
                                      _row_slab(buf_ref.at[half, k], n, slab), sem.at[half]).start(priority=k % 2)
            return carry
        lax.fori_loop(0, tm, start, 0, unroll=8)

    @pl.when(i == 0)
    def _():
        issue(0, 0)

    @pl.when(i + 1 < pl.num_programs(0))
    def _():
        issue(i + 1, (i + 1) % 2)

    half = i % 2
    for k in range(TOP_K):
        pltpu.make_async_copy(yb_ref.at[pl.ds(0, buf_ref.shape[2]), :], buf_ref.at[half, k], sem.at[half]).wait()
    gate = gate_ref[...]
    g0 = gate[:, 0:1]
    g1 = gate[:, 1:2]
    col = lambda k, j: buf_ref[half, k, pl.ds(j, tm, stride=slab), :]
    ff = jnp.concatenate([g0 * col(0, j) + g1 * col(1, j) for j in range(slab)], axis=1)
    o_ref[...] = _layer_norm(alpha * h_ref[...] + ff, g_ref[...], b_ref[...])


def _combine(dest_flat, yb, h, gates, ln_g, ln_b, alpha):
    m, d = h.shape
    tm = COMBINE_TILE
    assert m % tm == 0
    grid_spec = pltpu.PrefetchScalarGridSpec(
        num_scalar_prefetch=1,
        grid=(m // tm,),
        in_specs=[pl.BlockSpec(memory_space=pl.ANY),
                  pl.BlockSpec((tm, d), lambda i, dr: (i, 0)),
                  pl.BlockSpec((tm, LANES), lambda i, dr: (i, 0)),
                  pl.BlockSpec((1, d), lambda i, dr: (0, 0)),
                  pl.BlockSpec((1, d), lambda i, dr: (0, 0))],
        out_specs=pl.BlockSpec((tm, d), lambda i, dr: (i, 0)),
        scratch_shapes=[pltpu.VMEM((2, TOP_K, tm * (d // LANES), LANES), F32), pltpu.SemaphoreType.DMA((2,))],
    )
    return pl.pallas_call(
        functools.partial(_combine_kernel, alpha),
        grid_spec=grid_spec,
        out_shape=jax.ShapeDtypeStruct((m, d), F32),
        compiler_params=_cparams(("arbitrary",)),
        name="moe_combine_ln",
    )(dest_flat, yb, h, gates, ln_g.reshape(1, d), ln_b.reshape(1, d))


def _moe_tables(counts):
    nsub_e = (counts + MOE_ROWS - 1) // MOE_ROWS
    pstart = (jnp.cumsum(nsub_e) - nsub_e) * MOE_ROWS
    nitem_e = (nsub_e + MOE_SUBS_PER_ITEM - 1) // MOE_SUBS_PER_ITEM
    item_end = jnp.cumsum(nitem_e)
    return nsub_e, pstart, nitem_e, item_end


def _moe(h, h_rows, eid, gates, w_gate, w_up, w_down, ln_g, ln_b, alpha):
    m, d = h.shape
    n_assign = m * TOP_K
    n_blocks = (n_assign + N_EXPERTS * (MOE_ROWS - 1) + MOE_ROWS - 1) // MOE_ROWS
    n_rows = n_blocks * MOE_ROWS
    n_items = N_EXPERTS + n_assign // (MOE_ROWS * MOE_SUBS_PER_ITEM)

    eid_t = eid[:, :TOP_K].T
    counts = _expert_counts(eid_t)[:, 0].astype(I32)
    nsub_e, pstart, nitem_e, item_end = _moe_tables(counts)
    dest_t = _expert_slots(eid_t, pstart.astype(F32).reshape(N_EXPERTS, 1))
    dest_flat = dest_t.T.reshape(-1)

    it = jnp.arange(n_items, dtype=I32)
    total_items = item_end[-1]
    item_e = jnp.minimum(jnp.sum(item_end[None, :] <= it[:, None], axis=1).astype(I32), N_EXPERTS - 1)
    j = it - (item_end - nitem_e)[item_e]
    used = it < total_items
    last_e = item_e[jnp.maximum(total_items - 1, 0)]
    item_nsub = jnp.where(used, jnp.clip(nsub_e[item_e] - j * MOE_SUBS_PER_ITEM, 0, MOE_SUBS_PER_ITEM), 0).astype(I32)
    item_row = jnp.where(used, pstart[item_e] + j * (MOE_ROWS * MOE_SUBS_PER_ITEM), 0).astype(I32)
    item_e = jnp.where(used, item_e, last_e).astype(I32)

    pad_row = (pstart + counts).astype(I32)
    pad_len = (nsub_e * MOE_ROWS - counts).astype(I32)
    used_rows = jnp.sum(nsub_e) * MOE_ROWS
    tail = jnp.stack([used_rows, (n_rows - used_rows) // MOE_ROWS]).astype(I32)
    xb = _dispatch(dest_flat, pad_row, pad_len, tail, h_rows, m, n_rows)
    yb = _experts(item_e, item_row, item_nsub, w_gate, w_up, w_down, xb)
    return _combine(dest_flat, yb, h, gates, ln_g, ln_b, alpha)


def _pad_cols(w, n):
    return jnp.pad(w, ((0, 0), (0, n - w.shape[1])))


def _pad_rows(w, n):
    return jnp.pad(w, ((0, n - w.shape[0]), (0, 0)))


def kernel(x, w_in, attn_sinks, rw_mu_rkv, rw_mu_wag, rw_w0, rw_w1, rw_w2, rw_a0, rw_a1, rw_a2, rw_g1, rw_g2, rw_k_k, rw_k_a, rw_r_k, rw_lnx_w, rw_lnx_b, p_attn, p_rwkv, w_o, ln1_g, ln1_b, w_group, b_group, w_expert, b_expert, w_gate, w_up, w_down, ln2_g, ln2_b):
    b, t, d = x.shape
    depth = w_in.shape[0]
    m = b * t
    alpha = (2.0 * depth) ** 0.25
    cosb, sinb = _rope_tables(t)
    qkv_w = ATTN_Q_W + 2 * ATTN_KV_W
    rkv_w = 3 * RWKV_W
    h = x
    for l in range(depth):
        hf = h.reshape(m, d)
        hb = hf.astype(BF16)
        qkv = _matmul_cols(hb, w_in[l], 0, qkv_w, F32)
        rkv = _matmul_cols(hb, w_in[l], qkv_w, rkv_w, BF16)
        gates = _matmul_cols(hb, w_in[l], qkv_w + rkv_w, 2 * d, BF16)

        y_a = _attention(qkv.reshape(b, t, qkv_w), attn_sinks[l], cosb, sinb)

        lora = lambda w, n: _pad_cols(w, n).astype(BF16)
        lorb = lambda w, n: _pad_rows(w, n).astype(BF16)
        n_w = -(-rw_w1.shape[2] // LANES) * LANES
        n_a = -(-rw_a1.shape[2] // LANES) * LANES
        n_g = -(-rw_g1.shape[2] // LANES) * LANES
        r_, k_, v_, lw_, cum_, ag_, g_ = _rwkv_prep(
            h, rkv.reshape(b, t, rkv_w), rw_mu_rkv[l], rw_mu_wag[l], rw_w0[l],
            lora(rw_w1[l], n_w), lorb(rw_w2[l], n_w), rw_a0[l], lora(rw_a1[l], n_a), lorb(rw_a2[l], n_a),
            lora(rw_g1[l], n_g), lorb(rw_g2[l], n_g))
        y_r = _wkv_scan(r_, k_, v_, lw_, cum_, ag_, g_, rw_k_k[l], rw_k_a[l], rw_r_k[l], rw_lnx_w[l], rw_lnx_b[l])

        merged = _merge(y_a.reshape(m, ATTN_Q_W), y_r.reshape(m, RWKV_W), gates,
                        p_attn[l].astype(BF16), p_rwkv[l].astype(BF16))
        w_router = _pad_cols(jnp.concatenate([w_group[l], w_expert[l]], axis=1), LANES)
        w_router_hi = w_router.astype(BF16)
        w_router = jnp.stack([w_router_hi, (w_router - w_router_hi.astype(F32)).astype(BF16)])
        b_router = _pad_cols(jnp.concatenate([b_group[l], b_expert[l]])[None, :], LANES)
        h1, h1_rows, eid, gate = _outproj_router(merged, hf, w_o[l].astype(BF16), ln1_g[l], ln1_b[l],
                                        w_router, b_router, alpha)
        h2 = _moe(h1, h1_rows, eid, gate, w_gate[l], w_up[l], w_down[l], ln2_g[l], ln2_b[l], alpha)
        h = h2.reshape(b, t, d)
    return h
```
